```python
import jax, jax.numpy as jnp
from jax import lax
import numpy as np

D_MODEL = 1024
BATCH = 4
SEQ = 4096
DEPTH = 1

HEAD_DIM = 64
MIX_WIDTH = D_MODEL
A_WIDTH = MIX_WIDTH // 2
B_WIDTH = MIX_WIDTH - A_WIDTH
A_HEADS = A_WIDTH // HEAD_DIM
B_Q_HEADS = B_WIDTH // HEAD_DIM
B_KV_HEADS = 2
B_KV_WIDTH = B_KV_HEADS * HEAD_DIM
IN_WIDTH = 3 * A_WIDTH + B_WIDTH + 2 * B_KV_WIDTH
A_CONFIGS = ((128, 1), (512, 4), (2048, 16))
A_BLOCK = 64
B_HALF_WINDOW = 128
B_BLOCK = 128
N_GROUPS = 4
EXPERTS_PER_GROUP = 8
N_EXPERTS = N_GROUPS * EXPERTS_PER_GROUP
TOP_K = 2
D_EXPERT = D_MODEL // 2
PLE_DIM = 256
EPS = 1e-6
NEG = -1e30

kernel_name = "hybrid_dilated_swa_hiermoe_encoder"


def rmsnorm(x, g):
    xf = x.astype(jnp.float32)
    r = lax.rsqrt(jnp.mean(xf * xf, axis=-1, keepdims=True) + EPS)
    return (xf * r).astype(x.dtype) * g


def alibi_slopes(n):
    return jnp.exp2(-8.0 * jnp.arange(1, n + 1, dtype=jnp.float32) / n)


def banded_attention(q, k, v, half, blk, slopes, dist_scale, sink=None):
    n, l, hk, g, dh = q.shape
    nb = -(-l // blk)
    lp = nb * blk
    q = jnp.pad(q, ((0, 0), (0, lp - l), (0, 0), (0, 0), (0, 0)))
    pad_kv = ((0, 0), (blk, lp - l + blk), (0, 0), (0, 0))
    kb = jnp.pad(k, pad_kv).reshape(n, nb + 2, blk, hk, dh)
    vb = jnp.pad(v, pad_kv).reshape(n, nb + 2, blk, hk, dh)
    kw = jnp.concatenate([kb[:, :-2], kb[:, 1:-1], kb[:, 2:]], axis=2)
    vw = jnp.concatenate([vb[:, :-2], vb[:, 1:-1], vb[:, 2:]], axis=2)
    qb = q.reshape(n, nb, blk, hk, g, dh)
    s = jnp.einsum('nbqhgd,nbkhd->nbhgqk', qb, kw,
                   preferred_element_type=jnp.float32) * (dh ** -0.5)
    qpos = jnp.arange(lp).reshape(nb, blk)
    kpos = (jnp.arange(nb)[:, None] - 1) * blk + jnp.arange(3 * blk)[None, :]
    rel = kpos[:, None, :] - qpos[:, :, None]
    valid = (jnp.abs(rel) <= half) & (kpos[:, None, :] >= 0) & (kpos[:, None, :] < l)
    dist = jnp.abs(rel).astype(jnp.float32) * dist_scale
    s = s - slopes.astype(jnp.float32)[:, :, None, None] * dist[None, :, None, None]
    s = jnp.where(valid[None, :, None, None], s, NEG)
    m = jnp.max(s, axis=-1)
    if sink is not None:
        sk = sink.astype(jnp.float32)[:, :, None]
        m = jnp.maximum(m, sk)
    pexp = jnp.exp(s - m[..., None])
    den = jnp.sum(pexp, axis=-1)
    if sink is not None:
        den = den + jnp.exp(sk - m)
    o = jnp.einsum('nbhgqk,nbkhd->nbqhgd', pexp.astype(v.dtype), vw,
                   preferred_element_type=jnp.float32)
    o = o / den.transpose(0, 1, 4, 2, 3)[..., None]
    o = o.reshape(n, lp, hk, g, dh)[:, :l]
    lse = (m + jnp.log(den)).transpose(0, 1, 4, 2, 3).reshape(n, lp, hk, g)[:, :l]
    return o.astype(q.dtype), lse


def dilated_attention(q, k, v, slopes):
    b, s, h, dh = q.shape
    outs, lses = [], []
    for window, dil in A_CONFIGS:
        l = s // dil

        def to_sub(t):
            return t.reshape(b, l, dil, h, dh).transpose(0, 2, 1, 3, 4).reshape(b * dil, l, h, dh)

        o, lse = banded_attention(to_sub(q)[:, :, :, None], to_sub(k), to_sub(v),
                                  window // (2 * dil), A_BLOCK, slopes[:, None], float(dil))
        o = o[:, :, :, 0].reshape(b, dil, l, h, dh).transpose(0, 2, 1, 3, 4).reshape(b, s, h, dh)
        lse = lse[..., 0].reshape(b, dil, l, h).transpose(0, 2, 1, 3).reshape(b, s, h)
        outs.append(o)
        lses.append(lse)
    w = jax.nn.softmax(jnp.stack(lses, axis=0), axis=0)
    o = jnp.einsum('cbsh,cbshd->bshd', w, jnp.stack(outs, axis=0).astype(jnp.float32))
    return o.astype(q.dtype)


def windowed_gqa_sink(q, k, v, sink):
    b, s, hq, dh = q.shape
    g = hq // B_KV_HEADS
    qg = q.reshape(b, s, B_KV_HEADS, g, dh)
    slopes = alibi_slopes(hq).reshape(B_KV_HEADS, g)
    o, _ = banded_attention(qg, k, v, B_HALF_WINDOW, B_BLOCK, slopes, 1.0,
                            sink.reshape(B_KV_HEADS, g))
    return o.reshape(b, s, hq * dh)


def hier_moe(h, w_rg, b_rg, w_re, b_re, w_gate, w_up, w_down):
    b, s, d = h.shape
    t = h.reshape(b * s, d)
    n_tok = t.shape[0]
    glog = (t @ w_rg).astype(jnp.float32) + b_rg.astype(jnp.float32)
    gsel = jnp.argmax(glog, axis=-1)
    gw = jnp.max(jax.nn.softmax(glog, axis=-1), axis=-1)
    elog = ((t @ w_re).astype(jnp.float32) + b_re.astype(jnp.float32)).reshape(
        n_tok, N_GROUPS, EXPERTS_PER_GROUP)
    elog_sel = elog[jnp.arange(n_tok), gsel]
    top_v, top_i = lax.top_k(elog_sel, TOP_K)
    top_w = jax.nn.softmax(top_v, axis=-1) * gw[:, None]
    eid = gsel[:, None] * EXPERTS_PER_GROUP + top_i
    dense_w = jnp.sum(jax.nn.one_hot(eid, N_EXPERTS, dtype=jnp.float32) * top_w[..., None], axis=1)
    out = jnp.zeros((n_tok, d), jnp.float32)
    for grp in range(N_GROUPS):
        sl = slice(grp * EXPERTS_PER_GROUP, (grp + 1) * EXPERTS_PER_GROUP)
        a = jnp.einsum('td,edf->tef', t, w_gate[sl])
        u = jnp.einsum('td,edf->tef', t, w_up[sl])
        hid = jax.nn.silu(a) * u * dense_w[:, sl, None].astype(t.dtype)
        out = out + jnp.einsum('tef,efd->td', hid, w_down[sl], preferred_element_type=jnp.float32)
    return out.astype(h.dtype).reshape(b, s, d)


def setup_inputs(seed: int = 0) -> dict:
    key = jax.random.key(seed)
    ks = jax.random.split(key, 20)
    f32 = jnp.float32

    def nrm(k, shape, scale):
        return jax.random.normal(k, shape, f32) * scale

    def gain(k, shape):
        return 1.0 + 0.05 * jax.random.normal(k, shape, f32)

    return {
        "x": nrm(ks[0], (BATCH, SEQ, D_MODEL), 1.0),
        "p": nrm(ks[1], (DEPTH, BATCH, SEQ, PLE_DIM), 1.0),
        "g_mix": gain(ks[2], (DEPTH, D_MODEL)),
        "w_in": nrm(ks[3], (DEPTH, D_MODEL, IN_WIDTH), D_MODEL ** -0.5),
        "sink": nrm(ks[4], (DEPTH, B_Q_HEADS), 0.5),
        "g_grp_a": gain(ks[5], (DEPTH, A_WIDTH)),
        "g_grp_b": gain(ks[6], (DEPTH, B_WIDTH)),
        "w_out": nrm(ks[7], (DEPTH, MIX_WIDTH, D_MODEL), MIX_WIDTH ** -0.5),
        "g_ffn": gain(ks[8], (DEPTH, D_MODEL)),
        "w_router_group": nrm(ks[9], (DEPTH, D_MODEL, N_GROUPS), D_MODEL ** -0.5),
        "b_router_group": nrm(ks[10], (DEPTH, N_GROUPS), 0.01),
        "w_router_expert": nrm(ks[11], (DEPTH, D_MODEL, N_EXPERTS), D_MODEL ** -0.5),
        "b_router_expert": nrm(ks[12], (DEPTH, N_EXPERTS), 0.01),
        "w_expert_gate": nrm(ks[13], (DEPTH, N_EXPERTS, D_MODEL, D_EXPERT), D_MODEL ** -0.5),
        "w_expert_up": nrm(ks[14], (DEPTH, N_EXPERTS, D_MODEL, D_EXPERT), D_MODEL ** -0.5),
        "w_expert_down": nrm(ks[15], (DEPTH, N_EXPERTS, D_EXPERT, D_MODEL), D_EXPERT ** -0.5),
        "g_ple": gain(ks[16], (DEPTH, D_MODEL)),
        "w_ple_gate": nrm(ks[17], (DEPTH, D_MODEL, D_MODEL), D_MODEL ** -0.5),
        "w_ple_proj": nrm(ks[18], (DEPTH, PLE_DIM, D_MODEL), PLE_DIM ** -0.5),
        "g_final": gain(ks[19], (D_MODEL,)),
    }


def reference(x, p, g_mix, w_in, sink, g_grp_a, g_grp_b, w_out, g_ffn,
              w_router_group, b_router_group, w_router_expert, b_router_expert,
              w_expert_gate, w_expert_up, w_expert_down, g_ple, w_ple_gate, w_ple_proj,
              g_final):
    b, s, _ = x.shape
    split_at = [A_WIDTH, 2 * A_WIDTH, 3 * A_WIDTH, 3 * A_WIDTH + B_WIDTH,
                3 * A_WIDTH + B_WIDTH + B_KV_WIDTH]
    slopes_a = alibi_slopes(A_HEADS)
    for i in range(DEPTH):
        h = rmsnorm(x, g_mix[i])
        proj = h @ w_in[i]
        qa, ka, va, qb, kb, vb = jnp.split(proj, split_at, axis=-1)
        oa = dilated_attention(qa.reshape(b, s, A_HEADS, HEAD_DIM),
                               ka.reshape(b, s, A_HEADS, HEAD_DIM),
                               va.reshape(b, s, A_HEADS, HEAD_DIM), slopes_a).reshape(b, s, A_WIDTH)
        ob = windowed_gqa_sink(qb.reshape(b, s, B_Q_HEADS, HEAD_DIM),
                               kb.reshape(b, s, B_KV_HEADS, HEAD_DIM),
                               vb.reshape(b, s, B_KV_HEADS, HEAD_DIM), sink[i])
        mix = jnp.concatenate([rmsnorm(oa, g_grp_a[i]), rmsnorm(ob, g_grp_b[i])], axis=-1)
        x = x + mix @ w_out[i]
        h = rmsnorm(x, g_ffn[i])
        x = x + hier_moe(h, w_router_group[i], b_router_group[i], w_router_expert[i],
                         b_router_expert[i], w_expert_gate[i], w_expert_up[i], w_expert_down[i])
        gate = jax.nn.sigmoid(rmsnorm(x, g_ple[i]) @ w_ple_gate[i])
        x = x + (p[i] @ w_ple_proj[i]) * gate
    return rmsnorm(x, g_final)
```

```python
import functools

import jax
import jax.numpy as jnp
from jax import lax
from jax.experimental import pallas as pl
from jax.experimental.pallas import tpu as pltpu

D_MODEL = 1024
HEAD_DIM = 64
GROUP_WIDTH = 512
N_HEADS = 8
B_KV_HEADS = 2
B_KV_WIDTH = B_KV_HEADS * HEAD_DIM
A_CONFIGS = ((128, 1), (512, 4), (2048, 16))
B_HALF_WINDOW = 128
N_GROUPS = 4
EXPERTS_PER_GROUP = 8
N_EXPERTS = N_GROUPS * EXPERTS_PER_GROUP
D_EXPERT = 512
PLE_DIM = 256
EPS = 1e-6
NEG = -1e30

Q_BLOCK = 128
HEADS_PER_PASS = 4
PASS_WIDTH = HEADS_PER_PASS * HEAD_DIM
ROW_TILE = 256
ROUTER_ROWS = 128
VMEM_LIMIT = 48 * 1024 * 1024


def _rms(x, g):
    r = lax.rsqrt(jnp.mean(x * x, axis=-1, keepdims=True) + EPS)
    return (x * r) * g


def _split3(x):
    hi = x.astype(jnp.bfloat16)
    r1 = x - hi.astype(jnp.float32)
    mid = r1.astype(jnp.bfloat16)
    lo = (r1 - mid.astype(jnp.float32)).astype(jnp.bfloat16)
    return hi, mid, lo


def _dot(a, b):
    return jnp.dot(a, b, preferred_element_type=jnp.float32)


def _proj_kernel(x_ref, g_ref, w_ref, qa_ref, ka_ref, va_ref, qb_ref, kb_ref, vb_ref):
    h = _rms(x_ref[...], g_ref[...]).astype(jnp.bfloat16)
    scale = HEAD_DIM ** -0.5
    W = GROUP_WIDTH
    qa_ref[...] = (_dot(h, w_ref[:, 0:W]) * scale).astype(jnp.bfloat16)
    ka_ref[...] = _dot(h, w_ref[:, W:2 * W]).astype(jnp.bfloat16)
    va_ref[...] = _dot(h, w_ref[:, 2 * W:3 * W]).astype(jnp.bfloat16)
    qb_ref[...] = (_dot(h, w_ref[:, 3 * W:4 * W]) * scale).astype(jnp.bfloat16)
    kb_ref[...] = _dot(h, w_ref[:, 4 * W:4 * W + B_KV_WIDTH]).astype(jnp.bfloat16)
    vb_ref[...] = _dot(h, w_ref[:, 4 * W + B_KV_WIDTH:4 * W + 2 * B_KV_WIDTH]).astype(jnp.bfloat16)


def _proj(x2d, g_mix, w_in_bf):
    t = x2d.shape[0]
    tm = 512
    in_width = w_in_bf.shape[1]
    wide = jax.ShapeDtypeStruct((t, GROUP_WIDTH), jnp.bfloat16)
    narrow = jax.ShapeDtypeStruct((t, B_KV_WIDTH), jnp.bfloat16)
    row = lambda w: pl.BlockSpec((tm, w), lambda i: (i, 0))
    return pl.pallas_call(
        _proj_kernel,
        out_shape=(wide, wide, wide, wide, narrow, narrow),
        grid=(t // tm,),
        in_specs=[row(D_MODEL),
                  pl.BlockSpec((1, D_MODEL), lambda i: (0, 0)),
                  pl.BlockSpec((D_MODEL, in_width), lambda i: (0, 0))],
        out_specs=(row(GROUP_WIDTH), row(GROUP_WIDTH), row(GROUP_WIDTH), row(GROUP_WIDTH),
                   row(B_KV_WIDTH), row(B_KV_WIDTH)),
        compiler_params=pltpu.CompilerParams(dimension_semantics=("arbitrary",),
                                             vmem_limit_bytes=VMEM_LIMIT),
        name="proj",
    )(x2d, g_mix, w_in_bf)


def _fill_bias(bias_ref, half, nkeys, dist_scale):
    i = lax.broadcasted_iota(jnp.int32, (Q_BLOCK, nkeys), 0)
    j = lax.broadcasted_iota(jnp.int32, (Q_BLOCK, nkeys), 1)
    for var, rel0 in enumerate((0, -half, Q_BLOCK - nkeys)):
        d = jnp.abs(j + rel0 - i)
        dist = d.astype(jnp.float32) * dist_scale
        for h in range(N_HEADS):
            slope = 2.0 ** (-(h + 1))
            bias_ref[var, h] = jnp.where(d <= half, -(slope * dist), NEG)


def _attend_block(q_ref, k_ref, v_ref, bias_ref, o_ref, lse_ref, sink_ref, row0, q0, seq_len,
                  half, nkeys):
    lo = q0 - half
    ks = pl.multiple_of(jnp.clip(lo, 0, seq_len - nkeys), HEAD_DIM)
    var = jnp.where(lo < 0, 0, jnp.where(lo > seq_len - nkeys, 2, 1))
    lane_head = lax.broadcasted_iota(jnp.int32, (Q_BLOCK, PASS_WIDTH), 1) // HEAD_DIM
    for g in range(N_HEADS // HEADS_PER_PASS):
        cols = slice(g * PASS_WIDTH, (g + 1) * PASS_WIDTH)
        q = q_ref[pl.ds(row0, Q_BLOCK), cols]
        zero = jnp.zeros_like(q)
        q_stack = jnp.concatenate(
            [jnp.where(lane_head == h, q, zero) for h in range(HEADS_PER_PASS)], axis=0)
        k = k_ref[pl.ds(ks, nkeys), cols]
        v = v_ref[pl.ds(ks, nkeys), cols]
        s = lax.dot_general(q_stack, k, (((1,), (1,)), ((), ())),
                            preferred_element_type=jnp.float32)
        s = s.reshape(HEADS_PER_PASS, Q_BLOCK, nkeys)
        s = s + bias_ref[var, pl.ds(g * HEADS_PER_PASS, HEADS_PER_PASS)]
        m = jnp.max(s, axis=-1, keepdims=True)
        if sink_ref is not None:
            sk = jnp.stack([jnp.full((Q_BLOCK, 1), sink_ref[g * HEADS_PER_PASS + h], jnp.float32)
                            for h in range(HEADS_PER_PASS)], axis=0)
            m = jnp.maximum(m, sk)
        p = jnp.exp(s - m)
        den = jnp.sum(p, axis=-1, keepdims=True)
        if sink_ref is not None:
            den = den + jnp.exp(sk - m)
        pv = _dot(p.reshape(HEADS_PER_PASS * Q_BLOCK, nkeys).astype(jnp.bfloat16), v)
        pv = pv.reshape(HEADS_PER_PASS, Q_BLOCK, PASS_WIDTH) * (1.0 / den)
        o = jnp.zeros((Q_BLOCK, PASS_WIDTH), jnp.float32)
        for h in range(HEADS_PER_PASS):
            o = jnp.where(lane_head == h, pv[h], o)
        o_ref[pl.ds(row0, Q_BLOCK), cols] = o
        if lse_ref is not None:
            lse = m + jnp.log(den)
            for h in range(HEADS_PER_PASS):
                c = g * HEADS_PER_PASS + h
                lse_ref[pl.ds(row0, Q_BLOCK), c:c + 1] = lse[h]


def _attn_a_kernel(q_ref, k_ref, v_ref, o_ref, lse_ref, bias_ref, *, half, nkeys, dist_scale,
                   seq_len, q_tile):
    first = (pl.program_id(0) == 0) & (pl.program_id(1) == 0) & (pl.program_id(2) == 0)

    @pl.when(first)
    def _():
        _fill_bias(bias_ref, half, nkeys, dist_scale)

    base = pl.program_id(2) * q_tile

    def body(jb, carry):
        row0 = pl.multiple_of(jb * Q_BLOCK, Q_BLOCK)
        _attend_block(q_ref, k_ref, v_ref, bias_ref, o_ref, lse_ref, None, row0, base + row0,
                      seq_len, half, nkeys)
        return carry

    lax.fori_loop(0, q_tile // Q_BLOCK, body, 0)


def _attn_a(q, k, v, batch, seq, dil, window):
    l = seq // dil
    half = window // (2 * dil)
    nkeys = Q_BLOCK + 2 * half
    q_tile = min(l, 1024)
    view = lambda a: a.reshape(batch, l, dil * GROUP_WIDTH)
    kern = functools.partial(_attn_a_kernel, half=half, nkeys=nkeys, dist_scale=float(dil),
                             seq_len=l, q_tile=q_tile)
    o, lse = pl.pallas_call(
        kern,
        out_shape=(jax.ShapeDtypeStruct((batch, l, dil * GROUP_WIDTH), jnp.float32),
                   jax.ShapeDtypeStruct((batch, dil, l, N_HEADS), jnp.float32)),
        grid=(batch, dil, l // q_tile),
        in_specs=[pl.BlockSpec((None, q_tile, GROUP_WIDTH), lambda b, r, i: (b, i, r)),
                  pl.BlockSpec((None, l, GROUP_WIDTH), lambda b, r, i: (b, 0, r)),
                  pl.BlockSpec((None, l, GROUP_WIDTH), lambda b, r, i: (b, 0, r))],
        out_specs=(pl.BlockSpec((None, q_tile, GROUP_WIDTH), lambda b, r, i: (b, i, r)),
                   pl.BlockSpec((None, None, q_tile, N_HEADS), lambda b, r, i: (b, r, i, 0))),
        scratch_shapes=[pltpu.VMEM((3, N_HEADS, Q_BLOCK, nkeys), jnp.float32)],
        compiler_params=pltpu.CompilerParams(
            dimension_semantics=("arbitrary", "arbitrary", "arbitrary"),
            vmem_limit_bytes=VMEM_LIMIT),
        name=f"attn_a_d{dil}",
    )(view(q), view(k), view(v))
    o = o.reshape(batch * seq, GROUP_WIDTH)
    lse = lse.transpose(0, 2, 1, 3).reshape(batch * seq, N_HEADS)
    return o, lse


def _attn_b_kernel(sink_ref, q_ref, k_ref, v_ref, o_ref, bias_ref, k4_ref, v4_ref, *, half, nkeys,
                   seq_len, q_tile):
    first = (pl.program_id(0) == 0) & (pl.program_id(1) == 0)

    @pl.when(first)
    def _():
        _fill_bias(bias_ref, half, nkeys, 1.0)

    @pl.when(pl.program_id(1) == 0)
    def _():
        c = lax.broadcasted_iota(jnp.int32, (B_KV_WIDTH, GROUP_WIDTH), 0)
        j = lax.broadcasted_iota(jnp.int32, (B_KV_WIDTH, GROUP_WIDTH), 1)
        src = (j // PASS_WIDTH) * HEAD_DIM + j % HEAD_DIM
        rep = jnp.where(c == src, 1.0, 0.0).astype(jnp.bfloat16)
        chunk = 512

        def body(i, carry):
            rows = pl.ds(pl.multiple_of(i * chunk, chunk), chunk)
            k4_ref[rows, :] = _dot(k_ref[rows, :], rep).astype(jnp.bfloat16)
            v4_ref[rows, :] = _dot(v_ref[rows, :], rep).astype(jnp.bfloat16)
            return carry

        lax.fori_loop(0, seq_len // chunk, body, 0)

    base = pl.program_id(1) * q_tile

    def body(jb, carry):
        row0 = pl.multiple_of(jb * Q_BLOCK, Q_BLOCK)
        _attend_block(q_ref, k4_ref, v4_ref, bias_ref, o_ref, None, sink_ref, row0, base + row0,
                      seq_len, half, nkeys)
        return carry

    lax.fori_loop(0, q_tile // Q_BLOCK, body, 0)


def _attn_b(q, k, v, sink, batch, seq):
    half = B_HALF_WINDOW
    nkeys = Q_BLOCK + 2 * half
    q_tile = 1024
    view = lambda a: a.reshape(batch, seq, a.shape[-1])
    kern = functools.partial(_attn_b_kernel, half=half, nkeys=nkeys, seq_len=seq, q_tile=q_tile)
    o = pl.pallas_call(
        kern,
        out_shape=jax.ShapeDtypeStruct((batch, seq, GROUP_WIDTH), jnp.float32),
        grid=(batch, seq // q_tile),
        in_specs=[pl.BlockSpec(memory_space=pltpu.SMEM),
                  pl.BlockSpec((None, q_tile, GROUP_WIDTH), lambda b, i: (b, i, 0)),
                  pl.BlockSpec((None, seq, B_KV_WIDTH), lambda b, i: (b, 0, 0)),
                  pl.BlockSpec((None, seq, B_KV_WIDTH), lambda b, i: (b, 0, 0))],
        out_specs=pl.BlockSpec((None, q_tile, GROUP_WIDTH), lambda b, i: (b, i, 0)),
        scratch_shapes=[pltpu.VMEM((3, N_HEADS, Q_BLOCK, nkeys), jnp.float32),
                        pltpu.VMEM((seq, GROUP_WIDTH), jnp.bfloat16),
                        pltpu.VMEM((seq, GROUP_WIDTH), jnp.bfloat16)],
        compiler_params=pltpu.CompilerParams(dimension_semantics=("arbitrary", "arbitrary"),
                                             vmem_limit_bytes=VMEM_LIMIT),
        name="attn_b",
    )(sink, view(q), view(k), view(v))
    return o.reshape(batch * seq, GROUP_WIDTH)


def _mix_kernel(o1_ref, o4_ref, o16_ref, l1_ref, l4_ref, l16_ref, ob_ref, x_ref, ga_ref, gb_ref,
                wout_ref, gffn_ref, wr_ref, br_ref, x1_ref, h2_ref, ri_ref, rw_ref):
    tm = x_ref.shape[0]
    l1, l4, l16 = l1_ref[...], l4_ref[...], l16_ref[...]
    mx = jnp.maximum(jnp.maximum(l1, l4), l16)
    e1, e4, e16 = jnp.exp(l1 - mx), jnp.exp(l4 - mx), jnp.exp(l16 - mx)
    inv = 1.0 / (e1 + e4 + e16)
    hh = lax.broadcasted_iota(jnp.int32, (N_HEADS, GROUP_WIDTH), 0)
    jj = lax.broadcasted_iota(jnp.int32, (N_HEADS, GROUP_WIDTH), 1)
    expand = jnp.where(jj // HEAD_DIM == hh, 1.0, 0.0).astype(jnp.bfloat16)

    def widen(w):
        hi, mid, lo = _split3(w)
        return _dot(hi, expand) + _dot(mid, expand) + _dot(lo, expand)

    oa = (widen(e1 * inv) * o1_ref[...] + widen(e4 * inv) * o4_ref[...]
          + widen(e16 * inv) * o16_ref[...])
    na = _rms(oa, ga_ref[...]).astype(jnp.bfloat16)
    nb = _rms(ob_ref[...], gb_ref[...]).astype(jnp.bfloat16)
    x1 = (x_ref[...] + _dot(na, wout_ref[0:GROUP_WIDTH, :])
          + _dot(nb, wout_ref[GROUP_WIDTH:2 * GROUP_WIDTH, :]))
    x1_ref[...] = x1
    h2 = _rms(x1, gffn_ref[...])
    h2_ref[...] = h2

    h_hi = h2.astype(jnp.bfloat16)
    h_lo = (h2 - h_hi.astype(jnp.float32)).astype(jnp.bfloat16)
    wr = wr_ref[...]
    w_hi = wr.astype(jnp.bfloat16)
    w_lo = (wr - w_hi.astype(jnp.float32)).astype(jnp.bfloat16)
    lg = _dot(h_hi, w_hi) + _dot(h_hi, w_lo) + _dot(h_lo, w_hi) + _dot(h_lo, w_lo)
    lgt = lg.T + br_ref[...]

    rows = lax.broadcasted_iota(jnp.int32, (EXPERTS_PER_GROUP, tm), 0)
    big = jnp.int32(1 << 20)
    glog = jnp.where(rows < N_GROUPS, lgt[0:EXPERTS_PER_GROUP], -jnp.inf)
    gmax = jnp.max(glog, axis=0, keepdims=True)
    gsel = jnp.min(jnp.where(glog == gmax, rows, big), axis=0, keepdims=True)
    gw = 1.0 / jnp.sum(jnp.exp(glog - gmax), axis=0, keepdims=True)
    esel = jnp.zeros((EXPERTS_PER_GROUP, tm), jnp.float32)
    for grp in range(N_GROUPS):
        lo = EXPERTS_PER_GROUP * (grp + 1)
        esel = jnp.where(gsel == grp, lgt[lo:lo + EXPERTS_PER_GROUP], esel)
    v0 = jnp.max(esel, axis=0, keepdims=True)
    i0 = jnp.min(jnp.where(esel == v0, rows, big), axis=0, keepdims=True)
    rest = jnp.where(rows == i0, -jnp.inf, esel)
    v1 = jnp.max(rest, axis=0, keepdims=True)
    i1 = jnp.min(jnp.where(rest == v1, rows, big), axis=0, keepdims=True)
    e = jnp.exp(v1 - v0)
    w0 = (1.0 / (1.0 + e)) * gw
    w1 = (e / (1.0 + e)) * gw
    eid0 = gsel * EXPERTS_PER_GROUP + i0
    eid1 = gsel * EXPERTS_PER_GROUP + i1
    ri_ref[...] = jnp.where(rows == 0, eid0, jnp.where(rows == 1, eid1, 0))
    rw_ref[...] = jnp.where(rows == 0, w0, jnp.where(rows == 1, w1, 0.0))


def _mix(o1, o4, o16, l1, l4, l16, ob, x2d, g_a, g_b, w_out_bf, g_ffn, wr, br):
    t = x2d.shape[0]
    tm = 512
    row = lambda w: pl.BlockSpec((tm, w), lambda i: (i, 0))
    full = lambda a: pl.BlockSpec(a.shape, lambda i: (0, 0))
    return pl.pallas_call(
        _mix_kernel,
        out_shape=(jax.ShapeDtypeStruct((t, D_MODEL), jnp.float32),
                   jax.ShapeDtypeStruct((t, D_MODEL), jnp.float32),
                   jax.ShapeDtypeStruct((EXPERTS_PER_GROUP, t), jnp.int32),
                   jax.ShapeDtypeStruct((EXPERTS_PER_GROUP, t), jnp.float32)),
        grid=(t // tm,),
        in_specs=[row(GROUP_WIDTH), row(GROUP_WIDTH), row(GROUP_WIDTH),
                  row(N_HEADS), row(N_HEADS), row(N_HEADS),
                  row(GROUP_WIDTH), row(D_MODEL),
                  full(g_a), full(g_b), full(w_out_bf), full(g_ffn), full(wr), full(br)],
        out_specs=(row(D_MODEL), row(D_MODEL),
                   pl.BlockSpec((EXPERTS_PER_GROUP, tm), lambda i: (0, i)),
                   pl.BlockSpec((EXPERTS_PER_GROUP, tm), lambda i: (0, i))),
        compiler_params=pltpu.CompilerParams(dimension_semantics=("arbitrary",),
                                             vmem_limit_bytes=VMEM_LIMIT),
        name="mix_router",
    )(o1, o4, o16, l1, l4, l16, ob, x2d, g_a, g_b, w_out_bf, g_ffn, wr, br)


def _moe_kernel(texp_ref, first_ref, nused_ref, tok_ref,
                h_hbm, ws_ref, wg_ref, wu_ref, wd_ref, y_ref,
                xbuf, wg_bf, wu_bf, wd_bf, sem):
    j = pl.program_id(0)

    @pl.when(j < nused_ref[0])
    def _():
        base = j * ROW_TILE

        def issue(r, carry):
            tok = tok_ref[base + r]
            pltpu.make_async_copy(h_hbm.at[pl.ds(tok, 1), :], xbuf.at[pl.ds(r, 1), :], sem).start()
            return carry

        lax.fori_loop(0, ROW_TILE, issue, 0, unroll=8)

        @pl.when(first_ref[j] == 1)
        def _():
            wg_bf[...] = wg_ref[...].astype(jnp.bfloat16)
            wu_bf[...] = wu_ref[...].astype(jnp.bfloat16)
            wd_bf[...] = wd_ref[...].astype(jnp.bfloat16)

        pltpu.make_async_copy(h_hbm.at[pl.ds(0, ROW_TILE), :], xbuf, sem).wait()
        x = xbuf[...].astype(jnp.bfloat16)
        a = _dot(x, wg_bf[...])
        u = _dot(x, wu_bf[...])
        hid = (a * (1.0 / (1.0 + jnp.exp(-a)))) * u * ws_ref[...]
        y_ref[...] = _dot(hid.astype(jnp.bfloat16), wd_bf[...])

    @pl.when(j >= nused_ref[0])
    def _():
        y_ref[...] = jnp.zeros_like(y_ref)


def _moe(texp, first, nused, tok_sorted, h2, w_sorted, w_gate, w_up, w_down):
    p_rows = w_sorted.shape[0]
    n_tiles = p_rows // ROW_TILE
    grid_spec = pltpu.PrefetchScalarGridSpec(
        num_scalar_prefetch=4,
        grid=(n_tiles,),
        in_specs=[
            pl.BlockSpec(memory_space=pl.ANY),
            pl.BlockSpec((ROW_TILE, 1), lambda j, te, fi, nu, tk: (j, 0)),
            pl.BlockSpec((None, D_MODEL, D_EXPERT), lambda j, te, fi, nu, tk: (te[j], 0, 0)),
            pl.BlockSpec((None, D_MODEL, D_EXPERT), lambda j, te, fi, nu, tk: (te[j], 0, 0)),
            pl.BlockSpec((None, D_EXPERT, D_MODEL), lambda j, te, fi, nu, tk: (te[j], 0, 0)),
        ],
        out_specs=pl.BlockSpec((ROW_TILE, D_MODEL), lambda j, te, fi, nu, tk: (j, 0)),
        scratch_shapes=[pltpu.VMEM((ROW_TILE, D_MODEL), jnp.float32),
                        pltpu.VMEM((D_MODEL, D_EXPERT), jnp.bfloat16),
                        pltpu.VMEM((D_MODEL, D_EXPERT), jnp.bfloat16),
                        pltpu.VMEM((D_EXPERT, D_MODEL), jnp.bfloat16),
                        pltpu.SemaphoreType.DMA],
    )
    return pl.pallas_call(
        _moe_kernel,
        out_shape=jax.ShapeDtypeStruct((p_rows, D_MODEL), jnp.float32),
        grid_spec=grid_spec,
        compiler_params=pltpu.CompilerParams(dimension_semantics=("arbitrary",),
                                             vmem_limit_bytes=VMEM_LIMIT),
        name="moe_experts",
    )(texp, first, nused, tok_sorted, h2, w_sorted, w_gate, w_up, w_down)


def _final_kernel(pos_ref, y_hbm, x1_ref, p_ref, gple_ref, wg_ref, wp_ref, gfin_ref, out_ref,
                  ybuf, sem, *, n_tok, tm):
    base = pl.program_id(0) * tm

    def issue(r, carry):
        for k in range(2):
            pos = pos_ref[k * n_tok + base + r]
            pltpu.make_async_copy(y_hbm.at[pl.ds(pos, 1), :], ybuf.at[pl.ds(k * tm + r, 1), :],
                                  sem).start()
        return carry

    lax.fori_loop(0, tm, issue, 0, unroll=8)
    pp = _dot(p_ref[...].astype(jnp.bfloat16), wp_ref[...])
    pltpu.make_async_copy(y_hbm.at[pl.ds(0, 2 * tm), :], ybuf, sem).wait()
    x2 = x1_ref[...] + (ybuf[0:tm, :] + ybuf[tm:2 * tm, :])
    z = _dot(_rms(x2, gple_ref[...]).astype(jnp.bfloat16), wg_ref[...])
    gate = 1.0 / (1.0 + jnp.exp(-z))
    x3 = x2 + pp * gate
    out_ref[...] = _rms(x3, gfin_ref[...])


def _final(pos, y, x1, p2d, g_ple, w_gate_bf, w_proj_bf, g_final):
    t = x1.shape[0]
    tm = 256
    row = lambda w: pl.BlockSpec((tm, w), lambda i, ps: (i, 0))
    full = lambda a: pl.BlockSpec(a.shape, lambda i, ps: (0, 0))
    grid_spec = pltpu.PrefetchScalarGridSpec(
        num_scalar_prefetch=1,
        grid=(t // tm,),
        in_specs=[pl.BlockSpec(memory_space=pl.ANY), row(D_MODEL), row(PLE_DIM),
                  full(g_ple), full(w_gate_bf), full(w_proj_bf), full(g_final)],
        out_specs=row(D_MODEL),
        scratch_shapes=[pltpu.VMEM((2 * tm, D_MODEL), jnp.float32), pltpu.SemaphoreType.DMA],
    )
    return pl.pallas_call(
        functools.partial(_final_kernel, n_tok=t, tm=tm),
        out_shape=jax.ShapeDtypeStruct((t, D_MODEL), jnp.float32),
        grid_spec=grid_spec,
        compiler_params=pltpu.CompilerParams(dimension_semantics=("arbitrary",),
                                             vmem_limit_bytes=VMEM_LIMIT),
        name="combine_ple_final",
    )(pos, y, x1, p2d, g_ple, w_gate_bf, w_proj_bf, g_final)


def _dispatch_tables(route_i, route_w, n_tok):
    e_flat = route_i[:2].reshape(-1)
    w_flat = route_w[:2].reshape(-1)
    n_pairs = 2 * n_tok
    p_rows = n_pairs + N_EXPERTS * ROW_TILE
    n_tiles = p_rows // ROW_TILE
    onehot = (e_flat[:, None] == jnp.arange(N_EXPERTS, dtype=jnp.int32)[None, :]).astype(jnp.int32)
    csum = jnp.cumsum(onehot, axis=0)
    rank = jnp.sum(onehot * csum, axis=1) - 1
    counts = csum[-1]
    tiles_per = (counts + ROW_TILE - 1) // ROW_TILE
    tile_end = jnp.cumsum(tiles_per)
    tile_start = tile_end - tiles_per
    pos = tile_start[e_flat] * ROW_TILE + rank
    tok = jnp.arange(n_pairs, dtype=jnp.int32) % n_tok
    tok_sorted = jnp.zeros((p_rows,), jnp.int32).at[pos].set(tok, unique_indices=True)
    w_sorted = jnp.zeros((p_rows,), jnp.float32).at[pos].set(w_flat, unique_indices=True)
    n_used = tile_end[-1]
    tile_ids = jnp.arange(n_tiles, dtype=jnp.int32)
    tsrc = jnp.minimum(tile_ids, n_used - 1)
    texp = jnp.sum((tile_end[None, :] <= tsrc[:, None]).astype(jnp.int32), axis=1)
    first = ((tile_ids == tile_start[texp]) & (tile_ids < n_used)).astype(jnp.int32)
    return (texp, first, n_used.reshape(1).astype(jnp.int32), tok_sorted,
            w_sorted.reshape(p_rows, 1), pos.astype(jnp.int32))


def kernel(x, p, g_mix, w_in, sink, g_grp_a, g_grp_b, w_out, g_ffn, w_router_group, b_router_group,
           w_router_expert, b_router_expert, w_expert_gate, w_expert_up, w_expert_down, g_ple,
           w_ple_gate, w_ple_proj, g_final):
    batch, seq, _ = x.shape
    n_tok = batch * seq
    depth = w_in.shape[0]
    assert depth == 1, "the final RMSNorm is fused into the last layer's epilogue"
    bf = jnp.bfloat16
    xc = x.reshape(n_tok, D_MODEL)
    for i in range(depth):
        qa, ka, va, qb, kb, vb = _proj(xc, g_mix[i][None, :], w_in[i].astype(bf))
        outs = [_attn_a(qa, ka, va, batch, seq, dil, window) for window, dil in A_CONFIGS]
        ob = _attn_b(qb, kb, vb, sink[i], batch, seq)

        wr = jnp.zeros((D_MODEL, ROUTER_ROWS), jnp.float32)
        wr = wr.at[:, 0:N_GROUPS].set(w_router_group[i])
        wr = wr.at[:, EXPERTS_PER_GROUP:EXPERTS_PER_GROUP + N_EXPERTS].set(w_router_expert[i])
        br = jnp.zeros((ROUTER_ROWS, 1), jnp.float32)
        br = br.at[0:N_GROUPS, 0].set(b_router_group[i])
        br = br.at[EXPERTS_PER_GROUP:EXPERTS_PER_GROUP + N_EXPERTS, 0].set(b_router_expert[i])

        x1, h2, route_i, route_w = _mix(
            outs[0][0], outs[1][0], outs[2][0], outs[0][1], outs[1][1], outs[2][1], ob, xc,
            g_grp_a[i][None, :], g_grp_b[i][None, :], w_out[i].astype(bf), g_ffn[i][None, :], wr, br)

        texp, first, nused, tok_sorted, w_sorted, pos = _dispatch_tables(route_i, route_w, n_tok)
        y = _moe(texp, first, nused, tok_sorted, h2, w_sorted,
                 w_expert_gate[i], w_expert_up[i], w_expert_down[i])
        xc = _final(pos, y, x1, p[i].reshape(n_tok, PLE_DIM), g_ple[i][None, :],
                    w_ple_gate[i].astype(bf), w_ple_proj[i].astype(bf), g_final[None, :])
    return xc.reshape(batch, seq, D_MODEL)
```

```python
import functools

import jax
import jax.numpy as jnp
from jax import lax
from jax.experimental import pallas as pl
from jax.experimental.pallas import tpu as pltpu

D_MODEL = 1024
HEAD_DIM = 64
GROUP_WIDTH = 512
N_HEADS = 8
B_KV_HEADS = 2
B_KV_WIDTH = B_KV_HEADS * HEAD_DIM
A_CONFIGS = ((128, 1), (512, 4), (2048, 16))
B_HALF_WINDOW = 128
N_GROUPS = 4
EXPERTS_PER_GROUP = 8
N_EXPERTS = N_GROUPS * EXPERTS_PER_GROUP
D_EXPERT = 512
PLE_DIM = 256
EPS = 1e-6
NEG = -1e30

Q_BLOCK = 128
HEADS_PER_PASS = 4
PASS_WIDTH = HEADS_PER_PASS * HEAD_DIM
ROW_TILE = 256
ROUTER_ROWS = 128
VMEM_LIMIT = 48 * 1024 * 1024


def _rms(x, g):
    r = lax.rsqrt(jnp.mean(x * x, axis=-1, keepdims=True) + EPS)
    return (x * r) * g


def _split3(x):
    hi = x.astype(jnp.bfloat16)
    r1 = x - hi.astype(jnp.float32)
    mid = r1.astype(jnp.bfloat16)
    lo = (r1 - mid.astype(jnp.float32)).astype(jnp.bfloat16)
    return hi, mid, lo


def _dot(a, b):
    return jnp.dot(a, b, preferred_element_type=jnp.float32)


def _proj_kernel(x_ref, g_ref, w_ref, qa_ref, ka_ref, va_ref, qb_ref, kb_ref, vb_ref):
    h = _rms(x_ref[...], g_ref[...]).astype(jnp.bfloat16)
    scale = HEAD_DIM ** -0.5
    W = GROUP_WIDTH
    qa_ref[...] = (_dot(h, w_ref[:, 0:W]) * scale).astype(jnp.bfloat16)
    ka_ref[...] = _dot(h, w_ref[:, W:2 * W]).astype(jnp.bfloat16)
    va_ref[...] = _dot(h, w_ref[:, 2 * W:3 * W]).astype(jnp.bfloat16)
    qb_ref[...] = (_dot(h, w_ref[:, 3 * W:4 * W]) * scale).astype(jnp.bfloat16)
    kb_ref[...] = _dot(h, w_ref[:, 4 * W:4 * W + B_KV_WIDTH]).astype(jnp.bfloat16)
    vb_ref[...] = _dot(h, w_ref[:, 4 * W + B_KV_WIDTH:4 * W + 2 * B_KV_WIDTH]).astype(jnp.bfloat16)


def _proj(x2d, g_mix, w_in_bf):
    t = x2d.shape[0]
    tm = 512
    in_width = w_in_bf.shape[1]
    wide = jax.ShapeDtypeStruct((t, GROUP_WIDTH), jnp.bfloat16)
    narrow = jax.ShapeDtypeStruct((t, B_KV_WIDTH), jnp.bfloat16)
    row = lambda w: pl.BlockSpec((tm, w), lambda i: (i, 0))
    return pl.pallas_call(
        _proj_kernel,
        out_shape=(wide, wide, wide, wide, narrow, narrow),
        grid=(t // tm,),
        in_specs=[row(D_MODEL),
                  pl.BlockSpec((1, D_MODEL), lambda i: (0, 0)),
                  pl.BlockSpec((D_MODEL, in_width), lambda i: (0, 0))],
        out_specs=(row(GROUP_WIDTH), row(GROUP_WIDTH), row(GROUP_WIDTH), row(GROUP_WIDTH),
                   row(B_KV_WIDTH), row(B_KV_WIDTH)),
        compiler_params=pltpu.CompilerParams(dimension_semantics=("arbitrary",),
                                             vmem_limit_bytes=VMEM_LIMIT),
        name="proj",
    )(x2d, g_mix, w_in_bf)


def _fill_bias(bias_ref, half, nkeys, dist_scale):
    i = lax.broadcasted_iota(jnp.int32, (Q_BLOCK, nkeys), 0)
    j = lax.broadcasted_iota(jnp.int32, (Q_BLOCK, nkeys), 1)
    for var, rel0 in enumerate((0, -half, Q_BLOCK - nkeys)):
        d = jnp.abs(j + rel0 - i)
        dist = d.astype(jnp.float32) * dist_scale
        for h in range(N_HEADS):
            slope = 2.0 ** (-(h + 1))
            bias_ref[var, h] = jnp.where(d <= half, -(slope * dist), NEG)


def _attend_block(q_ref, k_ref, v_ref, bias_ref, o_ref, lse_ref, sink_ref, row0, q0, seq_len,
                  half, nkeys):
    lo = q0 - half
    ks = pl.multiple_of(jnp.clip(lo, 0, seq_len - nkeys), HEAD_DIM)
    var = jnp.where(lo < 0, 0, jnp.where(lo > seq_len - nkeys, 2, 1))
    lane_head = lax.broadcasted_iota(jnp.int32, (Q_BLOCK, PASS_WIDTH), 1) // HEAD_DIM
    for g in range(N_HEADS // HEADS_PER_PASS):
        cols = slice(g * PASS_WIDTH, (g + 1) * PASS_WIDTH)
        q = q_ref[pl.ds(row0, Q_BLOCK), cols]
        zero = jnp.zeros_like(q)
        q_stack = jnp.concatenate(
            [jnp.where(lane_head == h, q, zero) for h in range(HEADS_PER_PASS)], axis=0)
        k = k_ref[pl.ds(ks, nkeys), cols]
        v = v_ref[pl.ds(ks, nkeys), cols]
        s = lax.dot_general(q_stack, k, (((1,), (1,)), ((), ())),
                            preferred_element_type=jnp.float32)
        s = s.reshape(HEADS_PER_PASS, Q_BLOCK, nkeys)
        s = s + bias_ref[var, pl.ds(g * HEADS_PER_PASS, HEADS_PER_PASS)]
        m = jnp.max(s, axis=-1, keepdims=True)
        if sink_ref is not None:
            sk = jnp.stack([jnp.full((Q_BLOCK, 1), sink_ref[g * HEADS_PER_PASS + h], jnp.float32)
                            for h in range(HEADS_PER_PASS)], axis=0)
            m = jnp.maximum(m, sk)
        p = jnp.exp(s - m)
        den = jnp.sum(p, axis=-1, keepdims=True)
        if sink_ref is not None:
            den = den + jnp.exp(sk - m)
        pv = _dot(p.reshape(HEADS_PER_PASS * Q_BLOCK, nkeys).astype(jnp.bfloat16), v)
        pv = pv.reshape(HEADS_PER_PASS, Q_BLOCK, PASS_WIDTH) * (1.0 / den)
        o = jnp.zeros((Q_BLOCK, PASS_WIDTH), jnp.float32)
        for h in range(HEADS_PER_PASS):
            o = jnp.where(lane_head == h, pv[h], o)
        o_ref[pl.ds(row0, Q_BLOCK), cols] = o
        if lse_ref is not None:
            lse = m + jnp.log(den)
            for h in range(HEADS_PER_PASS):
                c = g * HEADS_PER_PASS + h
                lse_ref[pl.ds(row0, Q_BLOCK), c:c + 1] = lse[h]


def _attn_a_kernel(q_ref, k_ref, v_ref, o_ref, lse_ref, bias_ref, *, half, nkeys, dist_scale,
                   seq_len, q_tile):
    first = (pl.program_id(0) == 0) & (pl.program_id(1) == 0) & (pl.program_id(2) == 0)

    @pl.when(first)
    def _():
        _fill_bias(bias_ref, half, nkeys, dist_scale)

    base = pl.program_id(2) * q_tile

    def body(jb, carry):
        row0 = pl.multiple_of(jb * Q_BLOCK, Q_BLOCK)
        _attend_block(q_ref, k_ref, v_ref, bias_ref, o_ref, lse_ref, None, row0, base + row0,
                      seq_len, half, nkeys)
        return carry

    lax.fori_loop(0, q_tile // Q_BLOCK, body, 0)


def _attn_a(q, k, v, batch, seq, dil, window):
    l = seq // dil
    half = window // (2 * dil)
    nkeys = Q_BLOCK + 2 * half
    q_tile = min(l, 1024)
    view = lambda a: a.reshape(batch, l, dil * GROUP_WIDTH)
    kern = functools.partial(_attn_a_kernel, half=half, nkeys=nkeys, dist_scale=float(dil),
                             seq_len=l, q_tile=q_tile)
    o, lse = pl.pallas_call(
        kern,
        out_shape=(jax.ShapeDtypeStruct((batch, l, dil * GROUP_WIDTH), jnp.float32),
                   jax.ShapeDtypeStruct((batch, dil, l, N_HEADS), jnp.float32)),
        grid=(batch, dil, l // q_tile),
        in_specs=[pl.BlockSpec((None, q_tile, GROUP_WIDTH), lambda b, r, i: (b, i, r)),
                  pl.BlockSpec((None, l, GROUP_WIDTH), lambda b, r, i: (b, 0, r)),
                  pl.BlockSpec((None, l, GROUP_WIDTH), lambda b, r, i: (b, 0, r))],
        out_specs=(pl.BlockSpec((None, q_tile, GROUP_WIDTH), lambda b, r, i: (b, i, r)),
                   pl.BlockSpec((None, None, q_tile, N_HEADS), lambda b, r, i: (b, r, i, 0))),
        scratch_shapes=[pltpu.VMEM((3, N_HEADS, Q_BLOCK, nkeys), jnp.float32)],
        compiler_params=pltpu.CompilerParams(
            dimension_semantics=("arbitrary", "arbitrary", "arbitrary"),
            vmem_limit_bytes=VMEM_LIMIT),
        name=f"attn_a_d{dil}",
    )(view(q), view(k), view(v))
    o = o.reshape(batch * seq, GROUP_WIDTH)
    lse = lse.transpose(0, 2, 1, 3).reshape(batch * seq, N_HEADS)
    return o, lse


def _attn_b_kernel(sink_ref, q_ref, k_ref, v_ref, o_ref, bias_ref, k4_ref, v4_ref, *, half, nkeys,
                   seq_len, q_tile):
    first = (pl.program_id(0) == 0) & (pl.program_id(1) == 0)

    @pl.when(first)
    def _():
        _fill_bias(bias_ref, half, nkeys, 1.0)

    @pl.when(pl.program_id(1) == 0)
    def _():
        c = lax.broadcasted_iota(jnp.int32, (B_KV_WIDTH, GROUP_WIDTH), 0)
        j = lax.broadcasted_iota(jnp.int32, (B_KV_WIDTH, GROUP_WIDTH), 1)
        src = (j // PASS_WIDTH) * HEAD_DIM + j % HEAD_DIM
        rep = jnp.where(c == src, 1.0, 0.0).astype(jnp.bfloat16)
        chunk = 512

        def body(i, carry):
            rows = pl.ds(pl.multiple_of(i * chunk, chunk), chunk)
            k4_ref[rows, :] = _dot(k_ref[rows, :], rep).astype(jnp.bfloat16)
            v4_ref[rows, :] = _dot(v_ref[rows, :], rep).astype(jnp.bfloat16)
            return carry

        lax.fori_loop(0, seq_len // chunk, body, 0)

    base = pl.program_id(1) * q_tile

    def body(jb, carry):
        row0 = pl.multiple_of(jb * Q_BLOCK, Q_BLOCK)
        _attend_block(q_ref, k4_ref, v4_ref, bias_ref, o_ref, None, sink_ref, row0, base + row0,
                      seq_len, half, nkeys)
        return carry

    lax.fori_loop(0, q_tile // Q_BLOCK, body, 0)


def _attn_b(q, k, v, sink, batch, seq):
    half = B_HALF_WINDOW
    nkeys = Q_BLOCK + 2 * half
    q_tile = 1024
    view = lambda a: a.reshape(batch, seq, a.shape[-1])
    kern = functools.partial(_attn_b_kernel, half=half, nkeys=nkeys, seq_len=seq, q_tile=q_tile)
    o = pl.pallas_call(
        kern,
        out_shape=jax.ShapeDtypeStruct((batch, seq, GROUP_WIDTH), jnp.float32),
        grid=(batch, seq // q_tile),
        in_specs=[pl.BlockSpec(memory_space=pltpu.SMEM),
                  pl.BlockSpec((None, q_tile, GROUP_WIDTH), lambda b, i: (b, i, 0)),
                  pl.BlockSpec((None, seq, B_KV_WIDTH), lambda b, i: (b, 0, 0)),
                  pl.BlockSpec((None, seq, B_KV_WIDTH), lambda b, i: (b, 0, 0))],
        out_specs=pl.BlockSpec((None, q_tile, GROUP_WIDTH), lambda b, i: (b, i, 0)),
        scratch_shapes=[pltpu.VMEM((3, N_HEADS, Q_BLOCK, nkeys), jnp.float32),
                        pltpu.VMEM((seq, GROUP_WIDTH), jnp.bfloat16),
                        pltpu.VMEM((seq, GROUP_WIDTH), jnp.bfloat16)],
        compiler_params=pltpu.CompilerParams(dimension_semantics=("arbitrary", "arbitrary"),
                                             vmem_limit_bytes=VMEM_LIMIT),
        name="attn_b",
    )(sink, view(q), view(k), view(v))
    return o.reshape(batch * seq, GROUP_WIDTH)


def _mix_kernel(o1_ref, o4_ref, o16_ref, l1_ref, l4_ref, l16_ref, ob_ref, x_ref, ga_ref, gb_ref,
                wout_ref, gffn_ref, wr_ref, br_ref, x1_ref, h2_ref, ri_ref, rw_ref):
    tm = x_ref.shape[0]
    l1, l4, l16 = l1_ref[...], l4_ref[...], l16_ref[...]
    mx = jnp.maximum(jnp.maximum(l1, l4), l16)
    e1, e4, e16 = jnp.exp(l1 - mx), jnp.exp(l4 - mx), jnp.exp(l16 - mx)
    inv = 1.0 / (e1 + e4 + e16)
    hh = lax.broadcasted_iota(jnp.int32, (N_HEADS, GROUP_WIDTH), 0)
    jj = lax.broadcasted_iota(jnp.int32, (N_HEADS, GROUP_WIDTH), 1)
    expand = jnp.where(jj // HEAD_DIM == hh, 1.0, 0.0).astype(jnp.bfloat16)

    def widen(w):
        hi, mid, lo = _split3(w)
        return _dot(hi, expand) + _dot(mid, expand) + _dot(lo, expand)

    oa = (widen(e1 * inv) * o1_ref[...] + widen(e4 * inv) * o4_ref[...]
          + widen(e16 * inv) * o16_ref[...])
    na = _rms(oa, ga_ref[...]).astype(jnp.bfloat16)
    nb = _rms(ob_ref[...], gb_ref[...]).astype(jnp.bfloat16)
    x1 = (x_ref[...] + _dot(na, wout_ref[0:GROUP_WIDTH, :])
          + _dot(nb, wout_ref[GROUP_WIDTH:2 * GROUP_WIDTH, :]))
    x1_ref[...] = x1
    h2 = _rms(x1, gffn_ref[...])
    h2_ref[...] = h2

    h_hi = h2.astype(jnp.bfloat16)
    h_lo = (h2 - h_hi.astype(jnp.float32)).astype(jnp.bfloat16)
    wr = wr_ref[...]
    w_hi = wr.astype(jnp.bfloat16)
    w_lo = (wr - w_hi.astype(jnp.float32)).astype(jnp.bfloat16)
    lg = _dot(h_hi, w_hi) + _dot(h_hi, w_lo) + _dot(h_lo, w_hi) + _dot(h_lo, w_lo)
    lgt = lg.T + br_ref[...]

    rows = lax.broadcasted_iota(jnp.int32, (EXPERTS_PER_GROUP, tm), 0)
    big = jnp.int32(1 << 20)
    glog = jnp.where(rows < N_GROUPS, lgt[0:EXPERTS_PER_GROUP], -jnp.inf)
    gmax = jnp.max(glog, axis=0, keepdims=True)
    gsel = jnp.min(jnp.where(glog == gmax, rows, big), axis=0, keepdims=True)
    gw = 1.0 / jnp.sum(jnp.exp(glog - gmax), axis=0, keepdims=True)
    esel = jnp.zeros((EXPERTS_PER_GROUP, tm), jnp.float32)
    for grp in range(N_GROUPS):
        lo = EXPERTS_PER_GROUP * (grp + 1)
        esel = jnp.where(gsel == grp, lgt[lo:lo + EXPERTS_PER_GROUP], esel)
    v0 = jnp.max(esel, axis=0, keepdims=True)
    i0 = jnp.min(jnp.where(esel == v0, rows, big), axis=0, keepdims=True)
    rest = jnp.where(rows == i0, -jnp.inf, esel)
    v1 = jnp.max(rest, axis=0, keepdims=True)
    i1 = jnp.min(jnp.where(rest == v1, rows, big), axis=0, keepdims=True)
    e = jnp.exp(v1 - v0)
    w0 = (1.0 / (1.0 + e)) * gw
    w1 = (e / (1.0 + e)) * gw
    eid0 = gsel * EXPERTS_PER_GROUP + i0
    eid1 = gsel * EXPERTS_PER_GROUP + i1
    ri_ref[...] = jnp.where(rows == 0, eid0, jnp.where(rows == 1, eid1, 0))
    rows_t = lax.broadcasted_iota(jnp.int32, (ROUTER_ROWS, tm), 0)
    slab = jnp.where(rows_t == 0, w0, jnp.where(rows_t == 1, w1, 0.0))
    rw_ref[...] = slab.T[:, 0:EXPERTS_PER_GROUP]


def _mix(o1, o4, o16, l1, l4, l16, ob, x2d, g_a, g_b, w_out_bf, g_ffn, wr, br):
    t = x2d.shape[0]
    tm = 512
    row = lambda w: pl.BlockSpec((tm, w), lambda i: (i, 0))
    full = lambda a: pl.BlockSpec(a.shape, lambda i: (0, 0))
    return pl.pallas_call(
        _mix_kernel,
        out_shape=(jax.ShapeDtypeStruct((t, D_MODEL), jnp.float32),
                   jax.ShapeDtypeStruct((t, D_MODEL), jnp.float32),
                   jax.ShapeDtypeStruct((EXPERTS_PER_GROUP, t), jnp.int32),
                   jax.ShapeDtypeStruct((t, EXPERTS_PER_GROUP), jnp.float32)),
        grid=(t // tm,),
        in_specs=[row(GROUP_WIDTH), row(GROUP_WIDTH), row(GROUP_WIDTH),
                  row(N_HEADS), row(N_HEADS), row(N_HEADS),
                  row(GROUP_WIDTH), row(D_MODEL),
                  full(g_a), full(g_b), full(w_out_bf), full(g_ffn), full(wr), full(br)],
        out_specs=(row(D_MODEL), row(D_MODEL),
                   pl.BlockSpec((EXPERTS_PER_GROUP, tm), lambda i: (0, i)),
                   row(EXPERTS_PER_GROUP)),
        compiler_params=pltpu.CompilerParams(dimension_semantics=("arbitrary",),
                                             vmem_limit_bytes=VMEM_LIMIT),
        name="mix_router",
    )(o1, o4, o16, l1, l4, l16, ob, x2d, g_a, g_b, w_out_bf, g_ffn, wr, br)


DISPATCH_CHUNK = 2048


def _dispatch_kernel(pos_ref, pad_tile_ref, nused_ref, h_hbm, xs_hbm, zbuf, sem, zsem, *, n_tok,
                     n_tiles):
    step = pl.program_id(0)

    def zero_tile(tile):
        rows = pl.ds(pl.multiple_of(tile * ROW_TILE, ROW_TILE), ROW_TILE)
        return pltpu.make_async_copy(zbuf, xs_hbm.at[rows, :], zsem)

    @pl.when(step == 0)
    def _():
        zbuf[...] = jnp.zeros_like(zbuf)
        for e in range(N_EXPERTS):
            @pl.when(pad_tile_ref[e] >= 0)
            def _():
                zero_tile(pad_tile_ref[e]).start()
        lax.fori_loop(nused_ref[0], n_tiles, lambda t, c: (zero_tile(t).start(), c)[1], 0)
        for e in range(N_EXPERTS):
            @pl.when(pad_tile_ref[e] >= 0)
            def _():
                zero_tile(pad_tile_ref[e]).wait()
        lax.fori_loop(nused_ref[0], n_tiles, lambda t, c: (zero_tile(t).wait(), c)[1], 0)

    base = step * DISPATCH_CHUNK
    tok0 = base % n_tok

    def issue(i, carry):
        pos = pos_ref[base + i]
        pltpu.make_async_copy(h_hbm.at[pl.ds(tok0 + i, 1), :], xs_hbm.at[pl.ds(pos, 1), :],
                              sem).start()
        return carry

    lax.fori_loop(0, DISPATCH_CHUNK, issue, 0, unroll=8)
    pltpu.make_async_copy(h_hbm.at[pl.ds(0, DISPATCH_CHUNK), :],
                          xs_hbm.at[pl.ds(0, DISPATCH_CHUNK), :], sem).wait()


def _dispatch(pos, pad_tile, nused, h2, p_rows):
    n_tok = h2.shape[0]
    assert n_tok % DISPATCH_CHUNK == 0
    grid_spec = pltpu.PrefetchScalarGridSpec(
        num_scalar_prefetch=3,
        grid=(2 * n_tok // DISPATCH_CHUNK,),
        in_specs=[pl.BlockSpec(memory_space=pl.ANY)],
        out_specs=pl.BlockSpec(memory_space=pl.ANY),
        scratch_shapes=[pltpu.VMEM((ROW_TILE, D_MODEL), jnp.float32),
                        pltpu.SemaphoreType.DMA, pltpu.SemaphoreType.DMA],
    )
    return pl.pallas_call(
        functools.partial(_dispatch_kernel, n_tok=n_tok, n_tiles=p_rows // ROW_TILE),
        out_shape=jax.ShapeDtypeStruct((p_rows, D_MODEL), jnp.float32),
        grid_spec=grid_spec,
        compiler_params=pltpu.CompilerParams(dimension_semantics=("arbitrary",),
                                             vmem_limit_bytes=VMEM_LIMIT),
        name="dispatch",
    )(pos, pad_tile, nused, h2)


def _moe_kernel(texp_ref, first_ref, nused_ref, xs_ref, wg_ref, wu_ref, wd_ref, y_ref,
                wg_bf, wu_bf, wd_bf):
    j = pl.program_id(0)

    @pl.when(j < nused_ref[0])
    def _():
        @pl.when(first_ref[j] == 1)
        def _():
            wg_bf[...] = wg_ref[...].astype(jnp.bfloat16)
            wu_bf[...] = wu_ref[...].astype(jnp.bfloat16)
            wd_bf[...] = wd_ref[...].astype(jnp.bfloat16)

        x = xs_ref[...].astype(jnp.bfloat16)
        a = _dot(x, wg_bf[...])
        u = _dot(x, wu_bf[...])
        hid = (a * (1.0 / (1.0 + jnp.exp(-a)))) * u
        y_ref[...] = _dot(hid.astype(jnp.bfloat16), wd_bf[...])

    @pl.when(j >= nused_ref[0])
    def _():
        y_ref[...] = jnp.zeros_like(y_ref)


def _moe(texp, first, nused, xs, w_gate, w_up, w_down):
    p_rows = xs.shape[0]
    n_tiles = p_rows // ROW_TILE
    used = lambda j, nu: jnp.minimum(j, nu[0] - 1)
    grid_spec = pltpu.PrefetchScalarGridSpec(
        num_scalar_prefetch=3,
        grid=(n_tiles,),
        in_specs=[
            pl.BlockSpec((ROW_TILE, D_MODEL), lambda j, te, fi, nu: (used(j, nu), 0)),
            pl.BlockSpec((None, D_MODEL, D_EXPERT), lambda j, te, fi, nu: (te[j], 0, 0)),
            pl.BlockSpec((None, D_MODEL, D_EXPERT), lambda j, te, fi, nu: (te[j], 0, 0)),
            pl.BlockSpec((None, D_EXPERT, D_MODEL), lambda j, te, fi, nu: (te[j], 0, 0)),
        ],
        out_specs=pl.BlockSpec((ROW_TILE, D_MODEL), lambda j, te, fi, nu: (j, 0)),
        scratch_shapes=[pltpu.VMEM((D_MODEL, D_EXPERT), jnp.bfloat16),
                        pltpu.VMEM((D_MODEL, D_EXPERT), jnp.bfloat16),
                        pltpu.VMEM((D_EXPERT, D_MODEL), jnp.bfloat16)],
    )
    return pl.pallas_call(
        _moe_kernel,
        out_shape=jax.ShapeDtypeStruct((p_rows, D_MODEL), jnp.float32),
        grid_spec=grid_spec,
        compiler_params=pltpu.CompilerParams(dimension_semantics=("arbitrary",),
                                             vmem_limit_bytes=VMEM_LIMIT),
        name="moe_experts",
    )(texp, first, nused, xs, w_gate, w_up, w_down)


def _final_kernel(pos_ref, y_hbm, x1_ref, rw_ref, p_ref, gple_ref, wg_ref, wp_ref, gfin_ref, out_ref,
                  ybuf, sem, *, n_tok, tm):
    base = pl.program_id(0) * tm

    def issue(r, carry):
        for k in range(2):
            pos = pos_ref[k * n_tok + base + r]
            pltpu.make_async_copy(y_hbm.at[pl.ds(pos, 1), :], ybuf.at[pl.ds(k * tm + r, 1), :],
                                  sem).start()
        return carry

    lax.fori_loop(0, tm, issue, 0, unroll=8)
    pp = _dot(p_ref[...].astype(jnp.bfloat16), wp_ref[...])
    pltpu.make_async_copy(y_hbm.at[pl.ds(0, 2 * tm), :], ybuf, sem).wait()
    rw = rw_ref[...]
    x2 = x1_ref[...] + (rw[:, 0:1] * ybuf[0:tm, :] + rw[:, 1:2] * ybuf[tm:2 * tm, :])
    z = _dot(_rms(x2, gple_ref[...]).astype(jnp.bfloat16), wg_ref[...])
    gate = 1.0 / (1.0 + jnp.exp(-z))
    x3 = x2 + pp * gate
    out_ref[...] = _rms(x3, gfin_ref[...])


def _final(pos, y, x1, rw, p2d, g_ple, w_gate_bf, w_proj_bf, g_final):
    t = x1.shape[0]
    tm = 256
    row = lambda w: pl.BlockSpec((tm, w), lambda i, ps: (i, 0))
    full = lambda a: pl.BlockSpec(a.shape, lambda i, ps: (0, 0))
    grid_spec = pltpu.PrefetchScalarGridSpec(
        num_scalar_prefetch=1,
        grid=(t // tm,),
        in_specs=[pl.BlockSpec(memory_space=pl.ANY), row(D_MODEL), row(EXPERTS_PER_GROUP),
                  row(PLE_DIM), full(g_ple), full(w_gate_bf), full(w_proj_bf), full(g_final)],
        out_specs=row(D_MODEL),
        scratch_shapes=[pltpu.VMEM((2 * tm, D_MODEL), jnp.float32), pltpu.SemaphoreType.DMA],
    )
    return pl.pallas_call(
        functools.partial(_final_kernel, n_tok=t, tm=tm),
        out_shape=jax.ShapeDtypeStruct((t, D_MODEL), jnp.float32),
        grid_spec=grid_spec,
        compiler_params=pltpu.CompilerParams(dimension_semantics=("arbitrary",),
                                             vmem_limit_bytes=VMEM_LIMIT),
        name="combine_ple_final",
    )(pos, y, x1, rw, p2d, g_ple, w_gate_bf, w_proj_bf, g_final)


def _dispatch_tables(route_i, n_tok):
    e_flat = route_i[:2].reshape(-1)
    n_pairs = 2 * n_tok
    p_rows = n_pairs + N_EXPERTS * ROW_TILE
    n_tiles = p_rows // ROW_TILE
    onehot = (e_flat[:, None] == jnp.arange(N_EXPERTS, dtype=jnp.int32)[None, :]).astype(jnp.int32)
    csum = jnp.cumsum(onehot, axis=0)
    rank = jnp.sum(onehot * csum, axis=1) - 1
    counts = csum[-1]
    tiles_per = (counts + ROW_TILE - 1) // ROW_TILE
    tile_end = jnp.cumsum(tiles_per)
    tile_start = tile_end - tiles_per
    pos = tile_start[e_flat] * ROW_TILE + rank
    n_used = tile_end[-1]
    tile_ids = jnp.arange(n_tiles, dtype=jnp.int32)
    clamped = jnp.minimum(tile_ids, n_used - 1)
    texp = jnp.sum((tile_end[None, :] <= clamped[:, None]).astype(jnp.int32), axis=1)
    first = ((tile_ids == tile_start[texp]) & (tile_ids < n_used)).astype(jnp.int32)
    pad_tile = jnp.where(counts > 0, tile_end - 1, -1).astype(jnp.int32)
    return (texp, first, n_used.reshape(1).astype(jnp.int32), pad_tile, pos.astype(jnp.int32),
            p_rows)


def kernel(x, p, g_mix, w_in, sink, g_grp_a, g_grp_b, w_out, g_ffn, w_router_group, b_router_group,
           w_router_expert, b_router_expert, w_expert_gate, w_expert_up, w_expert_down, g_ple,
           w_ple_gate, w_ple_proj, g_final):
    batch, seq, _ = x.shape
    n_tok = batch * seq
    depth = w_in.shape[0]
    assert depth == 1, "the final RMSNorm is fused into the last layer's epilogue"
    bf = jnp.bfloat16
    xc = x.reshape(n_tok, D_MODEL)
    for i in range(depth):
        qa, ka, va, qb, kb, vb = _proj(xc, g_mix[i][None, :], w_in[i].astype(bf))
        outs = [_attn_a(qa, ka, va, batch, seq, dil, window) for window, dil in A_CONFIGS]
        ob = _attn_b(qb, kb, vb, sink[i], batch, seq)

        wr = jnp.zeros((D_MODEL, ROUTER_ROWS), jnp.float32)
        wr = wr.at[:, 0:N_GROUPS].set(w_router_group[i])
        wr = wr.at[:, EXPERTS_PER_GROUP:EXPERTS_PER_GROUP + N_EXPERTS].set(w_router_expert[i])
        br = jnp.zeros((ROUTER_ROWS, 1), jnp.float32)
        br = br.at[0:N_GROUPS, 0].set(b_router_group[i])
        br = br.at[EXPERTS_PER_GROUP:EXPERTS_PER_GROUP + N_EXPERTS, 0].set(b_router_expert[i])

        x1, h2, route_i, route_w = _mix(
            outs[0][0], outs[1][0], outs[2][0], outs[0][1], outs[1][1], outs[2][1], ob, xc,
            g_grp_a[i][None, :], g_grp_b[i][None, :], w_out[i].astype(bf), g_ffn[i][None, :], wr, br)

        texp, first, nused, pad_tile, pos, p_rows = _dispatch_tables(route_i, n_tok)
        xs = _dispatch(pos, pad_tile, nused, h2, p_rows)
        y = _moe(texp, first, nused, xs, w_expert_gate[i], w_expert_up[i], w_expert_down[i])
        xc = _final(pos, y, x1, route_w, p[i].reshape(n_tok, PLE_DIM), g_ple[i][None, :],
                    w_ple_gate[i].astype(bf), w_ple_proj[i].astype(bf), g_final[None, :])
    return xc.reshape(batch, seq, D_MODEL)
```

```python
import functools

import jax
import jax.numpy as jnp
from jax import lax
from jax.experimental import pallas as pl
from jax.experimental.pallas import tpu as pltpu

D_MODEL = 1024
HEAD_DIM = 64
GROUP_WIDTH = 512
N_HEADS = 8
B_KV_HEADS = 2
B_KV_WIDTH = B_KV_HEADS * HEAD_DIM
A_CONFIGS = ((128, 1), (512, 4), (2048, 16))
B_HALF_WINDOW = 128
N_GROUPS = 4
EXPERTS_PER_GROUP = 8
N_EXPERTS = N_GROUPS * EXPERTS_PER_GROUP
D_EXPERT = 512
PLE_DIM = 256
EPS = 1e-6
NEG = -1e30

Q_BLOCK = 128
HEADS_PER_PASS = 4
PASS_WIDTH = HEADS_PER_PASS * HEAD_DIM
ROW_TILE = 256
ROUTER_ROWS = 128
VMEM_LIMIT = 48 * 1024 * 1024


def _rms(x, g):
    r = lax.rsqrt(jnp.mean(x * x, axis=-1, keepdims=True) + EPS)
    return (x * r) * g


def _split3(x):
    hi = x.astype(jnp.bfloat16)
    r1 = x - hi.astype(jnp.float32)
    mid = r1.astype(jnp.bfloat16)
    lo = (r1 - mid.astype(jnp.float32)).astype(jnp.bfloat16)
    return hi, mid, lo


def _dot(a, b):
    return jnp.dot(a, b, preferred_element_type=jnp.float32)


def _proj_kernel(x_ref, g_ref, w_ref, qa_ref, ka_ref, va_ref, qb_ref, kb_ref, vb_ref):
    h = _rms(x_ref[...], g_ref[...]).astype(jnp.bfloat16)
    scale = HEAD_DIM ** -0.5
    W = GROUP_WIDTH
    qa_ref[...] = (_dot(h, w_ref[:, 0:W]) * scale).astype(jnp.bfloat16)
    ka_ref[...] = _dot(h, w_ref[:, W:2 * W]).astype(jnp.bfloat16)
    va_ref[...] = _dot(h, w_ref[:, 2 * W:3 * W]).astype(jnp.bfloat16)
    qb_ref[...] = (_dot(h, w_ref[:, 3 * W:4 * W]) * scale).astype(jnp.bfloat16)
    kb_ref[...] = _dot(h, w_ref[:, 4 * W:4 * W + B_KV_WIDTH]).astype(jnp.bfloat16)
    vb_ref[...] = _dot(h, w_ref[:, 4 * W + B_KV_WIDTH:4 * W + 2 * B_KV_WIDTH]).astype(jnp.bfloat16)


def _proj(x2d, g_mix, w_in_bf):
    t = x2d.shape[0]
    tm = 512
    in_width = w_in_bf.shape[1]
    wide = jax.ShapeDtypeStruct((t, GROUP_WIDTH), jnp.bfloat16)
    narrow = jax.ShapeDtypeStruct((t, B_KV_WIDTH), jnp.bfloat16)
    row = lambda w: pl.BlockSpec((tm, w), lambda i: (i, 0))
    return pl.pallas_call(
        _proj_kernel,
        out_shape=(wide, wide, wide, wide, narrow, narrow),
        grid=(t // tm,),
        in_specs=[row(D_MODEL),
                  pl.BlockSpec((1, D_MODEL), lambda i: (0, 0)),
                  pl.BlockSpec((D_MODEL, in_width), lambda i: (0, 0))],
        out_specs=(row(GROUP_WIDTH), row(GROUP_WIDTH), row(GROUP_WIDTH), row(GROUP_WIDTH),
                   row(B_KV_WIDTH), row(B_KV_WIDTH)),
        compiler_params=pltpu.CompilerParams(dimension_semantics=("arbitrary",),
                                             vmem_limit_bytes=VMEM_LIMIT),
        name="proj",
    )(x2d, g_mix, w_in_bf)


def _fill_bias(bias_ref, half, nkeys, dist_scale):
    i = lax.broadcasted_iota(jnp.int32, (Q_BLOCK, nkeys), 0)
    j = lax.broadcasted_iota(jnp.int32, (Q_BLOCK, nkeys), 1)
    for var, rel0 in enumerate((0, -half, Q_BLOCK - nkeys)):
        d = jnp.abs(j + rel0 - i)
        dist = d.astype(jnp.float32) * dist_scale
        for h in range(N_HEADS):
            slope = 2.0 ** (-(h + 1))
            bias_ref[var, h] = jnp.where(d <= half, -(slope * dist), NEG)


def _attend_block(q_ref, k_ref, v_ref, bias_ref, o_ref, lse_ref, sink_ref, row0, q0, seq_len,
                  half, nkeys):
    lo = q0 - half
    ks = pl.multiple_of(jnp.clip(lo, 0, seq_len - nkeys), HEAD_DIM)
    var = jnp.where(lo < 0, 0, jnp.where(lo > seq_len - nkeys, 2, 1))
    lane_head = lax.broadcasted_iota(jnp.int32, (Q_BLOCK, PASS_WIDTH), 1) // HEAD_DIM
    for g in range(N_HEADS // HEADS_PER_PASS):
        cols = slice(g * PASS_WIDTH, (g + 1) * PASS_WIDTH)
        q = q_ref[pl.ds(row0, Q_BLOCK), cols]
        zero = jnp.zeros_like(q)
        q_stack = jnp.concatenate(
            [jnp.where(lane_head == h, q, zero) for h in range(HEADS_PER_PASS)], axis=0)
        k = k_ref[pl.ds(ks, nkeys), cols]
        v = v_ref[pl.ds(ks, nkeys), cols]
        s = lax.dot_general(q_stack, k, (((1,), (1,)), ((), ())),
                            preferred_element_type=jnp.float32)
        s = s.reshape(HEADS_PER_PASS, Q_BLOCK, nkeys)
        s = s + bias_ref[var, pl.ds(g * HEADS_PER_PASS, HEADS_PER_PASS)]
        m = jnp.max(s, axis=-1, keepdims=True)
        if sink_ref is not None:
            sk = jnp.stack([jnp.full((Q_BLOCK, 1), sink_ref[g * HEADS_PER_PASS + h], jnp.float32)
                            for h in range(HEADS_PER_PASS)], axis=0)
            m = jnp.maximum(m, sk)
        p = jnp.exp(s - m)
        den = jnp.sum(p, axis=-1, keepdims=True)
        if sink_ref is not None:
            den = den + jnp.exp(sk - m)
        pv = _dot(p.reshape(HEADS_PER_PASS * Q_BLOCK, nkeys).astype(jnp.bfloat16), v)
        pv = pv.reshape(HEADS_PER_PASS, Q_BLOCK, PASS_WIDTH) * (1.0 / den)
        o = jnp.zeros((Q_BLOCK, PASS_WIDTH), jnp.float32)
        for h in range(HEADS_PER_PASS):
            o = jnp.where(lane_head == h, pv[h], o)
        o_ref[pl.ds(row0, Q_BLOCK), cols] = o
        if lse_ref is not None:
            lse = m + jnp.log(den)
            for h in range(HEADS_PER_PASS):
                c = g * HEADS_PER_PASS + h
                lse_ref[pl.ds(row0, Q_BLOCK), c:c + 1] = lse[h]


def _attn_a_kernel(q_ref, k_ref, v_ref, o_ref, lse_ref, bias_ref, *, half, nkeys, dist_scale,
                   seq_len, q_tile):
    first = (pl.program_id(0) == 0) & (pl.program_id(1) == 0) & (pl.program_id(2) == 0)

    @pl.when(first)
    def _():
        _fill_bias(bias_ref, half, nkeys, dist_scale)

    base = pl.program_id(2) * q_tile

    def body(jb, carry):
        row0 = pl.multiple_of(jb * Q_BLOCK, Q_BLOCK)
        _attend_block(q_ref, k_ref, v_ref, bias_ref, o_ref, lse_ref, None, row0, base + row0,
                      seq_len, half, nkeys)
        return carry

    lax.fori_loop(0, q_tile // Q_BLOCK, body, 0)


def _attn_a(q, k, v, batch, seq, dil, window):
    l = seq // dil
    half = window // (2 * dil)
    nkeys = Q_BLOCK + 2 * half
    q_tile = min(l, 1024)
    view = lambda a: a.reshape(batch, l, dil * GROUP_WIDTH)
    kern = functools.partial(_attn_a_kernel, half=half, nkeys=nkeys, dist_scale=float(dil),
                             seq_len=l, q_tile=q_tile)
    o, lse = pl.pallas_call(
        kern,
        out_shape=(jax.ShapeDtypeStruct((batch, l, dil * GROUP_WIDTH), jnp.float32),
                   jax.ShapeDtypeStruct((batch, dil, l, N_HEADS), jnp.float32)),
        grid=(batch, dil, l // q_tile),
        in_specs=[pl.BlockSpec((None, q_tile, GROUP_WIDTH), lambda b, r, i: (b, i, r)),
                  pl.BlockSpec((None, l, GROUP_WIDTH), lambda b, r, i: (b, 0, r)),
                  pl.BlockSpec((None, l, GROUP_WIDTH), lambda b, r, i: (b, 0, r))],
        out_specs=(pl.BlockSpec((None, q_tile, GROUP_WIDTH), lambda b, r, i: (b, i, r)),
                   pl.BlockSpec((None, None, q_tile, N_HEADS), lambda b, r, i: (b, r, i, 0))),
        scratch_shapes=[pltpu.VMEM((3, N_HEADS, Q_BLOCK, nkeys), jnp.float32)],
        compiler_params=pltpu.CompilerParams(
            dimension_semantics=("arbitrary", "arbitrary", "arbitrary"),
            vmem_limit_bytes=VMEM_LIMIT),
        name=f"attn_a_d{dil}",
    )(view(q), view(k), view(v))
    o = o.reshape(batch * seq, GROUP_WIDTH)
    lse = lse.transpose(0, 2, 1, 3).reshape(batch * seq, N_HEADS)
    return o, lse


def _attn_b_kernel(sink_ref, q_ref, k_ref, v_ref, o_ref, bias_ref, k4_ref, v4_ref, *, half, nkeys,
                   seq_len, q_tile):
    first = (pl.program_id(0) == 0) & (pl.program_id(1) == 0)

    @pl.when(first)
    def _():
        _fill_bias(bias_ref, half, nkeys, 1.0)

    @pl.when(pl.program_id(1) == 0)
    def _():
        c = lax.broadcasted_iota(jnp.int32, (B_KV_WIDTH, GROUP_WIDTH), 0)
        j = lax.broadcasted_iota(jnp.int32, (B_KV_WIDTH, GROUP_WIDTH), 1)
        src = (j // PASS_WIDTH) * HEAD_DIM + j % HEAD_DIM
        rep = jnp.where(c == src, 1.0, 0.0).astype(jnp.bfloat16)
        chunk = 512

        def body(i, carry):
            rows = pl.ds(pl.multiple_of(i * chunk, chunk), chunk)
            k4_ref[rows, :] = _dot(k_ref[rows, :], rep).astype(jnp.bfloat16)
            v4_ref[rows, :] = _dot(v_ref[rows, :], rep).astype(jnp.bfloat16)
            return carry

        lax.fori_loop(0, seq_len // chunk, body, 0)

    base = pl.program_id(1) * q_tile

    def body(jb, carry):
        row0 = pl.multiple_of(jb * Q_BLOCK, Q_BLOCK)
        _attend_block(q_ref, k4_ref, v4_ref, bias_ref, o_ref, None, sink_ref, row0, base + row0,
                      seq_len, half, nkeys)
        return carry

    lax.fori_loop(0, q_tile // Q_BLOCK, body, 0)


def _attn_b(q, k, v, sink, batch, seq):
    half = B_HALF_WINDOW
    nkeys = Q_BLOCK + 2 * half
    q_tile = 1024
    view = lambda a: a.reshape(batch, seq, a.shape[-1])
    kern = functools.partial(_attn_b_kernel, half=half, nkeys=nkeys, seq_len=seq, q_tile=q_tile)
    o = pl.pallas_call(
        kern,
        out_shape=jax.ShapeDtypeStruct((batch, seq, GROUP_WIDTH), jnp.float32),
        grid=(batch, seq // q_tile),
        in_specs=[pl.BlockSpec(memory_space=pltpu.SMEM),
                  pl.BlockSpec((None, q_tile, GROUP_WIDTH), lambda b, i: (b, i, 0)),
                  pl.BlockSpec((None, seq, B_KV_WIDTH), lambda b, i: (b, 0, 0)),
                  pl.BlockSpec((None, seq, B_KV_WIDTH), lambda b, i: (b, 0, 0))],
        out_specs=pl.BlockSpec((None, q_tile, GROUP_WIDTH), lambda b, i: (b, i, 0)),
        scratch_shapes=[pltpu.VMEM((3, N_HEADS, Q_BLOCK, nkeys), jnp.float32),
                        pltpu.VMEM((seq, GROUP_WIDTH), jnp.bfloat16),
                        pltpu.VMEM((seq, GROUP_WIDTH), jnp.bfloat16)],
        compiler_params=pltpu.CompilerParams(dimension_semantics=("arbitrary", "arbitrary"),
                                             vmem_limit_bytes=VMEM_LIMIT),
        name="attn_b",
    )(sink, view(q), view(k), view(v))
    return o.reshape(batch * seq, GROUP_WIDTH)


def _mix_kernel(o1_ref, o4_ref, o16_ref, l1_ref, l4_ref, l16_ref, ob_ref, x_ref, ga_ref, gb_ref,
                wout_ref, gffn_ref, wr_ref, br_ref, x1_ref, h2_ref, ri_ref, rc_ref):
    tm = x_ref.shape[0]
    l1, l4, l16 = l1_ref[...], l4_ref[...], l16_ref[...]
    mx = jnp.maximum(jnp.maximum(l1, l4), l16)
    e1, e4, e16 = jnp.exp(l1 - mx), jnp.exp(l4 - mx), jnp.exp(l16 - mx)
    inv = 1.0 / (e1 + e4 + e16)
    hh = lax.broadcasted_iota(jnp.int32, (N_HEADS, GROUP_WIDTH), 0)
    jj = lax.broadcasted_iota(jnp.int32, (N_HEADS, GROUP_WIDTH), 1)
    expand = jnp.where(jj // HEAD_DIM == hh, 1.0, 0.0).astype(jnp.bfloat16)

    def widen(w):
        hi, mid, lo = _split3(w)
        return _dot(hi, expand) + _dot(mid, expand) + _dot(lo, expand)

    oa = (widen(e1 * inv) * o1_ref[...] + widen(e4 * inv) * o4_ref[...]
          + widen(e16 * inv) * o16_ref[...])
    na = _rms(oa, ga_ref[...]).astype(jnp.bfloat16)
    nb = _rms(ob_ref[...], gb_ref[...]).astype(jnp.bfloat16)
    x1 = (x_ref[...] + _dot(na, wout_ref[0:GROUP_WIDTH, :])
          + _dot(nb, wout_ref[GROUP_WIDTH:2 * GROUP_WIDTH, :]))
    x1_ref[...] = x1
    h2 = _rms(x1, gffn_ref[...])
    h2_ref[...] = h2.astype(jnp.bfloat16)

    h_hi = h2.astype(jnp.bfloat16)
    h_lo = (h2 - h_hi.astype(jnp.float32)).astype(jnp.bfloat16)
    wr = wr_ref[...]
    w_hi = wr.astype(jnp.bfloat16)
    w_lo = (wr - w_hi.astype(jnp.float32)).astype(jnp.bfloat16)
    lg = _dot(h_hi, w_hi) + _dot(h_hi, w_lo) + _dot(h_lo, w_hi) + _dot(h_lo, w_lo)
    lgt = lg.T + br_ref[...]

    rows = lax.broadcasted_iota(jnp.int32, (EXPERTS_PER_GROUP, tm), 0)
    big = jnp.int32(1 << 20)
    glog = jnp.where(rows < N_GROUPS, lgt[0:EXPERTS_PER_GROUP], -jnp.inf)
    gmax = jnp.max(glog, axis=0, keepdims=True)
    gsel = jnp.min(jnp.where(glog == gmax, rows, big), axis=0, keepdims=True)
    gw = 1.0 / jnp.sum(jnp.exp(glog - gmax), axis=0, keepdims=True)
    esel = jnp.zeros((EXPERTS_PER_GROUP, tm), jnp.float32)
    for grp in range(N_GROUPS):
        lo = EXPERTS_PER_GROUP * (grp + 1)
        esel = jnp.where(gsel == grp, lgt[lo:lo + EXPERTS_PER_GROUP], esel)
    v0 = jnp.max(esel, axis=0, keepdims=True)
    i0 = jnp.min(jnp.where(esel == v0, rows, big), axis=0, keepdims=True)
    rest = jnp.where(rows == i0, -jnp.inf, esel)
    v1 = jnp.max(rest, axis=0, keepdims=True)
    i1 = jnp.min(jnp.where(rest == v1, rows, big), axis=0, keepdims=True)
    e = jnp.exp(v1 - v0)
    w0 = (1.0 / (1.0 + e)) * gw
    w1 = (e / (1.0 + e)) * gw
    eid0 = gsel * EXPERTS_PER_GROUP + i0
    eid1 = gsel * EXPERTS_PER_GROUP + i1
    ri_ref[...] = jnp.where(rows == 0, eid0, jnp.where(rows == 1, eid1, 0))
    rows_t = lax.broadcasted_iota(jnp.int32, (ROUTER_ROWS, tm), 0)
    slab = jnp.where(rows_t == 0, w0, jnp.where(rows_t == 1, w1, 0.0))
    slab = jnp.where(rows_t == 2, eid0.astype(jnp.float32),
                     jnp.where(rows_t == 3, eid1.astype(jnp.float32), slab))
    rc_ref[...] = slab.T[:, 0:EXPERTS_PER_GROUP]


def _mix(o1, o4, o16, l1, l4, l16, ob, x2d, g_a, g_b, w_out_bf, g_ffn, wr, br):
    t = x2d.shape[0]
    tm = 512
    row = lambda w: pl.BlockSpec((tm, w), lambda i: (i, 0))
    full = lambda a: pl.BlockSpec(a.shape, lambda i: (0, 0))
    return pl.pallas_call(
        _mix_kernel,
        out_shape=(jax.ShapeDtypeStruct((t, D_MODEL), jnp.float32),
                   jax.ShapeDtypeStruct((t, D_MODEL), jnp.bfloat16),
                   jax.ShapeDtypeStruct((EXPERTS_PER_GROUP, t), jnp.int32),
                   jax.ShapeDtypeStruct((t, EXPERTS_PER_GROUP), jnp.float32)),
        grid=(t // tm,),
        in_specs=[row(GROUP_WIDTH), row(GROUP_WIDTH), row(GROUP_WIDTH),
                  row(N_HEADS), row(N_HEADS), row(N_HEADS),
                  row(GROUP_WIDTH), row(D_MODEL),
                  full(g_a), full(g_b), full(w_out_bf), full(g_ffn), full(wr), full(br)],
        out_specs=(row(D_MODEL), row(D_MODEL),
                   pl.BlockSpec((EXPERTS_PER_GROUP, tm), lambda i: (0, i)),
                   row(EXPERTS_PER_GROUP)),
        compiler_params=pltpu.CompilerParams(dimension_semantics=("arbitrary",),
                                             vmem_limit_bytes=VMEM_LIMIT),
        name="mix_router",
    )(o1, o4, o16, l1, l4, l16, ob, x2d, g_a, g_b, w_out_bf, g_ffn, wr, br)


TOKEN_TILE = 512
RUN_ALIGN = 8
LOCAL_ROWS = 2 * TOKEN_TILE + N_EXPERTS * RUN_ALIGN
MAX_CHUNKS = LOCAL_ROWS // RUN_ALIGN


def _local_slots(member, lo_vec, prior, e_sel, axis):
    ids = lax.broadcasted_iota(jnp.int32, member.shape, axis)
    slot = jnp.where(ids == e_sel, lo_vec + prior, 0.0)
    return jnp.sum(slot, axis=axis, keepdims=True).astype(jnp.int32)


def _chunk_copy(dst_ref, tile, c, local, glob, sem, to_global):
    g = pl.multiple_of(dst_ref[tile * MAX_CHUNKS + c], RUN_ALIGN)
    l = pl.multiple_of(c * RUN_ALIGN, RUN_ALIGN)
    src, dst = local.at[pl.ds(l, RUN_ALIGN), :], glob.at[pl.ds(g, RUN_ALIGN), :]
    if not to_global:
        src, dst = dst, src
    return pltpu.make_async_copy(src, dst, sem)


def _start_chunks(nchunk_ref, dst_ref, tile, local, glob, sem, to_global):
    def body(c, carry):
        _chunk_copy(dst_ref, tile, c, local, glob, sem, to_global).start()
        return carry
    lax.fori_loop(0, nchunk_ref[tile], body, 0)


def _wait_chunks(nchunk_ref, dst_ref, tile, local, glob, sem, to_global):
    def body(c, carry):
        _chunk_copy(dst_ref, tile, c, local, glob, sem, to_global).wait()
        return carry
    lax.fori_loop(0, nchunk_ref[tile], body, 0)


def _lo_vector(lo_ref, tile, shape, axis):
    ids = lax.broadcasted_iota(jnp.int32, shape, axis)
    vec = jnp.zeros(shape, jnp.float32)
    for e in range(N_EXPERTS):
        vec = jnp.where(ids == e, lo_ref[tile * N_EXPERTS + e].astype(jnp.float32), vec)
    return vec


def _dispatch_kernel(lo_ref, nchunk_ref, dst_ref, pad_tile_ref, nused_ref, ri_ref, h_ref, xs_hbm,
                     upper, sbuf, zbuf, sem, zsem, *, n_tiles):
    i = pl.program_id(0)
    n_steps = pl.num_programs(0)
    slot = i % 2

    def zero_tile(tile):
        rows = pl.ds(pl.multiple_of(tile * ROW_TILE, ROW_TILE), ROW_TILE)
        return pltpu.make_async_copy(zbuf, xs_hbm.at[rows, :], zsem)

    @pl.when(i == 0)
    def _():
        r = lax.broadcasted_iota(jnp.int32, (TOKEN_TILE, TOKEN_TILE), 0)
        c = lax.broadcasted_iota(jnp.int32, (TOKEN_TILE, TOKEN_TILE), 1)
        upper[...] = jnp.where(r < c, 1.0, 0.0).astype(jnp.bfloat16)
        zbuf[...] = jnp.zeros_like(zbuf)
        for e in range(N_EXPERTS):
            @pl.when(pad_tile_ref[e] >= 0)
            def _():
                zero_tile(pad_tile_ref[e]).start()
        lax.fori_loop(nused_ref[0], n_tiles, lambda t, c: (zero_tile(t).start(), c)[1], 0)
        for e in range(N_EXPERTS):
            @pl.when(pad_tile_ref[e] >= 0)
            def _():
                zero_tile(pad_tile_ref[e]).wait()
        lax.fori_loop(nused_ref[0], n_tiles, lambda t, c: (zero_tile(t).wait(), c)[1], 0)

    e0 = ri_ref[0:1, :]
    e1 = ri_ref[1:2, :]
    ids = lax.broadcasted_iota(jnp.int32, (N_EXPERTS, TOKEN_TILE), 0)
    member = jnp.where((ids == e0) | (ids == e1), 1.0, 0.0)
    prior = _dot(member.astype(jnp.bfloat16), upper[...])
    lo_vec = _lo_vector(lo_ref, i, (N_EXPERTS, 1), 0)
    d0 = _local_slots(member, lo_vec, prior, e0, 0)
    d1 = _local_slots(member, lo_vec, prior, e1, 0)
    rows = lax.broadcasted_iota(jnp.int32, (LOCAL_ROWS, TOKEN_TILE), 0)
    select = jnp.where((rows == d0) | (rows == d1), 1.0, 0.0).astype(jnp.bfloat16)
    sorted_rows = _dot(select, h_ref[...])

    for s in range(2):
        @pl.when(slot == s)
        def _():
            sbuf[s] = sorted_rows

            @pl.when(i > 0)
            def _():
                _wait_chunks(nchunk_ref, dst_ref, i - 1, sbuf.at[1 - s], xs_hbm, sem, True)

            _start_chunks(nchunk_ref, dst_ref, i, sbuf.at[s], xs_hbm, sem, True)

            @pl.when(i == n_steps - 1)
            def _():
                _wait_chunks(nchunk_ref, dst_ref, i, sbuf.at[s], xs_hbm, sem, True)


def _dispatch(lo, nchunk, dst, pad_tile, nused, route_i, h2, p_rows):
    n_tok = h2.shape[0]
    grid_spec = pltpu.PrefetchScalarGridSpec(
        num_scalar_prefetch=5,
        grid=(n_tok // TOKEN_TILE,),
        in_specs=[pl.BlockSpec((EXPERTS_PER_GROUP, TOKEN_TILE), lambda i, *_: (0, i)),
                  pl.BlockSpec((TOKEN_TILE, D_MODEL), lambda i, *_: (i, 0))],
        out_specs=pl.BlockSpec(memory_space=pl.ANY),
        scratch_shapes=[pltpu.VMEM((TOKEN_TILE, TOKEN_TILE), jnp.bfloat16),
                        pltpu.VMEM((2, LOCAL_ROWS, D_MODEL), jnp.float32),
                        pltpu.VMEM((ROW_TILE, D_MODEL), jnp.float32),
                        pltpu.SemaphoreType.DMA, pltpu.SemaphoreType.DMA],
    )
    return pl.pallas_call(
        functools.partial(_dispatch_kernel, n_tiles=p_rows // ROW_TILE),
        out_shape=jax.ShapeDtypeStruct((p_rows, D_MODEL), jnp.float32),
        grid_spec=grid_spec,
        compiler_params=pltpu.CompilerParams(dimension_semantics=("arbitrary",),
                                             vmem_limit_bytes=VMEM_LIMIT),
        name="dispatch",
    )(lo, nchunk, dst, pad_tile, nused, route_i, h2)


def _moe_kernel(texp_ref, first_ref, nused_ref, xs_ref, wg_ref, wu_ref, wd_ref, y_ref,
                wg_bf, wu_bf, wd_bf):
    j = pl.program_id(0)

    @pl.when(j < nused_ref[0])
    def _():
        @pl.when(first_ref[j] == 1)
        def _():
            wg_bf[...] = wg_ref[...].astype(jnp.bfloat16)
            wu_bf[...] = wu_ref[...].astype(jnp.bfloat16)
            wd_bf[...] = wd_ref[...].astype(jnp.bfloat16)

        x = xs_ref[...].astype(jnp.bfloat16)
        a = _dot(x, wg_bf[...])
        u = _dot(x, wu_bf[...])
        hid = (a * (1.0 / (1.0 + jnp.exp(-a)))) * u
        y_ref[...] = _dot(hid.astype(jnp.bfloat16), wd_bf[...])

    @pl.when(j >= nused_ref[0])
    def _():
        y_ref[...] = jnp.zeros_like(y_ref)


def _moe(texp, first, nused, xs, w_gate, w_up, w_down):
    p_rows = xs.shape[0]
    n_tiles = p_rows // ROW_TILE
    used = lambda j, nu: jnp.minimum(j, nu[0] - 1)
    grid_spec = pltpu.PrefetchScalarGridSpec(
        num_scalar_prefetch=3,
        grid=(n_tiles,),
        in_specs=[
            pl.BlockSpec((ROW_TILE, D_MODEL), lambda j, te, fi, nu: (used(j, nu), 0)),
            pl.BlockSpec((None, D_MODEL, D_EXPERT), lambda j, te, fi, nu: (te[j], 0, 0)),
            pl.BlockSpec((None, D_MODEL, D_EXPERT), lambda j, te, fi, nu: (te[j], 0, 0)),
            pl.BlockSpec((None, D_EXPERT, D_MODEL), lambda j, te, fi, nu: (te[j], 0, 0)),
        ],
        out_specs=pl.BlockSpec((ROW_TILE, D_MODEL), lambda j, te, fi, nu: (j, 0)),
        scratch_shapes=[pltpu.VMEM((D_MODEL, D_EXPERT), jnp.bfloat16),
                        pltpu.VMEM((D_MODEL, D_EXPERT), jnp.bfloat16),
                        pltpu.VMEM((D_EXPERT, D_MODEL), jnp.bfloat16)],
    )
    return pl.pallas_call(
        _moe_kernel,
        out_shape=jax.ShapeDtypeStruct((p_rows, D_MODEL), jnp.float32),
        grid_spec=grid_spec,
        compiler_params=pltpu.CompilerParams(dimension_semantics=("arbitrary",),
                                             vmem_limit_bytes=VMEM_LIMIT),
        name="moe_experts",
    )(texp, first, nused, xs, w_gate, w_up, w_down)


def _final_kernel(lo_ref, nchunk_ref, dst_ref, y_hbm, x1_ref, rc_ref, p_ref, gple_ref, wg_ref, wp_ref,
                  gfin_ref, out_ref, lower, ybuf, sems):
    i = pl.program_id(0)
    n_steps = pl.num_programs(0)
    slot = i % 2

    @pl.when(i == 0)
    def _():
        r = lax.broadcasted_iota(jnp.int32, (TOKEN_TILE, TOKEN_TILE), 0)
        c = lax.broadcasted_iota(jnp.int32, (TOKEN_TILE, TOKEN_TILE), 1)
        lower[...] = jnp.where(c < r, 1.0, 0.0).astype(jnp.bfloat16)
        ybuf[...] = jnp.zeros_like(ybuf)
        _start_chunks(nchunk_ref, dst_ref, 0, ybuf.at[0], y_hbm, sems.at[0], False)

    for s in range(2):
        @pl.when((slot == s) & (i + 1 < n_steps))
        def _():
            _start_chunks(nchunk_ref, dst_ref, i + 1, ybuf.at[1 - s], y_hbm, sems.at[1 - s], False)

    pp = _dot(p_ref[...].astype(jnp.bfloat16), wp_ref[...])
    rc = rc_ref[...]
    w0, w1 = rc[:, 0:1], rc[:, 1:2]
    e0, e1 = rc[:, 2:3].astype(jnp.int32), rc[:, 3:4].astype(jnp.int32)
    ids = lax.broadcasted_iota(jnp.int32, (TOKEN_TILE, ROUTER_ROWS), 1)
    member = jnp.where((ids == e0) | (ids == e1), 1.0, 0.0)
    prior = _dot(lower[...], member.astype(jnp.bfloat16))
    lo_vec = _lo_vector(lo_ref, i, (1, ROUTER_ROWS), 1)
    d0 = _local_slots(member, lo_vec, prior, e0, 1)
    d1 = _local_slots(member, lo_vec, prior, e1, 1)
    cols = lax.broadcasted_iota(jnp.int32, (TOKEN_TILE, LOCAL_ROWS), 1)
    pick0 = jnp.where(cols == d0, 1.0, 0.0).astype(jnp.bfloat16)
    pick1 = jnp.where(cols == d1, 1.0, 0.0).astype(jnp.bfloat16)

    for s in range(2):
        @pl.when(slot == s)
        def _():
            _wait_chunks(nchunk_ref, dst_ref, i, ybuf.at[s], y_hbm, sems.at[s], False)
            yb = ybuf[s].astype(jnp.bfloat16)
            x2 = x1_ref[...] + (w0 * _dot(pick0, yb) + w1 * _dot(pick1, yb))
            z = _dot(_rms(x2, gple_ref[...]).astype(jnp.bfloat16), wg_ref[...])
            gate = 1.0 / (1.0 + jnp.exp(-z))
            x3 = x2 + pp * gate
            out_ref[...] = _rms(x3, gfin_ref[...])


def _final(lo, nchunk, dst, y, x1, rc, p2d, g_ple, w_gate_bf, w_proj_bf, g_final):
    t = x1.shape[0]
    tm = TOKEN_TILE
    row = lambda w: pl.BlockSpec((tm, w), lambda i, *_: (i, 0))
    full = lambda a: pl.BlockSpec(a.shape, lambda i, *_: (0, 0))
    grid_spec = pltpu.PrefetchScalarGridSpec(
        num_scalar_prefetch=3,
        grid=(t // tm,),
        in_specs=[pl.BlockSpec(memory_space=pl.ANY), row(D_MODEL), row(EXPERTS_PER_GROUP),
                  row(PLE_DIM), full(g_ple), full(w_gate_bf), full(w_proj_bf), full(g_final)],
        out_specs=row(D_MODEL),
        scratch_shapes=[pltpu.VMEM((TOKEN_TILE, TOKEN_TILE), jnp.bfloat16),
                        pltpu.VMEM((2, LOCAL_ROWS, D_MODEL), jnp.float32),
                        pltpu.SemaphoreType.DMA((2,))],
    )
    return pl.pallas_call(
        _final_kernel,
        out_shape=jax.ShapeDtypeStruct((t, D_MODEL), jnp.float32),
        grid_spec=grid_spec,
        compiler_params=pltpu.CompilerParams(dimension_semantics=("arbitrary",),
                                             vmem_limit_bytes=VMEM_LIMIT),
        name="combine_ple_final",
    )(lo, nchunk, dst, y, x1, rc, p2d, g_ple, w_gate_bf, w_proj_bf, g_final)


def _dispatch_tables(route_i, n_tok):
    n_tt = n_tok // TOKEN_TILE
    p_rows = 2 * n_tok + n_tt * N_EXPERTS * (RUN_ALIGN - 1) + N_EXPERTS * ROW_TILE
    p_rows = -(-p_rows // ROW_TILE) * ROW_TILE
    n_tiles = p_rows // ROW_TILE
    e = route_i[:2].reshape(2, n_tt, TOKEN_TILE)
    onehot = (e[..., None] == jnp.arange(N_EXPERTS, dtype=jnp.int32)).astype(jnp.int32)
    counts = jnp.sum(onehot, axis=(0, 2))
    n8 = (counts + RUN_ALIGN - 1) // RUN_ALIGN * RUN_ALIGN
    lo = jnp.cumsum(n8, axis=1) - n8
    total = jnp.sum(n8, axis=0)
    tiles_per = (total + ROW_TILE - 1) // ROW_TILE
    tile_end = jnp.cumsum(tiles_per)
    tile_start = tile_end - tiles_per
    g = tile_start[None, :] * ROW_TILE + (jnp.cumsum(n8, axis=0) - n8)
    n_used = tile_end[-1]
    tile_ids = jnp.arange(n_tiles, dtype=jnp.int32)
    clamped = jnp.minimum(tile_ids, n_used - 1)
    texp = jnp.sum((tile_end[None, :] <= clamped[:, None]).astype(jnp.int32), axis=1)
    first = ((tile_ids == tile_start[texp]) & (tile_ids < n_used)).astype(jnp.int32)
    pad_tile = jnp.where(total > 0, tile_end - 1, -1).astype(jnp.int32)
    chunk_row = jnp.arange(MAX_CHUNKS, dtype=jnp.int32)[None, :, None] * RUN_ALIGN
    in_run = (chunk_row >= lo[:, None, :]) & (chunk_row < (lo + n8)[:, None, :])
    dst = jnp.sum(jnp.where(in_run, g[:, None, :] + chunk_row - lo[:, None, :], 0), axis=2)
    nchunk = jnp.sum(n8, axis=1) // RUN_ALIGN
    i32 = lambda a: a.reshape(-1).astype(jnp.int32)
    return (texp, first, n_used.reshape(1).astype(jnp.int32), pad_tile, i32(lo), i32(nchunk),
            i32(dst), p_rows)


def kernel(x, p, g_mix, w_in, sink, g_grp_a, g_grp_b, w_out, g_ffn, w_router_group, b_router_group,
           w_router_expert, b_router_expert, w_expert_gate, w_expert_up, w_expert_down, g_ple,
           w_ple_gate, w_ple_proj, g_final):
    batch, seq, _ = x.shape
    n_tok = batch * seq
    depth = w_in.shape[0]
    assert depth == 1, "the final RMSNorm is fused into the last layer's epilogue"
    bf = jnp.bfloat16
    xc = x.reshape(n_tok, D_MODEL)
    for i in range(depth):
        qa, ka, va, qb, kb, vb = _proj(xc, g_mix[i][None, :], w_in[i].astype(bf))
        outs = [_attn_a(qa, ka, va, batch, seq, dil, window) for window, dil in A_CONFIGS]
        ob = _attn_b(qb, kb, vb, sink[i], batch, seq)

        wr = jnp.zeros((D_MODEL, ROUTER_ROWS), jnp.float32)
        wr = wr.at[:, 0:N_GROUPS].set(w_router_group[i])
        wr = wr.at[:, EXPERTS_PER_GROUP:EXPERTS_PER_GROUP + N_EXPERTS].set(w_router_expert[i])
        br = jnp.zeros((ROUTER_ROWS, 1), jnp.float32)
        br = br.at[0:N_GROUPS, 0].set(b_router_group[i])
        br = br.at[EXPERTS_PER_GROUP:EXPERTS_PER_GROUP + N_EXPERTS, 0].set(b_router_expert[i])

        x1, h2, route_i, route_c = _mix(
            outs[0][0], outs[1][0], outs[2][0], outs[0][1], outs[1][1], outs[2][1], ob, xc,
            g_grp_a[i][None, :], g_grp_b[i][None, :], w_out[i].astype(bf), g_ffn[i][None, :], wr, br)

        texp, first, nused, pad_tile, lo, nchunk, dst, p_rows = _dispatch_tables(route_i, n_tok)
        xs = _dispatch(lo, nchunk, dst, pad_tile, nused, route_i, h2, p_rows)
        y = _moe(texp, first, nused, xs, w_expert_gate[i], w_expert_up[i], w_expert_down[i])
        xc = _final(lo, nchunk, dst, y, x1, route_c, p[i].reshape(n_tok, PLE_DIM), g_ple[i][None, :],
                    w_ple_gate[i].astype(bf), w_ple_proj[i].astype(bf), g_final[None, :])
    return xc.reshape(batch, seq, D_MODEL)
```

```python
import functools

import jax
import jax.numpy as jnp
from jax import lax
from jax.experimental import pallas as pl
from jax.experimental.pallas import tpu as pltpu

D_MODEL = 1024
HEAD_DIM = 64
GROUP_WIDTH = 512
N_HEADS = 8
B_KV_HEADS = 2
B_KV_WIDTH = B_KV_HEADS * HEAD_DIM
A_CONFIGS = ((128, 1), (512, 4), (2048, 16))
B_HALF_WINDOW = 128
N_GROUPS = 4
EXPERTS_PER_GROUP = 8
N_EXPERTS = N_GROUPS * EXPERTS_PER_GROUP
D_EXPERT = 512
PLE_DIM = 256
EPS = 1e-6
NEG = -1e30

Q_BLOCK = 128
HEADS_PER_PASS = 2
PASS_WIDTH = HEADS_PER_PASS * HEAD_DIM
ROW_TILE = 256
ROUTER_ROWS = 128
VMEM_LIMIT = 48 * 1024 * 1024


def _rms(x, g):
    r = lax.rsqrt(jnp.mean(x * x, axis=-1, keepdims=True) + EPS)
    return (x * r) * g


def _split3(x):
    hi = x.astype(jnp.bfloat16)
    r1 = x - hi.astype(jnp.float32)
    mid = r1.astype(jnp.bfloat16)
    lo = (r1 - mid.astype(jnp.float32)).astype(jnp.bfloat16)
    return hi, mid, lo


def _dot(a, b):
    return jnp.dot(a, b, preferred_element_type=jnp.float32)


def _proj_kernel(x_ref, g_ref, w_ref, qa_ref, ka_ref, va_ref, qb_ref, kb_ref, vb_ref):
    h = _rms(x_ref[...], g_ref[...]).astype(jnp.bfloat16)
    scale = HEAD_DIM ** -0.5
    W = GROUP_WIDTH
    qa_ref[...] = (_dot(h, w_ref[:, 0:W]) * scale).astype(jnp.bfloat16)
    ka_ref[...] = _dot(h, w_ref[:, W:2 * W]).astype(jnp.bfloat16)
    va_ref[...] = _dot(h, w_ref[:, 2 * W:3 * W]).astype(jnp.bfloat16)
    qb_ref[...] = (_dot(h, w_ref[:, 3 * W:4 * W]) * scale).astype(jnp.bfloat16)
    kb_ref[...] = _dot(h, w_ref[:, 4 * W:4 * W + B_KV_WIDTH]).astype(jnp.bfloat16)
    vb_ref[...] = _dot(h, w_ref[:, 4 * W + B_KV_WIDTH:4 * W + 2 * B_KV_WIDTH]).astype(jnp.bfloat16)


def _proj(x2d, g_mix, w_in_bf):
    t = x2d.shape[0]
    tm = 512
    in_width = w_in_bf.shape[1]
    wide = jax.ShapeDtypeStruct((t, GROUP_WIDTH), jnp.bfloat16)
    narrow = jax.ShapeDtypeStruct((t, B_KV_WIDTH), jnp.bfloat16)
    row = lambda w: pl.BlockSpec((tm, w), lambda i: (i, 0))
    return pl.pallas_call(
        _proj_kernel,
        out_shape=(wide, wide, wide, wide, narrow, narrow),
        grid=(t // tm,),
        in_specs=[row(D_MODEL),
                  pl.BlockSpec((1, D_MODEL), lambda i: (0, 0)),
                  pl.BlockSpec((D_MODEL, in_width), lambda i: (0, 0))],
        out_specs=(row(GROUP_WIDTH), row(GROUP_WIDTH), row(GROUP_WIDTH), row(GROUP_WIDTH),
                   row(B_KV_WIDTH), row(B_KV_WIDTH)),
        compiler_params=pltpu.CompilerParams(dimension_semantics=("arbitrary",),
                                             vmem_limit_bytes=VMEM_LIMIT),
        name="proj",
    )(x2d, g_mix, w_in_bf)


def _fill_bias(bias_ref, half, nkeys, dist_scale):
    i = lax.broadcasted_iota(jnp.int32, (Q_BLOCK, nkeys), 0)
    j = lax.broadcasted_iota(jnp.int32, (Q_BLOCK, nkeys), 1)
    for var, rel0 in enumerate((0, -half, Q_BLOCK - nkeys)):
        d = jnp.abs(j + rel0 - i)
        dist = d.astype(jnp.float32) * dist_scale
        for h in range(N_HEADS):
            slope = 2.0 ** (-(h + 1))
            bias_ref[var, h] = jnp.where(d <= half, -(slope * dist), NEG)


LANES = 128


def _fold_lane_tiles(a, op):
    out = a[..., 0:LANES]
    for t in range(1, a.shape[-1] // LANES):
        out = op(out, a[..., t * LANES:(t + 1) * LANES])
    return out


def _attend_block(q_ref, k_ref, v_ref, bias_ref, o_ref, lse_ref, sink_ref, row0, q0, seq_len,
                  half, nkeys):
    lo = q0 - half
    ks = pl.multiple_of(jnp.clip(lo, 0, seq_len - nkeys), HEAD_DIM)
    var = jnp.where(lo < 0, 0, jnp.where(lo > seq_len - nkeys, 2, 1))
    lane_head = lax.broadcasted_iota(jnp.int32, (Q_BLOCK, PASS_WIDTH), 1) // HEAD_DIM
    for g in range(N_HEADS // HEADS_PER_PASS):
        cols = slice(g * PASS_WIDTH, (g + 1) * PASS_WIDTH)
        q = q_ref[pl.ds(row0, Q_BLOCK), cols]
        zero = jnp.zeros_like(q)
        q_stack = jnp.concatenate(
            [jnp.where(lane_head == h, q, zero) for h in range(HEADS_PER_PASS)], axis=0)
        k = k_ref[pl.ds(ks, nkeys), cols]
        v = v_ref[pl.ds(ks, nkeys), cols]
        s = lax.dot_general(q_stack, k, (((1,), (1,)), ((), ())),
                            preferred_element_type=jnp.float32)
        s = s.reshape(HEADS_PER_PASS, Q_BLOCK, nkeys)
        s = s + bias_ref[var, pl.ds(g * HEADS_PER_PASS, HEADS_PER_PASS)]
        m_tile = _fold_lane_tiles(s, jnp.maximum)
        if sink_ref is not None:
            tile_head = lax.broadcasted_iota(jnp.int32, m_tile.shape, 0)
            tile_lane = lax.broadcasted_iota(jnp.int32, m_tile.shape, 2)
            sk = jnp.zeros(m_tile.shape, jnp.float32)
            for h in range(HEADS_PER_PASS):
                sk = jnp.where(tile_head == h, sink_ref[g * HEADS_PER_PASS + h], sk)
            m_tile = jnp.maximum(m_tile, sk)
        m = jnp.max(m_tile, axis=-1, keepdims=True)
        p = jnp.exp(s - m)
        den_tile = _fold_lane_tiles(p, jnp.add)
        if sink_ref is not None:
            den_tile = den_tile + jnp.where(tile_lane == 0, jnp.exp(sk - m), 0.0)
        den = jnp.sum(den_tile, axis=-1, keepdims=True)
        pv = _dot(p.reshape(HEADS_PER_PASS * Q_BLOCK, nkeys).astype(jnp.bfloat16), v)
        pv = pv.reshape(HEADS_PER_PASS, Q_BLOCK, PASS_WIDTH) * (1.0 / den)
        o = jnp.zeros((Q_BLOCK, PASS_WIDTH), jnp.float32)
        for h in range(HEADS_PER_PASS):
            o = jnp.where(lane_head == h, pv[h], o)
        o_ref[pl.ds(row0, Q_BLOCK), cols] = o
        if lse_ref is not None:
            lse = m + jnp.log(den)
            for h in range(HEADS_PER_PASS):
                c = g * HEADS_PER_PASS + h
                lse_ref[pl.ds(row0, Q_BLOCK), c:c + 1] = lse[h]


def _attn_a_kernel(q_ref, k_ref, v_ref, o_ref, lse_ref, bias_ref, *, half, nkeys, dist_scale,
                   seq_len, q_tile):
    first = (pl.program_id(0) == 0) & (pl.program_id(1) == 0) & (pl.program_id(2) == 0)

    @pl.when(first)
    def _():
        _fill_bias(bias_ref, half, nkeys, dist_scale)

    base = pl.program_id(2) * q_tile

    def body(jb, carry):
        row0 = pl.multiple_of(jb * Q_BLOCK, Q_BLOCK)
        _attend_block(q_ref, k_ref, v_ref, bias_ref, o_ref, lse_ref, None, row0, base + row0,
                      seq_len, half, nkeys)
        return carry

    lax.fori_loop(0, q_tile // Q_BLOCK, body, 0, unroll=2)


def _attn_a(q, k, v, batch, seq, dil, window):
    l = seq // dil
    half = window // (2 * dil)
    nkeys = Q_BLOCK + 2 * half
    q_tile = min(l, 1024)
    view = lambda a: a.reshape(batch, l, dil * GROUP_WIDTH)
    kern = functools.partial(_attn_a_kernel, half=half, nkeys=nkeys, dist_scale=float(dil),
                             seq_len=l, q_tile=q_tile)
    o, lse = pl.pallas_call(
        kern,
        out_shape=(jax.ShapeDtypeStruct((batch, l, dil * GROUP_WIDTH), jnp.float32),
                   jax.ShapeDtypeStruct((batch, dil, l, N_HEADS), jnp.float32)),
        grid=(batch, dil, l // q_tile),
        in_specs=[pl.BlockSpec((None, q_tile, GROUP_WIDTH), lambda b, r, i: (b, i, r)),
                  pl.BlockSpec((None, l, GROUP_WIDTH), lambda b, r, i: (b, 0, r)),
                  pl.BlockSpec((None, l, GROUP_WIDTH), lambda b, r, i: (b, 0, r))],
        out_specs=(pl.BlockSpec((None, q_tile, GROUP_WIDTH), lambda b, r, i: (b, i, r)),
                   pl.BlockSpec((None, None, q_tile, N_HEADS), lambda b, r, i: (b, r, i, 0))),
        scratch_shapes=[pltpu.VMEM((3, N_HEADS, Q_BLOCK, nkeys), jnp.float32)],
        compiler_params=pltpu.CompilerParams(
            dimension_semantics=("arbitrary", "arbitrary", "arbitrary"),
            vmem_limit_bytes=VMEM_LIMIT),
        name=f"attn_a_d{dil}",
    )(view(q), view(k), view(v))
    o = o.reshape(batch * seq, GROUP_WIDTH)
    lse = lse.transpose(0, 2, 1, 3).reshape(batch * seq, N_HEADS)
    return o, lse


def _attn_b_kernel(sink_ref, q_ref, k_ref, v_ref, o_ref, bias_ref, k4_ref, v4_ref, *, half, nkeys,
                   seq_len, q_tile):
    first = (pl.program_id(0) == 0) & (pl.program_id(1) == 0)

    @pl.when(first)
    def _():
        _fill_bias(bias_ref, half, nkeys, 1.0)

    @pl.when(pl.program_id(1) == 0)
    def _():
        c = lax.broadcasted_iota(jnp.int32, (B_KV_WIDTH, GROUP_WIDTH), 0)
        j = lax.broadcasted_iota(jnp.int32, (B_KV_WIDTH, GROUP_WIDTH), 1)
        src = (j // (GROUP_WIDTH // B_KV_HEADS)) * HEAD_DIM + j % HEAD_DIM
        rep = jnp.where(c == src, 1.0, 0.0).astype(jnp.bfloat16)
        chunk = 512

        def body(i, carry):
            rows = pl.ds(pl.multiple_of(i * chunk, chunk), chunk)
            k4_ref[rows, :] = _dot(k_ref[rows, :], rep).astype(jnp.bfloat16)
            v4_ref[rows, :] = _dot(v_ref[rows, :], rep).astype(jnp.bfloat16)
            return carry

        lax.fori_loop(0, seq_len // chunk, body, 0)

    base = pl.program_id(1) * q_tile

    def body(jb, carry):
        row0 = pl.multiple_of(jb * Q_BLOCK, Q_BLOCK)
        _attend_block(q_ref, k4_ref, v4_ref, bias_ref, o_ref, None, sink_ref, row0, base + row0,
                      seq_len, half, nkeys)
        return carry

    lax.fori_loop(0, q_tile // Q_BLOCK, body, 0, unroll=2)


def _attn_b(q, k, v, sink, batch, seq):
    half = B_HALF_WINDOW
    nkeys = Q_BLOCK + 2 * half
    q_tile = 1024
    view = lambda a: a.reshape(batch, seq, a.shape[-1])
    kern = functools.partial(_attn_b_kernel, half=half, nkeys=nkeys, seq_len=seq, q_tile=q_tile)
    o = pl.pallas_call(
        kern,
        out_shape=jax.ShapeDtypeStruct((batch, seq, GROUP_WIDTH), jnp.float32),
        grid=(batch, seq // q_tile),
        in_specs=[pl.BlockSpec(memory_space=pltpu.SMEM),
                  pl.BlockSpec((None, q_tile, GROUP_WIDTH), lambda b, i: (b, i, 0)),
                  pl.BlockSpec((None, seq, B_KV_WIDTH), lambda b, i: (b, 0, 0)),
                  pl.BlockSpec((None, seq, B_KV_WIDTH), lambda b, i: (b, 0, 0))],
        out_specs=pl.BlockSpec((None, q_tile, GROUP_WIDTH), lambda b, i: (b, i, 0)),
        scratch_shapes=[pltpu.VMEM((3, N_HEADS, Q_BLOCK, nkeys), jnp.float32),
                        pltpu.VMEM((seq, GROUP_WIDTH), jnp.bfloat16),
                        pltpu.VMEM((seq, GROUP_WIDTH), jnp.bfloat16)],
        compiler_params=pltpu.CompilerParams(dimension_semantics=("arbitrary", "arbitrary"),
                                             vmem_limit_bytes=VMEM_LIMIT),
        name="attn_b",
    )(sink, view(q), view(k), view(v))
    return o.reshape(batch * seq, GROUP_WIDTH)


def _mix_kernel(o1_ref, o4_ref, o16_ref, l1_ref, l4_ref, l16_ref, ob_ref, x_ref, ga_ref, gb_ref,
                wout_ref, gffn_ref, wr_ref, br_ref, x1_ref, h2_ref, ri_ref, rc_ref):
    tm = x_ref.shape[0]
    l1, l4, l16 = l1_ref[...], l4_ref[...], l16_ref[...]
    mx = jnp.maximum(jnp.maximum(l1, l4), l16)
    e1, e4, e16 = jnp.exp(l1 - mx), jnp.exp(l4 - mx), jnp.exp(l16 - mx)
    inv = 1.0 / (e1 + e4 + e16)
    hh = lax.broadcasted_iota(jnp.int32, (N_HEADS, GROUP_WIDTH), 0)
    jj = lax.broadcasted_iota(jnp.int32, (N_HEADS, GROUP_WIDTH), 1)
    expand = jnp.where(jj // HEAD_DIM == hh, 1.0, 0.0).astype(jnp.bfloat16)

    def widen(w):
        hi, mid, lo = _split3(w)
        return _dot(hi, expand) + _dot(mid, expand) + _dot(lo, expand)

    oa = (widen(e1 * inv) * o1_ref[...] + widen(e4 * inv) * o4_ref[...]
          + widen(e16 * inv) * o16_ref[...])
    na = _rms(oa, ga_ref[...]).astype(jnp.bfloat16)
    nb = _rms(ob_ref[...], gb_ref[...]).astype(jnp.bfloat16)
    x1 = (x_ref[...] + _dot(na, wout_ref[0:GROUP_WIDTH, :])
          + _dot(nb, wout_ref[GROUP_WIDTH:2 * GROUP_WIDTH, :]))
    x1_ref[...] = x1
    h2 = _rms(x1, gffn_ref[...])
    h2_ref[...] = h2.astype(jnp.bfloat16)

    h_hi = h2.astype(jnp.bfloat16)
    h_lo = (h2 - h_hi.astype(jnp.float32)).astype(jnp.bfloat16)
    wr = wr_ref[...]
    w_hi = wr.astype(jnp.bfloat16)
    w_lo = (wr - w_hi.astype(jnp.float32)).astype(jnp.bfloat16)
    lg = _dot(h_hi, w_hi) + _dot(h_hi, w_lo) + _dot(h_lo, w_hi) + _dot(h_lo, w_lo)
    lgt = lg.T + br_ref[...]

    rows = lax.broadcasted_iota(jnp.int32, (EXPERTS_PER_GROUP, tm), 0)
    big = jnp.int32(1 << 20)
    glog = jnp.where(rows < N_GROUPS, lgt[0:EXPERTS_PER_GROUP], -jnp.inf)
    gmax = jnp.max(glog, axis=0, keepdims=True)
    gsel = jnp.min(jnp.where(glog == gmax, rows, big), axis=0, keepdims=True)
    gw = 1.0 / jnp.sum(jnp.exp(glog - gmax), axis=0, keepdims=True)
    esel = jnp.zeros((EXPERTS_PER_GROUP, tm), jnp.float32)
    for grp in range(N_GROUPS):
        lo = EXPERTS_PER_GROUP * (grp + 1)
        esel = jnp.where(gsel == grp, lgt[lo:lo + EXPERTS_PER_GROUP], esel)
    v0 = jnp.max(esel, axis=0, keepdims=True)
    i0 = jnp.min(jnp.where(esel == v0, rows, big), axis=0, keepdims=True)
    rest = jnp.where(rows == i0, -jnp.inf, esel)
    v1 = jnp.max(rest, axis=0, keepdims=True)
    i1 = jnp.min(jnp.where(rest == v1, rows, big), axis=0, keepdims=True)
    e = jnp.exp(v1 - v0)
    w0 = (1.0 / (1.0 + e)) * gw
    w1 = (e / (1.0 + e)) * gw
    eid0 = gsel * EXPERTS_PER_GROUP + i0
    eid1 = gsel * EXPERTS_PER_GROUP + i1
    ri_ref[...] = jnp.where(rows == 0, eid0, jnp.where(rows == 1, eid1, 0))
    rows_t = lax.broadcasted_iota(jnp.int32, (ROUTER_ROWS, tm), 0)
    slab = jnp.where(rows_t == 0, w0, jnp.where(rows_t == 1, w1, 0.0))
    slab = jnp.where(rows_t == 2, eid0.astype(jnp.float32),
                     jnp.where(rows_t == 3, eid1.astype(jnp.float32), slab))
    rc_ref[...] = slab.T[:, 0:EXPERTS_PER_GROUP]


def _mix(o1, o4, o16, l1, l4, l16, ob, x2d, g_a, g_b, w_out_bf, g_ffn, wr, br):
    t = x2d.shape[0]
    tm = 512
    row = lambda w: pl.BlockSpec((tm, w), lambda i: (i, 0))
    full = lambda a: pl.BlockSpec(a.shape, lambda i: (0, 0))
    return pl.pallas_call(
        _mix_kernel,
        out_shape=(jax.ShapeDtypeStruct((t, D_MODEL), jnp.float32),
                   jax.ShapeDtypeStruct((t, D_MODEL), jnp.bfloat16),
                   jax.ShapeDtypeStruct((EXPERTS_PER_GROUP, t), jnp.int32),
                   jax.ShapeDtypeStruct((t, EXPERTS_PER_GROUP), jnp.float32)),
        grid=(t // tm,),
        in_specs=[row(GROUP_WIDTH), row(GROUP_WIDTH), row(GROUP_WIDTH),
                  row(N_HEADS), row(N_HEADS), row(N_HEADS),
                  row(GROUP_WIDTH), row(D_MODEL),
                  full(g_a), full(g_b), full(w_out_bf), full(g_ffn), full(wr), full(br)],
        out_specs=(row(D_MODEL), row(D_MODEL),
                   pl.BlockSpec((EXPERTS_PER_GROUP, tm), lambda i: (0, i)),
                   row(EXPERTS_PER_GROUP)),
        compiler_params=pltpu.CompilerParams(dimension_semantics=("arbitrary",),
                                             vmem_limit_bytes=VMEM_LIMIT),
        name="mix_router",
    )(o1, o4, o16, l1, l4, l16, ob, x2d, g_a, g_b, w_out_bf, g_ffn, wr, br)


TOKEN_TILE = 512
RUN_ALIGN = 8
LOCAL_ROWS = 2 * TOKEN_TILE + N_EXPERTS * RUN_ALIGN
MAX_CHUNKS = LOCAL_ROWS // RUN_ALIGN


def _local_slots(member, lo_vec, prior, e_sel, axis):
    ids = lax.broadcasted_iota(jnp.int32, member.shape, axis)
    slot = jnp.where(ids == e_sel, lo_vec + prior, 0.0)
    return jnp.sum(slot, axis=axis, keepdims=True).astype(jnp.int32)


def _chunk_copy(dst_ref, tile, c, local, glob, sem, to_global):
    g = pl.multiple_of(dst_ref[tile * MAX_CHUNKS + c], RUN_ALIGN)
    l = pl.multiple_of(c * RUN_ALIGN, RUN_ALIGN)
    src, dst = local.at[pl.ds(l, RUN_ALIGN), :], glob.at[pl.ds(g, RUN_ALIGN), :]
    if not to_global:
        src, dst = dst, src
    return pltpu.make_async_copy(src, dst, sem)


def _start_chunks(nchunk_ref, dst_ref, tile, local, glob, sem, to_global):
    def body(c, carry):
        _chunk_copy(dst_ref, tile, c, local, glob, sem, to_global).start()
        return carry
    lax.fori_loop(0, nchunk_ref[tile], body, 0)


def _wait_chunks(nchunk_ref, dst_ref, tile, local, glob, sem, to_global):
    def body(c, carry):
        _chunk_copy(dst_ref, tile, c, local, glob, sem, to_global).wait()
        return carry
    lax.fori_loop(0, nchunk_ref[tile], body, 0)


def _lo_vector(lo_ref, tile, shape, axis):
    ids = lax.broadcasted_iota(jnp.int32, shape, axis)
    vec = jnp.zeros(shape, jnp.float32)
    for e in range(N_EXPERTS):
        vec = jnp.where(ids == e, lo_ref[tile * N_EXPERTS + e].astype(jnp.float32), vec)
    return vec


def _dispatch_kernel(lo_ref, nchunk_ref, dst_ref, pad_tile_ref, nused_ref, ri_ref, h_ref, xs_hbm,
                     upper, sbuf, zbuf, sem, zsem, *, n_tiles):
    i = pl.program_id(0)
    n_steps = pl.num_programs(0)
    slot = i % 2

    def zero_tile(tile):
        rows = pl.ds(pl.multiple_of(tile * ROW_TILE, ROW_TILE), ROW_TILE)
        return pltpu.make_async_copy(zbuf, xs_hbm.at[rows, :], zsem)

    @pl.when(i == 0)
    def _():
        r = lax.broadcasted_iota(jnp.int32, (TOKEN_TILE, TOKEN_TILE), 0)
        c = lax.broadcasted_iota(jnp.int32, (TOKEN_TILE, TOKEN_TILE), 1)
        upper[...] = jnp.where(r < c, 1.0, 0.0).astype(jnp.bfloat16)
        zbuf[...] = jnp.zeros_like(zbuf)
        for e in range(N_EXPERTS):
            @pl.when(pad_tile_ref[e] >= 0)
            def _():
                zero_tile(pad_tile_ref[e]).start()
        lax.fori_loop(nused_ref[0], n_tiles, lambda t, c: (zero_tile(t).start(), c)[1], 0)
        for e in range(N_EXPERTS):
            @pl.when(pad_tile_ref[e] >= 0)
            def _():
                zero_tile(pad_tile_ref[e]).wait()
        lax.fori_loop(nused_ref[0], n_tiles, lambda t, c: (zero_tile(t).wait(), c)[1], 0)

    e0 = ri_ref[0:1, :]
    e1 = ri_ref[1:2, :]
    ids = lax.broadcasted_iota(jnp.int32, (N_EXPERTS, TOKEN_TILE), 0)
    member = jnp.where((ids == e0) | (ids == e1), 1.0, 0.0)
    prior = _dot(member.astype(jnp.bfloat16), upper[...])
    lo_vec = _lo_vector(lo_ref, i, (N_EXPERTS, 1), 0)
    d0 = _local_slots(member, lo_vec, prior, e0, 0)
    d1 = _local_slots(member, lo_vec, prior, e1, 0)
    rows = lax.broadcasted_iota(jnp.int32, (LOCAL_ROWS, TOKEN_TILE), 0)
    select = jnp.where((rows == d0) | (rows == d1), 1.0, 0.0).astype(jnp.bfloat16)
    sorted_rows = _dot(select, h_ref[...])

    for s in range(2):
        @pl.when(slot == s)
        def _():
            sbuf[s] = sorted_rows

            @pl.when(i > 0)
            def _():
                _wait_chunks(nchunk_ref, dst_ref, i - 1, sbuf.at[1 - s], xs_hbm, sem, True)

            _start_chunks(nchunk_ref, dst_ref, i, sbuf.at[s], xs_hbm, sem, True)

            @pl.when(i == n_steps - 1)
            def _():
                _wait_chunks(nchunk_ref, dst_ref, i, sbuf.at[s], xs_hbm, sem, True)


def _dispatch(lo, nchunk, dst, pad_tile, nused, route_i, h2, p_rows):
    n_tok = h2.shape[0]
    grid_spec = pltpu.PrefetchScalarGridSpec(
        num_scalar_prefetch=5,
        grid=(n_tok // TOKEN_TILE,),
        in_specs=[pl.BlockSpec((EXPERTS_PER_GROUP, TOKEN_TILE), lambda i, *_: (0, i)),
                  pl.BlockSpec((TOKEN_TILE, D_MODEL), lambda i, *_: (i, 0))],
        out_specs=pl.BlockSpec(memory_space=pl.ANY),
        scratch_shapes=[pltpu.VMEM((TOKEN_TILE, TOKEN_TILE), jnp.bfloat16),
                        pltpu.VMEM((2, LOCAL_ROWS, D_MODEL), jnp.float32),
                        pltpu.VMEM((ROW_TILE, D_MODEL), jnp.float32),
                        pltpu.SemaphoreType.DMA, pltpu.SemaphoreType.DMA],
    )
    return pl.pallas_call(
        functools.partial(_dispatch_kernel, n_tiles=p_rows // ROW_TILE),
        out_shape=jax.ShapeDtypeStruct((p_rows, D_MODEL), jnp.float32),
        grid_spec=grid_spec,
        compiler_params=pltpu.CompilerParams(dimension_semantics=("arbitrary",),
                                             vmem_limit_bytes=VMEM_LIMIT),
        name="dispatch",
    )(lo, nchunk, dst, pad_tile, nused, route_i, h2)


def _moe_kernel(texp_ref, first_ref, nused_ref, xs_ref, wg_ref, wu_ref, wd_ref, y_ref,
                wg_bf, wu_bf, wd_bf):
    j = pl.program_id(0)

    @pl.when(j < nused_ref[0])
    def _():
        @pl.when(first_ref[j] == 1)
        def _():
            wg_bf[...] = wg_ref[...].astype(jnp.bfloat16)
            wu_bf[...] = wu_ref[...].astype(jnp.bfloat16)
            wd_bf[...] = wd_ref[...].astype(jnp.bfloat16)

        x = xs_ref[...].astype(jnp.bfloat16)
        a = _dot(x, wg_bf[...])
        u = _dot(x, wu_bf[...])
        hid = (a * (1.0 / (1.0 + jnp.exp(-a)))) * u
        y_ref[...] = _dot(hid.astype(jnp.bfloat16), wd_bf[...])

    @pl.when(j >= nused_ref[0])
    def _():
        y_ref[...] = jnp.zeros_like(y_ref)


def _moe(texp, first, nused, xs, w_gate, w_up, w_down):
    p_rows = xs.shape[0]
    n_tiles = p_rows // ROW_TILE
    used = lambda j, nu: jnp.minimum(j, nu[0] - 1)
    grid_spec = pltpu.PrefetchScalarGridSpec(
        num_scalar_prefetch=3,
        grid=(n_tiles,),
        in_specs=[
            pl.BlockSpec((ROW_TILE, D_MODEL), lambda j, te, fi, nu: (used(j, nu), 0)),
            pl.BlockSpec((None, D_MODEL, D_EXPERT), lambda j, te, fi, nu: (te[j], 0, 0)),
            pl.BlockSpec((None, D_MODEL, D_EXPERT), lambda j, te, fi, nu: (te[j], 0, 0)),
            pl.BlockSpec((None, D_EXPERT, D_MODEL), lambda j, te, fi, nu: (te[j], 0, 0)),
        ],
        out_specs=pl.BlockSpec((ROW_TILE, D_MODEL), lambda j, te, fi, nu: (j, 0)),
        scratch_shapes=[pltpu.VMEM((D_MODEL, D_EXPERT), jnp.bfloat16),
                        pltpu.VMEM((D_MODEL, D_EXPERT), jnp.bfloat16),
                        pltpu.VMEM((D_EXPERT, D_MODEL), jnp.bfloat16)],
    )
    return pl.pallas_call(
        _moe_kernel,
        out_shape=jax.ShapeDtypeStruct((p_rows, D_MODEL), jnp.float32),
        grid_spec=grid_spec,
        compiler_params=pltpu.CompilerParams(dimension_semantics=("arbitrary",),
                                             vmem_limit_bytes=VMEM_LIMIT),
        name="moe_experts",
    )(texp, first, nused, xs, w_gate, w_up, w_down)


def _final_kernel(lo_ref, nchunk_ref, dst_ref, y_hbm, x1_ref, rc_ref, p_ref, gple_ref, wg_ref, wp_ref,
                  gfin_ref, out_ref, lower, ybuf, sems):
    i = pl.program_id(0)
    n_steps = pl.num_programs(0)
    slot = i % 2

    @pl.when(i == 0)
    def _():
        r = lax.broadcasted_iota(jnp.int32, (TOKEN_TILE, TOKEN_TILE), 0)
        c = lax.broadcasted_iota(jnp.int32, (TOKEN_TILE, TOKEN_TILE), 1)
        lower[...] = jnp.where(c < r, 1.0, 0.0).astype(jnp.bfloat16)
        ybuf[...] = jnp.zeros_like(ybuf)
        _start_chunks(nchunk_ref, dst_ref, 0, ybuf.at[0], y_hbm, sems.at[0], False)

    for s in range(2):
        @pl.when((slot == s) & (i + 1 < n_steps))
        def _():
            _start_chunks(nchunk_ref, dst_ref, i + 1, ybuf.at[1 - s], y_hbm, sems.at[1 - s], False)

    pp = _dot(p_ref[...].astype(jnp.bfloat16), wp_ref[...])
    rc = rc_ref[...]
    w0, w1 = rc[:, 0:1], rc[:, 1:2]
    e0, e1 = rc[:, 2:3].astype(jnp.int32), rc[:, 3:4].astype(jnp.int32)
    ids = lax.broadcasted_iota(jnp.int32, (TOKEN_TILE, ROUTER_ROWS), 1)
    member = jnp.where((ids == e0) | (ids == e1), 1.0, 0.0)
    prior = _dot(lower[...], member.astype(jnp.bfloat16))
    lo_vec = _lo_vector(lo_ref, i, (1, ROUTER_ROWS), 1)
    d0 = _local_slots(member, lo_vec, prior, e0, 1)
    d1 = _local_slots(member, lo_vec, prior, e1, 1)
    cols = lax.broadcasted_iota(jnp.int32, (TOKEN_TILE, LOCAL_ROWS), 1)
    pick0 = jnp.where(cols == d0, 1.0, 0.0).astype(jnp.bfloat16)
    pick1 = jnp.where(cols == d1, 1.0, 0.0).astype(jnp.bfloat16)

    for s in range(2):
        @pl.when(slot == s)
        def _():
            _wait_chunks(nchunk_ref, dst_ref, i, ybuf.at[s], y_hbm, sems.at[s], False)
            yb = ybuf[s].astype(jnp.bfloat16)
            x2 = x1_ref[...] + (w0 * _dot(pick0, yb) + w1 * _dot(pick1, yb))
            z = _dot(_rms(x2, gple_ref[...]).astype(jnp.bfloat16), wg_ref[...])
            gate = 1.0 / (1.0 + jnp.exp(-z))
            x3 = x2 + pp * gate
            out_ref[...] = _rms(x3, gfin_ref[...])


def _final(lo, nchunk, dst, y, x1, rc, p2d, g_ple, w_gate_bf, w_proj_bf, g_final):
    t = x1.shape[0]
    tm = TOKEN_TILE
    row = lambda w: pl.BlockSpec((tm, w), lambda i, *_: (i, 0))
    full = lambda a: pl.BlockSpec(a.shape, lambda i, *_: (0, 0))
    grid_spec = pltpu.PrefetchScalarGridSpec(
        num_scalar_prefetch=3,
        grid=(t // tm,),
        in_specs=[pl.BlockSpec(memory_space=pl.ANY), row(D_MODEL), row(EXPERTS_PER_GROUP),
                  row(PLE_DIM), full(g_ple), full(w_gate_bf), full(w_proj_bf), full(g_final)],
        out_specs=row(D_MODEL),
        scratch_shapes=[pltpu.VMEM((TOKEN_TILE, TOKEN_TILE), jnp.bfloat16),
                        pltpu.VMEM((2, LOCAL_ROWS, D_MODEL), jnp.float32),
                        pltpu.SemaphoreType.DMA((2,))],
    )
    return pl.pallas_call(
        _final_kernel,
        out_shape=jax.ShapeDtypeStruct((t, D_MODEL), jnp.float32),
        grid_spec=grid_spec,
        compiler_params=pltpu.CompilerParams(dimension_semantics=("arbitrary",),
                                             vmem_limit_bytes=VMEM_LIMIT),
        name="combine_ple_final",
    )(lo, nchunk, dst, y, x1, rc, p2d, g_ple, w_gate_bf, w_proj_bf, g_final)


def _dispatch_tables(route_i, n_tok):
    n_tt = n_tok // TOKEN_TILE
    p_rows = 2 * n_tok + n_tt * N_EXPERTS * (RUN_ALIGN - 1) + N_EXPERTS * ROW_TILE
    p_rows = -(-p_rows // ROW_TILE) * ROW_TILE
    n_tiles = p_rows // ROW_TILE
    e = route_i[:2].reshape(2, n_tt, TOKEN_TILE)
    onehot = (e[..., None] == jnp.arange(N_EXPERTS, dtype=jnp.int32)).astype(jnp.int32)
    counts = jnp.sum(onehot, axis=(0, 2))
    n8 = (counts + RUN_ALIGN - 1) // RUN_ALIGN * RUN_ALIGN
    lo = jnp.cumsum(n8, axis=1) - n8
    total = jnp.sum(n8, axis=0)
    tiles_per = (total + ROW_TILE - 1) // ROW_TILE
    tile_end = jnp.cumsum(tiles_per)
    tile_start = tile_end - tiles_per
    g = tile_start[None, :] * ROW_TILE + (jnp.cumsum(n8, axis=0) - n8)
    n_used = tile_end[-1]
    tile_ids = jnp.arange(n_tiles, dtype=jnp.int32)
    clamped = jnp.minimum(tile_ids, n_used - 1)
    texp = jnp.sum((tile_end[None, :] <= clamped[:, None]).astype(jnp.int32), axis=1)
    first = ((tile_ids == tile_start[texp]) & (tile_ids < n_used)).astype(jnp.int32)
    pad_tile = jnp.where(total > 0, tile_end - 1, -1).astype(jnp.int32)
    chunk_row = jnp.arange(MAX_CHUNKS, dtype=jnp.int32)[None, :, None] * RUN_ALIGN
    in_run = (chunk_row >= lo[:, None, :]) & (chunk_row < (lo + n8)[:, None, :])
    dst = jnp.sum(jnp.where(in_run, g[:, None, :] + chunk_row - lo[:, None, :], 0), axis=2)
    nchunk = jnp.sum(n8, axis=1) // RUN_ALIGN
    i32 = lambda a: a.reshape(-1).astype(jnp.int32)
    return (texp, first, n_used.reshape(1).astype(jnp.int32), pad_tile, i32(lo), i32(nchunk),
            i32(dst), p_rows)


def kernel(x, p, g_mix, w_in, sink, g_grp_a, g_grp_b, w_out, g_ffn, w_router_group, b_router_group,
           w_router_expert, b_router_expert, w_expert_gate, w_expert_up, w_expert_down, g_ple,
           w_ple_gate, w_ple_proj, g_final):
    batch, seq, _ = x.shape
    n_tok = batch * seq
    depth = w_in.shape[0]
    assert depth == 1, "the final RMSNorm is fused into the last layer's epilogue"
    bf = jnp.bfloat16
    xc = x.reshape(n_tok, D_MODEL)
    for i in range(depth):
        qa, ka, va, qb, kb, vb = _proj(xc, g_mix[i][None, :], w_in[i].astype(bf))
        outs = [_attn_a(qa, ka, va, batch, seq, dil, window) for window, dil in A_CONFIGS]
        ob = _attn_b(qb, kb, vb, sink[i], batch, seq)

        wr = jnp.zeros((D_MODEL, ROUTER_ROWS), jnp.float32)
        wr = wr.at[:, 0:N_GROUPS].set(w_router_group[i])
        wr = wr.at[:, EXPERTS_PER_GROUP:EXPERTS_PER_GROUP + N_EXPERTS].set(w_router_expert[i])
        br = jnp.zeros((ROUTER_ROWS, 1), jnp.float32)
        br = br.at[0:N_GROUPS, 0].set(b_router_group[i])
        br = br.at[EXPERTS_PER_GROUP:EXPERTS_PER_GROUP + N_EXPERTS, 0].set(b_router_expert[i])

        x1, h2, route_i, route_c = _mix(
            outs[0][0], outs[1][0], outs[2][0], outs[0][1], outs[1][1], outs[2][1], ob, xc,
            g_grp_a[i][None, :], g_grp_b[i][None, :], w_out[i].astype(bf), g_ffn[i][None, :], wr, br)

        texp, first, nused, pad_tile, lo, nchunk, dst, p_rows = _dispatch_tables(route_i, n_tok)
        xs = _dispatch(lo, nchunk, dst, pad_tile, nused, route_i, h2, p_rows)
        y = _moe(texp, first, nused, xs, w_expert_gate[i], w_expert_up[i], w_expert_down[i])
        xc = _final(lo, nchunk, dst, y, x1, route_c, p[i].reshape(n_tok, PLE_DIM), g_ple[i][None, :],
                    w_ple_gate[i].astype(bf), w_ple_proj[i].astype(bf), g_final[None, :])
    return xc.reshape(batch, seq, D_MODEL)
```

```python
import functools

import jax
import jax.numpy as jnp
from jax import lax
from jax.experimental import pallas as pl
from jax.experimental.pallas import tpu as pltpu

D_MODEL = 1024
HEAD_DIM = 64
GROUP_WIDTH = 512
N_HEADS = 8
B_KV_HEADS = 2
B_KV_WIDTH = B_KV_HEADS * HEAD_DIM
A_CONFIGS = ((128, 1), (512, 4), (2048, 16))
B_HALF_WINDOW = 128
N_GROUPS = 4
EXPERTS_PER_GROUP = 8
N_EXPERTS = N_GROUPS * EXPERTS_PER_GROUP
D_EXPERT = 512
PLE_DIM = 256
EPS = 1e-6
NEG = -1e30

LANES = 128
Q_BLOCK = 128
HEADS_PER_PASS = 2
PASS_WIDTH = HEADS_PER_PASS * HEAD_DIM
ROW_TILE = 256
ROUTER_ROWS = 128
VMEM_LIMIT = 48 * 1024 * 1024
ATTN_VMEM_LIMIT = 56 * 1024 * 1024


def _rms(x, g):
    r = lax.rsqrt(jnp.mean(x * x, axis=-1, keepdims=True) + EPS)
    return (x * r) * g


def _dot(a, b):
    return jnp.dot(a, b, preferred_element_type=jnp.float32)


PROJ_TILE = 512


def _proj_kernel(x_ref, g_ref, w_ref, *refs):
    a_refs, (qb_ref, kb_ref, vb_ref, res) = refs[:9], refs[9:]
    h = _rms(x_ref[...], g_ref[...]).astype(jnp.bfloat16)
    scale = HEAD_DIM ** -0.5
    W = GROUP_WIDTH
    tiles = GROUP_WIDTH // PASS_WIDTH
    qkv_a = (_dot(h, w_ref[:, 0:W]) * scale, _dot(h, w_ref[:, W:2 * W]), _dot(h, w_ref[:, 2 * W:3 * W]))
    for part in range(3):
        for pair in range(tiles):
            res[part * tiles + pair] = qkv_a[part][:, pair * PASS_WIDTH:(pair + 1) * PASS_WIDTH]
    qb_ref[...] = (_dot(h, w_ref[:, 3 * W:4 * W]) * scale).astype(jnp.bfloat16)
    kb_ref[...] = _dot(h, w_ref[:, 4 * W:4 * W + B_KV_WIDTH]).astype(jnp.bfloat16)
    vb_ref[...] = _dot(h, w_ref[:, 4 * W + B_KV_WIDTH:4 * W + 2 * B_KV_WIDTH]).astype(jnp.bfloat16)
    for c, (_, dil) in enumerate(A_CONFIGS):
        rows = PROJ_TILE // dil
        for part in range(3):
            out = a_refs[3 * c + part]
            for pair in range(tiles):
                tile = res.at[part * tiles + pair]
                for rho in range(dil):
                    out[pair, rho] = tile[pl.ds(rho, rows, stride=dil), :].astype(jnp.bfloat16)


def _proj(x2d, g_mix, w_in_bf, batch, seq):
    t = x2d.shape[0]
    tm = PROJ_TILE
    tiles_per_seq = seq // tm
    in_width = w_in_bf.shape[1]
    n_pairs = N_HEADS // HEADS_PER_PASS
    row = lambda w: pl.BlockSpec((tm, w), lambda i: (i, 0))
    a_shapes, a_specs = [], []
    for _, dil in A_CONFIGS:
        for _ in range(3):
            a_shapes.append(jax.ShapeDtypeStruct((batch, n_pairs, dil, seq // dil, PASS_WIDTH),
                                                 jnp.bfloat16))
            a_specs.append(pl.BlockSpec((None, n_pairs, dil, tm // dil, PASS_WIDTH),
                                        lambda i: (i // tiles_per_seq, 0, 0, i % tiles_per_seq, 0)))
    wide = jax.ShapeDtypeStruct((t, GROUP_WIDTH), jnp.bfloat16)
    narrow = jax.ShapeDtypeStruct((t, B_KV_WIDTH), jnp.bfloat16)
    return pl.pallas_call(
        _proj_kernel,
        out_shape=tuple(a_shapes) + (wide, narrow, narrow),
        grid=(t // tm,),
        in_specs=[row(D_MODEL),
                  pl.BlockSpec((1, D_MODEL), lambda i: (0, 0)),
                  pl.BlockSpec((D_MODEL, in_width), lambda i: (0, 0))],
        out_specs=tuple(a_specs) + (row(GROUP_WIDTH), row(B_KV_WIDTH), row(B_KV_WIDTH)),
        scratch_shapes=[pltpu.VMEM((3 * GROUP_WIDTH // PASS_WIDTH, tm, PASS_WIDTH), jnp.float32)],
        compiler_params=pltpu.CompilerParams(dimension_semantics=("arbitrary",),
                                             vmem_limit_bytes=VMEM_LIMIT),
        name="proj",
    )(x2d, g_mix, w_in_bf)


def _fill_bias(bias_ref, half, nkeys, dist_scale):
    i = lax.broadcasted_iota(jnp.int32, (Q_BLOCK, nkeys), 0)
    j = lax.broadcasted_iota(jnp.int32, (Q_BLOCK, nkeys), 1)
    for var, rel0 in enumerate((0, -half, Q_BLOCK - nkeys)):
        d = jnp.abs(j + rel0 - i)
        dist = d.astype(jnp.float32) * dist_scale
        for h in range(N_HEADS):
            slope = 2.0 ** (-(h + 1))
            bias_ref[var, h] = jnp.where(d <= half, -(slope * dist), NEG)


def _key_window(q0, seq_len, half, nkeys):
    lo = q0 - half
    ks = pl.multiple_of(jnp.clip(lo, 0, seq_len - nkeys), HEAD_DIM)
    var = jnp.where(lo < 0, 0, jnp.where(lo > seq_len - nkeys, 2, 1))
    return ks, var


def _fold_lane_tiles(a, op):
    out = a[..., 0:LANES]
    for t in range(1, a.shape[-1] // LANES):
        out = op(out, a[..., t * LANES:(t + 1) * LANES])
    return out


def _attend_pass(q, k, v, bias, sinks):
    nkeys = k.shape[0]
    lane_head = lax.broadcasted_iota(jnp.int32, (Q_BLOCK, PASS_WIDTH), 1) // HEAD_DIM
    zero = jnp.zeros_like(q)
    q_stack = jnp.concatenate(
        [jnp.where(lane_head == h, q, zero) for h in range(HEADS_PER_PASS)], axis=0)
    s = lax.dot_general(q_stack, k, (((1,), (1,)), ((), ())), preferred_element_type=jnp.float32)
    s = s.reshape(HEADS_PER_PASS, Q_BLOCK, nkeys) + bias
    m_tile = _fold_lane_tiles(s, jnp.maximum)
    if sinks is not None:
        tile_head = lax.broadcasted_iota(jnp.int32, m_tile.shape, 0)
        tile_lane = lax.broadcasted_iota(jnp.int32, m_tile.shape, 2)
        sk = jnp.zeros(m_tile.shape, jnp.float32)
        for h in range(HEADS_PER_PASS):
            sk = jnp.where(tile_head == h, sinks[h], sk)
        m_tile = jnp.maximum(m_tile, sk)
    m = jnp.max(m_tile, axis=-1, keepdims=True)
    p = jnp.exp(s - m)
    den_tile = _fold_lane_tiles(p, jnp.add)
    if sinks is not None:
        den_tile = den_tile + jnp.where(tile_lane == 0, jnp.exp(sk - m), 0.0)
    den = jnp.sum(den_tile, axis=-1, keepdims=True)
    pv = _dot(p.reshape(HEADS_PER_PASS * Q_BLOCK, nkeys).astype(jnp.bfloat16), v)
    pv = pv.reshape(HEADS_PER_PASS, Q_BLOCK, PASS_WIDTH) * (1.0 / den)
    lse = m + jnp.log(den)
    o = jnp.zeros((Q_BLOCK, PASS_WIDTH), jnp.float32)
    lse_lanes = jnp.zeros((Q_BLOCK, PASS_WIDTH), jnp.float32)
    for h in range(HEADS_PER_PASS):
        o = jnp.where(lane_head == h, pv[h], o)
        lse_lanes = jnp.where(lane_head == h, lse[h], lse_lanes)
    return o, lse_lanes


A_HALF = 64
A_KEYS = Q_BLOCK + 2 * A_HALF


def _attn_a_kernel(*refs, seq):
    n_cfg = len(A_CONFIGS)
    qkv = refs[:3 * n_cfg]
    o_ref, bias_ref = refs[3 * n_cfg], refs[3 * n_cfg + 1]
    scratch = refs[3 * n_cfg + 2:]
    pair = pl.program_id(1)

    @pl.when((pl.program_id(0) == 0) & (pair == 0))
    def _():
        for c, (_, dil) in enumerate(A_CONFIGS):
            _fill_bias(bias_ref.at[c], A_HALF, A_KEYS, float(dil))

    for c, (window, dil) in enumerate(A_CONFIGS):
        assert window // (2 * dil) == A_HALF
        q_ref, k_ref, v_ref = qkv[3 * c:3 * c + 3]
        o_sc, l_sc = scratch[2 * c], scratch[2 * c + 1]
        sub_len = seq // dil
        blocks = sub_len // Q_BLOCK

        def body(idx, carry, q_ref=q_ref, k_ref=k_ref, v_ref=v_ref, o_sc=o_sc, l_sc=l_sc, c=c,
                 sub_len=sub_len, blocks=blocks):
            rho = idx // blocks
            q0 = pl.multiple_of((idx % blocks) * Q_BLOCK, Q_BLOCK)
            ks, var = _key_window(q0, sub_len, A_HALF, A_KEYS)
            bias = bias_ref[c, var, pl.ds(pair * HEADS_PER_PASS, HEADS_PER_PASS)]
            o, lse = _attend_pass(q_ref[rho, pl.ds(q0, Q_BLOCK), :], k_ref[rho, pl.ds(ks, A_KEYS), :],
                                  v_ref[rho, pl.ds(ks, A_KEYS), :], bias, None)
            o_sc[rho, pl.ds(q0, Q_BLOCK), :] = o
            l_sc[rho, pl.ds(q0, Q_BLOCK), :] = lse
            return carry

        lax.fori_loop(0, dil * blocks, body, 0, unroll=2)

    widest = A_CONFIGS[-1][1]
    rows = seq // widest
    for r in range(widest):
        outs, lses = [], []
        for c, (_, dil) in enumerate(A_CONFIGS):
            step = widest // dil
            idx = (r % dil, pl.ds(r // dil, rows, stride=step) if step > 1 else pl.ds(0, rows))
            outs.append(scratch[2 * c].at[idx[0]][idx[1], :])
            lses.append(scratch[2 * c + 1].at[idx[0]][idx[1], :])
        mx = functools.reduce(jnp.maximum, lses)
        es = [jnp.exp(l - mx) for l in lses]
        inv = 1.0 / functools.reduce(jnp.add, es)
        merged = functools.reduce(jnp.add, [(e * inv) * o for e, o in zip(es, outs)])
        o_ref[pl.ds(r, rows, stride=widest), :] = merged


def _attn_a(qkv, batch, seq):
    n_pairs = N_HEADS // HEADS_PER_PASS
    in_specs, scratch = [], [pltpu.VMEM((len(A_CONFIGS), 3, N_HEADS, Q_BLOCK, A_KEYS), jnp.float32)]
    for _, dil in A_CONFIGS:
        blk = (None, None, dil, seq // dil, PASS_WIDTH)
        in_specs += [pl.BlockSpec(blk, lambda b, p: (b, p, 0, 0, 0))] * 3
        scratch += [pltpu.VMEM((dil, seq // dil, PASS_WIDTH), jnp.float32)] * 2
    return pl.pallas_call(
        functools.partial(_attn_a_kernel, seq=seq),
        out_shape=jax.ShapeDtypeStruct((batch, seq, GROUP_WIDTH), jnp.float32),
        grid=(batch, n_pairs),
        in_specs=in_specs,
        out_specs=pl.BlockSpec((None, seq, PASS_WIDTH), lambda b, p: (b, 0, p)),
        scratch_shapes=scratch,
        compiler_params=pltpu.CompilerParams(dimension_semantics=("arbitrary", "arbitrary"),
                                             vmem_limit_bytes=ATTN_VMEM_LIMIT),
        name="attn_a",
    )(*qkv).reshape(batch * seq, GROUP_WIDTH)


def _attn_b_kernel(sink_ref, q_ref, k_ref, v_ref, o_ref, bias_ref, k4_ref, v4_ref, *, half, nkeys,
                   seq_len, q_tile):
    first = (pl.program_id(0) == 0) & (pl.program_id(1) == 0)

    @pl.when(first)
    def _():
        _fill_bias(bias_ref, half, nkeys, 1.0)

    @pl.when(pl.program_id(1) == 0)
    def _():
        c = lax.broadcasted_iota(jnp.int32, (B_KV_WIDTH, GROUP_WIDTH), 0)
        j = lax.broadcasted_iota(jnp.int32, (B_KV_WIDTH, GROUP_WIDTH), 1)
        src = (j // (GROUP_WIDTH // B_KV_HEADS)) * HEAD_DIM + j % HEAD_DIM
        rep = jnp.where(c == src, 1.0, 0.0).astype(jnp.bfloat16)
        chunk = 512

        def body(i, carry):
            rows = pl.ds(pl.multiple_of(i * chunk, chunk), chunk)
            k4_ref[rows, :] = _dot(k_ref[rows, :], rep).astype(jnp.bfloat16)
            v4_ref[rows, :] = _dot(v_ref[rows, :], rep).astype(jnp.bfloat16)
            return carry

        lax.fori_loop(0, seq_len // chunk, body, 0)

    base = pl.program_id(1) * q_tile

    def body(jb, carry):
        row0 = pl.multiple_of(jb * Q_BLOCK, Q_BLOCK)
        ks, var = _key_window(base + row0, seq_len, half, nkeys)
        for g in range(N_HEADS // HEADS_PER_PASS):
            cols = slice(g * PASS_WIDTH, (g + 1) * PASS_WIDTH)
            heads = pl.ds(g * HEADS_PER_PASS, HEADS_PER_PASS)
            sinks = [sink_ref[g * HEADS_PER_PASS + h] for h in range(HEADS_PER_PASS)]
            o, _ = _attend_pass(q_ref[pl.ds(row0, Q_BLOCK), cols], k4_ref[pl.ds(ks, nkeys), cols],
                                v4_ref[pl.ds(ks, nkeys), cols], bias_ref[var, heads], sinks)
            o_ref[pl.ds(row0, Q_BLOCK), cols] = o
        return carry

    lax.fori_loop(0, q_tile // Q_BLOCK, body, 0, unroll=2)


def _attn_b(q, k, v, sink, batch, seq):
    half = B_HALF_WINDOW
    nkeys = Q_BLOCK + 2 * half
    q_tile = 1024
    view = lambda a: a.reshape(batch, seq, a.shape[-1])
    kern = functools.partial(_attn_b_kernel, half=half, nkeys=nkeys, seq_len=seq, q_tile=q_tile)
    o = pl.pallas_call(
        kern,
        out_shape=jax.ShapeDtypeStruct((batch, seq, GROUP_WIDTH), jnp.float32),
        grid=(batch, seq // q_tile),
        in_specs=[pl.BlockSpec(memory_space=pltpu.SMEM),
                  pl.BlockSpec((None, q_tile, GROUP_WIDTH), lambda b, i: (b, i, 0)),
                  pl.BlockSpec((None, seq, B_KV_WIDTH), lambda b, i: (b, 0, 0)),
                  pl.BlockSpec((None, seq, B_KV_WIDTH), lambda b, i: (b, 0, 0))],
        out_specs=pl.BlockSpec((None, q_tile, GROUP_WIDTH), lambda b, i: (b, i, 0)),
        scratch_shapes=[pltpu.VMEM((3, N_HEADS, Q_BLOCK, nkeys), jnp.float32),
                        pltpu.VMEM((seq, GROUP_WIDTH), jnp.bfloat16),
                        pltpu.VMEM((seq, GROUP_WIDTH), jnp.bfloat16)],
        compiler_params=pltpu.CompilerParams(dimension_semantics=("arbitrary", "arbitrary"),
                                             vmem_limit_bytes=VMEM_LIMIT),
        name="attn_b",
    )(sink, view(q), view(k), view(v))
    return o.reshape(batch * seq, GROUP_WIDTH)


def _mix_kernel(oa_ref, ob_ref, x_ref, ga_ref, gb_ref, wout_ref, gffn_ref, wr_ref, br_ref,
                x1_ref, h2_ref, ri_ref, rc_ref):
    tm = x_ref.shape[0]
    na = _rms(oa_ref[...], ga_ref[...]).astype(jnp.bfloat16)
    nb = _rms(ob_ref[...], gb_ref[...]).astype(jnp.bfloat16)
    x1 = (x_ref[...] + _dot(na, wout_ref[0:GROUP_WIDTH, :])
          + _dot(nb, wout_ref[GROUP_WIDTH:2 * GROUP_WIDTH, :]))
    x1_ref[...] = x1
    h2 = _rms(x1, gffn_ref[...])
    h2_ref[...] = h2.astype(jnp.bfloat16)

    h_hi = h2.astype(jnp.bfloat16)
    h_lo = (h2 - h_hi.astype(jnp.float32)).astype(jnp.bfloat16)
    wr = wr_ref[...]
    w_hi = wr.astype(jnp.bfloat16)
    w_lo = (wr - w_hi.astype(jnp.float32)).astype(jnp.bfloat16)
    lg = _dot(h_hi, w_hi) + _dot(h_hi, w_lo) + _dot(h_lo, w_hi) + _dot(h_lo, w_lo)
    lgt = lg.T + br_ref[...]

    rows = lax.broadcasted_iota(jnp.int32, (EXPERTS_PER_GROUP, tm), 0)
    big = jnp.int32(1 << 20)
    glog = jnp.where(rows < N_GROUPS, lgt[0:EXPERTS_PER_GROUP], -jnp.inf)
    gmax = jnp.max(glog, axis=0, keepdims=True)
    gsel = jnp.min(jnp.where(glog == gmax, rows, big), axis=0, keepdims=True)
    gw = 1.0 / jnp.sum(jnp.exp(glog - gmax), axis=0, keepdims=True)
    esel = jnp.zeros((EXPERTS_PER_GROUP, tm), jnp.float32)
    for grp in range(N_GROUPS):
        lo = EXPERTS_PER_GROUP * (grp + 1)
        esel = jnp.where(gsel == grp, lgt[lo:lo + EXPERTS_PER_GROUP], esel)
    v0 = jnp.max(esel, axis=0, keepdims=True)
    i0 = jnp.min(jnp.where(esel == v0, rows, big), axis=0, keepdims=True)
    rest = jnp.where(rows == i0, -jnp.inf, esel)
    v1 = jnp.max(rest, axis=0, keepdims=True)
    i1 = jnp.min(jnp.where(rest == v1, rows, big), axis=0, keepdims=True)
    e = jnp.exp(v1 - v0)
    w0 = (1.0 / (1.0 + e)) * gw
    w1 = (e / (1.0 + e)) * gw
    eid0 = gsel * EXPERTS_PER_GROUP + i0
    eid1 = gsel * EXPERTS_PER_GROUP + i1
    ri_ref[...] = jnp.where(rows == 0, eid0, jnp.where(rows == 1, eid1, 0))
    rows_t = lax.broadcasted_iota(jnp.int32, (ROUTER_ROWS, tm), 0)
    slab = jnp.where(rows_t == 0, w0, jnp.where(rows_t == 1, w1, 0.0))
    slab = jnp.where(rows_t == 2, eid0.astype(jnp.float32),
                     jnp.where(rows_t == 3, eid1.astype(jnp.float32), slab))
    rc_ref[...] = slab.T[:, 0:EXPERTS_PER_GROUP]


def _mix(oa, ob, x2d, g_a, g_b, w_out_bf, g_ffn, wr, br):
    t = x2d.shape[0]
    tm = 512
    row = lambda w: pl.BlockSpec((tm, w), lambda i: (i, 0))
    full = lambda a: pl.BlockSpec(a.shape, lambda i: (0, 0))
    return pl.pallas_call(
        _mix_kernel,
        out_shape=(jax.ShapeDtypeStruct((t, D_MODEL), jnp.float32),
                   jax.ShapeDtypeStruct((t, D_MODEL), jnp.bfloat16),
                   jax.ShapeDtypeStruct((EXPERTS_PER_GROUP, t), jnp.int32),
                   jax.ShapeDtypeStruct((t, EXPERTS_PER_GROUP), jnp.float32)),
        grid=(t // tm,),
        in_specs=[row(GROUP_WIDTH), row(GROUP_WIDTH), row(D_MODEL),
                  full(g_a), full(g_b), full(w_out_bf), full(g_ffn), full(wr), full(br)],
        out_specs=(row(D_MODEL), row(D_MODEL),
                   pl.BlockSpec((EXPERTS_PER_GROUP, tm), lambda i: (0, i)),
                   row(EXPERTS_PER_GROUP)),
        compiler_params=pltpu.CompilerParams(dimension_semantics=("arbitrary",),
                                             vmem_limit_bytes=VMEM_LIMIT),
        name="mix_router",
    )(oa, ob, x2d, g_a, g_b, w_out_bf, g_ffn, wr, br)


TOKEN_TILE = 512
RUN_ALIGN = 8
LOCAL_ROWS = 2 * TOKEN_TILE + N_EXPERTS * RUN_ALIGN
MAX_CHUNKS = LOCAL_ROWS // RUN_ALIGN


def _local_slots(member, lo_vec, prior, e_sel, axis):
    ids = lax.broadcasted_iota(jnp.int32, member.shape, axis)
    slot = jnp.where(ids == e_sel, lo_vec + prior, 0.0)
    return jnp.sum(slot, axis=axis, keepdims=True).astype(jnp.int32)


def _chunk_copy(dst_ref, tile, c, local, glob, sem, to_global):
    g = pl.multiple_of(dst_ref[tile * MAX_CHUNKS + c], RUN_ALIGN)
    l = pl.multiple_of(c * RUN_ALIGN, RUN_ALIGN)
    src, dst = local.at[pl.ds(l, RUN_ALIGN), :], glob.at[pl.ds(g, RUN_ALIGN), :]
    if not to_global:
        src, dst = dst, src
    return pltpu.make_async_copy(src, dst, sem)


def _start_chunks(nchunk_ref, dst_ref, tile, local, glob, sem, to_global):
    def body(c, carry):
        _chunk_copy(dst_ref, tile, c, local, glob, sem, to_global).start()
        return carry
    lax.fori_loop(0, nchunk_ref[tile], body, 0)


def _wait_chunks(nchunk_ref, dst_ref, tile, local, glob, sem, to_global):
    def body(c, carry):
        _chunk_copy(dst_ref, tile, c, local, glob, sem, to_global).wait()
        return carry
    lax.fori_loop(0, nchunk_ref[tile], body, 0)


def _lo_vector(lo_ref, tile, shape, axis):
    ids = lax.broadcasted_iota(jnp.int32, shape, axis)
    vec = jnp.zeros(shape, jnp.float32)
    for e in range(N_EXPERTS):
        vec = jnp.where(ids == e, lo_ref[tile * N_EXPERTS + e].astype(jnp.float32), vec)
    return vec


def _dispatch_kernel(lo_ref, nchunk_ref, dst_ref, pad_tile_ref, nused_ref, ri_ref, h_ref, xs_hbm,
                     upper, sbuf, zbuf, sem, zsem, *, n_tiles):
    i = pl.program_id(0)
    n_steps = pl.num_programs(0)
    slot = i % 2

    def zero_tile(tile):
        rows = pl.ds(pl.multiple_of(tile * ROW_TILE, ROW_TILE), ROW_TILE)
        return pltpu.make_async_copy(zbuf, xs_hbm.at[rows, :], zsem)

    @pl.when(i == 0)
    def _():
        r = lax.broadcasted_iota(jnp.int32, (TOKEN_TILE, TOKEN_TILE), 0)
        c = lax.broadcasted_iota(jnp.int32, (TOKEN_TILE, TOKEN_TILE), 1)
        upper[...] = jnp.where(r < c, 1.0, 0.0).astype(jnp.bfloat16)
        zbuf[...] = jnp.zeros_like(zbuf)
        for e in range(N_EXPERTS):
            @pl.when(pad_tile_ref[e] >= 0)
            def _():
                zero_tile(pad_tile_ref[e]).start()
        lax.fori_loop(nused_ref[0], n_tiles, lambda t, c: (zero_tile(t).start(), c)[1], 0)
        for e in range(N_EXPERTS):
            @pl.when(pad_tile_ref[e] >= 0)
            def _():
                zero_tile(pad_tile_ref[e]).wait()
        lax.fori_loop(nused_ref[0], n_tiles, lambda t, c: (zero_tile(t).wait(), c)[1], 0)

    e0 = ri_ref[0:1, :]
    e1 = ri_ref[1:2, :]
    ids = lax.broadcasted_iota(jnp.int32, (N_EXPERTS, TOKEN_TILE), 0)
    member = jnp.where((ids == e0) | (ids == e1), 1.0, 0.0)
    prior = _dot(member.astype(jnp.bfloat16), upper[...])
    lo_vec = _lo_vector(lo_ref, i, (N_EXPERTS, 1), 0)
    d0 = _local_slots(member, lo_vec, prior, e0, 0)
    d1 = _local_slots(member, lo_vec, prior, e1, 0)
    rows = lax.broadcasted_iota(jnp.int32, (LOCAL_ROWS, TOKEN_TILE), 0)
    select = jnp.where((rows == d0) | (rows == d1), 1.0, 0.0).astype(jnp.bfloat16)
    sorted_rows = _dot(select, h_ref[...])

    for s in range(2):
        @pl.when(slot == s)
        def _():
            sbuf[s] = sorted_rows

            @pl.when(i > 0)
            def _():
                _wait_chunks(nchunk_ref, dst_ref, i - 1, sbuf.at[1 - s], xs_hbm, sem, True)

            _start_chunks(nchunk_ref, dst_ref, i, sbuf.at[s], xs_hbm, sem, True)

            @pl.when(i == n_steps - 1)
            def _():
                _wait_chunks(nchunk_ref, dst_ref, i, sbuf.at[s], xs_hbm, sem, True)


def _dispatch(lo, nchunk, dst, pad_tile, nused, route_i, h2, p_rows):
    n_tok = h2.shape[0]
    grid_spec = pltpu.PrefetchScalarGridSpec(
        num_scalar_prefetch=5,
        grid=(n_tok // TOKEN_TILE,),
        in_specs=[pl.BlockSpec((EXPERTS_PER_GROUP, TOKEN_TILE), lambda i, *_: (0, i)),
                  pl.BlockSpec((TOKEN_TILE, D_MODEL), lambda i, *_: (i, 0))],
        out_specs=pl.BlockSpec(memory_space=pl.ANY),
        scratch_shapes=[pltpu.VMEM((TOKEN_TILE, TOKEN_TILE), jnp.bfloat16),
                        pltpu.VMEM((2, LOCAL_ROWS, D_MODEL), jnp.float32),
                        pltpu.VMEM((ROW_TILE, D_MODEL), jnp.float32),
                        pltpu.SemaphoreType.DMA, pltpu.SemaphoreType.DMA],
    )
    return pl.pallas_call(
        functools.partial(_dispatch_kernel, n_tiles=p_rows // ROW_TILE),
        out_shape=jax.ShapeDtypeStruct((p_rows, D_MODEL), jnp.float32),
        grid_spec=grid_spec,
        compiler_params=pltpu.CompilerParams(dimension_semantics=("arbitrary",),
                                             vmem_limit_bytes=VMEM_LIMIT),
        name="dispatch",
    )(lo, nchunk, dst, pad_tile, nused, route_i, h2)


def _moe_kernel(texp_ref, first_ref, nused_ref, xs_ref, wg_ref, wu_ref, wd_ref, y_ref,
                wg_bf, wu_bf, wd_bf):
    j = pl.program_id(0)

    @pl.when(j < nused_ref[0])
    def _():
        @pl.when(first_ref[j] == 1)
        def _():
            wg_bf[...] = wg_ref[...].astype(jnp.bfloat16)
            wu_bf[...] = wu_ref[...].astype(jnp.bfloat16)
            wd_bf[...] = wd_ref[...].astype(jnp.bfloat16)

        x = xs_ref[...].astype(jnp.bfloat16)
        a = _dot(x, wg_bf[...])
        u = _dot(x, wu_bf[...])
        hid = (a * (1.0 / (1.0 + jnp.exp(-a)))) * u
        y_ref[...] = _dot(hid.astype(jnp.bfloat16), wd_bf[...])

    @pl.when(j >= nused_ref[0])
    def _():
        y_ref[...] = jnp.zeros_like(y_ref)


def _moe(texp, first, nused, xs, w_gate, w_up, w_down):
    p_rows = xs.shape[0]
    n_tiles = p_rows // ROW_TILE
    used = lambda j, nu: jnp.minimum(j, nu[0] - 1)
    grid_spec = pltpu.PrefetchScalarGridSpec(
        num_scalar_prefetch=3,
        grid=(n_tiles,),
        in_specs=[
            pl.BlockSpec((ROW_TILE, D_MODEL), lambda j, te, fi, nu: (used(j, nu), 0)),
            pl.BlockSpec((None, D_MODEL, D_EXPERT), lambda j, te, fi, nu: (te[j], 0, 0)),
            pl.BlockSpec((None, D_MODEL, D_EXPERT), lambda j, te, fi, nu: (te[j], 0, 0)),
            pl.BlockSpec((None, D_EXPERT, D_MODEL), lambda j, te, fi, nu: (te[j], 0, 0)),
        ],
        out_specs=pl.BlockSpec((ROW_TILE, D_MODEL), lambda j, te, fi, nu: (j, 0)),
        scratch_shapes=[pltpu.VMEM((D_MODEL, D_EXPERT), jnp.bfloat16),
                        pltpu.VMEM((D_MODEL, D_EXPERT), jnp.bfloat16),
                        pltpu.VMEM((D_EXPERT, D_MODEL), jnp.bfloat16)],
    )
    return pl.pallas_call(
        _moe_kernel,
        out_shape=jax.ShapeDtypeStruct((p_rows, D_MODEL), jnp.float32),
        grid_spec=grid_spec,
        compiler_params=pltpu.CompilerParams(dimension_semantics=("arbitrary",),
                                             vmem_limit_bytes=VMEM_LIMIT),
        name="moe_experts",
    )(texp, first, nused, xs, w_gate, w_up, w_down)


def _final_kernel(lo_ref, nchunk_ref, dst_ref, y_hbm, x1_ref, rc_ref, p_ref, gple_ref, wg_ref, wp_ref,
                  gfin_ref, out_ref, lower, ybuf, sems):
    i = pl.program_id(0)
    n_steps = pl.num_programs(0)
    slot = i % 2

    @pl.when(i == 0)
    def _():
        r = lax.broadcasted_iota(jnp.int32, (TOKEN_TILE, TOKEN_TILE), 0)
        c = lax.broadcasted_iota(jnp.int32, (TOKEN_TILE, TOKEN_TILE), 1)
        lower[...] = jnp.where(c < r, 1.0, 0.0).astype(jnp.bfloat16)
        ybuf[...] = jnp.zeros_like(ybuf)
        _start_chunks(nchunk_ref, dst_ref, 0, ybuf.at[0], y_hbm, sems.at[0], False)

    for s in range(2):
        @pl.when((slot == s) & (i + 1 < n_steps))
        def _():
            _start_chunks(nchunk_ref, dst_ref, i + 1, ybuf.at[1 - s], y_hbm, sems.at[1 - s], False)

    pp = _dot(p_ref[...].astype(jnp.bfloat16), wp_ref[...])
    rc = rc_ref[...]
    w0, w1 = rc[:, 0:1], rc[:, 1:2]
    e0, e1 = rc[:, 2:3].astype(jnp.int32), rc[:, 3:4].astype(jnp.int32)
    ids = lax.broadcasted_iota(jnp.int32, (TOKEN_TILE, ROUTER_ROWS), 1)
    member = jnp.where((ids == e0) | (ids == e1), 1.0, 0.0)
    prior = _dot(lower[...], member.astype(jnp.bfloat16))
    lo_vec = _lo_vector(lo_ref, i, (1, ROUTER_ROWS), 1)
    d0 = _local_slots(member, lo_vec, prior, e0, 1)
    d1 = _local_slots(member, lo_vec, prior, e1, 1)
    cols = lax.broadcasted_iota(jnp.int32, (TOKEN_TILE, LOCAL_ROWS), 1)
    pick0 = jnp.where(cols == d0, 1.0, 0.0).astype(jnp.bfloat16)
    pick1 = jnp.where(cols == d1, 1.0, 0.0).astype(jnp.bfloat16)

    for s in range(2):
        @pl.when(slot == s)
        def _():
            _wait_chunks(nchunk_ref, dst_ref, i, ybuf.at[s], y_hbm, sems.at[s], False)
            yb = ybuf[s].astype(jnp.bfloat16)
            x2 = x1_ref[...] + (w0 * _dot(pick0, yb) + w1 * _dot(pick1, yb))
            z = _dot(_rms(x2, gple_ref[...]).astype(jnp.bfloat16), wg_ref[...])
            gate = 1.0 / (1.0 + jnp.exp(-z))
            x3 = x2 + pp * gate
            out_ref[...] = _rms(x3, gfin_ref[...])


def _final(lo, nchunk, dst, y, x1, rc, p2d, g_ple, w_gate_bf, w_proj_bf, g_final):
    t = x1.shape[0]
    tm = TOKEN_TILE
    row = lambda w: pl.BlockSpec((tm, w), lambda i, *_: (i, 0))
    full = lambda a: pl.BlockSpec(a.shape, lambda i, *_: (0, 0))
    grid_spec = pltpu.PrefetchScalarGridSpec(
        num_scalar_prefetch=3,
        grid=(t // tm,),
        in_specs=[pl.BlockSpec(memory_space=pl.ANY), row(D_MODEL), row(EXPERTS_PER_GROUP),
                  row(PLE_DIM), full(g_ple), full(w_gate_bf), full(w_proj_bf), full(g_final)],
        out_specs=row(D_MODEL),
        scratch_shapes=[pltpu.VMEM((TOKEN_TILE, TOKEN_TILE), jnp.bfloat16),
                        pltpu.VMEM((2, LOCAL_ROWS, D_MODEL), jnp.float32),
                        pltpu.SemaphoreType.DMA((2,))],
    )
    return pl.pallas_call(
        _final_kernel,
        out_shape=jax.ShapeDtypeStruct((t, D_MODEL), jnp.float32),
        grid_spec=grid_spec,
        compiler_params=pltpu.CompilerParams(dimension_semantics=("arbitrary",),
                                             vmem_limit_bytes=VMEM_LIMIT),
        name="combine_ple_final",
    )(lo, nchunk, dst, y, x1, rc, p2d, g_ple, w_gate_bf, w_proj_bf, g_final)


def _dispatch_tables(route_i, n_tok):
    n_tt = n_tok // TOKEN_TILE
    p_rows = 2 * n_tok + n_tt * N_EXPERTS * (RUN_ALIGN - 1) + N_EXPERTS * ROW_TILE
    p_rows = -(-p_rows // ROW_TILE) * ROW_TILE
    n_tiles = p_rows // ROW_TILE
    e = route_i[:2].reshape(2, n_tt, TOKEN_TILE)
    onehot = (e[..., None] == jnp.arange(N_EXPERTS, dtype=jnp.int32)).astype(jnp.int32)
    counts = jnp.sum(onehot, axis=(0, 2))
    n8 = (counts + RUN_ALIGN - 1) // RUN_ALIGN * RUN_ALIGN
    lo = jnp.cumsum(n8, axis=1) - n8
    total = jnp.sum(n8, axis=0)
    tiles_per = (total + ROW_TILE - 1) // ROW_TILE
    tile_end = jnp.cumsum(tiles_per)
    tile_start = tile_end - tiles_per
    g = tile_start[None, :] * ROW_TILE + (jnp.cumsum(n8, axis=0) - n8)
    n_used = tile_end[-1]
    tile_ids = jnp.arange(n_tiles, dtype=jnp.int32)
    clamped = jnp.minimum(tile_ids, n_used - 1)
    texp = jnp.sum((tile_end[None, :] <= clamped[:, None]).astype(jnp.int32), axis=1)
    first = ((tile_ids == tile_start[texp]) & (tile_ids < n_used)).astype(jnp.int32)
    pad_tile = jnp.where(total > 0, tile_end - 1, -1).astype(jnp.int32)
    chunk_row = jnp.arange(MAX_CHUNKS, dtype=jnp.int32)[None, :, None] * RUN_ALIGN
    in_run = (chunk_row >= lo[:, None, :]) & (chunk_row < (lo + n8)[:, None, :])
    dst = jnp.sum(jnp.where(in_run, g[:, None, :] + chunk_row - lo[:, None, :], 0), axis=2)
    nchunk = jnp.sum(n8, axis=1) // RUN_ALIGN
    i32 = lambda a: a.reshape(-1).astype(jnp.int32)
    return (texp, first, n_used.reshape(1).astype(jnp.int32), pad_tile, i32(lo), i32(nchunk),
            i32(dst), p_rows)


def kernel(x, p, g_mix, w_in, sink, g_grp_a, g_grp_b, w_out, g_ffn, w_router_group, b_router_group,
           w_router_expert, b_router_expert, w_expert_gate, w_expert_up, w_expert_down, g_ple,
           w_ple_gate, w_ple_proj, g_final):
    batch, seq, _ = x.shape
    n_tok = batch * seq
    depth = w_in.shape[0]
    assert depth == 1, "the final RMSNorm is fused into the last layer's epilogue"
    bf = jnp.bfloat16
    xc = x.reshape(n_tok, D_MODEL)
    for i in range(depth):
        *qkv_a, qb, kb, vb = _proj(xc, g_mix[i][None, :], w_in[i].astype(bf), batch, seq)
        oa = _attn_a(qkv_a, batch, seq)
        ob = _attn_b(qb, kb, vb, sink[i], batch, seq)

        wr = jnp.zeros((D_MODEL, ROUTER_ROWS), jnp.float32)
        wr = wr.at[:, 0:N_GROUPS].set(w_router_group[i])
        wr = wr.at[:, EXPERTS_PER_GROUP:EXPERTS_PER_GROUP + N_EXPERTS].set(w_router_expert[i])
        br = jnp.zeros((ROUTER_ROWS, 1), jnp.float32)
        br = br.at[0:N_GROUPS, 0].set(b_router_group[i])
        br = br.at[EXPERTS_PER_GROUP:EXPERTS_PER_GROUP + N_EXPERTS, 0].set(b_router_expert[i])

        x1, h2, route_i, route_c = _mix(
            oa, ob, xc, g_grp_a[i][None, :], g_grp_b[i][None, :], w_out[i].astype(bf), g_ffn[i][None, :], wr, br)

        texp, first, nused, pad_tile, lo, nchunk, dst, p_rows = _dispatch_tables(route_i, n_tok)
        xs = _dispatch(lo, nchunk, dst, pad_tile, nused, route_i, h2, p_rows)
        y = _moe(texp, first, nused, xs, w_expert_gate[i], w_expert_up[i], w_expert_down[i])
        xc = _final(lo, nchunk, dst, y, x1, route_c, p[i].reshape(n_tok, PLE_DIM), g_ple[i][None, :],
                    w_ple_gate[i].astype(bf), w_ple_proj[i].astype(bf), g_final[None, :])
    return xc.reshape(batch, seq, D_MODEL)
```

```python
import functools

import jax
import jax.numpy as jnp
from jax import lax
from jax.experimental import pallas as pl
from jax.experimental.pallas import tpu as pltpu

D_MODEL = 1024
HEAD_DIM = 64
GROUP_WIDTH = 512
N_HEADS = 8
B_KV_HEADS = 2
B_KV_WIDTH = B_KV_HEADS * HEAD_DIM
A_CONFIGS = ((128, 1), (512, 4), (2048, 16))
B_HALF_WINDOW = 128
N_GROUPS = 4
EXPERTS_PER_GROUP = 8
N_EXPERTS = N_GROUPS * EXPERTS_PER_GROUP
D_EXPERT = 512
PLE_DIM = 256
EPS = 1e-6
NEG = -1e30
LOG2E = 1.4426950408889634
LN2 = 0.6931471805599453

LANES = 128
Q_BLOCK = 128
HEADS_PER_PASS = 2
PASS_WIDTH = HEADS_PER_PASS * HEAD_DIM
ROW_TILE = 256
ROUTER_ROWS = 128
VMEM_LIMIT = 48 * 1024 * 1024
ATTN_VMEM_LIMIT = 56 * 1024 * 1024


def _rms(x, g):
    r = lax.rsqrt(jnp.mean(x * x, axis=-1, keepdims=True) + EPS)
    return (x * r) * g


def _dot(a, b):
    return jnp.dot(a, b, preferred_element_type=jnp.float32)


PROJ_TILE = 512


def _proj_kernel(x_ref, g_ref, w_ref, *refs):
    a_refs, (qb_ref, kb_ref, vb_ref, res) = refs[:9], refs[9:]
    h = _rms(x_ref[...], g_ref[...]).astype(jnp.bfloat16)
    scale = HEAD_DIM ** -0.5 * LOG2E
    W = GROUP_WIDTH
    tiles = GROUP_WIDTH // PASS_WIDTH
    qkv_a = (_dot(h, w_ref[:, 0:W]) * scale, _dot(h, w_ref[:, W:2 * W]), _dot(h, w_ref[:, 2 * W:3 * W]))
    for part in range(3):
        for pair in range(tiles):
            res[part * tiles + pair] = qkv_a[part][:, pair * PASS_WIDTH:(pair + 1) * PASS_WIDTH]
    qb_ref[...] = (_dot(h, w_ref[:, 3 * W:4 * W]) * scale).astype(jnp.bfloat16)
    kb_ref[...] = _dot(h, w_ref[:, 4 * W:4 * W + B_KV_WIDTH]).astype(jnp.bfloat16)
    vb_ref[...] = _dot(h, w_ref[:, 4 * W + B_KV_WIDTH:4 * W + 2 * B_KV_WIDTH]).astype(jnp.bfloat16)
    for c, (_, dil) in enumerate(A_CONFIGS):
        rows = PROJ_TILE // dil
        for part in range(3):
            out = a_refs[3 * c + part]
            for pair in range(tiles):
                tile = res.at[part * tiles + pair]
                for rho in range(dil):
                    out[pair, rho] = tile[pl.ds(rho, rows, stride=dil), :].astype(jnp.bfloat16)


def _proj(x2d, g_mix, w_in_bf, batch, seq):
    t = x2d.shape[0]
    tm = PROJ_TILE
    tiles_per_seq = seq // tm
    in_width = w_in_bf.shape[1]
    n_pairs = N_HEADS // HEADS_PER_PASS
    row = lambda w: pl.BlockSpec((tm, w), lambda i: (i, 0))
    a_shapes, a_specs = [], []
    for _, dil in A_CONFIGS:
        for _ in range(3):
            a_shapes.append(jax.ShapeDtypeStruct((batch, n_pairs, dil, seq // dil, PASS_WIDTH),
                                                 jnp.bfloat16))
            a_specs.append(pl.BlockSpec((None, n_pairs, dil, tm // dil, PASS_WIDTH),
                                        lambda i: (i // tiles_per_seq, 0, 0, i % tiles_per_seq, 0)))
    wide = jax.ShapeDtypeStruct((t, GROUP_WIDTH), jnp.bfloat16)
    narrow = jax.ShapeDtypeStruct((t, B_KV_WIDTH), jnp.bfloat16)
    return pl.pallas_call(
        _proj_kernel,
        out_shape=tuple(a_shapes) + (wide, narrow, narrow),
        grid=(t // tm,),
        in_specs=[row(D_MODEL),
                  pl.BlockSpec((1, D_MODEL), lambda i: (0, 0)),
                  pl.BlockSpec((D_MODEL, in_width), lambda i: (0, 0))],
        out_specs=tuple(a_specs) + (row(GROUP_WIDTH), row(B_KV_WIDTH), row(B_KV_WIDTH)),
        scratch_shapes=[pltpu.VMEM((3 * GROUP_WIDTH // PASS_WIDTH, tm, PASS_WIDTH), jnp.float32)],
        compiler_params=pltpu.CompilerParams(dimension_semantics=("arbitrary",),
                                             vmem_limit_bytes=VMEM_LIMIT),
        name="proj",
    )(x2d, g_mix, w_in_bf)


def _fill_bias(bias_ref, half, nkeys, dist_scale):
    i = lax.broadcasted_iota(jnp.int32, (Q_BLOCK, nkeys), 0)
    j = lax.broadcasted_iota(jnp.int32, (Q_BLOCK, nkeys), 1)
    for var, rel0 in enumerate((0, -half, Q_BLOCK - nkeys)):
        d = jnp.abs(j + rel0 - i)
        dist = d.astype(jnp.float32) * dist_scale
        for h in range(N_HEADS):
            slope = 2.0 ** (-(h + 1))
            bias_ref[var, h] = jnp.where(d <= half, -(slope * dist) * LOG2E, NEG)


def _key_window(q0, seq_len, half, nkeys):
    lo = q0 - half
    ks = pl.multiple_of(jnp.clip(lo, 0, seq_len - nkeys), HEAD_DIM)
    var = jnp.where(lo < 0, 0, jnp.where(lo > seq_len - nkeys, 2, 1))
    return ks, var


def _fold_lane_tiles(a, op):
    out = a[..., 0:LANES]
    for t in range(1, a.shape[-1] // LANES):
        out = op(out, a[..., t * LANES:(t + 1) * LANES])
    return out


def _attend_pass(q, k, v, bias, sinks):
    nkeys = k.shape[0]
    lane_head = lax.broadcasted_iota(jnp.int32, (Q_BLOCK, PASS_WIDTH), 1) // HEAD_DIM
    zero = jnp.zeros_like(q)
    q_stack = jnp.concatenate(
        [jnp.where(lane_head == h, q, zero) for h in range(HEADS_PER_PASS)], axis=0)
    s = lax.dot_general(q_stack, k, (((1,), (1,)), ((), ())), preferred_element_type=jnp.float32)
    s = s.reshape(HEADS_PER_PASS, Q_BLOCK, nkeys) + bias
    m_tile = _fold_lane_tiles(s, jnp.maximum)
    if sinks is not None:
        tile_head = lax.broadcasted_iota(jnp.int32, m_tile.shape, 0)
        tile_lane = lax.broadcasted_iota(jnp.int32, m_tile.shape, 2)
        sk = jnp.zeros(m_tile.shape, jnp.float32)
        for h in range(HEADS_PER_PASS):
            sk = jnp.where(tile_head == h, sinks[h] * LOG2E, sk)
        m_tile = jnp.maximum(m_tile, sk)
    m = jnp.max(m_tile, axis=-1, keepdims=True)
    p = jnp.exp2(s - m)
    den_tile = _fold_lane_tiles(p, jnp.add)
    if sinks is not None:
        den_tile = den_tile + jnp.where(tile_lane == 0, jnp.exp2(sk - m), 0.0)
    den = jnp.sum(den_tile, axis=-1, keepdims=True)
    pv = _dot(p.reshape(HEADS_PER_PASS * Q_BLOCK, nkeys).astype(jnp.bfloat16), v)
    pv = pv.reshape(HEADS_PER_PASS, Q_BLOCK, PASS_WIDTH) * (1.0 / den)
    lse = m * LN2 + jnp.log(den)
    o = jnp.zeros((Q_BLOCK, PASS_WIDTH), jnp.float32)
    lse_lanes = jnp.zeros((Q_BLOCK, PASS_WIDTH), jnp.float32)
    for h in range(HEADS_PER_PASS):
        o = jnp.where(lane_head == h, pv[h], o)
        lse_lanes = jnp.where(lane_head == h, lse[h], lse_lanes)
    return o, lse_lanes


A_HALF = 64
A_KEYS = Q_BLOCK + 2 * A_HALF
PASSES_PER_BODY = 16


def _attn_a_kernel(*refs, seq):
    n_cfg = len(A_CONFIGS)
    qkv = refs[:3 * n_cfg]
    o_ref, bias_ref = refs[3 * n_cfg], refs[3 * n_cfg + 1]
    scratch = refs[3 * n_cfg + 2:]
    pair = pl.program_id(1)

    @pl.when((pl.program_id(0) == 0) & (pair == 0))
    def _():
        for c, (_, dil) in enumerate(A_CONFIGS):
            _fill_bias(bias_ref.at[c], A_HALF, A_KEYS, float(dil))

    for c, (window, dil) in enumerate(A_CONFIGS):
        assert window // (2 * dil) == A_HALF
        q_ref, k_ref, v_ref = qkv[3 * c:3 * c + 3]
        o_sc, l_sc = scratch[2 * c], scratch[2 * c + 1]
        sub_len = seq // dil
        blocks = sub_len // Q_BLOCK

        def body(idx, carry, q_ref=q_ref, k_ref=k_ref, v_ref=v_ref, o_sc=o_sc, l_sc=l_sc, c=c,
                 sub_len=sub_len, blocks=blocks):
            rho = idx // blocks
            q0 = pl.multiple_of((idx % blocks) * Q_BLOCK, Q_BLOCK)
            ks, var = _key_window(q0, sub_len, A_HALF, A_KEYS)
            bias = bias_ref[c, var, pl.ds(pair * HEADS_PER_PASS, HEADS_PER_PASS)]
            o, lse = _attend_pass(q_ref[rho, pl.ds(q0, Q_BLOCK), :], k_ref[rho, pl.ds(ks, A_KEYS), :],
                                  v_ref[rho, pl.ds(ks, A_KEYS), :], bias, None)
            o_sc[rho, pl.ds(q0, Q_BLOCK), :] = o
            l_sc[rho, pl.ds(q0, Q_BLOCK), :] = lse
            return carry

        lax.fori_loop(0, dil * blocks, body, 0, unroll=PASSES_PER_BODY)

    widest = A_CONFIGS[-1][1]
    rows = seq // widest
    for r in range(widest):
        outs, lses = [], []
        for c, (_, dil) in enumerate(A_CONFIGS):
            step = widest // dil
            idx = (r % dil, pl.ds(r // dil, rows, stride=step) if step > 1 else pl.ds(0, rows))
            outs.append(scratch[2 * c].at[idx[0]][idx[1], :])
            lses.append(scratch[2 * c + 1].at[idx[0]][idx[1], :])
        mx = functools.reduce(jnp.maximum, lses)
        es = [jnp.exp(l - mx) for l in lses]
        inv = 1.0 / functools.reduce(jnp.add, es)
        merged = functools.reduce(jnp.add, [(e * inv) * o for e, o in zip(es, outs)])
        o_ref[pl.ds(r, rows, stride=widest), :] = merged


def _attn_a(qkv, batch, seq):
    n_pairs = N_HEADS // HEADS_PER_PASS
    in_specs, scratch = [], [pltpu.VMEM((len(A_CONFIGS), 3, N_HEADS, Q_BLOCK, A_KEYS), jnp.float32)]
    for _, dil in A_CONFIGS:
        blk = (None, None, dil, seq // dil, PASS_WIDTH)
        in_specs += [pl.BlockSpec(blk, lambda b, p: (b, p, 0, 0, 0))] * 3
        scratch += [pltpu.VMEM((dil, seq // dil, PASS_WIDTH), jnp.float32)] * 2
    return pl.pallas_call(
        functools.partial(_attn_a_kernel, seq=seq),
        out_shape=jax.ShapeDtypeStruct((batch, seq, GROUP_WIDTH), jnp.float32),
        grid=(batch, n_pairs),
        in_specs=in_specs,
        out_specs=pl.BlockSpec((None, seq, PASS_WIDTH), lambda b, p: (b, 0, p)),
        scratch_shapes=scratch,
        compiler_params=pltpu.CompilerParams(dimension_semantics=("arbitrary", "arbitrary"),
                                             vmem_limit_bytes=ATTN_VMEM_LIMIT),
        name="attn_a",
    )(*qkv).reshape(batch * seq, GROUP_WIDTH)


def _attn_b_kernel(sink_ref, q_ref, k_ref, v_ref, o_ref, bias_ref, k4_ref, v4_ref, *, half, nkeys,
                   seq_len, q_tile):
    first = (pl.program_id(0) == 0) & (pl.program_id(1) == 0)

    @pl.when(first)
    def _():
        _fill_bias(bias_ref, half, nkeys, 1.0)

    @pl.when(pl.program_id(1) == 0)
    def _():
        c = lax.broadcasted_iota(jnp.int32, (B_KV_WIDTH, GROUP_WIDTH), 0)
        j = lax.broadcasted_iota(jnp.int32, (B_KV_WIDTH, GROUP_WIDTH), 1)
        src = (j // (GROUP_WIDTH // B_KV_HEADS)) * HEAD_DIM + j % HEAD_DIM
        rep = jnp.where(c == src, 1.0, 0.0).astype(jnp.bfloat16)
        chunk = 512

        def body(i, carry):
            rows = pl.ds(pl.multiple_of(i * chunk, chunk), chunk)
            k4_ref[rows, :] = _dot(k_ref[rows, :], rep).astype(jnp.bfloat16)
            v4_ref[rows, :] = _dot(v_ref[rows, :], rep).astype(jnp.bfloat16)
            return carry

        lax.fori_loop(0, seq_len // chunk, body, 0)

    base = pl.program_id(1) * q_tile

    def body(jb, carry):
        row0 = pl.multiple_of(jb * Q_BLOCK, Q_BLOCK)
        ks, var = _key_window(base + row0, seq_len, half, nkeys)
        for g in range(N_HEADS // HEADS_PER_PASS):
            cols = slice(g * PASS_WIDTH, (g + 1) * PASS_WIDTH)
            heads = pl.ds(g * HEADS_PER_PASS, HEADS_PER_PASS)
            sinks = [sink_ref[g * HEADS_PER_PASS + h] for h in range(HEADS_PER_PASS)]
            o, _ = _attend_pass(q_ref[pl.ds(row0, Q_BLOCK), cols], k4_ref[pl.ds(ks, nkeys), cols],
                                v4_ref[pl.ds(ks, nkeys), cols], bias_ref[var, heads], sinks)
            o_ref[pl.ds(row0, Q_BLOCK), cols] = o
        return carry

    lax.fori_loop(0, q_tile // Q_BLOCK, body, 0,
                  unroll=PASSES_PER_BODY // (N_HEADS // HEADS_PER_PASS))


def _attn_b(q, k, v, sink, batch, seq):
    half = B_HALF_WINDOW
    nkeys = Q_BLOCK + 2 * half
    q_tile = 1024
    view = lambda a: a.reshape(batch, seq, a.shape[-1])
    kern = functools.partial(_attn_b_kernel, half=half, nkeys=nkeys, seq_len=seq, q_tile=q_tile)
    o = pl.pallas_call(
        kern,
        out_shape=jax.ShapeDtypeStruct((batch, seq, GROUP_WIDTH), jnp.float32),
        grid=(batch, seq // q_tile),
        in_specs=[pl.BlockSpec(memory_space=pltpu.SMEM),
                  pl.BlockSpec((None, q_tile, GROUP_WIDTH), lambda b, i: (b, i, 0)),
                  pl.BlockSpec((None, seq, B_KV_WIDTH), lambda b, i: (b, 0, 0)),
                  pl.BlockSpec((None, seq, B_KV_WIDTH), lambda b, i: (b, 0, 0))],
        out_specs=pl.BlockSpec((None, q_tile, GROUP_WIDTH), lambda b, i: (b, i, 0)),
        scratch_shapes=[pltpu.VMEM((3, N_HEADS, Q_BLOCK, nkeys), jnp.float32),
                        pltpu.VMEM((seq, GROUP_WIDTH), jnp.bfloat16),
                        pltpu.VMEM((seq, GROUP_WIDTH), jnp.bfloat16)],
        compiler_params=pltpu.CompilerParams(dimension_semantics=("arbitrary", "arbitrary"),
                                             vmem_limit_bytes=VMEM_LIMIT),
        name="attn_b",
    )(sink, view(q), view(k), view(v))
    return o.reshape(batch * seq, GROUP_WIDTH)


def _mix_kernel(oa_ref, ob_ref, x_ref, ga_ref, gb_ref, wout_ref, gffn_ref, wr_ref, br_ref,
                x1_ref, h2_ref, ri_ref, rc_ref):
    tm = x_ref.shape[0]
    na = _rms(oa_ref[...], ga_ref[...]).astype(jnp.bfloat16)
    nb = _rms(ob_ref[...], gb_ref[...]).astype(jnp.bfloat16)
    x1 = (x_ref[...] + _dot(na, wout_ref[0:GROUP_WIDTH, :])
          + _dot(nb, wout_ref[GROUP_WIDTH:2 * GROUP_WIDTH, :]))
    x1_ref[...] = x1
    h2 = _rms(x1, gffn_ref[...])
    h2_ref[...] = h2.astype(jnp.bfloat16)

    h_hi = h2.astype(jnp.bfloat16)
    h_lo = (h2 - h_hi.astype(jnp.float32)).astype(jnp.bfloat16)
    wr = wr_ref[...]
    w_hi = wr.astype(jnp.bfloat16)
    w_lo = (wr - w_hi.astype(jnp.float32)).astype(jnp.bfloat16)
    lg = _dot(h_hi, w_hi) + _dot(h_hi, w_lo) + _dot(h_lo, w_hi) + _dot(h_lo, w_lo)
    lgt = lg.T + br_ref[...]

    rows = lax.broadcasted_iota(jnp.int32, (EXPERTS_PER_GROUP, tm), 0)
    big = jnp.int32(1 << 20)
    glog = jnp.where(rows < N_GROUPS, lgt[0:EXPERTS_PER_GROUP], -jnp.inf)
    gmax = jnp.max(glog, axis=0, keepdims=True)
    gsel = jnp.min(jnp.where(glog == gmax, rows, big), axis=0, keepdims=True)
    gw = 1.0 / jnp.sum(jnp.exp(glog - gmax), axis=0, keepdims=True)
    esel = jnp.zeros((EXPERTS_PER_GROUP, tm), jnp.float32)
    for grp in range(N_GROUPS):
        lo = EXPERTS_PER_GROUP * (grp + 1)
        esel = jnp.where(gsel == grp, lgt[lo:lo + EXPERTS_PER_GROUP], esel)
    v0 = jnp.max(esel, axis=0, keepdims=True)
    i0 = jnp.min(jnp.where(esel == v0, rows, big), axis=0, keepdims=True)
    rest = jnp.where(rows == i0, -jnp.inf, esel)
    v1 = jnp.max(rest, axis=0, keepdims=True)
    i1 = jnp.min(jnp.where(rest == v1, rows, big), axis=0, keepdims=True)
    e = jnp.exp(v1 - v0)
    w0 = (1.0 / (1.0 + e)) * gw
    w1 = (e / (1.0 + e)) * gw
    eid0 = gsel * EXPERTS_PER_GROUP + i0
    eid1 = gsel * EXPERTS_PER_GROUP + i1
    ri_ref[...] = jnp.where(rows == 0, eid0, jnp.where(rows == 1, eid1, 0))
    rows_t = lax.broadcasted_iota(jnp.int32, (ROUTER_ROWS, tm), 0)
    slab = jnp.where(rows_t == 0, w0, jnp.where(rows_t == 1, w1, 0.0))
    slab = jnp.where(rows_t == 2, eid0.astype(jnp.float32),
                     jnp.where(rows_t == 3, eid1.astype(jnp.float32), slab))
    rc_ref[...] = slab.T[:, 0:EXPERTS_PER_GROUP]


def _mix(oa, ob, x2d, g_a, g_b, w_out_bf, g_ffn, wr, br):
    t = x2d.shape[0]
    tm = 512
    row = lambda w: pl.BlockSpec((tm, w), lambda i: (i, 0))
    full = lambda a: pl.BlockSpec(a.shape, lambda i: (0, 0))
    return pl.pallas_call(
        _mix_kernel,
        out_shape=(jax.ShapeDtypeStruct((t, D_MODEL), jnp.float32),
                   jax.ShapeDtypeStruct((t, D_MODEL), jnp.bfloat16),
                   jax.ShapeDtypeStruct((EXPERTS_PER_GROUP, t), jnp.int32),
                   jax.ShapeDtypeStruct((t, EXPERTS_PER_GROUP), jnp.float32)),
        grid=(t // tm,),
        in_specs=[row(GROUP_WIDTH), row(GROUP_WIDTH), row(D_MODEL),
                  full(g_a), full(g_b), full(w_out_bf), full(g_ffn), full(wr), full(br)],
        out_specs=(row(D_MODEL), row(D_MODEL),
                   pl.BlockSpec((EXPERTS_PER_GROUP, tm), lambda i: (0, i)),
                   row(EXPERTS_PER_GROUP)),
        compiler_params=pltpu.CompilerParams(dimension_semantics=("arbitrary",),
                                             vmem_limit_bytes=VMEM_LIMIT),
        name="mix_router",
    )(oa, ob, x2d, g_a, g_b, w_out_bf, g_ffn, wr, br)


TOKEN_TILE = 512
RUN_ALIGN = 8
LOCAL_ROWS = 2 * TOKEN_TILE + N_EXPERTS * RUN_ALIGN
MAX_CHUNKS = LOCAL_ROWS // RUN_ALIGN


def _local_slots(member, lo_vec, prior, e_sel, axis):
    ids = lax.broadcasted_iota(jnp.int32, member.shape, axis)
    slot = jnp.where(ids == e_sel, lo_vec + prior, 0.0)
    return jnp.sum(slot, axis=axis, keepdims=True).astype(jnp.int32)


def _chunk_copy(dst_ref, tile, c, local, glob, sem, to_global):
    g = pl.multiple_of(dst_ref[tile * MAX_CHUNKS + c], RUN_ALIGN)
    l = pl.multiple_of(c * RUN_ALIGN, RUN_ALIGN)
    src, dst = local.at[pl.ds(l, RUN_ALIGN), :], glob.at[pl.ds(g, RUN_ALIGN), :]
    if not to_global:
        src, dst = dst, src
    return pltpu.make_async_copy(src, dst, sem)


def _start_chunks(nchunk_ref, dst_ref, tile, local, glob, sem, to_global):
    def body(c, carry):
        _chunk_copy(dst_ref, tile, c, local, glob, sem, to_global).start()
        return carry
    lax.fori_loop(0, nchunk_ref[tile], body, 0)


def _wait_chunks(nchunk_ref, dst_ref, tile, local, glob, sem, to_global):
    def body(c, carry):
        _chunk_copy(dst_ref, tile, c, local, glob, sem, to_global).wait()
        return carry
    lax.fori_loop(0, nchunk_ref[tile], body, 0)


def _lo_vector(lo_ref, tile, shape, axis):
    ids = lax.broadcasted_iota(jnp.int32, shape, axis)
    vec = jnp.zeros(shape, jnp.float32)
    for e in range(N_EXPERTS):
        vec = jnp.where(ids == e, lo_ref[tile * N_EXPERTS + e].astype(jnp.float32), vec)
    return vec


def _dispatch_kernel(lo_ref, nchunk_ref, dst_ref, pad_tile_ref, nused_ref, ri_ref, h_ref, xs_hbm,
                     upper, sbuf, zbuf, sem, zsem, *, n_tiles):
    i = pl.program_id(0)
    n_steps = pl.num_programs(0)
    slot = i % 2

    def zero_tile(tile):
        rows = pl.ds(pl.multiple_of(tile * ROW_TILE, ROW_TILE), ROW_TILE)
        return pltpu.make_async_copy(zbuf, xs_hbm.at[rows, :], zsem)

    @pl.when(i == 0)
    def _():
        r = lax.broadcasted_iota(jnp.int32, (TOKEN_TILE, TOKEN_TILE), 0)
        c = lax.broadcasted_iota(jnp.int32, (TOKEN_TILE, TOKEN_TILE), 1)
        upper[...] = jnp.where(r < c, 1.0, 0.0).astype(jnp.bfloat16)
        zbuf[...] = jnp.zeros_like(zbuf)
        for e in range(N_EXPERTS):
            @pl.when(pad_tile_ref[e] >= 0)
            def _():
                zero_tile(pad_tile_ref[e]).start()
        lax.fori_loop(nused_ref[0], n_tiles, lambda t, c: (zero_tile(t).start(), c)[1], 0)
        for e in range(N_EXPERTS):
            @pl.when(pad_tile_ref[e] >= 0)
            def _():
                zero_tile(pad_tile_ref[e]).wait()
        lax.fori_loop(nused_ref[0], n_tiles, lambda t, c: (zero_tile(t).wait(), c)[1], 0)

    e0 = ri_ref[0:1, :]
    e1 = ri_ref[1:2, :]
    ids = lax.broadcasted_iota(jnp.int32, (N_EXPERTS, TOKEN_TILE), 0)
    member = jnp.where((ids == e0) | (ids == e1), 1.0, 0.0)
    prior = _dot(member.astype(jnp.bfloat16), upper[...])
    lo_vec = _lo_vector(lo_ref, i, (N_EXPERTS, 1), 0)
    d0 = _local_slots(member, lo_vec, prior, e0, 0)
    d1 = _local_slots(member, lo_vec, prior, e1, 0)
    rows = lax.broadcasted_iota(jnp.int32, (LOCAL_ROWS, TOKEN_TILE), 0)
    select = jnp.where((rows == d0) | (rows == d1), 1.0, 0.0).astype(jnp.bfloat16)
    sorted_rows = _dot(select, h_ref[...])

    for s in range(2):
        @pl.when(slot == s)
        def _():
            sbuf[s] = sorted_rows

            @pl.when(i > 0)
            def _():
                _wait_chunks(nchunk_ref, dst_ref, i - 1, sbuf.at[1 - s], xs_hbm, sem, True)

            _start_chunks(nchunk_ref, dst_ref, i, sbuf.at[s], xs_hbm, sem, True)

            @pl.when(i == n_steps - 1)
            def _():
                _wait_chunks(nchunk_ref, dst_ref, i, sbuf.at[s], xs_hbm, sem, True)


def _dispatch(lo, nchunk, dst, pad_tile, nused, route_i, h2, p_rows):
    n_tok = h2.shape[0]
    grid_spec = pltpu.PrefetchScalarGridSpec(
        num_scalar_prefetch=5,
        grid=(n_tok // TOKEN_TILE,),
        in_specs=[pl.BlockSpec((EXPERTS_PER_GROUP, TOKEN_TILE), lambda i, *_: (0, i)),
                  pl.BlockSpec((TOKEN_TILE, D_MODEL), lambda i, *_: (i, 0))],
        out_specs=pl.BlockSpec(memory_space=pl.ANY),
        scratch_shapes=[pltpu.VMEM((TOKEN_TILE, TOKEN_TILE), jnp.bfloat16),
                        pltpu.VMEM((2, LOCAL_ROWS, D_MODEL), jnp.float32),
                        pltpu.VMEM((ROW_TILE, D_MODEL), jnp.float32),
                        pltpu.SemaphoreType.DMA, pltpu.SemaphoreType.DMA],
    )
    return pl.pallas_call(
        functools.partial(_dispatch_kernel, n_tiles=p_rows // ROW_TILE),
        out_shape=jax.ShapeDtypeStruct((p_rows, D_MODEL), jnp.float32),
        grid_spec=grid_spec,
        compiler_params=pltpu.CompilerParams(dimension_semantics=("arbitrary",),
                                             vmem_limit_bytes=VMEM_LIMIT),
        name="dispatch",
    )(lo, nchunk, dst, pad_tile, nused, route_i, h2)


def _moe_kernel(texp_ref, first_ref, nused_ref, xs_ref, wg_ref, wu_ref, wd_ref, y_ref,
                wg_bf, wu_bf, wd_bf):
    j = pl.program_id(0)

    @pl.when(j < nused_ref[0])
    def _():
        @pl.when(first_ref[j] == 1)
        def _():
            wg_bf[...] = wg_ref[...].astype(jnp.bfloat16)
            wu_bf[...] = wu_ref[...].astype(jnp.bfloat16)
            wd_bf[...] = wd_ref[...].astype(jnp.bfloat16)

        x = xs_ref[...].astype(jnp.bfloat16)
        a = _dot(x, wg_bf[...])
        u = _dot(x, wu_bf[...])
        hid = (a * (1.0 / (1.0 + jnp.exp(-a)))) * u
        y_ref[...] = _dot(hid.astype(jnp.bfloat16), wd_bf[...])

    @pl.when(j >= nused_ref[0])
    def _():
        y_ref[...] = jnp.zeros_like(y_ref)


def _moe(texp, first, nused, xs, w_gate, w_up, w_down):
    p_rows = xs.shape[0]
    n_tiles = p_rows // ROW_TILE
    used = lambda j, nu: jnp.minimum(j, nu[0] - 1)
    grid_spec = pltpu.PrefetchScalarGridSpec(
        num_scalar_prefetch=3,
        grid=(n_tiles,),
        in_specs=[
            pl.BlockSpec((ROW_TILE, D_MODEL), lambda j, te, fi, nu: (used(j, nu), 0)),
            pl.BlockSpec((None, D_MODEL, D_EXPERT), lambda j, te, fi, nu: (te[j], 0, 0)),
            pl.BlockSpec((None, D_MODEL, D_EXPERT), lambda j, te, fi, nu: (te[j], 0, 0)),
            pl.BlockSpec((None, D_EXPERT, D_MODEL), lambda j, te, fi, nu: (te[j], 0, 0)),
        ],
        out_specs=pl.BlockSpec((ROW_TILE, D_MODEL), lambda j, te, fi, nu: (j, 0)),
        scratch_shapes=[pltpu.VMEM((D_MODEL, D_EXPERT), jnp.bfloat16),
                        pltpu.VMEM((D_MODEL, D_EXPERT), jnp.bfloat16),
                        pltpu.VMEM((D_EXPERT, D_MODEL), jnp.bfloat16)],
    )
    return pl.pallas_call(
        _moe_kernel,
        out_shape=jax.ShapeDtypeStruct((p_rows, D_MODEL), jnp.float32),
        grid_spec=grid_spec,
        compiler_params=pltpu.CompilerParams(dimension_semantics=("arbitrary",),
                                             vmem_limit_bytes=VMEM_LIMIT),
        name="moe_experts",
    )(texp, first, nused, xs, w_gate, w_up, w_down)


def _final_kernel(lo_ref, nchunk_ref, dst_ref, y_hbm, x1_ref, rc_ref, p_ref, gple_ref, wg_ref, wp_ref,
                  gfin_ref, out_ref, lower, ybuf, sems):
    i = pl.program_id(0)
    n_steps = pl.num_programs(0)
    slot = i % 2

    @pl.when(i == 0)
    def _():
        r = lax.broadcasted_iota(jnp.int32, (TOKEN_TILE, TOKEN_TILE), 0)
        c = lax.broadcasted_iota(jnp.int32, (TOKEN_TILE, TOKEN_TILE), 1)
        lower[...] = jnp.where(c < r, 1.0, 0.0).astype(jnp.bfloat16)
        ybuf[...] = jnp.zeros_like(ybuf)
        _start_chunks(nchunk_ref, dst_ref, 0, ybuf.at[0], y_hbm, sems.at[0], False)

    for s in range(2):
        @pl.when((slot == s) & (i + 1 < n_steps))
        def _():
            _start_chunks(nchunk_ref, dst_ref, i + 1, ybuf.at[1 - s], y_hbm, sems.at[1 - s], False)

    pp = _dot(p_ref[...].astype(jnp.bfloat16), wp_ref[...])
    rc = rc_ref[...]
    w0, w1 = rc[:, 0:1], rc[:, 1:2]
    e0, e1 = rc[:, 2:3].astype(jnp.int32), rc[:, 3:4].astype(jnp.int32)
    ids = lax.broadcasted_iota(jnp.int32, (TOKEN_TILE, ROUTER_ROWS), 1)
    member = jnp.where((ids == e0) | (ids == e1), 1.0, 0.0)
    prior = _dot(lower[...], member.astype(jnp.bfloat16))
    lo_vec = _lo_vector(lo_ref, i, (1, ROUTER_ROWS), 1)
    d0 = _local_slots(member, lo_vec, prior, e0, 1)
    d1 = _local_slots(member, lo_vec, prior, e1, 1)
    cols = lax.broadcasted_iota(jnp.int32, (TOKEN_TILE, LOCAL_ROWS), 1)
    pick0 = jnp.where(cols == d0, 1.0, 0.0).astype(jnp.bfloat16)
    pick1 = jnp.where(cols == d1, 1.0, 0.0).astype(jnp.bfloat16)

    for s in range(2):
        @pl.when(slot == s)
        def _():
            _wait_chunks(nchunk_ref, dst_ref, i, ybuf.at[s], y_hbm, sems.at[s], False)
            yb = ybuf[s].astype(jnp.bfloat16)
            x2 = x1_ref[...] + (w0 * _dot(pick0, yb) + w1 * _dot(pick1, yb))
            z = _dot(_rms(x2, gple_ref[...]).astype(jnp.bfloat16), wg_ref[...])
            gate = 1.0 / (1.0 + jnp.exp(-z))
            x3 = x2 + pp * gate
            out_ref[...] = _rms(x3, gfin_ref[...])


def _final(lo, nchunk, dst, y, x1, rc, p2d, g_ple, w_gate_bf, w_proj_bf, g_final):
    t = x1.shape[0]
    tm = TOKEN_TILE
    row = lambda w: pl.BlockSpec((tm, w), lambda i, *_: (i, 0))
    full = lambda a: pl.BlockSpec(a.shape, lambda i, *_: (0, 0))
    grid_spec = pltpu.PrefetchScalarGridSpec(
        num_scalar_prefetch=3,
        grid=(t // tm,),
        in_specs=[pl.BlockSpec(memory_space=pl.ANY), row(D_MODEL), row(EXPERTS_PER_GROUP),
                  row(PLE_DIM), full(g_ple), full(w_gate_bf), full(w_proj_bf), full(g_final)],
        out_specs=row(D_MODEL),
        scratch_shapes=[pltpu.VMEM((TOKEN_TILE, TOKEN_TILE), jnp.bfloat16),
                        pltpu.VMEM((2, LOCAL_ROWS, D_MODEL), jnp.float32),
                        pltpu.SemaphoreType.DMA((2,))],
    )
    return pl.pallas_call(
        _final_kernel,
        out_shape=jax.ShapeDtypeStruct((t, D_MODEL), jnp.float32),
        grid_spec=grid_spec,
        compiler_params=pltpu.CompilerParams(dimension_semantics=("arbitrary",),
                                             vmem_limit_bytes=VMEM_LIMIT),
        name="combine_ple_final",
    )(lo, nchunk, dst, y, x1, rc, p2d, g_ple, w_gate_bf, w_proj_bf, g_final)


def _dispatch_tables(route_i, n_tok):
    n_tt = n_tok // TOKEN_TILE
    p_rows = 2 * n_tok + n_tt * N_EXPERTS * (RUN_ALIGN - 1) + N_EXPERTS * ROW_TILE
    p_rows = -(-p_rows // ROW_TILE) * ROW_TILE
    n_tiles = p_rows // ROW_TILE
    e = route_i[:2].reshape(2, n_tt, TOKEN_TILE)
    onehot = (e[..., None] == jnp.arange(N_EXPERTS, dtype=jnp.int32)).astype(jnp.int32)
    counts = jnp.sum(onehot, axis=(0, 2))
    n8 = (counts + RUN_ALIGN - 1) // RUN_ALIGN * RUN_ALIGN
    lo = jnp.cumsum(n8, axis=1) - n8
    total = jnp.sum(n8, axis=0)
    tiles_per = (total + ROW_TILE - 1) // ROW_TILE
    tile_end = jnp.cumsum(tiles_per)
    tile_start = tile_end - tiles_per
    g = tile_start[None, :] * ROW_TILE + (jnp.cumsum(n8, axis=0) - n8)
    n_used = tile_end[-1]
    tile_ids = jnp.arange(n_tiles, dtype=jnp.int32)
    clamped = jnp.minimum(tile_ids, n_used - 1)
    texp = jnp.sum((tile_end[None, :] <= clamped[:, None]).astype(jnp.int32), axis=1)
    first = ((tile_ids == tile_start[texp]) & (tile_ids < n_used)).astype(jnp.int32)
    pad_tile = jnp.where(total > 0, tile_end - 1, -1).astype(jnp.int32)
    chunk_row = jnp.arange(MAX_CHUNKS, dtype=jnp.int32)[None, :, None] * RUN_ALIGN
    in_run = (chunk_row >= lo[:, None, :]) & (chunk_row < (lo + n8)[:, None, :])
    dst = jnp.sum(jnp.where(in_run, g[:, None, :] + chunk_row - lo[:, None, :], 0), axis=2)
    nchunk = jnp.sum(n8, axis=1) // RUN_ALIGN
    i32 = lambda a: a.reshape(-1).astype(jnp.int32)
    return (texp, first, n_used.reshape(1).astype(jnp.int32), pad_tile, i32(lo), i32(nchunk),
            i32(dst), p_rows)


def kernel(x, p, g_mix, w_in, sink, g_grp_a, g_grp_b, w_out, g_ffn, w_router_group, b_router_group,
           w_router_expert, b_router_expert, w_expert_gate, w_expert_up, w_expert_down, g_ple,
           w_ple_gate, w_ple_proj, g_final):
    batch, seq, _ = x.shape
    n_tok = batch * seq
    depth = w_in.shape[0]
    assert depth == 1, "the final RMSNorm is fused into the last layer's epilogue"
    bf = jnp.bfloat16
    xc = x.reshape(n_tok, D_MODEL)
    for i in range(depth):
        *qkv_a, qb, kb, vb = _proj(xc, g_mix[i][None, :], w_in[i].astype(bf), batch, seq)
        oa = _attn_a(qkv_a, batch, seq)
        ob = _attn_b(qb, kb, vb, sink[i], batch, seq)

        wr = jnp.zeros((D_MODEL, ROUTER_ROWS), jnp.float32)
        wr = wr.at[:, 0:N_GROUPS].set(w_router_group[i])
        wr = wr.at[:, EXPERTS_PER_GROUP:EXPERTS_PER_GROUP + N_EXPERTS].set(w_router_expert[i])
        br = jnp.zeros((ROUTER_ROWS, 1), jnp.float32)
        br = br.at[0:N_GROUPS, 0].set(b_router_group[i])
        br = br.at[EXPERTS_PER_GROUP:EXPERTS_PER_GROUP + N_EXPERTS, 0].set(b_router_expert[i])

        x1, h2, route_i, route_c = _mix(
            oa, ob, xc, g_grp_a[i][None, :], g_grp_b[i][None, :], w_out[i].astype(bf), g_ffn[i][None, :], wr, br)

        texp, first, nused, pad_tile, lo, nchunk, dst, p_rows = _dispatch_tables(route_i, n_tok)
        xs = _dispatch(lo, nchunk, dst, pad_tile, nused, route_i, h2, p_rows)
        y = _moe(texp, first, nused, xs, w_expert_gate[i], w_expert_up[i], w_expert_down[i])
        xc = _final(lo, nchunk, dst, y, x1, route_c, p[i].reshape(n_tok, PLE_DIM), g_ple[i][None, :],
                    w_ple_gate[i].astype(bf), w_ple_proj[i].astype(bf), g_final[None, :])
    return xc.reshape(batch, seq, D_MODEL)
```

```python
import functools

import jax
import jax.numpy as jnp
from jax import lax
from jax.experimental import pallas as pl
from jax.experimental.pallas import tpu as pltpu

D_MODEL = 1024
HEAD_DIM = 64
GROUP_WIDTH = 512
N_HEADS = 8
B_KV_HEADS = 2
B_KV_WIDTH = B_KV_HEADS * HEAD_DIM
A_CONFIGS = ((128, 1), (512, 4), (2048, 16))
B_HALF_WINDOW = 128
N_GROUPS = 4
EXPERTS_PER_GROUP = 8
N_EXPERTS = N_GROUPS * EXPERTS_PER_GROUP
D_EXPERT = 512
PLE_DIM = 256
EPS = 1e-6
NEG = -1e30
LOG2E = 1.4426950408889634
LN2 = 0.6931471805599453

LANES = 128
Q_BLOCK = 128
HEADS_PER_PASS = 2
PASS_WIDTH = HEADS_PER_PASS * HEAD_DIM
ROW_TILE = 256
ROUTER_ROWS = 128
VMEM_LIMIT = 48 * 1024 * 1024
ATTN_VMEM_LIMIT = 56 * 1024 * 1024


def _rms(x, g):
    r = lax.rsqrt(jnp.mean(x * x, axis=-1, keepdims=True) + EPS)
    return (x * r) * g


def _dot(a, b):
    return jnp.dot(a, b, preferred_element_type=jnp.float32)


PROJ_TILE = 512


def _proj_kernel(x_ref, g_ref, w_ref, *refs):
    a_refs, (qb_ref, kb_ref, vb_ref, res) = refs[:9], refs[9:]
    h = _rms(x_ref[...], g_ref[...]).astype(jnp.bfloat16)
    scale = HEAD_DIM ** -0.5 * LOG2E
    W = GROUP_WIDTH
    tiles = GROUP_WIDTH // PASS_WIDTH
    qkv_a = (_dot(h, w_ref[:, 0:W]) * scale, _dot(h, w_ref[:, W:2 * W]), _dot(h, w_ref[:, 2 * W:3 * W]))
    for part in range(3):
        for pair in range(tiles):
            res[part * tiles + pair] = qkv_a[part][:, pair * PASS_WIDTH:(pair + 1) * PASS_WIDTH]
    qb_ref[...] = (_dot(h, w_ref[:, 3 * W:4 * W]) * scale).astype(jnp.bfloat16)
    kb_ref[...] = _dot(h, w_ref[:, 4 * W:4 * W + B_KV_WIDTH]).astype(jnp.bfloat16)
    vb_ref[...] = _dot(h, w_ref[:, 4 * W + B_KV_WIDTH:4 * W + 2 * B_KV_WIDTH]).astype(jnp.bfloat16)
    for c, (_, dil) in enumerate(A_CONFIGS):
        rows = PROJ_TILE // dil
        for part in range(3):
            out = a_refs[3 * c + part]
            for pair in range(tiles):
                tile = res.at[part * tiles + pair]
                for rho in range(dil):
                    out[pair, rho] = tile[pl.ds(rho, rows, stride=dil), :].astype(jnp.bfloat16)


def _proj(x2d, g_mix, w_in_bf, batch, seq):
    t = x2d.shape[0]
    tm = PROJ_TILE
    tiles_per_seq = seq // tm
    in_width = w_in_bf.shape[1]
    n_pairs = N_HEADS // HEADS_PER_PASS
    row = lambda w: pl.BlockSpec((tm, w), lambda i: (i, 0))
    a_shapes, a_specs = [], []
    for _, dil in A_CONFIGS:
        for _ in range(3):
            a_shapes.append(jax.ShapeDtypeStruct((batch, n_pairs, dil, seq // dil, PASS_WIDTH),
                                                 jnp.bfloat16))
            a_specs.append(pl.BlockSpec((None, n_pairs, dil, tm // dil, PASS_WIDTH),
                                        lambda i: (i // tiles_per_seq, 0, 0, i % tiles_per_seq, 0)))
    wide = jax.ShapeDtypeStruct((t, GROUP_WIDTH), jnp.bfloat16)
    narrow = jax.ShapeDtypeStruct((t, B_KV_WIDTH), jnp.bfloat16)
    return pl.pallas_call(
        _proj_kernel,
        out_shape=tuple(a_shapes) + (wide, narrow, narrow),
        grid=(t // tm,),
        in_specs=[row(D_MODEL),
                  pl.BlockSpec((1, D_MODEL), lambda i: (0, 0)),
                  pl.BlockSpec((D_MODEL, in_width), lambda i: (0, 0))],
        out_specs=tuple(a_specs) + (row(GROUP_WIDTH), row(B_KV_WIDTH), row(B_KV_WIDTH)),
        scratch_shapes=[pltpu.VMEM((3 * GROUP_WIDTH // PASS_WIDTH, tm, PASS_WIDTH), jnp.float32)],
        compiler_params=pltpu.CompilerParams(dimension_semantics=("arbitrary",),
                                             vmem_limit_bytes=VMEM_LIMIT),
        name="proj",
    )(x2d, g_mix, w_in_bf)


def _fill_bias(bias_ref, half, nkeys, dist_scale):
    i = lax.broadcasted_iota(jnp.int32, (Q_BLOCK, nkeys), 0)
    j = lax.broadcasted_iota(jnp.int32, (Q_BLOCK, nkeys), 1)
    for var, rel0 in enumerate((0, -half, Q_BLOCK - nkeys)):
        d = jnp.abs(j + rel0 - i)
        dist = d.astype(jnp.float32) * dist_scale
        for h in range(N_HEADS):
            slope = 2.0 ** (-(h + 1))
            bias_ref[var, h] = jnp.where(d <= half, -(slope * dist) * LOG2E, NEG)


def _key_window(q0, seq_len, half, nkeys):
    lo = q0 - half
    ks = pl.multiple_of(jnp.clip(lo, 0, seq_len - nkeys), HEAD_DIM)
    var = jnp.where(lo < 0, 0, jnp.where(lo > seq_len - nkeys, 2, 1))
    return ks, var


def _fold_lane_tiles(a, op):
    out = a[..., 0:LANES]
    for t in range(1, a.shape[-1] // LANES):
        out = op(out, a[..., t * LANES:(t + 1) * LANES])
    return out


def _attend_pass(q, k, v, bias, sinks):
    nkeys = k.shape[0]
    lane_head = lax.broadcasted_iota(jnp.int32, (Q_BLOCK, PASS_WIDTH), 1) // HEAD_DIM
    zero = jnp.zeros_like(q)
    q_stack = jnp.concatenate(
        [jnp.where(lane_head == h, q, zero) for h in range(HEADS_PER_PASS)], axis=0)
    s = lax.dot_general(q_stack, k, (((1,), (1,)), ((), ())), preferred_element_type=jnp.float32)
    s = s.reshape(HEADS_PER_PASS, Q_BLOCK, nkeys) + bias
    m_tile = _fold_lane_tiles(s, jnp.maximum)
    if sinks is not None:
        tile_head = lax.broadcasted_iota(jnp.int32, m_tile.shape, 0)
        tile_lane = lax.broadcasted_iota(jnp.int32, m_tile.shape, 2)
        sk = jnp.zeros(m_tile.shape, jnp.float32)
        for h in range(HEADS_PER_PASS):
            sk = jnp.where(tile_head == h, sinks[h] * LOG2E, sk)
        m_tile = jnp.maximum(m_tile, sk)
    m = jnp.max(m_tile, axis=-1, keepdims=True)
    p = jnp.exp2(s - m)
    den_tile = _fold_lane_tiles(p, jnp.add)
    if sinks is not None:
        den_tile = den_tile + jnp.where(tile_lane == 0, jnp.exp2(sk - m), 0.0)
    den = jnp.sum(den_tile, axis=-1, keepdims=True)
    pv = _dot(p.reshape(HEADS_PER_PASS * Q_BLOCK, nkeys).astype(jnp.bfloat16), v)
    pv = pv.reshape(HEADS_PER_PASS, Q_BLOCK, PASS_WIDTH) * (1.0 / den)
    lse = m * LN2 + jnp.log(den)
    o = jnp.zeros((Q_BLOCK, PASS_WIDTH), jnp.float32)
    lse_lanes = jnp.zeros((Q_BLOCK, PASS_WIDTH), jnp.float32)
    for h in range(HEADS_PER_PASS):
        o = jnp.where(lane_head == h, pv[h], o)
        lse_lanes = jnp.where(lane_head == h, lse[h], lse_lanes)
    return o, lse_lanes


A_HALF = 64
A_KEYS = Q_BLOCK + 2 * A_HALF
PASSES_PER_BODY = 16


def _attn_a_kernel(*refs, seq):
    n_cfg = len(A_CONFIGS)
    qkv = refs[:3 * n_cfg]
    o_ref, bias_ref = refs[3 * n_cfg], refs[3 * n_cfg + 1]
    scratch = refs[3 * n_cfg + 2:]
    pair = pl.program_id(1)

    @pl.when((pl.program_id(0) == 0) & (pair == 0))
    def _():
        for c, (_, dil) in enumerate(A_CONFIGS):
            _fill_bias(bias_ref.at[c], A_HALF, A_KEYS, float(dil))

    for c, (window, dil) in enumerate(A_CONFIGS):
        assert window // (2 * dil) == A_HALF
        q_ref, k_ref, v_ref = qkv[3 * c:3 * c + 3]
        o_sc, l_sc = scratch[2 * c], scratch[2 * c + 1]
        sub_len = seq // dil
        blocks = sub_len // Q_BLOCK

        def body(idx, carry, q_ref=q_ref, k_ref=k_ref, v_ref=v_ref, o_sc=o_sc, l_sc=l_sc, c=c,
                 sub_len=sub_len, blocks=blocks):
            rho = idx // blocks
            q0 = pl.multiple_of((idx % blocks) * Q_BLOCK, Q_BLOCK)
            ks, var = _key_window(q0, sub_len, A_HALF, A_KEYS)
            bias = bias_ref[c, var, pl.ds(pair * HEADS_PER_PASS, HEADS_PER_PASS)]
            o, lse = _attend_pass(q_ref[rho, pl.ds(q0, Q_BLOCK), :], k_ref[rho, pl.ds(ks, A_KEYS), :],
                                  v_ref[rho, pl.ds(ks, A_KEYS), :], bias, None)
            o_sc[rho, pl.ds(q0, Q_BLOCK), :] = o
            l_sc[rho, pl.ds(q0, Q_BLOCK), :] = lse
            return carry

        lax.fori_loop(0, dil * blocks, body, 0, unroll=PASSES_PER_BODY)

    widest = A_CONFIGS[-1][1]
    rows = seq // widest
    for r in range(widest):
        outs, lses = [], []
        for c, (_, dil) in enumerate(A_CONFIGS):
            step = widest // dil
            idx = (r % dil, pl.ds(r // dil, rows, stride=step) if step > 1 else pl.ds(0, rows))
            outs.append(scratch[2 * c].at[idx[0]][idx[1], :])
            lses.append(scratch[2 * c + 1].at[idx[0]][idx[1], :])
        mx = functools.reduce(jnp.maximum, lses)
        es = [jnp.exp(l - mx) for l in lses]
        inv = 1.0 / functools.reduce(jnp.add, es)
        merged = functools.reduce(jnp.add, [(e * inv) * o for e, o in zip(es, outs)])
        o_ref[pl.ds(r, rows, stride=widest), :] = merged


def _attn_a(qkv, batch, seq):
    n_pairs = N_HEADS // HEADS_PER_PASS
    in_specs, scratch = [], [pltpu.VMEM((len(A_CONFIGS), 3, N_HEADS, Q_BLOCK, A_KEYS), jnp.float32)]
    for _, dil in A_CONFIGS:
        blk = (None, None, dil, seq // dil, PASS_WIDTH)
        in_specs += [pl.BlockSpec(blk, lambda b, p: (b, p, 0, 0, 0))] * 3
        scratch += [pltpu.VMEM((dil, seq // dil, PASS_WIDTH), jnp.float32)] * 2
    return pl.pallas_call(
        functools.partial(_attn_a_kernel, seq=seq),
        out_shape=jax.ShapeDtypeStruct((batch, seq, GROUP_WIDTH), jnp.float32),
        grid=(batch, n_pairs),
        in_specs=in_specs,
        out_specs=pl.BlockSpec((None, seq, PASS_WIDTH), lambda b, p: (b, 0, p)),
        scratch_shapes=scratch,
        compiler_params=pltpu.CompilerParams(dimension_semantics=("arbitrary", "arbitrary"),
                                             vmem_limit_bytes=ATTN_VMEM_LIMIT),
        name="attn_a",
    )(*qkv).reshape(batch * seq, GROUP_WIDTH)


def _attn_b_kernel(sink_ref, q_ref, k_ref, v_ref, o_ref, bias_ref, k4_ref, v4_ref, *, half, nkeys,
                   seq_len, q_tile):
    first = (pl.program_id(0) == 0) & (pl.program_id(1) == 0)

    @pl.when(first)
    def _():
        _fill_bias(bias_ref, half, nkeys, 1.0)

    @pl.when(pl.program_id(1) == 0)
    def _():
        c = lax.broadcasted_iota(jnp.int32, (B_KV_WIDTH, GROUP_WIDTH), 0)
        j = lax.broadcasted_iota(jnp.int32, (B_KV_WIDTH, GROUP_WIDTH), 1)
        src = (j // (GROUP_WIDTH // B_KV_HEADS)) * HEAD_DIM + j % HEAD_DIM
        rep = jnp.where(c == src, 1.0, 0.0).astype(jnp.bfloat16)
        chunk = 512

        def body(i, carry):
            rows = pl.ds(pl.multiple_of(i * chunk, chunk), chunk)
            k4_ref[rows, :] = _dot(k_ref[rows, :], rep).astype(jnp.bfloat16)
            v4_ref[rows, :] = _dot(v_ref[rows, :], rep).astype(jnp.bfloat16)
            return carry

        lax.fori_loop(0, seq_len // chunk, body, 0)

    base = pl.program_id(1) * q_tile

    def body(jb, carry):
        row0 = pl.multiple_of(jb * Q_BLOCK, Q_BLOCK)
        ks, var = _key_window(base + row0, seq_len, half, nkeys)
        for g in range(N_HEADS // HEADS_PER_PASS):
            cols = slice(g * PASS_WIDTH, (g + 1) * PASS_WIDTH)
            heads = pl.ds(g * HEADS_PER_PASS, HEADS_PER_PASS)
            sinks = [sink_ref[g * HEADS_PER_PASS + h] for h in range(HEADS_PER_PASS)]
            o, _ = _attend_pass(q_ref[pl.ds(row0, Q_BLOCK), cols], k4_ref[pl.ds(ks, nkeys), cols],
                                v4_ref[pl.ds(ks, nkeys), cols], bias_ref[var, heads], sinks)
            o_ref[pl.ds(row0, Q_BLOCK), cols] = o
        return carry

    lax.fori_loop(0, q_tile // Q_BLOCK, body, 0,
                  unroll=PASSES_PER_BODY // (N_HEADS // HEADS_PER_PASS))


def _attn_b(q, k, v, sink, batch, seq):
    half = B_HALF_WINDOW
    nkeys = Q_BLOCK + 2 * half
    q_tile = 1024
    view = lambda a: a.reshape(batch, seq, a.shape[-1])
    kern = functools.partial(_attn_b_kernel, half=half, nkeys=nkeys, seq_len=seq, q_tile=q_tile)
    o = pl.pallas_call(
        kern,
        out_shape=jax.ShapeDtypeStruct((batch, seq, GROUP_WIDTH), jnp.float32),
        grid=(batch, seq // q_tile),
        in_specs=[pl.BlockSpec(memory_space=pltpu.SMEM),
                  pl.BlockSpec((None, q_tile, GROUP_WIDTH), lambda b, i: (b, i, 0)),
                  pl.BlockSpec((None, seq, B_KV_WIDTH), lambda b, i: (b, 0, 0)),
                  pl.BlockSpec((None, seq, B_KV_WIDTH), lambda b, i: (b, 0, 0))],
        out_specs=pl.BlockSpec((None, q_tile, GROUP_WIDTH), lambda b, i: (b, i, 0)),
        scratch_shapes=[pltpu.VMEM((3, N_HEADS, Q_BLOCK, nkeys), jnp.float32),
                        pltpu.VMEM((seq, GROUP_WIDTH), jnp.bfloat16),
                        pltpu.VMEM((seq, GROUP_WIDTH), jnp.bfloat16)],
        compiler_params=pltpu.CompilerParams(dimension_semantics=("arbitrary", "arbitrary"),
                                             vmem_limit_bytes=VMEM_LIMIT),
        name="attn_b",
    )(sink, view(q), view(k), view(v))
    return o.reshape(batch * seq, GROUP_WIDTH)


def _mix_kernel(oa_ref, ob_ref, x_ref, ga_ref, gb_ref, wout_ref, gffn_ref, wr_ref, br_ref,
                x1_ref, h2_ref, ri_ref, rc_ref):
    tm = x_ref.shape[0]
    na = _rms(oa_ref[...], ga_ref[...]).astype(jnp.bfloat16)
    nb = _rms(ob_ref[...], gb_ref[...]).astype(jnp.bfloat16)
    x1 = (x_ref[...] + _dot(na, wout_ref[0:GROUP_WIDTH, :])
          + _dot(nb, wout_ref[GROUP_WIDTH:2 * GROUP_WIDTH, :]))
    x1_ref[...] = x1
    h2 = _rms(x1, gffn_ref[...])
    h2_ref[...] = h2.astype(jnp.bfloat16)

    h_hi = h2.astype(jnp.bfloat16)
    h_lo = (h2 - h_hi.astype(jnp.float32)).astype(jnp.bfloat16)
    wr = wr_ref[...]
    w_hi = wr.astype(jnp.bfloat16)
    w_lo = (wr - w_hi.astype(jnp.float32)).astype(jnp.bfloat16)
    lg = _dot(h_hi, w_hi) + _dot(h_hi, w_lo) + _dot(h_lo, w_hi) + _dot(h_lo, w_lo)
    lgt = lg.T + br_ref[...]

    rows = lax.broadcasted_iota(jnp.int32, (EXPERTS_PER_GROUP, tm), 0)
    big = jnp.int32(1 << 20)
    glog = jnp.where(rows < N_GROUPS, lgt[0:EXPERTS_PER_GROUP], -jnp.inf)
    gmax = jnp.max(glog, axis=0, keepdims=True)
    gsel = jnp.min(jnp.where(glog == gmax, rows, big), axis=0, keepdims=True)
    gw = 1.0 / jnp.sum(jnp.exp(glog - gmax), axis=0, keepdims=True)
    esel = jnp.zeros((EXPERTS_PER_GROUP, tm), jnp.float32)
    for grp in range(N_GROUPS):
        lo = EXPERTS_PER_GROUP * (grp + 1)
        esel = jnp.where(gsel == grp, lgt[lo:lo + EXPERTS_PER_GROUP], esel)
    v0 = jnp.max(esel, axis=0, keepdims=True)
    i0 = jnp.min(jnp.where(esel == v0, rows, big), axis=0, keepdims=True)
    rest = jnp.where(rows == i0, -jnp.inf, esel)
    v1 = jnp.max(rest, axis=0, keepdims=True)
    i1 = jnp.min(jnp.where(rest == v1, rows, big), axis=0, keepdims=True)
    e = jnp.exp(v1 - v0)
    w0 = (1.0 / (1.0 + e)) * gw
    w1 = (e / (1.0 + e)) * gw
    eid0 = gsel * EXPERTS_PER_GROUP + i0
    eid1 = gsel * EXPERTS_PER_GROUP + i1
    ri_ref[...] = jnp.where(rows == 0, eid0, jnp.where(rows == 1, eid1, 0))
    rows_t = lax.broadcasted_iota(jnp.int32, (ROUTER_ROWS, tm), 0)
    slab = jnp.where(rows_t == 0, w0, jnp.where(rows_t == 1, w1, 0.0))
    slab = jnp.where(rows_t == 2, eid0.astype(jnp.float32),
                     jnp.where(rows_t == 3, eid1.astype(jnp.float32), slab))
    rc_ref[...] = slab.T[:, 0:EXPERTS_PER_GROUP]


def _mix(oa, ob, x2d, g_a, g_b, w_out_bf, g_ffn, wr, br):
    t = x2d.shape[0]
    tm = 512
    row = lambda w: pl.BlockSpec((tm, w), lambda i: (i, 0))
    full = lambda a: pl.BlockSpec(a.shape, lambda i: (0, 0))
    return pl.pallas_call(
        _mix_kernel,
        out_shape=(jax.ShapeDtypeStruct((t, D_MODEL), jnp.float32),
                   jax.ShapeDtypeStruct((t, D_MODEL), jnp.bfloat16),
                   jax.ShapeDtypeStruct((EXPERTS_PER_GROUP, t), jnp.int32),
                   jax.ShapeDtypeStruct((t, EXPERTS_PER_GROUP), jnp.float32)),
        grid=(t // tm,),
        in_specs=[row(GROUP_WIDTH), row(GROUP_WIDTH), row(D_MODEL),
                  full(g_a), full(g_b), full(w_out_bf), full(g_ffn), full(wr), full(br)],
        out_specs=(row(D_MODEL), row(D_MODEL),
                   pl.BlockSpec((EXPERTS_PER_GROUP, tm), lambda i: (0, i)),
                   row(EXPERTS_PER_GROUP)),
        compiler_params=pltpu.CompilerParams(dimension_semantics=("arbitrary",),
                                             vmem_limit_bytes=VMEM_LIMIT),
        name="mix_router",
    )(oa, ob, x2d, g_a, g_b, w_out_bf, g_ffn, wr, br)


TOKEN_TILE = 512
RUN_ALIGN = 16
LOCAL_ROWS = 2 * TOKEN_TILE + N_EXPERTS * RUN_ALIGN
MAX_CHUNKS = LOCAL_ROWS // RUN_ALIGN


def _local_slots(member, lo_vec, prior, e_sel, axis):
    ids = lax.broadcasted_iota(jnp.int32, member.shape, axis)
    slot = jnp.where(ids == e_sel, lo_vec + prior, 0.0)
    return jnp.sum(slot, axis=axis, keepdims=True).astype(jnp.int32)


def _chunk_copy(dst_ref, tile, c, local, glob, sem, to_global):
    g = pl.multiple_of(dst_ref[tile * MAX_CHUNKS + c], RUN_ALIGN)
    l = pl.multiple_of(c * RUN_ALIGN, RUN_ALIGN)
    src, dst = local.at[pl.ds(l, RUN_ALIGN), :], glob.at[pl.ds(g, RUN_ALIGN), :]
    if not to_global:
        src, dst = dst, src
    return pltpu.make_async_copy(src, dst, sem)


def _start_chunks(nchunk_ref, dst_ref, tile, local, glob, sem, to_global):
    def body(c, carry):
        _chunk_copy(dst_ref, tile, c, local, glob, sem, to_global).start()
        return carry
    lax.fori_loop(0, nchunk_ref[tile], body, 0)


def _wait_chunks(nchunk_ref, dst_ref, tile, local, glob, sem, to_global):
    def body(c, carry):
        _chunk_copy(dst_ref, tile, c, local, glob, sem, to_global).wait()
        return carry
    lax.fori_loop(0, nchunk_ref[tile], body, 0)


def _lo_vector(lo_ref, tile, shape, axis):
    ids = lax.broadcasted_iota(jnp.int32, shape, axis)
    vec = jnp.zeros(shape, jnp.float32)
    for e in range(N_EXPERTS):
        vec = jnp.where(ids == e, lo_ref[tile * N_EXPERTS + e].astype(jnp.float32), vec)
    return vec


def _dispatch_kernel(lo_ref, nchunk_ref, dst_ref, pad_tile_ref, nused_ref, ri_ref, h_ref, xs_hbm,
                     upper, sbuf, zbuf, sem, zsem, *, n_tiles):
    i = pl.program_id(0)
    n_steps = pl.num_programs(0)
    slot = i % 2

    def zero_tile(tile):
        rows = pl.ds(pl.multiple_of(tile * ROW_TILE, ROW_TILE), ROW_TILE)
        return pltpu.make_async_copy(zbuf, xs_hbm.at[rows, :], zsem)

    @pl.when(i == 0)
    def _():
        r = lax.broadcasted_iota(jnp.int32, (TOKEN_TILE, TOKEN_TILE), 0)
        c = lax.broadcasted_iota(jnp.int32, (TOKEN_TILE, TOKEN_TILE), 1)
        upper[...] = jnp.where(r < c, 1.0, 0.0).astype(jnp.bfloat16)
        zbuf[...] = jnp.zeros_like(zbuf)
        for e in range(N_EXPERTS):
            @pl.when(pad_tile_ref[e] >= 0)
            def _():
                zero_tile(pad_tile_ref[e]).start()
        lax.fori_loop(nused_ref[0], n_tiles, lambda t, c: (zero_tile(t).start(), c)[1], 0)
        for e in range(N_EXPERTS):
            @pl.when(pad_tile_ref[e] >= 0)
            def _():
                zero_tile(pad_tile_ref[e]).wait()
        lax.fori_loop(nused_ref[0], n_tiles, lambda t, c: (zero_tile(t).wait(), c)[1], 0)

    e0 = ri_ref[0:1, :]
    e1 = ri_ref[1:2, :]
    ids = lax.broadcasted_iota(jnp.int32, (N_EXPERTS, TOKEN_TILE), 0)
    member = jnp.where((ids == e0) | (ids == e1), 1.0, 0.0)
    prior = _dot(member.astype(jnp.bfloat16), upper[...])
    lo_vec = _lo_vector(lo_ref, i, (N_EXPERTS, 1), 0)
    d0 = _local_slots(member, lo_vec, prior, e0, 0)
    d1 = _local_slots(member, lo_vec, prior, e1, 0)
    rows = lax.broadcasted_iota(jnp.int32, (LOCAL_ROWS, TOKEN_TILE), 0)
    select = jnp.where((rows == d0) | (rows == d1), 1.0, 0.0).astype(jnp.bfloat16)
    sorted_rows = _dot(select, h_ref[...])

    for s in range(2):
        @pl.when(slot == s)
        def _():
            sbuf[s] = sorted_rows.astype(jnp.bfloat16)

            @pl.when(i > 0)
            def _():
                _wait_chunks(nchunk_ref, dst_ref, i - 1, sbuf.at[1 - s], xs_hbm, sem, True)

            _start_chunks(nchunk_ref, dst_ref, i, sbuf.at[s], xs_hbm, sem, True)

            @pl.when(i == n_steps - 1)
            def _():
                _wait_chunks(nchunk_ref, dst_ref, i, sbuf.at[s], xs_hbm, sem, True)


def _dispatch(lo, nchunk, dst, pad_tile, nused, route_i, h2, p_rows):
    n_tok = h2.shape[0]
    grid_spec = pltpu.PrefetchScalarGridSpec(
        num_scalar_prefetch=5,
        grid=(n_tok // TOKEN_TILE,),
        in_specs=[pl.BlockSpec((EXPERTS_PER_GROUP, TOKEN_TILE), lambda i, *_: (0, i)),
                  pl.BlockSpec((TOKEN_TILE, D_MODEL), lambda i, *_: (i, 0))],
        out_specs=pl.BlockSpec(memory_space=pl.ANY),
        scratch_shapes=[pltpu.VMEM((TOKEN_TILE, TOKEN_TILE), jnp.bfloat16),
                        pltpu.VMEM((2, LOCAL_ROWS, D_MODEL), jnp.bfloat16),
                        pltpu.VMEM((ROW_TILE, D_MODEL), jnp.bfloat16),
                        pltpu.SemaphoreType.DMA, pltpu.SemaphoreType.DMA],
    )
    return pl.pallas_call(
        functools.partial(_dispatch_kernel, n_tiles=p_rows // ROW_TILE),
        out_shape=jax.ShapeDtypeStruct((p_rows, D_MODEL), jnp.bfloat16),
        grid_spec=grid_spec,
        compiler_params=pltpu.CompilerParams(dimension_semantics=("arbitrary",),
                                             vmem_limit_bytes=VMEM_LIMIT),
        name="dispatch",
    )(lo, nchunk, dst, pad_tile, nused, route_i, h2)


def _moe_kernel(texp_ref, first_ref, nused_ref, xs_ref, wg_ref, wu_ref, wd_ref, y_ref,
                wg_bf, wu_bf, wd_bf):
    j = pl.program_id(0)

    @pl.when(j < nused_ref[0])
    def _():
        @pl.when(first_ref[j] == 1)
        def _():
            wg_bf[...] = wg_ref[...].astype(jnp.bfloat16)
            wu_bf[...] = wu_ref[...].astype(jnp.bfloat16)
            wd_bf[...] = wd_ref[...].astype(jnp.bfloat16)

        x = xs_ref[...]
        a = _dot(x, wg_bf[...])
        u = _dot(x, wu_bf[...])
        hid = (a * (1.0 / (1.0 + jnp.exp(-a)))) * u
        y_ref[...] = _dot(hid.astype(jnp.bfloat16), wd_bf[...]).astype(jnp.bfloat16)

    @pl.when(j >= nused_ref[0])
    def _():
        y_ref[...] = jnp.zeros_like(y_ref)


def _moe(texp, first, nused, xs, w_gate, w_up, w_down):
    p_rows = xs.shape[0]
    n_tiles = p_rows // ROW_TILE
    used = lambda j, nu: jnp.minimum(j, nu[0] - 1)
    grid_spec = pltpu.PrefetchScalarGridSpec(
        num_scalar_prefetch=3,
        grid=(n_tiles,),
        in_specs=[
            pl.BlockSpec((ROW_TILE, D_MODEL), lambda j, te, fi, nu: (used(j, nu), 0)),
            pl.BlockSpec((None, D_MODEL, D_EXPERT), lambda j, te, fi, nu: (te[j], 0, 0)),
            pl.BlockSpec((None, D_MODEL, D_EXPERT), lambda j, te, fi, nu: (te[j], 0, 0)),
            pl.BlockSpec((None, D_EXPERT, D_MODEL), lambda j, te, fi, nu: (te[j], 0, 0)),
        ],
        out_specs=pl.BlockSpec((ROW_TILE, D_MODEL), lambda j, te, fi, nu: (j, 0)),
        scratch_shapes=[pltpu.VMEM((D_MODEL, D_EXPERT), jnp.bfloat16),
                        pltpu.VMEM((D_MODEL, D_EXPERT), jnp.bfloat16),
                        pltpu.VMEM((D_EXPERT, D_MODEL), jnp.bfloat16)],
    )
    return pl.pallas_call(
        _moe_kernel,
        out_shape=jax.ShapeDtypeStruct((p_rows, D_MODEL), jnp.bfloat16),
        grid_spec=grid_spec,
        compiler_params=pltpu.CompilerParams(dimension_semantics=("arbitrary",),
                                             vmem_limit_bytes=VMEM_LIMIT),
        name="moe_experts",
    )(texp, first, nused, xs, w_gate, w_up, w_down)


def _final_kernel(lo_ref, nchunk_ref, dst_ref, y_hbm, x1_ref, rc_ref, p_ref, gple_ref, wg_ref, wp_ref,
                  gfin_ref, out_ref, lower, ybuf, sems):
    i = pl.program_id(0)
    n_steps = pl.num_programs(0)
    slot = i % 2

    @pl.when(i == 0)
    def _():
        r = lax.broadcasted_iota(jnp.int32, (TOKEN_TILE, TOKEN_TILE), 0)
        c = lax.broadcasted_iota(jnp.int32, (TOKEN_TILE, TOKEN_TILE), 1)
        lower[...] = jnp.where(c < r, 1.0, 0.0).astype(jnp.bfloat16)
        ybuf[...] = jnp.zeros_like(ybuf)
        _start_chunks(nchunk_ref, dst_ref, 0, ybuf.at[0], y_hbm, sems.at[0], False)

    for s in range(2):
        @pl.when((slot == s) & (i + 1 < n_steps))
        def _():
            _start_chunks(nchunk_ref, dst_ref, i + 1, ybuf.at[1 - s], y_hbm, sems.at[1 - s], False)

    pp = _dot(p_ref[...].astype(jnp.bfloat16), wp_ref[...])
    rc = rc_ref[...]
    w0, w1 = rc[:, 0:1], rc[:, 1:2]
    e0, e1 = rc[:, 2:3].astype(jnp.int32), rc[:, 3:4].astype(jnp.int32)
    ids = lax.broadcasted_iota(jnp.int32, (TOKEN_TILE, ROUTER_ROWS), 1)
    member = jnp.where((ids == e0) | (ids == e1), 1.0, 0.0)
    prior = _dot(lower[...], member.astype(jnp.bfloat16))
    lo_vec = _lo_vector(lo_ref, i, (1, ROUTER_ROWS), 1)
    d0 = _local_slots(member, lo_vec, prior, e0, 1)
    d1 = _local_slots(member, lo_vec, prior, e1, 1)
    cols = lax.broadcasted_iota(jnp.int32, (TOKEN_TILE, LOCAL_ROWS), 1)
    pick0 = jnp.where(cols == d0, 1.0, 0.0).astype(jnp.bfloat16)
    pick1 = jnp.where(cols == d1, 1.0, 0.0).astype(jnp.bfloat16)

    for s in range(2):
        @pl.when(slot == s)
        def _():
            _wait_chunks(nchunk_ref, dst_ref, i, ybuf.at[s], y_hbm, sems.at[s], False)
            yb = ybuf[s]
            x2 = x1_ref[...] + (w0 * _dot(pick0, yb) + w1 * _dot(pick1, yb))
            z = _dot(_rms(x2, gple_ref[...]).astype(jnp.bfloat16), wg_ref[...])
            gate = 1.0 / (1.0 + jnp.exp(-z))
            x3 = x2 + pp * gate
            out_ref[...] = _rms(x3, gfin_ref[...])


def _final(lo, nchunk, dst, y, x1, rc, p2d, g_ple, w_gate_bf, w_proj_bf, g_final):
    t = x1.shape[0]
    tm = TOKEN_TILE
    row = lambda w: pl.BlockSpec((tm, w), lambda i, *_: (i, 0))
    full = lambda a: pl.BlockSpec(a.shape, lambda i, *_: (0, 0))
    grid_spec = pltpu.PrefetchScalarGridSpec(
        num_scalar_prefetch=3,
        grid=(t // tm,),
        in_specs=[pl.BlockSpec(memory_space=pl.ANY), row(D_MODEL), row(EXPERTS_PER_GROUP),
                  row(PLE_DIM), full(g_ple), full(w_gate_bf), full(w_proj_bf), full(g_final)],
        out_specs=row(D_MODEL),
        scratch_shapes=[pltpu.VMEM((TOKEN_TILE, TOKEN_TILE), jnp.bfloat16),
                        pltpu.VMEM((2, LOCAL_ROWS, D_MODEL), jnp.bfloat16),
                        pltpu.SemaphoreType.DMA((2,))],
    )
    return pl.pallas_call(
        _final_kernel,
        out_shape=jax.ShapeDtypeStruct((t, D_MODEL), jnp.float32),
        grid_spec=grid_spec,
        compiler_params=pltpu.CompilerParams(dimension_semantics=("arbitrary",),
                                             vmem_limit_bytes=VMEM_LIMIT),
        name="combine_ple_final",
    )(lo, nchunk, dst, y, x1, rc, p2d, g_ple, w_gate_bf, w_proj_bf, g_final)


def _dispatch_tables(route_i, n_tok):
    n_tt = n_tok // TOKEN_TILE
    p_rows = 2 * n_tok + n_tt * N_EXPERTS * (RUN_ALIGN - 1) + N_EXPERTS * ROW_TILE
    p_rows = -(-p_rows // ROW_TILE) * ROW_TILE
    n_tiles = p_rows // ROW_TILE
    e = route_i[:2].reshape(2, n_tt, TOKEN_TILE)
    onehot = (e[..., None] == jnp.arange(N_EXPERTS, dtype=jnp.int32)).astype(jnp.int32)
    counts = jnp.sum(onehot, axis=(0, 2))
    n8 = (counts + RUN_ALIGN - 1) // RUN_ALIGN * RUN_ALIGN
    lo = jnp.cumsum(n8, axis=1) - n8
    total = jnp.sum(n8, axis=0)
    tiles_per = (total + ROW_TILE - 1) // ROW_TILE
    tile_end = jnp.cumsum(tiles_per)
    tile_start = tile_end - tiles_per
    g = tile_start[None, :] * ROW_TILE + (jnp.cumsum(n8, axis=0) - n8)
    n_used = tile_end[-1]
    tile_ids = jnp.arange(n_tiles, dtype=jnp.int32)
    clamped = jnp.minimum(tile_ids, n_used - 1)
    texp = jnp.sum((tile_end[None, :] <= clamped[:, None]).astype(jnp.int32), axis=1)
    first = ((tile_ids == tile_start[texp]) & (tile_ids < n_used)).astype(jnp.int32)
    pad_tile = jnp.where(total > 0, tile_end - 1, -1).astype(jnp.int32)
    chunk_row = jnp.arange(MAX_CHUNKS, dtype=jnp.int32)[None, :, None] * RUN_ALIGN
    in_run = (chunk_row >= lo[:, None, :]) & (chunk_row < (lo + n8)[:, None, :])
    dst = jnp.sum(jnp.where(in_run, g[:, None, :] + chunk_row - lo[:, None, :], 0), axis=2)
    nchunk = jnp.sum(n8, axis=1) // RUN_ALIGN
    i32 = lambda a: a.reshape(-1).astype(jnp.int32)
    return (texp, first, n_used.reshape(1).astype(jnp.int32), pad_tile, i32(lo), i32(nchunk),
            i32(dst), p_rows)


def kernel(x, p, g_mix, w_in, sink, g_grp_a, g_grp_b, w_out, g_ffn, w_router_group, b_router_group,
           w_router_expert, b_router_expert, w_expert_gate, w_expert_up, w_expert_down, g_ple,
           w_ple_gate, w_ple_proj, g_final):
    batch, seq, _ = x.shape
    n_tok = batch * seq
    depth = w_in.shape[0]
    assert depth == 1, "the final RMSNorm is fused into the last layer's epilogue"
    bf = jnp.bfloat16
    xc = x.reshape(n_tok, D_MODEL)
    for i in range(depth):
        *qkv_a, qb, kb, vb = _proj(xc, g_mix[i][None, :], w_in[i].astype(bf), batch, seq)
        oa = _attn_a(qkv_a, batch, seq)
        ob = _attn_b(qb, kb, vb, sink[i], batch, seq)

        wr = jnp.zeros((D_MODEL, ROUTER_ROWS), jnp.float32)
        wr = wr.at[:, 0:N_GROUPS].set(w_router_group[i])
        wr = wr.at[:, EXPERTS_PER_GROUP:EXPERTS_PER_GROUP + N_EXPERTS].set(w_router_expert[i])
        br = jnp.zeros((ROUTER_ROWS, 1), jnp.float32)
        br = br.at[0:N_GROUPS, 0].set(b_router_group[i])
        br = br.at[EXPERTS_PER_GROUP:EXPERTS_PER_GROUP + N_EXPERTS, 0].set(b_router_expert[i])

        x1, h2, route_i, route_c = _mix(
            oa, ob, xc, g_grp_a[i][None, :], g_grp_b[i][None, :], w_out[i].astype(bf), g_ffn[i][None, :], wr, br)

        texp, first, nused, pad_tile, lo, nchunk, dst, p_rows = _dispatch_tables(route_i, n_tok)
        xs = _dispatch(lo, nchunk, dst, pad_tile, nused, route_i, h2, p_rows)
        y = _moe(texp, first, nused, xs, w_expert_gate[i], w_expert_up[i], w_expert_down[i])
        xc = _final(lo, nchunk, dst, y, x1, route_c, p[i].reshape(n_tok, PLE_DIM), g_ple[i][None, :],
                    w_ple_gate[i].astype(bf), w_ple_proj[i].astype(bf), g_final[None, :])
    return xc.reshape(batch, seq, D_MODEL)
```

```python
import functools

import jax
import jax.numpy as jnp
from jax import lax
from jax.experimental import pallas as pl
from jax.experimental.pallas import tpu as pltpu

D_MODEL = 1024
HEAD_DIM = 64
GROUP_WIDTH = 512
N_HEADS = 8
B_KV_HEADS = 2
B_KV_WIDTH = B_KV_HEADS * HEAD_DIM
A_CONFIGS = ((128, 1), (512, 4), (2048, 16))
B_HALF_WINDOW = 128
N_GROUPS = 4
EXPERTS_PER_GROUP = 8
N_EXPERTS = N_GROUPS * EXPERTS_PER_GROUP
D_EXPERT = 512
PLE_DIM = 256
EPS = 1e-6
NEG = -1e30
LOG2E = 1.4426950408889634
LN2 = 0.6931471805599453

LANES = 128
Q_BLOCK = 128
HEADS_PER_PASS = 2
PASS_WIDTH = HEADS_PER_PASS * HEAD_DIM
ROW_TILE = 256
ROUTER_ROWS = 128
VMEM_LIMIT = 48 * 1024 * 1024
ATTN_VMEM_LIMIT = 56 * 1024 * 1024


def _rms(x, g):
    r = lax.rsqrt(jnp.mean(x * x, axis=-1, keepdims=True) + EPS)
    return (x * r) * g


def _dot(a, b):
    return jnp.dot(a, b, preferred_element_type=jnp.float32)


PROJ_TILE = 512


def _proj_kernel(x_ref, g_ref, w_ref, *refs):
    a_refs, (qb_ref, kb_ref, vb_ref, res) = refs[:9], refs[9:]
    h = _rms(x_ref[...], g_ref[...]).astype(jnp.bfloat16)
    scale = HEAD_DIM ** -0.5 * LOG2E
    W = GROUP_WIDTH
    tiles = GROUP_WIDTH // PASS_WIDTH
    qkv_a = (_dot(h, w_ref[:, 0:W]) * scale, _dot(h, w_ref[:, W:2 * W]), _dot(h, w_ref[:, 2 * W:3 * W]))
    for part in range(3):
        for pair in range(tiles):
            res[part * tiles + pair] = qkv_a[part][:, pair * PASS_WIDTH:(pair + 1) * PASS_WIDTH]
    qb_ref[...] = (_dot(h, w_ref[:, 3 * W:4 * W]) * scale).astype(jnp.bfloat16)
    kb_ref[...] = _dot(h, w_ref[:, 4 * W:4 * W + B_KV_WIDTH]).astype(jnp.bfloat16)
    vb_ref[...] = _dot(h, w_ref[:, 4 * W + B_KV_WIDTH:4 * W + 2 * B_KV_WIDTH]).astype(jnp.bfloat16)
    for c, (_, dil) in enumerate(A_CONFIGS):
        rows = PROJ_TILE // dil
        for part in range(3):
            out = a_refs[3 * c + part]
            for pair in range(tiles):
                tile = res.at[part * tiles + pair]
                for rho in range(dil):
                    out[pair, rho] = tile[pl.ds(rho, rows, stride=dil), :].astype(jnp.bfloat16)


def _proj(x2d, g_mix, w_in_bf, batch, seq):
    t = x2d.shape[0]
    tm = PROJ_TILE
    tiles_per_seq = seq // tm
    in_width = w_in_bf.shape[1]
    n_pairs = N_HEADS // HEADS_PER_PASS
    row = lambda w: pl.BlockSpec((tm, w), lambda i: (i, 0))
    a_shapes, a_specs = [], []
    for _, dil in A_CONFIGS:
        for _ in range(3):
            a_shapes.append(jax.ShapeDtypeStruct((batch, n_pairs, dil, seq // dil, PASS_WIDTH),
                                                 jnp.bfloat16))
            a_specs.append(pl.BlockSpec((None, n_pairs, dil, tm // dil, PASS_WIDTH),
                                        lambda i: (i // tiles_per_seq, 0, 0, i % tiles_per_seq, 0)))
    wide = jax.ShapeDtypeStruct((t, GROUP_WIDTH), jnp.bfloat16)
    narrow = jax.ShapeDtypeStruct((t, B_KV_WIDTH), jnp.bfloat16)
    return pl.pallas_call(
        _proj_kernel,
        out_shape=tuple(a_shapes) + (wide, narrow, narrow),
        grid=(t // tm,),
        in_specs=[row(D_MODEL),
                  pl.BlockSpec((1, D_MODEL), lambda i: (0, 0)),
                  pl.BlockSpec((D_MODEL, in_width), lambda i: (0, 0))],
        out_specs=tuple(a_specs) + (row(GROUP_WIDTH), row(B_KV_WIDTH), row(B_KV_WIDTH)),
        scratch_shapes=[pltpu.VMEM((3 * GROUP_WIDTH // PASS_WIDTH, tm, PASS_WIDTH), jnp.float32)],
        compiler_params=pltpu.CompilerParams(dimension_semantics=("arbitrary",),
                                             vmem_limit_bytes=VMEM_LIMIT),
        name="proj",
    )(x2d, g_mix, w_in_bf)


def _fill_bias(bias_ref, half, nkeys, dist_scale):
    i = lax.broadcasted_iota(jnp.int32, (Q_BLOCK, nkeys), 0)
    j = lax.broadcasted_iota(jnp.int32, (Q_BLOCK, nkeys), 1)
    for var, rel0 in enumerate((0, -half, Q_BLOCK - nkeys)):
        d = jnp.abs(j + rel0 - i)
        dist = d.astype(jnp.float32) * dist_scale
        for h in range(N_HEADS):
            slope = 2.0 ** (-(h + 1))
            bias_ref[var, h] = jnp.where(d <= half, -(slope * dist) * LOG2E, NEG)


def _key_window(q0, seq_len, half, nkeys):
    lo = q0 - half
    ks = pl.multiple_of(jnp.clip(lo, 0, seq_len - nkeys), HEAD_DIM)
    var = jnp.where(lo < 0, 0, jnp.where(lo > seq_len - nkeys, 2, 1))
    return ks, var


def _fold_lane_tiles(a, op):
    out = a[..., 0:LANES]
    for t in range(1, a.shape[-1] // LANES):
        out = op(out, a[..., t * LANES:(t + 1) * LANES])
    return out


def _attend_pass(q, k, v, bias, sinks):
    nkeys = k.shape[0]
    lane_head = lax.broadcasted_iota(jnp.int32, (Q_BLOCK, PASS_WIDTH), 1) // HEAD_DIM
    zero = jnp.zeros_like(q)
    q_stack = jnp.concatenate(
        [jnp.where(lane_head == h, q, zero) for h in range(HEADS_PER_PASS)], axis=0)
    s = lax.dot_general(q_stack, k, (((1,), (1,)), ((), ())), preferred_element_type=jnp.float32)
    s = s.reshape(HEADS_PER_PASS, Q_BLOCK, nkeys) + bias
    m_tile = _fold_lane_tiles(s, jnp.maximum)
    if sinks is not None:
        tile_head = lax.broadcasted_iota(jnp.int32, m_tile.shape, 0)
        tile_lane = lax.broadcasted_iota(jnp.int32, m_tile.shape, 2)
        sk = jnp.zeros(m_tile.shape, jnp.float32)
        for h in range(HEADS_PER_PASS):
            sk = jnp.where(tile_head == h, sinks[h] * LOG2E, sk)
        m_tile = jnp.maximum(m_tile, sk)
    m = jnp.max(m_tile, axis=-1, keepdims=True)
    p = jnp.exp2(s - m)
    den_tile = _fold_lane_tiles(p, jnp.add)
    if sinks is not None:
        den_tile = den_tile + jnp.where(tile_lane == 0, jnp.exp2(sk - m), 0.0)
    den = jnp.sum(den_tile, axis=-1, keepdims=True)
    pv = _dot(p.reshape(HEADS_PER_PASS * Q_BLOCK, nkeys).astype(jnp.bfloat16), v)
    pv = pv.reshape(HEADS_PER_PASS, Q_BLOCK, PASS_WIDTH) * (1.0 / den)
    lse = m * LN2 + jnp.log(den)
    o = jnp.zeros((Q_BLOCK, PASS_WIDTH), jnp.float32)
    lse_lanes = jnp.zeros((Q_BLOCK, PASS_WIDTH), jnp.float32)
    for h in range(HEADS_PER_PASS):
        o = jnp.where(lane_head == h, pv[h], o)
        lse_lanes = jnp.where(lane_head == h, lse[h], lse_lanes)
    return o, lse_lanes


A_HALF = 64
A_KEYS = Q_BLOCK + 2 * A_HALF
PASSES_PER_BODY = 16


def _attn_a_kernel(*refs, seq):
    n_cfg = len(A_CONFIGS)
    qkv = refs[:3 * n_cfg]
    o_ref, bias_ref = refs[3 * n_cfg], refs[3 * n_cfg + 1]
    scratch = refs[3 * n_cfg + 2:]
    pair = pl.program_id(1)

    @pl.when((pl.program_id(0) == 0) & (pair == 0))
    def _():
        for c, (_, dil) in enumerate(A_CONFIGS):
            _fill_bias(bias_ref.at[c], A_HALF, A_KEYS, float(dil))

    for c, (window, dil) in enumerate(A_CONFIGS):
        assert window // (2 * dil) == A_HALF
        q_ref, k_ref, v_ref = qkv[3 * c:3 * c + 3]
        o_sc, l_sc = scratch[2 * c], scratch[2 * c + 1]
        sub_len = seq // dil
        blocks = sub_len // Q_BLOCK

        def body(idx, carry, q_ref=q_ref, k_ref=k_ref, v_ref=v_ref, o_sc=o_sc, l_sc=l_sc, c=c,
                 sub_len=sub_len, blocks=blocks):
            rho = idx // blocks
            q0 = pl.multiple_of((idx % blocks) * Q_BLOCK, Q_BLOCK)
            ks, var = _key_window(q0, sub_len, A_HALF, A_KEYS)
            bias = bias_ref[c, var, pl.ds(pair * HEADS_PER_PASS, HEADS_PER_PASS)]
            o, lse = _attend_pass(q_ref[rho, pl.ds(q0, Q_BLOCK), :], k_ref[rho, pl.ds(ks, A_KEYS), :],
                                  v_ref[rho, pl.ds(ks, A_KEYS), :], bias, None)
            o_sc[rho, pl.ds(q0, Q_BLOCK), :] = o
            l_sc[rho, pl.ds(q0, Q_BLOCK), :] = lse
            return carry

        lax.fori_loop(0, dil * blocks, body, 0, unroll=PASSES_PER_BODY)

    widest = A_CONFIGS[-1][1]
    rows = seq // widest
    for r in range(widest):
        outs, lses = [], []
        for c, (_, dil) in enumerate(A_CONFIGS):
            step = widest // dil
            idx = (r % dil, pl.ds(r // dil, rows, stride=step) if step > 1 else pl.ds(0, rows))
            outs.append(scratch[2 * c].at[idx[0]][idx[1], :])
            lses.append(scratch[2 * c + 1].at[idx[0]][idx[1], :])
        mx = functools.reduce(jnp.maximum, lses)
        es = [jnp.exp(l - mx) for l in lses]
        inv = 1.0 / functools.reduce(jnp.add, es)
        merged = functools.reduce(jnp.add, [(e * inv) * o for e, o in zip(es, outs)])
        o_ref[pl.ds(r, rows, stride=widest), :] = merged


def _attn_a(qkv, batch, seq):
    n_pairs = N_HEADS // HEADS_PER_PASS
    in_specs, scratch = [], [pltpu.VMEM((len(A_CONFIGS), 3, N_HEADS, Q_BLOCK, A_KEYS), jnp.float32)]
    for _, dil in A_CONFIGS:
        blk = (None, None, dil, seq // dil, PASS_WIDTH)
        in_specs += [pl.BlockSpec(blk, lambda b, p: (b, p, 0, 0, 0))] * 3
        scratch += [pltpu.VMEM((dil, seq // dil, PASS_WIDTH), jnp.float32)] * 2
    return pl.pallas_call(
        functools.partial(_attn_a_kernel, seq=seq),
        out_shape=jax.ShapeDtypeStruct((batch, seq, GROUP_WIDTH), jnp.float32),
        grid=(batch, n_pairs),
        in_specs=in_specs,
        out_specs=pl.BlockSpec((None, seq, PASS_WIDTH), lambda b, p: (b, 0, p)),
        scratch_shapes=scratch,
        compiler_params=pltpu.CompilerParams(dimension_semantics=("arbitrary", "arbitrary"),
                                             vmem_limit_bytes=ATTN_VMEM_LIMIT),
        name="attn_a",
    )(*qkv).reshape(batch * seq, GROUP_WIDTH)


def _attn_b_kernel(sink_ref, q_ref, k_ref, v_ref, o_ref, bias_ref, k4_ref, v4_ref, *, half, nkeys,
                   seq_len, q_tile):
    first = (pl.program_id(0) == 0) & (pl.program_id(1) == 0)

    @pl.when(first)
    def _():
        _fill_bias(bias_ref, half, nkeys, 1.0)

    @pl.when(pl.program_id(1) == 0)
    def _():
        c = lax.broadcasted_iota(jnp.int32, (B_KV_WIDTH, GROUP_WIDTH), 0)
        j = lax.broadcasted_iota(jnp.int32, (B_KV_WIDTH, GROUP_WIDTH), 1)
        src = (j // (GROUP_WIDTH // B_KV_HEADS)) * HEAD_DIM + j % HEAD_DIM
        rep = jnp.where(c == src, 1.0, 0.0).astype(jnp.bfloat16)
        chunk = 512

        def body(i, carry):
            rows = pl.ds(pl.multiple_of(i * chunk, chunk), chunk)
            k4_ref[rows, :] = _dot(k_ref[rows, :], rep).astype(jnp.bfloat16)
            v4_ref[rows, :] = _dot(v_ref[rows, :], rep).astype(jnp.bfloat16)
            return carry

        lax.fori_loop(0, seq_len // chunk, body, 0)

    base = pl.program_id(1) * q_tile

    def body(jb, carry):
        row0 = pl.multiple_of(jb * Q_BLOCK, Q_BLOCK)
        ks, var = _key_window(base + row0, seq_len, half, nkeys)
        for g in range(N_HEADS // HEADS_PER_PASS):
            cols = slice(g * PASS_WIDTH, (g + 1) * PASS_WIDTH)
            heads = pl.ds(g * HEADS_PER_PASS, HEADS_PER_PASS)
            sinks = [sink_ref[g * HEADS_PER_PASS + h] for h in range(HEADS_PER_PASS)]
            o, _ = _attend_pass(q_ref[pl.ds(row0, Q_BLOCK), cols], k4_ref[pl.ds(ks, nkeys), cols],
                                v4_ref[pl.ds(ks, nkeys), cols], bias_ref[var, heads], sinks)
            o_ref[pl.ds(row0, Q_BLOCK), cols] = o
        return carry

    lax.fori_loop(0, q_tile // Q_BLOCK, body, 0,
                  unroll=PASSES_PER_BODY // (N_HEADS // HEADS_PER_PASS))


def _attn_b(q, k, v, sink, batch, seq):
    half = B_HALF_WINDOW
    nkeys = Q_BLOCK + 2 * half
    q_tile = 1024
    view = lambda a: a.reshape(batch, seq, a.shape[-1])
    kern = functools.partial(_attn_b_kernel, half=half, nkeys=nkeys, seq_len=seq, q_tile=q_tile)
    o = pl.pallas_call(
        kern,
        out_shape=jax.ShapeDtypeStruct((batch, seq, GROUP_WIDTH), jnp.float32),
        grid=(batch, seq // q_tile),
        in_specs=[pl.BlockSpec(memory_space=pltpu.SMEM),
                  pl.BlockSpec((None, q_tile, GROUP_WIDTH), lambda b, i: (b, i, 0)),
                  pl.BlockSpec((None, seq, B_KV_WIDTH), lambda b, i: (b, 0, 0)),
                  pl.BlockSpec((None, seq, B_KV_WIDTH), lambda b, i: (b, 0, 0))],
        out_specs=pl.BlockSpec((None, q_tile, GROUP_WIDTH), lambda b, i: (b, i, 0)),
        scratch_shapes=[pltpu.VMEM((3, N_HEADS, Q_BLOCK, nkeys), jnp.float32),
                        pltpu.VMEM((seq, GROUP_WIDTH), jnp.bfloat16),
                        pltpu.VMEM((seq, GROUP_WIDTH), jnp.bfloat16)],
        compiler_params=pltpu.CompilerParams(dimension_semantics=("arbitrary", "arbitrary"),
                                             vmem_limit_bytes=VMEM_LIMIT),
        name="attn_b",
    )(sink, view(q), view(k), view(v))
    return o.reshape(batch * seq, GROUP_WIDTH)


def _mix_kernel(oa_ref, ob_ref, x_ref, ga_ref, gb_ref, wout_ref, gffn_ref, wr_ref, br_ref,
                x1_ref, h2_ref, ri_ref, rc_ref):
    tm = x_ref.shape[0]
    na = _rms(oa_ref[...], ga_ref[...]).astype(jnp.bfloat16)
    nb = _rms(ob_ref[...], gb_ref[...]).astype(jnp.bfloat16)
    x1 = (x_ref[...] + _dot(na, wout_ref[0:GROUP_WIDTH, :])
          + _dot(nb, wout_ref[GROUP_WIDTH:2 * GROUP_WIDTH, :]))
    x1_ref[...] = x1
    h2 = _rms(x1, gffn_ref[...])
    h2_ref[...] = h2.astype(jnp.bfloat16)

    h_hi = h2.astype(jnp.bfloat16)
    h_lo = (h2 - h_hi.astype(jnp.float32)).astype(jnp.bfloat16)
    wr = wr_ref[...]
    w_hi = wr.astype(jnp.bfloat16)
    w_lo = (wr - w_hi.astype(jnp.float32)).astype(jnp.bfloat16)
    w_cat = jnp.concatenate([w_hi, w_lo], axis=1)
    lg2 = _dot(h_hi, w_cat) + _dot(h_lo, w_cat)
    lg = lg2[:, 0:ROUTER_ROWS] + lg2[:, ROUTER_ROWS:2 * ROUTER_ROWS]
    lgt = lg.T + br_ref[...]

    rows = lax.broadcasted_iota(jnp.int32, (EXPERTS_PER_GROUP, tm), 0)
    big = jnp.int32(1 << 20)
    glog = jnp.where(rows < N_GROUPS, lgt[0:EXPERTS_PER_GROUP], -jnp.inf)
    gmax = jnp.max(glog, axis=0, keepdims=True)
    gsel = jnp.min(jnp.where(glog == gmax, rows, big), axis=0, keepdims=True)
    gw = 1.0 / jnp.sum(jnp.exp(glog - gmax), axis=0, keepdims=True)
    esel = jnp.zeros((EXPERTS_PER_GROUP, tm), jnp.float32)
    for grp in range(N_GROUPS):
        lo = EXPERTS_PER_GROUP * (grp + 1)
        esel = jnp.where(gsel == grp, lgt[lo:lo + EXPERTS_PER_GROUP], esel)
    v0 = jnp.max(esel, axis=0, keepdims=True)
    i0 = jnp.min(jnp.where(esel == v0, rows, big), axis=0, keepdims=True)
    rest = jnp.where(rows == i0, -jnp.inf, esel)
    v1 = jnp.max(rest, axis=0, keepdims=True)
    i1 = jnp.min(jnp.where(rest == v1, rows, big), axis=0, keepdims=True)
    e = jnp.exp(v1 - v0)
    w0 = (1.0 / (1.0 + e)) * gw
    w1 = (e / (1.0 + e)) * gw
    eid0 = gsel * EXPERTS_PER_GROUP + i0
    eid1 = gsel * EXPERTS_PER_GROUP + i1
    ri_ref[...] = jnp.where(rows == 0, eid0, jnp.where(rows == 1, eid1, 0))
    rows_t = lax.broadcasted_iota(jnp.int32, (ROUTER_ROWS, tm), 0)
    slab = jnp.where(rows_t == 0, w0, jnp.where(rows_t == 1, w1, 0.0))
    slab = jnp.where(rows_t == 2, eid0.astype(jnp.float32),
                     jnp.where(rows_t == 3, eid1.astype(jnp.float32), slab))
    rc_ref[...] = slab.T[:, 0:EXPERTS_PER_GROUP]


def _mix(oa, ob, x2d, g_a, g_b, w_out_bf, g_ffn, wr, br):
    t = x2d.shape[0]
    tm = 512
    row = lambda w: pl.BlockSpec((tm, w), lambda i: (i, 0))
    full = lambda a: pl.BlockSpec(a.shape, lambda i: (0, 0))
    return pl.pallas_call(
        _mix_kernel,
        out_shape=(jax.ShapeDtypeStruct((t, D_MODEL), jnp.float32),
                   jax.ShapeDtypeStruct((t, D_MODEL), jnp.bfloat16),
                   jax.ShapeDtypeStruct((EXPERTS_PER_GROUP, t), jnp.int32),
                   jax.ShapeDtypeStruct((t, EXPERTS_PER_GROUP), jnp.float32)),
        grid=(t // tm,),
        in_specs=[row(GROUP_WIDTH), row(GROUP_WIDTH), row(D_MODEL),
                  full(g_a), full(g_b), full(w_out_bf), full(g_ffn), full(wr), full(br)],
        out_specs=(row(D_MODEL), row(D_MODEL),
                   pl.BlockSpec((EXPERTS_PER_GROUP, tm), lambda i: (0, i)),
                   row(EXPERTS_PER_GROUP)),
        compiler_params=pltpu.CompilerParams(dimension_semantics=("arbitrary",),
                                             vmem_limit_bytes=VMEM_LIMIT),
        name="mix_router",
    )(oa, ob, x2d, g_a, g_b, w_out_bf, g_ffn, wr, br)


TOKEN_TILE = 512
RUN_ALIGN = 16
LOCAL_ROWS = 2 * TOKEN_TILE + N_EXPERTS * RUN_ALIGN
MAX_CHUNKS = LOCAL_ROWS // RUN_ALIGN


def _local_slots(member, lo_vec, prior, e_sel, axis):
    ids = lax.broadcasted_iota(jnp.int32, member.shape, axis)
    slot = jnp.where(ids == e_sel, lo_vec + prior, 0.0)
    return jnp.sum(slot, axis=axis, keepdims=True).astype(jnp.int32)


def _chunk_copy(dst_ref, tile, c, local, glob, sem, to_global):
    g = pl.multiple_of(dst_ref[tile * MAX_CHUNKS + c], RUN_ALIGN)
    l = pl.multiple_of(c * RUN_ALIGN, RUN_ALIGN)
    src, dst = local.at[pl.ds(l, RUN_ALIGN), :], glob.at[pl.ds(g, RUN_ALIGN), :]
    if not to_global:
        src, dst = dst, src
    return pltpu.make_async_copy(src, dst, sem)


def _start_chunks(nchunk_ref, dst_ref, tile, local, glob, sem, to_global):
    def body(c, carry):
        _chunk_copy(dst_ref, tile, c, local, glob, sem, to_global).start()
        return carry
    lax.fori_loop(0, nchunk_ref[tile], body, 0)


def _wait_chunks(nchunk_ref, dst_ref, tile, local, glob, sem, to_global):
    def body(c, carry):
        _chunk_copy(dst_ref, tile, c, local, glob, sem, to_global).wait()
        return carry
    lax.fori_loop(0, nchunk_ref[tile], body, 0)


def _lo_vector(lo_ref, tile, shape, axis):
    ids = lax.broadcasted_iota(jnp.int32, shape, axis)
    vec = jnp.zeros(shape, jnp.float32)
    for e in range(N_EXPERTS):
        vec = jnp.where(ids == e, lo_ref[tile * N_EXPERTS + e].astype(jnp.float32), vec)
    return vec


def _dispatch_kernel(lo_ref, nchunk_ref, dst_ref, pad_tile_ref, nused_ref, ri_ref, h_ref, xs_hbm,
                     upper, sbuf, zbuf, sem, zsem, *, n_tiles):
    i = pl.program_id(0)
    n_steps = pl.num_programs(0)
    slot = i % 2

    def zero_tile(tile):
        rows = pl.ds(pl.multiple_of(tile * ROW_TILE, ROW_TILE), ROW_TILE)
        return pltpu.make_async_copy(zbuf, xs_hbm.at[rows, :], zsem)

    @pl.when(i == 0)
    def _():
        r = lax.broadcasted_iota(jnp.int32, (TOKEN_TILE, TOKEN_TILE), 0)
        c = lax.broadcasted_iota(jnp.int32, (TOKEN_TILE, TOKEN_TILE), 1)
        upper[...] = jnp.where(r < c, 1.0, 0.0).astype(jnp.bfloat16)
        zbuf[...] = jnp.zeros_like(zbuf)
        for e in range(N_EXPERTS):
            @pl.when(pad_tile_ref[e] >= 0)
            def _():
                zero_tile(pad_tile_ref[e]).start()
        lax.fori_loop(nused_ref[0], n_tiles, lambda t, c: (zero_tile(t).start(), c)[1], 0)
        for e in range(N_EXPERTS):
            @pl.when(pad_tile_ref[e] >= 0)
            def _():
                zero_tile(pad_tile_ref[e]).wait()
        lax.fori_loop(nused_ref[0], n_tiles, lambda t, c: (zero_tile(t).wait(), c)[1], 0)

    e0 = ri_ref[0:1, :]
    e1 = ri_ref[1:2, :]
    ids = lax.broadcasted_iota(jnp.int32, (N_EXPERTS, TOKEN_TILE), 0)
    member = jnp.where((ids == e0) | (ids == e1), 1.0, 0.0)
    prior = _dot(member.astype(jnp.bfloat16), upper[...])
    lo_vec = _lo_vector(lo_ref, i, (N_EXPERTS, 1), 0)
    d0 = _local_slots(member, lo_vec, prior, e0, 0)
    d1 = _local_slots(member, lo_vec, prior, e1, 0)
    rows = lax.broadcasted_iota(jnp.int32, (LOCAL_ROWS, TOKEN_TILE), 0)
    select = jnp.where((rows == d0) | (rows == d1), 1.0, 0.0).astype(jnp.bfloat16)
    sorted_rows = _dot(select, h_ref[...])

    for s in range(2):
        @pl.when(slot == s)
        def _():
            sbuf[s] = sorted_rows.astype(jnp.bfloat16)

            @pl.when(i > 0)
            def _():
                _wait_chunks(nchunk_ref, dst_ref, i - 1, sbuf.at[1 - s], xs_hbm, sem, True)

            _start_chunks(nchunk_ref, dst_ref, i, sbuf.at[s], xs_hbm, sem, True)

            @pl.when(i == n_steps - 1)
            def _():
                _wait_chunks(nchunk_ref, dst_ref, i, sbuf.at[s], xs_hbm, sem, True)


def _dispatch(lo, nchunk, dst, pad_tile, nused, route_i, h2, p_rows):
    n_tok = h2.shape[0]
    grid_spec = pltpu.PrefetchScalarGridSpec(
        num_scalar_prefetch=5,
        grid=(n_tok // TOKEN_TILE,),
        in_specs=[pl.BlockSpec((EXPERTS_PER_GROUP, TOKEN_TILE), lambda i, *_: (0, i)),
                  pl.BlockSpec((TOKEN_TILE, D_MODEL), lambda i, *_: (i, 0))],
        out_specs=pl.BlockSpec(memory_space=pl.ANY),
        scratch_shapes=[pltpu.VMEM((TOKEN_TILE, TOKEN_TILE), jnp.bfloat16),
                        pltpu.VMEM((2, LOCAL_ROWS, D_MODEL), jnp.bfloat16),
                        pltpu.VMEM((ROW_TILE, D_MODEL), jnp.bfloat16),
                        pltpu.SemaphoreType.DMA, pltpu.SemaphoreType.DMA],
    )
    return pl.pallas_call(
        functools.partial(_dispatch_kernel, n_tiles=p_rows // ROW_TILE),
        out_shape=jax.ShapeDtypeStruct((p_rows, D_MODEL), jnp.bfloat16),
        grid_spec=grid_spec,
        compiler_params=pltpu.CompilerParams(dimension_semantics=("arbitrary",),
                                             vmem_limit_bytes=VMEM_LIMIT),
        name="dispatch",
    )(lo, nchunk, dst, pad_tile, nused, route_i, h2)


def _moe_kernel(texp_ref, first_ref, nused_ref, slot_ref, next_ref, xs_ref, wg_hbm, wu_hbm, wd_hbm,
                y_ref, wg_f32, wu_f32, wd_f32, wg_bf, wu_bf, wd_bf, sems):
    j = pl.program_id(0)

    def fetch(expert, slot):
        return [pltpu.make_async_copy(src.at[expert], dst.at[slot], sems.at[slot, k])
                for k, (src, dst) in enumerate(((wg_hbm, wg_f32), (wu_hbm, wu_f32), (wd_hbm, wd_f32)))]

    @pl.when(j == 0)
    def _():
        for cp in fetch(texp_ref[0], 0):
            cp.start()

    @pl.when(j < nused_ref[0])
    def _():
        @pl.when(first_ref[j] == 1)
        def _():
            for s in range(2):
                @pl.when(slot_ref[j] == s)
                def _():
                    for cp in fetch(texp_ref[j], s):
                        cp.wait()
                    wg_bf[...] = wg_f32[s].astype(jnp.bfloat16)
                    wu_bf[...] = wu_f32[s].astype(jnp.bfloat16)
                    wd_bf[...] = wd_f32[s].astype(jnp.bfloat16)

                    @pl.when(next_ref[j] >= 0)
                    def _():
                        for cp in fetch(next_ref[j], 1 - s):
                            cp.start()

        x = xs_ref[...]
        a = _dot(x, wg_bf[...])
        u = _dot(x, wu_bf[...])
        hid = (a * (1.0 / (1.0 + jnp.exp(-a)))) * u
        y_ref[...] = _dot(hid.astype(jnp.bfloat16), wd_bf[...]).astype(jnp.bfloat16)

    @pl.when(j >= nused_ref[0])
    def _():
        y_ref[...] = jnp.zeros_like(y_ref)


def _moe(texp, first, nused, slot, nxt, xs, w_gate, w_up, w_down):
    p_rows = xs.shape[0]
    n_tiles = p_rows // ROW_TILE
    used = lambda j, nu: jnp.minimum(j, nu[0] - 1)
    grid_spec = pltpu.PrefetchScalarGridSpec(
        num_scalar_prefetch=5,
        grid=(n_tiles,),
        in_specs=[
            pl.BlockSpec((ROW_TILE, D_MODEL), lambda j, te, fi, nu, sl, nx: (used(j, nu), 0)),
            pl.BlockSpec(memory_space=pl.ANY),
            pl.BlockSpec(memory_space=pl.ANY),
            pl.BlockSpec(memory_space=pl.ANY),
        ],
        out_specs=pl.BlockSpec((ROW_TILE, D_MODEL), lambda j, te, fi, nu, sl, nx: (j, 0)),
        scratch_shapes=[pltpu.VMEM((2, D_MODEL, D_EXPERT), jnp.float32),
                        pltpu.VMEM((2, D_MODEL, D_EXPERT), jnp.float32),
                        pltpu.VMEM((2, D_EXPERT, D_MODEL), jnp.float32),
                        pltpu.VMEM((D_MODEL, D_EXPERT), jnp.bfloat16),
                        pltpu.VMEM((D_MODEL, D_EXPERT), jnp.bfloat16),
                        pltpu.VMEM((D_EXPERT, D_MODEL), jnp.bfloat16),
                        pltpu.SemaphoreType.DMA((2, 3))],
    )
    return pl.pallas_call(
        _moe_kernel,
        out_shape=jax.ShapeDtypeStruct((p_rows, D_MODEL), jnp.bfloat16),
        grid_spec=grid_spec,
        compiler_params=pltpu.CompilerParams(dimension_semantics=("arbitrary",),
                                             vmem_limit_bytes=VMEM_LIMIT),
        name="moe_experts",
    )(texp, first, nused, slot, nxt, xs, w_gate, w_up, w_down)


def _final_kernel(lo_ref, nchunk_ref, dst_ref, y_hbm, x1_ref, rc_ref, p_ref, gple_ref, wg_ref, wp_ref,
                  gfin_ref, out_ref, lower, ybuf, sems):
    i = pl.program_id(0)
    n_steps = pl.num_programs(0)
    slot = i % 2

    @pl.when(i == 0)
    def _():
        r = lax.broadcasted_iota(jnp.int32, (TOKEN_TILE, TOKEN_TILE), 0)
        c = lax.broadcasted_iota(jnp.int32, (TOKEN_TILE, TOKEN_TILE), 1)
        lower[...] = jnp.where(c < r, 1.0, 0.0).astype(jnp.bfloat16)
        ybuf[...] = jnp.zeros_like(ybuf)
        _start_chunks(nchunk_ref, dst_ref, 0, ybuf.at[0], y_hbm, sems.at[0], False)

    for s in range(2):
        @pl.when((slot == s) & (i + 1 < n_steps))
        def _():
            _start_chunks(nchunk_ref, dst_ref, i + 1, ybuf.at[1 - s], y_hbm, sems.at[1 - s], False)

    pp = _dot(p_ref[...].astype(jnp.bfloat16), wp_ref[...])
    rc = rc_ref[...]
    w0, w1 = rc[:, 0:1], rc[:, 1:2]
    e0, e1 = rc[:, 2:3].astype(jnp.int32), rc[:, 3:4].astype(jnp.int32)
    ids = lax.broadcasted_iota(jnp.int32, (TOKEN_TILE, ROUTER_ROWS), 1)
    member = jnp.where((ids == e0) | (ids == e1), 1.0, 0.0)
    prior = _dot(lower[...], member.astype(jnp.bfloat16))
    lo_vec = _lo_vector(lo_ref, i, (1, ROUTER_ROWS), 1)
    d0 = _local_slots(member, lo_vec, prior, e0, 1)
    d1 = _local_slots(member, lo_vec, prior, e1, 1)
    cols = lax.broadcasted_iota(jnp.int32, (TOKEN_TILE, LOCAL_ROWS), 1)
    pick0 = jnp.where(cols == d0, 1.0, 0.0).astype(jnp.bfloat16)
    pick1 = jnp.where(cols == d1, 1.0, 0.0).astype(jnp.bfloat16)

    for s in range(2):
        @pl.when(slot == s)
        def _():
            _wait_chunks(nchunk_ref, dst_ref, i, ybuf.at[s], y_hbm, sems.at[s], False)
            yb = ybuf[s]
            x2 = x1_ref[...] + (w0 * _dot(pick0, yb) + w1 * _dot(pick1, yb))
            z = _dot(_rms(x2, gple_ref[...]).astype(jnp.bfloat16), wg_ref[...])
            gate = 1.0 / (1.0 + jnp.exp(-z))
            x3 = x2 + pp * gate
            out_ref[...] = _rms(x3, gfin_ref[...])


def _final(lo, nchunk, dst, y, x1, rc, p2d, g_ple, w_gate_bf, w_proj_bf, g_final):
    t = x1.shape[0]
    tm = TOKEN_TILE
    row = lambda w: pl.BlockSpec((tm, w), lambda i, *_: (i, 0))
    full = lambda a: pl.BlockSpec(a.shape, lambda i, *_: (0, 0))
    grid_spec = pltpu.PrefetchScalarGridSpec(
        num_scalar_prefetch=3,
        grid=(t // tm,),
        in_specs=[pl.BlockSpec(memory_space=pl.ANY), row(D_MODEL), row(EXPERTS_PER_GROUP),
                  row(PLE_DIM), full(g_ple), full(w_gate_bf), full(w_proj_bf), full(g_final)],
        out_specs=row(D_MODEL),
        scratch_shapes=[pltpu.VMEM((TOKEN_TILE, TOKEN_TILE), jnp.bfloat16),
                        pltpu.VMEM((2, LOCAL_ROWS, D_MODEL), jnp.bfloat16),
                        pltpu.SemaphoreType.DMA((2,))],
    )
    return pl.pallas_call(
        _final_kernel,
        out_shape=jax.ShapeDtypeStruct((t, D_MODEL), jnp.float32),
        grid_spec=grid_spec,
        compiler_params=pltpu.CompilerParams(dimension_semantics=("arbitrary",),
                                             vmem_limit_bytes=VMEM_LIMIT),
        name="combine_ple_final",
    )(lo, nchunk, dst, y, x1, rc, p2d, g_ple, w_gate_bf, w_proj_bf, g_final)


def _dispatch_tables(route_i, n_tok):
    n_tt = n_tok // TOKEN_TILE
    p_rows = 2 * n_tok + n_tt * N_EXPERTS * (RUN_ALIGN - 1) + N_EXPERTS * ROW_TILE
    p_rows = -(-p_rows // ROW_TILE) * ROW_TILE
    n_tiles = p_rows // ROW_TILE
    e = route_i[:2].reshape(2, n_tt, TOKEN_TILE)
    onehot = (e[..., None] == jnp.arange(N_EXPERTS, dtype=jnp.int32)).astype(jnp.int32)
    counts = jnp.sum(onehot, axis=(0, 2))
    n8 = (counts + RUN_ALIGN - 1) // RUN_ALIGN * RUN_ALIGN
    lo = jnp.cumsum(n8, axis=1) - n8
    total = jnp.sum(n8, axis=0)
    tiles_per = (total + ROW_TILE - 1) // ROW_TILE
    tile_end = jnp.cumsum(tiles_per)
    tile_start = tile_end - tiles_per
    g = tile_start[None, :] * ROW_TILE + (jnp.cumsum(n8, axis=0) - n8)
    n_used = tile_end[-1]
    tile_ids = jnp.arange(n_tiles, dtype=jnp.int32)
    clamped = jnp.minimum(tile_ids, n_used - 1)
    texp = jnp.sum((tile_end[None, :] <= clamped[:, None]).astype(jnp.int32), axis=1)
    first = ((tile_ids == tile_start[texp]) & (tile_ids < n_used)).astype(jnp.int32)
    pad_tile = jnp.where(total > 0, tile_end - 1, -1).astype(jnp.int32)
    has_rows = total > 0
    experts = jnp.arange(N_EXPERTS, dtype=jnp.int32)
    later = has_rows[None, :] & (experts[None, :] > experts[:, None])
    next_expert = jnp.min(jnp.where(later, experts[None, :], N_EXPERTS), axis=1)
    next_expert = jnp.where(next_expert < N_EXPERTS, next_expert, -1)
    slot = (jnp.cumsum(has_rows.astype(jnp.int32)) - 1) % 2
    chunk_row = jnp.arange(MAX_CHUNKS, dtype=jnp.int32)[None, :, None] * RUN_ALIGN
    in_run = (chunk_row >= lo[:, None, :]) & (chunk_row < (lo + n8)[:, None, :])
    dst = jnp.sum(jnp.where(in_run, g[:, None, :] + chunk_row - lo[:, None, :], 0), axis=2)
    nchunk = jnp.sum(n8, axis=1) // RUN_ALIGN
    i32 = lambda a: a.reshape(-1).astype(jnp.int32)
    return (texp, first, n_used.reshape(1).astype(jnp.int32), i32(slot[texp]), i32(next_expert[texp]),
            pad_tile, i32(lo), i32(nchunk), i32(dst), p_rows)


def kernel(x, p, g_mix, w_in, sink, g_grp_a, g_grp_b, w_out, g_ffn, w_router_group, b_router_group,
           w_router_expert, b_router_expert, w_expert_gate, w_expert_up, w_expert_down, g_ple,
           w_ple_gate, w_ple_proj, g_final):
    batch, seq, _ = x.shape
    n_tok = batch * seq
    depth = w_in.shape[0]
    assert depth == 1, "the final RMSNorm is fused into the last layer's epilogue"
    bf = jnp.bfloat16
    xc = x.reshape(n_tok, D_MODEL)
    for i in range(depth):
        *qkv_a, qb, kb, vb = _proj(xc, g_mix[i][None, :], w_in[i].astype(bf), batch, seq)
        oa = _attn_a(qkv_a, batch, seq)
        ob = _attn_b(qb, kb, vb, sink[i], batch, seq)

        wr = jnp.zeros((D_MODEL, ROUTER_ROWS), jnp.float32)
        wr = wr.at[:, 0:N_GROUPS].set(w_router_group[i])
        wr = wr.at[:, EXPERTS_PER_GROUP:EXPERTS_PER_GROUP + N_EXPERTS].set(w_router_expert[i])
        br = jnp.zeros((ROUTER_ROWS, 1), jnp.float32)
        br = br.at[0:N_GROUPS, 0].set(b_router_group[i])
        br = br.at[EXPERTS_PER_GROUP:EXPERTS_PER_GROUP + N_EXPERTS, 0].set(b_router_expert[i])

        x1, h2, route_i, route_c = _mix(
            oa, ob, xc, g_grp_a[i][None, :], g_grp_b[i][None, :], w_out[i].astype(bf), g_ffn[i][None, :], wr, br)

        texp, first, nused, slot, nxt, pad_tile, lo, nchunk, dst, p_rows = _dispatch_tables(route_i, n_tok)
        xs = _dispatch(lo, nchunk, dst, pad_tile, nused, route_i, h2, p_rows)
        y = _moe(texp, first, nused, slot, nxt, xs, w_expert_gate[i], w_expert_up[i], w_expert_down[i])
        xc = _final(lo, nchunk, dst, y, x1, route_c, p[i].reshape(n_tok, PLE_DIM), g_ple[i][None, :],
                    w_ple_gate[i].astype(bf), w_ple_proj[i].astype(bf), g_final[None, :])
    return xc.reshape(batch, seq, D_MODEL)
```

```python
import functools

import jax
import jax.numpy as jnp
from jax import lax
from jax.experimental import pallas as pl
from jax.experimental.pallas import tpu as pltpu

D_MODEL = 1024
HEAD_DIM = 64
GROUP_WIDTH = 512
N_HEADS = 8
B_KV_HEADS = 2
B_KV_WIDTH = B_KV_HEADS * HEAD_DIM
A_CONFIGS = ((128, 1), (512, 4), (2048, 16))
B_HALF_WINDOW = 128
N_GROUPS = 4
EXPERTS_PER_GROUP = 8
N_EXPERTS = N_GROUPS * EXPERTS_PER_GROUP
D_EXPERT = 512
PLE_DIM = 256
EPS = 1e-6
NEG = -1e30
LOG2E = 1.4426950408889634
LN2 = 0.6931471805599453

LANES = 128
Q_BLOCK = 128
HEADS_PER_PASS = 2
PASS_WIDTH = HEADS_PER_PASS * HEAD_DIM
ROW_TILE = 256
TOKEN_TILE = 512
ROUTER_ROWS = 128
VMEM_LIMIT = 48 * 1024 * 1024
ATTN_VMEM_LIMIT = 56 * 1024 * 1024


def _rms(x, g):
    r = lax.rsqrt(jnp.mean(x * x, axis=-1, keepdims=True) + EPS)
    return (x * r) * g


def _dot(a, b):
    return jnp.dot(a, b, preferred_element_type=jnp.float32)


PROJ_TILE = 512


def _proj_kernel(x_ref, g_ref, w_ref, *refs):
    a_refs, (qb_ref, kb_ref, vb_ref, res, *stage_refs) = refs[:9], refs[9:]
    h = _rms(x_ref[...], g_ref[...]).astype(jnp.bfloat16)
    scale = HEAD_DIM ** -0.5 * LOG2E
    W = GROUP_WIDTH
    tiles = GROUP_WIDTH // PASS_WIDTH
    qkv_a = (_dot(h, w_ref[:, 0:W]) * scale, _dot(h, w_ref[:, W:2 * W]), _dot(h, w_ref[:, 2 * W:3 * W]))
    for part in range(3):
        for pair in range(tiles):
            res[part * tiles + pair, 0] = qkv_a[part][:, pair * PASS_WIDTH:(pair + 1) * PASS_WIDTH]
    qb_ref[...] = (_dot(h, w_ref[:, 3 * W:4 * W]) * scale).astype(jnp.bfloat16)
    kb_ref[...] = _dot(h, w_ref[:, 4 * W:4 * W + B_KV_WIDTH]).astype(jnp.bfloat16)
    vb_ref[...] = _dot(h, w_ref[:, 4 * W + B_KV_WIDTH:4 * W + 2 * B_KV_WIDTH]).astype(jnp.bfloat16)
    stages = (res,) + tuple(stage_refs)
    prev_dil = 1
    for c, (_, dil) in enumerate(A_CONFIGS):
        ratio = dil // prev_dil
        rows = PROJ_TILE // dil
        for idx in range(3 * tiles):
            part, pair = idx // tiles, idx % tiles
            out = a_refs[3 * c + part]
            for rho in range(dil):
                if c == 0:
                    sub = res[idx, 0]
                else:
                    sub = stages[c - 1].at[idx, rho % prev_dil][pl.ds(rho // prev_dil, rows, stride=ratio), :]
                    if c < len(A_CONFIGS) - 1:
                        stages[c][idx, rho] = sub
                out[pair, rho] = sub.astype(jnp.bfloat16)
        prev_dil = dil


def _proj(x2d, g_mix, w_in_bf, batch, seq):
    t = x2d.shape[0]
    tm = PROJ_TILE
    tiles_per_seq = seq // tm
    in_width = w_in_bf.shape[1]
    n_pairs = N_HEADS // HEADS_PER_PASS
    row = lambda w: pl.BlockSpec((tm, w), lambda i: (i, 0))
    a_shapes, a_specs = [], []
    for _, dil in A_CONFIGS:
        for _ in range(3):
            a_shapes.append(jax.ShapeDtypeStruct((batch, n_pairs, dil, seq // dil, PASS_WIDTH),
                                                 jnp.bfloat16))
            a_specs.append(pl.BlockSpec((None, n_pairs, dil, tm // dil, PASS_WIDTH),
                                        lambda i: (i // tiles_per_seq, 0, 0, i % tiles_per_seq, 0)))
    wide = jax.ShapeDtypeStruct((t, GROUP_WIDTH), jnp.bfloat16)
    narrow = jax.ShapeDtypeStruct((t, B_KV_WIDTH), jnp.bfloat16)
    return pl.pallas_call(
        _proj_kernel,
        out_shape=tuple(a_shapes) + (wide, narrow, narrow),
        grid=(t // tm,),
        in_specs=[row(D_MODEL),
                  pl.BlockSpec((1, D_MODEL), lambda i: (0, 0)),
                  pl.BlockSpec((D_MODEL, in_width), lambda i: (0, 0))],
        out_specs=tuple(a_specs) + (row(GROUP_WIDTH), row(B_KV_WIDTH), row(B_KV_WIDTH)),
        scratch_shapes=[pltpu.VMEM((3 * GROUP_WIDTH // PASS_WIDTH, dil, tm // dil, PASS_WIDTH), jnp.float32)
                        for _, dil in A_CONFIGS[:-1]],
        compiler_params=pltpu.CompilerParams(dimension_semantics=("arbitrary",),
                                             vmem_limit_bytes=VMEM_LIMIT),
        name="proj",
    )(x2d, g_mix, w_in_bf)


def _fill_bias(bias_ref, half, nkeys, dist_scale):
    i = lax.broadcasted_iota(jnp.int32, (Q_BLOCK, nkeys), 0)
    j = lax.broadcasted_iota(jnp.int32, (Q_BLOCK, nkeys), 1)
    for var, rel0 in enumerate((0, -half, Q_BLOCK - nkeys)):
        d = jnp.abs(j + rel0 - i)
        dist = d.astype(jnp.float32) * dist_scale
        for h in range(N_HEADS):
            slope = 2.0 ** (-(h + 1))
            bias_ref[var, h] = jnp.where(d <= half, -(slope * dist) * LOG2E, NEG)


def _key_window(q0, seq_len, half, nkeys):
    lo = q0 - half
    ks = pl.multiple_of(jnp.clip(lo, 0, seq_len - nkeys), HEAD_DIM)
    var = jnp.where(lo < 0, 0, jnp.where(lo > seq_len - nkeys, 2, 1))
    return ks, var


def _fold_lane_tiles(a, op):
    out = a[..., 0:LANES]
    for t in range(1, a.shape[-1] // LANES):
        out = op(out, a[..., t * LANES:(t + 1) * LANES])
    return out


def _attend_pass(q, k, v, bias, sinks):
    nkeys = k.shape[0]
    lane_head = lax.broadcasted_iota(jnp.int32, (Q_BLOCK, PASS_WIDTH), 1) // HEAD_DIM
    zero = jnp.zeros_like(q)
    q_stack = jnp.concatenate(
        [jnp.where(lane_head == h, q, zero) for h in range(HEADS_PER_PASS)], axis=0)
    s = lax.dot_general(q_stack, k, (((1,), (1,)), ((), ())), preferred_element_type=jnp.float32)
    s = s.reshape(HEADS_PER_PASS, Q_BLOCK, nkeys) + bias
    m_tile = _fold_lane_tiles(s, jnp.maximum)
    if sinks is not None:
        tile_head = lax.broadcasted_iota(jnp.int32, m_tile.shape, 0)
        tile_lane = lax.broadcasted_iota(jnp.int32, m_tile.shape, 2)
        sk = jnp.zeros(m_tile.shape, jnp.float32)
        for h in range(HEADS_PER_PASS):
            sk = jnp.where(tile_head == h, sinks[h] * LOG2E, sk)
        m_tile = jnp.maximum(m_tile, sk)
    m = jnp.max(m_tile, axis=-1, keepdims=True)
    p = jnp.exp2(s - m)
    den_tile = _fold_lane_tiles(p, jnp.add)
    if sinks is not None:
        den_tile = den_tile + jnp.where(tile_lane == 0, jnp.exp2(sk - m), 0.0)
    den = jnp.sum(den_tile, axis=-1, keepdims=True)
    pv = _dot(p.reshape(HEADS_PER_PASS * Q_BLOCK, nkeys).astype(jnp.bfloat16), v)
    pv = pv.reshape(HEADS_PER_PASS, Q_BLOCK, PASS_WIDTH) * (1.0 / den)
    lse = m * LN2 + jnp.log(den)
    o = jnp.zeros((Q_BLOCK, PASS_WIDTH), jnp.float32)
    lse_lanes = jnp.zeros((Q_BLOCK, PASS_WIDTH), jnp.float32)
    for h in range(HEADS_PER_PASS):
        o = jnp.where(lane_head == h, pv[h], o)
        lse_lanes = jnp.where(lane_head == h, lse[h], lse_lanes)
    return o, lse_lanes


A_HALF = 64
A_KEYS = Q_BLOCK + 2 * A_HALF
PASSES_PER_BODY = 16


def _attn_a_kernel(*refs, seq):
    n_cfg = len(A_CONFIGS)
    qkv = refs[:3 * n_cfg]
    o_ref, bias_ref = refs[3 * n_cfg], refs[3 * n_cfg + 1]
    scratch = refs[3 * n_cfg + 2:]
    pair = pl.program_id(1)

    @pl.when((pl.program_id(0) == 0) & (pair == 0))
    def _():
        for c, (_, dil) in enumerate(A_CONFIGS):
            _fill_bias(bias_ref.at[c], A_HALF, A_KEYS, float(dil))

    for c, (window, dil) in enumerate(A_CONFIGS):
        assert window // (2 * dil) == A_HALF
        q_ref, k_ref, v_ref = qkv[3 * c:3 * c + 3]
        o_sc, l_sc = scratch[2 * c], scratch[2 * c + 1]
        sub_len = seq // dil
        blocks = sub_len // Q_BLOCK

        def body(idx, carry, q_ref=q_ref, k_ref=k_ref, v_ref=v_ref, o_sc=o_sc, l_sc=l_sc, c=c,
                 sub_len=sub_len, blocks=blocks):
            rho = idx // blocks
            q0 = pl.multiple_of((idx % blocks) * Q_BLOCK, Q_BLOCK)
            ks, var = _key_window(q0, sub_len, A_HALF, A_KEYS)
            bias = bias_ref[c, var, pl.ds(pair * HEADS_PER_PASS, HEADS_PER_PASS)]
            o, lse = _attend_pass(q_ref[rho, pl.ds(q0, Q_BLOCK), :], k_ref[rho, pl.ds(ks, A_KEYS), :],
                                  v_ref[rho, pl.ds(ks, A_KEYS), :], bias, None)
            o_sc[rho, pl.ds(q0, Q_BLOCK), :] = o
            l_sc[rho, pl.ds(q0, Q_BLOCK), :] = lse
            return carry

        lax.fori_loop(0, dil * blocks, body, 0, unroll=PASSES_PER_BODY)

    widest = A_CONFIGS[-1][1]
    rows = seq // widest
    for r in range(widest):
        outs, lses = [], []
        for c, (_, dil) in enumerate(A_CONFIGS):
            step = widest // dil
            idx = (r % dil, pl.ds(r // dil, rows, stride=step) if step > 1 else pl.ds(0, rows))
            outs.append(scratch[2 * c].at[idx[0]][idx[1], :])
            lses.append(scratch[2 * c + 1].at[idx[0]][idx[1], :])
        mx = functools.reduce(jnp.maximum, lses)
        es = [jnp.exp(l - mx) for l in lses]
        inv = 1.0 / functools.reduce(jnp.add, es)
        merged = functools.reduce(jnp.add, [(e * inv) * o for e, o in zip(es, outs)])
        o_ref[pl.ds(r, rows, stride=widest), :] = merged


def _attn_a(qkv, batch, seq):
    n_pairs = N_HEADS // HEADS_PER_PASS
    in_specs, scratch = [], [pltpu.VMEM((len(A_CONFIGS), 3, N_HEADS, Q_BLOCK, A_KEYS), jnp.float32)]
    for _, dil in A_CONFIGS:
        blk = (None, None, dil, seq // dil, PASS_WIDTH)
        in_specs += [pl.BlockSpec(blk, lambda b, p: (b, p, 0, 0, 0))] * 3
        scratch += [pltpu.VMEM((dil, seq // dil, PASS_WIDTH), jnp.float32)] * 2
    return pl.pallas_call(
        functools.partial(_attn_a_kernel, seq=seq),
        out_shape=jax.ShapeDtypeStruct((batch, seq, GROUP_WIDTH), jnp.float32),
        grid=(batch, n_pairs),
        in_specs=in_specs,
        out_specs=pl.BlockSpec((None, seq, PASS_WIDTH), lambda b, p: (b, 0, p)),
        scratch_shapes=scratch,
        compiler_params=pltpu.CompilerParams(dimension_semantics=("arbitrary", "arbitrary"),
                                             vmem_limit_bytes=ATTN_VMEM_LIMIT),
        name="attn_a",
    )(*qkv).reshape(batch * seq, GROUP_WIDTH)


def _attn_b_kernel(sink_ref, q_ref, k_ref, v_ref, o_ref, bias_ref, k4_ref, v4_ref, *, half, nkeys,
                   seq_len, q_tile):
    first = (pl.program_id(0) == 0) & (pl.program_id(1) == 0)

    @pl.when(first)
    def _():
        _fill_bias(bias_ref, half, nkeys, 1.0)

    @pl.when(pl.program_id(1) == 0)
    def _():
        c = lax.broadcasted_iota(jnp.int32, (B_KV_WIDTH, GROUP_WIDTH), 0)
        j = lax.broadcasted_iota(jnp.int32, (B_KV_WIDTH, GROUP_WIDTH), 1)
        src = (j // (GROUP_WIDTH // B_KV_HEADS)) * HEAD_DIM + j % HEAD_DIM
        rep = jnp.where(c == src, 1.0, 0.0).astype(jnp.bfloat16)
        chunk = 512

        def body(i, carry):
            rows = pl.ds(pl.multiple_of(i * chunk, chunk), chunk)
            k4_ref[rows, :] = _dot(k_ref[rows, :], rep).astype(jnp.bfloat16)
            v4_ref[rows, :] = _dot(v_ref[rows, :], rep).astype(jnp.bfloat16)
            return carry

        lax.fori_loop(0, seq_len // chunk, body, 0)

    base = pl.program_id(1) * q_tile

    def body(jb, carry):
        row0 = pl.multiple_of(jb * Q_BLOCK, Q_BLOCK)
        ks, var = _key_window(base + row0, seq_len, half, nkeys)
        for g in range(N_HEADS // HEADS_PER_PASS):
            cols = slice(g * PASS_WIDTH, (g + 1) * PASS_WIDTH)
            heads = pl.ds(g * HEADS_PER_PASS, HEADS_PER_PASS)
            sinks = [sink_ref[g * HEADS_PER_PASS + h] for h in range(HEADS_PER_PASS)]
            o, _ = _attend_pass(q_ref[pl.ds(row0, Q_BLOCK), cols], k4_ref[pl.ds(ks, nkeys), cols],
                                v4_ref[pl.ds(ks, nkeys), cols], bias_ref[var, heads], sinks)
            o_ref[pl.ds(row0, Q_BLOCK), cols] = o
        return carry

    lax.fori_loop(0, q_tile // Q_BLOCK, body, 0,
                  unroll=PASSES_PER_BODY // (N_HEADS // HEADS_PER_PASS))


def _attn_b(q, k, v, sink, batch, seq):
    half = B_HALF_WINDOW
    nkeys = Q_BLOCK + 2 * half
    q_tile = 1024
    view = lambda a: a.reshape(batch, seq, a.shape[-1])
    kern = functools.partial(_attn_b_kernel, half=half, nkeys=nkeys, seq_len=seq, q_tile=q_tile)
    o = pl.pallas_call(
        kern,
        out_shape=jax.ShapeDtypeStruct((batch, seq, GROUP_WIDTH), jnp.float32),
        grid=(batch, seq // q_tile),
        in_specs=[pl.BlockSpec(memory_space=pltpu.SMEM),
                  pl.BlockSpec((None, q_tile, GROUP_WIDTH), lambda b, i: (b, i, 0)),
                  pl.BlockSpec((None, seq, B_KV_WIDTH), lambda b, i: (b, 0, 0)),
                  pl.BlockSpec((None, seq, B_KV_WIDTH), lambda b, i: (b, 0, 0))],
        out_specs=pl.BlockSpec((None, q_tile, GROUP_WIDTH), lambda b, i: (b, i, 0)),
        scratch_shapes=[pltpu.VMEM((3, N_HEADS, Q_BLOCK, nkeys), jnp.float32),
                        pltpu.VMEM((seq, GROUP_WIDTH), jnp.bfloat16),
                        pltpu.VMEM((seq, GROUP_WIDTH), jnp.bfloat16)],
        compiler_params=pltpu.CompilerParams(dimension_semantics=("arbitrary", "arbitrary"),
                                             vmem_limit_bytes=VMEM_LIMIT),
        name="attn_b",
    )(sink, view(q), view(k), view(v))
    return o.reshape(batch * seq, GROUP_WIDTH)


def _mix_kernel(oa_ref, ob_ref, x_ref, ga_ref, gb_ref, wout_ref, gffn_ref, wr_ref, br_ref,
                x1_ref, h2_ref, ri_ref, rc_ref, cnt_ref):
    tm = x_ref.shape[0]
    na = _rms(oa_ref[...], ga_ref[...]).astype(jnp.bfloat16)
    nb = _rms(ob_ref[...], gb_ref[...]).astype(jnp.bfloat16)
    x1 = (x_ref[...] + _dot(na, wout_ref[0:GROUP_WIDTH, :])
          + _dot(nb, wout_ref[GROUP_WIDTH:2 * GROUP_WIDTH, :]))
    x1_ref[...] = x1
    h2 = _rms(x1, gffn_ref[...])
    h2_ref[...] = h2.astype(jnp.bfloat16)

    h_hi = h2.astype(jnp.bfloat16)
    h_lo = (h2 - h_hi.astype(jnp.float32)).astype(jnp.bfloat16)
    wr = wr_ref[...]
    w_hi = wr.astype(jnp.bfloat16)
    w_lo = (wr - w_hi.astype(jnp.float32)).astype(jnp.bfloat16)
    w_cat = jnp.concatenate([w_hi, w_lo], axis=1)
    lg2 = _dot(h_hi, w_cat) + _dot(h_lo, w_cat)
    lg = lg2[:, 0:ROUTER_ROWS] + lg2[:, ROUTER_ROWS:2 * ROUTER_ROWS]
    lgt = lg.T + br_ref[...]

    rows = lax.broadcasted_iota(jnp.int32, (EXPERTS_PER_GROUP, tm), 0)
    big = jnp.int32(1 << 20)
    glog = jnp.where(rows < N_GROUPS, lgt[0:EXPERTS_PER_GROUP], -jnp.inf)
    gmax = jnp.max(glog, axis=0, keepdims=True)
    gsel = jnp.min(jnp.where(glog == gmax, rows, big), axis=0, keepdims=True)
    gw = 1.0 / jnp.sum(jnp.exp(glog - gmax), axis=0, keepdims=True)
    esel = jnp.zeros((EXPERTS_PER_GROUP, tm), jnp.float32)
    for grp in range(N_GROUPS):
        lo = EXPERTS_PER_GROUP * (grp + 1)
        esel = jnp.where(gsel == grp, lgt[lo:lo + EXPERTS_PER_GROUP], esel)
    v0 = jnp.max(esel, axis=0, keepdims=True)
    i0 = jnp.min(jnp.where(esel == v0, rows, big), axis=0, keepdims=True)
    rest = jnp.where(rows == i0, -jnp.inf, esel)
    v1 = jnp.max(rest, axis=0, keepdims=True)
    i1 = jnp.min(jnp.where(rest == v1, rows, big), axis=0, keepdims=True)
    e = jnp.exp(v1 - v0)
    w0 = (1.0 / (1.0 + e)) * gw
    w1 = (e / (1.0 + e)) * gw
    eid0 = gsel * EXPERTS_PER_GROUP + i0
    eid1 = gsel * EXPERTS_PER_GROUP + i1
    ri_ref[...] = jnp.where(rows == 0, eid0, jnp.where(rows == 1, eid1, 0))
    rows_t = lax.broadcasted_iota(jnp.int32, (ROUTER_ROWS, tm), 0)
    slab = jnp.where(rows_t == 0, w0, jnp.where(rows_t == 1, w1, 0.0))
    slab = jnp.where(rows_t == 2, eid0.astype(jnp.float32),
                     jnp.where(rows_t == 3, eid1.astype(jnp.float32), slab))
    rc_ref[...] = slab.T[:, 0:EXPERTS_PER_GROUP]
    ids = lax.broadcasted_iota(jnp.int32, (N_EXPERTS, tm), 0)
    member = jnp.where((ids == eid0) | (ids == eid1), 1.0, 0.0)
    cnt_ref[...] = jnp.sum(member, axis=1, keepdims=True).astype(jnp.int32)


def _mix(oa, ob, x2d, g_a, g_b, w_out_bf, g_ffn, wr, br):
    t = x2d.shape[0]
    tm = TOKEN_TILE
    row = lambda w: pl.BlockSpec((tm, w), lambda i: (i, 0))
    full = lambda a: pl.BlockSpec(a.shape, lambda i: (0, 0))
    return pl.pallas_call(
        _mix_kernel,
        out_shape=(jax.ShapeDtypeStruct((t, D_MODEL), jnp.float32),
                   jax.ShapeDtypeStruct((t, D_MODEL), jnp.bfloat16),
                   jax.ShapeDtypeStruct((EXPERTS_PER_GROUP, t), jnp.int32),
                   jax.ShapeDtypeStruct((t, EXPERTS_PER_GROUP), jnp.float32),
                   jax.ShapeDtypeStruct((t // tm, N_EXPERTS, 1), jnp.int32)),
        grid=(t // tm,),
        in_specs=[row(GROUP_WIDTH), row(GROUP_WIDTH), row(D_MODEL),
                  full(g_a), full(g_b), full(w_out_bf), full(g_ffn), full(wr), full(br)],
        out_specs=(row(D_MODEL), row(D_MODEL),
                   pl.BlockSpec((EXPERTS_PER_GROUP, tm), lambda i: (0, i)),
                   row(EXPERTS_PER_GROUP),
                   pl.BlockSpec((None, N_EXPERTS, 1), lambda i: (i, 0, 0))),
        compiler_params=pltpu.CompilerParams(dimension_semantics=("arbitrary",),
                                             vmem_limit_bytes=VMEM_LIMIT),
        name="mix_router",
    )(oa, ob, x2d, g_a, g_b, w_out_bf, g_ffn, wr, br)


RUN_ALIGN = 16
LOCAL_ROWS = 2 * TOKEN_TILE + N_EXPERTS * RUN_ALIGN
MAX_CHUNKS = LOCAL_ROWS // RUN_ALIGN


def _local_slots(member, lo_vec, prior, e_sel, axis):
    ids = lax.broadcasted_iota(jnp.int32, member.shape, axis)
    slot = jnp.where(ids == e_sel, lo_vec + prior, 0.0)
    return jnp.sum(slot, axis=axis, keepdims=True).astype(jnp.int32)


def _chunk_copy(dst_ref, tile, c, local, glob, sem, to_global):
    g = pl.multiple_of(dst_ref[tile * MAX_CHUNKS + c], RUN_ALIGN)
    l = pl.multiple_of(c * RUN_ALIGN, RUN_ALIGN)
    src, dst = local.at[pl.ds(l, RUN_ALIGN), :], glob.at[pl.ds(g, RUN_ALIGN), :]
    if not to_global:
        src, dst = dst, src
    return pltpu.make_async_copy(src, dst, sem)


def _start_chunks(nchunk_ref, dst_ref, tile, local, glob, sem, to_global):
    def body(c, carry):
        _chunk_copy(dst_ref, tile, c, local, glob, sem, to_global).start()
        return carry
    lax.fori_loop(0, nchunk_ref[tile], body, 0)


def _wait_chunks(nchunk_ref, dst_ref, tile, local, glob, sem, to_global):
    def body(c, carry):
        _chunk_copy(dst_ref, tile, c, local, glob, sem, to_global).wait()
        return carry
    lax.fori_loop(0, nchunk_ref[tile], body, 0)


def _lo_vector(lo_ref, tile, shape, axis):
    ids = lax.broadcasted_iota(jnp.int32, shape, axis)
    vec = jnp.zeros(shape, jnp.float32)
    for e in range(N_EXPERTS):
        vec = jnp.where(ids == e, lo_ref[tile * N_EXPERTS + e].astype(jnp.float32), vec)
    return vec


def _dispatch_kernel(lo_ref, nchunk_ref, dst_ref, pad_tile_ref, nused_ref, ri_ref, h_ref, xs_hbm,
                     upper, sbuf, zbuf, sem, zsem, *, n_tiles):
    i = pl.program_id(0)
    n_steps = pl.num_programs(0)
    slot = i % 2

    def zero_tile(tile):
        rows = pl.ds(pl.multiple_of(tile * ROW_TILE, ROW_TILE), ROW_TILE)
        return pltpu.make_async_copy(zbuf, xs_hbm.at[rows, :], zsem)

    @pl.when(i == 0)
    def _():
        r = lax.broadcasted_iota(jnp.int32, (TOKEN_TILE, TOKEN_TILE), 0)
        c = lax.broadcasted_iota(jnp.int32, (TOKEN_TILE, TOKEN_TILE), 1)
        upper[...] = jnp.where(r < c, 1.0, 0.0).astype(jnp.bfloat16)
        zbuf[...] = jnp.zeros_like(zbuf)
        for e in range(N_EXPERTS):
            @pl.when(pad_tile_ref[e] >= 0)
            def _():
                zero_tile(pad_tile_ref[e]).start()
        lax.fori_loop(nused_ref[0], n_tiles, lambda t, c: (zero_tile(t).start(), c)[1], 0)
        for e in range(N_EXPERTS):
            @pl.when(pad_tile_ref[e] >= 0)
            def _():
                zero_tile(pad_tile_ref[e]).wait()
        lax.fori_loop(nused_ref[0], n_tiles, lambda t, c: (zero_tile(t).wait(), c)[1], 0)

    e0 = ri_ref[0:1, :]
    e1 = ri_ref[1:2, :]
    ids = lax.broadcasted_iota(jnp.int32, (N_EXPERTS, TOKEN_TILE), 0)
    member = jnp.where((ids == e0) | (ids == e1), 1.0, 0.0)
    prior = _dot(member.astype(jnp.bfloat16), upper[...])
    lo_vec = _lo_vector(lo_ref, i, (N_EXPERTS, 1), 0)
    d0 = _local_slots(member, lo_vec, prior, e0, 0)
    d1 = _local_slots(member, lo_vec, prior, e1, 0)
    rows = lax.broadcasted_iota(jnp.int32, (LOCAL_ROWS, TOKEN_TILE), 0)
    select = jnp.where((rows == d0) | (rows == d1), 1.0, 0.0).astype(jnp.bfloat16)
    sorted_rows = _dot(select, h_ref[...])

    for s in range(2):
        @pl.when(slot == s)
        def _():
            sbuf[s] = sorted_rows.astype(jnp.bfloat16)

            @pl.when(i > 0)
            def _():
                _wait_chunks(nchunk_ref, dst_ref, i - 1, sbuf.at[1 - s], xs_hbm, sem, True)

            _start_chunks(nchunk_ref, dst_ref, i, sbuf.at[s], xs_hbm, sem, True)

            @pl.when(i == n_steps - 1)
            def _():
                _wait_chunks(nchunk_ref, dst_ref, i, sbuf.at[s], xs_hbm, sem, True)


def _dispatch(lo, nchunk, dst, pad_tile, nused, route_i, h2, p_rows):
    n_tok = h2.shape[0]
    grid_spec = pltpu.PrefetchScalarGridSpec(
        num_scalar_prefetch=5,
        grid=(n_tok // TOKEN_TILE,),
        in_specs=[pl.BlockSpec((EXPERTS_PER_GROUP, TOKEN_TILE), lambda i, *_: (0, i)),
                  pl.BlockSpec((TOKEN_TILE, D_MODEL), lambda i, *_: (i, 0))],
        out_specs=pl.BlockSpec(memory_space=pl.ANY),
        scratch_shapes=[pltpu.VMEM((TOKEN_TILE, TOKEN_TILE), jnp.bfloat16),
                        pltpu.VMEM((2, LOCAL_ROWS, D_MODEL), jnp.bfloat16),
                        pltpu.VMEM((ROW_TILE, D_MODEL), jnp.bfloat16),
                        pltpu.SemaphoreType.DMA, pltpu.SemaphoreType.DMA],
    )
    return pl.pallas_call(
        functools.partial(_dispatch_kernel, n_tiles=p_rows // ROW_TILE),
        out_shape=jax.ShapeDtypeStruct((p_rows, D_MODEL), jnp.bfloat16),
        grid_spec=grid_spec,
        compiler_params=pltpu.CompilerParams(dimension_semantics=("arbitrary",),
                                             vmem_limit_bytes=VMEM_LIMIT),
        name="dispatch",
    )(lo, nchunk, dst, pad_tile, nused, route_i, h2)


def _moe_kernel(texp_ref, first_ref, nused_ref, slot_ref, next_ref, xs_ref, wg_hbm, wu_hbm, wd_hbm,
                y_ref, wg_f32, wu_f32, wd_f32, wg_bf, wu_bf, wd_bf, sems):
    j = pl.program_id(0)

    def fetch(expert, slot):
        return [pltpu.make_async_copy(src.at[expert], dst.at[slot], sems.at[slot, k])
                for k, (src, dst) in enumerate(((wg_hbm, wg_f32), (wu_hbm, wu_f32), (wd_hbm, wd_f32)))]

    @pl.when(j == 0)
    def _():
        for cp in fetch(texp_ref[0], 0):
            cp.start()

    @pl.when(j < nused_ref[0])
    def _():
        @pl.when(first_ref[j] == 1)
        def _():
            for s in range(2):
                @pl.when(slot_ref[j] == s)
                def _():
                    for cp in fetch(texp_ref[j], s):
                        cp.wait()
                    wg_bf[...] = wg_f32[s].astype(jnp.bfloat16)
                    wu_bf[...] = wu_f32[s].astype(jnp.bfloat16)
                    wd_bf[...] = wd_f32[s].astype(jnp.bfloat16)

                    @pl.when(next_ref[j] >= 0)
                    def _():
                        for cp in fetch(next_ref[j], 1 - s):
                            cp.start()

        x = xs_ref[...]
        a = _dot(x, wg_bf[...])
        u = _dot(x, wu_bf[...])
        hid = (a * (1.0 / (1.0 + jnp.exp(-a)))) * u
        y_ref[...] = _dot(hid.astype(jnp.bfloat16), wd_bf[...]).astype(jnp.bfloat16)

    @pl.when(j >= nused_ref[0])
    def _():
        y_ref[...] = jnp.zeros_like(y_ref)


def _moe(texp, first, nused, slot, nxt, xs, w_gate, w_up, w_down):
    p_rows = xs.shape[0]
    n_tiles = p_rows // ROW_TILE
    used = lambda j, nu: jnp.minimum(j, nu[0] - 1)
    grid_spec = pltpu.PrefetchScalarGridSpec(
        num_scalar_prefetch=5,
        grid=(n_tiles,),
        in_specs=[
            pl.BlockSpec((ROW_TILE, D_MODEL), lambda j, te, fi, nu, sl, nx: (used(j, nu), 0)),
            pl.BlockSpec(memory_space=pl.ANY),
            pl.BlockSpec(memory_space=pl.ANY),
            pl.BlockSpec(memory_space=pl.ANY),
        ],
        out_specs=pl.BlockSpec((ROW_TILE, D_MODEL), lambda j, te, fi, nu, sl, nx: (j, 0)),
        scratch_shapes=[pltpu.VMEM((2, D_MODEL, D_EXPERT), jnp.float32),
                        pltpu.VMEM((2, D_MODEL, D_EXPERT), jnp.float32),
                        pltpu.VMEM((2, D_EXPERT, D_MODEL), jnp.float32),
                        pltpu.VMEM((D_MODEL, D_EXPERT), jnp.bfloat16),
                        pltpu.VMEM((D_MODEL, D_EXPERT), jnp.bfloat16),
                        pltpu.VMEM((D_EXPERT, D_MODEL), jnp.bfloat16),
                        pltpu.SemaphoreType.DMA((2, 3))],
    )
    return pl.pallas_call(
        _moe_kernel,
        out_shape=jax.ShapeDtypeStruct((p_rows, D_MODEL), jnp.bfloat16),
        grid_spec=grid_spec,
        compiler_params=pltpu.CompilerParams(dimension_semantics=("arbitrary",),
                                             vmem_limit_bytes=VMEM_LIMIT),
        name="moe_experts",
    )(texp, first, nused, slot, nxt, xs, w_gate, w_up, w_down)


def _final_kernel(lo_ref, nchunk_ref, dst_ref, y_hbm, x1_ref, rc_ref, p_ref, gple_ref, wg_ref, wp_ref,
                  gfin_ref, out_ref, lower, ybuf, sems):
    i = pl.program_id(0)
    n_steps = pl.num_programs(0)
    slot = i % 2

    @pl.when(i == 0)
    def _():
        r = lax.broadcasted_iota(jnp.int32, (TOKEN_TILE, TOKEN_TILE), 0)
        c = lax.broadcasted_iota(jnp.int32, (TOKEN_TILE, TOKEN_TILE), 1)
        lower[...] = jnp.where(c < r, 1.0, 0.0).astype(jnp.bfloat16)
        ybuf[...] = jnp.zeros_like(ybuf)
        _start_chunks(nchunk_ref, dst_ref, 0, ybuf.at[0], y_hbm, sems.at[0], False)

    for s in range(2):
        @pl.when((slot == s) & (i + 1 < n_steps))
        def _():
            _start_chunks(nchunk_ref, dst_ref, i + 1, ybuf.at[1 - s], y_hbm, sems.at[1 - s], False)

    pp = _dot(p_ref[...].astype(jnp.bfloat16), wp_ref[...])
    rc = rc_ref[...]
    w0, w1 = rc[:, 0:1], rc[:, 1:2]
    e0, e1 = rc[:, 2:3].astype(jnp.int32), rc[:, 3:4].astype(jnp.int32)
    ids = lax.broadcasted_iota(jnp.int32, (TOKEN_TILE, ROUTER_ROWS), 1)
    member = jnp.where((ids == e0) | (ids == e1), 1.0, 0.0)
    prior = _dot(lower[...], member.astype(jnp.bfloat16))
    lo_vec = _lo_vector(lo_ref, i, (1, ROUTER_ROWS), 1)
    d0 = _local_slots(member, lo_vec, prior, e0, 1)
    d1 = _local_slots(member, lo_vec, prior, e1, 1)
    cols = lax.broadcasted_iota(jnp.int32, (TOKEN_TILE, LOCAL_ROWS), 1)
    pick0 = jnp.where(cols == d0, 1.0, 0.0).astype(jnp.bfloat16)
    pick1 = jnp.where(cols == d1, 1.0, 0.0).astype(jnp.bfloat16)

    for s in range(2):
        @pl.when(slot == s)
        def _():
            _wait_chunks(nchunk_ref, dst_ref, i, ybuf.at[s], y_hbm, sems.at[s], False)
            yb = ybuf[s]
            x2 = x1_ref[...] + (w0 * _dot(pick0, yb) + w1 * _dot(pick1, yb))
            z = _dot(_rms(x2, gple_ref[...]).astype(jnp.bfloat16), wg_ref[...])
            gate = 1.0 / (1.0 + jnp.exp(-z))
            x3 = x2 + pp * gate
            out_ref[...] = _rms(x3, gfin_ref[...])


def _final(lo, nchunk, dst, y, x1, rc, p2d, g_ple, w_gate_bf, w_proj_bf, g_final):
    t = x1.shape[0]
    tm = TOKEN_TILE
    row = lambda w: pl.BlockSpec((tm, w), lambda i, *_: (i, 0))
    full = lambda a: pl.BlockSpec(a.shape, lambda i, *_: (0, 0))
    grid_spec = pltpu.PrefetchScalarGridSpec(
        num_scalar_prefetch=3,
        grid=(t // tm,),
        in_specs=[pl.BlockSpec(memory_space=pl.ANY), row(D_MODEL), row(EXPERTS_PER_GROUP),
                  row(PLE_DIM), full(g_ple), full(w_gate_bf), full(w_proj_bf), full(g_final)],
        out_specs=row(D_MODEL),
        scratch_shapes=[pltpu.VMEM((TOKEN_TILE, TOKEN_TILE), jnp.bfloat16),
                        pltpu.VMEM((2, LOCAL_ROWS, D_MODEL), jnp.bfloat16),
                        pltpu.SemaphoreType.DMA((2,))],
    )
    return pl.pallas_call(
        _final_kernel,
        out_shape=jax.ShapeDtypeStruct((t, D_MODEL), jnp.float32),
        grid_spec=grid_spec,
        compiler_params=pltpu.CompilerParams(dimension_semantics=("arbitrary",),
                                             vmem_limit_bytes=VMEM_LIMIT),
        name="combine_ple_final",
    )(lo, nchunk, dst, y, x1, rc, p2d, g_ple, w_gate_bf, w_proj_bf, g_final)


def _dispatch_tables(counts):
    n_tt = counts.shape[0]
    p_rows = 2 * n_tt * TOKEN_TILE + n_tt * N_EXPERTS * (RUN_ALIGN - 1) + N_EXPERTS * ROW_TILE
    p_rows = -(-p_rows // ROW_TILE) * ROW_TILE
    n_tiles = p_rows // ROW_TILE
    n8 = (counts + RUN_ALIGN - 1) // RUN_ALIGN * RUN_ALIGN
    lo = jnp.cumsum(n8, axis=1) - n8
    total = jnp.sum(n8, axis=0)
    tiles_per = (total + ROW_TILE - 1) // ROW_TILE
    tile_end = jnp.cumsum(tiles_per)
    tile_start = tile_end - tiles_per
    g = tile_start[None, :] * ROW_TILE + (jnp.cumsum(n8, axis=0) - n8)
    n_used = tile_end[-1]
    tile_ids = jnp.arange(n_tiles, dtype=jnp.int32)
    clamped = jnp.minimum(tile_ids, n_used - 1)
    texp = jnp.sum((tile_end[None, :] <= clamped[:, None]).astype(jnp.int32), axis=1)
    experts = jnp.arange(N_EXPERTS, dtype=jnp.int32)
    of_tile = texp[:, None] == experts[None, :]
    per_tile = lambda v: jnp.sum(jnp.where(of_tile, v[None, :], 0), axis=1)
    first = ((tile_ids == per_tile(tile_start)) & (tile_ids < n_used)).astype(jnp.int32)
    pad_tile = jnp.where(total > 0, tile_end - 1, -1).astype(jnp.int32)
    has_rows = total > 0
    later = has_rows[None, :] & (experts[None, :] > experts[:, None])
    next_expert = jnp.min(jnp.where(later, experts[None, :], N_EXPERTS), axis=1)
    next_expert = jnp.where(next_expert < N_EXPERTS, next_expert, -1)
    slot = (jnp.cumsum(has_rows.astype(jnp.int32)) - 1) % 2
    chunk_row = jnp.arange(MAX_CHUNKS, dtype=jnp.int32)[None, :, None] * RUN_ALIGN
    in_run = (chunk_row >= lo[:, None, :]) & (chunk_row < (lo + n8)[:, None, :])
    dst = jnp.sum(jnp.where(in_run, g[:, None, :] + chunk_row - lo[:, None, :], 0), axis=2)
    nchunk = jnp.sum(n8, axis=1) // RUN_ALIGN
    i32 = lambda a: a.reshape(-1).astype(jnp.int32)
    return (texp, first, n_used.reshape(1).astype(jnp.int32), i32(per_tile(slot)),
            i32(per_tile(next_expert)), pad_tile, i32(lo), i32(nchunk), i32(dst), p_rows)


def kernel(x, p, g_mix, w_in, sink, g_grp_a, g_grp_b, w_out, g_ffn, w_router_group, b_router_group,
           w_router_expert, b_router_expert, w_expert_gate, w_expert_up, w_expert_down, g_ple,
           w_ple_gate, w_ple_proj, g_final):
    batch, seq, _ = x.shape
    n_tok = batch * seq
    depth = w_in.shape[0]
    assert depth == 1, "the final RMSNorm is fused into the last layer's epilogue"
    bf = jnp.bfloat16
    xc = x.reshape(n_tok, D_MODEL)
    for i in range(depth):
        *qkv_a, qb, kb, vb = _proj(xc, g_mix[i][None, :], w_in[i].astype(bf), batch, seq)
        oa = _attn_a(qkv_a, batch, seq)
        ob = _attn_b(qb, kb, vb, sink[i], batch, seq)

        wr = jnp.zeros((D_MODEL, ROUTER_ROWS), jnp.float32)
        wr = wr.at[:, 0:N_GROUPS].set(w_router_group[i])
        wr = wr.at[:, EXPERTS_PER_GROUP:EXPERTS_PER_GROUP + N_EXPERTS].set(w_router_expert[i])
        br = jnp.zeros((ROUTER_ROWS, 1), jnp.float32)
        br = br.at[0:N_GROUPS, 0].set(b_router_group[i])
        br = br.at[EXPERTS_PER_GROUP:EXPERTS_PER_GROUP + N_EXPERTS, 0].set(b_router_expert[i])

        x1, h2, route_i, route_c, counts = _mix(
            oa, ob, xc, g_grp_a[i][None, :], g_grp_b[i][None, :], w_out[i].astype(bf), g_ffn[i][None, :], wr, br)

        texp, first, nused, slot, nxt, pad_tile, lo, nchunk, dst, p_rows = _dispatch_tables(counts[:, :, 0])
        xs = _dispatch(lo, nchunk, dst, pad_tile, nused, route_i, h2, p_rows)
        y = _moe(texp, first, nused, slot, nxt, xs, w_expert_gate[i], w_expert_up[i], w_expert_down[i])
        xc = _final(lo, nchunk, dst, y, x1, route_c, p[i].reshape(n_tok, PLE_DIM), g_ple[i][None, :],
                    w_ple_gate[i].astype(bf), w_ple_proj[i].astype(bf), g_final[None, :])
    return xc.reshape(batch, seq, D_MODEL)
```

```python
import functools

import jax
import jax.numpy as jnp
from jax import lax
from jax.experimental import pallas as pl
from jax.experimental.pallas import tpu as pltpu

D_MODEL = 1024
HEAD_DIM = 64
GROUP_WIDTH = 512
N_HEADS = 8
B_KV_HEADS = 2
B_KV_WIDTH = B_KV_HEADS * HEAD_DIM
A_CONFIGS = ((128, 1), (512, 4), (2048, 16))
B_HALF_WINDOW = 128
N_GROUPS = 4
EXPERTS_PER_GROUP = 8
N_EXPERTS = N_GROUPS * EXPERTS_PER_GROUP
D_EXPERT = 512
PLE_DIM = 256
EPS = 1e-6
NEG = -1e30
LOG2E = 1.4426950408889634
LN2 = 0.6931471805599453

LANES = 128
Q_BLOCK = 128
HEADS_PER_PASS = 2
PASS_WIDTH = HEADS_PER_PASS * HEAD_DIM
ROW_TILE = 512
TOKEN_TILE = 512
ROUTER_ROWS = 128
VMEM_LIMIT = 48 * 1024 * 1024
ATTN_VMEM_LIMIT = 56 * 1024 * 1024


def _rms(x, g):
    r = lax.rsqrt(jnp.mean(x * x, axis=-1, keepdims=True) + EPS)
    return (x * r) * g


def _dot(a, b):
    return jnp.dot(a, b, preferred_element_type=jnp.float32)


PROJ_TILE = 512


def _proj_kernel(x_ref, g_ref, w_ref, *refs):
    a_refs, (qb_ref, kb_ref, vb_ref, res, *stage_refs) = refs[:9], refs[9:]
    h = _rms(x_ref[...], g_ref[...]).astype(jnp.bfloat16)
    scale = HEAD_DIM ** -0.5 * LOG2E
    W = GROUP_WIDTH
    tiles = GROUP_WIDTH // PASS_WIDTH
    qkv_a = (_dot(h, w_ref[:, 0:W]) * scale, _dot(h, w_ref[:, W:2 * W]), _dot(h, w_ref[:, 2 * W:3 * W]))
    for part in range(3):
        for pair in range(tiles):
            res[part * tiles + pair, 0] = qkv_a[part][:, pair * PASS_WIDTH:(pair + 1) * PASS_WIDTH]
    qb_ref[...] = (_dot(h, w_ref[:, 3 * W:4 * W]) * scale).astype(jnp.bfloat16)
    kb_ref[...] = _dot(h, w_ref[:, 4 * W:4 * W + B_KV_WIDTH]).astype(jnp.bfloat16)
    vb_ref[...] = _dot(h, w_ref[:, 4 * W + B_KV_WIDTH:4 * W + 2 * B_KV_WIDTH]).astype(jnp.bfloat16)
    stages = (res,) + tuple(stage_refs)
    prev_dil = 1
    for c, (_, dil) in enumerate(A_CONFIGS):
        ratio = dil // prev_dil
        rows = PROJ_TILE // dil
        for idx in range(3 * tiles):
            part, pair = idx // tiles, idx % tiles
            out = a_refs[3 * c + part]
            for rho in range(dil):
                if c == 0:
                    sub = res[idx, 0]
                else:
                    sub = stages[c - 1].at[idx, rho % prev_dil][pl.ds(rho // prev_dil, rows, stride=ratio), :]
                    if c < len(A_CONFIGS) - 1:
                        stages[c][idx, rho] = sub
                out[pair, rho] = sub.astype(jnp.bfloat16)
        prev_dil = dil


def _proj(x2d, g_mix, w_in_bf, batch, seq):
    t = x2d.shape[0]
    tm = PROJ_TILE
    tiles_per_seq = seq // tm
    in_width = w_in_bf.shape[1]
    n_pairs = N_HEADS // HEADS_PER_PASS
    row = lambda w: pl.BlockSpec((tm, w), lambda i: (i, 0))
    a_shapes, a_specs = [], []
    for _, dil in A_CONFIGS:
        for _ in range(3):
            a_shapes.append(jax.ShapeDtypeStruct((batch, n_pairs, dil, seq // dil, PASS_WIDTH),
                                                 jnp.bfloat16))
            a_specs.append(pl.BlockSpec((None, n_pairs, dil, tm // dil, PASS_WIDTH),
                                        lambda i: (i // tiles_per_seq, 0, 0, i % tiles_per_seq, 0)))
    wide = jax.ShapeDtypeStruct((t, GROUP_WIDTH), jnp.bfloat16)
    narrow = jax.ShapeDtypeStruct((t, B_KV_WIDTH), jnp.bfloat16)
    return pl.pallas_call(
        _proj_kernel,
        out_shape=tuple(a_shapes) + (wide, narrow, narrow),
        grid=(t // tm,),
        in_specs=[row(D_MODEL),
                  pl.BlockSpec((1, D_MODEL), lambda i: (0, 0)),
                  pl.BlockSpec((D_MODEL, in_width), lambda i: (0, 0))],
        out_specs=tuple(a_specs) + (row(GROUP_WIDTH), row(B_KV_WIDTH), row(B_KV_WIDTH)),
        scratch_shapes=[pltpu.VMEM((3 * GROUP_WIDTH // PASS_WIDTH, dil, tm // dil, PASS_WIDTH), jnp.float32)
                        for _, dil in A_CONFIGS[:-1]],
        compiler_params=pltpu.CompilerParams(dimension_semantics=("arbitrary",),
                                             vmem_limit_bytes=VMEM_LIMIT),
        name="proj",
    )(x2d, g_mix, w_in_bf)


def _fill_bias(bias_ref, half, nkeys, dist_scale):
    i = lax.broadcasted_iota(jnp.int32, (Q_BLOCK, nkeys), 0)
    j = lax.broadcasted_iota(jnp.int32, (Q_BLOCK, nkeys), 1)
    for var, rel0 in enumerate((0, -half, Q_BLOCK - nkeys)):
        d = jnp.abs(j + rel0 - i)
        dist = d.astype(jnp.float32) * dist_scale
        for h in range(N_HEADS):
            slope = 2.0 ** (-(h + 1))
            bias_ref[var, h] = jnp.where(d <= half, -(slope * dist) * LOG2E, NEG)


def _key_window(q0, seq_len, half, nkeys):
    lo = q0 - half
    ks = pl.multiple_of(jnp.clip(lo, 0, seq_len - nkeys), HEAD_DIM)
    var = jnp.where(lo < 0, 0, jnp.where(lo > seq_len - nkeys, 2, 1))
    return ks, var


def _fold_lane_tiles(a, op):
    out = a[..., 0:LANES]
    for t in range(1, a.shape[-1] // LANES):
        out = op(out, a[..., t * LANES:(t + 1) * LANES])
    return out


def _attend_pass(q, k, v, bias, sinks):
    nkeys = k.shape[0]
    lane_head = lax.broadcasted_iota(jnp.int32, (Q_BLOCK, PASS_WIDTH), 1) // HEAD_DIM
    zero = jnp.zeros_like(q)
    q_stack = jnp.concatenate(
        [jnp.where(lane_head == h, q, zero) for h in range(HEADS_PER_PASS)], axis=0)
    s = lax.dot_general(q_stack, k, (((1,), (1,)), ((), ())), preferred_element_type=jnp.float32)
    s = s.reshape(HEADS_PER_PASS, Q_BLOCK, nkeys) + bias
    m_tile = _fold_lane_tiles(s, jnp.maximum)
    if sinks is not None:
        tile_head = lax.broadcasted_iota(jnp.int32, m_tile.shape, 0)
        tile_lane = lax.broadcasted_iota(jnp.int32, m_tile.shape, 2)
        sk = jnp.zeros(m_tile.shape, jnp.float32)
        for h in range(HEADS_PER_PASS):
            sk = jnp.where(tile_head == h, sinks[h] * LOG2E, sk)
        m_tile = jnp.maximum(m_tile, sk)
    m = jnp.max(m_tile, axis=-1, keepdims=True)
    p = jnp.exp2(s - m)
    den_tile = _fold_lane_tiles(p, jnp.add)
    if sinks is not None:
        den_tile = den_tile + jnp.where(tile_lane == 0, jnp.exp2(sk - m), 0.0)
    den = jnp.sum(den_tile, axis=-1, keepdims=True)
    pv = _dot(p.reshape(HEADS_PER_PASS * Q_BLOCK, nkeys).astype(jnp.bfloat16), v)
    pv = pv.reshape(HEADS_PER_PASS, Q_BLOCK, PASS_WIDTH) * (1.0 / den)
    lse = m * LN2 + jnp.log(den)
    o = jnp.zeros((Q_BLOCK, PASS_WIDTH), jnp.float32)
    lse_lanes = jnp.zeros((Q_BLOCK, PASS_WIDTH), jnp.float32)
    for h in range(HEADS_PER_PASS):
        o = jnp.where(lane_head == h, pv[h], o)
        lse_lanes = jnp.where(lane_head == h, lse[h], lse_lanes)
    return o, lse_lanes


A_HALF = 64
A_KEYS = Q_BLOCK + 2 * A_HALF
PASSES_PER_BODY = 16


def _attn_a_kernel(*refs, seq):
    n_cfg = len(A_CONFIGS)
    qkv = refs[:3 * n_cfg]
    o_ref, bias_ref = refs[3 * n_cfg], refs[3 * n_cfg + 1]
    scratch = refs[3 * n_cfg + 2:]
    pair = pl.program_id(1)

    @pl.when((pl.program_id(0) == 0) & (pair == 0))
    def _():
        for c, (_, dil) in enumerate(A_CONFIGS):
            _fill_bias(bias_ref.at[c], A_HALF, A_KEYS, float(dil))

    for c, (window, dil) in enumerate(A_CONFIGS):
        assert window // (2 * dil) == A_HALF
        q_ref, k_ref, v_ref = qkv[3 * c:3 * c + 3]
        o_sc, l_sc = scratch[2 * c], scratch[2 * c + 1]
        sub_len = seq // dil
        blocks = sub_len // Q_BLOCK

        def body(idx, carry, q_ref=q_ref, k_ref=k_ref, v_ref=v_ref, o_sc=o_sc, l_sc=l_sc, c=c,
                 sub_len=sub_len, blocks=blocks):
            rho = idx // blocks
            q0 = pl.multiple_of((idx % blocks) * Q_BLOCK, Q_BLOCK)
            ks, var = _key_window(q0, sub_len, A_HALF, A_KEYS)
            bias = bias_ref[c, var, pl.ds(pair * HEADS_PER_PASS, HEADS_PER_PASS)]
            o, lse = _attend_pass(q_ref[rho, pl.ds(q0, Q_BLOCK), :], k_ref[rho, pl.ds(ks, A_KEYS), :],
                                  v_ref[rho, pl.ds(ks, A_KEYS), :], bias, None)
            o_sc[rho, pl.ds(q0, Q_BLOCK), :] = o
            l_sc[rho, pl.ds(q0, Q_BLOCK), :] = lse
            return carry

        lax.fori_loop(0, dil * blocks, body, 0, unroll=PASSES_PER_BODY)

    widest = A_CONFIGS[-1][1]
    rows = seq // widest
    for r in range(widest):
        outs, lses = [], []
        for c, (_, dil) in enumerate(A_CONFIGS):
            step = widest // dil
            idx = (r % dil, pl.ds(r // dil, rows, stride=step) if step > 1 else pl.ds(0, rows))
            outs.append(scratch[2 * c].at[idx[0]][idx[1], :])
            lses.append(scratch[2 * c + 1].at[idx[0]][idx[1], :])
        mx = functools.reduce(jnp.maximum, lses)
        es = [jnp.exp(l - mx) for l in lses]
        inv = 1.0 / functools.reduce(jnp.add, es)
        merged = functools.reduce(jnp.add, [(e * inv) * o for e, o in zip(es, outs)])
        o_ref[pl.ds(r, rows, stride=widest), :] = merged


def _attn_a(qkv, batch, seq):
    n_pairs = N_HEADS // HEADS_PER_PASS
    in_specs, scratch = [], [pltpu.VMEM((len(A_CONFIGS), 3, N_HEADS, Q_BLOCK, A_KEYS), jnp.float32)]
    for _, dil in A_CONFIGS:
        blk = (None, None, dil, seq // dil, PASS_WIDTH)
        in_specs += [pl.BlockSpec(blk, lambda b, p: (b, p, 0, 0, 0))] * 3
        scratch += [pltpu.VMEM((dil, seq // dil, PASS_WIDTH), jnp.float32)] * 2
    return pl.pallas_call(
        functools.partial(_attn_a_kernel, seq=seq),
        out_shape=jax.ShapeDtypeStruct((batch, seq, GROUP_WIDTH), jnp.float32),
        grid=(batch, n_pairs),
        in_specs=in_specs,
        out_specs=pl.BlockSpec((None, seq, PASS_WIDTH), lambda b, p: (b, 0, p)),
        scratch_shapes=scratch,
        compiler_params=pltpu.CompilerParams(dimension_semantics=("arbitrary", "arbitrary"),
                                             vmem_limit_bytes=ATTN_VMEM_LIMIT),
        name="attn_a",
    )(*qkv).reshape(batch * seq, GROUP_WIDTH)


def _attn_b_kernel(sink_ref, q_ref, k_ref, v_ref, o_ref, bias_ref, k4_ref, v4_ref, *, half, nkeys,
                   seq_len, q_tile):
    first = (pl.program_id(0) == 0) & (pl.program_id(1) == 0)

    @pl.when(first)
    def _():
        _fill_bias(bias_ref, half, nkeys, 1.0)

    @pl.when(pl.program_id(1) == 0)
    def _():
        c = lax.broadcasted_iota(jnp.int32, (B_KV_WIDTH, GROUP_WIDTH), 0)
        j = lax.broadcasted_iota(jnp.int32, (B_KV_WIDTH, GROUP_WIDTH), 1)
        src = (j // (GROUP_WIDTH // B_KV_HEADS)) * HEAD_DIM + j % HEAD_DIM
        rep = jnp.where(c == src, 1.0, 0.0).astype(jnp.bfloat16)
        chunk = 512

        def body(i, carry):
            rows = pl.ds(pl.multiple_of(i * chunk, chunk), chunk)
            k4_ref[rows, :] = _dot(k_ref[rows, :], rep).astype(jnp.bfloat16)
            v4_ref[rows, :] = _dot(v_ref[rows, :], rep).astype(jnp.bfloat16)
            return carry

        lax.fori_loop(0, seq_len // chunk, body, 0)

    base = pl.program_id(1) * q_tile

    def body(jb, carry):
        row0 = pl.multiple_of(jb * Q_BLOCK, Q_BLOCK)
        ks, var = _key_window(base + row0, seq_len, half, nkeys)
        for g in range(N_HEADS // HEADS_PER_PASS):
            cols = slice(g * PASS_WIDTH, (g + 1) * PASS_WIDTH)
            heads = pl.ds(g * HEADS_PER_PASS, HEADS_PER_PASS)
            sinks = [sink_ref[g * HEADS_PER_PASS + h] for h in range(HEADS_PER_PASS)]
            o, _ = _attend_pass(q_ref[pl.ds(row0, Q_BLOCK), cols], k4_ref[pl.ds(ks, nkeys), cols],
                                v4_ref[pl.ds(ks, nkeys), cols], bias_ref[var, heads], sinks)
            o_ref[pl.ds(row0, Q_BLOCK), cols] = o
        return carry

    lax.fori_loop(0, q_tile // Q_BLOCK, body, 0,
                  unroll=PASSES_PER_BODY // (N_HEADS // HEADS_PER_PASS))


def _attn_b(q, k, v, sink, batch, seq):
    half = B_HALF_WINDOW
    nkeys = Q_BLOCK + 2 * half
    q_tile = 1024
    view = lambda a: a.reshape(batch, seq, a.shape[-1])
    kern = functools.partial(_attn_b_kernel, half=half, nkeys=nkeys, seq_len=seq, q_tile=q_tile)
    o = pl.pallas_call(
        kern,
        out_shape=jax.ShapeDtypeStruct((batch, seq, GROUP_WIDTH), jnp.float32),
        grid=(batch, seq // q_tile),
        in_specs=[pl.BlockSpec(memory_space=pltpu.SMEM),
                  pl.BlockSpec((None, q_tile, GROUP_WIDTH), lambda b, i: (b, i, 0)),
                  pl.BlockSpec((None, seq, B_KV_WIDTH), lambda b, i: (b, 0, 0)),
                  pl.BlockSpec((None, seq, B_KV_WIDTH), lambda b, i: (b, 0, 0))],
        out_specs=pl.BlockSpec((None, q_tile, GROUP_WIDTH), lambda b, i: (b, i, 0)),
        scratch_shapes=[pltpu.VMEM((3, N_HEADS, Q_BLOCK, nkeys), jnp.float32),
                        pltpu.VMEM((seq, GROUP_WIDTH), jnp.bfloat16),
                        pltpu.VMEM((seq, GROUP_WIDTH), jnp.bfloat16)],
        compiler_params=pltpu.CompilerParams(dimension_semantics=("arbitrary", "arbitrary"),
                                             vmem_limit_bytes=VMEM_LIMIT),
        name="attn_b",
    )(sink, view(q), view(k), view(v))
    return o.reshape(batch * seq, GROUP_WIDTH)


def _mix_kernel(oa_ref, ob_ref, x_ref, ga_ref, gb_ref, wout_ref, gffn_ref, wr_ref, br_ref,
                x1_ref, h2_ref, ri_ref, rc_ref, cnt_ref):
    tm = x_ref.shape[0]
    na = _rms(oa_ref[...], ga_ref[...]).astype(jnp.bfloat16)
    nb = _rms(ob_ref[...], gb_ref[...]).astype(jnp.bfloat16)
    x1 = (x_ref[...] + _dot(na, wout_ref[0:GROUP_WIDTH, :])
          + _dot(nb, wout_ref[GROUP_WIDTH:2 * GROUP_WIDTH, :]))
    x1_ref[...] = x1
    h2 = _rms(x1, gffn_ref[...])
    h2_ref[...] = h2.astype(jnp.bfloat16)

    h_hi = h2.astype(jnp.bfloat16)
    h_lo = (h2 - h_hi.astype(jnp.float32)).astype(jnp.bfloat16)
    wr = wr_ref[...]
    w_hi = wr.astype(jnp.bfloat16)
    w_lo = (wr - w_hi.astype(jnp.float32)).astype(jnp.bfloat16)
    w_cat = jnp.concatenate([w_hi, w_lo], axis=1)
    lg2 = _dot(h_hi, w_cat) + _dot(h_lo, w_cat)
    lg = lg2[:, 0:ROUTER_ROWS] + lg2[:, ROUTER_ROWS:2 * ROUTER_ROWS]
    lgt = lg.T + br_ref[...]

    rows = lax.broadcasted_iota(jnp.int32, (EXPERTS_PER_GROUP, tm), 0)
    big = jnp.int32(1 << 20)
    glog = jnp.where(rows < N_GROUPS, lgt[0:EXPERTS_PER_GROUP], -jnp.inf)
    gmax = jnp.max(glog, axis=0, keepdims=True)
    gsel = jnp.min(jnp.where(glog == gmax, rows, big), axis=0, keepdims=True)
    gw = 1.0 / jnp.sum(jnp.exp(glog - gmax), axis=0, keepdims=True)
    esel = jnp.zeros((EXPERTS_PER_GROUP, tm), jnp.float32)
    for grp in range(N_GROUPS):
        lo = EXPERTS_PER_GROUP * (grp + 1)
        esel = jnp.where(gsel == grp, lgt[lo:lo + EXPERTS_PER_GROUP], esel)
    v0 = jnp.max(esel, axis=0, keepdims=True)
    i0 = jnp.min(jnp.where(esel == v0, rows, big), axis=0, keepdims=True)
    rest = jnp.where(rows == i0, -jnp.inf, esel)
    v1 = jnp.max(rest, axis=0, keepdims=True)
    i1 = jnp.min(jnp.where(rest == v1, rows, big), axis=0, keepdims=True)
    e = jnp.exp(v1 - v0)
    w0 = (1.0 / (1.0 + e)) * gw
    w1 = (e / (1.0 + e)) * gw
    eid0 = gsel * EXPERTS_PER_GROUP + i0
    eid1 = gsel * EXPERTS_PER_GROUP + i1
    ri_ref[...] = jnp.where(rows == 0, eid0, jnp.where(rows == 1, eid1, 0))
    rows_t = lax.broadcasted_iota(jnp.int32, (ROUTER_ROWS, tm), 0)
    slab = jnp.where(rows_t == 0, w0, jnp.where(rows_t == 1, w1, 0.0))
    slab = jnp.where(rows_t == 2, eid0.astype(jnp.float32),
                     jnp.where(rows_t == 3, eid1.astype(jnp.float32), slab))
    rc_ref[...] = slab.T[:, 0:EXPERTS_PER_GROUP]
    ids = lax.broadcasted_iota(jnp.int32, (N_EXPERTS, tm), 0)
    member = jnp.where((ids == eid0) | (ids == eid1), 1.0, 0.0)
    cnt_ref[...] = jnp.sum(member, axis=1, keepdims=True).astype(jnp.int32)


def _mix(oa, ob, x2d, g_a, g_b, w_out_bf, g_ffn, wr, br):
    t = x2d.shape[0]
    tm = TOKEN_TILE
    row = lambda w: pl.BlockSpec((tm, w), lambda i: (i, 0))
    full = lambda a: pl.BlockSpec(a.shape, lambda i: (0, 0))
    return pl.pallas_call(
        _mix_kernel,
        out_shape=(jax.ShapeDtypeStruct((t, D_MODEL), jnp.float32),
                   jax.ShapeDtypeStruct((t, D_MODEL), jnp.bfloat16),
                   jax.ShapeDtypeStruct((EXPERTS_PER_GROUP, t), jnp.int32),
                   jax.ShapeDtypeStruct((t, EXPERTS_PER_GROUP), jnp.float32),
                   jax.ShapeDtypeStruct((t // tm, N_EXPERTS, 1), jnp.int32)),
        grid=(t // tm,),
        in_specs=[row(GROUP_WIDTH), row(GROUP_WIDTH), row(D_MODEL),
                  full(g_a), full(g_b), full(w_out_bf), full(g_ffn), full(wr), full(br)],
        out_specs=(row(D_MODEL), row(D_MODEL),
                   pl.BlockSpec((EXPERTS_PER_GROUP, tm), lambda i: (0, i)),
                   row(EXPERTS_PER_GROUP),
                   pl.BlockSpec((None, N_EXPERTS, 1), lambda i: (i, 0, 0))),
        compiler_params=pltpu.CompilerParams(dimension_semantics=("arbitrary",),
                                             vmem_limit_bytes=VMEM_LIMIT),
        name="mix_router",
    )(oa, ob, x2d, g_a, g_b, w_out_bf, g_ffn, wr, br)


RUN_ALIGN = 16
LOCAL_ROWS = 2 * TOKEN_TILE + N_EXPERTS * RUN_ALIGN
MAX_CHUNKS = LOCAL_ROWS // RUN_ALIGN


def _local_slots(member, lo_vec, prior, e_sel, axis):
    ids = lax.broadcasted_iota(jnp.int32, member.shape, axis)
    slot = jnp.where(ids == e_sel, lo_vec + prior, 0.0)
    return jnp.sum(slot, axis=axis, keepdims=True).astype(jnp.int32)


def _chunk_copy(dst_ref, tile, c, local, glob, sem, to_global):
    g = pl.multiple_of(dst_ref[tile * MAX_CHUNKS + c], RUN_ALIGN)
    l = pl.multiple_of(c * RUN_ALIGN, RUN_ALIGN)
    src, dst = local.at[pl.ds(l, RUN_ALIGN), :], glob.at[pl.ds(g, RUN_ALIGN), :]
    if not to_global:
        src, dst = dst, src
    return pltpu.make_async_copy(src, dst, sem)


def _start_chunks(nchunk_ref, dst_ref, tile, local, glob, sem, to_global):
    def body(c, carry):
        _chunk_copy(dst_ref, tile, c, local, glob, sem, to_global).start()
        return carry
    lax.fori_loop(0, nchunk_ref[tile], body, 0)


def _wait_chunks(nchunk_ref, dst_ref, tile, local, glob, sem, to_global):
    def body(c, carry):
        _chunk_copy(dst_ref, tile, c, local, glob, sem, to_global).wait()
        return carry
    lax.fori_loop(0, nchunk_ref[tile], body, 0)


def _lo_vector(lo_ref, tile, shape, axis):
    ids = lax.broadcasted_iota(jnp.int32, shape, axis)
    vec = jnp.zeros(shape, jnp.float32)
    for e in range(N_EXPERTS):
        vec = jnp.where(ids == e, lo_ref[tile * N_EXPERTS + e].astype(jnp.float32), vec)
    return vec


def _dispatch_kernel(lo_ref, nchunk_ref, dst_ref, pad_tile_ref, nused_ref, ri_ref, h_ref, xs_hbm,
                     upper, sbuf, zbuf, sem, zsem, *, n_tiles):
    i = pl.program_id(0)
    n_steps = pl.num_programs(0)
    slot = i % 2

    def zero_tile(tile):
        rows = pl.ds(pl.multiple_of(tile * ROW_TILE, ROW_TILE), ROW_TILE)
        return pltpu.make_async_copy(zbuf, xs_hbm.at[rows, :], zsem)

    @pl.when(i == 0)
    def _():
        r = lax.broadcasted_iota(jnp.int32, (TOKEN_TILE, TOKEN_TILE), 0)
        c = lax.broadcasted_iota(jnp.int32, (TOKEN_TILE, TOKEN_TILE), 1)
        upper[...] = jnp.where(r < c, 1.0, 0.0).astype(jnp.bfloat16)
        zbuf[...] = jnp.zeros_like(zbuf)
        for e in range(N_EXPERTS):
            @pl.when(pad_tile_ref[e] >= 0)
            def _():
                zero_tile(pad_tile_ref[e]).start()
        lax.fori_loop(nused_ref[0], n_tiles, lambda t, c: (zero_tile(t).start(), c)[1], 0)
        for e in range(N_EXPERTS):
            @pl.when(pad_tile_ref[e] >= 0)
            def _():
                zero_tile(pad_tile_ref[e]).wait()
        lax.fori_loop(nused_ref[0], n_tiles, lambda t, c: (zero_tile(t).wait(), c)[1], 0)

    e0 = ri_ref[0:1, :]
    e1 = ri_ref[1:2, :]
    ids = lax.broadcasted_iota(jnp.int32, (N_EXPERTS, TOKEN_TILE), 0)
    member = jnp.where((ids == e0) | (ids == e1), 1.0, 0.0)
    prior = _dot(member.astype(jnp.bfloat16), upper[...])
    lo_vec = _lo_vector(lo_ref, i, (N_EXPERTS, 1), 0)
    d0 = _local_slots(member, lo_vec, prior, e0, 0)
    d1 = _local_slots(member, lo_vec, prior, e1, 0)
    rows = lax.broadcasted_iota(jnp.int32, (LOCAL_ROWS, TOKEN_TILE), 0)
    select = jnp.where((rows == d0) | (rows == d1), 1.0, 0.0).astype(jnp.bfloat16)
    sorted_rows = _dot(select, h_ref[...])

    for s in range(2):
        @pl.when(slot == s)
        def _():
            sbuf[s] = sorted_rows.astype(jnp.bfloat16)

            @pl.when(i > 0)
            def _():
                _wait_chunks(nchunk_ref, dst_ref, i - 1, sbuf.at[1 - s], xs_hbm, sem, True)

            _start_chunks(nchunk_ref, dst_ref, i, sbuf.at[s], xs_hbm, sem, True)

            @pl.when(i == n_steps - 1)
            def _():
                _wait_chunks(nchunk_ref, dst_ref, i, sbuf.at[s], xs_hbm, sem, True)


def _dispatch(lo, nchunk, dst, pad_tile, nused, route_i, h2, p_rows):
    n_tok = h2.shape[0]
    grid_spec = pltpu.PrefetchScalarGridSpec(
        num_scalar_prefetch=5,
        grid=(n_tok // TOKEN_TILE,),
        in_specs=[pl.BlockSpec((EXPERTS_PER_GROUP, TOKEN_TILE), lambda i, *_: (0, i)),
                  pl.BlockSpec((TOKEN_TILE, D_MODEL), lambda i, *_: (i, 0))],
        out_specs=pl.BlockSpec(memory_space=pl.ANY),
        scratch_shapes=[pltpu.VMEM((TOKEN_TILE, TOKEN_TILE), jnp.bfloat16),
                        pltpu.VMEM((2, LOCAL_ROWS, D_MODEL), jnp.bfloat16),
                        pltpu.VMEM((ROW_TILE, D_MODEL), jnp.bfloat16),
                        pltpu.SemaphoreType.DMA, pltpu.SemaphoreType.DMA],
    )
    return pl.pallas_call(
        functools.partial(_dispatch_kernel, n_tiles=p_rows // ROW_TILE),
        out_shape=jax.ShapeDtypeStruct((p_rows, D_MODEL), jnp.bfloat16),
        grid_spec=grid_spec,
        compiler_params=pltpu.CompilerParams(dimension_semantics=("arbitrary",),
                                             vmem_limit_bytes=VMEM_LIMIT),
        name="dispatch",
    )(lo, nchunk, dst, pad_tile, nused, route_i, h2)


def _moe_kernel(texp_ref, first_ref, nused_ref, slot_ref, next_ref, xs_ref, wg_hbm, wu_hbm, wd_hbm,
                y_ref, wg_f32, wu_f32, wd_f32, wg_bf, wu_bf, wd_bf, sems):
    j = pl.program_id(0)

    def fetch(expert, slot):
        return [pltpu.make_async_copy(src.at[expert], dst.at[slot], sems.at[slot, k])
                for k, (src, dst) in enumerate(((wg_hbm, wg_f32), (wu_hbm, wu_f32), (wd_hbm, wd_f32)))]

    @pl.when(j == 0)
    def _():
        for cp in fetch(texp_ref[0], 0):
            cp.start()

    @pl.when(j < nused_ref[0])
    def _():
        @pl.when(first_ref[j] == 1)
        def _():
            for s in range(2):
                @pl.when(slot_ref[j] == s)
                def _():
                    for cp in fetch(texp_ref[j], s):
                        cp.wait()
                    wg_bf[...] = wg_f32[s].astype(jnp.bfloat16)
                    wu_bf[...] = wu_f32[s].astype(jnp.bfloat16)
                    wd_bf[...] = wd_f32[s].astype(jnp.bfloat16)

                    @pl.when(next_ref[j] >= 0)
                    def _():
                        for cp in fetch(next_ref[j], 1 - s):
                            cp.start()

        x = xs_ref[...]
        a = _dot(x, wg_bf[...])
        u = _dot(x, wu_bf[...])
        hid = (a * (1.0 / (1.0 + jnp.exp(-a)))) * u
        y_ref[...] = _dot(hid.astype(jnp.bfloat16), wd_bf[...]).astype(jnp.bfloat16)

    @pl.when(j >= nused_ref[0])
    def _():
        y_ref[...] = jnp.zeros_like(y_ref)


def _moe(texp, first, nused, slot, nxt, xs, w_gate, w_up, w_down):
    p_rows = xs.shape[0]
    n_tiles = p_rows // ROW_TILE
    used = lambda j, nu: jnp.minimum(j, nu[0] - 1)
    grid_spec = pltpu.PrefetchScalarGridSpec(
        num_scalar_prefetch=5,
        grid=(n_tiles,),
        in_specs=[
            pl.BlockSpec((ROW_TILE, D_MODEL), lambda j, te, fi, nu, sl, nx: (used(j, nu), 0)),
            pl.BlockSpec(memory_space=pl.ANY),
            pl.BlockSpec(memory_space=pl.ANY),
            pl.BlockSpec(memory_space=pl.ANY),
        ],
        out_specs=pl.BlockSpec((ROW_TILE, D_MODEL), lambda j, te, fi, nu, sl, nx: (j, 0)),
        scratch_shapes=[pltpu.VMEM((2, D_MODEL, D_EXPERT), jnp.float32),
                        pltpu.VMEM((2, D_MODEL, D_EXPERT), jnp.float32),
                        pltpu.VMEM((2, D_EXPERT, D_MODEL), jnp.float32),
                        pltpu.VMEM((D_MODEL, D_EXPERT), jnp.bfloat16),
                        pltpu.VMEM((D_MODEL, D_EXPERT), jnp.bfloat16),
                        pltpu.VMEM((D_EXPERT, D_MODEL), jnp.bfloat16),
                        pltpu.SemaphoreType.DMA((2, 3))],
    )
    return pl.pallas_call(
        _moe_kernel,
        out_shape=jax.ShapeDtypeStruct((p_rows, D_MODEL), jnp.bfloat16),
        grid_spec=grid_spec,
        compiler_params=pltpu.CompilerParams(dimension_semantics=("arbitrary",),
                                             vmem_limit_bytes=VMEM_LIMIT),
        name="moe_experts",
    )(texp, first, nused, slot, nxt, xs, w_gate, w_up, w_down)


def _final_kernel(lo_ref, nchunk_ref, dst_ref, y_hbm, x1_ref, rc_ref, p_ref, gple_ref, wg_ref, wp_ref,
                  gfin_ref, out_ref, lower, ybuf, sems):
    i = pl.program_id(0)
    n_steps = pl.num_programs(0)
    slot = i % 2

    @pl.when(i == 0)
    def _():
        r = lax.broadcasted_iota(jnp.int32, (TOKEN_TILE, TOKEN_TILE), 0)
        c = lax.broadcasted_iota(jnp.int32, (TOKEN_TILE, TOKEN_TILE), 1)
        lower[...] = jnp.where(c < r, 1.0, 0.0).astype(jnp.bfloat16)
        ybuf[...] = jnp.zeros_like(ybuf)
        _start_chunks(nchunk_ref, dst_ref, 0, ybuf.at[0], y_hbm, sems.at[0], False)

    for s in range(2):
        @pl.when((slot == s) & (i + 1 < n_steps))
        def _():
            _start_chunks(nchunk_ref, dst_ref, i + 1, ybuf.at[1 - s], y_hbm, sems.at[1 - s], False)

    pp = _dot(p_ref[...].astype(jnp.bfloat16), wp_ref[...])
    rc = rc_ref[...]
    w0, w1 = rc[:, 0:1], rc[:, 1:2]
    e0, e1 = rc[:, 2:3].astype(jnp.int32), rc[:, 3:4].astype(jnp.int32)
    ids = lax.broadcasted_iota(jnp.int32, (TOKEN_TILE, ROUTER_ROWS), 1)
    member = jnp.where((ids == e0) | (ids == e1), 1.0, 0.0)
    prior = _dot(lower[...], member.astype(jnp.bfloat16))
    lo_vec = _lo_vector(lo_ref, i, (1, ROUTER_ROWS), 1)
    d0 = _local_slots(member, lo_vec, prior, e0, 1)
    d1 = _local_slots(member, lo_vec, prior, e1, 1)
    cols = lax.broadcasted_iota(jnp.int32, (TOKEN_TILE, LOCAL_ROWS), 1)
    pick = jnp.where(cols == d0, w0, jnp.where(cols == d1, w1, 0.0)).astype(jnp.bfloat16)

    for s in range(2):
        @pl.when(slot == s)
        def _():
            _wait_chunks(nchunk_ref, dst_ref, i, ybuf.at[s], y_hbm, sems.at[s], False)
            yb = ybuf[s]
            x2 = x1_ref[...] + _dot(pick, yb)
            z = _dot(_rms(x2, gple_ref[...]).astype(jnp.bfloat16), wg_ref[...])
            gate = 1.0 / (1.0 + jnp.exp(-z))
            x3 = x2 + pp * gate
            out_ref[...] = _rms(x3, gfin_ref[...])


def _final(lo, nchunk, dst, y, x1, rc, p2d, g_ple, w_gate_bf, w_proj_bf, g_final):
    t = x1.shape[0]
    tm = TOKEN_TILE
    row = lambda w: pl.BlockSpec((tm, w), lambda i, *_: (i, 0))
    full = lambda a: pl.BlockSpec(a.shape, lambda i, *_: (0, 0))
    grid_spec = pltpu.PrefetchScalarGridSpec(
        num_scalar_prefetch=3,
        grid=(t // tm,),
        in_specs=[pl.BlockSpec(memory_space=pl.ANY), row(D_MODEL), row(EXPERTS_PER_GROUP),
                  row(PLE_DIM), full(g_ple), full(w_gate_bf), full(w_proj_bf), full(g_final)],
        out_specs=row(D_MODEL),
        scratch_shapes=[pltpu.VMEM((TOKEN_TILE, TOKEN_TILE), jnp.bfloat16),
                        pltpu.VMEM((2, LOCAL_ROWS, D_MODEL), jnp.bfloat16),
                        pltpu.SemaphoreType.DMA((2,))],
    )
    return pl.pallas_call(
        _final_kernel,
        out_shape=jax.ShapeDtypeStruct((t, D_MODEL), jnp.float32),
        grid_spec=grid_spec,
        compiler_params=pltpu.CompilerParams(dimension_semantics=("arbitrary",),
                                             vmem_limit_bytes=VMEM_LIMIT),
        name="combine_ple_final",
    )(lo, nchunk, dst, y, x1, rc, p2d, g_ple, w_gate_bf, w_proj_bf, g_final)


def _dispatch_tables(counts):
    n_tt = counts.shape[0]
    p_rows = 2 * n_tt * TOKEN_TILE + n_tt * N_EXPERTS * (RUN_ALIGN - 1) + N_EXPERTS * ROW_TILE
    p_rows = -(-p_rows // ROW_TILE) * ROW_TILE
    n_tiles = p_rows // ROW_TILE
    n8 = (counts + RUN_ALIGN - 1) // RUN_ALIGN * RUN_ALIGN
    lo = jnp.cumsum(n8, axis=1) - n8
    total = jnp.sum(n8, axis=0)
    tiles_per = (total + ROW_TILE - 1) // ROW_TILE
    tile_end = jnp.cumsum(tiles_per)
    tile_start = tile_end - tiles_per
    g = tile_start[None, :] * ROW_TILE + (jnp.cumsum(n8, axis=0) - n8)
    n_used = tile_end[-1]
    tile_ids = jnp.arange(n_tiles, dtype=jnp.int32)
    clamped = jnp.minimum(tile_ids, n_used - 1)
    texp = jnp.sum((tile_end[None, :] <= clamped[:, None]).astype(jnp.int32), axis=1)
    experts = jnp.arange(N_EXPERTS, dtype=jnp.int32)
    of_tile = texp[:, None] == experts[None, :]
    per_tile = lambda v: jnp.sum(jnp.where(of_tile, v[None, :], 0), axis=1)
    first = ((tile_ids == per_tile(tile_start)) & (tile_ids < n_used)).astype(jnp.int32)
    pad_tile = jnp.where(total > 0, tile_end - 1, -1).astype(jnp.int32)
    has_rows = total > 0
    later = has_rows[None, :] & (experts[None, :] > experts[:, None])
    next_expert = jnp.min(jnp.where(later, experts[None, :], N_EXPERTS), axis=1)
    next_expert = jnp.where(next_expert < N_EXPERTS, next_expert, -1)
    slot = (jnp.cumsum(has_rows.astype(jnp.int32)) - 1) % 2
    chunk_row = jnp.arange(MAX_CHUNKS, dtype=jnp.int32)[None, :, None] * RUN_ALIGN
    in_run = (chunk_row >= lo[:, None, :]) & (chunk_row < (lo + n8)[:, None, :])
    dst = jnp.sum(jnp.where(in_run, g[:, None, :] + chunk_row - lo[:, None, :], 0), axis=2)
    nchunk = jnp.sum(n8, axis=1) // RUN_ALIGN
    i32 = lambda a: a.reshape(-1).astype(jnp.int32)
    return (texp, first, n_used.reshape(1).astype(jnp.int32), i32(per_tile(slot)),
            i32(per_tile(next_expert)), pad_tile, i32(lo), i32(nchunk), i32(dst), p_rows)


def kernel(x, p, g_mix, w_in, sink, g_grp_a, g_grp_b, w_out, g_ffn, w_router_group, b_router_group,
           w_router_expert, b_router_expert, w_expert_gate, w_expert_up, w_expert_down, g_ple,
           w_ple_gate, w_ple_proj, g_final):
    batch, seq, _ = x.shape
    n_tok = batch * seq
    depth = w_in.shape[0]
    assert depth == 1, "the final RMSNorm is fused into the last layer's epilogue"
    bf = jnp.bfloat16
    xc = x.reshape(n_tok, D_MODEL)
    for i in range(depth):
        *qkv_a, qb, kb, vb = _proj(xc, g_mix[i][None, :], w_in[i].astype(bf), batch, seq)
        oa = _attn_a(qkv_a, batch, seq)
        ob = _attn_b(qb, kb, vb, sink[i], batch, seq)

        wr = jnp.zeros((D_MODEL, ROUTER_ROWS), jnp.float32)
        wr = wr.at[:, 0:N_GROUPS].set(w_router_group[i])
        wr = wr.at[:, EXPERTS_PER_GROUP:EXPERTS_PER_GROUP + N_EXPERTS].set(w_router_expert[i])
        br = jnp.zeros((ROUTER_ROWS, 1), jnp.float32)
        br = br.at[0:N_GROUPS, 0].set(b_router_group[i])
        br = br.at[EXPERTS_PER_GROUP:EXPERTS_PER_GROUP + N_EXPERTS, 0].set(b_router_expert[i])

        x1, h2, route_i, route_c, counts = _mix(
            oa, ob, xc, g_grp_a[i][None, :], g_grp_b[i][None, :], w_out[i].astype(bf), g_ffn[i][None, :], wr, br)

        texp, first, nused, slot, nxt, pad_tile, lo, nchunk, dst, p_rows = _dispatch_tables(counts[:, :, 0])
        xs = _dispatch(lo, nchunk, dst, pad_tile, nused, route_i, h2, p_rows)
        y = _moe(texp, first, nused, slot, nxt, xs, w_expert_gate[i], w_expert_up[i], w_expert_down[i])
        xc = _final(lo, nchunk, dst, y, x1, route_c, p[i].reshape(n_tok, PLE_DIM), g_ple[i][None, :],
                    w_ple_gate[i].astype(bf), w_ple_proj[i].astype(bf), g_final[None, :])
    return xc.reshape(batch, seq, D_MODEL)
```

```python
import functools

import jax
import jax.numpy as jnp
from jax import lax
from jax.experimental import pallas as pl
from jax.experimental.pallas import tpu as pltpu

D_MODEL = 1024
HEAD_DIM = 64
GROUP_WIDTH = 512
N_HEADS = 8
B_KV_HEADS = 2
B_KV_WIDTH = B_KV_HEADS * HEAD_DIM
A_CONFIGS = ((128, 1), (512, 4), (2048, 16))
B_HALF_WINDOW = 128
N_GROUPS = 4
EXPERTS_PER_GROUP = 8
N_EXPERTS = N_GROUPS * EXPERTS_PER_GROUP
D_EXPERT = 512
PLE_DIM = 256
EPS = 1e-6
NEG = -1e30
LOG2E = 1.4426950408889634
LN2 = 0.6931471805599453

LANES = 128
Q_BLOCK = 128
HEADS_PER_PASS = 2
PASS_WIDTH = HEADS_PER_PASS * HEAD_DIM
ROW_TILE = 512
TOKEN_TILE = 512
ROUTER_ROWS = 128
VMEM_LIMIT = 48 * 1024 * 1024
ATTN_VMEM_LIMIT = 56 * 1024 * 1024


def _rms(x, g):
    r = lax.rsqrt(jnp.mean(x * x, axis=-1, keepdims=True) + EPS)
    return (x * r) * g


def _dot(a, b):
    return jnp.dot(a, b, preferred_element_type=jnp.float32)


PROJ_TILE = 512


def _proj_kernel(x_ref, g_ref, w_ref, *refs):
    a_refs, (qb_ref, kb_ref, vb_ref, res, *stage_refs) = refs[:9], refs[9:]
    h = _rms(x_ref[...], g_ref[...]).astype(jnp.bfloat16)
    scale = HEAD_DIM ** -0.5 * LOG2E
    W = GROUP_WIDTH
    tiles = GROUP_WIDTH // PASS_WIDTH
    qkv_a = (_dot(h, w_ref[:, 0:W]) * scale, _dot(h, w_ref[:, W:2 * W]), _dot(h, w_ref[:, 2 * W:3 * W]))
    for part in range(3):
        for pair in range(tiles):
            res[part * tiles + pair, 0] = qkv_a[part][:, pair * PASS_WIDTH:(pair + 1) * PASS_WIDTH]
    qb_ref[...] = (_dot(h, w_ref[:, 3 * W:4 * W]) * scale).astype(jnp.bfloat16)
    kb_ref[...] = _dot(h, w_ref[:, 4 * W:4 * W + B_KV_WIDTH]).astype(jnp.bfloat16)
    vb_ref[...] = _dot(h, w_ref[:, 4 * W + B_KV_WIDTH:4 * W + 2 * B_KV_WIDTH]).astype(jnp.bfloat16)
    stages = (res,) + tuple(stage_refs)
    prev_dil = 1
    for c, (_, dil) in enumerate(A_CONFIGS):
        ratio = dil // prev_dil
        rows = PROJ_TILE // dil
        for idx in range(3 * tiles):
            part, pair = idx // tiles, idx % tiles
            out = a_refs[3 * c + part]
            for rho in range(dil):
                if c == 0:
                    sub = res[idx, 0]
                else:
                    sub = stages[c - 1].at[idx, rho % prev_dil][pl.ds(rho // prev_dil, rows, stride=ratio), :]
                    if c < len(A_CONFIGS) - 1:
                        stages[c][idx, rho] = sub
                out[pair, rho] = sub.astype(jnp.bfloat16)
        prev_dil = dil


def _proj(x2d, g_mix, w_in_bf, batch, seq):
    t = x2d.shape[0]
    tm = PROJ_TILE
    tiles_per_seq = seq // tm
    in_width = w_in_bf.shape[1]
    n_pairs = N_HEADS // HEADS_PER_PASS
    row = lambda w: pl.BlockSpec((tm, w), lambda i: (i, 0))
    a_shapes, a_specs = [], []
    for _, dil in A_CONFIGS:
        for _ in range(3):
            a_shapes.append(jax.ShapeDtypeStruct((batch, n_pairs, dil, seq // dil, PASS_WIDTH),
                                                 jnp.bfloat16))
            a_specs.append(pl.BlockSpec((None, n_pairs, dil, tm // dil, PASS_WIDTH),
                                        lambda i: (i // tiles_per_seq, 0, 0, i % tiles_per_seq, 0)))
    wide = jax.ShapeDtypeStruct((t, GROUP_WIDTH), jnp.bfloat16)
    narrow = jax.ShapeDtypeStruct((t, B_KV_WIDTH), jnp.bfloat16)
    return pl.pallas_call(
        _proj_kernel,
        out_shape=tuple(a_shapes) + (wide, narrow, narrow),
        grid=(t // tm,),
        in_specs=[row(D_MODEL),
                  pl.BlockSpec((1, D_MODEL), lambda i: (0, 0)),
                  pl.BlockSpec((D_MODEL, in_width), lambda i: (0, 0))],
        out_specs=tuple(a_specs) + (row(GROUP_WIDTH), row(B_KV_WIDTH), row(B_KV_WIDTH)),
        scratch_shapes=[pltpu.VMEM((3 * GROUP_WIDTH // PASS_WIDTH, dil, tm // dil, PASS_WIDTH), jnp.float32)
                        for _, dil in A_CONFIGS[:-1]],
        compiler_params=pltpu.CompilerParams(dimension_semantics=("arbitrary",),
                                             vmem_limit_bytes=VMEM_LIMIT),
        name="proj",
    )(x2d, g_mix, w_in_bf)


def _fill_bias(bias_ref, half, nkeys, dist_scale):
    i = lax.broadcasted_iota(jnp.int32, (Q_BLOCK, nkeys), 0)
    j = lax.broadcasted_iota(jnp.int32, (Q_BLOCK, nkeys), 1)
    for var, rel0 in enumerate((0, -half, Q_BLOCK - nkeys)):
        d = jnp.abs(j + rel0 - i)
        dist = d.astype(jnp.float32) * dist_scale
        for h in range(N_HEADS):
            slope = 2.0 ** (-(h + 1))
            bias_ref[var, h] = jnp.where(d <= half, -(slope * dist) * LOG2E, NEG)


def _key_window(q0, seq_len, half, nkeys):
    lo = q0 - half
    ks = pl.multiple_of(jnp.clip(lo, 0, seq_len - nkeys), HEAD_DIM)
    var = jnp.where(lo < 0, 0, jnp.where(lo > seq_len - nkeys, 2, 1))
    return ks, var


def _fold_lane_tiles(a, op):
    out = a[..., 0:LANES]
    for t in range(1, a.shape[-1] // LANES):
        out = op(out, a[..., t * LANES:(t + 1) * LANES])
    return out


def _attend_pass(q, k, v, bias, sinks):
    nkeys = k.shape[0]
    lane_head = lax.broadcasted_iota(jnp.int32, (Q_BLOCK, PASS_WIDTH), 1) // HEAD_DIM
    zero = jnp.zeros_like(q)
    q_stack = jnp.concatenate(
        [jnp.where(lane_head == h, q, zero) for h in range(HEADS_PER_PASS)], axis=0)
    s = lax.dot_general(q_stack, k, (((1,), (1,)), ((), ())), preferred_element_type=jnp.float32)
    s = s.reshape(HEADS_PER_PASS, Q_BLOCK, nkeys) + bias
    m_tile = _fold_lane_tiles(s, jnp.maximum)
    if sinks is not None:
        tile_head = lax.broadcasted_iota(jnp.int32, m_tile.shape, 0)
        tile_lane = lax.broadcasted_iota(jnp.int32, m_tile.shape, 2)
        sk = jnp.zeros(m_tile.shape, jnp.float32)
        for h in range(HEADS_PER_PASS):
            sk = jnp.where(tile_head == h, sinks[h] * LOG2E, sk)
        m_tile = jnp.maximum(m_tile, sk)
    m = jnp.max(m_tile, axis=-1, keepdims=True)
    p = jnp.exp2(s - m)
    den_tile = _fold_lane_tiles(p, jnp.add)
    if sinks is not None:
        den_tile = den_tile + jnp.where(tile_lane == 0, jnp.exp2(sk - m), 0.0)
    den = jnp.sum(den_tile, axis=-1, keepdims=True)
    pv = _dot(p.reshape(HEADS_PER_PASS * Q_BLOCK, nkeys).astype(jnp.bfloat16), v)
    pv = pv.reshape(HEADS_PER_PASS, Q_BLOCK, PASS_WIDTH) * (1.0 / den)
    lse = m * LN2 + jnp.log(den)
    o = jnp.zeros((Q_BLOCK, PASS_WIDTH), jnp.float32)
    lse_lanes = jnp.zeros((Q_BLOCK, PASS_WIDTH), jnp.float32)
    for h in range(HEADS_PER_PASS):
        o = jnp.where(lane_head == h, pv[h], o)
        lse_lanes = jnp.where(lane_head == h, lse[h], lse_lanes)
    return o, lse_lanes


A_HALF = 64
A_KEYS = Q_BLOCK + 2 * A_HALF
PASSES_PER_BODY = 16


def _attn_a_kernel(*refs, seq):
    n_cfg = len(A_CONFIGS)
    qkv = refs[:3 * n_cfg]
    o_ref, bias_ref = refs[3 * n_cfg], refs[3 * n_cfg + 1]
    scratch = refs[3 * n_cfg + 2:]
    pair = pl.program_id(1)

    @pl.when((pl.program_id(0) == 0) & (pair == 0))
    def _():
        for c, (_, dil) in enumerate(A_CONFIGS):
            _fill_bias(bias_ref.at[c], A_HALF, A_KEYS, float(dil))

    for c, (window, dil) in enumerate(A_CONFIGS):
        assert window // (2 * dil) == A_HALF
        q_ref, k_ref, v_ref = qkv[3 * c:3 * c + 3]
        o_sc, l_sc = scratch[2 * c], scratch[2 * c + 1]
        sub_len = seq // dil
        blocks = sub_len // Q_BLOCK

        def body(idx, carry, q_ref=q_ref, k_ref=k_ref, v_ref=v_ref, o_sc=o_sc, l_sc=l_sc, c=c,
                 sub_len=sub_len, blocks=blocks):
            rho = idx // blocks
            q0 = pl.multiple_of((idx % blocks) * Q_BLOCK, Q_BLOCK)
            ks, var = _key_window(q0, sub_len, A_HALF, A_KEYS)
            bias = bias_ref[c, var, pl.ds(pair * HEADS_PER_PASS, HEADS_PER_PASS)]
            o, lse = _attend_pass(q_ref[rho, pl.ds(q0, Q_BLOCK), :], k_ref[rho, pl.ds(ks, A_KEYS), :],
                                  v_ref[rho, pl.ds(ks, A_KEYS), :], bias, None)
            o_sc[rho, pl.ds(q0, Q_BLOCK), :] = o
            l_sc[rho, pl.ds(q0, Q_BLOCK), :] = lse
            return carry

        lax.fori_loop(0, dil * blocks, body, 0, unroll=PASSES_PER_BODY)

    widest = A_CONFIGS[-1][1]
    rows = seq // widest
    for r in range(widest):
        outs, lses = [], []
        for c, (_, dil) in enumerate(A_CONFIGS):
            step = widest // dil
            idx = (r % dil, pl.ds(r // dil, rows, stride=step) if step > 1 else pl.ds(0, rows))
            outs.append(scratch[2 * c].at[idx[0]][idx[1], :])
            lses.append(scratch[2 * c + 1].at[idx[0]][idx[1], :])
        mx = functools.reduce(jnp.maximum, lses)
        es = [jnp.exp(l - mx) for l in lses]
        inv = 1.0 / functools.reduce(jnp.add, es)
        merged = functools.reduce(jnp.add, [(e * inv) * o for e, o in zip(es, outs)])
        o_ref[pl.ds(r, rows, stride=widest), :] = merged


def _attn_a(qkv, batch, seq):
    n_pairs = N_HEADS // HEADS_PER_PASS
    in_specs, scratch = [], [pltpu.VMEM((len(A_CONFIGS), 3, N_HEADS, Q_BLOCK, A_KEYS), jnp.float32)]
    for _, dil in A_CONFIGS:
        blk = (None, None, dil, seq // dil, PASS_WIDTH)
        in_specs += [pl.BlockSpec(blk, lambda b, p: (b, p, 0, 0, 0))] * 3
        scratch += [pltpu.VMEM((dil, seq // dil, PASS_WIDTH), jnp.float32)] * 2
    return pl.pallas_call(
        functools.partial(_attn_a_kernel, seq=seq),
        out_shape=jax.ShapeDtypeStruct((batch, seq, GROUP_WIDTH), jnp.float32),
        grid=(batch, n_pairs),
        in_specs=in_specs,
        out_specs=pl.BlockSpec((None, seq, PASS_WIDTH), lambda b, p: (b, 0, p)),
        scratch_shapes=scratch,
        compiler_params=pltpu.CompilerParams(dimension_semantics=("arbitrary", "arbitrary"),
                                             vmem_limit_bytes=ATTN_VMEM_LIMIT),
        name="attn_a",
    )(*qkv).reshape(batch * seq, GROUP_WIDTH)


def _attn_b_kernel(sink_ref, q_ref, k_ref, v_ref, o_ref, bias_ref, k4_ref, v4_ref, *, half, nkeys,
                   seq_len, q_tile):
    first = (pl.program_id(0) == 0) & (pl.program_id(1) == 0)

    @pl.when(first)
    def _():
        _fill_bias(bias_ref, half, nkeys, 1.0)

    @pl.when(pl.program_id(1) == 0)
    def _():
        c = lax.broadcasted_iota(jnp.int32, (B_KV_WIDTH, GROUP_WIDTH), 0)
        j = lax.broadcasted_iota(jnp.int32, (B_KV_WIDTH, GROUP_WIDTH), 1)
        src = (j // (GROUP_WIDTH // B_KV_HEADS)) * HEAD_DIM + j % HEAD_DIM
        rep = jnp.where(c == src, 1.0, 0.0).astype(jnp.bfloat16)
        chunk = 512

        def body(i, carry):
            rows = pl.ds(pl.multiple_of(i * chunk, chunk), chunk)
            k4_ref[rows, :] = _dot(k_ref[rows, :], rep).astype(jnp.bfloat16)
            v4_ref[rows, :] = _dot(v_ref[rows, :], rep).astype(jnp.bfloat16)
            return carry

        lax.fori_loop(0, seq_len // chunk, body, 0)

    base = pl.program_id(1) * q_tile

    def body(jb, carry):
        row0 = pl.multiple_of(jb * Q_BLOCK, Q_BLOCK)
        ks, var = _key_window(base + row0, seq_len, half, nkeys)
        for g in range(N_HEADS // HEADS_PER_PASS):
            cols = slice(g * PASS_WIDTH, (g + 1) * PASS_WIDTH)
            heads = pl.ds(g * HEADS_PER_PASS, HEADS_PER_PASS)
            sinks = [sink_ref[g * HEADS_PER_PASS + h] for h in range(HEADS_PER_PASS)]
            o, _ = _attend_pass(q_ref[pl.ds(row0, Q_BLOCK), cols], k4_ref[pl.ds(ks, nkeys), cols],
                                v4_ref[pl.ds(ks, nkeys), cols], bias_ref[var, heads], sinks)
            o_ref[pl.ds(row0, Q_BLOCK), cols] = o
        return carry

    lax.fori_loop(0, q_tile // Q_BLOCK, body, 0,
                  unroll=PASSES_PER_BODY // (N_HEADS // HEADS_PER_PASS))


def _attn_b(q, k, v, sink, batch, seq):
    half = B_HALF_WINDOW
    nkeys = Q_BLOCK + 2 * half
    q_tile = 1024
    view = lambda a: a.reshape(batch, seq, a.shape[-1])
    kern = functools.partial(_attn_b_kernel, half=half, nkeys=nkeys, seq_len=seq, q_tile=q_tile)
    o = pl.pallas_call(
        kern,
        out_shape=jax.ShapeDtypeStruct((batch, seq, GROUP_WIDTH), jnp.float32),
        grid=(batch, seq // q_tile),
        in_specs=[pl.BlockSpec(memory_space=pltpu.SMEM),
                  pl.BlockSpec((None, q_tile, GROUP_WIDTH), lambda b, i: (b, i, 0)),
                  pl.BlockSpec((None, seq, B_KV_WIDTH), lambda b, i: (b, 0, 0)),
                  pl.BlockSpec((None, seq, B_KV_WIDTH), lambda b, i: (b, 0, 0))],
        out_specs=pl.BlockSpec((None, q_tile, GROUP_WIDTH), lambda b, i: (b, i, 0)),
        scratch_shapes=[pltpu.VMEM((3, N_HEADS, Q_BLOCK, nkeys), jnp.float32),
                        pltpu.VMEM((seq, GROUP_WIDTH), jnp.bfloat16),
                        pltpu.VMEM((seq, GROUP_WIDTH), jnp.bfloat16)],
        compiler_params=pltpu.CompilerParams(dimension_semantics=("arbitrary", "arbitrary"),
                                             vmem_limit_bytes=VMEM_LIMIT),
        name="attn_b",
    )(sink, view(q), view(k), view(v))
    return o.reshape(batch * seq, GROUP_WIDTH)


def _mix_kernel(oa_ref, ob_ref, x_ref, ga_ref, gb_ref, wout_ref, gffn_ref, wr_ref, br_ref,
                x1_ref, h2_ref, ri_ref, rc_ref, cnt_ref):
    tm = x_ref.shape[0]
    na = _rms(oa_ref[...], ga_ref[...]).astype(jnp.bfloat16)
    nb = _rms(ob_ref[...], gb_ref[...]).astype(jnp.bfloat16)
    x1 = (x_ref[...] + _dot(na, wout_ref[0:GROUP_WIDTH, :])
          + _dot(nb, wout_ref[GROUP_WIDTH:2 * GROUP_WIDTH, :]))
    x1_ref[...] = x1
    h2 = _rms(x1, gffn_ref[...])
    h2_ref[...] = h2.astype(jnp.bfloat16)

    h_hi = h2.astype(jnp.bfloat16)
    h_lo = (h2 - h_hi.astype(jnp.float32)).astype(jnp.bfloat16)
    wr = wr_ref[...]
    w_hi = wr.astype(jnp.bfloat16)
    w_lo = (wr - w_hi.astype(jnp.float32)).astype(jnp.bfloat16)
    w_cat = jnp.concatenate([w_hi, w_lo], axis=1)
    lg2 = _dot(h_hi, w_cat) + _dot(h_lo, w_cat)
    lg = lg2[:, 0:ROUTER_ROWS] + lg2[:, ROUTER_ROWS:2 * ROUTER_ROWS]
    lgt = lg.T + br_ref[...]

    rows = lax.broadcasted_iota(jnp.int32, (EXPERTS_PER_GROUP, tm), 0)
    big = jnp.int32(1 << 20)
    glog = jnp.where(rows < N_GROUPS, lgt[0:EXPERTS_PER_GROUP], -jnp.inf)
    gmax = jnp.max(glog, axis=0, keepdims=True)
    gsel = jnp.min(jnp.where(glog == gmax, rows, big), axis=0, keepdims=True)
    gw = 1.0 / jnp.sum(jnp.exp(glog - gmax), axis=0, keepdims=True)
    esel = jnp.zeros((EXPERTS_PER_GROUP, tm), jnp.float32)
    for grp in range(N_GROUPS):
        lo = EXPERTS_PER_GROUP * (grp + 1)
        esel = jnp.where(gsel == grp, lgt[lo:lo + EXPERTS_PER_GROUP], esel)
    v0 = jnp.max(esel, axis=0, keepdims=True)
    i0 = jnp.min(jnp.where(esel == v0, rows, big), axis=0, keepdims=True)
    rest = jnp.where(rows == i0, -jnp.inf, esel)
    v1 = jnp.max(rest, axis=0, keepdims=True)
    i1 = jnp.min(jnp.where(rest == v1, rows, big), axis=0, keepdims=True)
    e = jnp.exp(v1 - v0)
    w0 = (1.0 / (1.0 + e)) * gw
    w1 = (e / (1.0 + e)) * gw
    eid0 = gsel * EXPERTS_PER_GROUP + i0
    eid1 = gsel * EXPERTS_PER_GROUP + i1
    ri_ref[...] = jnp.where(rows == 0, eid0, jnp.where(rows == 1, eid1, 0))
    rows_t = lax.broadcasted_iota(jnp.int32, (ROUTER_ROWS, tm), 0)
    slab = jnp.where(rows_t == 0, w0, jnp.where(rows_t == 1, w1, 0.0))
    slab = jnp.where(rows_t == 2, eid0.astype(jnp.float32),
                     jnp.where(rows_t == 3, eid1.astype(jnp.float32), slab))
    rc_ref[...] = slab.T[:, 0:EXPERTS_PER_GROUP]
    ids = lax.broadcasted_iota(jnp.int32, (N_EXPERTS, tm), 0)
    member = jnp.where((ids == eid0) | (ids == eid1), 1.0, 0.0)
    cnt_ref[...] = jnp.sum(member, axis=1, keepdims=True).astype(jnp.int32)


def _mix(oa, ob, x2d, g_a, g_b, w_out_bf, g_ffn, wr, br):
    t = x2d.shape[0]
    tm = TOKEN_TILE
    row = lambda w: pl.BlockSpec((tm, w), lambda i: (i, 0))
    full = lambda a: pl.BlockSpec(a.shape, lambda i: (0, 0))
    return pl.pallas_call(
        _mix_kernel,
        out_shape=(jax.ShapeDtypeStruct((t, D_MODEL), jnp.float32),
                   jax.ShapeDtypeStruct((t, D_MODEL), jnp.bfloat16),
                   jax.ShapeDtypeStruct((EXPERTS_PER_GROUP, t), jnp.int32),
                   jax.ShapeDtypeStruct((t, EXPERTS_PER_GROUP), jnp.float32),
                   jax.ShapeDtypeStruct((t // tm, N_EXPERTS, 1), jnp.int32)),
        grid=(t // tm,),
        in_specs=[row(GROUP_WIDTH), row(GROUP_WIDTH), row(D_MODEL),
                  full(g_a), full(g_b), full(w_out_bf), full(g_ffn), full(wr), full(br)],
        out_specs=(row(D_MODEL), row(D_MODEL),
                   pl.BlockSpec((EXPERTS_PER_GROUP, tm), lambda i: (0, i)),
                   row(EXPERTS_PER_GROUP),
                   pl.BlockSpec((None, N_EXPERTS, 1), lambda i: (i, 0, 0))),
        compiler_params=pltpu.CompilerParams(dimension_semantics=("arbitrary",),
                                             vmem_limit_bytes=VMEM_LIMIT),
        name="mix_router",
    )(oa, ob, x2d, g_a, g_b, w_out_bf, g_ffn, wr, br)


RUN_ALIGN = 16
LOCAL_ROWS = 2 * TOKEN_TILE + N_EXPERTS * RUN_ALIGN
MAX_CHUNKS = LOCAL_ROWS // RUN_ALIGN
SORT_BLOCK = 256
FINAL_BLOCK = 256


def _local_slots(member, lo_vec, prior, e_sel, axis):
    ids = lax.broadcasted_iota(jnp.int32, member.shape, axis)
    slot = jnp.where(ids == e_sel, lo_vec + prior, 0.0)
    return jnp.sum(slot, axis=axis, keepdims=True).astype(jnp.int32)


def _chunk_copy(dst_ref, tile, c, local, glob, sem, to_global):
    g = pl.multiple_of(dst_ref[tile * MAX_CHUNKS + c], RUN_ALIGN)
    l = pl.multiple_of(c * RUN_ALIGN, RUN_ALIGN)
    src, dst = local.at[pl.ds(l, RUN_ALIGN), :], glob.at[pl.ds(g, RUN_ALIGN), :]
    if not to_global:
        src, dst = dst, src
    return pltpu.make_async_copy(src, dst, sem)


def _start_chunks(nchunk_ref, dst_ref, tile, local, glob, sem, to_global):
    def body(c, carry):
        _chunk_copy(dst_ref, tile, c, local, glob, sem, to_global).start()
        return carry
    lax.fori_loop(0, nchunk_ref[tile], body, 0)


def _wait_chunks(nchunk_ref, dst_ref, tile, local, glob, sem, to_global):
    def body(c, carry):
        _chunk_copy(dst_ref, tile, c, local, glob, sem, to_global).wait()
        return carry
    lax.fori_loop(0, nchunk_ref[tile], body, 0)


def _lo_vector(lo_ref, tile, shape, axis):
    ids = lax.broadcasted_iota(jnp.int32, shape, axis)
    vec = jnp.zeros(shape, jnp.float32)
    for e in range(N_EXPERTS):
        vec = jnp.where(ids == e, lo_ref[tile * N_EXPERTS + e].astype(jnp.float32), vec)
    return vec


def _dispatch_kernel(lo_ref, nchunk_ref, dst_ref, pad_tile_ref, nused_ref, ri_ref, h_ref, xs_hbm,
                     upper, sbuf, zbuf, sem, zsem, *, n_tiles):
    i = pl.program_id(0)
    n_steps = pl.num_programs(0)
    slot = i % 2

    def zero_tile(tile):
        rows = pl.ds(pl.multiple_of(tile * ROW_TILE, ROW_TILE), ROW_TILE)
        return pltpu.make_async_copy(zbuf, xs_hbm.at[rows, :], zsem)

    @pl.when(i == 0)
    def _():
        r = lax.broadcasted_iota(jnp.int32, (TOKEN_TILE, TOKEN_TILE), 0)
        c = lax.broadcasted_iota(jnp.int32, (TOKEN_TILE, TOKEN_TILE), 1)
        upper[...] = jnp.where(r < c, 1.0, 0.0).astype(jnp.bfloat16)
        zbuf[...] = jnp.zeros_like(zbuf)
        for e in range(N_EXPERTS):
            @pl.when(pad_tile_ref[e] >= 0)
            def _():
                zero_tile(pad_tile_ref[e]).start()
        lax.fori_loop(nused_ref[0], n_tiles, lambda t, c: (zero_tile(t).start(), c)[1], 0)
        for e in range(N_EXPERTS):
            @pl.when(pad_tile_ref[e] >= 0)
            def _():
                zero_tile(pad_tile_ref[e]).wait()
        lax.fori_loop(nused_ref[0], n_tiles, lambda t, c: (zero_tile(t).wait(), c)[1], 0)

    e0 = ri_ref[0:1, :]
    e1 = ri_ref[1:2, :]
    ids = lax.broadcasted_iota(jnp.int32, (N_EXPERTS, TOKEN_TILE), 0)
    member = jnp.where((ids == e0) | (ids == e1), 1.0, 0.0)
    prior = _dot(member.astype(jnp.bfloat16), upper[...])
    lo_vec = _lo_vector(lo_ref, i, (N_EXPERTS, 1), 0)
    d0 = _local_slots(member, lo_vec, prior, e0, 0)
    d1 = _local_slots(member, lo_vec, prior, e1, 0)
    local = sbuf.at[slot]
    used_rows = nchunk_ref[i] * RUN_ALIGN
    h = h_ref[...]
    for blk in range(LOCAL_ROWS // SORT_BLOCK):
        @pl.when(blk * SORT_BLOCK < used_rows)
        def _():
            rows = blk * SORT_BLOCK + lax.broadcasted_iota(jnp.int32, (SORT_BLOCK, TOKEN_TILE), 0)
            select = jnp.where((rows == d0) | (rows == d1), 1.0, 0.0).astype(jnp.bfloat16)
            local[blk * SORT_BLOCK:(blk + 1) * SORT_BLOCK, :] = _dot(select, h).astype(jnp.bfloat16)

    @pl.when(i > 0)
    def _():
        _wait_chunks(nchunk_ref, dst_ref, i - 1, sbuf.at[1 - slot], xs_hbm, sem, True)

    _start_chunks(nchunk_ref, dst_ref, i, local, xs_hbm, sem, True)

    @pl.when(i == n_steps - 1)
    def _():
        _wait_chunks(nchunk_ref, dst_ref, i, local, xs_hbm, sem, True)


def _dispatch(lo, nchunk, dst, pad_tile, nused, route_i, h2, p_rows):
    n_tok = h2.shape[0]
    grid_spec = pltpu.PrefetchScalarGridSpec(
        num_scalar_prefetch=5,
        grid=(n_tok // TOKEN_TILE,),
        in_specs=[pl.BlockSpec((EXPERTS_PER_GROUP, TOKEN_TILE), lambda i, *_: (0, i)),
                  pl.BlockSpec((TOKEN_TILE, D_MODEL), lambda i, *_: (i, 0))],
        out_specs=pl.BlockSpec(memory_space=pl.ANY),
        scratch_shapes=[pltpu.VMEM((TOKEN_TILE, TOKEN_TILE), jnp.bfloat16),
                        pltpu.VMEM((2, LOCAL_ROWS, D_MODEL), jnp.bfloat16),
                        pltpu.VMEM((ROW_TILE, D_MODEL), jnp.bfloat16),
                        pltpu.SemaphoreType.DMA, pltpu.SemaphoreType.DMA],
    )
    return pl.pallas_call(
        functools.partial(_dispatch_kernel, n_tiles=p_rows // ROW_TILE),
        out_shape=jax.ShapeDtypeStruct((p_rows, D_MODEL), jnp.bfloat16),
        grid_spec=grid_spec,
        compiler_params=pltpu.CompilerParams(dimension_semantics=("arbitrary",),
                                             vmem_limit_bytes=VMEM_LIMIT),
        name="dispatch",
    )(lo, nchunk, dst, pad_tile, nused, route_i, h2)


def _moe_kernel(texp_ref, first_ref, nused_ref, slot_ref, next_ref, xs_ref, wg_hbm, wu_hbm, wd_hbm,
                y_ref, wg_f32, wu_f32, wd_f32, wg_bf, wu_bf, wd_bf, sems):
    j = pl.program_id(0)

    def fetch(expert, slot):
        return [pltpu.make_async_copy(src.at[expert], dst.at[slot], sems.at[slot, k])
                for k, (src, dst) in enumerate(((wg_hbm, wg_f32), (wu_hbm, wu_f32), (wd_hbm, wd_f32)))]

    @pl.when(j == 0)
    def _():
        for cp in fetch(texp_ref[0], 0):
            cp.start()

    @pl.when(j < nused_ref[0])
    def _():
        @pl.when(first_ref[j] == 1)
        def _():
            for s in range(2):
                @pl.when(slot_ref[j] == s)
                def _():
                    for cp in fetch(texp_ref[j], s):
                        cp.wait()
                    wg_bf[...] = wg_f32[s].astype(jnp.bfloat16)
                    wu_bf[...] = wu_f32[s].astype(jnp.bfloat16)
                    wd_bf[...] = wd_f32[s].astype(jnp.bfloat16)

                    @pl.when(next_ref[j] >= 0)
                    def _():
                        for cp in fetch(next_ref[j], 1 - s):
                            cp.start()

        x = xs_ref[...]
        a = _dot(x, wg_bf[...])
        u = _dot(x, wu_bf[...])
        hid = (a * (1.0 / (1.0 + jnp.exp(-a)))) * u
        y_ref[...] = _dot(hid.astype(jnp.bfloat16), wd_bf[...]).astype(jnp.bfloat16)

    @pl.when(j >= nused_ref[0])
    def _():
        y_ref[...] = jnp.zeros_like(y_ref)


def _moe(texp, first, nused, slot, nxt, xs, w_gate, w_up, w_down):
    p_rows = xs.shape[0]
    n_tiles = p_rows // ROW_TILE
    used = lambda j, nu: jnp.minimum(j, nu[0] - 1)
    grid_spec = pltpu.PrefetchScalarGridSpec(
        num_scalar_prefetch=5,
        grid=(n_tiles,),
        in_specs=[
            pl.BlockSpec((ROW_TILE, D_MODEL), lambda j, te, fi, nu, sl, nx: (used(j, nu), 0)),
            pl.BlockSpec(memory_space=pl.ANY),
            pl.BlockSpec(memory_space=pl.ANY),
            pl.BlockSpec(memory_space=pl.ANY),
        ],
        out_specs=pl.BlockSpec((ROW_TILE, D_MODEL), lambda j, te, fi, nu, sl, nx: (j, 0)),
        scratch_shapes=[pltpu.VMEM((2, D_MODEL, D_EXPERT), jnp.float32),
                        pltpu.VMEM((2, D_MODEL, D_EXPERT), jnp.float32),
                        pltpu.VMEM((2, D_EXPERT, D_MODEL), jnp.float32),
                        pltpu.VMEM((D_MODEL, D_EXPERT), jnp.bfloat16),
                        pltpu.VMEM((D_MODEL, D_EXPERT), jnp.bfloat16),
                        pltpu.VMEM((D_EXPERT, D_MODEL), jnp.bfloat16),
                        pltpu.SemaphoreType.DMA((2, 3))],
    )
    return pl.pallas_call(
        _moe_kernel,
        out_shape=jax.ShapeDtypeStruct((p_rows, D_MODEL), jnp.bfloat16),
        grid_spec=grid_spec,
        compiler_params=pltpu.CompilerParams(dimension_semantics=("arbitrary",),
                                             vmem_limit_bytes=VMEM_LIMIT),
        name="moe_experts",
    )(texp, first, nused, slot, nxt, xs, w_gate, w_up, w_down)


def _final_kernel(lo_ref, nchunk_ref, dst_ref, y_hbm, x1_ref, rc_ref, p_ref, gple_ref, wg_ref, wp_ref,
                  gfin_ref, out_ref, lower, ybuf, sems):
    i = pl.program_id(0)
    n_steps = pl.num_programs(0)
    slot = i % 2

    @pl.when(i == 0)
    def _():
        r = lax.broadcasted_iota(jnp.int32, (TOKEN_TILE, TOKEN_TILE), 0)
        c = lax.broadcasted_iota(jnp.int32, (TOKEN_TILE, TOKEN_TILE), 1)
        lower[...] = jnp.where(c < r, 1.0, 0.0).astype(jnp.bfloat16)
        ybuf[...] = jnp.zeros_like(ybuf)
        _start_chunks(nchunk_ref, dst_ref, 0, ybuf.at[0], y_hbm, sems.at[0], False)

    @pl.when(i + 1 < n_steps)
    def _():
        _start_chunks(nchunk_ref, dst_ref, i + 1, ybuf.at[1 - slot], y_hbm, sems.at[1 - slot], False)

    rc = rc_ref[...]
    w0, w1 = rc[:, 0:1], rc[:, 1:2]
    e0, e1 = rc[:, 2:3].astype(jnp.int32), rc[:, 3:4].astype(jnp.int32)
    ids = lax.broadcasted_iota(jnp.int32, (TOKEN_TILE, ROUTER_ROWS), 1)
    member = jnp.where((ids == e0) | (ids == e1), 1.0, 0.0)
    prior = _dot(lower[...], member.astype(jnp.bfloat16))
    lo_vec = _lo_vector(lo_ref, i, (1, ROUTER_ROWS), 1)
    d0 = _local_slots(member, lo_vec, prior, e0, 1)
    d1 = _local_slots(member, lo_vec, prior, e1, 1)

    local = ybuf.at[slot]
    _wait_chunks(nchunk_ref, dst_ref, i, local, y_hbm, sems.at[slot], False)
    for blk in range(TOKEN_TILE // FINAL_BLOCK):
        rows = slice(blk * FINAL_BLOCK, (blk + 1) * FINAL_BLOCK)
        cols = lax.broadcasted_iota(jnp.int32, (FINAL_BLOCK, LOCAL_ROWS), 1)
        pick = jnp.where(cols == d0[rows], w0[rows], jnp.where(cols == d1[rows], w1[rows], 0.0))
        x2 = x1_ref[rows, :] + _dot(pick.astype(jnp.bfloat16), local[...])
        z = _dot(_rms(x2, gple_ref[...]).astype(jnp.bfloat16), wg_ref[...])
        gate = 1.0 / (1.0 + jnp.exp(-z))
        x3 = x2 + _dot(p_ref[rows, :].astype(jnp.bfloat16), wp_ref[...]) * gate
        out_ref[rows, :] = _rms(x3, gfin_ref[...])


def _final(lo, nchunk, dst, y, x1, rc, p2d, g_ple, w_gate_bf, w_proj_bf, g_final):
    t = x1.shape[0]
    tm = TOKEN_TILE
    row = lambda w: pl.BlockSpec((tm, w), lambda i, *_: (i, 0))
    full = lambda a: pl.BlockSpec(a.shape, lambda i, *_: (0, 0))
    grid_spec = pltpu.PrefetchScalarGridSpec(
        num_scalar_prefetch=3,
        grid=(t // tm,),
        in_specs=[pl.BlockSpec(memory_space=pl.ANY), row(D_MODEL), row(EXPERTS_PER_GROUP),
                  row(PLE_DIM), full(g_ple), full(w_gate_bf), full(w_proj_bf), full(g_final)],
        out_specs=row(D_MODEL),
        scratch_shapes=[pltpu.VMEM((TOKEN_TILE, TOKEN_TILE), jnp.bfloat16),
                        pltpu.VMEM((2, LOCAL_ROWS, D_MODEL), jnp.bfloat16),
                        pltpu.SemaphoreType.DMA((2,))],
    )
    return pl.pallas_call(
        _final_kernel,
        out_shape=jax.ShapeDtypeStruct((t, D_MODEL), jnp.float32),
        grid_spec=grid_spec,
        compiler_params=pltpu.CompilerParams(dimension_semantics=("arbitrary",),
                                             vmem_limit_bytes=VMEM_LIMIT),
        name="combine_ple_final",
    )(lo, nchunk, dst, y, x1, rc, p2d, g_ple, w_gate_bf, w_proj_bf, g_final)


def _dispatch_tables(counts):
    n_tt = counts.shape[0]
    p_rows = 2 * n_tt * TOKEN_TILE + n_tt * N_EXPERTS * (RUN_ALIGN - 1) + N_EXPERTS * ROW_TILE
    p_rows = -(-p_rows // ROW_TILE) * ROW_TILE
    n_tiles = p_rows // ROW_TILE
    n8 = (counts + RUN_ALIGN - 1) // RUN_ALIGN * RUN_ALIGN
    lo = jnp.cumsum(n8, axis=1) - n8
    total = jnp.sum(n8, axis=0)
    tiles_per = (total + ROW_TILE - 1) // ROW_TILE
    tile_end = jnp.cumsum(tiles_per)
    tile_start = tile_end - tiles_per
    g = tile_start[None, :] * ROW_TILE + (jnp.cumsum(n8, axis=0) - n8)
    n_used = tile_end[-1]
    tile_ids = jnp.arange(n_tiles, dtype=jnp.int32)
    clamped = jnp.minimum(tile_ids, n_used - 1)
    texp = jnp.sum((tile_end[None, :] <= clamped[:, None]).astype(jnp.int32), axis=1)
    experts = jnp.arange(N_EXPERTS, dtype=jnp.int32)
    of_tile = texp[:, None] == experts[None, :]
    per_tile = lambda v: jnp.sum(jnp.where(of_tile, v[None, :], 0), axis=1)
    first = ((tile_ids == per_tile(tile_start)) & (tile_ids < n_used)).astype(jnp.int32)
    pad_tile = jnp.where(total > 0, tile_end - 1, -1).astype(jnp.int32)
    has_rows = total > 0
    later = has_rows[None, :] & (experts[None, :] > experts[:, None])
    next_expert = jnp.min(jnp.where(later, experts[None, :], N_EXPERTS), axis=1)
    next_expert = jnp.where(next_expert < N_EXPERTS, next_expert, -1)
    slot = (jnp.cumsum(has_rows.astype(jnp.int32)) - 1) % 2
    chunk_row = jnp.arange(MAX_CHUNKS, dtype=jnp.int32)[None, :, None] * RUN_ALIGN
    in_run = (chunk_row >= lo[:, None, :]) & (chunk_row < (lo + n8)[:, None, :])
    dst = jnp.sum(jnp.where(in_run, g[:, None, :] + chunk_row - lo[:, None, :], 0), axis=2)
    nchunk = jnp.sum(n8, axis=1) // RUN_ALIGN
    i32 = lambda a: a.reshape(-1).astype(jnp.int32)
    return (texp, first, n_used.reshape(1).astype(jnp.int32), i32(per_tile(slot)),
            i32(per_tile(next_expert)), pad_tile, i32(lo), i32(nchunk), i32(dst), p_rows)


def kernel(x, p, g_mix, w_in, sink, g_grp_a, g_grp_b, w_out, g_ffn, w_router_group, b_router_group,
           w_router_expert, b_router_expert, w_expert_gate, w_expert_up, w_expert_down, g_ple,
           w_ple_gate, w_ple_proj, g_final):
    batch, seq, _ = x.shape
    n_tok = batch * seq
    depth = w_in.shape[0]
    assert depth == 1, "the final RMSNorm is fused into the last layer's epilogue"
    bf = jnp.bfloat16
    xc = x.reshape(n_tok, D_MODEL)
    for i in range(depth):
        *qkv_a, qb, kb, vb = _proj(xc, g_mix[i][None, :], w_in[i].astype(bf), batch, seq)
        oa = _attn_a(qkv_a, batch, seq)
        ob = _attn_b(qb, kb, vb, sink[i], batch, seq)

        wr = jnp.zeros((D_MODEL, ROUTER_ROWS), jnp.float32)
        wr = wr.at[:, 0:N_GROUPS].set(w_router_group[i])
        wr = wr.at[:, EXPERTS_PER_GROUP:EXPERTS_PER_GROUP + N_EXPERTS].set(w_router_expert[i])
        br = jnp.zeros((ROUTER_ROWS, 1), jnp.float32)
        br = br.at[0:N_GROUPS, 0].set(b_router_group[i])
        br = br.at[EXPERTS_PER_GROUP:EXPERTS_PER_GROUP + N_EXPERTS, 0].set(b_router_expert[i])

        x1, h2, route_i, route_c, counts = _mix(
            oa, ob, xc, g_grp_a[i][None, :], g_grp_b[i][None, :], w_out[i].astype(bf), g_ffn[i][None, :], wr, br)

        texp, first, nused, slot, nxt, pad_tile, lo, nchunk, dst, p_rows = _dispatch_tables(counts[:, :, 0])
        xs = _dispatch(lo, nchunk, dst, pad_tile, nused, route_i, h2, p_rows)
        y = _moe(texp, first, nused, slot, nxt, xs, w_expert_gate[i], w_expert_up[i], w_expert_down[i])
        xc = _final(lo, nchunk, dst, y, x1, route_c, p[i].reshape(n_tok, PLE_DIM), g_ple[i][None, :],
                    w_ple_gate[i].astype(bf), w_ple_proj[i].astype(bf), g_final[None, :])
    return xc.reshape(batch, seq, D_MODEL)
```

```python
import functools

import jax
import jax.numpy as jnp
from jax import lax
from jax.experimental import pallas as pl
from jax.experimental.pallas import tpu as pltpu

D_MODEL = 1024
HEAD_DIM = 64
GROUP_WIDTH = 512
N_HEADS = 8
B_KV_HEADS = 2
B_KV_WIDTH = B_KV_HEADS * HEAD_DIM
A_CONFIGS = ((128, 1), (512, 4), (2048, 16))
B_HALF_WINDOW = 128
N_GROUPS = 4
EXPERTS_PER_GROUP = 8
N_EXPERTS = N_GROUPS * EXPERTS_PER_GROUP
D_EXPERT = 512
PLE_DIM = 256
EPS = 1e-6
NEG = -1e30
LOG2E = 1.4426950408889634

LANES = 128
Q_BLOCK = 128
HEADS_PER_PASS = 2
PASS_WIDTH = HEADS_PER_PASS * HEAD_DIM
ROW_TILE = 512
TOKEN_TILE = 512
MOE_BLOCK = 256
ROUTER_ROWS = 128
VMEM_LIMIT = 48 * 1024 * 1024
ATTN_VMEM_LIMIT = 56 * 1024 * 1024


def _rms(x, g):
    r = lax.rsqrt(jnp.mean(x * x, axis=-1, keepdims=True) + EPS)
    return (x * r) * g


def _dot(a, b):
    return jnp.dot(a, b, preferred_element_type=jnp.float32)


PROJ_TILE = 512


def _proj_kernel(x_ref, g_ref, w_ref, *refs):
    a_refs, (qb_ref, kb_ref, vb_ref, res, *stage_refs) = refs[:9], refs[9:]
    h = _rms(x_ref[...], g_ref[...]).astype(jnp.bfloat16)
    scale = HEAD_DIM ** -0.5 * LOG2E
    W = GROUP_WIDTH
    tiles = GROUP_WIDTH // PASS_WIDTH
    qkv_a = (_dot(h, w_ref[:, 0:W]) * scale, _dot(h, w_ref[:, W:2 * W]), _dot(h, w_ref[:, 2 * W:3 * W]))
    for part in range(3):
        for pair in range(tiles):
            res[part * tiles + pair, 0] = qkv_a[part][:, pair * PASS_WIDTH:(pair + 1) * PASS_WIDTH]
    qb_ref[...] = (_dot(h, w_ref[:, 3 * W:4 * W]) * scale).astype(jnp.bfloat16)
    kb_ref[...] = _dot(h, w_ref[:, 4 * W:4 * W + B_KV_WIDTH]).astype(jnp.bfloat16)
    vb_ref[...] = _dot(h, w_ref[:, 4 * W + B_KV_WIDTH:4 * W + 2 * B_KV_WIDTH]).astype(jnp.bfloat16)
    stages = (res,) + tuple(stage_refs)
    prev_dil = 1
    for c, (_, dil) in enumerate(A_CONFIGS):
        ratio = dil // prev_dil
        rows = PROJ_TILE // dil
        for idx in range(3 * tiles):
            part, pair = idx // tiles, idx % tiles
            out = a_refs[3 * c + part]
            for rho in range(dil):
                if c == 0:
                    sub = res[idx, 0]
                else:
                    sub = stages[c - 1].at[idx, rho % prev_dil][pl.ds(rho // prev_dil, rows, stride=ratio), :]
                    if c < len(A_CONFIGS) - 1:
                        stages[c][idx, rho] = sub
                out[pair, rho] = sub.astype(jnp.bfloat16)
        prev_dil = dil


def _proj(x2d, g_mix, w_in_bf, batch, seq):
    t = x2d.shape[0]
    tm = PROJ_TILE
    tiles_per_seq = seq // tm
    in_width = w_in_bf.shape[1]
    n_pairs = N_HEADS // HEADS_PER_PASS
    row = lambda w: pl.BlockSpec((tm, w), lambda i: (i, 0))
    a_shapes, a_specs = [], []
    for _, dil in A_CONFIGS:
        for _ in range(3):
            a_shapes.append(jax.ShapeDtypeStruct((batch, n_pairs, dil, seq // dil, PASS_WIDTH),
                                                 jnp.bfloat16))
            a_specs.append(pl.BlockSpec((None, n_pairs, dil, tm // dil, PASS_WIDTH),
                                        lambda i: (i // tiles_per_seq, 0, 0, i % tiles_per_seq, 0)))
    wide = jax.ShapeDtypeStruct((t, GROUP_WIDTH), jnp.bfloat16)
    narrow = jax.ShapeDtypeStruct((t, B_KV_WIDTH), jnp.bfloat16)
    return pl.pallas_call(
        _proj_kernel,
        out_shape=tuple(a_shapes) + (wide, narrow, narrow),
        grid=(t // tm,),
        in_specs=[row(D_MODEL),
                  pl.BlockSpec((1, D_MODEL), lambda i: (0, 0)),
                  pl.BlockSpec((D_MODEL, in_width), lambda i: (0, 0))],
        out_specs=tuple(a_specs) + (row(GROUP_WIDTH), row(B_KV_WIDTH), row(B_KV_WIDTH)),
        scratch_shapes=[pltpu.VMEM((3 * GROUP_WIDTH // PASS_WIDTH, dil, tm // dil, PASS_WIDTH), jnp.float32)
                        for _, dil in A_CONFIGS[:-1]],
        compiler_params=pltpu.CompilerParams(dimension_semantics=("arbitrary",),
                                             vmem_limit_bytes=VMEM_LIMIT),
        name="proj",
    )(x2d, g_mix, w_in_bf)


def _fill_bias(bias_ref, half, nkeys, dist_scale):
    i = lax.broadcasted_iota(jnp.int32, (Q_BLOCK, nkeys), 0)
    j = lax.broadcasted_iota(jnp.int32, (Q_BLOCK, nkeys), 1)
    for var, rel0 in enumerate((0, -half, Q_BLOCK - nkeys)):
        d = jnp.abs(j + rel0 - i)
        dist = d.astype(jnp.float32) * dist_scale
        for h in range(N_HEADS):
            slope = 2.0 ** (-(h + 1))
            bias_ref[var, h] = jnp.where(d <= half, -(slope * dist) * LOG2E, NEG)


def _key_window(q0, seq_len, half, nkeys):
    lo = q0 - half
    ks = pl.multiple_of(jnp.clip(lo, 0, seq_len - nkeys), HEAD_DIM)
    var = jnp.where(lo < 0, 0, jnp.where(lo > seq_len - nkeys, 2, 1))
    return ks, var


def _fold_lane_tiles(a, op):
    out = a[..., 0:LANES]
    for t in range(1, a.shape[-1] // LANES):
        out = op(out, a[..., t * LANES:(t + 1) * LANES])
    return out


def _attend_pass(q, k, v, bias, sinks):
    nkeys = k.shape[0]
    lane_head = lax.broadcasted_iota(jnp.int32, (Q_BLOCK, PASS_WIDTH), 1) // HEAD_DIM
    zero = jnp.zeros_like(q)
    q_stack = jnp.concatenate(
        [jnp.where(lane_head == h, q, zero) for h in range(HEADS_PER_PASS)], axis=0)
    s = lax.dot_general(q_stack, k, (((1,), (1,)), ((), ())), preferred_element_type=jnp.float32)
    s = s.reshape(HEADS_PER_PASS, Q_BLOCK, nkeys) + bias
    m_tile = _fold_lane_tiles(s, jnp.maximum)
    if sinks is not None:
        tile_head = lax.broadcasted_iota(jnp.int32, m_tile.shape, 0)
        tile_lane = lax.broadcasted_iota(jnp.int32, m_tile.shape, 2)
        sk = jnp.zeros(m_tile.shape, jnp.float32)
        for h in range(HEADS_PER_PASS):
            sk = jnp.where(tile_head == h, sinks[h] * LOG2E, sk)
        m_tile = jnp.maximum(m_tile, sk)
    m = jnp.max(m_tile, axis=-1, keepdims=True)
    p = jnp.exp2(s - m)
    den_tile = _fold_lane_tiles(p, jnp.add)
    if sinks is not None:
        den_tile = den_tile + jnp.where(tile_lane == 0, jnp.exp2(sk - m), 0.0)
    den = jnp.sum(den_tile, axis=-1, keepdims=True)
    pv = _dot(p.reshape(HEADS_PER_PASS * Q_BLOCK, nkeys).astype(jnp.bfloat16), v)
    pv = pv.reshape(HEADS_PER_PASS, Q_BLOCK, PASS_WIDTH)
    shape = (Q_BLOCK, PASS_WIDTH)
    o, m_lanes, den_lanes = (jnp.zeros(shape, jnp.float32) for _ in range(3))
    for h in range(HEADS_PER_PASS):
        o = jnp.where(lane_head == h, pv[h], o)
        m_lanes = jnp.where(lane_head == h, m[h], m_lanes)
        den_lanes = jnp.where(lane_head == h, den[h], den_lanes)
    return o, m_lanes, den_lanes


A_HALF = 64
A_KEYS = Q_BLOCK + 2 * A_HALF
PASSES_PER_BODY = 16


def _attn_a_kernel(*refs, seq):
    n_cfg = len(A_CONFIGS)
    qkv = refs[:3 * n_cfg]
    o_ref, bias_ref = refs[3 * n_cfg], refs[3 * n_cfg + 1]
    scratch = refs[3 * n_cfg + 2:]
    pair = pl.program_id(1)

    @pl.when((pl.program_id(0) == 0) & (pair == 0))
    def _():
        for c, (_, dil) in enumerate(A_CONFIGS):
            _fill_bias(bias_ref.at[c], A_HALF, A_KEYS, float(dil))

    for c, (window, dil) in enumerate(A_CONFIGS):
        assert window // (2 * dil) == A_HALF
        q_ref, k_ref, v_ref = qkv[3 * c:3 * c + 3]
        o_sc, m_sc, d_sc = scratch[3 * c:3 * c + 3]
        sub_len = seq // dil
        blocks = sub_len // Q_BLOCK

        def body(idx, carry, q_ref=q_ref, k_ref=k_ref, v_ref=v_ref, o_sc=o_sc, m_sc=m_sc, d_sc=d_sc, c=c,
                 sub_len=sub_len, blocks=blocks):
            rho = idx // blocks
            q0 = pl.multiple_of((idx % blocks) * Q_BLOCK, Q_BLOCK)
            ks, var = _key_window(q0, sub_len, A_HALF, A_KEYS)
            bias = bias_ref[c, var, pl.ds(pair * HEADS_PER_PASS, HEADS_PER_PASS)]
            o, m, den = _attend_pass(q_ref[rho, pl.ds(q0, Q_BLOCK), :], k_ref[rho, pl.ds(ks, A_KEYS), :],
                                     v_ref[rho, pl.ds(ks, A_KEYS), :], bias, None)
            o_sc[rho, pl.ds(q0, Q_BLOCK), :] = o
            m_sc[rho, pl.ds(q0, Q_BLOCK), :] = m
            d_sc[rho, pl.ds(q0, Q_BLOCK), :] = den
            return carry

        lax.fori_loop(0, dil * blocks, body, 0, unroll=PASSES_PER_BODY)

    widest = A_CONFIGS[-1][1]
    rows = seq // widest
    for r in range(widest):
        pvs, ms, dens = [], [], []
        for c, (_, dil) in enumerate(A_CONFIGS):
            step = widest // dil
            sub = pl.ds(r // dil, rows, stride=step) if step > 1 else pl.ds(0, rows)
            pvs.append(scratch[3 * c].at[r % dil][sub, :])
            ms.append(scratch[3 * c + 1].at[r % dil][sub, :])
            dens.append(scratch[3 * c + 2].at[r % dil][sub, :])
        mx = functools.reduce(jnp.maximum, ms)
        es = [jnp.exp2(m - mx) for m in ms]
        total = functools.reduce(jnp.add, [e * d for e, d in zip(es, dens)])
        merged = functools.reduce(jnp.add, [e * pv for e, pv in zip(es, pvs)]) * (1.0 / total)
        o_ref[pl.ds(r, rows, stride=widest), :] = merged


def _attn_a(qkv, batch, seq):
    n_pairs = N_HEADS // HEADS_PER_PASS
    in_specs, scratch = [], [pltpu.VMEM((len(A_CONFIGS), 3, N_HEADS, Q_BLOCK, A_KEYS), jnp.float32)]
    for _, dil in A_CONFIGS:
        blk = (None, None, dil, seq // dil, PASS_WIDTH)
        in_specs += [pl.BlockSpec(blk, lambda b, p: (b, p, 0, 0, 0))] * 3
        scratch += [pltpu.VMEM((dil, seq // dil, PASS_WIDTH), jnp.float32)] * 3
    return pl.pallas_call(
        functools.partial(_attn_a_kernel, seq=seq),
        out_shape=jax.ShapeDtypeStruct((batch, seq, GROUP_WIDTH), jnp.float32),
        grid=(batch, n_pairs),
        in_specs=in_specs,
        out_specs=pl.BlockSpec((None, seq, PASS_WIDTH), lambda b, p: (b, 0, p)),
        scratch_shapes=scratch,
        compiler_params=pltpu.CompilerParams(dimension_semantics=("arbitrary", "arbitrary"),
                                             vmem_limit_bytes=ATTN_VMEM_LIMIT),
        name="attn_a",
    )(*qkv).reshape(batch * seq, GROUP_WIDTH)


def _attn_b_kernel(sink_ref, q_ref, k_ref, v_ref, o_ref, bias_ref, k4_ref, v4_ref, *, half, nkeys,
                   seq_len, q_tile):
    first = (pl.program_id(0) == 0) & (pl.program_id(1) == 0)

    @pl.when(first)
    def _():
        _fill_bias(bias_ref, half, nkeys, 1.0)

    @pl.when(pl.program_id(1) == 0)
    def _():
        c = lax.broadcasted_iota(jnp.int32, (B_KV_WIDTH, GROUP_WIDTH), 0)
        j = lax.broadcasted_iota(jnp.int32, (B_KV_WIDTH, GROUP_WIDTH), 1)
        src = (j // (GROUP_WIDTH // B_KV_HEADS)) * HEAD_DIM + j % HEAD_DIM
        rep = jnp.where(c == src, 1.0, 0.0).astype(jnp.bfloat16)
        chunk = 512

        def body(i, carry):
            rows = pl.ds(pl.multiple_of(i * chunk, chunk), chunk)
            k4_ref[rows, :] = _dot(k_ref[rows, :], rep).astype(jnp.bfloat16)
            v4_ref[rows, :] = _dot(v_ref[rows, :], rep).astype(jnp.bfloat16)
            return carry

        lax.fori_loop(0, seq_len // chunk, body, 0)

    base = pl.program_id(1) * q_tile

    def body(jb, carry):
        row0 = pl.multiple_of(jb * Q_BLOCK, Q_BLOCK)
        ks, var = _key_window(base + row0, seq_len, half, nkeys)
        for g in range(N_HEADS // HEADS_PER_PASS):
            cols = slice(g * PASS_WIDTH, (g + 1) * PASS_WIDTH)
            heads = pl.ds(g * HEADS_PER_PASS, HEADS_PER_PASS)
            sinks = [sink_ref[g * HEADS_PER_PASS + h] for h in range(HEADS_PER_PASS)]
            o, _, den = _attend_pass(q_ref[pl.ds(row0, Q_BLOCK), cols], k4_ref[pl.ds(ks, nkeys), cols],
                                     v4_ref[pl.ds(ks, nkeys), cols], bias_ref[var, heads], sinks)
            o_ref[pl.ds(row0, Q_BLOCK), cols] = o * (1.0 / den)
        return carry

    lax.fori_loop(0, q_tile // Q_BLOCK, body, 0,
                  unroll=PASSES_PER_BODY // (N_HEADS // HEADS_PER_PASS))


def _attn_b(q, k, v, sink, batch, seq):
    half = B_HALF_WINDOW
    nkeys = Q_BLOCK + 2 * half
    q_tile = 1024
    view = lambda a: a.reshape(batch, seq, a.shape[-1])
    kern = functools.partial(_attn_b_kernel, half=half, nkeys=nkeys, seq_len=seq, q_tile=q_tile)
    o = pl.pallas_call(
        kern,
        out_shape=jax.ShapeDtypeStruct((batch, seq, GROUP_WIDTH), jnp.float32),
        grid=(batch, seq // q_tile),
        in_specs=[pl.BlockSpec(memory_space=pltpu.SMEM),
                  pl.BlockSpec((None, q_tile, GROUP_WIDTH), lambda b, i: (b, i, 0)),
                  pl.BlockSpec((None, seq, B_KV_WIDTH), lambda b, i: (b, 0, 0)),
                  pl.BlockSpec((None, seq, B_KV_WIDTH), lambda b, i: (b, 0, 0))],
        out_specs=pl.BlockSpec((None, q_tile, GROUP_WIDTH), lambda b, i: (b, i, 0)),
        scratch_shapes=[pltpu.VMEM((3, N_HEADS, Q_BLOCK, nkeys), jnp.float32),
                        pltpu.VMEM((seq, GROUP_WIDTH), jnp.bfloat16),
                        pltpu.VMEM((seq, GROUP_WIDTH), jnp.bfloat16)],
        compiler_params=pltpu.CompilerParams(dimension_semantics=("arbitrary", "arbitrary"),
                                             vmem_limit_bytes=VMEM_LIMIT),
        name="attn_b",
    )(sink, view(q), view(k), view(v))
    return o.reshape(batch * seq, GROUP_WIDTH)


def _mix_kernel(oa_ref, ob_ref, x_ref, ga_ref, gb_ref, wout_ref, gffn_ref, wr_ref, br_ref,
                x1_ref, h2_ref, ri_ref, rc_ref, cnt_ref):
    tm = x_ref.shape[0]
    na = _rms(oa_ref[...], ga_ref[...]).astype(jnp.bfloat16)
    nb = _rms(ob_ref[...], gb_ref[...]).astype(jnp.bfloat16)
    x1 = (x_ref[...] + _dot(na, wout_ref[0:GROUP_WIDTH, :])
          + _dot(nb, wout_ref[GROUP_WIDTH:2 * GROUP_WIDTH, :]))
    x1_ref[...] = x1
    h2 = _rms(x1, gffn_ref[...])
    h2_ref[...] = h2.astype(jnp.bfloat16)

    h_hi = h2.astype(jnp.bfloat16)
    h_lo = (h2 - h_hi.astype(jnp.float32)).astype(jnp.bfloat16)
    wr = wr_ref[...]
    w_hi = wr.astype(jnp.bfloat16)
    w_lo = (wr - w_hi.astype(jnp.float32)).astype(jnp.bfloat16)
    w_cat = jnp.concatenate([w_hi, w_lo], axis=1)
    lg2 = _dot(h_hi, w_cat) + _dot(h_lo, w_cat)
    lg = lg2[:, 0:ROUTER_ROWS] + lg2[:, ROUTER_ROWS:2 * ROUTER_ROWS]
    lgt = lg.T + br_ref[...]

    rows = lax.broadcasted_iota(jnp.int32, (EXPERTS_PER_GROUP, tm), 0)
    big = jnp.int32(1 << 20)
    glog = jnp.where(rows < N_GROUPS, lgt[0:EXPERTS_PER_GROUP], -jnp.inf)
    gmax = jnp.max(glog, axis=0, keepdims=True)
    gsel = jnp.min(jnp.where(glog == gmax, rows, big), axis=0, keepdims=True)
    gw = 1.0 / jnp.sum(jnp.exp(glog - gmax), axis=0, keepdims=True)
    esel = jnp.zeros((EXPERTS_PER_GROUP, tm), jnp.float32)
    for grp in range(N_GROUPS):
        lo = EXPERTS_PER_GROUP * (grp + 1)
        esel = jnp.where(gsel == grp, lgt[lo:lo + EXPERTS_PER_GROUP], esel)
    v0 = jnp.max(esel, axis=0, keepdims=True)
    i0 = jnp.min(jnp.where(esel == v0, rows, big), axis=0, keepdims=True)
    rest = jnp.where(rows == i0, -jnp.inf, esel)
    v1 = jnp.max(rest, axis=0, keepdims=True)
    i1 = jnp.min(jnp.where(rest == v1, rows, big), axis=0, keepdims=True)
    e = jnp.exp(v1 - v0)
    w0 = (1.0 / (1.0 + e)) * gw
    w1 = (e / (1.0 + e)) * gw
    eid0 = gsel * EXPERTS_PER_GROUP + i0
    eid1 = gsel * EXPERTS_PER_GROUP + i1
    ri_ref[...] = jnp.where(rows == 0, eid0, jnp.where(rows == 1, eid1, 0))
    rows_t = lax.broadcasted_iota(jnp.int32, (ROUTER_ROWS, tm), 0)
    slab = jnp.where(rows_t == 0, w0, jnp.where(rows_t == 1, w1, 0.0))
    slab = jnp.where(rows_t == 2, eid0.astype(jnp.float32),
                     jnp.where(rows_t == 3, eid1.astype(jnp.float32), slab))
    rc_ref[...] = slab.T[:, 0:EXPERTS_PER_GROUP]
    ids = lax.broadcasted_iota(jnp.int32, (N_EXPERTS, tm), 0)
    member = jnp.where((ids == eid0) | (ids == eid1), 1.0, 0.0)
    cnt_ref[...] = jnp.sum(member, axis=1, keepdims=True).astype(jnp.int32)


def _mix(oa, ob, x2d, g_a, g_b, w_out_bf, g_ffn, wr, br):
    t = x2d.shape[0]
    tm = TOKEN_TILE
    row = lambda w: pl.BlockSpec((tm, w), lambda i: (i, 0))
    full = lambda a: pl.BlockSpec(a.shape, lambda i: (0, 0))
    return pl.pallas_call(
        _mix_kernel,
        out_shape=(jax.ShapeDtypeStruct((t, D_MODEL), jnp.float32),
                   jax.ShapeDtypeStruct((t, D_MODEL), jnp.bfloat16),
                   jax.ShapeDtypeStruct((EXPERTS_PER_GROUP, t), jnp.int32),
                   jax.ShapeDtypeStruct((t, EXPERTS_PER_GROUP), jnp.float32),
                   jax.ShapeDtypeStruct((t // tm, N_EXPERTS, 1), jnp.int32)),
        grid=(t // tm,),
        in_specs=[row(GROUP_WIDTH), row(GROUP_WIDTH), row(D_MODEL),
                  full(g_a), full(g_b), full(w_out_bf), full(g_ffn), full(wr), full(br)],
        out_specs=(row(D_MODEL), row(D_MODEL),
                   pl.BlockSpec((EXPERTS_PER_GROUP, tm), lambda i: (0, i)),
                   row(EXPERTS_PER_GROUP),
                   pl.BlockSpec((None, N_EXPERTS, 1), lambda i: (i, 0, 0))),
        compiler_params=pltpu.CompilerParams(dimension_semantics=("arbitrary",),
                                             vmem_limit_bytes=VMEM_LIMIT),
        name="mix_router",
    )(oa, ob, x2d, g_a, g_b, w_out_bf, g_ffn, wr, br)


RUN_ALIGN = 16
LOCAL_ROWS = 2 * TOKEN_TILE + N_EXPERTS * RUN_ALIGN
MAX_CHUNKS = LOCAL_ROWS // RUN_ALIGN
SORT_BLOCK = 256
FINAL_BLOCK = 256


def _local_slots(member, lo_vec, prior, e_sel, axis):
    ids = lax.broadcasted_iota(jnp.int32, member.shape, axis)
    slot = jnp.where(ids == e_sel, lo_vec + prior, 0.0)
    return jnp.sum(slot, axis=axis, keepdims=True).astype(jnp.int32)


def _chunk_copy(dst_ref, tile, c, local, glob, sem, to_global):
    g = pl.multiple_of(dst_ref[tile * MAX_CHUNKS + c], RUN_ALIGN)
    l = pl.multiple_of(c * RUN_ALIGN, RUN_ALIGN)
    src, dst = local.at[pl.ds(l, RUN_ALIGN), :], glob.at[pl.ds(g, RUN_ALIGN), :]
    if not to_global:
        src, dst = dst, src
    return pltpu.make_async_copy(src, dst, sem)


def _start_chunks(nchunk_ref, dst_ref, tile, local, glob, sem, to_global):
    def body(c, carry):
        _chunk_copy(dst_ref, tile, c, local, glob, sem, to_global).start()
        return carry
    lax.fori_loop(0, nchunk_ref[tile], body, 0)


def _wait_chunks(nchunk_ref, dst_ref, tile, local, glob, sem, to_global):
    def body(c, carry):
        _chunk_copy(dst_ref, tile, c, local, glob, sem, to_global).wait()
        return carry
    lax.fori_loop(0, nchunk_ref[tile], body, 0)


def _lo_vector(lo_ref, tile, shape, axis):
    ids = lax.broadcasted_iota(jnp.int32, shape, axis)
    vec = jnp.zeros(shape, jnp.float32)
    for e in range(N_EXPERTS):
        vec = jnp.where(ids == e, lo_ref[tile * N_EXPERTS + e].astype(jnp.float32), vec)
    return vec


def _dispatch_kernel(lo_ref, nchunk_ref, dst_ref, pad_tile_ref, nused_ref, ri_ref, h_ref, xs_hbm,
                     upper, sbuf, zbuf, sem, zsem, *, n_tiles):
    i = pl.program_id(0)
    n_steps = pl.num_programs(0)
    slot = i % 2

    def zero_tile(tile):
        rows = pl.ds(pl.multiple_of(tile * ROW_TILE, ROW_TILE), ROW_TILE)
        return pltpu.make_async_copy(zbuf, xs_hbm.at[rows, :], zsem)

    @pl.when(i == 0)
    def _():
        r = lax.broadcasted_iota(jnp.int32, (TOKEN_TILE, TOKEN_TILE), 0)
        c = lax.broadcasted_iota(jnp.int32, (TOKEN_TILE, TOKEN_TILE), 1)
        upper[...] = jnp.where(r < c, 1.0, 0.0).astype(jnp.bfloat16)
        zbuf[...] = jnp.zeros_like(zbuf)
        for e in range(N_EXPERTS):
            @pl.when(pad_tile_ref[e] >= 0)
            def _():
                zero_tile(pad_tile_ref[e]).start()
        lax.fori_loop(nused_ref[0], n_tiles, lambda t, c: (zero_tile(t).start(), c)[1], 0)
        for e in range(N_EXPERTS):
            @pl.when(pad_tile_ref[e] >= 0)
            def _():
                zero_tile(pad_tile_ref[e]).wait()
        lax.fori_loop(nused_ref[0], n_tiles, lambda t, c: (zero_tile(t).wait(), c)[1], 0)

    e0 = ri_ref[0:1, :]
    e1 = ri_ref[1:2, :]
    ids = lax.broadcasted_iota(jnp.int32, (N_EXPERTS, TOKEN_TILE), 0)
    member = jnp.where((ids == e0) | (ids == e1), 1.0, 0.0)
    prior = _dot(member.astype(jnp.bfloat16), upper[...])
    lo_vec = _lo_vector(lo_ref, i, (N_EXPERTS, 1), 0)
    d0 = _local_slots(member, lo_vec, prior, e0, 0)
    d1 = _local_slots(member, lo_vec, prior, e1, 0)
    local = sbuf.at[slot]
    used_rows = nchunk_ref[i] * RUN_ALIGN
    h = h_ref[...]
    for blk in range(LOCAL_ROWS // SORT_BLOCK):
        @pl.when(blk * SORT_BLOCK < used_rows)
        def _():
            rows = blk * SORT_BLOCK + lax.broadcasted_iota(jnp.int32, (SORT_BLOCK, TOKEN_TILE), 0)
            select = jnp.where((rows == d0) | (rows == d1), 1.0, 0.0).astype(jnp.bfloat16)
            local[blk * SORT_BLOCK:(blk + 1) * SORT_BLOCK, :] = _dot(select, h).astype(jnp.bfloat16)

    @pl.when(i > 0)
    def _():
        _wait_chunks(nchunk_ref, dst_ref, i - 1, sbuf.at[1 - slot], xs_hbm, sem, True)

    _start_chunks(nchunk_ref, dst_ref, i, local, xs_hbm, sem, True)

    @pl.when(i == n_steps - 1)
    def _():
        _wait_chunks(nchunk_ref, dst_ref, i, local, xs_hbm, sem, True)


def _dispatch(lo, nchunk, dst, pad_tile, nused, route_i, h2, p_rows):
    n_tok = h2.shape[0]
    grid_spec = pltpu.PrefetchScalarGridSpec(
        num_scalar_prefetch=5,
        grid=(n_tok // TOKEN_TILE,),
        in_specs=[pl.BlockSpec((EXPERTS_PER_GROUP, TOKEN_TILE), lambda i, *_: (0, i)),
                  pl.BlockSpec((TOKEN_TILE, D_MODEL), lambda i, *_: (i, 0))],
        out_specs=pl.BlockSpec(memory_space=pl.ANY),
        scratch_shapes=[pltpu.VMEM((TOKEN_TILE, TOKEN_TILE), jnp.bfloat16),
                        pltpu.VMEM((2, LOCAL_ROWS, D_MODEL), jnp.bfloat16),
                        pltpu.VMEM((ROW_TILE, D_MODEL), jnp.bfloat16),
                        pltpu.SemaphoreType.DMA, pltpu.SemaphoreType.DMA],
    )
    return pl.pallas_call(
        functools.partial(_dispatch_kernel, n_tiles=p_rows // ROW_TILE),
        out_shape=jax.ShapeDtypeStruct((p_rows, D_MODEL), jnp.bfloat16),
        grid_spec=grid_spec,
        compiler_params=pltpu.CompilerParams(dimension_semantics=("arbitrary",),
                                             vmem_limit_bytes=VMEM_LIMIT),
        name="dispatch",
    )(lo, nchunk, dst, pad_tile, nused, route_i, h2)


def _moe_kernel(texp_ref, first_ref, nused_ref, slot_ref, next_ref, xs_ref, wg_hbm, wu_hbm, wd_hbm,
                y_ref, wg_f32, wu_f32, wd_f32, wg_bf, wu_bf, wd_bf, sems):
    j = pl.program_id(0)

    def fetch(expert, slot):
        return [pltpu.make_async_copy(src.at[expert], dst.at[slot], sems.at[slot, k])
                for k, (src, dst) in enumerate(((wg_hbm, wg_f32), (wu_hbm, wu_f32), (wd_hbm, wd_f32)))]

    @pl.when(j == 0)
    def _():
        for cp in fetch(texp_ref[0], 0):
            cp.start()

    @pl.when(j < nused_ref[0])
    def _():
        @pl.when(first_ref[j] == 1)
        def _():
            for s in range(2):
                @pl.when(slot_ref[j] == s)
                def _():
                    for cp in fetch(texp_ref[j], s):
                        cp.wait()
                    wg_bf[...] = wg_f32[s].astype(jnp.bfloat16)
                    wu_bf[...] = wu_f32[s].astype(jnp.bfloat16)
                    wd_bf[...] = wd_f32[s].astype(jnp.bfloat16)

                    @pl.when(next_ref[j] >= 0)
                    def _():
                        for cp in fetch(next_ref[j], 1 - s):
                            cp.start()

        for blk in range(ROW_TILE // MOE_BLOCK):
            rows = slice(blk * MOE_BLOCK, (blk + 1) * MOE_BLOCK)
            x = xs_ref[rows, :]
            a = _dot(x, wg_bf[...])
            u = _dot(x, wu_bf[...])
            hid = (a * (1.0 / (1.0 + jnp.exp(-a)))) * u
            y_ref[rows, :] = _dot(hid.astype(jnp.bfloat16), wd_bf[...]).astype(jnp.bfloat16)

    @pl.when(j >= nused_ref[0])
    def _():
        y_ref[...] = jnp.zeros_like(y_ref)


def _moe(texp, first, nused, slot, nxt, xs, w_gate, w_up, w_down):
    p_rows = xs.shape[0]
    n_tiles = p_rows // ROW_TILE
    used = lambda j, nu: jnp.minimum(j, nu[0] - 1)
    grid_spec = pltpu.PrefetchScalarGridSpec(
        num_scalar_prefetch=5,
        grid=(n_tiles,),
        in_specs=[
            pl.BlockSpec((ROW_TILE, D_MODEL), lambda j, te, fi, nu, sl, nx: (used(j, nu), 0)),
            pl.BlockSpec(memory_space=pl.ANY),
            pl.BlockSpec(memory_space=pl.ANY),
            pl.BlockSpec(memory_space=pl.ANY),
        ],
        out_specs=pl.BlockSpec((ROW_TILE, D_MODEL), lambda j, te, fi, nu, sl, nx: (j, 0)),
        scratch_shapes=[pltpu.VMEM((2, D_MODEL, D_EXPERT), jnp.float32),
                        pltpu.VMEM((2, D_MODEL, D_EXPERT), jnp.float32),
                        pltpu.VMEM((2, D_EXPERT, D_MODEL), jnp.float32),
                        pltpu.VMEM((D_MODEL, D_EXPERT), jnp.bfloat16),
                        pltpu.VMEM((D_MODEL, D_EXPERT), jnp.bfloat16),
                        pltpu.VMEM((D_EXPERT, D_MODEL), jnp.bfloat16),
                        pltpu.SemaphoreType.DMA((2, 3))],
    )
    return pl.pallas_call(
        _moe_kernel,
        out_shape=jax.ShapeDtypeStruct((p_rows, D_MODEL), jnp.bfloat16),
        grid_spec=grid_spec,
        compiler_params=pltpu.CompilerParams(dimension_semantics=("arbitrary",),
                                             vmem_limit_bytes=VMEM_LIMIT),
        name="moe_experts",
    )(texp, first, nused, slot, nxt, xs, w_gate, w_up, w_down)


def _final_kernel(lo_ref, nchunk_ref, dst_ref, y_hbm, x1_ref, rc_ref, p_ref, gple_ref, wg_ref, wp_ref,
                  gfin_ref, out_ref, lower, ybuf, sems):
    i = pl.program_id(0)
    n_steps = pl.num_programs(0)
    slot = i % 2

    @pl.when(i == 0)
    def _():
        r = lax.broadcasted_iota(jnp.int32, (TOKEN_TILE, TOKEN_TILE), 0)
        c = lax.broadcasted_iota(jnp.int32, (TOKEN_TILE, TOKEN_TILE), 1)
        lower[...] = jnp.where(c < r, 1.0, 0.0).astype(jnp.bfloat16)
        ybuf[...] = jnp.zeros_like(ybuf)
        _start_chunks(nchunk_ref, dst_ref, 0, ybuf.at[0], y_hbm, sems.at[0], False)

    @pl.when(i + 1 < n_steps)
    def _():
        _start_chunks(nchunk_ref, dst_ref, i + 1, ybuf.at[1 - slot], y_hbm, sems.at[1 - slot], False)

    rc = rc_ref[...]
    w0, w1 = rc[:, 0:1], rc[:, 1:2]
    e0, e1 = rc[:, 2:3].astype(jnp.int32), rc[:, 3:4].astype(jnp.int32)
    ids = lax.broadcasted_iota(jnp.int32, (TOKEN_TILE, ROUTER_ROWS), 1)
    member = jnp.where((ids == e0) | (ids == e1), 1.0, 0.0)
    prior = _dot(lower[...], member.astype(jnp.bfloat16))
    lo_vec = _lo_vector(lo_ref, i, (1, ROUTER_ROWS), 1)
    d0 = _local_slots(member, lo_vec, prior, e0, 1)
    d1 = _local_slots(member, lo_vec, prior, e1, 1)

    local = ybuf.at[slot]
    _wait_chunks(nchunk_ref, dst_ref, i, local, y_hbm, sems.at[slot], False)
    for blk in range(TOKEN_TILE // FINAL_BLOCK):
        rows = slice(blk * FINAL_BLOCK, (blk + 1) * FINAL_BLOCK)
        cols = lax.broadcasted_iota(jnp.int32, (FINAL_BLOCK, LOCAL_ROWS), 1)
        pick = jnp.where(cols == d0[rows], w0[rows], jnp.where(cols == d1[rows], w1[rows], 0.0))
        x2 = x1_ref[rows, :] + _dot(pick.astype(jnp.bfloat16), local[...])
        z = _dot(_rms(x2, gple_ref[...]).astype(jnp.bfloat16), wg_ref[...])
        gate = 1.0 / (1.0 + jnp.exp(-z))
        x3 = x2 + _dot(p_ref[rows, :].astype(jnp.bfloat16), wp_ref[...]) * gate
        out_ref[rows, :] = _rms(x3, gfin_ref[...])


def _final(lo, nchunk, dst, y, x1, rc, p2d, g_ple, w_gate_bf, w_proj_bf, g_final):
    t = x1.shape[0]
    tm = TOKEN_TILE
    row = lambda w: pl.BlockSpec((tm, w), lambda i, *_: (i, 0))
    full = lambda a: pl.BlockSpec(a.shape, lambda i, *_: (0, 0))
    grid_spec = pltpu.PrefetchScalarGridSpec(
        num_scalar_prefetch=3,
        grid=(t // tm,),
        in_specs=[pl.BlockSpec(memory_space=pl.ANY), row(D_MODEL), row(EXPERTS_PER_GROUP),
                  row(PLE_DIM), full(g_ple), full(w_gate_bf), full(w_proj_bf), full(g_final)],
        out_specs=row(D_MODEL),
        scratch_shapes=[pltpu.VMEM((TOKEN_TILE, TOKEN_TILE), jnp.bfloat16),
                        pltpu.VMEM((2, LOCAL_ROWS, D_MODEL), jnp.bfloat16),
                        pltpu.SemaphoreType.DMA((2,))],
    )
    return pl.pallas_call(
        _final_kernel,
        out_shape=jax.ShapeDtypeStruct((t, D_MODEL), jnp.float32),
        grid_spec=grid_spec,
        compiler_params=pltpu.CompilerParams(dimension_semantics=("arbitrary",),
                                             vmem_limit_bytes=VMEM_LIMIT),
        name="combine_ple_final",
    )(lo, nchunk, dst, y, x1, rc, p2d, g_ple, w_gate_bf, w_proj_bf, g_final)


def _dispatch_tables(counts):
    n_tt = counts.shape[0]
    p_rows = 2 * n_tt * TOKEN_TILE + n_tt * N_EXPERTS * (RUN_ALIGN - 1) + N_EXPERTS * ROW_TILE
    p_rows = -(-p_rows // ROW_TILE) * ROW_TILE
    n_tiles = p_rows // ROW_TILE
    n8 = (counts + RUN_ALIGN - 1) // RUN_ALIGN * RUN_ALIGN
    lo = jnp.cumsum(n8, axis=1) - n8
    total = jnp.sum(n8, axis=0)
    tiles_per = (total + ROW_TILE - 1) // ROW_TILE
    tile_end = jnp.cumsum(tiles_per)
    tile_start = tile_end - tiles_per
    g = tile_start[None, :] * ROW_TILE + (jnp.cumsum(n8, axis=0) - n8)
    n_used = tile_end[-1]
    tile_ids = jnp.arange(n_tiles, dtype=jnp.int32)
    clamped = jnp.minimum(tile_ids, n_used - 1)
    texp = jnp.sum((tile_end[None, :] <= clamped[:, None]).astype(jnp.int32), axis=1)
    experts = jnp.arange(N_EXPERTS, dtype=jnp.int32)
    of_tile = texp[:, None] == experts[None, :]
    per_tile = lambda v: jnp.sum(jnp.where(of_tile, v[None, :], 0), axis=1)
    first = ((tile_ids == per_tile(tile_start)) & (tile_ids < n_used)).astype(jnp.int32)
    pad_tile = jnp.where(total > 0, tile_end - 1, -1).astype(jnp.int32)
    has_rows = total > 0
    later = has_rows[None, :] & (experts[None, :] > experts[:, None])
    next_expert = jnp.min(jnp.where(later, experts[None, :], N_EXPERTS), axis=1)
    next_expert = jnp.where(next_expert < N_EXPERTS, next_expert, -1)
    slot = (jnp.cumsum(has_rows.astype(jnp.int32)) - 1) % 2
    chunk_row = jnp.arange(MAX_CHUNKS, dtype=jnp.int32)[None, :, None] * RUN_ALIGN
    in_run = (chunk_row >= lo[:, None, :]) & (chunk_row < (lo + n8)[:, None, :])
    dst = jnp.sum(jnp.where(in_run, g[:, None, :] + chunk_row - lo[:, None, :], 0), axis=2)
    nchunk = jnp.sum(n8, axis=1) // RUN_ALIGN
    i32 = lambda a: a.reshape(-1).astype(jnp.int32)
    return (texp, first, n_used.reshape(1).astype(jnp.int32), i32(per_tile(slot)),
            i32(per_tile(next_expert)), pad_tile, i32(lo), i32(nchunk), i32(dst), p_rows)


def kernel(x, p, g_mix, w_in, sink, g_grp_a, g_grp_b, w_out, g_ffn, w_router_group, b_router_group,
           w_router_expert, b_router_expert, w_expert_gate, w_expert_up, w_expert_down, g_ple,
           w_ple_gate, w_ple_proj, g_final):
    batch, seq, _ = x.shape
    n_tok = batch * seq
    depth = w_in.shape[0]
    assert depth == 1, "the final RMSNorm is fused into the last layer's epilogue"
    bf = jnp.bfloat16
    xc = x.reshape(n_tok, D_MODEL)
    for i in range(depth):
        *qkv_a, qb, kb, vb = _proj(xc, g_mix[i][None, :], w_in[i].astype(bf), batch, seq)
        oa = _attn_a(qkv_a, batch, seq)
        ob = _attn_b(qb, kb, vb, sink[i], batch, seq)

        wr = jnp.zeros((D_MODEL, ROUTER_ROWS), jnp.float32)
        wr = wr.at[:, 0:N_GROUPS].set(w_router_group[i])
        wr = wr.at[:, EXPERTS_PER_GROUP:EXPERTS_PER_GROUP + N_EXPERTS].set(w_router_expert[i])
        br = jnp.zeros((ROUTER_ROWS, 1), jnp.float32)
        br = br.at[0:N_GROUPS, 0].set(b_router_group[i])
        br = br.at[EXPERTS_PER_GROUP:EXPERTS_PER_GROUP + N_EXPERTS, 0].set(b_router_expert[i])

        x1, h2, route_i, route_c, counts = _mix(
            oa, ob, xc, g_grp_a[i][None, :], g_grp_b[i][None, :], w_out[i].astype(bf), g_ffn[i][None, :], wr, br)

        texp, first, nused, slot, nxt, pad_tile, lo, nchunk, dst, p_rows = _dispatch_tables(counts[:, :, 0])
        xs = _dispatch(lo, nchunk, dst, pad_tile, nused, route_i, h2, p_rows)
        y = _moe(texp, first, nused, slot, nxt, xs, w_expert_gate[i], w_expert_up[i], w_expert_down[i])
        xc = _final(lo, nchunk, dst, y, x1, route_c, p[i].reshape(n_tok, PLE_DIM), g_ple[i][None, :],
                    w_ple_gate[i].astype(bf), w_ple_proj[i].astype(bf), g_final[None, :])
    return xc.reshape(batch, seq, D_MODEL)
```

```python
import functools

import jax
import jax.numpy as jnp
from jax import lax
from jax.experimental import pallas as pl
from jax.experimental.pallas import tpu as pltpu

D_MODEL = 1024
HEAD_DIM = 64
GROUP_WIDTH = 512
N_HEADS = 8
B_KV_HEADS = 2
B_KV_WIDTH = B_KV_HEADS * HEAD_DIM
A_CONFIGS = ((128, 1), (512, 4), (2048, 16))
B_HALF_WINDOW = 128
N_GROUPS = 4
EXPERTS_PER_GROUP = 8
N_EXPERTS = N_GROUPS * EXPERTS_PER_GROUP
D_EXPERT = 512
PLE_DIM = 256
EPS = 1e-6
NEG = -1e30
LOG2E = 1.4426950408889634

LANES = 128
Q_BLOCK = 128
HEADS_PER_PASS = 2
PASS_WIDTH = HEADS_PER_PASS * HEAD_DIM
ROW_TILE = 512
TOKEN_TILE = 512
ROUTER_ROWS = 128
VMEM_LIMIT = 48 * 1024 * 1024
ATTN_VMEM_LIMIT = 56 * 1024 * 1024


def _rms(x, g):
    r = lax.rsqrt(jnp.mean(x * x, axis=-1, keepdims=True) + EPS)
    return (x * r) * g


def _dot(a, b):
    return jnp.dot(a, b, preferred_element_type=jnp.float32)


PROJ_TILE = 512


def _proj_kernel(x_ref, g_ref, w_ref, *refs):
    a_refs, (qb_ref, kb_ref, vb_ref, res, *stage_refs) = refs[:9], refs[9:]
    h = _rms(x_ref[...], g_ref[...]).astype(jnp.bfloat16)
    scale = HEAD_DIM ** -0.5 * LOG2E
    W = GROUP_WIDTH
    tiles = GROUP_WIDTH // PASS_WIDTH
    qkv_a = (_dot(h, w_ref[:, 0:W]) * scale, _dot(h, w_ref[:, W:2 * W]), _dot(h, w_ref[:, 2 * W:3 * W]))
    for part in range(3):
        for pair in range(tiles):
            res[part * tiles + pair, 0] = qkv_a[part][:, pair * PASS_WIDTH:(pair + 1) * PASS_WIDTH]
    qb_ref[...] = (_dot(h, w_ref[:, 3 * W:4 * W]) * scale).astype(jnp.bfloat16)
    kb_ref[...] = _dot(h, w_ref[:, 4 * W:4 * W + B_KV_WIDTH]).astype(jnp.bfloat16)
    vb_ref[...] = _dot(h, w_ref[:, 4 * W + B_KV_WIDTH:4 * W + 2 * B_KV_WIDTH]).astype(jnp.bfloat16)
    stages = (res,) + tuple(stage_refs)
    prev_dil = 1
    for c, (_, dil) in enumerate(A_CONFIGS):
        ratio = dil // prev_dil
        rows = PROJ_TILE // dil
        for idx in range(3 * tiles):
            part, pair = idx // tiles, idx % tiles
            out = a_refs[3 * c + part]
            for rho in range(dil):
                if c == 0:
                    sub = res[idx, 0]
                else:
                    sub = stages[c - 1].at[idx, rho % prev_dil][pl.ds(rho // prev_dil, rows, stride=ratio), :]
                    if c < len(A_CONFIGS) - 1:
                        stages[c][idx, rho] = sub
                out[pair, rho] = sub.astype(jnp.bfloat16)
        prev_dil = dil


def _proj(x2d, g_mix, w_in_bf, batch, seq):
    t = x2d.shape[0]
    tm = PROJ_TILE
    tiles_per_seq = seq // tm
    in_width = w_in_bf.shape[1]
    n_pairs = N_HEADS // HEADS_PER_PASS
    row = lambda w: pl.BlockSpec((tm, w), lambda i: (i, 0))
    a_shapes, a_specs = [], []
    for _, dil in A_CONFIGS:
        for _ in range(3):
            a_shapes.append(jax.ShapeDtypeStruct((batch, n_pairs, dil, seq // dil, PASS_WIDTH),
                                                 jnp.bfloat16))
            a_specs.append(pl.BlockSpec((None, n_pairs, dil, tm // dil, PASS_WIDTH),
                                        lambda i: (i // tiles_per_seq, 0, 0, i % tiles_per_seq, 0)))
    wide = jax.ShapeDtypeStruct((t, GROUP_WIDTH), jnp.bfloat16)
    narrow = jax.ShapeDtypeStruct((t, B_KV_WIDTH), jnp.bfloat16)
    return pl.pallas_call(
        _proj_kernel,
        out_shape=tuple(a_shapes) + (wide, narrow, narrow),
        grid=(t // tm,),
        in_specs=[row(D_MODEL),
                  pl.BlockSpec((1, D_MODEL), lambda i: (0, 0)),
                  pl.BlockSpec((D_MODEL, in_width), lambda i: (0, 0))],
        out_specs=tuple(a_specs) + (row(GROUP_WIDTH), row(B_KV_WIDTH), row(B_KV_WIDTH)),
        scratch_shapes=[pltpu.VMEM((3 * GROUP_WIDTH // PASS_WIDTH, dil, tm // dil, PASS_WIDTH), jnp.float32)
                        for _, dil in A_CONFIGS[:-1]],
        compiler_params=pltpu.CompilerParams(dimension_semantics=("arbitrary",),
                                             vmem_limit_bytes=VMEM_LIMIT),
        name="proj",
    )(x2d, g_mix, w_in_bf)


def _fill_bias(bias_ref, half, nkeys, dist_scale):
    i = lax.broadcasted_iota(jnp.int32, (Q_BLOCK, nkeys), 0)
    j = lax.broadcasted_iota(jnp.int32, (Q_BLOCK, nkeys), 1)
    for var, rel0 in enumerate((0, -half, Q_BLOCK - nkeys)):
        d = jnp.abs(j + rel0 - i)
        dist = d.astype(jnp.float32) * dist_scale
        for h in range(N_HEADS):
            slope = 2.0 ** (-(h + 1))
            bias_ref[var, h] = jnp.where(d <= half, -(slope * dist) * LOG2E, NEG)


def _key_window(q0, seq_len, half, nkeys):
    lo = q0 - half
    ks = pl.multiple_of(jnp.clip(lo, 0, seq_len - nkeys), HEAD_DIM)
    var = jnp.where(lo < 0, 0, jnp.where(lo > seq_len - nkeys, 2, 1))
    return ks, var


def _fold_lane_tiles(a, op):
    out = a[..., 0:LANES]
    for t in range(1, a.shape[-1] // LANES):
        out = op(out, a[..., t * LANES:(t + 1) * LANES])
    return out


def _attend_pass(q, k, v, bias, sinks):
    nkeys = k.shape[0]
    lane_head = lax.broadcasted_iota(jnp.int32, (Q_BLOCK, PASS_WIDTH), 1) // HEAD_DIM
    zero = jnp.zeros_like(q)
    q_stack = jnp.concatenate(
        [jnp.where(lane_head == h, q, zero) for h in range(HEADS_PER_PASS)], axis=0)
    s = lax.dot_general(q_stack, k, (((1,), (1,)), ((), ())), preferred_element_type=jnp.float32)
    s = s.reshape(HEADS_PER_PASS, Q_BLOCK, nkeys) + bias
    m_tile = _fold_lane_tiles(s, jnp.maximum)
    if sinks is not None:
        tile_head = lax.broadcasted_iota(jnp.int32, m_tile.shape, 0)
        tile_lane = lax.broadcasted_iota(jnp.int32, m_tile.shape, 2)
        sk = jnp.zeros(m_tile.shape, jnp.float32)
        for h in range(HEADS_PER_PASS):
            sk = jnp.where(tile_head == h, sinks[h] * LOG2E, sk)
        m_tile = jnp.maximum(m_tile, sk)
    m = jnp.max(m_tile, axis=-1, keepdims=True)
    p = jnp.exp2(s - m)
    den_tile = _fold_lane_tiles(p, jnp.add)
    if sinks is not None:
        den_tile = den_tile + jnp.where(tile_lane == 0, jnp.exp2(sk - m), 0.0)
    den = jnp.sum(den_tile, axis=-1, keepdims=True)
    pv = _dot(p.reshape(HEADS_PER_PASS * Q_BLOCK, nkeys).astype(jnp.bfloat16), v)
    pv = pv.reshape(HEADS_PER_PASS, Q_BLOCK, PASS_WIDTH)
    shape = (Q_BLOCK, PASS_WIDTH)
    o, m_lanes, den_lanes = (jnp.zeros(shape, jnp.float32) for _ in range(3))
    for h in range(HEADS_PER_PASS):
        o = jnp.where(lane_head == h, pv[h], o)
        m_lanes = jnp.where(lane_head == h, m[h], m_lanes)
        den_lanes = jnp.where(lane_head == h, den[h], den_lanes)
    return o, m_lanes, den_lanes


A_HALF = 64
A_KEYS = Q_BLOCK + 2 * A_HALF
PASSES_PER_BODY = 16
MERGE_BLOCK = 256


def _attn_a_kernel(*refs, seq):
    n_cfg = len(A_CONFIGS)
    qkv = refs[:3 * n_cfg]
    o_ref, bias_ref = refs[3 * n_cfg], refs[3 * n_cfg + 1]
    scratch = refs[3 * n_cfg + 2:]
    pair = pl.program_id(1)

    @pl.when((pl.program_id(0) == 0) & (pair == 0))
    def _():
        for c, (_, dil) in enumerate(A_CONFIGS):
            _fill_bias(bias_ref.at[c], A_HALF, A_KEYS, float(dil))

    for c, (window, dil) in enumerate(A_CONFIGS):
        assert window // (2 * dil) == A_HALF
        q_ref, k_ref, v_ref = qkv[3 * c:3 * c + 3]
        o_sc, m_sc, d_sc = scratch[3 * c:3 * c + 3]
        sub_len = seq // dil
        blocks = sub_len // Q_BLOCK

        def body(idx, carry, q_ref=q_ref, k_ref=k_ref, v_ref=v_ref, o_sc=o_sc, m_sc=m_sc, d_sc=d_sc, c=c,
                 dil=dil, sub_len=sub_len, blocks=blocks):
            rho = idx // blocks
            q0 = pl.multiple_of((idx % blocks) * Q_BLOCK, Q_BLOCK)
            ks, var = _key_window(q0, sub_len, A_HALF, A_KEYS)
            bias = bias_ref[c, var, pl.ds(pair * HEADS_PER_PASS, HEADS_PER_PASS)]
            o, m, den = _attend_pass(q_ref[rho, pl.ds(q0, Q_BLOCK), :], k_ref[rho, pl.ds(ks, A_KEYS), :],
                                     v_ref[rho, pl.ds(ks, A_KEYS), :], bias, None)
            tokens = pl.ds(rho + dil * q0, Q_BLOCK, stride=dil) if dil > 1 else pl.ds(q0, Q_BLOCK)
            o_sc[tokens, :] = o
            m_sc[tokens, :] = m
            d_sc[tokens, :] = den
            return carry

        lax.fori_loop(0, dil * blocks, body, 0, unroll=PASSES_PER_BODY)

    n_cfg = len(A_CONFIGS)

    def merge(blk, carry):
        rows = pl.ds(pl.multiple_of(blk * MERGE_BLOCK, MERGE_BLOCK), MERGE_BLOCK)
        pvs = [scratch[3 * c][rows, :] for c in range(n_cfg)]
        ms = [scratch[3 * c + 1][rows, :] for c in range(n_cfg)]
        dens = [scratch[3 * c + 2][rows, :] for c in range(n_cfg)]
        mx = functools.reduce(jnp.maximum, ms)
        es = [jnp.exp2(m - mx) for m in ms]
        total = functools.reduce(jnp.add, [e * d for e, d in zip(es, dens)])
        o_ref[rows, :] = functools.reduce(jnp.add, [e * pv for e, pv in zip(es, pvs)]) * (1.0 / total)
        return carry

    lax.fori_loop(0, seq // MERGE_BLOCK, merge, 0, unroll=2)


def _attn_a(qkv, batch, seq):
    n_pairs = N_HEADS // HEADS_PER_PASS
    in_specs, scratch = [], [pltpu.VMEM((len(A_CONFIGS), 3, N_HEADS, Q_BLOCK, A_KEYS), jnp.float32)]
    for _, dil in A_CONFIGS:
        blk = (None, None, dil, seq // dil, PASS_WIDTH)
        in_specs += [pl.BlockSpec(blk, lambda b, p: (b, p, 0, 0, 0))] * 3
        scratch += [pltpu.VMEM((seq, PASS_WIDTH), jnp.float32)] * 3
    return pl.pallas_call(
        functools.partial(_attn_a_kernel, seq=seq),
        out_shape=jax.ShapeDtypeStruct((batch, seq, GROUP_WIDTH), jnp.float32),
        grid=(batch, n_pairs),
        in_specs=in_specs,
        out_specs=pl.BlockSpec((None, seq, PASS_WIDTH), lambda b, p: (b, 0, p)),
        scratch_shapes=scratch,
        compiler_params=pltpu.CompilerParams(dimension_semantics=("arbitrary", "arbitrary"),
                                             vmem_limit_bytes=ATTN_VMEM_LIMIT),
        name="attn_a",
    )(*qkv).reshape(batch * seq, GROUP_WIDTH)


def _attn_b_kernel(sink_ref, q_ref, k_ref, v_ref, o_ref, bias_ref, k4_ref, v4_ref, *, half, nkeys,
                   seq_len, q_tile):
    first = (pl.program_id(0) == 0) & (pl.program_id(1) == 0)

    @pl.when(first)
    def _():
        _fill_bias(bias_ref, half, nkeys, 1.0)

    @pl.when(pl.program_id(1) == 0)
    def _():
        c = lax.broadcasted_iota(jnp.int32, (B_KV_WIDTH, GROUP_WIDTH), 0)
        j = lax.broadcasted_iota(jnp.int32, (B_KV_WIDTH, GROUP_WIDTH), 1)
        src = (j // (GROUP_WIDTH // B_KV_HEADS)) * HEAD_DIM + j % HEAD_DIM
        rep = jnp.where(c == src, 1.0, 0.0).astype(jnp.bfloat16)
        chunk = 512

        def body(i, carry):
            rows = pl.ds(pl.multiple_of(i * chunk, chunk), chunk)
            k4_ref[rows, :] = _dot(k_ref[rows, :], rep).astype(jnp.bfloat16)
            v4_ref[rows, :] = _dot(v_ref[rows, :], rep).astype(jnp.bfloat16)
            return carry

        lax.fori_loop(0, seq_len // chunk, body, 0)

    base = pl.program_id(1) * q_tile

    def body(jb, carry):
        row0 = pl.multiple_of(jb * Q_BLOCK, Q_BLOCK)
        ks, var = _key_window(base + row0, seq_len, half, nkeys)
        for g in range(N_HEADS // HEADS_PER_PASS):
            cols = slice(g * PASS_WIDTH, (g + 1) * PASS_WIDTH)
            heads = pl.ds(g * HEADS_PER_PASS, HEADS_PER_PASS)
            sinks = [sink_ref[g * HEADS_PER_PASS + h] for h in range(HEADS_PER_PASS)]
            o, _, den = _attend_pass(q_ref[pl.ds(row0, Q_BLOCK), cols], k4_ref[pl.ds(ks, nkeys), cols],
                                     v4_ref[pl.ds(ks, nkeys), cols], bias_ref[var, heads], sinks)
            o_ref[pl.ds(row0, Q_BLOCK), cols] = o * (1.0 / den)
        return carry

    lax.fori_loop(0, q_tile // Q_BLOCK, body, 0,
                  unroll=PASSES_PER_BODY // (N_HEADS // HEADS_PER_PASS))


def _attn_b(q, k, v, sink, batch, seq):
    half = B_HALF_WINDOW
    nkeys = Q_BLOCK + 2 * half
    q_tile = 1024
    view = lambda a: a.reshape(batch, seq, a.shape[-1])
    kern = functools.partial(_attn_b_kernel, half=half, nkeys=nkeys, seq_len=seq, q_tile=q_tile)
    o = pl.pallas_call(
        kern,
        out_shape=jax.ShapeDtypeStruct((batch, seq, GROUP_WIDTH), jnp.float32),
        grid=(batch, seq // q_tile),
        in_specs=[pl.BlockSpec(memory_space=pltpu.SMEM),
                  pl.BlockSpec((None, q_tile, GROUP_WIDTH), lambda b, i: (b, i, 0)),
                  pl.BlockSpec((None, seq, B_KV_WIDTH), lambda b, i: (b, 0, 0)),
                  pl.BlockSpec((None, seq, B_KV_WIDTH), lambda b, i: (b, 0, 0))],
        out_specs=pl.BlockSpec((None, q_tile, GROUP_WIDTH), lambda b, i: (b, i, 0)),
        scratch_shapes=[pltpu.VMEM((3, N_HEADS, Q_BLOCK, nkeys), jnp.float32),
                        pltpu.VMEM((seq, GROUP_WIDTH), jnp.bfloat16),
                        pltpu.VMEM((seq, GROUP_WIDTH), jnp.bfloat16)],
        compiler_params=pltpu.CompilerParams(dimension_semantics=("arbitrary", "arbitrary"),
                                             vmem_limit_bytes=VMEM_LIMIT),
        name="attn_b",
    )(sink, view(q), view(k), view(v))
    return o.reshape(batch * seq, GROUP_WIDTH)


def _mix_kernel(oa_ref, ob_ref, x_ref, ga_ref, gb_ref, wout_ref, gffn_ref, wr_ref, br_ref,
                x1_ref, h2_ref, ri_ref, rc_ref, cnt_ref):
    tm = x_ref.shape[0]
    na = _rms(oa_ref[...], ga_ref[...]).astype(jnp.bfloat16)
    nb = _rms(ob_ref[...], gb_ref[...]).astype(jnp.bfloat16)
    x1 = (x_ref[...] + _dot(na, wout_ref[0:GROUP_WIDTH, :])
          + _dot(nb, wout_ref[GROUP_WIDTH:2 * GROUP_WIDTH, :]))
    x1_ref[...] = x1
    h2 = _rms(x1, gffn_ref[...])
    h2_ref[...] = h2.astype(jnp.bfloat16)

    h_hi = h2.astype(jnp.bfloat16)
    h_lo = (h2 - h_hi.astype(jnp.float32)).astype(jnp.bfloat16)
    wr = wr_ref[...]
    w_hi = wr.astype(jnp.bfloat16)
    w_lo = (wr - w_hi.astype(jnp.float32)).astype(jnp.bfloat16)
    w_cat = jnp.concatenate([w_hi, w_lo], axis=1)
    lg2 = _dot(h_hi, w_cat) + _dot(h_lo, w_cat)
    lg = lg2[:, 0:ROUTER_ROWS] + lg2[:, ROUTER_ROWS:2 * ROUTER_ROWS]
    lgt = lg.T + br_ref[...]

    rows = lax.broadcasted_iota(jnp.int32, (EXPERTS_PER_GROUP, tm), 0)
    big = jnp.int32(1 << 20)
    glog = jnp.where(rows < N_GROUPS, lgt[0:EXPERTS_PER_GROUP], -jnp.inf)
    gmax = jnp.max(glog, axis=0, keepdims=True)
    gsel = jnp.min(jnp.where(glog == gmax, rows, big), axis=0, keepdims=True)
    gw = 1.0 / jnp.sum(jnp.exp(glog - gmax), axis=0, keepdims=True)
    esel = jnp.zeros((EXPERTS_PER_GROUP, tm), jnp.float32)
    for grp in range(N_GROUPS):
        lo = EXPERTS_PER_GROUP * (grp + 1)
        esel = jnp.where(gsel == grp, lgt[lo:lo + EXPERTS_PER_GROUP], esel)
    v0 = jnp.max(esel, axis=0, keepdims=True)
    i0 = jnp.min(jnp.where(esel == v0, rows, big), axis=0, keepdims=True)
    rest = jnp.where(rows == i0, -jnp.inf, esel)
    v1 = jnp.max(rest, axis=0, keepdims=True)
    i1 = jnp.min(jnp.where(rest == v1, rows, big), axis=0, keepdims=True)
    e = jnp.exp(v1 - v0)
    w0 = (1.0 / (1.0 + e)) * gw
    w1 = (e / (1.0 + e)) * gw
    eid0 = gsel * EXPERTS_PER_GROUP + i0
    eid1 = gsel * EXPERTS_PER_GROUP + i1
    ri_ref[...] = jnp.where(rows == 0, eid0, jnp.where(rows == 1, eid1, 0))
    rows_t = lax.broadcasted_iota(jnp.int32, (ROUTER_ROWS, tm), 0)
    slab = jnp.where(rows_t == 0, w0, jnp.where(rows_t == 1, w1, 0.0))
    slab = jnp.where(rows_t == 2, eid0.astype(jnp.float32),
                     jnp.where(rows_t == 3, eid1.astype(jnp.float32), slab))
    rc_ref[...] = slab.T[:, 0:EXPERTS_PER_GROUP]
    ids = lax.broadcasted_iota(jnp.int32, (N_EXPERTS, tm), 0)
    member = jnp.where((ids == eid0) | (ids == eid1), 1.0, 0.0)
    cnt_ref[...] = jnp.sum(member, axis=1, keepdims=True).astype(jnp.int32)


def _mix(oa, ob, x2d, g_a, g_b, w_out_bf, g_ffn, wr, br):
    t = x2d.shape[0]
    tm = TOKEN_TILE
    row = lambda w: pl.BlockSpec((tm, w), lambda i: (i, 0))
    full = lambda a: pl.BlockSpec(a.shape, lambda i: (0, 0))
    return pl.pallas_call(
        _mix_kernel,
        out_shape=(jax.ShapeDtypeStruct((t, D_MODEL), jnp.float32),
                   jax.ShapeDtypeStruct((t, D_MODEL), jnp.bfloat16),
                   jax.ShapeDtypeStruct((EXPERTS_PER_GROUP, t), jnp.int32),
                   jax.ShapeDtypeStruct((t, EXPERTS_PER_GROUP), jnp.float32),
                   jax.ShapeDtypeStruct((t // tm, N_EXPERTS, 1), jnp.int32)),
        grid=(t // tm,),
        in_specs=[row(GROUP_WIDTH), row(GROUP_WIDTH), row(D_MODEL),
                  full(g_a), full(g_b), full(w_out_bf), full(g_ffn), full(wr), full(br)],
        out_specs=(row(D_MODEL), row(D_MODEL),
                   pl.BlockSpec((EXPERTS_PER_GROUP, tm), lambda i: (0, i)),
                   row(EXPERTS_PER_GROUP),
                   pl.BlockSpec((None, N_EXPERTS, 1), lambda i: (i, 0, 0))),
        compiler_params=pltpu.CompilerParams(dimension_semantics=("arbitrary",),
                                             vmem_limit_bytes=VMEM_LIMIT),
        name="mix_router",
    )(oa, ob, x2d, g_a, g_b, w_out_bf, g_ffn, wr, br)


RUN_ALIGN = 16
LOCAL_ROWS = 2 * TOKEN_TILE + N_EXPERTS * RUN_ALIGN
MAX_CHUNKS = LOCAL_ROWS // RUN_ALIGN
SORT_BLOCK = 256
FINAL_BLOCK = 256


def _local_slots(member, lo_vec, prior, e_sel, axis):
    ids = lax.broadcasted_iota(jnp.int32, member.shape, axis)
    slot = jnp.where(ids == e_sel, lo_vec + prior, 0.0)
    return jnp.sum(slot, axis=axis, keepdims=True).astype(jnp.int32)


def _chunk_copy(dst_ref, tile, c, local, glob, sem, to_global):
    g = pl.multiple_of(dst_ref[tile * MAX_CHUNKS + c], RUN_ALIGN)
    l = pl.multiple_of(c * RUN_ALIGN, RUN_ALIGN)
    src, dst = local.at[pl.ds(l, RUN_ALIGN), :], glob.at[pl.ds(g, RUN_ALIGN), :]
    if not to_global:
        src, dst = dst, src
    return pltpu.make_async_copy(src, dst, sem)


def _start_chunks(nchunk_ref, dst_ref, tile, local, glob, sem, to_global):
    def body(c, carry):
        _chunk_copy(dst_ref, tile, c, local, glob, sem, to_global).start()
        return carry
    lax.fori_loop(0, nchunk_ref[tile], body, 0)


def _wait_chunks(nchunk_ref, dst_ref, tile, local, glob, sem, to_global):
    def body(c, carry):
        _chunk_copy(dst_ref, tile, c, local, glob, sem, to_global).wait()
        return carry
    lax.fori_loop(0, nchunk_ref[tile], body, 0)


def _lo_vector(lo_ref, tile, shape, axis):
    ids = lax.broadcasted_iota(jnp.int32, shape, axis)
    vec = jnp.zeros(shape, jnp.float32)
    for e in range(N_EXPERTS):
        vec = jnp.where(ids == e, lo_ref[tile * N_EXPERTS + e].astype(jnp.float32), vec)
    return vec


def _dispatch_kernel(lo_ref, nchunk_ref, dst_ref, pad_tile_ref, nused_ref, ri_ref, h_ref, xs_hbm,
                     upper, sbuf, zbuf, sem, zsem, *, n_tiles):
    i = pl.program_id(0)
    n_steps = pl.num_programs(0)
    slot = i % 2

    def zero_tile(tile):
        rows = pl.ds(pl.multiple_of(tile * ROW_TILE, ROW_TILE), ROW_TILE)
        return pltpu.make_async_copy(zbuf, xs_hbm.at[rows, :], zsem)

    @pl.when(i == 0)
    def _():
        r = lax.broadcasted_iota(jnp.int32, (TOKEN_TILE, TOKEN_TILE), 0)
        c = lax.broadcasted_iota(jnp.int32, (TOKEN_TILE, TOKEN_TILE), 1)
        upper[...] = jnp.where(r < c, 1.0, 0.0).astype(jnp.bfloat16)
        zbuf[...] = jnp.zeros_like(zbuf)
        for e in range(N_EXPERTS):
            @pl.when(pad_tile_ref[e] >= 0)
            def _():
                zero_tile(pad_tile_ref[e]).start()
        lax.fori_loop(nused_ref[0], n_tiles, lambda t, c: (zero_tile(t).start(), c)[1], 0)
        for e in range(N_EXPERTS):
            @pl.when(pad_tile_ref[e] >= 0)
            def _():
                zero_tile(pad_tile_ref[e]).wait()
        lax.fori_loop(nused_ref[0], n_tiles, lambda t, c: (zero_tile(t).wait(), c)[1], 0)

    e0 = ri_ref[0:1, :]
    e1 = ri_ref[1:2, :]
    ids = lax.broadcasted_iota(jnp.int32, (N_EXPERTS, TOKEN_TILE), 0)
    member = jnp.where((ids == e0) | (ids == e1), 1.0, 0.0)
    prior = _dot(member.astype(jnp.bfloat16), upper[...])
    lo_vec = _lo_vector(lo_ref, i, (N_EXPERTS, 1), 0)
    d0 = _local_slots(member, lo_vec, prior, e0, 0)
    d1 = _local_slots(member, lo_vec, prior, e1, 0)
    local = sbuf.at[slot]
    used_rows = nchunk_ref[i] * RUN_ALIGN
    h = h_ref[...]
    for blk in range(LOCAL_ROWS // SORT_BLOCK):
        @pl.when(blk * SORT_BLOCK < used_rows)
        def _():
            rows = blk * SORT_BLOCK + lax.broadcasted_iota(jnp.int32, (SORT_BLOCK, TOKEN_TILE), 0)
            select = jnp.where((rows == d0) | (rows == d1), 1.0, 0.0).astype(jnp.bfloat16)
            local[blk * SORT_BLOCK:(blk + 1) * SORT_BLOCK, :] = _dot(select, h).astype(jnp.bfloat16)

    @pl.when(i > 0)
    def _():
        _wait_chunks(nchunk_ref, dst_ref, i - 1, sbuf.at[1 - slot], xs_hbm, sem, True)

    _start_chunks(nchunk_ref, dst_ref, i, local, xs_hbm, sem, True)

    @pl.when(i == n_steps - 1)
    def _():
        _wait_chunks(nchunk_ref, dst_ref, i, local, xs_hbm, sem, True)


def _dispatch(lo, nchunk, dst, pad_tile, nused, route_i, h2, p_rows):
    n_tok = h2.shape[0]
    grid_spec = pltpu.PrefetchScalarGridSpec(
        num_scalar_prefetch=5,
        grid=(n_tok // TOKEN_TILE,),
        in_specs=[pl.BlockSpec((EXPERTS_PER_GROUP, TOKEN_TILE), lambda i, *_: (0, i)),
                  pl.BlockSpec((TOKEN_TILE, D_MODEL), lambda i, *_: (i, 0))],
        out_specs=pl.BlockSpec(memory_space=pl.ANY),
        scratch_shapes=[pltpu.VMEM((TOKEN_TILE, TOKEN_TILE), jnp.bfloat16),
                        pltpu.VMEM((2, LOCAL_ROWS, D_MODEL), jnp.bfloat16),
                        pltpu.VMEM((ROW_TILE, D_MODEL), jnp.bfloat16),
                        pltpu.SemaphoreType.DMA, pltpu.SemaphoreType.DMA],
    )
    return pl.pallas_call(
        functools.partial(_dispatch_kernel, n_tiles=p_rows // ROW_TILE),
        out_shape=jax.ShapeDtypeStruct((p_rows, D_MODEL), jnp.bfloat16),
        grid_spec=grid_spec,
        compiler_params=pltpu.CompilerParams(dimension_semantics=("arbitrary",),
                                             vmem_limit_bytes=VMEM_LIMIT),
        name="dispatch",
    )(lo, nchunk, dst, pad_tile, nused, route_i, h2)


def _moe_kernel(texp_ref, first_ref, nused_ref, slot_ref, next_ref, xs_ref, wg_hbm, wu_hbm, wd_hbm,
                y_ref, wg_f32, wu_f32, wd_f32, wg_bf, wu_bf, wd_bf, sems):
    j = pl.program_id(0)

    def fetch(expert, slot):
        return [pltpu.make_async_copy(src.at[expert], dst.at[slot], sems.at[slot, k])
                for k, (src, dst) in enumerate(((wg_hbm, wg_f32), (wu_hbm, wu_f32), (wd_hbm, wd_f32)))]

    @pl.when(j == 0)
    def _():
        for cp in fetch(texp_ref[0], 0):
            cp.start()

    @pl.when(j < nused_ref[0])
    def _():
        @pl.when(first_ref[j] == 1)
        def _():
            for s in range(2):
                @pl.when(slot_ref[j] == s)
                def _():
                    for cp in fetch(texp_ref[j], s):
                        cp.wait()
                    wg_bf[...] = wg_f32[s].astype(jnp.bfloat16)
                    wu_bf[...] = wu_f32[s].astype(jnp.bfloat16)
                    wd_bf[...] = wd_f32[s].astype(jnp.bfloat16)

                    @pl.when(next_ref[j] >= 0)
                    def _():
                        for cp in fetch(next_ref[j], 1 - s):
                            cp.start()

        x = xs_ref[...]
        a = _dot(x, wg_bf[...])
        u = _dot(x, wu_bf[...])
        hid = (a * (1.0 / (1.0 + jnp.exp(-a)))) * u
        y_ref[...] = _dot(hid.astype(jnp.bfloat16), wd_bf[...]).astype(jnp.bfloat16)

    @pl.when(j >= nused_ref[0])
    def _():
        y_ref[...] = jnp.zeros_like(y_ref)


def _moe(texp, first, nused, slot, nxt, xs, w_gate, w_up, w_down):
    p_rows = xs.shape[0]
    n_tiles = p_rows // ROW_TILE
    used = lambda j, nu: jnp.minimum(j, nu[0] - 1)
    grid_spec = pltpu.PrefetchScalarGridSpec(
        num_scalar_prefetch=5,
        grid=(n_tiles,),
        in_specs=[
            pl.BlockSpec((ROW_TILE, D_MODEL), lambda j, te, fi, nu, sl, nx: (used(j, nu), 0)),
            pl.BlockSpec(memory_space=pl.ANY),
            pl.BlockSpec(memory_space=pl.ANY),
            pl.BlockSpec(memory_space=pl.ANY),
        ],
        out_specs=pl.BlockSpec((ROW_TILE, D_MODEL), lambda j, te, fi, nu, sl, nx: (j, 0)),
        scratch_shapes=[pltpu.VMEM((2, D_MODEL, D_EXPERT), jnp.float32),
                        pltpu.VMEM((2, D_MODEL, D_EXPERT), jnp.float32),
                        pltpu.VMEM((2, D_EXPERT, D_MODEL), jnp.float32),
                        pltpu.VMEM((D_MODEL, D_EXPERT), jnp.bfloat16),
                        pltpu.VMEM((D_MODEL, D_EXPERT), jnp.bfloat16),
                        pltpu.VMEM((D_EXPERT, D_MODEL), jnp.bfloat16),
                        pltpu.SemaphoreType.DMA((2, 3))],
    )
    return pl.pallas_call(
        _moe_kernel,
        out_shape=jax.ShapeDtypeStruct((p_rows, D_MODEL), jnp.bfloat16),
        grid_spec=grid_spec,
        compiler_params=pltpu.CompilerParams(dimension_semantics=("arbitrary",),
                                             vmem_limit_bytes=VMEM_LIMIT),
        name="moe_experts",
    )(texp, first, nused, slot, nxt, xs, w_gate, w_up, w_down)


def _final_kernel(lo_ref, nchunk_ref, dst_ref, y_hbm, x1_ref, rc_ref, p_ref, gple_ref, wg_ref, wp_ref,
                  gfin_ref, out_ref, lower, ybuf, sems):
    i = pl.program_id(0)
    n_steps = pl.num_programs(0)
    slot = i % 2

    @pl.when(i == 0)
    def _():
        r = lax.broadcasted_iota(jnp.int32, (TOKEN_TILE, TOKEN_TILE), 0)
        c = lax.broadcasted_iota(jnp.int32, (TOKEN_TILE, TOKEN_TILE), 1)
        lower[...] = jnp.where(c < r, 1.0, 0.0).astype(jnp.bfloat16)
        ybuf[...] = jnp.zeros_like(ybuf)
        _start_chunks(nchunk_ref, dst_ref, 0, ybuf.at[0], y_hbm, sems.at[0], False)

    @pl.when(i + 1 < n_steps)
    def _():
        _start_chunks(nchunk_ref, dst_ref, i + 1, ybuf.at[1 - slot], y_hbm, sems.at[1 - slot], False)

    rc = rc_ref[...]
    w0, w1 = rc[:, 0:1], rc[:, 1:2]
    e0, e1 = rc[:, 2:3].astype(jnp.int32), rc[:, 3:4].astype(jnp.int32)
    ids = lax.broadcasted_iota(jnp.int32, (TOKEN_TILE, ROUTER_ROWS), 1)
    member = jnp.where((ids == e0) | (ids == e1), 1.0, 0.0)
    prior = _dot(lower[...], member.astype(jnp.bfloat16))
    lo_vec = _lo_vector(lo_ref, i, (1, ROUTER_ROWS), 1)
    d0 = _local_slots(member, lo_vec, prior, e0, 1)
    d1 = _local_slots(member, lo_vec, prior, e1, 1)

    local = ybuf.at[slot]
    _wait_chunks(nchunk_ref, dst_ref, i, local, y_hbm, sems.at[slot], False)
    for blk in range(TOKEN_TILE // FINAL_BLOCK):
        rows = slice(blk * FINAL_BLOCK, (blk + 1) * FINAL_BLOCK)
        cols = lax.broadcasted_iota(jnp.int32, (FINAL_BLOCK, LOCAL_ROWS), 1)
        pick = jnp.where(cols == d0[rows], w0[rows], jnp.where(cols == d1[rows], w1[rows], 0.0))
        x2 = x1_ref[rows, :] + _dot(pick.astype(jnp.bfloat16), local[...])
        z = _dot(_rms(x2, gple_ref[...]).astype(jnp.bfloat16), wg_ref[...])
        gate = 1.0 / (1.0 + jnp.exp(-z))
        x3 = x2 + _dot(p_ref[rows, :].astype(jnp.bfloat16), wp_ref[...]) * gate
        out_ref[rows, :] = _rms(x3, gfin_ref[...])


def _final(lo, nchunk, dst, y, x1, rc, p2d, g_ple, w_gate_bf, w_proj_bf, g_final):
    t = x1.shape[0]
    tm = TOKEN_TILE
    row = lambda w: pl.BlockSpec((tm, w), lambda i, *_: (i, 0))
    full = lambda a: pl.BlockSpec(a.shape, lambda i, *_: (0, 0))
    grid_spec = pltpu.PrefetchScalarGridSpec(
        num_scalar_prefetch=3,
        grid=(t // tm,),
        in_specs=[pl.BlockSpec(memory_space=pl.ANY), row(D_MODEL), row(EXPERTS_PER_GROUP),
                  row(PLE_DIM), full(g_ple), full(w_gate_bf), full(w_proj_bf), full(g_final)],
        out_specs=row(D_MODEL),
        scratch_shapes=[pltpu.VMEM((TOKEN_TILE, TOKEN_TILE), jnp.bfloat16),
                        pltpu.VMEM((2, LOCAL_ROWS, D_MODEL), jnp.bfloat16),
                        pltpu.SemaphoreType.DMA((2,))],
    )
    return pl.pallas_call(
        _final_kernel,
        out_shape=jax.ShapeDtypeStruct((t, D_MODEL), jnp.float32),
        grid_spec=grid_spec,
        compiler_params=pltpu.CompilerParams(dimension_semantics=("arbitrary",),
                                             vmem_limit_bytes=VMEM_LIMIT),
        name="combine_ple_final",
    )(lo, nchunk, dst, y, x1, rc, p2d, g_ple, w_gate_bf, w_proj_bf, g_final)


def _dispatch_tables(counts):
    n_tt = counts.shape[0]
    p_rows = 2 * n_tt * TOKEN_TILE + n_tt * N_EXPERTS * (RUN_ALIGN - 1) + N_EXPERTS * ROW_TILE
    p_rows = -(-p_rows // ROW_TILE) * ROW_TILE
    n_tiles = p_rows // ROW_TILE
    n8 = (counts + RUN_ALIGN - 1) // RUN_ALIGN * RUN_ALIGN
    lo = jnp.cumsum(n8, axis=1) - n8
    total = jnp.sum(n8, axis=0)
    tiles_per = (total + ROW_TILE - 1) // ROW_TILE
    tile_end = jnp.cumsum(tiles_per)
    tile_start = tile_end - tiles_per
    g = tile_start[None, :] * ROW_TILE + (jnp.cumsum(n8, axis=0) - n8)
    n_used = tile_end[-1]
    tile_ids = jnp.arange(n_tiles, dtype=jnp.int32)
    clamped = jnp.minimum(tile_ids, n_used - 1)
    texp = jnp.sum((tile_end[None, :] <= clamped[:, None]).astype(jnp.int32), axis=1)
    experts = jnp.arange(N_EXPERTS, dtype=jnp.int32)
    of_tile = texp[:, None] == experts[None, :]
    per_tile = lambda v: jnp.sum(jnp.where(of_tile, v[None, :], 0), axis=1)
    first = ((tile_ids == per_tile(tile_start)) & (tile_ids < n_used)).astype(jnp.int32)
    pad_tile = jnp.where(total > 0, tile_end - 1, -1).astype(jnp.int32)
    has_rows = total > 0
    later = has_rows[None, :] & (experts[None, :] > experts[:, None])
    next_expert = jnp.min(jnp.where(later, experts[None, :], N_EXPERTS), axis=1)
    next_expert = jnp.where(next_expert < N_EXPERTS, next_expert, -1)
    slot = (jnp.cumsum(has_rows.astype(jnp.int32)) - 1) % 2
    chunk_row = jnp.arange(MAX_CHUNKS, dtype=jnp.int32)[None, :, None] * RUN_ALIGN
    in_run = (chunk_row >= lo[:, None, :]) & (chunk_row < (lo + n8)[:, None, :])
    dst = jnp.sum(jnp.where(in_run, g[:, None, :] + chunk_row - lo[:, None, :], 0), axis=2)
    nchunk = jnp.sum(n8, axis=1) // RUN_ALIGN
    i32 = lambda a: a.reshape(-1).astype(jnp.int32)
    return (texp, first, n_used.reshape(1).astype(jnp.int32), i32(per_tile(slot)),
            i32(per_tile(next_expert)), pad_tile, i32(lo), i32(nchunk), i32(dst), p_rows)


def kernel(x, p, g_mix, w_in, sink, g_grp_a, g_grp_b, w_out, g_ffn, w_router_group, b_router_group,
           w_router_expert, b_router_expert, w_expert_gate, w_expert_up, w_expert_down, g_ple,
           w_ple_gate, w_ple_proj, g_final):
    batch, seq, _ = x.shape
    n_tok = batch * seq
    depth = w_in.shape[0]
    assert depth == 1, "the final RMSNorm is fused into the last layer's epilogue"
    bf = jnp.bfloat16
    xc = x.reshape(n_tok, D_MODEL)
    for i in range(depth):
        *qkv_a, qb, kb, vb = _proj(xc, g_mix[i][None, :], w_in[i].astype(bf), batch, seq)
        oa = _attn_a(qkv_a, batch, seq)
        ob = _attn_b(qb, kb, vb, sink[i], batch, seq)

        wr = jnp.zeros((D_MODEL, ROUTER_ROWS), jnp.float32)
        wr = wr.at[:, 0:N_GROUPS].set(w_router_group[i])
        wr = wr.at[:, EXPERTS_PER_GROUP:EXPERTS_PER_GROUP + N_EXPERTS].set(w_router_expert[i])
        br = jnp.zeros((ROUTER_ROWS, 1), jnp.float32)
        br = br.at[0:N_GROUPS, 0].set(b_router_group[i])
        br = br.at[EXPERTS_PER_GROUP:EXPERTS_PER_GROUP + N_EXPERTS, 0].set(b_router_expert[i])

        x1, h2, route_i, route_c, counts = _mix(
            oa, ob, xc, g_grp_a[i][None, :], g_grp_b[i][None, :], w_out[i].astype(bf), g_ffn[i][None, :], wr, br)

        texp, first, nused, slot, nxt, pad_tile, lo, nchunk, dst, p_rows = _dispatch_tables(counts[:, :, 0])
        xs = _dispatch(lo, nchunk, dst, pad_tile, nused, route_i, h2, p_rows)
        y = _moe(texp, first, nused, slot, nxt, xs, w_expert_gate[i], w_expert_up[i], w_expert_down[i])
        xc = _final(lo, nchunk, dst, y, x1, route_c, p[i].reshape(n_tok, PLE_DIM), g_ple[i][None, :],
                    w_ple_gate[i].astype(bf), w_ple_proj[i].astype(bf), g_final[None, :])
    return xc.reshape(batch, seq, D_MODEL)
```

```python
import functools

import jax
import jax.numpy as jnp
from jax import lax
from jax.experimental import pallas as pl
from jax.experimental.pallas import tpu as pltpu

D_MODEL = 1024
HEAD_DIM = 64
GROUP_WIDTH = 512
N_HEADS = 8
B_KV_HEADS = 2
B_KV_WIDTH = B_KV_HEADS * HEAD_DIM
A_CONFIGS = ((128, 1), (512, 4), (2048, 16))
B_HALF_WINDOW = 128
N_GROUPS = 4
EXPERTS_PER_GROUP = 8
N_EXPERTS = N_GROUPS * EXPERTS_PER_GROUP
D_EXPERT = 512
PLE_DIM = 256
EPS = 1e-6
NEG = -1e30
LOG2E = 1.4426950408889634

LANES = 128
Q_BLOCK = 128
HEADS_PER_PASS = 2
PASS_WIDTH = HEADS_PER_PASS * HEAD_DIM
ROW_TILE = 512
TOKEN_TILE = 512
ROUTER_ROWS = 128
VMEM_LIMIT = 48 * 1024 * 1024
ATTN_VMEM_LIMIT = 56 * 1024 * 1024


def _rms(x, g):
    r = lax.rsqrt(jnp.mean(x * x, axis=-1, keepdims=True) + EPS)
    return (x * r) * g


def _dot(a, b):
    return jnp.dot(a, b, preferred_element_type=jnp.float32)


PROJ_TILE = 1024


def _proj_kernel(x_ref, g_ref, w_ref, *refs):
    a_refs, (qb_ref, kb_ref, vb_ref, res, *stage_refs) = refs[:9], refs[9:]
    h = _rms(x_ref[...], g_ref[...]).astype(jnp.bfloat16)
    scale = HEAD_DIM ** -0.5 * LOG2E
    W = GROUP_WIDTH
    tiles = GROUP_WIDTH // PASS_WIDTH
    qkv_a = (_dot(h, w_ref[:, 0:W]) * scale, _dot(h, w_ref[:, W:2 * W]), _dot(h, w_ref[:, 2 * W:3 * W]))
    for part in range(3):
        for pair in range(tiles):
            res[part * tiles + pair, 0] = qkv_a[part][:, pair * PASS_WIDTH:(pair + 1) * PASS_WIDTH]
    qb_ref[...] = (_dot(h, w_ref[:, 3 * W:4 * W]) * scale).astype(jnp.bfloat16)
    kb_ref[...] = _dot(h, w_ref[:, 4 * W:4 * W + B_KV_WIDTH]).astype(jnp.bfloat16)
    vb_ref[...] = _dot(h, w_ref[:, 4 * W + B_KV_WIDTH:4 * W + 2 * B_KV_WIDTH]).astype(jnp.bfloat16)
    stages = (res,) + tuple(stage_refs)
    prev_dil = 1
    for c, (_, dil) in enumerate(A_CONFIGS):
        ratio = dil // prev_dil
        rows = PROJ_TILE // dil
        for idx in range(3 * tiles):
            part, pair = idx // tiles, idx % tiles
            out = a_refs[3 * c + part]
            for rho in range(dil):
                if c == 0:
                    sub = res[idx, 0]
                else:
                    sub = stages[c - 1].at[idx, rho % prev_dil][pl.ds(rho // prev_dil, rows, stride=ratio), :]
                    if c < len(A_CONFIGS) - 1:
                        stages[c][idx, rho] = sub
                out[pair, rho] = sub.astype(jnp.bfloat16)
        prev_dil = dil


def _proj(x2d, g_mix, w_in_bf, batch, seq):
    t = x2d.shape[0]
    tm = PROJ_TILE
    tiles_per_seq = seq // tm
    in_width = w_in_bf.shape[1]
    n_pairs = N_HEADS // HEADS_PER_PASS
    row = lambda w: pl.BlockSpec((tm, w), lambda i: (i, 0))
    a_shapes, a_specs = [], []
    for _, dil in A_CONFIGS:
        for _ in range(3):
            a_shapes.append(jax.ShapeDtypeStruct((batch, n_pairs, dil, seq // dil, PASS_WIDTH),
                                                 jnp.bfloat16))
            a_specs.append(pl.BlockSpec((None, n_pairs, dil, tm // dil, PASS_WIDTH),
                                        lambda i: (i // tiles_per_seq, 0, 0, i % tiles_per_seq, 0)))
    wide = jax.ShapeDtypeStruct((t, GROUP_WIDTH), jnp.bfloat16)
    narrow = jax.ShapeDtypeStruct((t, B_KV_WIDTH), jnp.bfloat16)
    return pl.pallas_call(
        _proj_kernel,
        out_shape=tuple(a_shapes) + (wide, narrow, narrow),
        grid=(t // tm,),
        in_specs=[row(D_MODEL),
                  pl.BlockSpec((1, D_MODEL), lambda i: (0, 0)),
                  pl.BlockSpec((D_MODEL, in_width), lambda i: (0, 0), pipeline_mode=pl.Buffered(1))],
        out_specs=tuple(a_specs) + (row(GROUP_WIDTH), row(B_KV_WIDTH), row(B_KV_WIDTH)),
        scratch_shapes=[pltpu.VMEM((3 * GROUP_WIDTH // PASS_WIDTH, dil, tm // dil, PASS_WIDTH), jnp.float32)
                        for _, dil in A_CONFIGS[:-1]],
        compiler_params=pltpu.CompilerParams(dimension_semantics=("arbitrary",),
                                             vmem_limit_bytes=ATTN_VMEM_LIMIT),
        name="proj",
    )(x2d, g_mix, w_in_bf)


def _fill_bias(bias_ref, half, nkeys, dist_scale):
    i = lax.broadcasted_iota(jnp.int32, (Q_BLOCK, nkeys), 0)
    j = lax.broadcasted_iota(jnp.int32, (Q_BLOCK, nkeys), 1)
    for var, rel0 in enumerate((0, -half, Q_BLOCK - nkeys)):
        d = jnp.abs(j + rel0 - i)
        dist = d.astype(jnp.float32) * dist_scale
        for h in range(N_HEADS):
            slope = 2.0 ** (-(h + 1))
            bias_ref[var, h] = jnp.where(d <= half, -(slope * dist) * LOG2E, NEG)


def _key_window(q0, seq_len, half, nkeys):
    lo = q0 - half
    ks = pl.multiple_of(jnp.clip(lo, 0, seq_len - nkeys), HEAD_DIM)
    var = jnp.where(lo < 0, 0, jnp.where(lo > seq_len - nkeys, 2, 1))
    return ks, var


def _fold_lane_tiles(a, op):
    out = a[..., 0:LANES]
    for t in range(1, a.shape[-1] // LANES):
        out = op(out, a[..., t * LANES:(t + 1) * LANES])
    return out


def _attend_pass(q, k, v, bias, sinks):
    nkeys = k.shape[0]
    lane_head = lax.broadcasted_iota(jnp.int32, (Q_BLOCK, PASS_WIDTH), 1) // HEAD_DIM
    zero = jnp.zeros_like(q)
    q_stack = jnp.concatenate(
        [jnp.where(lane_head == h, q, zero) for h in range(HEADS_PER_PASS)], axis=0)
    s = lax.dot_general(q_stack, k, (((1,), (1,)), ((), ())), preferred_element_type=jnp.float32)
    s = s.reshape(HEADS_PER_PASS, Q_BLOCK, nkeys) + bias
    m_tile = _fold_lane_tiles(s, jnp.maximum)
    if sinks is not None:
        tile_head = lax.broadcasted_iota(jnp.int32, m_tile.shape, 0)
        tile_lane = lax.broadcasted_iota(jnp.int32, m_tile.shape, 2)
        sk = jnp.zeros(m_tile.shape, jnp.float32)
        for h in range(HEADS_PER_PASS):
            sk = jnp.where(tile_head == h, sinks[h] * LOG2E, sk)
        m_tile = jnp.maximum(m_tile, sk)
    m = jnp.max(m_tile, axis=-1, keepdims=True)
    p = jnp.exp2(s - m)
    den_tile = _fold_lane_tiles(p, jnp.add)
    if sinks is not None:
        den_tile = den_tile + jnp.where(tile_lane == 0, jnp.exp2(sk - m), 0.0)
    den = jnp.sum(den_tile, axis=-1, keepdims=True)
    pv = _dot(p.reshape(HEADS_PER_PASS * Q_BLOCK, nkeys).astype(jnp.bfloat16), v)
    pv = pv.reshape(HEADS_PER_PASS, Q_BLOCK, PASS_WIDTH)
    shape = (Q_BLOCK, PASS_WIDTH)
    o, m_lanes, den_lanes = (jnp.zeros(shape, jnp.float32) for _ in range(3))
    for h in range(HEADS_PER_PASS):
        o = jnp.where(lane_head == h, pv[h], o)
        m_lanes = jnp.where(lane_head == h, m[h], m_lanes)
        den_lanes = jnp.where(lane_head == h, den[h], den_lanes)
    return o, m_lanes, den_lanes


A_HALF = 64
A_KEYS = Q_BLOCK + 2 * A_HALF
PASSES_PER_BODY = 16
MERGE_BLOCK = 256


def _attn_a_kernel(*refs, seq):
    n_cfg = len(A_CONFIGS)
    qkv = refs[:3 * n_cfg]
    o_ref, bias_ref = refs[3 * n_cfg], refs[3 * n_cfg + 1]
    scratch = refs[3 * n_cfg + 2:]
    pair = pl.program_id(1)

    @pl.when((pl.program_id(0) == 0) & (pair == 0))
    def _():
        for c, (_, dil) in enumerate(A_CONFIGS):
            _fill_bias(bias_ref.at[c], A_HALF, A_KEYS, float(dil))

    for c, (window, dil) in enumerate(A_CONFIGS):
        assert window // (2 * dil) == A_HALF
        q_ref, k_ref, v_ref = qkv[3 * c:3 * c + 3]
        o_sc, m_sc, d_sc = scratch[3 * c:3 * c + 3]
        sub_len = seq // dil
        blocks = sub_len // Q_BLOCK

        def body(idx, carry, q_ref=q_ref, k_ref=k_ref, v_ref=v_ref, o_sc=o_sc, m_sc=m_sc, d_sc=d_sc, c=c,
                 dil=dil, sub_len=sub_len, blocks=blocks):
            rho = idx // blocks
            q0 = pl.multiple_of((idx % blocks) * Q_BLOCK, Q_BLOCK)
            ks, var = _key_window(q0, sub_len, A_HALF, A_KEYS)
            bias = bias_ref[c, var, pl.ds(pair * HEADS_PER_PASS, HEADS_PER_PASS)]
            o, m, den = _attend_pass(q_ref[rho, pl.ds(q0, Q_BLOCK), :], k_ref[rho, pl.ds(ks, A_KEYS), :],
                                     v_ref[rho, pl.ds(ks, A_KEYS), :], bias, None)
            tokens = pl.ds(rho + dil * q0, Q_BLOCK, stride=dil) if dil > 1 else pl.ds(q0, Q_BLOCK)
            o_sc[tokens, :] = o
            m_sc[tokens, :] = m
            d_sc[tokens, :] = den
            return carry

        lax.fori_loop(0, dil * blocks, body, 0, unroll=PASSES_PER_BODY)

    n_cfg = len(A_CONFIGS)

    def merge(blk, carry):
        rows = pl.ds(pl.multiple_of(blk * MERGE_BLOCK, MERGE_BLOCK), MERGE_BLOCK)
        pvs = [scratch[3 * c][rows, :] for c in range(n_cfg)]
        ms = [scratch[3 * c + 1][rows, :] for c in range(n_cfg)]
        dens = [scratch[3 * c + 2][rows, :] for c in range(n_cfg)]
        mx = functools.reduce(jnp.maximum, ms)
        es = [jnp.exp2(m - mx) for m in ms]
        total = functools.reduce(jnp.add, [e * d for e, d in zip(es, dens)])
        o_ref[rows, :] = functools.reduce(jnp.add, [e * pv for e, pv in zip(es, pvs)]) * (1.0 / total)
        return carry

    lax.fori_loop(0, seq // MERGE_BLOCK, merge, 0, unroll=2)


def _attn_a(qkv, batch, seq):
    n_pairs = N_HEADS // HEADS_PER_PASS
    in_specs, scratch = [], [pltpu.VMEM((len(A_CONFIGS), 3, N_HEADS, Q_BLOCK, A_KEYS), jnp.float32)]
    for _, dil in A_CONFIGS:
        blk = (None, None, dil, seq // dil, PASS_WIDTH)
        in_specs += [pl.BlockSpec(blk, lambda b, p: (b, p, 0, 0, 0))] * 3
        scratch += [pltpu.VMEM((seq, PASS_WIDTH), jnp.float32)] * 3
    return pl.pallas_call(
        functools.partial(_attn_a_kernel, seq=seq),
        out_shape=jax.ShapeDtypeStruct((batch, seq, GROUP_WIDTH), jnp.float32),
        grid=(batch, n_pairs),
        in_specs=in_specs,
        out_specs=pl.BlockSpec((None, seq, PASS_WIDTH), lambda b, p: (b, 0, p)),
        scratch_shapes=scratch,
        compiler_params=pltpu.CompilerParams(dimension_semantics=("arbitrary", "arbitrary"),
                                             vmem_limit_bytes=ATTN_VMEM_LIMIT),
        name="attn_a",
    )(*qkv).reshape(batch * seq, GROUP_WIDTH)


def _attn_b_kernel(sink_ref, q_ref, k_ref, v_ref, o_ref, bias_ref, k4_ref, v4_ref, *, half, nkeys,
                   seq_len, q_tile):
    first = (pl.program_id(0) == 0) & (pl.program_id(1) == 0)

    @pl.when(first)
    def _():
        _fill_bias(bias_ref, half, nkeys, 1.0)

    @pl.when(pl.program_id(1) == 0)
    def _():
        c = lax.broadcasted_iota(jnp.int32, (B_KV_WIDTH, GROUP_WIDTH), 0)
        j = lax.broadcasted_iota(jnp.int32, (B_KV_WIDTH, GROUP_WIDTH), 1)
        src = (j // (GROUP_WIDTH // B_KV_HEADS)) * HEAD_DIM + j % HEAD_DIM
        rep = jnp.where(c == src, 1.0, 0.0).astype(jnp.bfloat16)
        chunk = 512

        def body(i, carry):
            rows = pl.ds(pl.multiple_of(i * chunk, chunk), chunk)
            k4_ref[rows, :] = _dot(k_ref[rows, :], rep).astype(jnp.bfloat16)
            v4_ref[rows, :] = _dot(v_ref[rows, :], rep).astype(jnp.bfloat16)
            return carry

        lax.fori_loop(0, seq_len // chunk, body, 0)

    base = pl.program_id(1) * q_tile

    def body(jb, carry):
        row0 = pl.multiple_of(jb * Q_BLOCK, Q_BLOCK)
        ks, var = _key_window(base + row0, seq_len, half, nkeys)
        for g in range(N_HEADS // HEADS_PER_PASS):
            cols = slice(g * PASS_WIDTH, (g + 1) * PASS_WIDTH)
            heads = pl.ds(g * HEADS_PER_PASS, HEADS_PER_PASS)
            sinks = [sink_ref[g * HEADS_PER_PASS + h] for h in range(HEADS_PER_PASS)]
            o, _, den = _attend_pass(q_ref[pl.ds(row0, Q_BLOCK), cols], k4_ref[pl.ds(ks, nkeys), cols],
                                     v4_ref[pl.ds(ks, nkeys), cols], bias_ref[var, heads], sinks)
            o_ref[pl.ds(row0, Q_BLOCK), cols] = (o * (1.0 / den)).astype(o_ref.dtype)
        return carry

    lax.fori_loop(0, q_tile // Q_BLOCK, body, 0,
                  unroll=PASSES_PER_BODY // (N_HEADS // HEADS_PER_PASS))


def _attn_b(q, k, v, sink, batch, seq):
    half = B_HALF_WINDOW
    nkeys = Q_BLOCK + 2 * half
    q_tile = 1024
    view = lambda a: a.reshape(batch, seq, a.shape[-1])
    kern = functools.partial(_attn_b_kernel, half=half, nkeys=nkeys, seq_len=seq, q_tile=q_tile)
    o = pl.pallas_call(
        kern,
        out_shape=jax.ShapeDtypeStruct((batch, seq, GROUP_WIDTH), jnp.bfloat16),
        grid=(batch, seq // q_tile),
        in_specs=[pl.BlockSpec(memory_space=pltpu.SMEM),
                  pl.BlockSpec((None, q_tile, GROUP_WIDTH), lambda b, i: (b, i, 0)),
                  pl.BlockSpec((None, seq, B_KV_WIDTH), lambda b, i: (b, 0, 0)),
                  pl.BlockSpec((None, seq, B_KV_WIDTH), lambda b, i: (b, 0, 0))],
        out_specs=pl.BlockSpec((None, q_tile, GROUP_WIDTH), lambda b, i: (b, i, 0)),
        scratch_shapes=[pltpu.VMEM((3, N_HEADS, Q_BLOCK, nkeys), jnp.float32),
                        pltpu.VMEM((seq, GROUP_WIDTH), jnp.bfloat16),
                        pltpu.VMEM((seq, GROUP_WIDTH), jnp.bfloat16)],
        compiler_params=pltpu.CompilerParams(dimension_semantics=("arbitrary", "arbitrary"),
                                             vmem_limit_bytes=VMEM_LIMIT),
        name="attn_b",
    )(sink, view(q), view(k), view(v))
    return o.reshape(batch * seq, GROUP_WIDTH)


def _mix_kernel(oa_ref, ob_ref, x_ref, ga_ref, gb_ref, wout_ref, gffn_ref, wr_ref, br_ref,
                x1_ref, h2_ref, ri_ref, rc_ref, cnt_ref):
    tm = x_ref.shape[0]
    na = _rms(oa_ref[...], ga_ref[...]).astype(jnp.bfloat16)
    nb = _rms(ob_ref[...].astype(jnp.float32), gb_ref[...]).astype(jnp.bfloat16)
    x1 = (x_ref[...] + _dot(na, wout_ref[0:GROUP_WIDTH, :])
          + _dot(nb, wout_ref[GROUP_WIDTH:2 * GROUP_WIDTH, :]))
    x1_ref[...] = x1
    h2 = _rms(x1, gffn_ref[...])
    h2_ref[...] = h2.astype(jnp.bfloat16)

    h_hi = h2.astype(jnp.bfloat16)
    h_lo = (h2 - h_hi.astype(jnp.float32)).astype(jnp.bfloat16)
    wr = wr_ref[...]
    w_hi = wr.astype(jnp.bfloat16)
    w_lo = (wr - w_hi.astype(jnp.float32)).astype(jnp.bfloat16)
    w_cat = jnp.concatenate([w_hi, w_lo], axis=1)
    lg2 = _dot(h_hi, w_cat) + _dot(h_lo, w_cat)
    lg = lg2[:, 0:ROUTER_ROWS] + lg2[:, ROUTER_ROWS:2 * ROUTER_ROWS]
    lgt = lg.T + br_ref[...]

    rows = lax.broadcasted_iota(jnp.int32, (EXPERTS_PER_GROUP, tm), 0)
    big = jnp.int32(1 << 20)
    glog = jnp.where(rows < N_GROUPS, lgt[0:EXPERTS_PER_GROUP], -jnp.inf)
    gmax = jnp.max(glog, axis=0, keepdims=True)
    gsel = jnp.min(jnp.where(glog == gmax, rows, big), axis=0, keepdims=True)
    gw = 1.0 / jnp.sum(jnp.exp(glog - gmax), axis=0, keepdims=True)
    esel = jnp.zeros((EXPERTS_PER_GROUP, tm), jnp.float32)
    for grp in range(N_GROUPS):
        lo = EXPERTS_PER_GROUP * (grp + 1)
        esel = jnp.where(gsel == grp, lgt[lo:lo + EXPERTS_PER_GROUP], esel)
    v0 = jnp.max(esel, axis=0, keepdims=True)
    i0 = jnp.min(jnp.where(esel == v0, rows, big), axis=0, keepdims=True)
    rest = jnp.where(rows == i0, -jnp.inf, esel)
    v1 = jnp.max(rest, axis=0, keepdims=True)
    i1 = jnp.min(jnp.where(rest == v1, rows, big), axis=0, keepdims=True)
    e = jnp.exp(v1 - v0)
    w0 = (1.0 / (1.0 + e)) * gw
    w1 = (e / (1.0 + e)) * gw
    eid0 = gsel * EXPERTS_PER_GROUP + i0
    eid1 = gsel * EXPERTS_PER_GROUP + i1
    ri_ref[...] = jnp.where(rows == 0, eid0, jnp.where(rows == 1, eid1, 0))
    rows_t = lax.broadcasted_iota(jnp.int32, (ROUTER_ROWS, tm), 0)
    slab = jnp.where(rows_t == 0, w0, jnp.where(rows_t == 1, w1, 0.0))
    slab = jnp.where(rows_t == 2, eid0.astype(jnp.float32),
                     jnp.where(rows_t == 3, eid1.astype(jnp.float32), slab))
    rc_ref[...] = slab.T[:, 0:EXPERTS_PER_GROUP]
    ids = lax.broadcasted_iota(jnp.int32, (N_EXPERTS, tm), 0)
    member = jnp.where((ids == eid0) | (ids == eid1), 1.0, 0.0)
    cnt_ref[...] = jnp.sum(member, axis=1, keepdims=True).astype(jnp.int32)


def _mix(oa, ob, x2d, g_a, g_b, w_out_bf, g_ffn, wr, br):
    t = x2d.shape[0]
    tm = TOKEN_TILE
    row = lambda w: pl.BlockSpec((tm, w), lambda i: (i, 0))
    full = lambda a: pl.BlockSpec(a.shape, lambda i: (0, 0))
    return pl.pallas_call(
        _mix_kernel,
        out_shape=(jax.ShapeDtypeStruct((t, D_MODEL), jnp.float32),
                   jax.ShapeDtypeStruct((t, D_MODEL), jnp.bfloat16),
                   jax.ShapeDtypeStruct((EXPERTS_PER_GROUP, t), jnp.int32),
                   jax.ShapeDtypeStruct((t, EXPERTS_PER_GROUP), jnp.float32),
                   jax.ShapeDtypeStruct((t // tm, N_EXPERTS, 1), jnp.int32)),
        grid=(t // tm,),
        in_specs=[row(GROUP_WIDTH), row(GROUP_WIDTH), row(D_MODEL),
                  full(g_a), full(g_b), full(w_out_bf), full(g_ffn), full(wr), full(br)],
        out_specs=(row(D_MODEL), row(D_MODEL),
                   pl.BlockSpec((EXPERTS_PER_GROUP, tm), lambda i: (0, i)),
                   row(EXPERTS_PER_GROUP),
                   pl.BlockSpec((None, N_EXPERTS, 1), lambda i: (i, 0, 0))),
        compiler_params=pltpu.CompilerParams(dimension_semantics=("arbitrary",),
                                             vmem_limit_bytes=VMEM_LIMIT),
        name="mix_router",
    )(oa, ob, x2d, g_a, g_b, w_out_bf, g_ffn, wr, br)


RUN_ALIGN = 16
LOCAL_ROWS = 2 * TOKEN_TILE + N_EXPERTS * RUN_ALIGN
MAX_CHUNKS = LOCAL_ROWS // RUN_ALIGN
WAIT_GROUP = 8
SORT_BLOCK = 256
FINAL_BLOCK = 256


def _local_slots(member, lo_vec, prior, e_sel, axis):
    ids = lax.broadcasted_iota(jnp.int32, member.shape, axis)
    slot = jnp.where(ids == e_sel, lo_vec + prior, 0.0)
    return jnp.sum(slot, axis=axis, keepdims=True).astype(jnp.int32)


def _chunk_copy(dst_ref, tile, c, local, glob, sem, to_global):
    g = pl.multiple_of(dst_ref[tile * MAX_CHUNKS + c], RUN_ALIGN)
    l = pl.multiple_of(c * RUN_ALIGN, RUN_ALIGN)
    src, dst = local.at[pl.ds(l, RUN_ALIGN), :], glob.at[pl.ds(g, RUN_ALIGN), :]
    if not to_global:
        src, dst = dst, src
    return pltpu.make_async_copy(src, dst, sem)


def _start_chunks(nchunk_ref, dst_ref, tile, local, glob, sem, to_global):
    def body(c, carry):
        _chunk_copy(dst_ref, tile, c, local, glob, sem, to_global).start()
        return carry
    lax.fori_loop(0, nchunk_ref[tile], body, 0)


def _wait_chunks(nchunk_ref, dst_ref, tile, local, glob, sem, to_global):
    del dst_ref, to_global

    def wait_rows(rows):
        pltpu.make_async_copy(local.at[pl.ds(0, rows), :], glob.at[pl.ds(0, rows), :], sem).wait()

    n = nchunk_ref[tile]
    lax.fori_loop(0, n // WAIT_GROUP, lambda c, carry: (wait_rows(WAIT_GROUP * RUN_ALIGN), carry)[1], 0)
    lax.fori_loop(0, n % WAIT_GROUP, lambda c, carry: (wait_rows(RUN_ALIGN), carry)[1], 0)


def _lo_vector(lo_ref, tile, shape, axis):
    ids = lax.broadcasted_iota(jnp.int32, shape, axis)
    vec = jnp.zeros(shape, jnp.float32)
    for e in range(N_EXPERTS):
        vec = jnp.where(ids == e, lo_ref[tile * N_EXPERTS + e].astype(jnp.float32), vec)
    return vec


def _dispatch_kernel(lo_ref, nchunk_ref, dst_ref, pad_tile_ref, nused_ref, ri_ref, h_ref, xs_hbm,
                     upper, sbuf, zbuf, sem, zsem, *, n_tiles):
    i = pl.program_id(0)
    n_steps = pl.num_programs(0)
    slot = i % 2

    def zero_tile(tile):
        rows = pl.ds(pl.multiple_of(tile * ROW_TILE, ROW_TILE), ROW_TILE)
        return pltpu.make_async_copy(zbuf, xs_hbm.at[rows, :], zsem)

    @pl.when(i == 0)
    def _():
        r = lax.broadcasted_iota(jnp.int32, (TOKEN_TILE, TOKEN_TILE), 0)
        c = lax.broadcasted_iota(jnp.int32, (TOKEN_TILE, TOKEN_TILE), 1)
        upper[...] = jnp.where(r < c, 1.0, 0.0).astype(jnp.bfloat16)
        zbuf[...] = jnp.zeros_like(zbuf)
        for e in range(N_EXPERTS):
            @pl.when(pad_tile_ref[e] >= 0)
            def _():
                zero_tile(pad_tile_ref[e]).start()
        lax.fori_loop(nused_ref[0], n_tiles, lambda t, c: (zero_tile(t).start(), c)[1], 0)
        for e in range(N_EXPERTS):
            @pl.when(pad_tile_ref[e] >= 0)
            def _():
                zero_tile(pad_tile_ref[e]).wait()
        lax.fori_loop(nused_ref[0], n_tiles, lambda t, c: (zero_tile(t).wait(), c)[1], 0)

    e0 = ri_ref[0:1, :]
    e1 = ri_ref[1:2, :]
    ids = lax.broadcasted_iota(jnp.int32, (N_EXPERTS, TOKEN_TILE), 0)
    member = jnp.where((ids == e0) | (ids == e1), 1.0, 0.0)
    prior = _dot(member.astype(jnp.bfloat16), upper[...])
    lo_vec = _lo_vector(lo_ref, i, (N_EXPERTS, 1), 0)
    d0 = _local_slots(member, lo_vec, prior, e0, 0)
    d1 = _local_slots(member, lo_vec, prior, e1, 0)
    local = sbuf.at[slot]
    used_rows = nchunk_ref[i] * RUN_ALIGN
    h = h_ref[...]
    for blk in range(LOCAL_ROWS // SORT_BLOCK):
        @pl.when(blk * SORT_BLOCK < used_rows)
        def _():
            rows = blk * SORT_BLOCK + lax.broadcasted_iota(jnp.int32, (SORT_BLOCK, TOKEN_TILE), 0)
            select = jnp.where((rows == d0) | (rows == d1), 1.0, 0.0).astype(jnp.bfloat16)
            local[blk * SORT_BLOCK:(blk + 1) * SORT_BLOCK, :] = _dot(select, h).astype(jnp.bfloat16)

    @pl.when(i > 0)
    def _():
        _wait_chunks(nchunk_ref, dst_ref, i - 1, sbuf.at[1 - slot], xs_hbm, sem, True)

    _start_chunks(nchunk_ref, dst_ref, i, local, xs_hbm, sem, True)

    @pl.when(i == n_steps - 1)
    def _():
        _wait_chunks(nchunk_ref, dst_ref, i, local, xs_hbm, sem, True)


def _dispatch(lo, nchunk, dst, pad_tile, nused, route_i, h2, p_rows):
    n_tok = h2.shape[0]
    grid_spec = pltpu.PrefetchScalarGridSpec(
        num_scalar_prefetch=5,
        grid=(n_tok // TOKEN_TILE,),
        in_specs=[pl.BlockSpec((EXPERTS_PER_GROUP, TOKEN_TILE), lambda i, *_: (0, i)),
                  pl.BlockSpec((TOKEN_TILE, D_MODEL), lambda i, *_: (i, 0))],
        out_specs=pl.BlockSpec(memory_space=pl.ANY),
        scratch_shapes=[pltpu.VMEM((TOKEN_TILE, TOKEN_TILE), jnp.bfloat16),
                        pltpu.VMEM((2, LOCAL_ROWS, D_MODEL), jnp.bfloat16),
                        pltpu.VMEM((ROW_TILE, D_MODEL), jnp.bfloat16),
                        pltpu.SemaphoreType.DMA, pltpu.SemaphoreType.DMA],
    )
    return pl.pallas_call(
        functools.partial(_dispatch_kernel, n_tiles=p_rows // ROW_TILE),
        out_shape=jax.ShapeDtypeStruct((p_rows, D_MODEL), jnp.bfloat16),
        grid_spec=grid_spec,
        compiler_params=pltpu.CompilerParams(dimension_semantics=("arbitrary",),
                                             vmem_limit_bytes=VMEM_LIMIT),
        name="dispatch",
    )(lo, nchunk, dst, pad_tile, nused, route_i, h2)


def _moe_kernel(texp_ref, first_ref, nused_ref, slot_ref, next_ref, xs_ref, wg_hbm, wu_hbm, wd_hbm,
                y_ref, wg_f32, wu_f32, wd_f32, wg_bf, wu_bf, wd_bf, sems):
    j = pl.program_id(0)

    def fetch(expert, slot):
        return [pltpu.make_async_copy(src.at[expert], dst.at[slot], sems.at[slot, k])
                for k, (src, dst) in enumerate(((wg_hbm, wg_f32), (wu_hbm, wu_f32), (wd_hbm, wd_f32)))]

    @pl.when(j == 0)
    def _():
        for cp in fetch(texp_ref[0], 0):
            cp.start()

    @pl.when(j < nused_ref[0])
    def _():
        @pl.when(first_ref[j] == 1)
        def _():
            for s in range(2):
                @pl.when(slot_ref[j] == s)
                def _():
                    for cp in fetch(texp_ref[j], s):
                        cp.wait()
                    wg_bf[...] = wg_f32[s].astype(jnp.bfloat16)
                    wu_bf[...] = wu_f32[s].astype(jnp.bfloat16)
                    wd_bf[...] = wd_f32[s].astype(jnp.bfloat16)

                    @pl.when(next_ref[j] >= 0)
                    def _():
                        for cp in fetch(next_ref[j], 1 - s):
                            cp.start()

        x = xs_ref[...]
        a = _dot(x, wg_bf[...])
        u = _dot(x, wu_bf[...])
        hid = (a * (1.0 / (1.0 + jnp.exp(-a)))) * u
        y_ref[...] = _dot(hid.astype(jnp.bfloat16), wd_bf[...]).astype(jnp.bfloat16)

    @pl.when(j >= nused_ref[0])
    def _():
        y_ref[...] = jnp.zeros_like(y_ref)


def _moe(texp, first, nused, slot, nxt, xs, w_gate, w_up, w_down):
    p_rows = xs.shape[0]
    n_tiles = p_rows // ROW_TILE
    used = lambda j, nu: jnp.minimum(j, nu[0] - 1)
    grid_spec = pltpu.PrefetchScalarGridSpec(
        num_scalar_prefetch=5,
        grid=(n_tiles,),
        in_specs=[
            pl.BlockSpec((ROW_TILE, D_MODEL), lambda j, te, fi, nu, sl, nx: (used(j, nu), 0)),
            pl.BlockSpec(memory_space=pl.ANY),
            pl.BlockSpec(memory_space=pl.ANY),
            pl.BlockSpec(memory_space=pl.ANY),
        ],
        out_specs=pl.BlockSpec((ROW_TILE, D_MODEL), lambda j, te, fi, nu, sl, nx: (j, 0)),
        scratch_shapes=[pltpu.VMEM((2, D_MODEL, D_EXPERT), jnp.float32),
                        pltpu.VMEM((2, D_MODEL, D_EXPERT), jnp.float32),
                        pltpu.VMEM((2, D_EXPERT, D_MODEL), jnp.float32),
                        pltpu.VMEM((D_MODEL, D_EXPERT), jnp.bfloat16),
                        pltpu.VMEM((D_MODEL, D_EXPERT), jnp.bfloat16),
                        pltpu.VMEM((D_EXPERT, D_MODEL), jnp.bfloat16),
                        pltpu.SemaphoreType.DMA((2, 3))],
    )
    return pl.pallas_call(
        _moe_kernel,
        out_shape=jax.ShapeDtypeStruct((p_rows, D_MODEL), jnp.bfloat16),
        grid_spec=grid_spec,
        compiler_params=pltpu.CompilerParams(dimension_semantics=("arbitrary",),
                                             vmem_limit_bytes=VMEM_LIMIT),
        name="moe_experts",
    )(texp, first, nused, slot, nxt, xs, w_gate, w_up, w_down)


def _final_kernel(lo_ref, nchunk_ref, dst_ref, y_hbm, x1_ref, rc_ref, p_ref, gple_ref, wg_ref, wp_ref,
                  gfin_ref, out_ref, lower, ybuf, sems):
    i = pl.program_id(0)
    n_steps = pl.num_programs(0)
    slot = i % 2

    @pl.when(i == 0)
    def _():
        r = lax.broadcasted_iota(jnp.int32, (TOKEN_TILE, TOKEN_TILE), 0)
        c = lax.broadcasted_iota(jnp.int32, (TOKEN_TILE, TOKEN_TILE), 1)
        lower[...] = jnp.where(c < r, 1.0, 0.0).astype(jnp.bfloat16)
        ybuf[...] = jnp.zeros_like(ybuf)
        _start_chunks(nchunk_ref, dst_ref, 0, ybuf.at[0], y_hbm, sems.at[0], False)

    @pl.when(i + 1 < n_steps)
    def _():
        _start_chunks(nchunk_ref, dst_ref, i + 1, ybuf.at[1 - slot], y_hbm, sems.at[1 - slot], False)

    rc = rc_ref[...]
    w0, w1 = rc[:, 0:1], rc[:, 1:2]
    e0, e1 = rc[:, 2:3].astype(jnp.int32), rc[:, 3:4].astype(jnp.int32)
    ids = lax.broadcasted_iota(jnp.int32, (TOKEN_TILE, ROUTER_ROWS), 1)
    member = jnp.where((ids == e0) | (ids == e1), 1.0, 0.0)
    prior = _dot(lower[...], member.astype(jnp.bfloat16))
    lo_vec = _lo_vector(lo_ref, i, (1, ROUTER_ROWS), 1)
    d0 = _local_slots(member, lo_vec, prior, e0, 1)
    d1 = _local_slots(member, lo_vec, prior, e1, 1)

    local = ybuf.at[slot]
    _wait_chunks(nchunk_ref, dst_ref, i, local, y_hbm, sems.at[slot], False)
    for blk in range(TOKEN_TILE // FINAL_BLOCK):
        rows = slice(blk * FINAL_BLOCK, (blk + 1) * FINAL_BLOCK)
        cols = lax.broadcasted_iota(jnp.int32, (FINAL_BLOCK, LOCAL_ROWS), 1)
        pick = jnp.where(cols == d0[rows], w0[rows], jnp.where(cols == d1[rows], w1[rows], 0.0))
        x2 = x1_ref[rows, :] + _dot(pick.astype(jnp.bfloat16), local[...])
        z = _dot(_rms(x2, gple_ref[...]).astype(jnp.bfloat16), wg_ref[...])
        gate = 1.0 / (1.0 + jnp.exp(-z))
        x3 = x2 + _dot(p_ref[rows, :].astype(jnp.bfloat16), wp_ref[...]) * gate
        out_ref[rows, :] = _rms(x3, gfin_ref[...])


def _final(lo, nchunk, dst, y, x1, rc, p2d, g_ple, w_gate_bf, w_proj_bf, g_final):
    t = x1.shape[0]
    tm = TOKEN_TILE
    row = lambda w: pl.BlockSpec((tm, w), lambda i, *_: (i, 0))
    full = lambda a: pl.BlockSpec(a.shape, lambda i, *_: (0, 0))
    grid_spec = pltpu.PrefetchScalarGridSpec(
        num_scalar_prefetch=3,
        grid=(t // tm,),
        in_specs=[pl.BlockSpec(memory_space=pl.ANY), row(D_MODEL), row(EXPERTS_PER_GROUP),
                  row(PLE_DIM), full(g_ple), full(w_gate_bf), full(w_proj_bf), full(g_final)],
        out_specs=row(D_MODEL),
        scratch_shapes=[pltpu.VMEM((TOKEN_TILE, TOKEN_TILE), jnp.bfloat16),
                        pltpu.VMEM((2, LOCAL_ROWS, D_MODEL), jnp.bfloat16),
                        pltpu.SemaphoreType.DMA((2,))],
    )
    return pl.pallas_call(
        _final_kernel,
        out_shape=jax.ShapeDtypeStruct((t, D_MODEL), jnp.float32),
        grid_spec=grid_spec,
        compiler_params=pltpu.CompilerParams(dimension_semantics=("arbitrary",),
                                             vmem_limit_bytes=VMEM_LIMIT),
        name="combine_ple_final",
    )(lo, nchunk, dst, y, x1, rc, p2d, g_ple, w_gate_bf, w_proj_bf, g_final)


def _dispatch_tables(counts):
    n_tt = counts.shape[0]
    p_rows = 2 * n_tt * TOKEN_TILE + n_tt * N_EXPERTS * (RUN_ALIGN - 1) + N_EXPERTS * ROW_TILE
    p_rows = -(-p_rows // ROW_TILE) * ROW_TILE
    n_tiles = p_rows // ROW_TILE
    n8 = (counts + RUN_ALIGN - 1) // RUN_ALIGN * RUN_ALIGN
    lo = jnp.cumsum(n8, axis=1) - n8
    total = jnp.sum(n8, axis=0)
    tiles_per = (total + ROW_TILE - 1) // ROW_TILE
    tile_end = jnp.cumsum(tiles_per)
    tile_start = tile_end - tiles_per
    g = tile_start[None, :] * ROW_TILE + (jnp.cumsum(n8, axis=0) - n8)
    n_used = tile_end[-1]
    tile_ids = jnp.arange(n_tiles, dtype=jnp.int32)
    clamped = jnp.minimum(tile_ids, n_used - 1)
    texp = jnp.sum((tile_end[None, :] <= clamped[:, None]).astype(jnp.int32), axis=1)
    experts = jnp.arange(N_EXPERTS, dtype=jnp.int32)
    of_tile = texp[:, None] == experts[None, :]
    per_tile = lambda v: jnp.sum(jnp.where(of_tile, v[None, :], 0), axis=1)
    first = ((tile_ids == per_tile(tile_start)) & (tile_ids < n_used)).astype(jnp.int32)
    pad_tile = jnp.where(total > 0, tile_end - 1, -1).astype(jnp.int32)
    has_rows = total > 0
    later = has_rows[None, :] & (experts[None, :] > experts[:, None])
    next_expert = jnp.min(jnp.where(later, experts[None, :], N_EXPERTS), axis=1)
    next_expert = jnp.where(next_expert < N_EXPERTS, next_expert, -1)
    slot = (jnp.cumsum(has_rows.astype(jnp.int32)) - 1) % 2
    chunk_row = jnp.arange(MAX_CHUNKS, dtype=jnp.int32)[None, :, None] * RUN_ALIGN
    in_run = (chunk_row >= lo[:, None, :]) & (chunk_row < (lo + n8)[:, None, :])
    dst = jnp.sum(jnp.where(in_run, g[:, None, :] + chunk_row - lo[:, None, :], 0), axis=2)
    nchunk = jnp.sum(n8, axis=1) // RUN_ALIGN
    i32 = lambda a: a.reshape(-1).astype(jnp.int32)
    return (texp, first, n_used.reshape(1).astype(jnp.int32), i32(per_tile(slot)),
            i32(per_tile(next_expert)), pad_tile, i32(lo), i32(nchunk), i32(dst), p_rows)


def kernel(x, p, g_mix, w_in, sink, g_grp_a, g_grp_b, w_out, g_ffn, w_router_group, b_router_group,
           w_router_expert, b_router_expert, w_expert_gate, w_expert_up, w_expert_down, g_ple,
           w_ple_gate, w_ple_proj, g_final):
    batch, seq, _ = x.shape
    n_tok = batch * seq
    depth = w_in.shape[0]
    assert depth == 1, "the final RMSNorm is fused into the last layer's epilogue"
    bf = jnp.bfloat16
    xc = x.reshape(n_tok, D_MODEL)
    for i in range(depth):
        *qkv_a, qb, kb, vb = _proj(xc, g_mix[i][None, :], w_in[i].astype(bf), batch, seq)
        oa = _attn_a(qkv_a, batch, seq)
        ob = _attn_b(qb, kb, vb, sink[i], batch, seq)

        wr = jnp.zeros((D_MODEL, ROUTER_ROWS), jnp.float32)
        wr = wr.at[:, 0:N_GROUPS].set(w_router_group[i])
        wr = wr.at[:, EXPERTS_PER_GROUP:EXPERTS_PER_GROUP + N_EXPERTS].set(w_router_expert[i])
        br = jnp.zeros((ROUTER_ROWS, 1), jnp.float32)
        br = br.at[0:N_GROUPS, 0].set(b_router_group[i])
        br = br.at[EXPERTS_PER_GROUP:EXPERTS_PER_GROUP + N_EXPERTS, 0].set(b_router_expert[i])

        x1, h2, route_i, route_c, counts = _mix(
            oa, ob, xc, g_grp_a[i][None, :], g_grp_b[i][None, :], w_out[i].astype(bf), g_ffn[i][None, :], wr, br)

        texp, first, nused, slot, nxt, pad_tile, lo, nchunk, dst, p_rows = _dispatch_tables(counts[:, :, 0])
        xs = _dispatch(lo, nchunk, dst, pad_tile, nused, route_i, h2, p_rows)
        y = _moe(texp, first, nused, slot, nxt, xs, w_expert_gate[i], w_expert_up[i], w_expert_down[i])
        xc = _final(lo, nchunk, dst, y, x1, route_c, p[i].reshape(n_tok, PLE_DIM), g_ple[i][None, :],
                    w_ple_gate[i].astype(bf), w_ple_proj[i].astype(bf), g_final[None, :])
    return xc.reshape(batch, seq, D_MODEL)
```

```python
import functools

import jax
import jax.numpy as jnp
from jax import lax
from jax.experimental import pallas as pl
from jax.experimental.pallas import tpu as pltpu

D_MODEL = 1024
HEAD_DIM = 64
GROUP_WIDTH = 512
N_HEADS = 8
B_KV_HEADS = 2
B_KV_WIDTH = B_KV_HEADS * HEAD_DIM
A_CONFIGS = ((128, 1), (512, 4), (2048, 16))
B_HALF_WINDOW = 128
N_GROUPS = 4
EXPERTS_PER_GROUP = 8
N_EXPERTS = N_GROUPS * EXPERTS_PER_GROUP
D_EXPERT = 512
PLE_DIM = 256
EPS = 1e-6
NEG = -1e30
LOG2E = 1.4426950408889634

LANES = 128
Q_BLOCK = 128
HEADS_PER_PASS = 2
PASS_WIDTH = HEADS_PER_PASS * HEAD_DIM
ROW_TILE = 512
TOKEN_TILE = 512
ROUTER_ROWS = 128
VMEM_LIMIT = 48 * 1024 * 1024
ATTN_VMEM_LIMIT = 56 * 1024 * 1024


def _rms(x, g):
    r = lax.rsqrt(jnp.mean(x * x, axis=-1, keepdims=True) + EPS)
    return (x * r) * g


def _dot(a, b):
    return jnp.dot(a, b, preferred_element_type=jnp.float32)


PROJ_TILE = 1024


def _proj_kernel(x_ref, g_ref, w_ref, *refs):
    a_refs, (qb_ref, kb_ref, vb_ref, res, *stage_refs) = refs[:9], refs[9:]
    h = _rms(x_ref[...], g_ref[...]).astype(jnp.bfloat16)
    scale = HEAD_DIM ** -0.5 * LOG2E
    W = GROUP_WIDTH
    tiles = GROUP_WIDTH // PASS_WIDTH
    qkv_a = (_dot(h, w_ref[:, 0:W]) * scale, _dot(h, w_ref[:, W:2 * W]), _dot(h, w_ref[:, 2 * W:3 * W]))
    for part in range(3):
        for pair in range(tiles):
            res[part * tiles + pair, 0] = qkv_a[part][:, pair * PASS_WIDTH:(pair + 1) * PASS_WIDTH]
    qb_ref[...] = (_dot(h, w_ref[:, 3 * W:4 * W]) * scale).astype(jnp.bfloat16)
    kb_ref[...] = _dot(h, w_ref[:, 4 * W:4 * W + B_KV_WIDTH]).astype(jnp.bfloat16)
    vb_ref[...] = _dot(h, w_ref[:, 4 * W + B_KV_WIDTH:4 * W + 2 * B_KV_WIDTH]).astype(jnp.bfloat16)
    stages = (res,) + tuple(stage_refs)
    prev_dil = 1
    for c, (_, dil) in enumerate(A_CONFIGS):
        ratio = dil // prev_dil
        rows = PROJ_TILE // dil
        for idx in range(3 * tiles):
            part, pair = idx // tiles, idx % tiles
            out = a_refs[3 * c + part]
            for rho in range(dil):
                if c == 0:
                    sub = res[idx, 0]
                else:
                    sub = stages[c - 1].at[idx, rho % prev_dil][pl.ds(rho // prev_dil, rows, stride=ratio), :]
                    if c < len(A_CONFIGS) - 1:
                        stages[c][idx, rho] = sub
                out[pair, rho] = sub.astype(jnp.bfloat16)
        prev_dil = dil


def _proj(x2d, g_mix, w_in_bf, batch, seq):
    t = x2d.shape[0]
    tm = PROJ_TILE
    tiles_per_seq = seq // tm
    in_width = w_in_bf.shape[1]
    n_pairs = N_HEADS // HEADS_PER_PASS
    row = lambda w: pl.BlockSpec((tm, w), lambda i: (i, 0))
    a_shapes, a_specs = [], []
    for _, dil in A_CONFIGS:
        for _ in range(3):
            a_shapes.append(jax.ShapeDtypeStruct((batch, n_pairs, dil, seq // dil, PASS_WIDTH),
                                                 jnp.bfloat16))
            a_specs.append(pl.BlockSpec((None, n_pairs, dil, tm // dil, PASS_WIDTH),
                                        lambda i: (i // tiles_per_seq, 0, 0, i % tiles_per_seq, 0)))
    wide = jax.ShapeDtypeStruct((t, GROUP_WIDTH), jnp.bfloat16)
    narrow = jax.ShapeDtypeStruct((t, B_KV_WIDTH), jnp.bfloat16)
    return pl.pallas_call(
        _proj_kernel,
        out_shape=tuple(a_shapes) + (wide, narrow, narrow),
        grid=(t // tm,),
        in_specs=[row(D_MODEL),
                  pl.BlockSpec((1, D_MODEL), lambda i: (0, 0)),
                  pl.BlockSpec((D_MODEL, in_width), lambda i: (0, 0), pipeline_mode=pl.Buffered(1))],
        out_specs=tuple(a_specs) + (row(GROUP_WIDTH), row(B_KV_WIDTH), row(B_KV_WIDTH)),
        scratch_shapes=[pltpu.VMEM((3 * GROUP_WIDTH // PASS_WIDTH, dil, tm // dil, PASS_WIDTH), jnp.float32)
                        for _, dil in A_CONFIGS[:-1]],
        compiler_params=pltpu.CompilerParams(dimension_semantics=("arbitrary",),
                                             vmem_limit_bytes=ATTN_VMEM_LIMIT),
        name="proj",
    )(x2d, g_mix, w_in_bf)


def _fill_bias(bias_ref, half, nkeys, dist_scale):
    i = lax.broadcasted_iota(jnp.int32, (Q_BLOCK, nkeys), 0)
    j = lax.broadcasted_iota(jnp.int32, (Q_BLOCK, nkeys), 1)
    for var, rel0 in enumerate((0, -half, Q_BLOCK - nkeys)):
        d = jnp.abs(j + rel0 - i)
        dist = d.astype(jnp.float32) * dist_scale
        for h in range(N_HEADS):
            slope = 2.0 ** (-(h + 1))
            bias_ref[var, h] = jnp.where(d <= half, -(slope * dist) * LOG2E, NEG)


def _key_window(q0, seq_len, half, nkeys):
    lo = q0 - half
    ks = pl.multiple_of(jnp.clip(lo, 0, seq_len - nkeys), HEAD_DIM)
    var = jnp.where(lo < 0, 0, jnp.where(lo > seq_len - nkeys, 2, 1))
    return ks, var


def _fold_lane_tiles(a, op):
    out = a[..., 0:LANES]
    for t in range(1, a.shape[-1] // LANES):
        out = op(out, a[..., t * LANES:(t + 1) * LANES])
    return out


def _attend_pass(q, k, v, bias, sinks):
    nkeys = k.shape[0]
    lane_head = lax.broadcasted_iota(jnp.int32, (Q_BLOCK, PASS_WIDTH), 1) // HEAD_DIM
    zero = jnp.zeros_like(q)
    q_stack = jnp.concatenate(
        [jnp.where(lane_head == h, q, zero) for h in range(HEADS_PER_PASS)], axis=0)
    s = lax.dot_general(q_stack, k, (((1,), (1,)), ((), ())), preferred_element_type=jnp.float32)
    s = s.reshape(HEADS_PER_PASS, Q_BLOCK, nkeys) + bias
    m_tile = _fold_lane_tiles(s, jnp.maximum)
    if sinks is not None:
        tile_head = lax.broadcasted_iota(jnp.int32, m_tile.shape, 0)
        tile_lane = lax.broadcasted_iota(jnp.int32, m_tile.shape, 2)
        sk = jnp.zeros(m_tile.shape, jnp.float32)
        for h in range(HEADS_PER_PASS):
            sk = jnp.where(tile_head == h, sinks[h] * LOG2E, sk)
        m_tile = jnp.maximum(m_tile, sk)
    m = jnp.max(m_tile, axis=-1, keepdims=True)
    p = jnp.exp2(s - m)
    den_tile = _fold_lane_tiles(p, jnp.add)
    if sinks is not None:
        den_tile = den_tile + jnp.where(tile_lane == 0, jnp.exp2(sk - m), 0.0)
    den = jnp.sum(den_tile, axis=-1, keepdims=True)
    pv = _dot(p.reshape(HEADS_PER_PASS * Q_BLOCK, nkeys).astype(jnp.bfloat16), v)
    pv = pv.reshape(HEADS_PER_PASS, Q_BLOCK, PASS_WIDTH)
    shape = (Q_BLOCK, PASS_WIDTH)
    o, m_lanes, den_lanes = (jnp.zeros(shape, jnp.float32) for _ in range(3))
    for h in range(HEADS_PER_PASS):
        o = jnp.where(lane_head == h, pv[h], o)
        m_lanes = jnp.where(lane_head == h, m[h], m_lanes)
        den_lanes = jnp.where(lane_head == h, den[h], den_lanes)
    return o, m_lanes, den_lanes


A_HALF = 64
A_KEYS = Q_BLOCK + 2 * A_HALF
PASSES_PER_BODY = 16
MERGE_BLOCK = 256


def _attn_a_kernel(*refs, seq):
    n_cfg = len(A_CONFIGS)
    qkv = refs[:3 * n_cfg]
    o_ref, bias_ref = refs[3 * n_cfg], refs[3 * n_cfg + 1]
    scratch = refs[3 * n_cfg + 2:]
    pair = pl.program_id(1)

    @pl.when((pl.program_id(0) == 0) & (pair == 0))
    def _():
        for c, (_, dil) in enumerate(A_CONFIGS):
            _fill_bias(bias_ref.at[c], A_HALF, A_KEYS, float(dil))

    for c, (window, dil) in enumerate(A_CONFIGS):
        assert window // (2 * dil) == A_HALF
        q_ref, k_ref, v_ref = qkv[3 * c:3 * c + 3]
        o_sc, m_sc, d_sc = scratch[3 * c:3 * c + 3]
        sub_len = seq // dil
        blocks = sub_len // Q_BLOCK

        def body(idx, carry, q_ref=q_ref, k_ref=k_ref, v_ref=v_ref, o_sc=o_sc, m_sc=m_sc, d_sc=d_sc, c=c,
                 dil=dil, sub_len=sub_len, blocks=blocks):
            rho = idx // blocks
            q0 = pl.multiple_of((idx % blocks) * Q_BLOCK, Q_BLOCK)
            ks, var = _key_window(q0, sub_len, A_HALF, A_KEYS)
            bias = bias_ref[c, var, pl.ds(pair * HEADS_PER_PASS, HEADS_PER_PASS)]
            o, m, den = _attend_pass(q_ref[rho, pl.ds(q0, Q_BLOCK), :], k_ref[rho, pl.ds(ks, A_KEYS), :],
                                     v_ref[rho, pl.ds(ks, A_KEYS), :], bias, None)
            tokens = pl.ds(rho + dil * q0, Q_BLOCK, stride=dil) if dil > 1 else pl.ds(q0, Q_BLOCK)
            o_sc[tokens, :] = o
            m_sc[tokens, :] = m
            d_sc[tokens, :] = den
            return carry

        lax.fori_loop(0, dil * blocks, body, 0, unroll=PASSES_PER_BODY)

    n_cfg = len(A_CONFIGS)

    def merge(blk, carry):
        rows = pl.ds(pl.multiple_of(blk * MERGE_BLOCK, MERGE_BLOCK), MERGE_BLOCK)
        pvs = [scratch[3 * c][rows, :] for c in range(n_cfg)]
        ms = [scratch[3 * c + 1][rows, :] for c in range(n_cfg)]
        dens = [scratch[3 * c + 2][rows, :] for c in range(n_cfg)]
        mx = functools.reduce(jnp.maximum, ms)
        es = [jnp.exp2(m - mx) for m in ms]
        total = functools.reduce(jnp.add, [e * d for e, d in zip(es, dens)])
        o_ref[rows, :] = functools.reduce(jnp.add, [e * pv for e, pv in zip(es, pvs)]) * (1.0 / total)
        return carry

    lax.fori_loop(0, seq // MERGE_BLOCK, merge, 0, unroll=2)


def _attn_a(qkv, batch, seq):
    n_pairs = N_HEADS // HEADS_PER_PASS
    in_specs, scratch = [], [pltpu.VMEM((len(A_CONFIGS), 3, N_HEADS, Q_BLOCK, A_KEYS), jnp.float32)]
    for _, dil in A_CONFIGS:
        blk = (None, None, dil, seq // dil, PASS_WIDTH)
        in_specs += [pl.BlockSpec(blk, lambda b, p: (b, p, 0, 0, 0))] * 3
        scratch += [pltpu.VMEM((seq, PASS_WIDTH), jnp.float32)] * 3
    return pl.pallas_call(
        functools.partial(_attn_a_kernel, seq=seq),
        out_shape=jax.ShapeDtypeStruct((batch, seq, GROUP_WIDTH), jnp.float32),
        grid=(batch, n_pairs),
        in_specs=in_specs,
        out_specs=pl.BlockSpec((None, seq, PASS_WIDTH), lambda b, p: (b, 0, p)),
        scratch_shapes=scratch,
        compiler_params=pltpu.CompilerParams(dimension_semantics=("arbitrary", "arbitrary"),
                                             vmem_limit_bytes=ATTN_VMEM_LIMIT),
        name="attn_a",
    )(*qkv).reshape(batch * seq, GROUP_WIDTH)


def _attn_b_kernel(sink_ref, q_ref, k_ref, v_ref, o_ref, bias_ref, k4_ref, v4_ref, *, half, nkeys,
                   seq_len, q_tile):
    first = (pl.program_id(0) == 0) & (pl.program_id(1) == 0)

    @pl.when(first)
    def _():
        _fill_bias(bias_ref, half, nkeys, 1.0)

    @pl.when(pl.program_id(1) == 0)
    def _():
        c = lax.broadcasted_iota(jnp.int32, (B_KV_WIDTH, GROUP_WIDTH), 0)
        j = lax.broadcasted_iota(jnp.int32, (B_KV_WIDTH, GROUP_WIDTH), 1)
        src = (j // (GROUP_WIDTH // B_KV_HEADS)) * HEAD_DIM + j % HEAD_DIM
        rep = jnp.where(c == src, 1.0, 0.0).astype(jnp.bfloat16)
        chunk = 512

        def body(i, carry):
            rows = pl.ds(pl.multiple_of(i * chunk, chunk), chunk)
            k4_ref[rows, :] = _dot(k_ref[rows, :], rep).astype(jnp.bfloat16)
            v4_ref[rows, :] = _dot(v_ref[rows, :], rep).astype(jnp.bfloat16)
            return carry

        lax.fori_loop(0, seq_len // chunk, body, 0)

    base = pl.program_id(1) * q_tile

    def body(jb, carry):
        row0 = pl.multiple_of(jb * Q_BLOCK, Q_BLOCK)
        ks, var = _key_window(base + row0, seq_len, half, nkeys)
        for g in range(N_HEADS // HEADS_PER_PASS):
            cols = slice(g * PASS_WIDTH, (g + 1) * PASS_WIDTH)
            heads = pl.ds(g * HEADS_PER_PASS, HEADS_PER_PASS)
            sinks = [sink_ref[g * HEADS_PER_PASS + h] for h in range(HEADS_PER_PASS)]
            o, _, den = _attend_pass(q_ref[pl.ds(row0, Q_BLOCK), cols], k4_ref[pl.ds(ks, nkeys), cols],
                                     v4_ref[pl.ds(ks, nkeys), cols], bias_ref[var, heads], sinks)
            o_ref[pl.ds(row0, Q_BLOCK), cols] = (o * (1.0 / den)).astype(o_ref.dtype)
        return carry

    lax.fori_loop(0, q_tile // Q_BLOCK, body, 0,
                  unroll=PASSES_PER_BODY // (N_HEADS // HEADS_PER_PASS))


def _attn_b(q, k, v, sink, batch, seq):
    half = B_HALF_WINDOW
    nkeys = Q_BLOCK + 2 * half
    q_tile = 1024
    view = lambda a: a.reshape(batch, seq, a.shape[-1])
    kern = functools.partial(_attn_b_kernel, half=half, nkeys=nkeys, seq_len=seq, q_tile=q_tile)
    o = pl.pallas_call(
        kern,
        out_shape=jax.ShapeDtypeStruct((batch, seq, GROUP_WIDTH), jnp.bfloat16),
        grid=(batch, seq // q_tile),
        in_specs=[pl.BlockSpec(memory_space=pltpu.SMEM),
                  pl.BlockSpec((None, q_tile, GROUP_WIDTH), lambda b, i: (b, i, 0)),
                  pl.BlockSpec((None, seq, B_KV_WIDTH), lambda b, i: (b, 0, 0)),
                  pl.BlockSpec((None, seq, B_KV_WIDTH), lambda b, i: (b, 0, 0))],
        out_specs=pl.BlockSpec((None, q_tile, GROUP_WIDTH), lambda b, i: (b, i, 0)),
        scratch_shapes=[pltpu.VMEM((3, N_HEADS, Q_BLOCK, nkeys), jnp.float32),
                        pltpu.VMEM((seq, GROUP_WIDTH), jnp.bfloat16),
                        pltpu.VMEM((seq, GROUP_WIDTH), jnp.bfloat16)],
        compiler_params=pltpu.CompilerParams(dimension_semantics=("arbitrary", "arbitrary"),
                                             vmem_limit_bytes=VMEM_LIMIT),
        name="attn_b",
    )(sink, view(q), view(k), view(v))
    return o.reshape(batch * seq, GROUP_WIDTH)


def _mix_kernel(oa_ref, ob_ref, x_ref, ga_ref, gb_ref, wout_ref, gffn_ref, wr_ref, br_ref,
                x1_ref, h2_ref, ri_ref, rc_ref, cnt_ref):
    tm = x_ref.shape[0]
    na = _rms(oa_ref[...], ga_ref[...]).astype(jnp.bfloat16)
    nb = _rms(ob_ref[...].astype(jnp.float32), gb_ref[...]).astype(jnp.bfloat16)
    x1 = (x_ref[...] + _dot(na, wout_ref[0:GROUP_WIDTH, :])
          + _dot(nb, wout_ref[GROUP_WIDTH:2 * GROUP_WIDTH, :]))
    x1_ref[...] = x1
    h2 = _rms(x1, gffn_ref[...])
    h2_ref[...] = h2.astype(jnp.bfloat16)

    h_hi = h2.astype(jnp.bfloat16)
    h_lo = (h2 - h_hi.astype(jnp.float32)).astype(jnp.bfloat16)
    wr = wr_ref[...]
    w_hi = wr.astype(jnp.bfloat16)
    w_lo = (wr - w_hi.astype(jnp.float32)).astype(jnp.bfloat16)
    w_cat = jnp.concatenate([w_hi, w_lo], axis=1)
    lg2 = _dot(h_hi, w_cat) + _dot(h_lo, w_cat)
    lg = lg2[:, 0:ROUTER_ROWS] + lg2[:, ROUTER_ROWS:2 * ROUTER_ROWS]
    lgt = lg.T + br_ref[...]

    rows = lax.broadcasted_iota(jnp.int32, (EXPERTS_PER_GROUP, tm), 0)
    big = jnp.int32(1 << 20)
    glog = jnp.where(rows < N_GROUPS, lgt[0:EXPERTS_PER_GROUP], -jnp.inf)
    gmax = jnp.max(glog, axis=0, keepdims=True)
    gsel = jnp.min(jnp.where(glog == gmax, rows, big), axis=0, keepdims=True)
    gw = 1.0 / jnp.sum(jnp.exp(glog - gmax), axis=0, keepdims=True)
    esel = jnp.zeros((EXPERTS_PER_GROUP, tm), jnp.float32)
    for grp in range(N_GROUPS):
        lo = EXPERTS_PER_GROUP * (grp + 1)
        esel = jnp.where(gsel == grp, lgt[lo:lo + EXPERTS_PER_GROUP], esel)
    v0 = jnp.max(esel, axis=0, keepdims=True)
    i0 = jnp.min(jnp.where(esel == v0, rows, big), axis=0, keepdims=True)
    rest = jnp.where(rows == i0, -jnp.inf, esel)
    v1 = jnp.max(rest, axis=0, keepdims=True)
    i1 = jnp.min(jnp.where(rest == v1, rows, big), axis=0, keepdims=True)
    e = jnp.exp(v1 - v0)
    w0 = (1.0 / (1.0 + e)) * gw
    w1 = (e / (1.0 + e)) * gw
    eid0 = gsel * EXPERTS_PER_GROUP + i0
    eid1 = gsel * EXPERTS_PER_GROUP + i1
    ri_ref[...] = jnp.where(rows == 0, eid0, jnp.where(rows == 1, eid1, 0))
    rows_t = lax.broadcasted_iota(jnp.int32, (ROUTER_ROWS, tm), 0)
    slab = jnp.where(rows_t == 0, w0, jnp.where(rows_t == 1, w1, 0.0))
    slab = jnp.where(rows_t == 2, eid0.astype(jnp.float32),
                     jnp.where(rows_t == 3, eid1.astype(jnp.float32), slab))
    rc_ref[...] = slab.T[:, 0:EXPERTS_PER_GROUP]
    ids = lax.broadcasted_iota(jnp.int32, (N_EXPERTS, tm), 0)
    member = jnp.where((ids == eid0) | (ids == eid1), 1.0, 0.0)
    cnt_ref[...] = jnp.sum(member, axis=1, keepdims=True).astype(jnp.int32)


def _mix(oa, ob, x2d, g_a, g_b, w_out_bf, g_ffn, wr, br):
    t = x2d.shape[0]
    tm = TOKEN_TILE
    row = lambda w: pl.BlockSpec((tm, w), lambda i: (i, 0))
    full = lambda a: pl.BlockSpec(a.shape, lambda i: (0, 0))
    return pl.pallas_call(
        _mix_kernel,
        out_shape=(jax.ShapeDtypeStruct((t, D_MODEL), jnp.float32),
                   jax.ShapeDtypeStruct((t, D_MODEL), jnp.bfloat16),
                   jax.ShapeDtypeStruct((EXPERTS_PER_GROUP, t), jnp.int32),
                   jax.ShapeDtypeStruct((t, EXPERTS_PER_GROUP), jnp.float32),
                   jax.ShapeDtypeStruct((t // tm, N_EXPERTS, 1), jnp.int32)),
        grid=(t // tm,),
        in_specs=[row(GROUP_WIDTH), row(GROUP_WIDTH), row(D_MODEL),
                  full(g_a), full(g_b), full(w_out_bf), full(g_ffn), full(wr), full(br)],
        out_specs=(row(D_MODEL), row(D_MODEL),
                   pl.BlockSpec((EXPERTS_PER_GROUP, tm), lambda i: (0, i)),
                   row(EXPERTS_PER_GROUP),
                   pl.BlockSpec((None, N_EXPERTS, 1), lambda i: (i, 0, 0))),
        compiler_params=pltpu.CompilerParams(dimension_semantics=("arbitrary",),
                                             vmem_limit_bytes=VMEM_LIMIT),
        name="mix_router",
    )(oa, ob, x2d, g_a, g_b, w_out_bf, g_ffn, wr, br)


RUN_ALIGN = 16
LOCAL_ROWS = 2 * TOKEN_TILE + N_EXPERTS * RUN_ALIGN
MAX_CHUNKS = LOCAL_ROWS // RUN_ALIGN
START_UNROLL = 4
SORT_BLOCK = 256
FINAL_BLOCK = 256


def _local_slots(member, lo_vec, prior, e_sel, axis):
    ids = lax.broadcasted_iota(jnp.int32, member.shape, axis)
    slot = jnp.where(ids == e_sel, lo_vec + prior, 0.0)
    return jnp.sum(slot, axis=axis, keepdims=True).astype(jnp.int32)


def _chunk_copy(dst_ref, tile, c, local, glob, sem, to_global):
    g = pl.multiple_of(dst_ref[tile * MAX_CHUNKS + c], RUN_ALIGN)
    l = pl.multiple_of(c * RUN_ALIGN, RUN_ALIGN)
    src, dst = local.at[pl.ds(l, RUN_ALIGN), :], glob.at[pl.ds(g, RUN_ALIGN), :]
    if not to_global:
        src, dst = dst, src
    return pltpu.make_async_copy(src, dst, sem)


def _start_chunks(nchunk_ref, dst_ref, tile, local, glob, sem, to_global):
    n = nchunk_ref[tile]
    groups = n // START_UNROLL

    def group(gi, carry):
        for u in range(START_UNROLL):
            _chunk_copy(dst_ref, tile, gi * START_UNROLL + u, local, glob, sem, to_global).start()
        return carry

    def single(c, carry):
        _chunk_copy(dst_ref, tile, groups * START_UNROLL + c, local, glob, sem, to_global).start()
        return carry

    lax.fori_loop(0, groups, group, 0)
    lax.fori_loop(0, n % START_UNROLL, single, 0)


def _wait_chunks(nchunk_ref, tile, local, glob, sem):
    n = nchunk_ref[tile]
    size = 1
    while size <= MAX_CHUNKS:
        rows = size * RUN_ALIGN

        @pl.when((n // size) % 2 == 1)
        def _(rows=rows):
            pltpu.make_async_copy(local.at[pl.ds(0, rows), :], glob.at[pl.ds(0, rows), :], sem).wait()

        size *= 2


def _lo_vector(lo_ref, tile, shape, axis):
    ids = lax.broadcasted_iota(jnp.int32, shape, axis)
    vec = jnp.zeros(shape, jnp.float32)
    for e in range(N_EXPERTS):
        vec = jnp.where(ids == e, lo_ref[tile * N_EXPERTS + e].astype(jnp.float32), vec)
    return vec


def _dispatch_kernel(lo_ref, nchunk_ref, dst_ref, pad_tile_ref, nused_ref, ri_ref, h_ref, xs_hbm,
                     upper, sbuf, zbuf, sem, zsem, *, n_tiles):
    i = pl.program_id(0)
    n_steps = pl.num_programs(0)
    slot = i % 2

    def zero_tile(tile):
        rows = pl.ds(pl.multiple_of(tile * ROW_TILE, ROW_TILE), ROW_TILE)
        return pltpu.make_async_copy(zbuf, xs_hbm.at[rows, :], zsem)

    @pl.when(i == 0)
    def _():
        r = lax.broadcasted_iota(jnp.int32, (TOKEN_TILE, TOKEN_TILE), 0)
        c = lax.broadcasted_iota(jnp.int32, (TOKEN_TILE, TOKEN_TILE), 1)
        upper[...] = jnp.where(r < c, 1.0, 0.0).astype(jnp.bfloat16)
        zbuf[...] = jnp.zeros_like(zbuf)
        for e in range(N_EXPERTS):
            @pl.when(pad_tile_ref[e] >= 0)
            def _():
                zero_tile(pad_tile_ref[e]).start()
        lax.fori_loop(nused_ref[0], n_tiles, lambda t, c: (zero_tile(t).start(), c)[1], 0)
        for e in range(N_EXPERTS):
            @pl.when(pad_tile_ref[e] >= 0)
            def _():
                zero_tile(pad_tile_ref[e]).wait()
        lax.fori_loop(nused_ref[0], n_tiles, lambda t, c: (zero_tile(t).wait(), c)[1], 0)

    e0 = ri_ref[0:1, :]
    e1 = ri_ref[1:2, :]
    ids = lax.broadcasted_iota(jnp.int32, (N_EXPERTS, TOKEN_TILE), 0)
    member = jnp.where((ids == e0) | (ids == e1), 1.0, 0.0)
    prior = _dot(member.astype(jnp.bfloat16), upper[...])
    lo_vec = _lo_vector(lo_ref, i, (N_EXPERTS, 1), 0)
    d0 = _local_slots(member, lo_vec, prior, e0, 0)
    d1 = _local_slots(member, lo_vec, prior, e1, 0)
    local = sbuf.at[slot]
    used_rows = nchunk_ref[i] * RUN_ALIGN
    h = h_ref[...]
    for blk in range(LOCAL_ROWS // SORT_BLOCK):
        @pl.when(blk * SORT_BLOCK < used_rows)
        def _():
            rows = blk * SORT_BLOCK + lax.broadcasted_iota(jnp.int32, (SORT_BLOCK, TOKEN_TILE), 0)
            select = jnp.where((rows == d0) | (rows == d1), 1.0, 0.0).astype(jnp.bfloat16)
            local[blk * SORT_BLOCK:(blk + 1) * SORT_BLOCK, :] = _dot(select, h).astype(jnp.bfloat16)

    @pl.when(i > 0)
    def _():
        _wait_chunks(nchunk_ref, i - 1, sbuf.at[1 - slot], xs_hbm, sem)

    _start_chunks(nchunk_ref, dst_ref, i, local, xs_hbm, sem, True)

    @pl.when(i == n_steps - 1)
    def _():
        _wait_chunks(nchunk_ref, i, local, xs_hbm, sem)


def _dispatch(lo, nchunk, dst, pad_tile, nused, route_i, h2, p_rows):
    n_tok = h2.shape[0]
    grid_spec = pltpu.PrefetchScalarGridSpec(
        num_scalar_prefetch=5,
        grid=(n_tok // TOKEN_TILE,),
        in_specs=[pl.BlockSpec((EXPERTS_PER_GROUP, TOKEN_TILE), lambda i, *_: (0, i)),
                  pl.BlockSpec((TOKEN_TILE, D_MODEL), lambda i, *_: (i, 0))],
        out_specs=pl.BlockSpec(memory_space=pl.ANY),
        scratch_shapes=[pltpu.VMEM((TOKEN_TILE, TOKEN_TILE), jnp.bfloat16),
                        pltpu.VMEM((2, LOCAL_ROWS, D_MODEL), jnp.bfloat16),
                        pltpu.VMEM((ROW_TILE, D_MODEL), jnp.bfloat16),
                        pltpu.SemaphoreType.DMA, pltpu.SemaphoreType.DMA],
    )
    return pl.pallas_call(
        functools.partial(_dispatch_kernel, n_tiles=p_rows // ROW_TILE),
        out_shape=jax.ShapeDtypeStruct((p_rows, D_MODEL), jnp.bfloat16),
        grid_spec=grid_spec,
        compiler_params=pltpu.CompilerParams(dimension_semantics=("arbitrary",),
                                             vmem_limit_bytes=VMEM_LIMIT),
        name="dispatch",
    )(lo, nchunk, dst, pad_tile, nused, route_i, h2)


def _moe_kernel(texp_ref, first_ref, nused_ref, slot_ref, next_ref, xs_ref, wg_hbm, wu_hbm, wd_hbm,
                y_ref, wg_f32, wu_f32, wd_f32, wg_bf, wu_bf, wd_bf, sems):
    j = pl.program_id(0)

    def fetch(expert, slot):
        return [pltpu.make_async_copy(src.at[expert], dst.at[slot], sems.at[slot, k])
                for k, (src, dst) in enumerate(((wg_hbm, wg_f32), (wu_hbm, wu_f32), (wd_hbm, wd_f32)))]

    @pl.when(j == 0)
    def _():
        for cp in fetch(texp_ref[0], 0):
            cp.start()

    @pl.when(j < nused_ref[0])
    def _():
        @pl.when(first_ref[j] == 1)
        def _():
            for s in range(2):
                @pl.when(slot_ref[j] == s)
                def _():
                    for cp in fetch(texp_ref[j], s):
                        cp.wait()
                    wg_bf[...] = wg_f32[s].astype(jnp.bfloat16)
                    wu_bf[...] = wu_f32[s].astype(jnp.bfloat16)
                    wd_bf[...] = wd_f32[s].astype(jnp.bfloat16)

                    @pl.when(next_ref[j] >= 0)
                    def _():
                        for cp in fetch(next_ref[j], 1 - s):
                            cp.start()

        x = xs_ref[...]
        a = _dot(x, wg_bf[...])
        u = _dot(x, wu_bf[...])
        hid = (a * (1.0 / (1.0 + jnp.exp(-a)))) * u
        y_ref[...] = _dot(hid.astype(jnp.bfloat16), wd_bf[...]).astype(jnp.bfloat16)

    @pl.when(j >= nused_ref[0])
    def _():
        y_ref[...] = jnp.zeros_like(y_ref)


def _moe(texp, first, nused, slot, nxt, xs, w_gate, w_up, w_down):
    p_rows = xs.shape[0]
    n_tiles = p_rows // ROW_TILE
    used = lambda j, nu: jnp.minimum(j, nu[0] - 1)
    grid_spec = pltpu.PrefetchScalarGridSpec(
        num_scalar_prefetch=5,
        grid=(n_tiles,),
        in_specs=[
            pl.BlockSpec((ROW_TILE, D_MODEL), lambda j, te, fi, nu, sl, nx: (used(j, nu), 0)),
            pl.BlockSpec(memory_space=pl.ANY),
            pl.BlockSpec(memory_space=pl.ANY),
            pl.BlockSpec(memory_space=pl.ANY),
        ],
        out_specs=pl.BlockSpec((ROW_TILE, D_MODEL), lambda j, te, fi, nu, sl, nx: (j, 0)),
        scratch_shapes=[pltpu.VMEM((2, D_MODEL, D_EXPERT), jnp.float32),
                        pltpu.VMEM((2, D_MODEL, D_EXPERT), jnp.float32),
                        pltpu.VMEM((2, D_EXPERT, D_MODEL), jnp.float32),
                        pltpu.VMEM((D_MODEL, D_EXPERT), jnp.bfloat16),
                        pltpu.VMEM((D_MODEL, D_EXPERT), jnp.bfloat16),
                        pltpu.VMEM((D_EXPERT, D_MODEL), jnp.bfloat16),
                        pltpu.SemaphoreType.DMA((2, 3))],
    )
    return pl.pallas_call(
        _moe_kernel,
        out_shape=jax.ShapeDtypeStruct((p_rows, D_MODEL), jnp.bfloat16),
        grid_spec=grid_spec,
        compiler_params=pltpu.CompilerParams(dimension_semantics=("arbitrary",),
                                             vmem_limit_bytes=VMEM_LIMIT),
        name="moe_experts",
    )(texp, first, nused, slot, nxt, xs, w_gate, w_up, w_down)


def _final_kernel(lo_ref, nchunk_ref, dst_ref, y_hbm, x1_ref, rc_ref, p_ref, gple_ref, wg_ref, wp_ref,
                  gfin_ref, out_ref, lower, ybuf, sems):
    i = pl.program_id(0)
    n_steps = pl.num_programs(0)
    slot = i % 2

    @pl.when(i == 0)
    def _():
        r = lax.broadcasted_iota(jnp.int32, (TOKEN_TILE, TOKEN_TILE), 0)
        c = lax.broadcasted_iota(jnp.int32, (TOKEN_TILE, TOKEN_TILE), 1)
        lower[...] = jnp.where(c < r, 1.0, 0.0).astype(jnp.bfloat16)
        ybuf[...] = jnp.zeros_like(ybuf)
        _start_chunks(nchunk_ref, dst_ref, 0, ybuf.at[0], y_hbm, sems.at[0], False)

    @pl.when(i + 1 < n_steps)
    def _():
        _start_chunks(nchunk_ref, dst_ref, i + 1, ybuf.at[1 - slot], y_hbm, sems.at[1 - slot], False)

    rc = rc_ref[...]
    w0, w1 = rc[:, 0:1], rc[:, 1:2]
    e0, e1 = rc[:, 2:3].astype(jnp.int32), rc[:, 3:4].astype(jnp.int32)
    ids = lax.broadcasted_iota(jnp.int32, (TOKEN_TILE, ROUTER_ROWS), 1)
    member = jnp.where((ids == e0) | (ids == e1), 1.0, 0.0)
    prior = _dot(lower[...], member.astype(jnp.bfloat16))
    lo_vec = _lo_vector(lo_ref, i, (1, ROUTER_ROWS), 1)
    d0 = _local_slots(member, lo_vec, prior, e0, 1)
    d1 = _local_slots(member, lo_vec, prior, e1, 1)

    local = ybuf.at[slot]
    _wait_chunks(nchunk_ref, i, local, y_hbm, sems.at[slot])
    for blk in range(TOKEN_TILE // FINAL_BLOCK):
        rows = slice(blk * FINAL_BLOCK, (blk + 1) * FINAL_BLOCK)
        cols = lax.broadcasted_iota(jnp.int32, (FINAL_BLOCK, LOCAL_ROWS), 1)
        pick = jnp.where(cols == d0[rows], w0[rows], jnp.where(cols == d1[rows], w1[rows], 0.0))
        x2 = x1_ref[rows, :] + _dot(pick.astype(jnp.bfloat16), local[...])
        z = _dot(_rms(x2, gple_ref[...]).astype(jnp.bfloat16), wg_ref[...])
        gate = 1.0 / (1.0 + jnp.exp(-z))
        x3 = x2 + _dot(p_ref[rows, :].astype(jnp.bfloat16), wp_ref[...]) * gate
        out_ref[rows, :] = _rms(x3, gfin_ref[...])


def _final(lo, nchunk, dst, y, x1, rc, p2d, g_ple, w_gate_bf, w_proj_bf, g_final):
    t = x1.shape[0]
    tm = TOKEN_TILE
    row = lambda w: pl.BlockSpec((tm, w), lambda i, *_: (i, 0))
    full = lambda a: pl.BlockSpec(a.shape, lambda i, *_: (0, 0))
    grid_spec = pltpu.PrefetchScalarGridSpec(
        num_scalar_prefetch=3,
        grid=(t // tm,),
        in_specs=[pl.BlockSpec(memory_space=pl.ANY), row(D_MODEL), row(EXPERTS_PER_GROUP),
                  row(PLE_DIM), full(g_ple), full(w_gate_bf), full(w_proj_bf), full(g_final)],
        out_specs=row(D_MODEL),
        scratch_shapes=[pltpu.VMEM((TOKEN_TILE, TOKEN_TILE), jnp.bfloat16),
                        pltpu.VMEM((2, LOCAL_ROWS, D_MODEL), jnp.bfloat16),
                        pltpu.SemaphoreType.DMA((2,))],
    )
    return pl.pallas_call(
        _final_kernel,
        out_shape=jax.ShapeDtypeStruct((t, D_MODEL), jnp.float32),
        grid_spec=grid_spec,
        compiler_params=pltpu.CompilerParams(dimension_semantics=("arbitrary",),
                                             vmem_limit_bytes=VMEM_LIMIT),
        name="combine_ple_final",
    )(lo, nchunk, dst, y, x1, rc, p2d, g_ple, w_gate_bf, w_proj_bf, g_final)


def _dispatch_tables(counts):
    n_tt = counts.shape[0]
    p_rows = 2 * n_tt * TOKEN_TILE + n_tt * N_EXPERTS * (RUN_ALIGN - 1) + N_EXPERTS * ROW_TILE
    p_rows = -(-p_rows // ROW_TILE) * ROW_TILE
    n_tiles = p_rows // ROW_TILE
    n8 = (counts + RUN_ALIGN - 1) // RUN_ALIGN * RUN_ALIGN
    lo = jnp.cumsum(n8, axis=1) - n8
    total = jnp.sum(n8, axis=0)
    tiles_per = (total + ROW_TILE - 1) // ROW_TILE
    tile_end = jnp.cumsum(tiles_per)
    tile_start = tile_end - tiles_per
    g = tile_start[None, :] * ROW_TILE + (jnp.cumsum(n8, axis=0) - n8)
    n_used = tile_end[-1]
    tile_ids = jnp.arange(n_tiles, dtype=jnp.int32)
    clamped = jnp.minimum(tile_ids, n_used - 1)
    texp = jnp.sum((tile_end[None, :] <= clamped[:, None]).astype(jnp.int32), axis=1)
    experts = jnp.arange(N_EXPERTS, dtype=jnp.int32)
    of_tile = texp[:, None] == experts[None, :]
    per_tile = lambda v: jnp.sum(jnp.where(of_tile, v[None, :], 0), axis=1)
    first = ((tile_ids == per_tile(tile_start)) & (tile_ids < n_used)).astype(jnp.int32)
    pad_tile = jnp.where(total > 0, tile_end - 1, -1).astype(jnp.int32)
    has_rows = total > 0
    later = has_rows[None, :] & (experts[None, :] > experts[:, None])
    next_expert = jnp.min(jnp.where(later, experts[None, :], N_EXPERTS), axis=1)
    next_expert = jnp.where(next_expert < N_EXPERTS, next_expert, -1)
    slot = (jnp.cumsum(has_rows.astype(jnp.int32)) - 1) % 2
    chunk_row = jnp.arange(MAX_CHUNKS, dtype=jnp.int32)[None, :, None] * RUN_ALIGN
    in_run = (chunk_row >= lo[:, None, :]) & (chunk_row < (lo + n8)[:, None, :])
    dst = jnp.sum(jnp.where(in_run, g[:, None, :] + chunk_row - lo[:, None, :], 0), axis=2)
    nchunk = jnp.sum(n8, axis=1) // RUN_ALIGN
    i32 = lambda a: a.reshape(-1).astype(jnp.int32)
    return (texp, first, n_used.reshape(1).astype(jnp.int32), i32(per_tile(slot)),
            i32(per_tile(next_expert)), pad_tile, i32(lo), i32(nchunk), i32(dst), p_rows)


def kernel(x, p, g_mix, w_in, sink, g_grp_a, g_grp_b, w_out, g_ffn, w_router_group, b_router_group,
           w_router_expert, b_router_expert, w_expert_gate, w_expert_up, w_expert_down, g_ple,
           w_ple_gate, w_ple_proj, g_final):
    batch, seq, _ = x.shape
    n_tok = batch * seq
    depth = w_in.shape[0]
    assert depth == 1, "the final RMSNorm is fused into the last layer's epilogue"
    bf = jnp.bfloat16
    xc = x.reshape(n_tok, D_MODEL)
    for i in range(depth):
        *qkv_a, qb, kb, vb = _proj(xc, g_mix[i][None, :], w_in[i].astype(bf), batch, seq)
        oa = _attn_a(qkv_a, batch, seq)
        ob = _attn_b(qb, kb, vb, sink[i], batch, seq)

        wr = jnp.zeros((D_MODEL, ROUTER_ROWS), jnp.float32)
        wr = wr.at[:, 0:N_GROUPS].set(w_router_group[i])
        wr = wr.at[:, EXPERTS_PER_GROUP:EXPERTS_PER_GROUP + N_EXPERTS].set(w_router_expert[i])
        br = jnp.zeros((ROUTER_ROWS, 1), jnp.float32)
        br = br.at[0:N_GROUPS, 0].set(b_router_group[i])
        br = br.at[EXPERTS_PER_GROUP:EXPERTS_PER_GROUP + N_EXPERTS, 0].set(b_router_expert[i])

        x1, h2, route_i, route_c, counts = _mix(
            oa, ob, xc, g_grp_a[i][None, :], g_grp_b[i][None, :], w_out[i].astype(bf), g_ffn[i][None, :], wr, br)

        texp, first, nused, slot, nxt, pad_tile, lo, nchunk, dst, p_rows = _dispatch_tables(counts[:, :, 0])
        xs = _dispatch(lo, nchunk, dst, pad_tile, nused, route_i, h2, p_rows)
        y = _moe(texp, first, nused, slot, nxt, xs, w_expert_gate[i], w_expert_up[i], w_expert_down[i])
        xc = _final(lo, nchunk, dst, y, x1, route_c, p[i].reshape(n_tok, PLE_DIM), g_ple[i][None, :],
                    w_ple_gate[i].astype(bf), w_ple_proj[i].astype(bf), g_final[None, :])
    return xc.reshape(batch, seq, D_MODEL)
```

```python
import functools

import jax
import jax.numpy as jnp
from jax import lax
from jax.experimental import pallas as pl
from jax.experimental.pallas import tpu as pltpu

D_MODEL = 1024
HEAD_DIM = 64
GROUP_WIDTH = 512
N_HEADS = 8
B_KV_HEADS = 2
B_KV_WIDTH = B_KV_HEADS * HEAD_DIM
A_CONFIGS = ((128, 1), (512, 4), (2048, 16))
B_HALF_WINDOW = 128
N_GROUPS = 4
EXPERTS_PER_GROUP = 8
N_EXPERTS = N_GROUPS * EXPERTS_PER_GROUP
D_EXPERT = 512
PLE_DIM = 256
EPS = 1e-6
NEG = -1e30
LOG2E = 1.4426950408889634

LANES = 128
Q_BLOCK = 128
HEADS_PER_PASS = 2
PASS_WIDTH = HEADS_PER_PASS * HEAD_DIM
ROW_TILE = 448
TOKEN_TILE = 512
ROUTER_ROWS = 128
VMEM_LIMIT = 48 * 1024 * 1024
ATTN_VMEM_LIMIT = 56 * 1024 * 1024


def _rms(x, g):
    r = lax.rsqrt(jnp.mean(x * x, axis=-1, keepdims=True) + EPS)
    return (x * r) * g


def _dot(a, b):
    return jnp.dot(a, b, preferred_element_type=jnp.float32)


PROJ_TILE = 1024


def _proj_kernel(x_ref, g_ref, w_ref, *refs):
    a_refs, (qb_ref, kb_ref, vb_ref, res, *stage_refs) = refs[:9], refs[9:]
    h = _rms(x_ref[...], g_ref[...]).astype(jnp.bfloat16)
    scale = HEAD_DIM ** -0.5 * LOG2E
    W = GROUP_WIDTH
    tiles = GROUP_WIDTH // PASS_WIDTH
    qkv_a = (_dot(h, w_ref[:, 0:W]) * scale, _dot(h, w_ref[:, W:2 * W]), _dot(h, w_ref[:, 2 * W:3 * W]))
    for part in range(3):
        for pair in range(tiles):
            res[part * tiles + pair, 0] = qkv_a[part][:, pair * PASS_WIDTH:(pair + 1) * PASS_WIDTH]
    qb_ref[...] = (_dot(h, w_ref[:, 3 * W:4 * W]) * scale).astype(jnp.bfloat16)
    kb_ref[...] = _dot(h, w_ref[:, 4 * W:4 * W + B_KV_WIDTH]).astype(jnp.bfloat16)
    vb_ref[...] = _dot(h, w_ref[:, 4 * W + B_KV_WIDTH:4 * W + 2 * B_KV_WIDTH]).astype(jnp.bfloat16)
    stages = (res,) + tuple(stage_refs)
    prev_dil = 1
    for c, (_, dil) in enumerate(A_CONFIGS):
        ratio = dil // prev_dil
        rows = PROJ_TILE // dil
        for idx in range(3 * tiles):
            part, pair = idx // tiles, idx % tiles
            out = a_refs[3 * c + part]
            for rho in range(dil):
                if c == 0:
                    sub = res[idx, 0]
                else:
                    sub = stages[c - 1].at[idx, rho % prev_dil][pl.ds(rho // prev_dil, rows, stride=ratio), :]
                    if c < len(A_CONFIGS) - 1:
                        stages[c][idx, rho] = sub
                out[pair, rho] = sub.astype(jnp.bfloat16)
        prev_dil = dil


def _proj(x2d, g_mix, w_in_bf, batch, seq):
    t = x2d.shape[0]
    tm = PROJ_TILE
    tiles_per_seq = seq // tm
    in_width = w_in_bf.shape[1]
    n_pairs = N_HEADS // HEADS_PER_PASS
    row = lambda w: pl.BlockSpec((tm, w), lambda i: (i, 0))
    a_shapes, a_specs = [], []
    for _, dil in A_CONFIGS:
        for _ in range(3):
            a_shapes.append(jax.ShapeDtypeStruct((batch, n_pairs, dil, seq // dil, PASS_WIDTH),
                                                 jnp.bfloat16))
            a_specs.append(pl.BlockSpec((None, n_pairs, dil, tm // dil, PASS_WIDTH),
                                        lambda i: (i // tiles_per_seq, 0, 0, i % tiles_per_seq, 0)))
    wide = jax.ShapeDtypeStruct((t, GROUP_WIDTH), jnp.bfloat16)
    narrow = jax.ShapeDtypeStruct((t, B_KV_WIDTH), jnp.bfloat16)
    return pl.pallas_call(
        _proj_kernel,
        out_shape=tuple(a_shapes) + (wide, narrow, narrow),
        grid=(t // tm,),
        in_specs=[row(D_MODEL),
                  pl.BlockSpec((1, D_MODEL), lambda i: (0, 0)),
                  pl.BlockSpec((D_MODEL, in_width), lambda i: (0, 0), pipeline_mode=pl.Buffered(1))],
        out_specs=tuple(a_specs) + (row(GROUP_WIDTH), row(B_KV_WIDTH), row(B_KV_WIDTH)),
        scratch_shapes=[pltpu.VMEM((3 * GROUP_WIDTH // PASS_WIDTH, dil, tm // dil, PASS_WIDTH), jnp.float32)
                        for _, dil in A_CONFIGS[:-1]],
        compiler_params=pltpu.CompilerParams(dimension_semantics=("arbitrary",),
                                             vmem_limit_bytes=ATTN_VMEM_LIMIT),
        name="proj",
    )(x2d, g_mix, w_in_bf)


def _fill_bias(bias_ref, half, nkeys, dist_scale):
    i = lax.broadcasted_iota(jnp.int32, (Q_BLOCK, nkeys), 0)
    j = lax.broadcasted_iota(jnp.int32, (Q_BLOCK, nkeys), 1)
    for var, rel0 in enumerate((0, -half, Q_BLOCK - nkeys)):
        d = jnp.abs(j + rel0 - i)
        dist = d.astype(jnp.float32) * dist_scale
        for h in range(N_HEADS):
            slope = 2.0 ** (-(h + 1))
            bias_ref[var, h] = jnp.where(d <= half, -(slope * dist) * LOG2E, NEG)


def _key_window(q0, seq_len, half, nkeys):
    lo = q0 - half
    ks = pl.multiple_of(jnp.clip(lo, 0, seq_len - nkeys), HEAD_DIM)
    var = jnp.where(lo < 0, 0, jnp.where(lo > seq_len - nkeys, 2, 1))
    return ks, var


def _fold_lane_tiles(a, op):
    out = a[..., 0:LANES]
    for t in range(1, a.shape[-1] // LANES):
        out = op(out, a[..., t * LANES:(t + 1) * LANES])
    return out


def _attend_pass(q, k, v, bias, sinks):
    nkeys = k.shape[0]
    lane_head = lax.broadcasted_iota(jnp.int32, (Q_BLOCK, PASS_WIDTH), 1) // HEAD_DIM
    zero = jnp.zeros_like(q)
    q_stack = jnp.concatenate(
        [jnp.where(lane_head == h, q, zero) for h in range(HEADS_PER_PASS)], axis=0)
    s = lax.dot_general(q_stack, k, (((1,), (1,)), ((), ())), preferred_element_type=jnp.float32)
    s = s.reshape(HEADS_PER_PASS, Q_BLOCK, nkeys) + bias
    m_tile = _fold_lane_tiles(s, jnp.maximum)
    if sinks is not None:
        tile_head = lax.broadcasted_iota(jnp.int32, m_tile.shape, 0)
        tile_lane = lax.broadcasted_iota(jnp.int32, m_tile.shape, 2)
        sk = jnp.zeros(m_tile.shape, jnp.float32)
        for h in range(HEADS_PER_PASS):
            sk = jnp.where(tile_head == h, sinks[h] * LOG2E, sk)
        m_tile = jnp.maximum(m_tile, sk)
    m = jnp.max(m_tile, axis=-1, keepdims=True)
    p = jnp.exp2(s - m)
    den_tile = _fold_lane_tiles(p, jnp.add)
    if sinks is not None:
        den_tile = den_tile + jnp.where(tile_lane == 0, jnp.exp2(sk - m), 0.0)
    den = jnp.sum(den_tile, axis=-1, keepdims=True)
    pv = _dot(p.reshape(HEADS_PER_PASS * Q_BLOCK, nkeys).astype(jnp.bfloat16), v)
    pv = pv.reshape(HEADS_PER_PASS, Q_BLOCK, PASS_WIDTH)
    shape = (Q_BLOCK, PASS_WIDTH)
    o, m_lanes, den_lanes = (jnp.zeros(shape, jnp.float32) for _ in range(3))
    for h in range(HEADS_PER_PASS):
        o = jnp.where(lane_head == h, pv[h], o)
        m_lanes = jnp.where(lane_head == h, m[h], m_lanes)
        den_lanes = jnp.where(lane_head == h, den[h], den_lanes)
    return o, m_lanes, den_lanes


A_HALF = 64
A_KEYS = Q_BLOCK + 2 * A_HALF
PASSES_PER_BODY = 16
MERGE_BLOCK = 256


def _attn_a_kernel(*refs, seq):
    n_cfg = len(A_CONFIGS)
    qkv = refs[:3 * n_cfg]
    o_ref, bias_ref = refs[3 * n_cfg], refs[3 * n_cfg + 1]
    scratch = refs[3 * n_cfg + 2:]
    pair = pl.program_id(1)

    @pl.when((pl.program_id(0) == 0) & (pair == 0))
    def _():
        for c, (_, dil) in enumerate(A_CONFIGS):
            _fill_bias(bias_ref.at[c], A_HALF, A_KEYS, float(dil))

    for c, (window, dil) in enumerate(A_CONFIGS):
        assert window // (2 * dil) == A_HALF
        q_ref, k_ref, v_ref = qkv[3 * c:3 * c + 3]
        o_sc, m_sc, d_sc = scratch[3 * c:3 * c + 3]
        sub_len = seq // dil
        blocks = sub_len // Q_BLOCK

        def body(idx, carry, q_ref=q_ref, k_ref=k_ref, v_ref=v_ref, o_sc=o_sc, m_sc=m_sc, d_sc=d_sc, c=c,
                 dil=dil, sub_len=sub_len, blocks=blocks):
            rho = idx // blocks
            q0 = pl.multiple_of((idx % blocks) * Q_BLOCK, Q_BLOCK)
            ks, var = _key_window(q0, sub_len, A_HALF, A_KEYS)
            bias = bias_ref[c, var, pl.ds(pair * HEADS_PER_PASS, HEADS_PER_PASS)]
            o, m, den = _attend_pass(q_ref[rho, pl.ds(q0, Q_BLOCK), :], k_ref[rho, pl.ds(ks, A_KEYS), :],
                                     v_ref[rho, pl.ds(ks, A_KEYS), :], bias, None)
            tokens = pl.ds(rho + dil * q0, Q_BLOCK, stride=dil) if dil > 1 else pl.ds(q0, Q_BLOCK)
            o_sc[tokens, :] = o
            m_sc[tokens, :] = m
            d_sc[tokens, :] = den
            return carry

        lax.fori_loop(0, dil * blocks, body, 0, unroll=PASSES_PER_BODY)

    n_cfg = len(A_CONFIGS)

    def merge(blk, carry):
        rows = pl.ds(pl.multiple_of(blk * MERGE_BLOCK, MERGE_BLOCK), MERGE_BLOCK)
        pvs = [scratch[3 * c][rows, :] for c in range(n_cfg)]
        ms = [scratch[3 * c + 1][rows, :] for c in range(n_cfg)]
        dens = [scratch[3 * c + 2][rows, :] for c in range(n_cfg)]
        mx = functools.reduce(jnp.maximum, ms)
        es = [jnp.exp2(m - mx) for m in ms]
        total = functools.reduce(jnp.add, [e * d for e, d in zip(es, dens)])
        o_ref[rows, :] = functools.reduce(jnp.add, [e * pv for e, pv in zip(es, pvs)]) * (1.0 / total)
        return carry

    lax.fori_loop(0, seq // MERGE_BLOCK, merge, 0, unroll=2)


def _attn_a(qkv, batch, seq):
    n_pairs = N_HEADS // HEADS_PER_PASS
    in_specs, scratch = [], [pltpu.VMEM((len(A_CONFIGS), 3, N_HEADS, Q_BLOCK, A_KEYS), jnp.float32)]
    for _, dil in A_CONFIGS:
        blk = (None, None, dil, seq // dil, PASS_WIDTH)
        in_specs += [pl.BlockSpec(blk, lambda b, p: (b, p, 0, 0, 0))] * 3
        scratch += [pltpu.VMEM((seq, PASS_WIDTH), jnp.float32)] * 3
    return pl.pallas_call(
        functools.partial(_attn_a_kernel, seq=seq),
        out_shape=jax.ShapeDtypeStruct((batch, seq, GROUP_WIDTH), jnp.float32),
        grid=(batch, n_pairs),
        in_specs=in_specs,
        out_specs=pl.BlockSpec((None, seq, PASS_WIDTH), lambda b, p: (b, 0, p)),
        scratch_shapes=scratch,
        compiler_params=pltpu.CompilerParams(dimension_semantics=("arbitrary", "arbitrary"),
                                             vmem_limit_bytes=ATTN_VMEM_LIMIT),
        name="attn_a",
    )(*qkv).reshape(batch * seq, GROUP_WIDTH)


def _attn_b_kernel(sink_ref, q_ref, k_ref, v_ref, o_ref, bias_ref, k4_ref, v4_ref, *, half, nkeys,
                   seq_len, q_tile):
    first = (pl.program_id(0) == 0) & (pl.program_id(1) == 0)

    @pl.when(first)
    def _():
        _fill_bias(bias_ref, half, nkeys, 1.0)

    @pl.when(pl.program_id(1) == 0)
    def _():
        c = lax.broadcasted_iota(jnp.int32, (B_KV_WIDTH, GROUP_WIDTH), 0)
        j = lax.broadcasted_iota(jnp.int32, (B_KV_WIDTH, GROUP_WIDTH), 1)
        src = (j // (GROUP_WIDTH // B_KV_HEADS)) * HEAD_DIM + j % HEAD_DIM
        rep = jnp.where(c == src, 1.0, 0.0).astype(jnp.bfloat16)
        chunk = 512

        def body(i, carry):
            rows = pl.ds(pl.multiple_of(i * chunk, chunk), chunk)
            k4_ref[rows, :] = _dot(k_ref[rows, :], rep).astype(jnp.bfloat16)
            v4_ref[rows, :] = _dot(v_ref[rows, :], rep).astype(jnp.bfloat16)
            return carry

        lax.fori_loop(0, seq_len // chunk, body, 0)

    base = pl.program_id(1) * q_tile

    def body(jb, carry):
        row0 = pl.multiple_of(jb * Q_BLOCK, Q_BLOCK)
        ks, var = _key_window(base + row0, seq_len, half, nkeys)
        for g in range(N_HEADS // HEADS_PER_PASS):
            cols = slice(g * PASS_WIDTH, (g + 1) * PASS_WIDTH)
            heads = pl.ds(g * HEADS_PER_PASS, HEADS_PER_PASS)
            sinks = [sink_ref[g * HEADS_PER_PASS + h] for h in range(HEADS_PER_PASS)]
            o, _, den = _attend_pass(q_ref[pl.ds(row0, Q_BLOCK), cols], k4_ref[pl.ds(ks, nkeys), cols],
                                     v4_ref[pl.ds(ks, nkeys), cols], bias_ref[var, heads], sinks)
            o_ref[pl.ds(row0, Q_BLOCK), cols] = (o * (1.0 / den)).astype(o_ref.dtype)
        return carry

    lax.fori_loop(0, q_tile // Q_BLOCK, body, 0,
                  unroll=PASSES_PER_BODY // (N_HEADS // HEADS_PER_PASS))


def _attn_b(q, k, v, sink, batch, seq):
    half = B_HALF_WINDOW
    nkeys = Q_BLOCK + 2 * half
    q_tile = 1024
    view = lambda a: a.reshape(batch, seq, a.shape[-1])
    kern = functools.partial(_attn_b_kernel, half=half, nkeys=nkeys, seq_len=seq, q_tile=q_tile)
    o = pl.pallas_call(
        kern,
        out_shape=jax.ShapeDtypeStruct((batch, seq, GROUP_WIDTH), jnp.bfloat16),
        grid=(batch, seq // q_tile),
        in_specs=[pl.BlockSpec(memory_space=pltpu.SMEM),
                  pl.BlockSpec((None, q_tile, GROUP_WIDTH), lambda b, i: (b, i, 0)),
                  pl.BlockSpec((None, seq, B_KV_WIDTH), lambda b, i: (b, 0, 0)),
                  pl.BlockSpec((None, seq, B_KV_WIDTH), lambda b, i: (b, 0, 0))],
        out_specs=pl.BlockSpec((None, q_tile, GROUP_WIDTH), lambda b, i: (b, i, 0)),
        scratch_shapes=[pltpu.VMEM((3, N_HEADS, Q_BLOCK, nkeys), jnp.float32),
                        pltpu.VMEM((seq, GROUP_WIDTH), jnp.bfloat16),
                        pltpu.VMEM((seq, GROUP_WIDTH), jnp.bfloat16)],
        compiler_params=pltpu.CompilerParams(dimension_semantics=("arbitrary", "arbitrary"),
                                             vmem_limit_bytes=VMEM_LIMIT),
        name="attn_b",
    )(sink, view(q), view(k), view(v))
    return o.reshape(batch * seq, GROUP_WIDTH)


def _mix_kernel(oa_ref, ob_ref, x_ref, ga_ref, gb_ref, wout_ref, gffn_ref, wr_ref, br_ref,
                x1_ref, h2_ref, ri_ref, rc_ref, cnt_ref):
    tm = x_ref.shape[0]
    na = _rms(oa_ref[...], ga_ref[...]).astype(jnp.bfloat16)
    nb = _rms(ob_ref[...].astype(jnp.float32), gb_ref[...]).astype(jnp.bfloat16)
    x1 = (x_ref[...] + _dot(na, wout_ref[0:GROUP_WIDTH, :])
          + _dot(nb, wout_ref[GROUP_WIDTH:2 * GROUP_WIDTH, :]))
    x1_ref[...] = x1
    h2 = _rms(x1, gffn_ref[...])
    h2_ref[...] = h2.astype(jnp.bfloat16)

    h_hi = h2.astype(jnp.bfloat16)
    h_lo = (h2 - h_hi.astype(jnp.float32)).astype(jnp.bfloat16)
    wr = wr_ref[...]
    w_hi = wr.astype(jnp.bfloat16)
    w_lo = (wr - w_hi.astype(jnp.float32)).astype(jnp.bfloat16)
    w_cat = jnp.concatenate([w_hi, w_lo], axis=1)
    lg2 = _dot(h_hi, w_cat) + _dot(h_lo, w_cat)
    lg = lg2[:, 0:ROUTER_ROWS] + lg2[:, ROUTER_ROWS:2 * ROUTER_ROWS]
    lgt = lg.T + br_ref[...]

    rows = lax.broadcasted_iota(jnp.int32, (EXPERTS_PER_GROUP, tm), 0)
    big = jnp.int32(1 << 20)
    glog = jnp.where(rows < N_GROUPS, lgt[0:EXPERTS_PER_GROUP], -jnp.inf)
    gmax = jnp.max(glog, axis=0, keepdims=True)
    gsel = jnp.min(jnp.where(glog == gmax, rows, big), axis=0, keepdims=True)
    gw = 1.0 / jnp.sum(jnp.exp(glog - gmax), axis=0, keepdims=True)
    esel = jnp.zeros((EXPERTS_PER_GROUP, tm), jnp.float32)
    for grp in range(N_GROUPS):
        lo = EXPERTS_PER_GROUP * (grp + 1)
        esel = jnp.where(gsel == grp, lgt[lo:lo + EXPERTS_PER_GROUP], esel)
    v0 = jnp.max(esel, axis=0, keepdims=True)
    i0 = jnp.min(jnp.where(esel == v0, rows, big), axis=0, keepdims=True)
    rest = jnp.where(rows == i0, -jnp.inf, esel)
    v1 = jnp.max(rest, axis=0, keepdims=True)
    i1 = jnp.min(jnp.where(rest == v1, rows, big), axis=0, keepdims=True)
    e = jnp.exp(v1 - v0)
    w0 = (1.0 / (1.0 + e)) * gw
    w1 = (e / (1.0 + e)) * gw
    eid0 = gsel * EXPERTS_PER_GROUP + i0
    eid1 = gsel * EXPERTS_PER_GROUP + i1
    ri_ref[...] = jnp.where(rows == 0, eid0, jnp.where(rows == 1, eid1, 0))
    rows_t = lax.broadcasted_iota(jnp.int32, (ROUTER_ROWS, tm), 0)
    slab = jnp.where(rows_t == 0, w0, jnp.where(rows_t == 1, w1, 0.0))
    slab = jnp.where(rows_t == 2, eid0.astype(jnp.float32),
                     jnp.where(rows_t == 3, eid1.astype(jnp.float32), slab))
    rc_ref[...] = slab.T[:, 0:EXPERTS_PER_GROUP]
    ids = lax.broadcasted_iota(jnp.int32, (N_EXPERTS, tm), 0)
    member = jnp.where((ids == eid0) | (ids == eid1), 1.0, 0.0)
    cnt_ref[...] = jnp.sum(member, axis=1, keepdims=True).astype(jnp.int32)


def _mix(oa, ob, x2d, g_a, g_b, w_out_bf, g_ffn, wr, br):
    t = x2d.shape[0]
    tm = TOKEN_TILE
    row = lambda w: pl.BlockSpec((tm, w), lambda i: (i, 0))
    full = lambda a: pl.BlockSpec(a.shape, lambda i: (0, 0))
    return pl.pallas_call(
        _mix_kernel,
        out_shape=(jax.ShapeDtypeStruct((t, D_MODEL), jnp.float32),
                   jax.ShapeDtypeStruct((t, D_MODEL), jnp.bfloat16),
                   jax.ShapeDtypeStruct((EXPERTS_PER_GROUP, t), jnp.int32),
                   jax.ShapeDtypeStruct((t, EXPERTS_PER_GROUP), jnp.float32),
                   jax.ShapeDtypeStruct((t // tm, N_EXPERTS, 1), jnp.int32)),
        grid=(t // tm,),
        in_specs=[row(GROUP_WIDTH), row(GROUP_WIDTH), row(D_MODEL),
                  full(g_a), full(g_b), full(w_out_bf), full(g_ffn), full(wr), full(br)],
        out_specs=(row(D_MODEL), row(D_MODEL),
                   pl.BlockSpec((EXPERTS_PER_GROUP, tm), lambda i: (0, i)),
                   row(EXPERTS_PER_GROUP),
                   pl.BlockSpec((None, N_EXPERTS, 1), lambda i: (i, 0, 0))),
        compiler_params=pltpu.CompilerParams(dimension_semantics=("arbitrary",),
                                             vmem_limit_bytes=VMEM_LIMIT),
        name="mix_router",
    )(oa, ob, x2d, g_a, g_b, w_out_bf, g_ffn, wr, br)


RUN_ALIGN = 16
LOCAL_ROWS = 2 * TOKEN_TILE + N_EXPERTS * RUN_ALIGN
MAX_CHUNKS = LOCAL_ROWS // RUN_ALIGN
START_UNROLL = 4
SORT_BLOCK = 256
FINAL_BLOCK = 256


def _local_slots(member, lo_vec, prior, e_sel, axis):
    ids = lax.broadcasted_iota(jnp.int32, member.shape, axis)
    slot = jnp.where(ids == e_sel, lo_vec + prior, 0.0)
    return jnp.sum(slot, axis=axis, keepdims=True).astype(jnp.int32)


def _chunk_copy(dst_ref, tile, c, local, glob, sem, to_global):
    g = pl.multiple_of(dst_ref[tile * MAX_CHUNKS + c], RUN_ALIGN)
    l = pl.multiple_of(c * RUN_ALIGN, RUN_ALIGN)
    src, dst = local.at[pl.ds(l, RUN_ALIGN), :], glob.at[pl.ds(g, RUN_ALIGN), :]
    if not to_global:
        src, dst = dst, src
    return pltpu.make_async_copy(src, dst, sem)


def _start_chunks(nchunk_ref, dst_ref, tile, local, glob, sem, to_global):
    n = nchunk_ref[tile]
    groups = n // START_UNROLL

    def group(gi, carry):
        for u in range(START_UNROLL):
            _chunk_copy(dst_ref, tile, gi * START_UNROLL + u, local, glob, sem, to_global).start()
        return carry

    def single(c, carry):
        _chunk_copy(dst_ref, tile, groups * START_UNROLL + c, local, glob, sem, to_global).start()
        return carry

    lax.fori_loop(0, groups, group, 0)
    lax.fori_loop(0, n % START_UNROLL, single, 0)


def _wait_chunks(nchunk_ref, tile, local, glob, sem):
    n = nchunk_ref[tile]
    size = 1
    while size <= MAX_CHUNKS:
        rows = size * RUN_ALIGN

        @pl.when((n // size) % 2 == 1)
        def _(rows=rows):
            pltpu.make_async_copy(local.at[pl.ds(0, rows), :], glob.at[pl.ds(0, rows), :], sem).wait()

        size *= 2


def _lo_vector(lo_ref, tile, shape, axis):
    ids = lax.broadcasted_iota(jnp.int32, shape, axis)
    vec = jnp.zeros(shape, jnp.float32)
    for e in range(N_EXPERTS):
        vec = jnp.where(ids == e, lo_ref[tile * N_EXPERTS + e].astype(jnp.float32), vec)
    return vec


def _dispatch_kernel(lo_ref, nchunk_ref, dst_ref, pad_tile_ref, nused_ref, ri_ref, h_ref, xs_hbm,
                     upper, sbuf, zbuf, sem, zsem, *, n_tiles):
    i = pl.program_id(0)
    n_steps = pl.num_programs(0)
    slot = i % 2

    def zero_tile(tile):
        rows = pl.ds(pl.multiple_of(tile * ROW_TILE, ROW_TILE), ROW_TILE)
        return pltpu.make_async_copy(zbuf, xs_hbm.at[rows, :], zsem)

    @pl.when(i == 0)
    def _():
        r = lax.broadcasted_iota(jnp.int32, (TOKEN_TILE, TOKEN_TILE), 0)
        c = lax.broadcasted_iota(jnp.int32, (TOKEN_TILE, TOKEN_TILE), 1)
        upper[...] = jnp.where(r < c, 1.0, 0.0).astype(jnp.bfloat16)
        zbuf[...] = jnp.zeros_like(zbuf)
        for e in range(N_EXPERTS):
            @pl.when(pad_tile_ref[e] >= 0)
            def _():
                zero_tile(pad_tile_ref[e]).start()
        lax.fori_loop(nused_ref[0], n_tiles, lambda t, c: (zero_tile(t).start(), c)[1], 0)
        for e in range(N_EXPERTS):
            @pl.when(pad_tile_ref[e] >= 0)
            def _():
                zero_tile(pad_tile_ref[e]).wait()
        lax.fori_loop(nused_ref[0], n_tiles, lambda t, c: (zero_tile(t).wait(), c)[1], 0)

    e0 = ri_ref[0:1, :]
    e1 = ri_ref[1:2, :]
    ids = lax.broadcasted_iota(jnp.int32, (N_EXPERTS, TOKEN_TILE), 0)
    member = jnp.where((ids == e0) | (ids == e1), 1.0, 0.0)
    prior = _dot(member.astype(jnp.bfloat16), upper[...])
    lo_vec = _lo_vector(lo_ref, i, (N_EXPERTS, 1), 0)
    d0 = _local_slots(member, lo_vec, prior, e0, 0)
    d1 = _local_slots(member, lo_vec, prior, e1, 0)
    local = sbuf.at[slot]
    used_rows = nchunk_ref[i] * RUN_ALIGN
    h = h_ref[...]
    for blk in range(LOCAL_ROWS // SORT_BLOCK):
        @pl.when(blk * SORT_BLOCK < used_rows)
        def _():
            rows = blk * SORT_BLOCK + lax.broadcasted_iota(jnp.int32, (SORT_BLOCK, TOKEN_TILE), 0)
            select = jnp.where((rows == d0) | (rows == d1), 1.0, 0.0).astype(jnp.bfloat16)
            local[blk * SORT_BLOCK:(blk + 1) * SORT_BLOCK, :] = _dot(select, h).astype(jnp.bfloat16)

    @pl.when(i > 0)
    def _():
        _wait_chunks(nchunk_ref, i - 1, sbuf.at[1 - slot], xs_hbm, sem)

    _start_chunks(nchunk_ref, dst_ref, i, local, xs_hbm, sem, True)

    @pl.when(i == n_steps - 1)
    def _():
        _wait_chunks(nchunk_ref, i, local, xs_hbm, sem)


def _dispatch(lo, nchunk, dst, pad_tile, nused, route_i, h2, p_rows):
    n_tok = h2.shape[0]
    grid_spec = pltpu.PrefetchScalarGridSpec(
        num_scalar_prefetch=5,
        grid=(n_tok // TOKEN_TILE,),
        in_specs=[pl.BlockSpec((EXPERTS_PER_GROUP, TOKEN_TILE), lambda i, *_: (0, i)),
                  pl.BlockSpec((TOKEN_TILE, D_MODEL), lambda i, *_: (i, 0))],
        out_specs=pl.BlockSpec(memory_space=pl.ANY),
        scratch_shapes=[pltpu.VMEM((TOKEN_TILE, TOKEN_TILE), jnp.bfloat16),
                        pltpu.VMEM((2, LOCAL_ROWS, D_MODEL), jnp.bfloat16),
                        pltpu.VMEM((ROW_TILE, D_MODEL), jnp.bfloat16),
                        pltpu.SemaphoreType.DMA, pltpu.SemaphoreType.DMA],
    )
    return pl.pallas_call(
        functools.partial(_dispatch_kernel, n_tiles=p_rows // ROW_TILE),
        out_shape=jax.ShapeDtypeStruct((p_rows, D_MODEL), jnp.bfloat16),
        grid_spec=grid_spec,
        compiler_params=pltpu.CompilerParams(dimension_semantics=("arbitrary",),
                                             vmem_limit_bytes=VMEM_LIMIT),
        name="dispatch",
    )(lo, nchunk, dst, pad_tile, nused, route_i, h2)


def _moe_kernel(texp_ref, first_ref, nused_ref, slot_ref, next_ref, xs_ref, wg_hbm, wu_hbm, wd_hbm,
                y_ref, wg_f32, wu_f32, wd_f32, wg_bf, wu_bf, wd_bf, sems):
    j = pl.program_id(0)

    def fetch(expert, slot):
        return [pltpu.make_async_copy(src.at[expert], dst.at[slot], sems.at[slot, k])
                for k, (src, dst) in enumerate(((wg_hbm, wg_f32), (wu_hbm, wu_f32), (wd_hbm, wd_f32)))]

    @pl.when(j == 0)
    def _():
        for cp in fetch(texp_ref[0], 0):
            cp.start()

    @pl.when(j < nused_ref[0])
    def _():
        @pl.when(first_ref[j] == 1)
        def _():
            for s in range(2):
                @pl.when(slot_ref[j] == s)
                def _():
                    for cp in fetch(texp_ref[j], s):
                        cp.wait()
                    wg_bf[...] = wg_f32[s].astype(jnp.bfloat16)
                    wu_bf[...] = wu_f32[s].astype(jnp.bfloat16)
                    wd_bf[...] = wd_f32[s].astype(jnp.bfloat16)

                    @pl.when(next_ref[j] >= 0)
                    def _():
                        for cp in fetch(next_ref[j], 1 - s):
                            cp.start()

        x = xs_ref[...]
        a = _dot(x, wg_bf[...])
        u = _dot(x, wu_bf[...])
        hid = (a * (1.0 / (1.0 + jnp.exp(-a)))) * u
        y_ref[...] = _dot(hid.astype(jnp.bfloat16), wd_bf[...]).astype(jnp.bfloat16)

    @pl.when(j >= nused_ref[0])
    def _():
        y_ref[...] = jnp.zeros_like(y_ref)


def _moe(texp, first, nused, slot, nxt, xs, w_gate, w_up, w_down):
    p_rows = xs.shape[0]
    n_tiles = p_rows // ROW_TILE
    used = lambda j, nu: jnp.minimum(j, nu[0] - 1)
    grid_spec = pltpu.PrefetchScalarGridSpec(
        num_scalar_prefetch=5,
        grid=(n_tiles,),
        in_specs=[
            pl.BlockSpec((ROW_TILE, D_MODEL), lambda j, te, fi, nu, sl, nx: (used(j, nu), 0)),
            pl.BlockSpec(memory_space=pl.ANY),
            pl.BlockSpec(memory_space=pl.ANY),
            pl.BlockSpec(memory_space=pl.ANY),
        ],
        out_specs=pl.BlockSpec((ROW_TILE, D_MODEL), lambda j, te, fi, nu, sl, nx: (j, 0)),
        scratch_shapes=[pltpu.VMEM((2, D_MODEL, D_EXPERT), jnp.float32),
                        pltpu.VMEM((2, D_MODEL, D_EXPERT), jnp.float32),
                        pltpu.VMEM((2, D_EXPERT, D_MODEL), jnp.float32),
                        pltpu.VMEM((D_MODEL, D_EXPERT), jnp.bfloat16),
                        pltpu.VMEM((D_MODEL, D_EXPERT), jnp.bfloat16),
                        pltpu.VMEM((D_EXPERT, D_MODEL), jnp.bfloat16),
                        pltpu.SemaphoreType.DMA((2, 3))],
    )
    return pl.pallas_call(
        _moe_kernel,
        out_shape=jax.ShapeDtypeStruct((p_rows, D_MODEL), jnp.bfloat16),
        grid_spec=grid_spec,
        compiler_params=pltpu.CompilerParams(dimension_semantics=("arbitrary",),
                                             vmem_limit_bytes=VMEM_LIMIT),
        name="moe_experts",
    )(texp, first, nused, slot, nxt, xs, w_gate, w_up, w_down)


def _final_kernel(lo_ref, nchunk_ref, dst_ref, y_hbm, x1_ref, rc_ref, p_ref, gple_ref, wg_ref, wp_ref,
                  gfin_ref, out_ref, lower, ybuf, sems):
    i = pl.program_id(0)
    n_steps = pl.num_programs(0)
    slot = i % 2

    @pl.when(i == 0)
    def _():
        r = lax.broadcasted_iota(jnp.int32, (TOKEN_TILE, TOKEN_TILE), 0)
        c = lax.broadcasted_iota(jnp.int32, (TOKEN_TILE, TOKEN_TILE), 1)
        lower[...] = jnp.where(c < r, 1.0, 0.0).astype(jnp.bfloat16)
        ybuf[...] = jnp.zeros_like(ybuf)
        _start_chunks(nchunk_ref, dst_ref, 0, ybuf.at[0], y_hbm, sems.at[0], False)

    @pl.when(i + 1 < n_steps)
    def _():
        _start_chunks(nchunk_ref, dst_ref, i + 1, ybuf.at[1 - slot], y_hbm, sems.at[1 - slot], False)

    rc = rc_ref[...]
    w0, w1 = rc[:, 0:1], rc[:, 1:2]
    e0, e1 = rc[:, 2:3].astype(jnp.int32), rc[:, 3:4].astype(jnp.int32)
    ids = lax.broadcasted_iota(jnp.int32, (TOKEN_TILE, ROUTER_ROWS), 1)
    member = jnp.where((ids == e0) | (ids == e1), 1.0, 0.0)
    prior = _dot(lower[...], member.astype(jnp.bfloat16))
    lo_vec = _lo_vector(lo_ref, i, (1, ROUTER_ROWS), 1)
    d0 = _local_slots(member, lo_vec, prior, e0, 1)
    d1 = _local_slots(member, lo_vec, prior, e1, 1)

    local = ybuf.at[slot]
    _wait_chunks(nchunk_ref, i, local, y_hbm, sems.at[slot])
    for blk in range(TOKEN_TILE // FINAL_BLOCK):
        rows = slice(blk * FINAL_BLOCK, (blk + 1) * FINAL_BLOCK)
        cols = lax.broadcasted_iota(jnp.int32, (FINAL_BLOCK, LOCAL_ROWS), 1)
        pick = jnp.where(cols == d0[rows], w0[rows], jnp.where(cols == d1[rows], w1[rows], 0.0))
        x2 = x1_ref[rows, :] + _dot(pick.astype(jnp.bfloat16), local[...])
        z = _dot(_rms(x2, gple_ref[...]).astype(jnp.bfloat16), wg_ref[...])
        gate = 1.0 / (1.0 + jnp.exp(-z))
        x3 = x2 + _dot(p_ref[rows, :].astype(jnp.bfloat16), wp_ref[...]) * gate
        out_ref[rows, :] = _rms(x3, gfin_ref[...])


def _final(lo, nchunk, dst, y, x1, rc, p2d, g_ple, w_gate_bf, w_proj_bf, g_final):
    t = x1.shape[0]
    tm = TOKEN_TILE
    row = lambda w: pl.BlockSpec((tm, w), lambda i, *_: (i, 0))
    full = lambda a: pl.BlockSpec(a.shape, lambda i, *_: (0, 0))
    grid_spec = pltpu.PrefetchScalarGridSpec(
        num_scalar_prefetch=3,
        grid=(t // tm,),
        in_specs=[pl.BlockSpec(memory_space=pl.ANY), row(D_MODEL), row(EXPERTS_PER_GROUP),
                  row(PLE_DIM), full(g_ple), full(w_gate_bf), full(w_proj_bf), full(g_final)],
        out_specs=row(D_MODEL),
        scratch_shapes=[pltpu.VMEM((TOKEN_TILE, TOKEN_TILE), jnp.bfloat16),
                        pltpu.VMEM((2, LOCAL_ROWS, D_MODEL), jnp.bfloat16),
                        pltpu.SemaphoreType.DMA((2,))],
    )
    return pl.pallas_call(
        _final_kernel,
        out_shape=jax.ShapeDtypeStruct((t, D_MODEL), jnp.float32),
        grid_spec=grid_spec,
        compiler_params=pltpu.CompilerParams(dimension_semantics=("arbitrary",),
                                             vmem_limit_bytes=VMEM_LIMIT),
        name="combine_ple_final",
    )(lo, nchunk, dst, y, x1, rc, p2d, g_ple, w_gate_bf, w_proj_bf, g_final)


def _dispatch_tables(counts):
    n_tt = counts.shape[0]
    p_rows = 2 * n_tt * TOKEN_TILE + n_tt * N_EXPERTS * (RUN_ALIGN - 1) + N_EXPERTS * ROW_TILE
    p_rows = -(-p_rows // ROW_TILE) * ROW_TILE
    n_tiles = p_rows // ROW_TILE
    n8 = (counts + RUN_ALIGN - 1) // RUN_ALIGN * RUN_ALIGN
    lo = jnp.cumsum(n8, axis=1) - n8
    total = jnp.sum(n8, axis=0)
    tiles_per = (total + ROW_TILE - 1) // ROW_TILE
    tile_end = jnp.cumsum(tiles_per)
    tile_start = tile_end - tiles_per
    g = tile_start[None, :] * ROW_TILE + (jnp.cumsum(n8, axis=0) - n8)
    n_used = tile_end[-1]
    tile_ids = jnp.arange(n_tiles, dtype=jnp.int32)
    clamped = jnp.minimum(tile_ids, n_used - 1)
    texp = jnp.sum((tile_end[None, :] <= clamped[:, None]).astype(jnp.int32), axis=1)
    experts = jnp.arange(N_EXPERTS, dtype=jnp.int32)
    of_tile = texp[:, None] == experts[None, :]
    per_tile = lambda v: jnp.sum(jnp.where(of_tile, v[None, :], 0), axis=1)
    first = ((tile_ids == per_tile(tile_start)) & (tile_ids < n_used)).astype(jnp.int32)
    pad_tile = jnp.where(total > 0, tile_end - 1, -1).astype(jnp.int32)
    has_rows = total > 0
    later = has_rows[None, :] & (experts[None, :] > experts[:, None])
    next_expert = jnp.min(jnp.where(later, experts[None, :], N_EXPERTS), axis=1)
    next_expert = jnp.where(next_expert < N_EXPERTS, next_expert, -1)
    slot = (jnp.cumsum(has_rows.astype(jnp.int32)) - 1) % 2
    chunk_row = jnp.arange(MAX_CHUNKS, dtype=jnp.int32)[None, :, None] * RUN_ALIGN
    in_run = (chunk_row >= lo[:, None, :]) & (chunk_row < (lo + n8)[:, None, :])
    dst = jnp.sum(jnp.where(in_run, g[:, None, :] + chunk_row - lo[:, None, :], 0), axis=2)
    nchunk = jnp.sum(n8, axis=1) // RUN_ALIGN
    i32 = lambda a: a.reshape(-1).astype(jnp.int32)
    return (texp, first, n_used.reshape(1).astype(jnp.int32), i32(per_tile(slot)),
            i32(per_tile(next_expert)), pad_tile, i32(lo), i32(nchunk), i32(dst), p_rows)


def kernel(x, p, g_mix, w_in, sink, g_grp_a, g_grp_b, w_out, g_ffn, w_router_group, b_router_group,
           w_router_expert, b_router_expert, w_expert_gate, w_expert_up, w_expert_down, g_ple,
           w_ple_gate, w_ple_proj, g_final):
    batch, seq, _ = x.shape
    n_tok = batch * seq
    depth = w_in.shape[0]
    assert depth == 1, "the final RMSNorm is fused into the last layer's epilogue"
    bf = jnp.bfloat16
    xc = x.reshape(n_tok, D_MODEL)
    for i in range(depth):
        *qkv_a, qb, kb, vb = _proj(xc, g_mix[i][None, :], w_in[i].astype(bf), batch, seq)
        oa = _attn_a(qkv_a, batch, seq)
        ob = _attn_b(qb, kb, vb, sink[i], batch, seq)

        wr = jnp.zeros((D_MODEL, ROUTER_ROWS), jnp.float32)
        wr = wr.at[:, 0:N_GROUPS].set(w_router_group[i])
        wr = wr.at[:, EXPERTS_PER_GROUP:EXPERTS_PER_GROUP + N_EXPERTS].set(w_router_expert[i])
        br = jnp.zeros((ROUTER_ROWS, 1), jnp.float32)
        br = br.at[0:N_GROUPS, 0].set(b_router_group[i])
        br = br.at[EXPERTS_PER_GROUP:EXPERTS_PER_GROUP + N_EXPERTS, 0].set(b_router_expert[i])

        x1, h2, route_i, route_c, counts = _mix(
            oa, ob, xc, g_grp_a[i][None, :], g_grp_b[i][None, :], w_out[i].astype(bf), g_ffn[i][None, :], wr, br)

        texp, first, nused, slot, nxt, pad_tile, lo, nchunk, dst, p_rows = _dispatch_tables(counts[:, :, 0])
        xs = _dispatch(lo, nchunk, dst, pad_tile, nused, route_i, h2, p_rows)
        y = _moe(texp, first, nused, slot, nxt, xs, w_expert_gate[i], w_expert_up[i], w_expert_down[i])
        xc = _final(lo, nchunk, dst, y, x1, route_c, p[i].reshape(n_tok, PLE_DIM), g_ple[i][None, :],
                    w_ple_gate[i].astype(bf), w_ple_proj[i].astype(bf), g_final[None, :])
    return xc.reshape(batch, seq, D_MODEL)
```

```python
import functools

import jax
import jax.numpy as jnp
from jax import lax
from jax.experimental import pallas as pl
from jax.experimental.pallas import tpu as pltpu

D_MODEL = 1024
HEAD_DIM = 64
GROUP_WIDTH = 512
N_HEADS = 8
B_KV_HEADS = 2
B_KV_WIDTH = B_KV_HEADS * HEAD_DIM
A_CONFIGS = ((128, 1), (512, 4), (2048, 16))
B_HALF_WINDOW = 128
N_GROUPS = 4
EXPERTS_PER_GROUP = 8
N_EXPERTS = N_GROUPS * EXPERTS_PER_GROUP
D_EXPERT = 512
PLE_DIM = 256
EPS = 1e-6
NEG = -1e30
LOG2E = 1.4426950408889634

LANES = 128
Q_BLOCK = 128
HEADS_PER_PASS = 2
PASS_WIDTH = HEADS_PER_PASS * HEAD_DIM
ROW_TILE = 448
TOKEN_TILE = 512
ROUTER_ROWS = 128
VMEM_LIMIT = 48 * 1024 * 1024
ATTN_VMEM_LIMIT = 56 * 1024 * 1024


def _rms(x, g):
    r = lax.rsqrt(jnp.mean(x * x, axis=-1, keepdims=True) + EPS)
    return (x * r) * g


def _dot(a, b):
    return jnp.dot(a, b, preferred_element_type=jnp.float32)


PROJ_TILE = 1024


def _proj_kernel(x_ref, g_ref, w_ref, *refs):
    a_refs, (qb_ref, kb_ref, vb_ref, res, *stage_refs) = refs[:9], refs[9:]
    h = _rms(x_ref[...], g_ref[...]).astype(jnp.bfloat16)
    scale = HEAD_DIM ** -0.5 * LOG2E
    W = GROUP_WIDTH
    tiles = GROUP_WIDTH // PASS_WIDTH
    qkv_a = (_dot(h, w_ref[:, 0:W]) * scale, _dot(h, w_ref[:, W:2 * W]), _dot(h, w_ref[:, 2 * W:3 * W]))
    for part in range(3):
        for pair in range(tiles):
            res[part * tiles + pair, 0] = qkv_a[part][:, pair * PASS_WIDTH:(pair + 1) * PASS_WIDTH]
    qb_ref[...] = (_dot(h, w_ref[:, 3 * W:4 * W]) * scale).astype(jnp.bfloat16)
    kb_ref[...] = _dot(h, w_ref[:, 4 * W:4 * W + B_KV_WIDTH]).astype(jnp.bfloat16)
    vb_ref[...] = _dot(h, w_ref[:, 4 * W + B_KV_WIDTH:4 * W + 2 * B_KV_WIDTH]).astype(jnp.bfloat16)
    stages = (res,) + tuple(stage_refs)
    prev_dil = 1
    for c, (_, dil) in enumerate(A_CONFIGS):
        ratio = dil // prev_dil
        rows = PROJ_TILE // dil
        for idx in range(3 * tiles):
            part, pair = idx // tiles, idx % tiles
            out = a_refs[3 * c + part]
            for rho in range(dil):
                if c == 0:
                    sub = res[idx, 0]
                else:
                    sub = stages[c - 1].at[idx, rho % prev_dil][pl.ds(rho // prev_dil, rows, stride=ratio), :]
                    if c < len(A_CONFIGS) - 1:
                        stages[c][idx, rho] = sub
                out[pair, rho] = sub.astype(jnp.bfloat16)
        prev_dil = dil


def _proj(x2d, g_mix, w_in_bf, batch, seq):
    t = x2d.shape[0]
    tm = PROJ_TILE
    tiles_per_seq = seq // tm
    in_width = w_in_bf.shape[1]
    n_pairs = N_HEADS // HEADS_PER_PASS
    row = lambda w: pl.BlockSpec((tm, w), lambda i: (i, 0))
    a_shapes, a_specs = [], []
    for _, dil in A_CONFIGS:
        for _ in range(3):
            a_shapes.append(jax.ShapeDtypeStruct((batch, n_pairs, dil, seq // dil, PASS_WIDTH),
                                                 jnp.bfloat16))
            a_specs.append(pl.BlockSpec((None, n_pairs, dil, tm // dil, PASS_WIDTH),
                                        lambda i: (i // tiles_per_seq, 0, 0, i % tiles_per_seq, 0)))
    wide = jax.ShapeDtypeStruct((t, GROUP_WIDTH), jnp.bfloat16)
    narrow = jax.ShapeDtypeStruct((t, B_KV_WIDTH), jnp.bfloat16)
    return pl.pallas_call(
        _proj_kernel,
        out_shape=tuple(a_shapes) + (wide, narrow, narrow),
        grid=(t // tm,),
        in_specs=[row(D_MODEL),
                  pl.BlockSpec((1, D_MODEL), lambda i: (0, 0)),
                  pl.BlockSpec((D_MODEL, in_width), lambda i: (0, 0), pipeline_mode=pl.Buffered(1))],
        out_specs=tuple(a_specs) + (row(GROUP_WIDTH), row(B_KV_WIDTH), row(B_KV_WIDTH)),
        scratch_shapes=[pltpu.VMEM((3 * GROUP_WIDTH // PASS_WIDTH, dil, tm // dil, PASS_WIDTH), jnp.float32)
                        for _, dil in A_CONFIGS[:-1]],
        compiler_params=pltpu.CompilerParams(dimension_semantics=("arbitrary",),
                                             vmem_limit_bytes=ATTN_VMEM_LIMIT),
        name="proj",
    )(x2d, g_mix, w_in_bf)


def _fill_bias(bias_ref, half, nkeys, dist_scale):
    i = lax.broadcasted_iota(jnp.int32, (Q_BLOCK, nkeys), 0)
    j = lax.broadcasted_iota(jnp.int32, (Q_BLOCK, nkeys), 1)
    for var, rel0 in enumerate((0, -half, Q_BLOCK - nkeys)):
        d = jnp.abs(j + rel0 - i)
        dist = d.astype(jnp.float32) * dist_scale
        for h in range(N_HEADS):
            slope = 2.0 ** (-(h + 1))
            bias_ref[var, h] = jnp.where(d <= half, -(slope * dist) * LOG2E, NEG)


def _key_window(q0, seq_len, half, nkeys):
    lo = q0 - half
    ks = pl.multiple_of(jnp.clip(lo, 0, seq_len - nkeys), HEAD_DIM)
    var = jnp.where(lo < 0, 0, jnp.where(lo > seq_len - nkeys, 2, 1))
    return ks, var


def _fold_lane_tiles(a, op):
    out = a[..., 0:LANES]
    for t in range(1, a.shape[-1] // LANES):
        out = op(out, a[..., t * LANES:(t + 1) * LANES])
    return out


def _attend_pass(q, k, v, bias, sinks):
    nkeys = k.shape[0]
    lane_head = lax.broadcasted_iota(jnp.int32, (Q_BLOCK, PASS_WIDTH), 1) // HEAD_DIM
    zero = jnp.zeros_like(q)
    q_stack = jnp.concatenate(
        [jnp.where(lane_head == h, q, zero) for h in range(HEADS_PER_PASS)], axis=0)
    s = lax.dot_general(q_stack, k, (((1,), (1,)), ((), ())), preferred_element_type=jnp.float32)
    s = s.reshape(HEADS_PER_PASS, Q_BLOCK, nkeys) + bias
    m_tile = _fold_lane_tiles(s, jnp.maximum)
    if sinks is not None:
        tile_head = lax.broadcasted_iota(jnp.int32, m_tile.shape, 0)
        tile_lane = lax.broadcasted_iota(jnp.int32, m_tile.shape, 2)
        sk = jnp.zeros(m_tile.shape, jnp.float32)
        for h in range(HEADS_PER_PASS):
            sk = jnp.where(tile_head == h, sinks[h] * LOG2E, sk)
        m_tile = jnp.maximum(m_tile, sk)
    m = jnp.max(m_tile, axis=-1, keepdims=True)
    p = jnp.exp2(s - m)
    den_tile = _fold_lane_tiles(p, jnp.add)
    if sinks is not None:
        den_tile = den_tile + jnp.where(tile_lane == 0, jnp.exp2(sk - m), 0.0)
    den = jnp.sum(den_tile, axis=-1, keepdims=True)
    pv = _dot(p.reshape(HEADS_PER_PASS * Q_BLOCK, nkeys).astype(jnp.bfloat16), v)
    pv = pv.reshape(HEADS_PER_PASS, Q_BLOCK, PASS_WIDTH)
    shape = (Q_BLOCK, PASS_WIDTH)
    o, m_lanes, den_lanes = (jnp.zeros(shape, jnp.float32) for _ in range(3))
    for h in range(HEADS_PER_PASS):
        o = jnp.where(lane_head == h, pv[h], o)
        m_lanes = jnp.where(lane_head == h, m[h], m_lanes)
        den_lanes = jnp.where(lane_head == h, den[h], den_lanes)
    return o, m_lanes, den_lanes


A_HALF = 64
A_KEYS = Q_BLOCK + 2 * A_HALF
PASSES_PER_BODY = 16
MERGE_BLOCK = 256


def _attn_a_kernel(*refs, seq):
    n_cfg = len(A_CONFIGS)
    qkv = refs[:3 * n_cfg]
    o_ref, bias_ref = refs[3 * n_cfg], refs[3 * n_cfg + 1]
    scratch = refs[3 * n_cfg + 2:]
    pair = pl.program_id(1)

    @pl.when((pl.program_id(0) == 0) & (pair == 0))
    def _():
        for c, (_, dil) in enumerate(A_CONFIGS):
            _fill_bias(bias_ref.at[c], A_HALF, A_KEYS, float(dil))

    for c, (window, dil) in enumerate(A_CONFIGS):
        assert window // (2 * dil) == A_HALF
        q_ref, k_ref, v_ref = qkv[3 * c:3 * c + 3]
        o_sc, m_sc, d_sc = scratch[3 * c:3 * c + 3]
        sub_len = seq // dil
        blocks = sub_len // Q_BLOCK

        def body(idx, carry, q_ref=q_ref, k_ref=k_ref, v_ref=v_ref, o_sc=o_sc, m_sc=m_sc, d_sc=d_sc, c=c,
                 dil=dil, sub_len=sub_len, blocks=blocks):
            rho = idx // blocks
            q0 = pl.multiple_of((idx % blocks) * Q_BLOCK, Q_BLOCK)
            ks, var = _key_window(q0, sub_len, A_HALF, A_KEYS)
            bias = bias_ref[c, var, pl.ds(pair * HEADS_PER_PASS, HEADS_PER_PASS)]
            o, m, den = _attend_pass(q_ref[rho, pl.ds(q0, Q_BLOCK), :], k_ref[rho, pl.ds(ks, A_KEYS), :],
                                     v_ref[rho, pl.ds(ks, A_KEYS), :], bias, None)
            tokens = pl.ds(rho + dil * q0, Q_BLOCK, stride=dil) if dil > 1 else pl.ds(q0, Q_BLOCK)
            o_sc[tokens, :] = o
            m_sc[tokens, :] = m
            d_sc[tokens, :] = den
            return carry

        lax.fori_loop(0, dil * blocks, body, 0, unroll=PASSES_PER_BODY)

    n_cfg = len(A_CONFIGS)

    def merge(blk, carry):
        rows = pl.ds(pl.multiple_of(blk * MERGE_BLOCK, MERGE_BLOCK), MERGE_BLOCK)
        pvs = [scratch[3 * c][rows, :] for c in range(n_cfg)]
        ms = [scratch[3 * c + 1][rows, :] for c in range(n_cfg)]
        dens = [scratch[3 * c + 2][rows, :] for c in range(n_cfg)]
        mx = functools.reduce(jnp.maximum, ms)
        es = [jnp.exp2(m - mx) for m in ms]
        total = functools.reduce(jnp.add, [e * d for e, d in zip(es, dens)])
        o_ref[rows, :] = functools.reduce(jnp.add, [e * pv for e, pv in zip(es, pvs)]) * (1.0 / total)
        return carry

    lax.fori_loop(0, seq // MERGE_BLOCK, merge, 0, unroll=2)


def _attn_a(qkv, batch, seq):
    n_pairs = N_HEADS // HEADS_PER_PASS
    in_specs, scratch = [], [pltpu.VMEM((len(A_CONFIGS), 3, N_HEADS, Q_BLOCK, A_KEYS), jnp.float32)]
    for _, dil in A_CONFIGS:
        blk = (None, None, dil, seq // dil, PASS_WIDTH)
        in_specs += [pl.BlockSpec(blk, lambda b, p: (b, p, 0, 0, 0))] * 3
        scratch += [pltpu.VMEM((seq, PASS_WIDTH), jnp.float32)] * 3
    return pl.pallas_call(
        functools.partial(_attn_a_kernel, seq=seq),
        out_shape=jax.ShapeDtypeStruct((batch, seq, GROUP_WIDTH), jnp.float32),
        grid=(batch, n_pairs),
        in_specs=in_specs,
        out_specs=pl.BlockSpec((None, seq, PASS_WIDTH), lambda b, p: (b, 0, p)),
        scratch_shapes=scratch,
        compiler_params=pltpu.CompilerParams(dimension_semantics=("arbitrary", "arbitrary"),
                                             vmem_limit_bytes=ATTN_VMEM_LIMIT),
        name="attn_a",
    )(*qkv).reshape(batch * seq, GROUP_WIDTH)


def _attn_b_kernel(sink_ref, q_ref, k_ref, v_ref, o_ref, bias_ref, k4_ref, v4_ref, *, half, nkeys,
                   seq_len, q_tile):
    first = (pl.program_id(0) == 0) & (pl.program_id(1) == 0)

    @pl.when(first)
    def _():
        _fill_bias(bias_ref, half, nkeys, 1.0)

    @pl.when(pl.program_id(1) == 0)
    def _():
        c = lax.broadcasted_iota(jnp.int32, (B_KV_WIDTH, GROUP_WIDTH), 0)
        j = lax.broadcasted_iota(jnp.int32, (B_KV_WIDTH, GROUP_WIDTH), 1)
        src = (j // (GROUP_WIDTH // B_KV_HEADS)) * HEAD_DIM + j % HEAD_DIM
        rep = jnp.where(c == src, 1.0, 0.0).astype(jnp.bfloat16)
        chunk = 512

        def body(i, carry):
            rows = pl.ds(pl.multiple_of(i * chunk, chunk), chunk)
            k4_ref[rows, :] = _dot(k_ref[rows, :], rep).astype(jnp.bfloat16)
            v4_ref[rows, :] = _dot(v_ref[rows, :], rep).astype(jnp.bfloat16)
            return carry

        lax.fori_loop(0, seq_len // chunk, body, 0)

    base = pl.program_id(1) * q_tile

    def body(jb, carry):
        row0 = pl.multiple_of(jb * Q_BLOCK, Q_BLOCK)
        ks, var = _key_window(base + row0, seq_len, half, nkeys)
        for g in range(N_HEADS // HEADS_PER_PASS):
            cols = slice(g * PASS_WIDTH, (g + 1) * PASS_WIDTH)
            heads = pl.ds(g * HEADS_PER_PASS, HEADS_PER_PASS)
            sinks = [sink_ref[g * HEADS_PER_PASS + h] for h in range(HEADS_PER_PASS)]
            o, _, den = _attend_pass(q_ref[pl.ds(row0, Q_BLOCK), cols], k4_ref[pl.ds(ks, nkeys), cols],
                                     v4_ref[pl.ds(ks, nkeys), cols], bias_ref[var, heads], sinks)
            o_ref[pl.ds(row0, Q_BLOCK), cols] = (o * (1.0 / den)).astype(o_ref.dtype)
        return carry

    lax.fori_loop(0, q_tile // Q_BLOCK, body, 0,
                  unroll=PASSES_PER_BODY // (N_HEADS // HEADS_PER_PASS))


def _attn_b(q, k, v, sink, batch, seq):
    half = B_HALF_WINDOW
    nkeys = Q_BLOCK + 2 * half
    q_tile = 1024
    view = lambda a: a.reshape(batch, seq, a.shape[-1])
    kern = functools.partial(_attn_b_kernel, half=half, nkeys=nkeys, seq_len=seq, q_tile=q_tile)
    o = pl.pallas_call(
        kern,
        out_shape=jax.ShapeDtypeStruct((batch, seq, GROUP_WIDTH), jnp.bfloat16),
        grid=(batch, seq // q_tile),
        in_specs=[pl.BlockSpec(memory_space=pltpu.SMEM),
                  pl.BlockSpec((None, q_tile, GROUP_WIDTH), lambda b, i: (b, i, 0)),
                  pl.BlockSpec((None, seq, B_KV_WIDTH), lambda b, i: (b, 0, 0)),
                  pl.BlockSpec((None, seq, B_KV_WIDTH), lambda b, i: (b, 0, 0))],
        out_specs=pl.BlockSpec((None, q_tile, GROUP_WIDTH), lambda b, i: (b, i, 0)),
        scratch_shapes=[pltpu.VMEM((3, N_HEADS, Q_BLOCK, nkeys), jnp.float32),
                        pltpu.VMEM((seq, GROUP_WIDTH), jnp.bfloat16),
                        pltpu.VMEM((seq, GROUP_WIDTH), jnp.bfloat16)],
        compiler_params=pltpu.CompilerParams(dimension_semantics=("arbitrary", "arbitrary"),
                                             vmem_limit_bytes=VMEM_LIMIT),
        name="attn_b",
    )(sink, view(q), view(k), view(v))
    return o.reshape(batch * seq, GROUP_WIDTH)


def _mix_kernel(oa_ref, ob_ref, x_ref, ga_ref, gb_ref, wout_ref, gffn_ref, wr_ref, br_ref,
                x1_ref, h2_ref, ri_ref, rc_ref, cnt_ref):
    tm = x_ref.shape[0]
    na = _rms(oa_ref[...], ga_ref[...]).astype(jnp.bfloat16)
    nb = _rms(ob_ref[...].astype(jnp.float32), gb_ref[...]).astype(jnp.bfloat16)
    x1 = x_ref[...] + _dot(jnp.concatenate([na, nb], axis=1), wout_ref[...])
    x1_ref[...] = x1
    h2 = _rms(x1, gffn_ref[...])
    h2_ref[...] = h2.astype(jnp.bfloat16)

    h_hi = h2.astype(jnp.bfloat16)
    h_lo = (h2 - h_hi.astype(jnp.float32)).astype(jnp.bfloat16)
    wr = wr_ref[...]
    w_hi = wr.astype(jnp.bfloat16)
    w_lo = (wr - w_hi.astype(jnp.float32)).astype(jnp.bfloat16)
    w_cat = jnp.concatenate([w_hi, w_lo], axis=0)
    nt = (((1,), (1,)), ((), ()))
    lg2 = (lax.dot_general(w_cat, h_hi, nt, preferred_element_type=jnp.float32)
           + lax.dot_general(w_cat, h_lo, nt, preferred_element_type=jnp.float32))
    lgt = lg2[0:ROUTER_ROWS] + lg2[ROUTER_ROWS:2 * ROUTER_ROWS] + br_ref[...]

    rows = lax.broadcasted_iota(jnp.int32, (EXPERTS_PER_GROUP, tm), 0)
    big = jnp.int32(1 << 20)
    glog = jnp.where(rows < N_GROUPS, lgt[0:EXPERTS_PER_GROUP], -jnp.inf)
    gmax = jnp.max(glog, axis=0, keepdims=True)
    gsel = jnp.min(jnp.where(glog == gmax, rows, big), axis=0, keepdims=True)
    gw = 1.0 / jnp.sum(jnp.exp(glog - gmax), axis=0, keepdims=True)
    esel = jnp.zeros((EXPERTS_PER_GROUP, tm), jnp.float32)
    for grp in range(N_GROUPS):
        lo = EXPERTS_PER_GROUP * (grp + 1)
        esel = jnp.where(gsel == grp, lgt[lo:lo + EXPERTS_PER_GROUP], esel)
    v0 = jnp.max(esel, axis=0, keepdims=True)
    i0 = jnp.min(jnp.where(esel == v0, rows, big), axis=0, keepdims=True)
    rest = jnp.where(rows == i0, -jnp.inf, esel)
    v1 = jnp.max(rest, axis=0, keepdims=True)
    i1 = jnp.min(jnp.where(rest == v1, rows, big), axis=0, keepdims=True)
    e = jnp.exp(v1 - v0)
    w0 = (1.0 / (1.0 + e)) * gw
    w1 = (e / (1.0 + e)) * gw
    eid0 = gsel * EXPERTS_PER_GROUP + i0
    eid1 = gsel * EXPERTS_PER_GROUP + i1
    ri_ref[...] = jnp.where(rows == 0, eid0, jnp.where(rows == 1, eid1, 0))
    rows_t = lax.broadcasted_iota(jnp.int32, (ROUTER_ROWS, tm), 0)
    slab = jnp.where(rows_t == 0, w0, jnp.where(rows_t == 1, w1, 0.0))
    slab = jnp.where(rows_t == 2, eid0.astype(jnp.float32),
                     jnp.where(rows_t == 3, eid1.astype(jnp.float32), slab))
    rc_ref[...] = slab.T[:, 0:EXPERTS_PER_GROUP]
    ids = lax.broadcasted_iota(jnp.int32, (N_EXPERTS, tm), 0)
    member = jnp.where((ids == eid0) | (ids == eid1), 1.0, 0.0)
    cnt_ref[...] = jnp.sum(member, axis=1, keepdims=True).astype(jnp.int32)


def _mix(oa, ob, x2d, g_a, g_b, w_out_bf, g_ffn, wr, br):
    t = x2d.shape[0]
    tm = TOKEN_TILE
    row = lambda w: pl.BlockSpec((tm, w), lambda i: (i, 0))
    full = lambda a: pl.BlockSpec(a.shape, lambda i: (0, 0))
    return pl.pallas_call(
        _mix_kernel,
        out_shape=(jax.ShapeDtypeStruct((t, D_MODEL), jnp.float32),
                   jax.ShapeDtypeStruct((t, D_MODEL), jnp.bfloat16),
                   jax.ShapeDtypeStruct((EXPERTS_PER_GROUP, t), jnp.int32),
                   jax.ShapeDtypeStruct((t, EXPERTS_PER_GROUP), jnp.float32),
                   jax.ShapeDtypeStruct((t // tm, N_EXPERTS, 1), jnp.int32)),
        grid=(t // tm,),
        in_specs=[row(GROUP_WIDTH), row(GROUP_WIDTH), row(D_MODEL),
                  full(g_a), full(g_b), full(w_out_bf), full(g_ffn), full(wr), full(br)],
        out_specs=(row(D_MODEL), row(D_MODEL),
                   pl.BlockSpec((EXPERTS_PER_GROUP, tm), lambda i: (0, i)),
                   row(EXPERTS_PER_GROUP),
                   pl.BlockSpec((None, N_EXPERTS, 1), lambda i: (i, 0, 0))),
        compiler_params=pltpu.CompilerParams(dimension_semantics=("arbitrary",),
                                             vmem_limit_bytes=VMEM_LIMIT),
        name="mix_router",
    )(oa, ob, x2d, g_a, g_b, w_out_bf, g_ffn, wr, br)


RUN_ALIGN = 16
LOCAL_ROWS = 2 * TOKEN_TILE + N_EXPERTS * RUN_ALIGN
MAX_CHUNKS = LOCAL_ROWS // RUN_ALIGN
START_UNROLL = 4
SORT_BLOCK = 256
FINAL_BLOCK = 256


def _local_slots(member, lo_vec, prior, e_sel, axis):
    ids = lax.broadcasted_iota(jnp.int32, member.shape, axis)
    slot = jnp.where(ids == e_sel, lo_vec + prior, 0.0)
    return jnp.sum(slot, axis=axis, keepdims=True).astype(jnp.int32)


def _chunk_copy(dst_ref, tile, c, local, glob, sem, to_global):
    g = pl.multiple_of(dst_ref[tile * MAX_CHUNKS + c], RUN_ALIGN)
    l = pl.multiple_of(c * RUN_ALIGN, RUN_ALIGN)
    src, dst = local.at[pl.ds(l, RUN_ALIGN), :], glob.at[pl.ds(g, RUN_ALIGN), :]
    if not to_global:
        src, dst = dst, src
    return pltpu.make_async_copy(src, dst, sem)


def _start_chunks(nchunk_ref, dst_ref, tile, local, glob, sem, to_global):
    n = nchunk_ref[tile]
    groups = n // START_UNROLL

    def group(gi, carry):
        for u in range(START_UNROLL):
            _chunk_copy(dst_ref, tile, gi * START_UNROLL + u, local, glob, sem, to_global).start()
        return carry

    def single(c, carry):
        _chunk_copy(dst_ref, tile, groups * START_UNROLL + c, local, glob, sem, to_global).start()
        return carry

    lax.fori_loop(0, groups, group, 0)
    lax.fori_loop(0, n % START_UNROLL, single, 0)


def _wait_chunks(nchunk_ref, tile, local, glob, sem):
    n = nchunk_ref[tile]
    size = 1
    while size <= MAX_CHUNKS:
        rows = size * RUN_ALIGN

        @pl.when((n // size) % 2 == 1)
        def _(rows=rows):
            pltpu.make_async_copy(local.at[pl.ds(0, rows), :], glob.at[pl.ds(0, rows), :], sem).wait()

        size *= 2


def _lo_vector(lo_ref, tile, shape, axis):
    ids = lax.broadcasted_iota(jnp.int32, shape, axis)
    vec = jnp.zeros(shape, jnp.float32)
    for e in range(N_EXPERTS):
        vec = jnp.where(ids == e, lo_ref[tile * N_EXPERTS + e].astype(jnp.float32), vec)
    return vec


def _dispatch_kernel(lo_ref, nchunk_ref, dst_ref, pad_tile_ref, nused_ref, ri_ref, h_ref, xs_hbm,
                     upper, sbuf, zbuf, sem, zsem, *, n_tiles):
    i = pl.program_id(0)
    n_steps = pl.num_programs(0)
    slot = i % 2

    def zero_tile(tile):
        rows = pl.ds(pl.multiple_of(tile * ROW_TILE, ROW_TILE), ROW_TILE)
        return pltpu.make_async_copy(zbuf, xs_hbm.at[rows, :], zsem)

    @pl.when(i == 0)
    def _():
        r = lax.broadcasted_iota(jnp.int32, (TOKEN_TILE, TOKEN_TILE), 0)
        c = lax.broadcasted_iota(jnp.int32, (TOKEN_TILE, TOKEN_TILE), 1)
        upper[...] = jnp.where(r < c, 1.0, 0.0).astype(jnp.bfloat16)
        zbuf[...] = jnp.zeros_like(zbuf)
        for e in range(N_EXPERTS):
            @pl.when(pad_tile_ref[e] >= 0)
            def _():
                zero_tile(pad_tile_ref[e]).start()
        lax.fori_loop(nused_ref[0], n_tiles, lambda t, c: (zero_tile(t).start(), c)[1], 0)
        for e in range(N_EXPERTS):
            @pl.when(pad_tile_ref[e] >= 0)
            def _():
                zero_tile(pad_tile_ref[e]).wait()
        lax.fori_loop(nused_ref[0], n_tiles, lambda t, c: (zero_tile(t).wait(), c)[1], 0)

    e0 = ri_ref[0:1, :]
    e1 = ri_ref[1:2, :]
    ids = lax.broadcasted_iota(jnp.int32, (N_EXPERTS, TOKEN_TILE), 0)
    member = jnp.where((ids == e0) | (ids == e1), 1.0, 0.0)
    prior = _dot(member.astype(jnp.bfloat16), upper[...])
    lo_vec = _lo_vector(lo_ref, i, (N_EXPERTS, 1), 0)
    d0 = _local_slots(member, lo_vec, prior, e0, 0)
    d1 = _local_slots(member, lo_vec, prior, e1, 0)
    local = sbuf.at[slot]
    used_rows = nchunk_ref[i] * RUN_ALIGN
    h = h_ref[...]
    for blk in range(LOCAL_ROWS // SORT_BLOCK):
        @pl.when(blk * SORT_BLOCK < used_rows)
        def _():
            rows = blk * SORT_BLOCK + lax.broadcasted_iota(jnp.int32, (SORT_BLOCK, TOKEN_TILE), 0)
            select = jnp.where((rows == d0) | (rows == d1), 1.0, 0.0).astype(jnp.bfloat16)
            local[blk * SORT_BLOCK:(blk + 1) * SORT_BLOCK, :] = _dot(select, h).astype(jnp.bfloat16)

    @pl.when(i > 0)
    def _():
        _wait_chunks(nchunk_ref, i - 1, sbuf.at[1 - slot], xs_hbm, sem)

    _start_chunks(nchunk_ref, dst_ref, i, local, xs_hbm, sem, True)

    @pl.when(i == n_steps - 1)
    def _():
        _wait_chunks(nchunk_ref, i, local, xs_hbm, sem)


def _dispatch(lo, nchunk, dst, pad_tile, nused, route_i, h2, p_rows):
    n_tok = h2.shape[0]
    grid_spec = pltpu.PrefetchScalarGridSpec(
        num_scalar_prefetch=5,
        grid=(n_tok // TOKEN_TILE,),
        in_specs=[pl.BlockSpec((EXPERTS_PER_GROUP, TOKEN_TILE), lambda i, *_: (0, i)),
                  pl.BlockSpec((TOKEN_TILE, D_MODEL), lambda i, *_: (i, 0))],
        out_specs=pl.BlockSpec(memory_space=pl.ANY),
        scratch_shapes=[pltpu.VMEM((TOKEN_TILE, TOKEN_TILE), jnp.bfloat16),
                        pltpu.VMEM((2, LOCAL_ROWS, D_MODEL), jnp.bfloat16),
                        pltpu.VMEM((ROW_TILE, D_MODEL), jnp.bfloat16),
                        pltpu.SemaphoreType.DMA, pltpu.SemaphoreType.DMA],
    )
    return pl.pallas_call(
        functools.partial(_dispatch_kernel, n_tiles=p_rows // ROW_TILE),
        out_shape=jax.ShapeDtypeStruct((p_rows, D_MODEL), jnp.bfloat16),
        grid_spec=grid_spec,
        compiler_params=pltpu.CompilerParams(dimension_semantics=("arbitrary",),
                                             vmem_limit_bytes=VMEM_LIMIT),
        name="dispatch",
    )(lo, nchunk, dst, pad_tile, nused, route_i, h2)


def _moe_kernel(texp_ref, first_ref, nused_ref, slot_ref, next_ref, xs_ref, wg_hbm, wu_hbm, wd_hbm,
                y_ref, wg_f32, wu_f32, wd_f32, wg_bf, wu_bf, wd_bf, sems):
    j = pl.program_id(0)

    def fetch(expert, slot):
        return [pltpu.make_async_copy(src.at[expert], dst.at[slot], sems.at[slot, k])
                for k, (src, dst) in enumerate(((wg_hbm, wg_f32), (wu_hbm, wu_f32), (wd_hbm, wd_f32)))]

    @pl.when(j == 0)
    def _():
        for cp in fetch(texp_ref[0], 0):
            cp.start()

    @pl.when(j < nused_ref[0])
    def _():
        @pl.when(first_ref[j] == 1)
        def _():
            for s in range(2):
                @pl.when(slot_ref[j] == s)
                def _():
                    for cp in fetch(texp_ref[j], s):
                        cp.wait()
                    wg_bf[...] = wg_f32[s].astype(jnp.bfloat16)
                    wu_bf[...] = wu_f32[s].astype(jnp.bfloat16)
                    wd_bf[...] = wd_f32[s].astype(jnp.bfloat16)

                    @pl.when(next_ref[j] >= 0)
                    def _():
                        for cp in fetch(next_ref[j], 1 - s):
                            cp.start()

        x = xs_ref[...]
        a = _dot(x, wg_bf[...])
        u = _dot(x, wu_bf[...])
        hid = (a * (1.0 / (1.0 + jnp.exp(-a)))) * u
        y_ref[...] = _dot(hid.astype(jnp.bfloat16), wd_bf[...]).astype(jnp.bfloat16)

    @pl.when(j >= nused_ref[0])
    def _():
        y_ref[...] = jnp.zeros_like(y_ref)


def _moe(texp, first, nused, slot, nxt, xs, w_gate, w_up, w_down):
    p_rows = xs.shape[0]
    n_tiles = p_rows // ROW_TILE
    used = lambda j, nu: jnp.minimum(j, nu[0] - 1)
    grid_spec = pltpu.PrefetchScalarGridSpec(
        num_scalar_prefetch=5,
        grid=(n_tiles,),
        in_specs=[
            pl.BlockSpec((ROW_TILE, D_MODEL), lambda j, te, fi, nu, sl, nx: (used(j, nu), 0)),
            pl.BlockSpec(memory_space=pl.ANY),
            pl.BlockSpec(memory_space=pl.ANY),
            pl.BlockSpec(memory_space=pl.ANY),
        ],
        out_specs=pl.BlockSpec((ROW_TILE, D_MODEL), lambda j, te, fi, nu, sl, nx: (j, 0)),
        scratch_shapes=[pltpu.VMEM((2, D_MODEL, D_EXPERT), jnp.float32),
                        pltpu.VMEM((2, D_MODEL, D_EXPERT), jnp.float32),
                        pltpu.VMEM((2, D_EXPERT, D_MODEL), jnp.float32),
                        pltpu.VMEM((D_MODEL, D_EXPERT), jnp.bfloat16),
                        pltpu.VMEM((D_MODEL, D_EXPERT), jnp.bfloat16),
                        pltpu.VMEM((D_EXPERT, D_MODEL), jnp.bfloat16),
                        pltpu.SemaphoreType.DMA((2, 3))],
    )
    return pl.pallas_call(
        _moe_kernel,
        out_shape=jax.ShapeDtypeStruct((p_rows, D_MODEL), jnp.bfloat16),
        grid_spec=grid_spec,
        compiler_params=pltpu.CompilerParams(dimension_semantics=("arbitrary",),
                                             vmem_limit_bytes=VMEM_LIMIT),
        name="moe_experts",
    )(texp, first, nused, slot, nxt, xs, w_gate, w_up, w_down)


def _final_kernel(lo_ref, nchunk_ref, dst_ref, y_hbm, x1_ref, rc_ref, p_ref, gple_ref, wg_ref, wp_ref,
                  gfin_ref, out_ref, lower, ybuf, sems):
    i = pl.program_id(0)
    n_steps = pl.num_programs(0)
    slot = i % 2

    @pl.when(i == 0)
    def _():
        r = lax.broadcasted_iota(jnp.int32, (TOKEN_TILE, TOKEN_TILE), 0)
        c = lax.broadcasted_iota(jnp.int32, (TOKEN_TILE, TOKEN_TILE), 1)
        lower[...] = jnp.where(c < r, 1.0, 0.0).astype(jnp.bfloat16)
        ybuf[...] = jnp.zeros_like(ybuf)
        _start_chunks(nchunk_ref, dst_ref, 0, ybuf.at[0], y_hbm, sems.at[0], False)

    @pl.when(i + 1 < n_steps)
    def _():
        _start_chunks(nchunk_ref, dst_ref, i + 1, ybuf.at[1 - slot], y_hbm, sems.at[1 - slot], False)

    rc = rc_ref[...]
    w0, w1 = rc[:, 0:1], rc[:, 1:2]
    e0, e1 = rc[:, 2:3].astype(jnp.int32), rc[:, 3:4].astype(jnp.int32)
    ids = lax.broadcasted_iota(jnp.int32, (TOKEN_TILE, ROUTER_ROWS), 1)
    member = jnp.where((ids == e0) | (ids == e1), 1.0, 0.0)
    prior = _dot(lower[...], member.astype(jnp.bfloat16))
    lo_vec = _lo_vector(lo_ref, i, (1, ROUTER_ROWS), 1)
    d0 = _local_slots(member, lo_vec, prior, e0, 1)
    d1 = _local_slots(member, lo_vec, prior, e1, 1)

    local = ybuf.at[slot]
    _wait_chunks(nchunk_ref, i, local, y_hbm, sems.at[slot])
    for blk in range(TOKEN_TILE // FINAL_BLOCK):
        rows = slice(blk * FINAL_BLOCK, (blk + 1) * FINAL_BLOCK)
        cols = lax.broadcasted_iota(jnp.int32, (FINAL_BLOCK, LOCAL_ROWS), 1)
        pick = jnp.where(cols == d0[rows], w0[rows], jnp.where(cols == d1[rows], w1[rows], 0.0))
        x2 = x1_ref[rows, :] + _dot(pick.astype(jnp.bfloat16), local[...])
        z = _dot(_rms(x2, gple_ref[...]).astype(jnp.bfloat16), wg_ref[...])
        gate = 1.0 / (1.0 + jnp.exp(-z))
        x3 = x2 + _dot(p_ref[rows, :].astype(jnp.bfloat16), wp_ref[...]) * gate
        out_ref[rows, :] = _rms(x3, gfin_ref[...])


def _final(lo, nchunk, dst, y, x1, rc, p2d, g_ple, w_gate_bf, w_proj_bf, g_final):
    t = x1.shape[0]
    tm = TOKEN_TILE
    row = lambda w: pl.BlockSpec((tm, w), lambda i, *_: (i, 0))
    full = lambda a: pl.BlockSpec(a.shape, lambda i, *_: (0, 0))
    grid_spec = pltpu.PrefetchScalarGridSpec(
        num_scalar_prefetch=3,
        grid=(t // tm,),
        in_specs=[pl.BlockSpec(memory_space=pl.ANY), row(D_MODEL), row(EXPERTS_PER_GROUP),
                  row(PLE_DIM), full(g_ple), full(w_gate_bf), full(w_proj_bf), full(g_final)],
        out_specs=row(D_MODEL),
        scratch_shapes=[pltpu.VMEM((TOKEN_TILE, TOKEN_TILE), jnp.bfloat16),
                        pltpu.VMEM((2, LOCAL_ROWS, D_MODEL), jnp.bfloat16),
                        pltpu.SemaphoreType.DMA((2,))],
    )
    return pl.pallas_call(
        _final_kernel,
        out_shape=jax.ShapeDtypeStruct((t, D_MODEL), jnp.float32),
        grid_spec=grid_spec,
        compiler_params=pltpu.CompilerParams(dimension_semantics=("arbitrary",),
                                             vmem_limit_bytes=VMEM_LIMIT),
        name="combine_ple_final",
    )(lo, nchunk, dst, y, x1, rc, p2d, g_ple, w_gate_bf, w_proj_bf, g_final)


def _dispatch_tables(counts):
    n_tt = counts.shape[0]
    p_rows = 2 * n_tt * TOKEN_TILE + n_tt * N_EXPERTS * (RUN_ALIGN - 1) + N_EXPERTS * ROW_TILE
    p_rows = -(-p_rows // ROW_TILE) * ROW_TILE
    n_tiles = p_rows // ROW_TILE
    n8 = (counts + RUN_ALIGN - 1) // RUN_ALIGN * RUN_ALIGN
    lo = jnp.cumsum(n8, axis=1) - n8
    total = jnp.sum(n8, axis=0)
    tiles_per = (total + ROW_TILE - 1) // ROW_TILE
    tile_end = jnp.cumsum(tiles_per)
    tile_start = tile_end - tiles_per
    g = tile_start[None, :] * ROW_TILE + (jnp.cumsum(n8, axis=0) - n8)
    n_used = tile_end[-1]
    tile_ids = jnp.arange(n_tiles, dtype=jnp.int32)
    clamped = jnp.minimum(tile_ids, n_used - 1)
    texp = jnp.sum((tile_end[None, :] <= clamped[:, None]).astype(jnp.int32), axis=1)
    experts = jnp.arange(N_EXPERTS, dtype=jnp.int32)
    of_tile = texp[:, None] == experts[None, :]
    per_tile = lambda v: jnp.sum(jnp.where(of_tile, v[None, :], 0), axis=1)
    first = ((tile_ids == per_tile(tile_start)) & (tile_ids < n_used)).astype(jnp.int32)
    pad_tile = jnp.where(total > 0, tile_end - 1, -1).astype(jnp.int32)
    has_rows = total > 0
    later = has_rows[None, :] & (experts[None, :] > experts[:, None])
    next_expert = jnp.min(jnp.where(later, experts[None, :], N_EXPERTS), axis=1)
    next_expert = jnp.where(next_expert < N_EXPERTS, next_expert, -1)
    slot = (jnp.cumsum(has_rows.astype(jnp.int32)) - 1) % 2
    chunk_row = jnp.arange(MAX_CHUNKS, dtype=jnp.int32)[None, :, None] * RUN_ALIGN
    in_run = (chunk_row >= lo[:, None, :]) & (chunk_row < (lo + n8)[:, None, :])
    dst = jnp.sum(jnp.where(in_run, g[:, None, :] + chunk_row - lo[:, None, :], 0), axis=2)
    nchunk = jnp.sum(n8, axis=1) // RUN_ALIGN
    i32 = lambda a: a.reshape(-1).astype(jnp.int32)
    return (texp, first, n_used.reshape(1).astype(jnp.int32), i32(per_tile(slot)),
            i32(per_tile(next_expert)), pad_tile, i32(lo), i32(nchunk), i32(dst), p_rows)


def kernel(x, p, g_mix, w_in, sink, g_grp_a, g_grp_b, w_out, g_ffn, w_router_group, b_router_group,
           w_router_expert, b_router_expert, w_expert_gate, w_expert_up, w_expert_down, g_ple,
           w_ple_gate, w_ple_proj, g_final):
    batch, seq, _ = x.shape
    n_tok = batch * seq
    depth = w_in.shape[0]
    assert depth == 1, "the final RMSNorm is fused into the last layer's epilogue"
    bf = jnp.bfloat16
    xc = x.reshape(n_tok, D_MODEL)
    for i in range(depth):
        *qkv_a, qb, kb, vb = _proj(xc, g_mix[i][None, :], w_in[i].astype(bf), batch, seq)
        oa = _attn_a(qkv_a, batch, seq)
        ob = _attn_b(qb, kb, vb, sink[i], batch, seq)

        wr = jnp.zeros((ROUTER_ROWS, D_MODEL), jnp.float32)
        wr = wr.at[0:N_GROUPS].set(w_router_group[i].T)
        wr = wr.at[EXPERTS_PER_GROUP:EXPERTS_PER_GROUP + N_EXPERTS].set(w_router_expert[i].T)
        br = jnp.zeros((ROUTER_ROWS, 1), jnp.float32)
        br = br.at[0:N_GROUPS, 0].set(b_router_group[i])
        br = br.at[EXPERTS_PER_GROUP:EXPERTS_PER_GROUP + N_EXPERTS, 0].set(b_router_expert[i])

        x1, h2, route_i, route_c, counts = _mix(
            oa, ob, xc, g_grp_a[i][None, :], g_grp_b[i][None, :], w_out[i].astype(bf), g_ffn[i][None, :], wr, br)

        texp, first, nused, slot, nxt, pad_tile, lo, nchunk, dst, p_rows = _dispatch_tables(counts[:, :, 0])
        xs = _dispatch(lo, nchunk, dst, pad_tile, nused, route_i, h2, p_rows)
        y = _moe(texp, first, nused, slot, nxt, xs, w_expert_gate[i], w_expert_up[i], w_expert_down[i])
        xc = _final(lo, nchunk, dst, y, x1, route_c, p[i].reshape(n_tok, PLE_DIM), g_ple[i][None, :],
                    w_ple_gate[i].astype(bf), w_ple_proj[i].astype(bf), g_final[None, :])
    return xc.reshape(batch, seq, D_MODEL)
```

```python
import functools

import jax
import jax.numpy as jnp
from jax import lax
from jax.experimental import pallas as pl
from jax.experimental.pallas import tpu as pltpu

D_MODEL = 1024
HEAD_DIM = 64
GROUP_WIDTH = 512
N_HEADS = 8
B_KV_HEADS = 2
B_KV_WIDTH = B_KV_HEADS * HEAD_DIM
A_CONFIGS = ((128, 1), (512, 4), (2048, 16))
B_HALF_WINDOW = 128
N_GROUPS = 4
EXPERTS_PER_GROUP = 8
N_EXPERTS = N_GROUPS * EXPERTS_PER_GROUP
D_EXPERT = 512
PLE_DIM = 256
EPS = 1e-6
NEG = -1e30
LOG2E = 1.4426950408889634

LANES = 128
Q_BLOCK = 128
HEADS_PER_PASS = 2
PASS_WIDTH = HEADS_PER_PASS * HEAD_DIM
ROW_TILE = 448
TOKEN_TILE = 512
ROUTER_ROWS = 128
VMEM_LIMIT = 48 * 1024 * 1024
ATTN_VMEM_LIMIT = 56 * 1024 * 1024


def _rms(x, g):
    r = lax.rsqrt(jnp.mean(x * x, axis=-1, keepdims=True) + EPS)
    return (x * r) * g


def _dot(a, b):
    return jnp.dot(a, b, preferred_element_type=jnp.float32)


PROJ_TILE = 1024


def _proj_kernel(x_ref, g_ref, w_ref, *refs):
    a_refs, (qb_ref, kb_ref, vb_ref, res, *stage_refs) = refs[:9], refs[9:]
    h = _rms(x_ref[...], g_ref[...]).astype(jnp.bfloat16)
    scale = HEAD_DIM ** -0.5 * LOG2E
    W = GROUP_WIDTH
    tiles = GROUP_WIDTH // PASS_WIDTH
    qkv_a = (_dot(h, w_ref[:, 0:W]) * scale, _dot(h, w_ref[:, W:2 * W]), _dot(h, w_ref[:, 2 * W:3 * W]))
    for part in range(3):
        for pair in range(tiles):
            res[part * tiles + pair, 0] = qkv_a[part][:, pair * PASS_WIDTH:(pair + 1) * PASS_WIDTH]
    qkv_b = _dot(h, w_ref[:, 3 * W:4 * W + 2 * B_KV_WIDTH])
    qb_ref[...] = (qkv_b[:, 0:W] * scale).astype(jnp.bfloat16)
    kb_ref[...] = qkv_b[:, W:W + B_KV_WIDTH].astype(jnp.bfloat16)
    vb_ref[...] = qkv_b[:, W + B_KV_WIDTH:W + 2 * B_KV_WIDTH].astype(jnp.bfloat16)
    stages = (res,) + tuple(stage_refs)
    prev_dil = 1
    for c, (_, dil) in enumerate(A_CONFIGS):
        ratio = dil // prev_dil
        rows = PROJ_TILE // dil
        for idx in range(3 * tiles):
            part, pair = idx // tiles, idx % tiles
            out = a_refs[3 * c + part]
            for rho in range(dil):
                if c == 0:
                    sub = res[idx, 0]
                else:
                    sub = stages[c - 1].at[idx, rho % prev_dil][pl.ds(rho // prev_dil, rows, stride=ratio), :]
                    if c < len(A_CONFIGS) - 1:
                        stages[c][idx, rho] = sub
                out[pair, rho] = sub.astype(jnp.bfloat16)
        prev_dil = dil


def _proj(x2d, g_mix, w_in_bf, batch, seq):
    t = x2d.shape[0]
    tm = PROJ_TILE
    tiles_per_seq = seq // tm
    in_width = w_in_bf.shape[1]
    n_pairs = N_HEADS // HEADS_PER_PASS
    row = lambda w: pl.BlockSpec((tm, w), lambda i: (i, 0))
    a_shapes, a_specs = [], []
    for _, dil in A_CONFIGS:
        for _ in range(3):
            a_shapes.append(jax.ShapeDtypeStruct((batch, n_pairs, dil, seq // dil, PASS_WIDTH),
                                                 jnp.bfloat16))
            a_specs.append(pl.BlockSpec((None, n_pairs, dil, tm // dil, PASS_WIDTH),
                                        lambda i: (i // tiles_per_seq, 0, 0, i % tiles_per_seq, 0)))
    wide = jax.ShapeDtypeStruct((t, GROUP_WIDTH), jnp.bfloat16)
    narrow = jax.ShapeDtypeStruct((t, B_KV_WIDTH), jnp.bfloat16)
    return pl.pallas_call(
        _proj_kernel,
        out_shape=tuple(a_shapes) + (wide, narrow, narrow),
        grid=(t // tm,),
        in_specs=[row(D_MODEL),
                  pl.BlockSpec((1, D_MODEL), lambda i: (0, 0)),
                  pl.BlockSpec((D_MODEL, in_width), lambda i: (0, 0), pipeline_mode=pl.Buffered(1))],
        out_specs=tuple(a_specs) + (row(GROUP_WIDTH), row(B_KV_WIDTH), row(B_KV_WIDTH)),
        scratch_shapes=[pltpu.VMEM((3 * GROUP_WIDTH // PASS_WIDTH, dil, tm // dil, PASS_WIDTH), jnp.float32)
                        for _, dil in A_CONFIGS[:-1]],
        compiler_params=pltpu.CompilerParams(dimension_semantics=("arbitrary",),
                                             vmem_limit_bytes=ATTN_VMEM_LIMIT),
        name="proj",
    )(x2d, g_mix, w_in_bf)


def _fill_bias(bias_ref, half, nkeys, dist_scale):
    i = lax.broadcasted_iota(jnp.int32, (Q_BLOCK, nkeys), 0)
    j = lax.broadcasted_iota(jnp.int32, (Q_BLOCK, nkeys), 1)
    for var, rel0 in enumerate((0, -half, Q_BLOCK - nkeys)):
        d = jnp.abs(j + rel0 - i)
        dist = d.astype(jnp.float32) * dist_scale
        for h in range(N_HEADS):
            slope = 2.0 ** (-(h + 1))
            bias_ref[var, h] = jnp.where(d <= half, -(slope * dist) * LOG2E, NEG)


def _key_window(q0, seq_len, half, nkeys):
    lo = q0 - half
    ks = pl.multiple_of(jnp.clip(lo, 0, seq_len - nkeys), HEAD_DIM)
    var = jnp.where(lo < 0, 0, jnp.where(lo > seq_len - nkeys, 2, 1))
    return ks, var


def _fold_lane_tiles(a, op):
    out = a[..., 0:LANES]
    for t in range(1, a.shape[-1] // LANES):
        out = op(out, a[..., t * LANES:(t + 1) * LANES])
    return out


def _attend_pass(q, k, v, bias, sinks):
    nkeys = k.shape[0]
    lane_head = lax.broadcasted_iota(jnp.int32, (Q_BLOCK, PASS_WIDTH), 1) // HEAD_DIM
    zero = jnp.zeros_like(q)
    q_stack = jnp.concatenate(
        [jnp.where(lane_head == h, q, zero) for h in range(HEADS_PER_PASS)], axis=0)
    s = lax.dot_general(q_stack, k, (((1,), (1,)), ((), ())), preferred_element_type=jnp.float32)
    s = s.reshape(HEADS_PER_PASS, Q_BLOCK, nkeys) + bias
    m_tile = _fold_lane_tiles(s, jnp.maximum)
    if sinks is not None:
        tile_head = lax.broadcasted_iota(jnp.int32, m_tile.shape, 0)
        tile_lane = lax.broadcasted_iota(jnp.int32, m_tile.shape, 2)
        sk = jnp.zeros(m_tile.shape, jnp.float32)
        for h in range(HEADS_PER_PASS):
            sk = jnp.where(tile_head == h, sinks[h] * LOG2E, sk)
        m_tile = jnp.maximum(m_tile, sk)
    m = jnp.max(m_tile, axis=-1, keepdims=True)
    p = jnp.exp2(s - m)
    den_tile = _fold_lane_tiles(p, jnp.add)
    if sinks is not None:
        den_tile = den_tile + jnp.where(tile_lane == 0, jnp.exp2(sk - m), 0.0)
    den = jnp.sum(den_tile, axis=-1, keepdims=True)
    pv = _dot(p.reshape(HEADS_PER_PASS * Q_BLOCK, nkeys).astype(jnp.bfloat16), v)
    pv = pv.reshape(HEADS_PER_PASS, Q_BLOCK, PASS_WIDTH)
    shape = (Q_BLOCK, PASS_WIDTH)
    o, m_lanes, den_lanes = (jnp.zeros(shape, jnp.float32) for _ in range(3))
    for h in range(HEADS_PER_PASS):
        o = jnp.where(lane_head == h, pv[h], o)
        m_lanes = jnp.where(lane_head == h, m[h], m_lanes)
        den_lanes = jnp.where(lane_head == h, den[h], den_lanes)
    return o, m_lanes, den_lanes


A_HALF = 64
A_KEYS = Q_BLOCK + 2 * A_HALF
PASSES_PER_BODY = 16
MERGE_BLOCK = 256


def _attn_a_kernel(*refs, seq):
    n_cfg = len(A_CONFIGS)
    qkv = refs[:3 * n_cfg]
    o_ref, bias_ref = refs[3 * n_cfg], refs[3 * n_cfg + 1]
    scratch = refs[3 * n_cfg + 2:]
    pair = pl.program_id(1)

    @pl.when((pl.program_id(0) == 0) & (pair == 0))
    def _():
        for c, (_, dil) in enumerate(A_CONFIGS):
            _fill_bias(bias_ref.at[c], A_HALF, A_KEYS, float(dil))

    for c, (window, dil) in enumerate(A_CONFIGS):
        assert window // (2 * dil) == A_HALF
        q_ref, k_ref, v_ref = qkv[3 * c:3 * c + 3]
        o_sc, m_sc, d_sc = scratch[3 * c:3 * c + 3]
        sub_len = seq // dil
        blocks = sub_len // Q_BLOCK

        def body(idx, carry, q_ref=q_ref, k_ref=k_ref, v_ref=v_ref, o_sc=o_sc, m_sc=m_sc, d_sc=d_sc, c=c,
                 dil=dil, sub_len=sub_len, blocks=blocks):
            rho = idx // blocks
            q0 = pl.multiple_of((idx % blocks) * Q_BLOCK, Q_BLOCK)
            ks, var = _key_window(q0, sub_len, A_HALF, A_KEYS)
            bias = bias_ref[c, var, pl.ds(pair * HEADS_PER_PASS, HEADS_PER_PASS)]
            o, m, den = _attend_pass(q_ref[rho, pl.ds(q0, Q_BLOCK), :], k_ref[rho, pl.ds(ks, A_KEYS), :],
                                     v_ref[rho, pl.ds(ks, A_KEYS), :], bias, None)
            tokens = pl.ds(rho + dil * q0, Q_BLOCK, stride=dil) if dil > 1 else pl.ds(q0, Q_BLOCK)
            o_sc[tokens, :] = o
            m_sc[tokens, :] = m
            d_sc[tokens, :] = den
            return carry

        lax.fori_loop(0, dil * blocks, body, 0, unroll=PASSES_PER_BODY)

    n_cfg = len(A_CONFIGS)

    def merge(blk, carry):
        rows = pl.ds(pl.multiple_of(blk * MERGE_BLOCK, MERGE_BLOCK), MERGE_BLOCK)
        pvs = [scratch[3 * c][rows, :] for c in range(n_cfg)]
        ms = [scratch[3 * c + 1][rows, :] for c in range(n_cfg)]
        dens = [scratch[3 * c + 2][rows, :] for c in range(n_cfg)]
        mx = functools.reduce(jnp.maximum, ms)
        es = [jnp.exp2(m - mx) for m in ms]
        total = functools.reduce(jnp.add, [e * d for e, d in zip(es, dens)])
        o_ref[rows, :] = functools.reduce(jnp.add, [e * pv for e, pv in zip(es, pvs)]) * (1.0 / total)
        return carry

    lax.fori_loop(0, seq // MERGE_BLOCK, merge, 0, unroll=2)


def _attn_a(qkv, batch, seq):
    n_pairs = N_HEADS // HEADS_PER_PASS
    in_specs, scratch = [], [pltpu.VMEM((len(A_CONFIGS), 3, N_HEADS, Q_BLOCK, A_KEYS), jnp.float32)]
    for _, dil in A_CONFIGS:
        blk = (None, None, dil, seq // dil, PASS_WIDTH)
        in_specs += [pl.BlockSpec(blk, lambda b, p: (b, p, 0, 0, 0))] * 3
        scratch += [pltpu.VMEM((seq, PASS_WIDTH), jnp.float32)] * 3
    return pl.pallas_call(
        functools.partial(_attn_a_kernel, seq=seq),
        out_shape=jax.ShapeDtypeStruct((batch, seq, GROUP_WIDTH), jnp.float32),
        grid=(batch, n_pairs),
        in_specs=in_specs,
        out_specs=pl.BlockSpec((None, seq, PASS_WIDTH), lambda b, p: (b, 0, p)),
        scratch_shapes=scratch,
        compiler_params=pltpu.CompilerParams(dimension_semantics=("arbitrary", "arbitrary"),
                                             vmem_limit_bytes=ATTN_VMEM_LIMIT),
        name="attn_a",
    )(*qkv).reshape(batch * seq, GROUP_WIDTH)


def _attn_b_kernel(sink_ref, q_ref, k_ref, v_ref, o_ref, bias_ref, k4_ref, v4_ref, *, half, nkeys,
                   seq_len, q_tile):
    first = (pl.program_id(0) == 0) & (pl.program_id(1) == 0)

    @pl.when(first)
    def _():
        _fill_bias(bias_ref, half, nkeys, 1.0)

    @pl.when(pl.program_id(1) == 0)
    def _():
        c = lax.broadcasted_iota(jnp.int32, (B_KV_WIDTH, GROUP_WIDTH), 0)
        j = lax.broadcasted_iota(jnp.int32, (B_KV_WIDTH, GROUP_WIDTH), 1)
        src = (j // (GROUP_WIDTH // B_KV_HEADS)) * HEAD_DIM + j % HEAD_DIM
        rep = jnp.where(c == src, 1.0, 0.0).astype(jnp.bfloat16)
        chunk = 512

        def body(i, carry):
            rows = pl.ds(pl.multiple_of(i * chunk, chunk), chunk)
            k4_ref[rows, :] = _dot(k_ref[rows, :], rep).astype(jnp.bfloat16)
            v4_ref[rows, :] = _dot(v_ref[rows, :], rep).astype(jnp.bfloat16)
            return carry

        lax.fori_loop(0, seq_len // chunk, body, 0)

    base = pl.program_id(1) * q_tile

    def body(jb, carry):
        row0 = pl.multiple_of(jb * Q_BLOCK, Q_BLOCK)
        ks, var = _key_window(base + row0, seq_len, half, nkeys)
        for g in range(N_HEADS // HEADS_PER_PASS):
            cols = slice(g * PASS_WIDTH, (g + 1) * PASS_WIDTH)
            heads = pl.ds(g * HEADS_PER_PASS, HEADS_PER_PASS)
            sinks = [sink_ref[g * HEADS_PER_PASS + h] for h in range(HEADS_PER_PASS)]
            o, _, den = _attend_pass(q_ref[pl.ds(row0, Q_BLOCK), cols], k4_ref[pl.ds(ks, nkeys), cols],
                                     v4_ref[pl.ds(ks, nkeys), cols], bias_ref[var, heads], sinks)
            o_ref[pl.ds(row0, Q_BLOCK), cols] = (o * (1.0 / den)).astype(o_ref.dtype)
        return carry

    lax.fori_loop(0, q_tile // Q_BLOCK, body, 0,
                  unroll=PASSES_PER_BODY // (N_HEADS // HEADS_PER_PASS))


def _attn_b(q, k, v, sink, batch, seq):
    half = B_HALF_WINDOW
    nkeys = Q_BLOCK + 2 * half
    q_tile = 1024
    view = lambda a: a.reshape(batch, seq, a.shape[-1])
    kern = functools.partial(_attn_b_kernel, half=half, nkeys=nkeys, seq_len=seq, q_tile=q_tile)
    o = pl.pallas_call(
        kern,
        out_shape=jax.ShapeDtypeStruct((batch, seq, GROUP_WIDTH), jnp.bfloat16),
        grid=(batch, seq // q_tile),
        in_specs=[pl.BlockSpec(memory_space=pltpu.SMEM),
                  pl.BlockSpec((None, q_tile, GROUP_WIDTH), lambda b, i: (b, i, 0)),
                  pl.BlockSpec((None, seq, B_KV_WIDTH), lambda b, i: (b, 0, 0)),
                  pl.BlockSpec((None, seq, B_KV_WIDTH), lambda b, i: (b, 0, 0))],
        out_specs=pl.BlockSpec((None, q_tile, GROUP_WIDTH), lambda b, i: (b, i, 0)),
        scratch_shapes=[pltpu.VMEM((3, N_HEADS, Q_BLOCK, nkeys), jnp.float32),
                        pltpu.VMEM((seq, GROUP_WIDTH), jnp.bfloat16),
                        pltpu.VMEM((seq, GROUP_WIDTH), jnp.bfloat16)],
        compiler_params=pltpu.CompilerParams(dimension_semantics=("arbitrary", "arbitrary"),
                                             vmem_limit_bytes=VMEM_LIMIT),
        name="attn_b",
    )(sink, view(q), view(k), view(v))
    return o.reshape(batch * seq, GROUP_WIDTH)


def _mix_kernel(oa_ref, ob_ref, x_ref, ga_ref, gb_ref, wout_ref, gffn_ref, wr_ref, br_ref,
                x1_ref, h2_ref, ri_ref, rc_ref, cnt_ref):
    tm = x_ref.shape[0]
    na = _rms(oa_ref[...], ga_ref[...]).astype(jnp.bfloat16)
    nb = _rms(ob_ref[...].astype(jnp.float32), gb_ref[...]).astype(jnp.bfloat16)
    x1 = x_ref[...] + _dot(jnp.concatenate([na, nb], axis=1), wout_ref[...])
    x1_ref[...] = x1
    h2 = _rms(x1, gffn_ref[...])
    h2_ref[...] = h2.astype(jnp.bfloat16)

    h_hi = h2.astype(jnp.bfloat16)
    h_lo = (h2 - h_hi.astype(jnp.float32)).astype(jnp.bfloat16)
    wr = wr_ref[...]
    w_hi = wr.astype(jnp.bfloat16)
    w_lo = (wr - w_hi.astype(jnp.float32)).astype(jnp.bfloat16)
    w_cat = jnp.concatenate([w_hi, w_lo], axis=0)
    nt = (((1,), (1,)), ((), ()))
    lg2 = (lax.dot_general(w_cat, h_hi, nt, preferred_element_type=jnp.float32)
           + lax.dot_general(w_cat, h_lo, nt, preferred_element_type=jnp.float32))
    lgt = lg2[0:ROUTER_ROWS] + lg2[ROUTER_ROWS:2 * ROUTER_ROWS] + br_ref[...]

    rows = lax.broadcasted_iota(jnp.int32, (EXPERTS_PER_GROUP, tm), 0)
    big = jnp.int32(1 << 20)
    glog = jnp.where(rows < N_GROUPS, lgt[0:EXPERTS_PER_GROUP], -jnp.inf)
    gmax = jnp.max(glog, axis=0, keepdims=True)
    gsel = jnp.min(jnp.where(glog == gmax, rows, big), axis=0, keepdims=True)
    gw = 1.0 / jnp.sum(jnp.exp(glog - gmax), axis=0, keepdims=True)
    esel = jnp.zeros((EXPERTS_PER_GROUP, tm), jnp.float32)
    for grp in range(N_GROUPS):
        lo = EXPERTS_PER_GROUP * (grp + 1)
        esel = jnp.where(gsel == grp, lgt[lo:lo + EXPERTS_PER_GROUP], esel)
    v0 = jnp.max(esel, axis=0, keepdims=True)
    i0 = jnp.min(jnp.where(esel == v0, rows, big), axis=0, keepdims=True)
    rest = jnp.where(rows == i0, -jnp.inf, esel)
    v1 = jnp.max(rest, axis=0, keepdims=True)
    i1 = jnp.min(jnp.where(rest == v1, rows, big), axis=0, keepdims=True)
    e = jnp.exp(v1 - v0)
    w0 = (1.0 / (1.0 + e)) * gw
    w1 = (e / (1.0 + e)) * gw
    eid0 = gsel * EXPERTS_PER_GROUP + i0
    eid1 = gsel * EXPERTS_PER_GROUP + i1
    ri_ref[...] = jnp.where(rows == 0, eid0, jnp.where(rows == 1, eid1, 0))
    rows_t = lax.broadcasted_iota(jnp.int32, (ROUTER_ROWS, tm), 0)
    slab = jnp.where(rows_t == 0, w0, jnp.where(rows_t == 1, w1, 0.0))
    slab = jnp.where(rows_t == 2, eid0.astype(jnp.float32),
                     jnp.where(rows_t == 3, eid1.astype(jnp.float32), slab))
    rc_ref[...] = slab.T[:, 0:EXPERTS_PER_GROUP]
    ids = lax.broadcasted_iota(jnp.int32, (N_EXPERTS, tm), 0)
    member = jnp.where((ids == eid0) | (ids == eid1), 1.0, 0.0)
    cnt_ref[...] = jnp.sum(member, axis=1, keepdims=True).astype(jnp.int32)


def _mix(oa, ob, x2d, g_a, g_b, w_out_bf, g_ffn, wr, br):
    t = x2d.shape[0]
    tm = TOKEN_TILE
    row = lambda w: pl.BlockSpec((tm, w), lambda i: (i, 0))
    full = lambda a: pl.BlockSpec(a.shape, lambda i: (0, 0))
    return pl.pallas_call(
        _mix_kernel,
        out_shape=(jax.ShapeDtypeStruct((t, D_MODEL), jnp.float32),
                   jax.ShapeDtypeStruct((t, D_MODEL), jnp.bfloat16),
                   jax.ShapeDtypeStruct((EXPERTS_PER_GROUP, t), jnp.int32),
                   jax.ShapeDtypeStruct((t, EXPERTS_PER_GROUP), jnp.float32),
                   jax.ShapeDtypeStruct((t // tm, N_EXPERTS, 1), jnp.int32)),
        grid=(t // tm,),
        in_specs=[row(GROUP_WIDTH), row(GROUP_WIDTH), row(D_MODEL),
                  full(g_a), full(g_b), full(w_out_bf), full(g_ffn), full(wr), full(br)],
        out_specs=(row(D_MODEL), row(D_MODEL),
                   pl.BlockSpec((EXPERTS_PER_GROUP, tm), lambda i: (0, i)),
                   row(EXPERTS_PER_GROUP),
                   pl.BlockSpec((None, N_EXPERTS, 1), lambda i: (i, 0, 0))),
        compiler_params=pltpu.CompilerParams(dimension_semantics=("arbitrary",),
                                             vmem_limit_bytes=VMEM_LIMIT),
        name="mix_router",
    )(oa, ob, x2d, g_a, g_b, w_out_bf, g_ffn, wr, br)


RUN_ALIGN = 16
LOCAL_ROWS = 2 * TOKEN_TILE + N_EXPERTS * RUN_ALIGN
MAX_CHUNKS = LOCAL_ROWS // RUN_ALIGN
START_UNROLL = 4
SORT_BLOCK = 256
FINAL_BLOCK = 256


def _local_slots(member, lo_vec, prior, e_sel, axis):
    ids = lax.broadcasted_iota(jnp.int32, member.shape, axis)
    slot = jnp.where(ids == e_sel, lo_vec + prior, 0.0)
    return jnp.sum(slot, axis=axis, keepdims=True).astype(jnp.int32)


def _chunk_copy(dst_ref, tile, c, local, glob, sem, to_global):
    g = pl.multiple_of(dst_ref[tile * MAX_CHUNKS + c], RUN_ALIGN)
    l = pl.multiple_of(c * RUN_ALIGN, RUN_ALIGN)
    src, dst = local.at[pl.ds(l, RUN_ALIGN), :], glob.at[pl.ds(g, RUN_ALIGN), :]
    if not to_global:
        src, dst = dst, src
    return pltpu.make_async_copy(src, dst, sem)


def _start_chunks(nchunk_ref, dst_ref, tile, local, glob, sem, to_global):
    n = nchunk_ref[tile]
    groups = n // START_UNROLL

    def group(gi, carry):
        for u in range(START_UNROLL):
            _chunk_copy(dst_ref, tile, gi * START_UNROLL + u, local, glob, sem, to_global).start()
        return carry

    def single(c, carry):
        _chunk_copy(dst_ref, tile, groups * START_UNROLL + c, local, glob, sem, to_global).start()
        return carry

    lax.fori_loop(0, groups, group, 0)
    lax.fori_loop(0, n % START_UNROLL, single, 0)


def _wait_chunks(nchunk_ref, tile, local, glob, sem):
    n = nchunk_ref[tile]
    size = 1
    while size <= MAX_CHUNKS:
        rows = size * RUN_ALIGN

        @pl.when((n // size) % 2 == 1)
        def _(rows=rows):
            pltpu.make_async_copy(local.at[pl.ds(0, rows), :], glob.at[pl.ds(0, rows), :], sem).wait()

        size *= 2


def _lo_vector(lo_ref, tile, shape, axis):
    ids = lax.broadcasted_iota(jnp.int32, shape, axis)
    vec = jnp.zeros(shape, jnp.float32)
    for e in range(N_EXPERTS):
        vec = jnp.where(ids == e, lo_ref[tile * N_EXPERTS + e].astype(jnp.float32), vec)
    return vec


def _dispatch_kernel(lo_ref, nchunk_ref, dst_ref, pad_tile_ref, nused_ref, ri_ref, h_ref, xs_hbm,
                     upper, sbuf, zbuf, sem, zsem, *, n_tiles):
    i = pl.program_id(0)
    n_steps = pl.num_programs(0)
    slot = i % 2

    def zero_tile(tile):
        rows = pl.ds(pl.multiple_of(tile * ROW_TILE, ROW_TILE), ROW_TILE)
        return pltpu.make_async_copy(zbuf, xs_hbm.at[rows, :], zsem)

    @pl.when(i == 0)
    def _():
        r = lax.broadcasted_iota(jnp.int32, (TOKEN_TILE, TOKEN_TILE), 0)
        c = lax.broadcasted_iota(jnp.int32, (TOKEN_TILE, TOKEN_TILE), 1)
        upper[...] = jnp.where(r < c, 1.0, 0.0).astype(jnp.bfloat16)
        zbuf[...] = jnp.zeros_like(zbuf)
        for e in range(N_EXPERTS):
            @pl.when(pad_tile_ref[e] >= 0)
            def _():
                zero_tile(pad_tile_ref[e]).start()
        lax.fori_loop(nused_ref[0], n_tiles, lambda t, c: (zero_tile(t).start(), c)[1], 0)
        for e in range(N_EXPERTS):
            @pl.when(pad_tile_ref[e] >= 0)
            def _():
                zero_tile(pad_tile_ref[e]).wait()
        lax.fori_loop(nused_ref[0], n_tiles, lambda t, c: (zero_tile(t).wait(), c)[1], 0)

    e0 = ri_ref[0:1, :]
    e1 = ri_ref[1:2, :]
    ids = lax.broadcasted_iota(jnp.int32, (N_EXPERTS, TOKEN_TILE), 0)
    member = jnp.where((ids == e0) | (ids == e1), 1.0, 0.0)
    prior = _dot(member.astype(jnp.bfloat16), upper[...])
    lo_vec = _lo_vector(lo_ref, i, (N_EXPERTS, 1), 0)
    d0 = _local_slots(member, lo_vec, prior, e0, 0)
    d1 = _local_slots(member, lo_vec, prior, e1, 0)
    local = sbuf.at[slot]
    used_rows = nchunk_ref[i] * RUN_ALIGN
    h = h_ref[...]
    for blk in range(LOCAL_ROWS // SORT_BLOCK):
        @pl.when(blk * SORT_BLOCK < used_rows)
        def _():
            rows = blk * SORT_BLOCK + lax.broadcasted_iota(jnp.int32, (SORT_BLOCK, TOKEN_TILE), 0)
            select = jnp.where((rows == d0) | (rows == d1), 1.0, 0.0).astype(jnp.bfloat16)
            local[blk * SORT_BLOCK:(blk + 1) * SORT_BLOCK, :] = _dot(select, h).astype(jnp.bfloat16)

    @pl.when(i > 0)
    def _():
        _wait_chunks(nchunk_ref, i - 1, sbuf.at[1 - slot], xs_hbm, sem)

    _start_chunks(nchunk_ref, dst_ref, i, local, xs_hbm, sem, True)

    @pl.when(i == n_steps - 1)
    def _():
        _wait_chunks(nchunk_ref, i, local, xs_hbm, sem)


def _dispatch(lo, nchunk, dst, pad_tile, nused, route_i, h2, p_rows):
    n_tok = h2.shape[0]
    grid_spec = pltpu.PrefetchScalarGridSpec(
        num_scalar_prefetch=5,
        grid=(n_tok // TOKEN_TILE,),
        in_specs=[pl.BlockSpec((EXPERTS_PER_GROUP, TOKEN_TILE), lambda i, *_: (0, i)),
                  pl.BlockSpec((TOKEN_TILE, D_MODEL), lambda i, *_: (i, 0))],
        out_specs=pl.BlockSpec(memory_space=pl.ANY),
        scratch_shapes=[pltpu.VMEM((TOKEN_TILE, TOKEN_TILE), jnp.bfloat16),
                        pltpu.VMEM((2, LOCAL_ROWS, D_MODEL), jnp.bfloat16),
                        pltpu.VMEM((ROW_TILE, D_MODEL), jnp.bfloat16),
                        pltpu.SemaphoreType.DMA, pltpu.SemaphoreType.DMA],
    )
    return pl.pallas_call(
        functools.partial(_dispatch_kernel, n_tiles=p_rows // ROW_TILE),
        out_shape=jax.ShapeDtypeStruct((p_rows, D_MODEL), jnp.bfloat16),
        grid_spec=grid_spec,
        compiler_params=pltpu.CompilerParams(dimension_semantics=("arbitrary",),
                                             vmem_limit_bytes=VMEM_LIMIT),
        name="dispatch",
    )(lo, nchunk, dst, pad_tile, nused, route_i, h2)


def _moe_kernel(texp_ref, first_ref, nused_ref, slot_ref, next_ref, xs_ref, wg_hbm, wu_hbm, wd_hbm,
                y_ref, wg_f32, wu_f32, wd_f32, wg_bf, wu_bf, wd_bf, sems):
    j = pl.program_id(0)

    def fetch(expert, slot):
        return [pltpu.make_async_copy(src.at[expert], dst.at[slot], sems.at[slot, k])
                for k, (src, dst) in enumerate(((wg_hbm, wg_f32), (wu_hbm, wu_f32), (wd_hbm, wd_f32)))]

    @pl.when(j == 0)
    def _():
        for cp in fetch(texp_ref[0], 0):
            cp.start()

    @pl.when(j < nused_ref[0])
    def _():
        @pl.when(first_ref[j] == 1)
        def _():
            for s in range(2):
                @pl.when(slot_ref[j] == s)
                def _():
                    for cp in fetch(texp_ref[j], s):
                        cp.wait()
                    wg_bf[...] = wg_f32[s].astype(jnp.bfloat16)
                    wu_bf[...] = wu_f32[s].astype(jnp.bfloat16)
                    wd_bf[...] = wd_f32[s].astype(jnp.bfloat16)

                    @pl.when(next_ref[j] >= 0)
                    def _():
                        for cp in fetch(next_ref[j], 1 - s):
                            cp.start()

        x = xs_ref[...]
        a = _dot(x, wg_bf[...])
        u = _dot(x, wu_bf[...])
        hid = (a * (1.0 / (1.0 + jnp.exp(-a)))) * u
        y_ref[...] = _dot(hid.astype(jnp.bfloat16), wd_bf[...]).astype(jnp.bfloat16)

    @pl.when(j >= nused_ref[0])
    def _():
        y_ref[...] = jnp.zeros_like(y_ref)


def _moe(texp, first, nused, slot, nxt, xs, w_gate, w_up, w_down):
    p_rows = xs.shape[0]
    n_tiles = p_rows // ROW_TILE
    used = lambda j, nu: jnp.minimum(j, nu[0] - 1)
    grid_spec = pltpu.PrefetchScalarGridSpec(
        num_scalar_prefetch=5,
        grid=(n_tiles,),
        in_specs=[
            pl.BlockSpec((ROW_TILE, D_MODEL), lambda j, te, fi, nu, sl, nx: (used(j, nu), 0)),
            pl.BlockSpec(memory_space=pl.ANY),
            pl.BlockSpec(memory_space=pl.ANY),
            pl.BlockSpec(memory_space=pl.ANY),
        ],
        out_specs=pl.BlockSpec((ROW_TILE, D_MODEL), lambda j, te, fi, nu, sl, nx: (j, 0)),
        scratch_shapes=[pltpu.VMEM((2, D_MODEL, D_EXPERT), jnp.float32),
                        pltpu.VMEM((2, D_MODEL, D_EXPERT), jnp.float32),
                        pltpu.VMEM((2, D_EXPERT, D_MODEL), jnp.float32),
                        pltpu.VMEM((D_MODEL, D_EXPERT), jnp.bfloat16),
                        pltpu.VMEM((D_MODEL, D_EXPERT), jnp.bfloat16),
                        pltpu.VMEM((D_EXPERT, D_MODEL), jnp.bfloat16),
                        pltpu.SemaphoreType.DMA((2, 3))],
    )
    return pl.pallas_call(
        _moe_kernel,
        out_shape=jax.ShapeDtypeStruct((p_rows, D_MODEL), jnp.bfloat16),
        grid_spec=grid_spec,
        compiler_params=pltpu.CompilerParams(dimension_semantics=("arbitrary",),
                                             vmem_limit_bytes=VMEM_LIMIT),
        name="moe_experts",
    )(texp, first, nused, slot, nxt, xs, w_gate, w_up, w_down)


def _final_kernel(lo_ref, nchunk_ref, dst_ref, y_hbm, x1_ref, rc_ref, p_ref, gple_ref, wg_ref, wp_ref,
                  gfin_ref, out_ref, lower, ybuf, sems):
    i = pl.program_id(0)
    n_steps = pl.num_programs(0)
    slot = i % 2

    @pl.when(i == 0)
    def _():
        r = lax.broadcasted_iota(jnp.int32, (TOKEN_TILE, TOKEN_TILE), 0)
        c = lax.broadcasted_iota(jnp.int32, (TOKEN_TILE, TOKEN_TILE), 1)
        lower[...] = jnp.where(c < r, 1.0, 0.0).astype(jnp.bfloat16)
        ybuf[...] = jnp.zeros_like(ybuf)
        _start_chunks(nchunk_ref, dst_ref, 0, ybuf.at[0], y_hbm, sems.at[0], False)

    @pl.when(i + 1 < n_steps)
    def _():
        _start_chunks(nchunk_ref, dst_ref, i + 1, ybuf.at[1 - slot], y_hbm, sems.at[1 - slot], False)

    rc = rc_ref[...]
    w0, w1 = rc[:, 0:1], rc[:, 1:2]
    e0, e1 = rc[:, 2:3].astype(jnp.int32), rc[:, 3:4].astype(jnp.int32)
    ids = lax.broadcasted_iota(jnp.int32, (TOKEN_TILE, ROUTER_ROWS), 1)
    member = jnp.where((ids == e0) | (ids == e1), 1.0, 0.0)
    prior = _dot(lower[...], member.astype(jnp.bfloat16))
    lo_vec = _lo_vector(lo_ref, i, (1, ROUTER_ROWS), 1)
    d0 = _local_slots(member, lo_vec, prior, e0, 1)
    d1 = _local_slots(member, lo_vec, prior, e1, 1)

    local = ybuf.at[slot]
    _wait_chunks(nchunk_ref, i, local, y_hbm, sems.at[slot])
    for blk in range(TOKEN_TILE // FINAL_BLOCK):
        rows = slice(blk * FINAL_BLOCK, (blk + 1) * FINAL_BLOCK)
        cols = lax.broadcasted_iota(jnp.int32, (FINAL_BLOCK, LOCAL_ROWS), 1)
        pick = jnp.where(cols == d0[rows], w0[rows], jnp.where(cols == d1[rows], w1[rows], 0.0))
        x2 = x1_ref[rows, :] + _dot(pick.astype(jnp.bfloat16), local[...])
        z = _dot(_rms(x2, gple_ref[...]).astype(jnp.bfloat16), wg_ref[...])
        gate = 1.0 / (1.0 + jnp.exp(-z))
        x3 = x2 + _dot(p_ref[rows, :].astype(jnp.bfloat16), wp_ref[...]) * gate
        out_ref[rows, :] = _rms(x3, gfin_ref[...])


def _final(lo, nchunk, dst, y, x1, rc, p2d, g_ple, w_gate_bf, w_proj_bf, g_final):
    t = x1.shape[0]
    tm = TOKEN_TILE
    row = lambda w: pl.BlockSpec((tm, w), lambda i, *_: (i, 0))
    full = lambda a: pl.BlockSpec(a.shape, lambda i, *_: (0, 0))
    grid_spec = pltpu.PrefetchScalarGridSpec(
        num_scalar_prefetch=3,
        grid=(t // tm,),
        in_specs=[pl.BlockSpec(memory_space=pl.ANY), row(D_MODEL), row(EXPERTS_PER_GROUP),
                  row(PLE_DIM), full(g_ple), full(w_gate_bf), full(w_proj_bf), full(g_final)],
        out_specs=row(D_MODEL),
        scratch_shapes=[pltpu.VMEM((TOKEN_TILE, TOKEN_TILE), jnp.bfloat16),
                        pltpu.VMEM((2, LOCAL_ROWS, D_MODEL), jnp.bfloat16),
                        pltpu.SemaphoreType.DMA((2,))],
    )
    return pl.pallas_call(
        _final_kernel,
        out_shape=jax.ShapeDtypeStruct((t, D_MODEL), jnp.float32),
        grid_spec=grid_spec,
        compiler_params=pltpu.CompilerParams(dimension_semantics=("arbitrary",),
                                             vmem_limit_bytes=VMEM_LIMIT),
        name="combine_ple_final",
    )(lo, nchunk, dst, y, x1, rc, p2d, g_ple, w_gate_bf, w_proj_bf, g_final)


def _dispatch_tables(counts):
    n_tt = counts.shape[0]
    p_rows = 2 * n_tt * TOKEN_TILE + n_tt * N_EXPERTS * (RUN_ALIGN - 1) + N_EXPERTS * ROW_TILE
    p_rows = -(-p_rows // ROW_TILE) * ROW_TILE
    n_tiles = p_rows // ROW_TILE
    n8 = (counts + RUN_ALIGN - 1) // RUN_ALIGN * RUN_ALIGN
    lo = jnp.cumsum(n8, axis=1) - n8
    total = jnp.sum(n8, axis=0)
    tiles_per = (total + ROW_TILE - 1) // ROW_TILE
    tile_end = jnp.cumsum(tiles_per)
    tile_start = tile_end - tiles_per
    g = tile_start[None, :] * ROW_TILE + (jnp.cumsum(n8, axis=0) - n8)
    n_used = tile_end[-1]
    tile_ids = jnp.arange(n_tiles, dtype=jnp.int32)
    clamped = jnp.minimum(tile_ids, n_used - 1)
    texp = jnp.sum((tile_end[None, :] <= clamped[:, None]).astype(jnp.int32), axis=1)
    experts = jnp.arange(N_EXPERTS, dtype=jnp.int32)
    of_tile = texp[:, None] == experts[None, :]
    per_tile = lambda v: jnp.sum(jnp.where(of_tile, v[None, :], 0), axis=1)
    first = ((tile_ids == per_tile(tile_start)) & (tile_ids < n_used)).astype(jnp.int32)
    pad_tile = jnp.where(total > 0, tile_end - 1, -1).astype(jnp.int32)
    has_rows = total > 0
    later = has_rows[None, :] & (experts[None, :] > experts[:, None])
    next_expert = jnp.min(jnp.where(later, experts[None, :], N_EXPERTS), axis=1)
    next_expert = jnp.where(next_expert < N_EXPERTS, next_expert, -1)
    slot = (jnp.cumsum(has_rows.astype(jnp.int32)) - 1) % 2
    chunk_row = jnp.arange(MAX_CHUNKS, dtype=jnp.int32)[None, :, None] * RUN_ALIGN
    in_run = (chunk_row >= lo[:, None, :]) & (chunk_row < (lo + n8)[:, None, :])
    dst = jnp.sum(jnp.where(in_run, g[:, None, :] + chunk_row - lo[:, None, :], 0), axis=2)
    nchunk = jnp.sum(n8, axis=1) // RUN_ALIGN
    i32 = lambda a: a.reshape(-1).astype(jnp.int32)
    return (texp, first, n_used.reshape(1).astype(jnp.int32), i32(per_tile(slot)),
            i32(per_tile(next_expert)), pad_tile, i32(lo), i32(nchunk), i32(dst), p_rows)


def kernel(x, p, g_mix, w_in, sink, g_grp_a, g_grp_b, w_out, g_ffn, w_router_group, b_router_group,
           w_router_expert, b_router_expert, w_expert_gate, w_expert_up, w_expert_down, g_ple,
           w_ple_gate, w_ple_proj, g_final):
    batch, seq, _ = x.shape
    n_tok = batch * seq
    depth = w_in.shape[0]
    assert depth == 1, "the final RMSNorm is fused into the last layer's epilogue"
    bf = jnp.bfloat16
    xc = x.reshape(n_tok, D_MODEL)
    for i in range(depth):
        *qkv_a, qb, kb, vb = _proj(xc, g_mix[i][None, :], w_in[i].astype(bf), batch, seq)
        oa = _attn_a(qkv_a, batch, seq)
        ob = _attn_b(qb, kb, vb, sink[i], batch, seq)

        wr = jnp.zeros((ROUTER_ROWS, D_MODEL), jnp.float32)
        wr = wr.at[0:N_GROUPS].set(w_router_group[i].T)
        wr = wr.at[EXPERTS_PER_GROUP:EXPERTS_PER_GROUP + N_EXPERTS].set(w_router_expert[i].T)
        br = jnp.zeros((ROUTER_ROWS, 1), jnp.float32)
        br = br.at[0:N_GROUPS, 0].set(b_router_group[i])
        br = br.at[EXPERTS_PER_GROUP:EXPERTS_PER_GROUP + N_EXPERTS, 0].set(b_router_expert[i])

        x1, h2, route_i, route_c, counts = _mix(
            oa, ob, xc, g_grp_a[i][None, :], g_grp_b[i][None, :], w_out[i].astype(bf), g_ffn[i][None, :], wr, br)

        texp, first, nused, slot, nxt, pad_tile, lo, nchunk, dst, p_rows = _dispatch_tables(counts[:, :, 0])
        xs = _dispatch(lo, nchunk, dst, pad_tile, nused, route_i, h2, p_rows)
        y = _moe(texp, first, nused, slot, nxt, xs, w_expert_gate[i], w_expert_up[i], w_expert_down[i])
        xc = _final(lo, nchunk, dst, y, x1, route_c, p[i].reshape(n_tok, PLE_DIM), g_ple[i][None, :],
                    w_ple_gate[i].astype(bf), w_ple_proj[i].astype(bf), g_final[None, :])
    return xc.reshape(batch, seq, D_MODEL)
```

```python
import functools

import jax
import jax.numpy as jnp
from jax import lax
from jax.experimental import pallas as pl
from jax.experimental.pallas import tpu as pltpu

D_MODEL = 1024
HEAD_DIM = 64
GROUP_WIDTH = 512
N_HEADS = 8
B_KV_HEADS = 2
B_KV_WIDTH = B_KV_HEADS * HEAD_DIM
A_CONFIGS = ((128, 1), (512, 4), (2048, 16))
B_HALF_WINDOW = 128
N_GROUPS = 4
EXPERTS_PER_GROUP = 8
N_EXPERTS = N_GROUPS * EXPERTS_PER_GROUP
D_EXPERT = 512
PLE_DIM = 256
EPS = 1e-6
NEG = -1e30
LOG2E = 1.4426950408889634

LANES = 128
Q_BLOCK = 128
HEADS_PER_PASS = 2
PASS_WIDTH = HEADS_PER_PASS * HEAD_DIM
ROW_TILE = 448
TOKEN_TILE = 512
ROUTER_ROWS = 128
VMEM_LIMIT = 48 * 1024 * 1024
ATTN_VMEM_LIMIT = 56 * 1024 * 1024


def _rms(x, g):
    r = lax.rsqrt(jnp.mean(x * x, axis=-1, keepdims=True) + EPS)
    return (x * r) * g


def _dot(a, b):
    return jnp.dot(a, b, preferred_element_type=jnp.float32)


PROJ_TILE = 1024


def _proj_kernel(x_ref, g_ref, w_ref, *refs):
    a_refs, (qb_ref, kb_ref, vb_ref, res, *stage_refs) = refs[:9], refs[9:]
    h = _rms(x_ref[...], g_ref[...]).astype(jnp.bfloat16)
    scale = HEAD_DIM ** -0.5 * LOG2E
    W = GROUP_WIDTH
    tiles = GROUP_WIDTH // PASS_WIDTH
    qkv_a = (_dot(h, w_ref[:, 0:W]) * scale, _dot(h, w_ref[:, W:2 * W]), _dot(h, w_ref[:, 2 * W:3 * W]))
    for part in range(3):
        for pair in range(tiles):
            res[part * tiles + pair, 0] = qkv_a[part][:, pair * PASS_WIDTH:(pair + 1) * PASS_WIDTH]
    qkv_b = _dot(h, w_ref[:, 3 * W:4 * W + 2 * B_KV_WIDTH])
    qb_ref[...] = (qkv_b[:, 0:W] * scale).astype(jnp.bfloat16)
    kb_ref[...] = qkv_b[:, W:W + B_KV_WIDTH].astype(jnp.bfloat16)
    vb_ref[...] = qkv_b[:, W + B_KV_WIDTH:W + 2 * B_KV_WIDTH].astype(jnp.bfloat16)
    stages = (res,) + tuple(stage_refs)
    prev_dil = 1
    for c, (_, dil) in enumerate(A_CONFIGS):
        ratio = dil // prev_dil
        rows = PROJ_TILE // dil
        for idx in range(3 * tiles):
            part, pair = idx // tiles, idx % tiles
            out = a_refs[3 * c + part]
            for rho in range(dil):
                if c == 0:
                    sub = res[idx, 0]
                else:
                    sub = stages[c - 1].at[idx, rho % prev_dil][pl.ds(rho // prev_dil, rows, stride=ratio), :]
                    if c < len(A_CONFIGS) - 1:
                        stages[c][idx, rho] = sub
                out[pair, rho] = sub.astype(jnp.bfloat16)
        prev_dil = dil


def _proj(x2d, g_mix, w_in_bf, batch, seq):
    t = x2d.shape[0]
    tm = PROJ_TILE
    tiles_per_seq = seq // tm
    in_width = w_in_bf.shape[1]
    n_pairs = N_HEADS // HEADS_PER_PASS
    row = lambda w: pl.BlockSpec((tm, w), lambda i: (i, 0))
    a_shapes, a_specs = [], []
    for _, dil in A_CONFIGS:
        for _ in range(3):
            a_shapes.append(jax.ShapeDtypeStruct((batch, n_pairs, dil, seq // dil, PASS_WIDTH),
                                                 jnp.bfloat16))
            a_specs.append(pl.BlockSpec((None, n_pairs, dil, tm // dil, PASS_WIDTH),
                                        lambda i: (i // tiles_per_seq, 0, 0, i % tiles_per_seq, 0)))
    wide = jax.ShapeDtypeStruct((t, GROUP_WIDTH), jnp.bfloat16)
    narrow = jax.ShapeDtypeStruct((t, B_KV_WIDTH), jnp.bfloat16)
    return pl.pallas_call(
        _proj_kernel,
        out_shape=tuple(a_shapes) + (wide, narrow, narrow),
        grid=(t // tm,),
        in_specs=[row(D_MODEL),
                  pl.BlockSpec((1, D_MODEL), lambda i: (0, 0)),
                  pl.BlockSpec((D_MODEL, in_width), lambda i: (0, 0), pipeline_mode=pl.Buffered(1))],
        out_specs=tuple(a_specs) + (row(GROUP_WIDTH), row(B_KV_WIDTH), row(B_KV_WIDTH)),
        scratch_shapes=[pltpu.VMEM((3 * GROUP_WIDTH // PASS_WIDTH, dil, tm // dil, PASS_WIDTH), jnp.float32)
                        for _, dil in A_CONFIGS[:-1]],
        compiler_params=pltpu.CompilerParams(dimension_semantics=("arbitrary",),
                                             vmem_limit_bytes=ATTN_VMEM_LIMIT),
        name="proj",
    )(x2d, g_mix, w_in_bf)


def _fill_bias(bias_ref, half, nkeys, dist_scale):
    i = lax.broadcasted_iota(jnp.int32, (Q_BLOCK, nkeys), 0)
    j = lax.broadcasted_iota(jnp.int32, (Q_BLOCK, nkeys), 1)
    for var, rel0 in enumerate((0, -half, Q_BLOCK - nkeys)):
        d = jnp.abs(j + rel0 - i)
        dist = d.astype(jnp.float32) * dist_scale
        for h in range(N_HEADS):
            slope = 2.0 ** (-(h + 1))
            bias_ref[var, h] = jnp.where(d <= half, -(slope * dist) * LOG2E, NEG)


def _key_window(q0, seq_len, half, nkeys):
    lo = q0 - half
    ks = pl.multiple_of(jnp.clip(lo, 0, seq_len - nkeys), HEAD_DIM)
    var = jnp.where(lo < 0, 0, jnp.where(lo > seq_len - nkeys, 2, 1))
    return ks, var


def _fold_lane_tiles(a, op):
    out = a[..., 0:LANES]
    for t in range(1, a.shape[-1] // LANES):
        out = op(out, a[..., t * LANES:(t + 1) * LANES])
    return out


def _attend_pass(q, k, v, bias, sinks):
    nkeys = k.shape[0]
    lane_head = lax.broadcasted_iota(jnp.int32, (Q_BLOCK, PASS_WIDTH), 1) // HEAD_DIM
    zero = jnp.zeros_like(q)
    q_stack = jnp.concatenate(
        [jnp.where(lane_head == h, q, zero) for h in range(HEADS_PER_PASS)], axis=0)
    s = lax.dot_general(q_stack, k, (((1,), (1,)), ((), ())), preferred_element_type=jnp.float32)
    s = s.reshape(HEADS_PER_PASS, Q_BLOCK, nkeys) + bias
    m_tile = _fold_lane_tiles(s, jnp.maximum)
    if sinks is not None:
        tile_head = lax.broadcasted_iota(jnp.int32, m_tile.shape, 0)
        tile_lane = lax.broadcasted_iota(jnp.int32, m_tile.shape, 2)
        sk = jnp.zeros(m_tile.shape, jnp.float32)
        for h in range(HEADS_PER_PASS):
            sk = jnp.where(tile_head == h, sinks[h] * LOG2E, sk)
        m_tile = jnp.maximum(m_tile, sk)
    m = jnp.max(m_tile, axis=-1, keepdims=True)
    p = jnp.exp2(s - m)
    den_tile = _fold_lane_tiles(p, jnp.add)
    if sinks is not None:
        den_tile = den_tile + jnp.where(tile_lane == 0, jnp.exp2(sk - m), 0.0)
    den = jnp.sum(den_tile, axis=-1, keepdims=True)
    pv = _dot(p.reshape(HEADS_PER_PASS * Q_BLOCK, nkeys).astype(jnp.bfloat16), v)
    pv = pv.reshape(HEADS_PER_PASS, Q_BLOCK, PASS_WIDTH)
    shape = (Q_BLOCK, PASS_WIDTH)
    o, m_lanes, den_lanes = (jnp.zeros(shape, jnp.float32) for _ in range(3))
    for h in range(HEADS_PER_PASS):
        o = jnp.where(lane_head == h, pv[h], o)
        m_lanes = jnp.where(lane_head == h, m[h], m_lanes)
        den_lanes = jnp.where(lane_head == h, den[h], den_lanes)
    return o, m_lanes, den_lanes


A_HALF = 64
A_KEYS = Q_BLOCK + 2 * A_HALF
PASSES_PER_BODY = 16
MERGE_BLOCK = 256


def _attn_a_kernel(*refs, seq):
    n_cfg = len(A_CONFIGS)
    qkv = refs[:3 * n_cfg]
    o_ref, bias_ref = refs[3 * n_cfg], refs[3 * n_cfg + 1]
    scratch = refs[3 * n_cfg + 2:]
    pair = pl.program_id(1)

    @pl.when((pl.program_id(0) == 0) & (pair == 0))
    def _():
        for c, (_, dil) in enumerate(A_CONFIGS):
            _fill_bias(bias_ref.at[c], A_HALF, A_KEYS, float(dil))

    for c, (window, dil) in enumerate(A_CONFIGS):
        assert window // (2 * dil) == A_HALF
        q_ref, k_ref, v_ref = qkv[3 * c:3 * c + 3]
        o_sc, m_sc, d_sc = scratch[3 * c:3 * c + 3]
        sub_len = seq // dil
        blocks = sub_len // Q_BLOCK

        def body(idx, carry, q_ref=q_ref, k_ref=k_ref, v_ref=v_ref, o_sc=o_sc, m_sc=m_sc, d_sc=d_sc, c=c,
                 dil=dil, sub_len=sub_len, blocks=blocks):
            rho = idx // blocks
            q0 = pl.multiple_of((idx % blocks) * Q_BLOCK, Q_BLOCK)
            ks, var = _key_window(q0, sub_len, A_HALF, A_KEYS)
            bias = bias_ref[c, var, pl.ds(pair * HEADS_PER_PASS, HEADS_PER_PASS)]
            o, m, den = _attend_pass(q_ref[rho, pl.ds(q0, Q_BLOCK), :], k_ref[rho, pl.ds(ks, A_KEYS), :],
                                     v_ref[rho, pl.ds(ks, A_KEYS), :], bias, None)
            tokens = pl.ds(rho + dil * q0, Q_BLOCK, stride=dil) if dil > 1 else pl.ds(q0, Q_BLOCK)
            o_sc[tokens, :] = o
            m_sc[tokens, :] = m
            d_sc[tokens, :] = den
            return carry

        lax.fori_loop(0, dil * blocks, body, 0, unroll=PASSES_PER_BODY)

    n_cfg = len(A_CONFIGS)

    def merge(blk, carry):
        rows = pl.ds(pl.multiple_of(blk * MERGE_BLOCK, MERGE_BLOCK), MERGE_BLOCK)
        pvs = [scratch[3 * c][rows, :] for c in range(n_cfg)]
        ms = [scratch[3 * c + 1][rows, :] for c in range(n_cfg)]
        dens = [scratch[3 * c + 2][rows, :] for c in range(n_cfg)]
        mx = functools.reduce(jnp.maximum, ms)
        es = [jnp.exp2(m - mx) for m in ms]
        total = functools.reduce(jnp.add, [e * d for e, d in zip(es, dens)])
        o_ref[rows, :] = functools.reduce(jnp.add, [e * pv for e, pv in zip(es, pvs)]) * (1.0 / total)
        return carry

    lax.fori_loop(0, seq // MERGE_BLOCK, merge, 0, unroll=2)


def _attn_a(qkv, batch, seq):
    n_pairs = N_HEADS // HEADS_PER_PASS
    in_specs, scratch = [], [pltpu.VMEM((len(A_CONFIGS), 3, N_HEADS, Q_BLOCK, A_KEYS), jnp.float32)]
    for _, dil in A_CONFIGS:
        blk = (None, None, dil, seq // dil, PASS_WIDTH)
        in_specs += [pl.BlockSpec(blk, lambda b, p: (b, p, 0, 0, 0))] * 3
        scratch += [pltpu.VMEM((seq, PASS_WIDTH), jnp.float32)] * 3
    return pl.pallas_call(
        functools.partial(_attn_a_kernel, seq=seq),
        out_shape=jax.ShapeDtypeStruct((batch, seq, GROUP_WIDTH), jnp.float32),
        grid=(batch, n_pairs),
        in_specs=in_specs,
        out_specs=pl.BlockSpec((None, seq, PASS_WIDTH), lambda b, p: (b, 0, p)),
        scratch_shapes=scratch,
        compiler_params=pltpu.CompilerParams(dimension_semantics=("arbitrary", "arbitrary"),
                                             vmem_limit_bytes=ATTN_VMEM_LIMIT),
        name="attn_a",
    )(*qkv).reshape(batch * seq, GROUP_WIDTH)


def _attn_b_kernel(sink_ref, q_ref, k_ref, v_ref, o_ref, bias_ref, k4_ref, v4_ref, *, half, nkeys,
                   seq_len, q_tile):
    first = (pl.program_id(0) == 0) & (pl.program_id(1) == 0)

    @pl.when(first)
    def _():
        _fill_bias(bias_ref, half, nkeys, 1.0)

    @pl.when(pl.program_id(1) == 0)
    def _():
        c = lax.broadcasted_iota(jnp.int32, (B_KV_WIDTH, GROUP_WIDTH), 0)
        j = lax.broadcasted_iota(jnp.int32, (B_KV_WIDTH, GROUP_WIDTH), 1)
        src = (j // (GROUP_WIDTH // B_KV_HEADS)) * HEAD_DIM + j % HEAD_DIM
        rep = jnp.where(c == src, 1.0, 0.0).astype(jnp.bfloat16)
        chunk = 512

        def body(i, carry):
            rows = pl.ds(pl.multiple_of(i * chunk, chunk), chunk)
            k4_ref[rows, :] = _dot(k_ref[rows, :], rep).astype(jnp.bfloat16)
            v4_ref[rows, :] = _dot(v_ref[rows, :], rep).astype(jnp.bfloat16)
            return carry

        lax.fori_loop(0, seq_len // chunk, body, 0)

    base = pl.program_id(1) * q_tile

    def body(jb, carry):
        row0 = pl.multiple_of(jb * Q_BLOCK, Q_BLOCK)
        ks, var = _key_window(base + row0, seq_len, half, nkeys)
        for g in range(N_HEADS // HEADS_PER_PASS):
            cols = slice(g * PASS_WIDTH, (g + 1) * PASS_WIDTH)
            heads = pl.ds(g * HEADS_PER_PASS, HEADS_PER_PASS)
            sinks = [sink_ref[g * HEADS_PER_PASS + h] for h in range(HEADS_PER_PASS)]
            o, _, den = _attend_pass(q_ref[pl.ds(row0, Q_BLOCK), cols], k4_ref[pl.ds(ks, nkeys), cols],
                                     v4_ref[pl.ds(ks, nkeys), cols], bias_ref[var, heads], sinks)
            o_ref[pl.ds(row0, Q_BLOCK), cols] = (o * (1.0 / den)).astype(o_ref.dtype)
        return carry

    lax.fori_loop(0, q_tile // Q_BLOCK, body, 0,
                  unroll=PASSES_PER_BODY // (N_HEADS // HEADS_PER_PASS))


def _attn_b(q, k, v, sink, batch, seq):
    half = B_HALF_WINDOW
    nkeys = Q_BLOCK + 2 * half
    q_tile = 1024
    view = lambda a: a.reshape(batch, seq, a.shape[-1])
    kern = functools.partial(_attn_b_kernel, half=half, nkeys=nkeys, seq_len=seq, q_tile=q_tile)
    o = pl.pallas_call(
        kern,
        out_shape=jax.ShapeDtypeStruct((batch, seq, GROUP_WIDTH), jnp.bfloat16),
        grid=(batch, seq // q_tile),
        in_specs=[pl.BlockSpec(memory_space=pltpu.SMEM),
                  pl.BlockSpec((None, q_tile, GROUP_WIDTH), lambda b, i: (b, i, 0)),
                  pl.BlockSpec((None, seq, B_KV_WIDTH), lambda b, i: (b, 0, 0)),
                  pl.BlockSpec((None, seq, B_KV_WIDTH), lambda b, i: (b, 0, 0))],
        out_specs=pl.BlockSpec((None, q_tile, GROUP_WIDTH), lambda b, i: (b, i, 0)),
        scratch_shapes=[pltpu.VMEM((3, N_HEADS, Q_BLOCK, nkeys), jnp.float32),
                        pltpu.VMEM((seq, GROUP_WIDTH), jnp.bfloat16),
                        pltpu.VMEM((seq, GROUP_WIDTH), jnp.bfloat16)],
        compiler_params=pltpu.CompilerParams(dimension_semantics=("arbitrary", "arbitrary"),
                                             vmem_limit_bytes=VMEM_LIMIT),
        name="attn_b",
    )(sink, view(q), view(k), view(v))
    return o.reshape(batch * seq, GROUP_WIDTH)


def _mix_kernel(oa_ref, ob_ref, x_ref, ga_ref, gb_ref, wout_ref, gffn_ref, wr_ref, br_ref,
                x1_ref, h2_ref, ri_ref, rc_ref, cnt_ref):
    tm = x_ref.shape[0]
    na = _rms(oa_ref[...], ga_ref[...]).astype(jnp.bfloat16)
    nb = _rms(ob_ref[...].astype(jnp.float32), gb_ref[...]).astype(jnp.bfloat16)
    x1 = x_ref[...] + _dot(jnp.concatenate([na, nb], axis=1), wout_ref[...])
    x1_ref[...] = x1
    h2 = _rms(x1, gffn_ref[...])
    h2_ref[...] = h2.astype(jnp.bfloat16)

    h_hi = h2.astype(jnp.bfloat16)
    h_lo = (h2 - h_hi.astype(jnp.float32)).astype(jnp.bfloat16)
    wr = wr_ref[...]
    w_hi = wr.astype(jnp.bfloat16)
    w_lo = (wr - w_hi.astype(jnp.float32)).astype(jnp.bfloat16)
    w_cat = jnp.concatenate([w_hi, w_lo], axis=0)
    nt = (((1,), (1,)), ((), ()))
    lg2 = (lax.dot_general(w_cat, h_hi, nt, preferred_element_type=jnp.float32)
           + lax.dot_general(w_cat, h_lo, nt, preferred_element_type=jnp.float32))
    lgt = lg2[0:ROUTER_ROWS] + lg2[ROUTER_ROWS:2 * ROUTER_ROWS] + br_ref[...]

    rows = lax.broadcasted_iota(jnp.int32, (EXPERTS_PER_GROUP, tm), 0)
    big = jnp.int32(1 << 20)
    glog = jnp.where(rows < N_GROUPS, lgt[0:EXPERTS_PER_GROUP], -jnp.inf)
    gmax = jnp.max(glog, axis=0, keepdims=True)
    gsel = jnp.min(jnp.where(glog == gmax, rows, big), axis=0, keepdims=True)
    gw = 1.0 / jnp.sum(jnp.exp(glog - gmax), axis=0, keepdims=True)
    esel = jnp.zeros((EXPERTS_PER_GROUP, tm), jnp.float32)
    for grp in range(N_GROUPS):
        lo = EXPERTS_PER_GROUP * (grp + 1)
        esel = jnp.where(gsel == grp, lgt[lo:lo + EXPERTS_PER_GROUP], esel)
    v0 = jnp.max(esel, axis=0, keepdims=True)
    i0 = jnp.min(jnp.where(esel == v0, rows, big), axis=0, keepdims=True)
    rest = jnp.where(rows == i0, -jnp.inf, esel)
    v1 = jnp.max(rest, axis=0, keepdims=True)
    i1 = jnp.min(jnp.where(rest == v1, rows, big), axis=0, keepdims=True)
    e = jnp.exp(v1 - v0)
    w0 = (1.0 / (1.0 + e)) * gw
    w1 = (e / (1.0 + e)) * gw
    eid0 = gsel * EXPERTS_PER_GROUP + i0
    eid1 = gsel * EXPERTS_PER_GROUP + i1
    ri_ref[...] = jnp.where(rows == 0, eid0, jnp.where(rows == 1, eid1, 0))
    rows_t = lax.broadcasted_iota(jnp.int32, (ROUTER_ROWS, tm), 0)
    slab = jnp.where(rows_t == 0, w0, jnp.where(rows_t == 1, w1, 0.0))
    slab = jnp.where(rows_t == 2, eid0.astype(jnp.float32),
                     jnp.where(rows_t == 3, eid1.astype(jnp.float32), slab))
    rc_ref[...] = slab.T[:, 0:EXPERTS_PER_GROUP]
    ids = lax.broadcasted_iota(jnp.int32, (N_EXPERTS, tm), 0)
    member = jnp.where((ids == eid0) | (ids == eid1), 1.0, 0.0)
    cnt_ref[...] = jnp.sum(member, axis=1, keepdims=True).astype(jnp.int32)


def _mix(oa, ob, x2d, g_a, g_b, w_out_bf, g_ffn, wr, br):
    t = x2d.shape[0]
    tm = TOKEN_TILE
    row = lambda w: pl.BlockSpec((tm, w), lambda i: (i, 0))
    full = lambda a: pl.BlockSpec(a.shape, lambda i: (0, 0))
    return pl.pallas_call(
        _mix_kernel,
        out_shape=(jax.ShapeDtypeStruct((t, D_MODEL), jnp.float32),
                   jax.ShapeDtypeStruct((t, D_MODEL), jnp.bfloat16),
                   jax.ShapeDtypeStruct((EXPERTS_PER_GROUP, t), jnp.int32),
                   jax.ShapeDtypeStruct((t, EXPERTS_PER_GROUP), jnp.float32),
                   jax.ShapeDtypeStruct((t // tm, N_EXPERTS, 1), jnp.int32)),
        grid=(t // tm,),
        in_specs=[row(GROUP_WIDTH), row(GROUP_WIDTH), row(D_MODEL),
                  full(g_a), full(g_b), full(w_out_bf), full(g_ffn), full(wr), full(br)],
        out_specs=(row(D_MODEL), row(D_MODEL),
                   pl.BlockSpec((EXPERTS_PER_GROUP, tm), lambda i: (0, i)),
                   row(EXPERTS_PER_GROUP),
                   pl.BlockSpec((None, N_EXPERTS, 1), lambda i: (i, 0, 0))),
        compiler_params=pltpu.CompilerParams(dimension_semantics=("arbitrary",),
                                             vmem_limit_bytes=VMEM_LIMIT),
        name="mix_router",
    )(oa, ob, x2d, g_a, g_b, w_out_bf, g_ffn, wr, br)


RUN_ALIGN = 16
LOCAL_ROWS = 2 * TOKEN_TILE + N_EXPERTS * RUN_ALIGN
MAX_CHUNKS = LOCAL_ROWS // RUN_ALIGN
START_UNROLL = 4
SORT_BLOCK = 256
FINAL_BLOCK = 256


def _local_slots(member, lo_vec, prior, e_sel, axis):
    ids = lax.broadcasted_iota(jnp.int32, member.shape, axis)
    slot = jnp.where(ids == e_sel, lo_vec + prior, 0.0)
    return jnp.sum(slot, axis=axis, keepdims=True).astype(jnp.int32)


def _chunk_copy(dst_ref, tile, c, local, glob, sem, to_global):
    g = pl.multiple_of(dst_ref[tile * MAX_CHUNKS + c], RUN_ALIGN)
    l = pl.multiple_of(c * RUN_ALIGN, RUN_ALIGN)
    src, dst = local.at[pl.ds(l, RUN_ALIGN), :], glob.at[pl.ds(g, RUN_ALIGN), :]
    if not to_global:
        src, dst = dst, src
    return pltpu.make_async_copy(src, dst, sem)


def _start_chunks(nchunk_ref, dst_ref, tile, local, glob, sem, to_global):
    n = nchunk_ref[tile]
    groups = n // START_UNROLL

    def group(gi, carry):
        for u in range(START_UNROLL):
            _chunk_copy(dst_ref, tile, gi * START_UNROLL + u, local, glob, sem, to_global).start()
        return carry

    def single(c, carry):
        _chunk_copy(dst_ref, tile, groups * START_UNROLL + c, local, glob, sem, to_global).start()
        return carry

    lax.fori_loop(0, groups, group, 0)
    lax.fori_loop(0, n % START_UNROLL, single, 0)


def _wait_chunks(nchunk_ref, tile, local, glob, sem):
    n = nchunk_ref[tile]
    size = 1
    while size <= MAX_CHUNKS:
        rows = size * RUN_ALIGN

        @pl.when((n // size) % 2 == 1)
        def _(rows=rows):
            pltpu.make_async_copy(local.at[pl.ds(0, rows), :], glob.at[pl.ds(0, rows), :], sem).wait()

        size *= 2


def _lo_vector(lo_ref, tile, shape, axis):
    ids = lax.broadcasted_iota(jnp.int32, shape, axis)
    vec = jnp.zeros(shape, jnp.float32)
    for e in range(N_EXPERTS):
        vec = jnp.where(ids == e, lo_ref[tile * N_EXPERTS + e].astype(jnp.float32), vec)
    return vec


def _dispatch_kernel(lo_ref, nchunk_ref, dst_ref, pad_tile_ref, nused_ref, ri_ref, h_ref, xs_hbm,
                     upper, sbuf, zbuf, sem, zsem, tail_sem, *, n_tiles):
    i = pl.program_id(0)
    n_steps = pl.num_programs(0)
    slot = i % 2

    def zero_tile(tile, zero_sem):
        rows = pl.ds(pl.multiple_of(tile * ROW_TILE, ROW_TILE), ROW_TILE)
        return pltpu.make_async_copy(zbuf, xs_hbm.at[rows, :], zero_sem)

    @pl.when(i == 0)
    def _():
        r = lax.broadcasted_iota(jnp.int32, (TOKEN_TILE, TOKEN_TILE), 0)
        c = lax.broadcasted_iota(jnp.int32, (TOKEN_TILE, TOKEN_TILE), 1)
        upper[...] = jnp.where(r < c, 1.0, 0.0).astype(jnp.bfloat16)
        zbuf[...] = jnp.zeros_like(zbuf)
        for e in range(N_EXPERTS):
            @pl.when(pad_tile_ref[e] >= 0)
            def _():
                zero_tile(pad_tile_ref[e], zsem).start()
        lax.fori_loop(nused_ref[0], n_tiles, lambda t, c: (zero_tile(t, tail_sem).start(), c)[1], 0)
        for e in range(N_EXPERTS):
            @pl.when(pad_tile_ref[e] >= 0)
            def _():
                zero_tile(pad_tile_ref[e], zsem).wait()

    e0 = ri_ref[0:1, :]
    e1 = ri_ref[1:2, :]
    ids = lax.broadcasted_iota(jnp.int32, (N_EXPERTS, TOKEN_TILE), 0)
    member = jnp.where((ids == e0) | (ids == e1), 1.0, 0.0)
    prior = _dot(member.astype(jnp.bfloat16), upper[...])
    lo_vec = _lo_vector(lo_ref, i, (N_EXPERTS, 1), 0)
    d0 = _local_slots(member, lo_vec, prior, e0, 0)
    d1 = _local_slots(member, lo_vec, prior, e1, 0)
    local = sbuf.at[slot]
    used_rows = nchunk_ref[i] * RUN_ALIGN
    h = h_ref[...]
    for blk in range(LOCAL_ROWS // SORT_BLOCK):
        @pl.when(blk * SORT_BLOCK < used_rows)
        def _():
            rows = blk * SORT_BLOCK + lax.broadcasted_iota(jnp.int32, (SORT_BLOCK, TOKEN_TILE), 0)
            select = jnp.where((rows == d0) | (rows == d1), 1.0, 0.0).astype(jnp.bfloat16)
            local[blk * SORT_BLOCK:(blk + 1) * SORT_BLOCK, :] = _dot(select, h).astype(jnp.bfloat16)

    @pl.when(i > 0)
    def _():
        _wait_chunks(nchunk_ref, i - 1, sbuf.at[1 - slot], xs_hbm, sem)

    _start_chunks(nchunk_ref, dst_ref, i, local, xs_hbm, sem, True)

    @pl.when(i == n_steps - 1)
    def _():
        _wait_chunks(nchunk_ref, i, local, xs_hbm, sem)
        lax.fori_loop(nused_ref[0], n_tiles, lambda t, c: (zero_tile(t, tail_sem).wait(), c)[1], 0)


def _dispatch(lo, nchunk, dst, pad_tile, nused, route_i, h2, p_rows):
    n_tok = h2.shape[0]
    grid_spec = pltpu.PrefetchScalarGridSpec(
        num_scalar_prefetch=5,
        grid=(n_tok // TOKEN_TILE,),
        in_specs=[pl.BlockSpec((EXPERTS_PER_GROUP, TOKEN_TILE), lambda i, *_: (0, i)),
                  pl.BlockSpec((TOKEN_TILE, D_MODEL), lambda i, *_: (i, 0))],
        out_specs=pl.BlockSpec(memory_space=pl.ANY),
        scratch_shapes=[pltpu.VMEM((TOKEN_TILE, TOKEN_TILE), jnp.bfloat16),
                        pltpu.VMEM((2, LOCAL_ROWS, D_MODEL), jnp.bfloat16),
                        pltpu.VMEM((ROW_TILE, D_MODEL), jnp.bfloat16),
                        pltpu.SemaphoreType.DMA, pltpu.SemaphoreType.DMA, pltpu.SemaphoreType.DMA],
    )
    return pl.pallas_call(
        functools.partial(_dispatch_kernel, n_tiles=p_rows // ROW_TILE),
        out_shape=jax.ShapeDtypeStruct((p_rows, D_MODEL), jnp.bfloat16),
        grid_spec=grid_spec,
        compiler_params=pltpu.CompilerParams(dimension_semantics=("arbitrary",),
                                             vmem_limit_bytes=VMEM_LIMIT),
        name="dispatch",
    )(lo, nchunk, dst, pad_tile, nused, route_i, h2)


def _moe_kernel(texp_ref, first_ref, nused_ref, slot_ref, next_ref, xs_ref, wg_hbm, wu_hbm, wd_hbm,
                y_ref, wg_f32, wu_f32, wd_f32, wg_bf, wu_bf, wd_bf, sems):
    j = pl.program_id(0)

    def fetch(expert, slot):
        return [pltpu.make_async_copy(src.at[expert], dst.at[slot], sems.at[slot, k])
                for k, (src, dst) in enumerate(((wg_hbm, wg_f32), (wu_hbm, wu_f32), (wd_hbm, wd_f32)))]

    @pl.when(j == 0)
    def _():
        for cp in fetch(texp_ref[0], 0):
            cp.start()

    @pl.when(j < nused_ref[0])
    def _():
        @pl.when(first_ref[j] == 1)
        def _():
            for s in range(2):
                @pl.when(slot_ref[j] == s)
                def _():
                    for cp in fetch(texp_ref[j], s):
                        cp.wait()
                    wg_bf[...] = wg_f32[s].astype(jnp.bfloat16)
                    wu_bf[...] = wu_f32[s].astype(jnp.bfloat16)
                    wd_bf[...] = wd_f32[s].astype(jnp.bfloat16)

                    @pl.when(next_ref[j] >= 0)
                    def _():
                        for cp in fetch(next_ref[j], 1 - s):
                            cp.start()

        x = xs_ref[...]
        a = _dot(x, wg_bf[...])
        u = _dot(x, wu_bf[...])
        hid = (a * (1.0 / (1.0 + jnp.exp(-a)))) * u
        y_ref[...] = _dot(hid.astype(jnp.bfloat16), wd_bf[...]).astype(jnp.bfloat16)

    @pl.when(j >= nused_ref[0])
    def _():
        y_ref[...] = jnp.zeros_like(y_ref)


def _moe(texp, first, nused, slot, nxt, xs, w_gate, w_up, w_down):
    p_rows = xs.shape[0]
    n_tiles = p_rows // ROW_TILE
    used = lambda j, nu: jnp.minimum(j, nu[0] - 1)
    grid_spec = pltpu.PrefetchScalarGridSpec(
        num_scalar_prefetch=5,
        grid=(n_tiles,),
        in_specs=[
            pl.BlockSpec((ROW_TILE, D_MODEL), lambda j, te, fi, nu, sl, nx: (used(j, nu), 0)),
            pl.BlockSpec(memory_space=pl.ANY),
            pl.BlockSpec(memory_space=pl.ANY),
            pl.BlockSpec(memory_space=pl.ANY),
        ],
        out_specs=pl.BlockSpec((ROW_TILE, D_MODEL), lambda j, te, fi, nu, sl, nx: (j, 0)),
        scratch_shapes=[pltpu.VMEM((2, D_MODEL, D_EXPERT), jnp.float32),
                        pltpu.VMEM((2, D_MODEL, D_EXPERT), jnp.float32),
                        pltpu.VMEM((2, D_EXPERT, D_MODEL), jnp.float32),
                        pltpu.VMEM((D_MODEL, D_EXPERT), jnp.bfloat16),
                        pltpu.VMEM((D_MODEL, D_EXPERT), jnp.bfloat16),
                        pltpu.VMEM((D_EXPERT, D_MODEL), jnp.bfloat16),
                        pltpu.SemaphoreType.DMA((2, 3))],
    )
    return pl.pallas_call(
        _moe_kernel,
        out_shape=jax.ShapeDtypeStruct((p_rows, D_MODEL), jnp.bfloat16),
        grid_spec=grid_spec,
        compiler_params=pltpu.CompilerParams(dimension_semantics=("arbitrary",),
                                             vmem_limit_bytes=VMEM_LIMIT),
        name="moe_experts",
    )(texp, first, nused, slot, nxt, xs, w_gate, w_up, w_down)


def _final_kernel(lo_ref, nchunk_ref, dst_ref, y_hbm, x1_ref, rc_ref, p_ref, gple_ref, wg_ref, wp_ref,
                  gfin_ref, out_ref, lower, ybuf, sems):
    i = pl.program_id(0)
    n_steps = pl.num_programs(0)
    slot = i % 2

    @pl.when(i == 0)
    def _():
        r = lax.broadcasted_iota(jnp.int32, (TOKEN_TILE, TOKEN_TILE), 0)
        c = lax.broadcasted_iota(jnp.int32, (TOKEN_TILE, TOKEN_TILE), 1)
        lower[...] = jnp.where(c < r, 1.0, 0.0).astype(jnp.bfloat16)
        ybuf[...] = jnp.zeros_like(ybuf)
        _start_chunks(nchunk_ref, dst_ref, 0, ybuf.at[0], y_hbm, sems.at[0], False)

    @pl.when(i + 1 < n_steps)
    def _():
        _start_chunks(nchunk_ref, dst_ref, i + 1, ybuf.at[1 - slot], y_hbm, sems.at[1 - slot], False)

    rc = rc_ref[...]
    w0, w1 = rc[:, 0:1], rc[:, 1:2]
    e0, e1 = rc[:, 2:3].astype(jnp.int32), rc[:, 3:4].astype(jnp.int32)
    ids = lax.broadcasted_iota(jnp.int32, (TOKEN_TILE, ROUTER_ROWS), 1)
    member = jnp.where((ids == e0) | (ids == e1), 1.0, 0.0)
    prior = _dot(lower[...], member.astype(jnp.bfloat16))
    lo_vec = _lo_vector(lo_ref, i, (1, ROUTER_ROWS), 1)
    d0 = _local_slots(member, lo_vec, prior, e0, 1)
    d1 = _local_slots(member, lo_vec, prior, e1, 1)

    local = ybuf.at[slot]
    _wait_chunks(nchunk_ref, i, local, y_hbm, sems.at[slot])
    for blk in range(TOKEN_TILE // FINAL_BLOCK):
        rows = slice(blk * FINAL_BLOCK, (blk + 1) * FINAL_BLOCK)
        cols = lax.broadcasted_iota(jnp.int32, (FINAL_BLOCK, LOCAL_ROWS), 1)
        pick = jnp.where(cols == d0[rows], w0[rows], jnp.where(cols == d1[rows], w1[rows], 0.0))
        x2 = x1_ref[rows, :] + _dot(pick.astype(jnp.bfloat16), local[...])
        z = _dot(_rms(x2, gple_ref[...]).astype(jnp.bfloat16), wg_ref[...])
        gate = 1.0 / (1.0 + jnp.exp(-z))
        x3 = x2 + _dot(p_ref[rows, :].astype(jnp.bfloat16), wp_ref[...]) * gate
        out_ref[rows, :] = _rms(x3, gfin_ref[...])


def _final(lo, nchunk, dst, y, x1, rc, p2d, g_ple, w_gate_bf, w_proj_bf, g_final):
    t = x1.shape[0]
    tm = TOKEN_TILE
    row = lambda w: pl.BlockSpec((tm, w), lambda i, *_: (i, 0))
    full = lambda a: pl.BlockSpec(a.shape, lambda i, *_: (0, 0))
    grid_spec = pltpu.PrefetchScalarGridSpec(
        num_scalar_prefetch=3,
        grid=(t // tm,),
        in_specs=[pl.BlockSpec(memory_space=pl.ANY), row(D_MODEL), row(EXPERTS_PER_GROUP),
                  row(PLE_DIM), full(g_ple), full(w_gate_bf), full(w_proj_bf), full(g_final)],
        out_specs=row(D_MODEL),
        scratch_shapes=[pltpu.VMEM((TOKEN_TILE, TOKEN_TILE), jnp.bfloat16),
                        pltpu.VMEM((2, LOCAL_ROWS, D_MODEL), jnp.bfloat16),
                        pltpu.SemaphoreType.DMA((2,))],
    )
    return pl.pallas_call(
        _final_kernel,
        out_shape=jax.ShapeDtypeStruct((t, D_MODEL), jnp.float32),
        grid_spec=grid_spec,
        compiler_params=pltpu.CompilerParams(dimension_semantics=("arbitrary",),
                                             vmem_limit_bytes=VMEM_LIMIT),
        name="combine_ple_final",
    )(lo, nchunk, dst, y, x1, rc, p2d, g_ple, w_gate_bf, w_proj_bf, g_final)


def _dispatch_tables(counts):
    n_tt = counts.shape[0]
    p_rows = 2 * n_tt * TOKEN_TILE + n_tt * N_EXPERTS * (RUN_ALIGN - 1) + N_EXPERTS * ROW_TILE
    p_rows = -(-p_rows // ROW_TILE) * ROW_TILE
    n_tiles = p_rows // ROW_TILE
    n8 = (counts + RUN_ALIGN - 1) // RUN_ALIGN * RUN_ALIGN
    lo = jnp.cumsum(n8, axis=1) - n8
    total = jnp.sum(n8, axis=0)
    tiles_per = (total + ROW_TILE - 1) // ROW_TILE
    tile_end = jnp.cumsum(tiles_per)
    tile_start = tile_end - tiles_per
    g = tile_start[None, :] * ROW_TILE + (jnp.cumsum(n8, axis=0) - n8)
    n_used = tile_end[-1]
    tile_ids = jnp.arange(n_tiles, dtype=jnp.int32)
    clamped = jnp.minimum(tile_ids, n_used - 1)
    texp = jnp.sum((tile_end[None, :] <= clamped[:, None]).astype(jnp.int32), axis=1)
    experts = jnp.arange(N_EXPERTS, dtype=jnp.int32)
    of_tile = texp[:, None] == experts[None, :]
    per_tile = lambda v: jnp.sum(jnp.where(of_tile, v[None, :], 0), axis=1)
    first = ((tile_ids == per_tile(tile_start)) & (tile_ids < n_used)).astype(jnp.int32)
    pad_tile = jnp.where(total > 0, tile_end - 1, -1).astype(jnp.int32)
    has_rows = total > 0
    later = has_rows[None, :] & (experts[None, :] > experts[:, None])
    next_expert = jnp.min(jnp.where(later, experts[None, :], N_EXPERTS), axis=1)
    next_expert = jnp.where(next_expert < N_EXPERTS, next_expert, -1)
    slot = (jnp.cumsum(has_rows.astype(jnp.int32)) - 1) % 2
    chunk_row = jnp.arange(MAX_CHUNKS, dtype=jnp.int32)[None, :, None] * RUN_ALIGN
    in_run = (chunk_row >= lo[:, None, :]) & (chunk_row < (lo + n8)[:, None, :])
    dst = jnp.sum(jnp.where(in_run, g[:, None, :] + chunk_row - lo[:, None, :], 0), axis=2)
    nchunk = jnp.sum(n8, axis=1) // RUN_ALIGN
    i32 = lambda a: a.reshape(-1).astype(jnp.int32)
    return (texp, first, n_used.reshape(1).astype(jnp.int32), i32(per_tile(slot)),
            i32(per_tile(next_expert)), pad_tile, i32(lo), i32(nchunk), i32(dst), p_rows)


def kernel(x, p, g_mix, w_in, sink, g_grp_a, g_grp_b, w_out, g_ffn, w_router_group, b_router_group,
           w_router_expert, b_router_expert, w_expert_gate, w_expert_up, w_expert_down, g_ple,
           w_ple_gate, w_ple_proj, g_final):
    batch, seq, _ = x.shape
    n_tok = batch * seq
    depth = w_in.shape[0]
    assert depth == 1, "the final RMSNorm is fused into the last layer's epilogue"
    bf = jnp.bfloat16
    xc = x.reshape(n_tok, D_MODEL)
    for i in range(depth):
        *qkv_a, qb, kb, vb = _proj(xc, g_mix[i][None, :], w_in[i].astype(bf), batch, seq)
        oa = _attn_a(qkv_a, batch, seq)
        ob = _attn_b(qb, kb, vb, sink[i], batch, seq)

        wr = jnp.zeros((ROUTER_ROWS, D_MODEL), jnp.float32)
        wr = wr.at[0:N_GROUPS].set(w_router_group[i].T)
        wr = wr.at[EXPERTS_PER_GROUP:EXPERTS_PER_GROUP + N_EXPERTS].set(w_router_expert[i].T)
        br = jnp.zeros((ROUTER_ROWS, 1), jnp.float32)
        br = br.at[0:N_GROUPS, 0].set(b_router_group[i])
        br = br.at[EXPERTS_PER_GROUP:EXPERTS_PER_GROUP + N_EXPERTS, 0].set(b_router_expert[i])

        x1, h2, route_i, route_c, counts = _mix(
            oa, ob, xc, g_grp_a[i][None, :], g_grp_b[i][None, :], w_out[i].astype(bf), g_ffn[i][None, :], wr, br)

        texp, first, nused, slot, nxt, pad_tile, lo, nchunk, dst, p_rows = _dispatch_tables(counts[:, :, 0])
        xs = _dispatch(lo, nchunk, dst, pad_tile, nused, route_i, h2, p_rows)
        y = _moe(texp, first, nused, slot, nxt, xs, w_expert_gate[i], w_expert_up[i], w_expert_down[i])
        xc = _final(lo, nchunk, dst, y, x1, route_c, p[i].reshape(n_tok, PLE_DIM), g_ple[i][None, :],
                    w_ple_gate[i].astype(bf), w_ple_proj[i].astype(bf), g_final[None, :])
    return xc.reshape(batch, seq, D_MODEL)
```

```python
import functools

import jax
import jax.numpy as jnp
from jax import lax
from jax.experimental import pallas as pl
from jax.experimental.pallas import tpu as pltpu

D_MODEL = 1024
HEAD_DIM = 64
GROUP_WIDTH = 512
N_HEADS = 8
B_KV_HEADS = 2
B_KV_WIDTH = B_KV_HEADS * HEAD_DIM
A_CONFIGS = ((128, 1), (512, 4), (2048, 16))
B_HALF_WINDOW = 128
N_GROUPS = 4
EXPERTS_PER_GROUP = 8
N_EXPERTS = N_GROUPS * EXPERTS_PER_GROUP
D_EXPERT = 512
PLE_DIM = 256
EPS = 1e-6
NEG = -1e30
LOG2E = 1.4426950408889634

LANES = 128
Q_BLOCK = 128
HEADS_PER_PASS = 2
PASS_WIDTH = HEADS_PER_PASS * HEAD_DIM
ROW_TILE = 448
TOKEN_TILE = 512
ROUTER_ROWS = 128
VMEM_LIMIT = 48 * 1024 * 1024
ATTN_VMEM_LIMIT = 56 * 1024 * 1024


def _rms(x, g):
    r = lax.rsqrt(jnp.mean(x * x, axis=-1, keepdims=True) + EPS)
    return (x * r) * g


def _dot(a, b):
    return jnp.dot(a, b, preferred_element_type=jnp.float32)


PROJ_TILE = 1024


def _proj_kernel(x_ref, g_ref, w_ref, *refs):
    a_refs, (qb_ref, kb_ref, vb_ref, res, *stage_refs) = refs[:9], refs[9:]
    h = _rms(x_ref[...], g_ref[...]).astype(jnp.bfloat16)
    scale = HEAD_DIM ** -0.5 * LOG2E
    W = GROUP_WIDTH
    tiles = GROUP_WIDTH // PASS_WIDTH
    qkv_a = (_dot(h, w_ref[:, 0:W]) * scale, _dot(h, w_ref[:, W:2 * W]), _dot(h, w_ref[:, 2 * W:3 * W]))
    for part in range(3):
        for pair in range(tiles):
            res[part * tiles + pair, 0] = qkv_a[part][:, pair * PASS_WIDTH:(pair + 1) * PASS_WIDTH]
    qkv_b = _dot(h, w_ref[:, 3 * W:4 * W + 2 * B_KV_WIDTH])
    qb_ref[...] = (qkv_b[:, 0:W] * scale).astype(jnp.bfloat16)
    kb_ref[...] = qkv_b[:, W:W + B_KV_WIDTH].astype(jnp.bfloat16)
    vb_ref[...] = qkv_b[:, W + B_KV_WIDTH:W + 2 * B_KV_WIDTH].astype(jnp.bfloat16)
    stages = (res,) + tuple(stage_refs)
    prev_dil = 1
    for c, (_, dil) in enumerate(A_CONFIGS):
        ratio = dil // prev_dil
        rows = PROJ_TILE // dil
        for idx in range(3 * tiles):
            part, pair = idx // tiles, idx % tiles
            out = a_refs[3 * c + part]
            for rho in range(dil):
                if c == 0:
                    sub = res[idx, 0]
                else:
                    sub = stages[c - 1].at[idx, rho % prev_dil][pl.ds(rho // prev_dil, rows, stride=ratio), :]
                    if c < len(A_CONFIGS) - 1:
                        stages[c][idx, rho] = sub
                out[pair, rho] = sub.astype(jnp.bfloat16)
        prev_dil = dil


def _proj(x2d, g_mix, w_in_bf, batch, seq):
    t = x2d.shape[0]
    tm = PROJ_TILE
    tiles_per_seq = seq // tm
    in_width = w_in_bf.shape[1]
    n_pairs = N_HEADS // HEADS_PER_PASS
    row = lambda w: pl.BlockSpec((tm, w), lambda i: (i, 0))
    a_shapes, a_specs = [], []
    for _, dil in A_CONFIGS:
        for _ in range(3):
            a_shapes.append(jax.ShapeDtypeStruct((batch, n_pairs, dil, seq // dil, PASS_WIDTH),
                                                 jnp.bfloat16))
            a_specs.append(pl.BlockSpec((None, n_pairs, dil, tm // dil, PASS_WIDTH),
                                        lambda i: (i // tiles_per_seq, 0, 0, i % tiles_per_seq, 0)))
    wide = jax.ShapeDtypeStruct((t, GROUP_WIDTH), jnp.bfloat16)
    narrow = jax.ShapeDtypeStruct((t, B_KV_WIDTH), jnp.bfloat16)
    return pl.pallas_call(
        _proj_kernel,
        out_shape=tuple(a_shapes) + (wide, narrow, narrow),
        grid=(t // tm,),
        in_specs=[row(D_MODEL),
                  pl.BlockSpec((1, D_MODEL), lambda i: (0, 0)),
                  pl.BlockSpec((D_MODEL, in_width), lambda i: (0, 0), pipeline_mode=pl.Buffered(1))],
        out_specs=tuple(a_specs) + (row(GROUP_WIDTH), row(B_KV_WIDTH), row(B_KV_WIDTH)),
        scratch_shapes=[pltpu.VMEM((3 * GROUP_WIDTH // PASS_WIDTH, dil, tm // dil, PASS_WIDTH), jnp.float32)
                        for _, dil in A_CONFIGS[:-1]],
        compiler_params=pltpu.CompilerParams(dimension_semantics=("arbitrary",),
                                             vmem_limit_bytes=ATTN_VMEM_LIMIT),
        name="proj",
    )(x2d, g_mix, w_in_bf)


def _fill_bias(bias_ref, half, nkeys, dist_scale):
    i = lax.broadcasted_iota(jnp.int32, (Q_BLOCK, nkeys), 0)
    j = lax.broadcasted_iota(jnp.int32, (Q_BLOCK, nkeys), 1)
    for var, rel0 in enumerate((0, -half, Q_BLOCK - nkeys)):
        d = jnp.abs(j + rel0 - i)
        dist = d.astype(jnp.float32) * dist_scale
        for h in range(N_HEADS):
            slope = 2.0 ** (-(h + 1))
            bias_ref[var, h] = jnp.where(d <= half, -(slope * dist) * LOG2E, NEG)


def _key_window(q0, seq_len, half, nkeys):
    lo = q0 - half
    ks = pl.multiple_of(jnp.clip(lo, 0, seq_len - nkeys), HEAD_DIM)
    var = jnp.where(lo < 0, 0, jnp.where(lo > seq_len - nkeys, 2, 1))
    return ks, var


def _fold_lane_tiles(a, op):
    out = a[..., 0:LANES]
    for t in range(1, a.shape[-1] // LANES):
        out = op(out, a[..., t * LANES:(t + 1) * LANES])
    return out


def _attend_pass(q, k, v, bias, sinks):
    nkeys = k.shape[0]
    lane_head = lax.broadcasted_iota(jnp.int32, (Q_BLOCK, PASS_WIDTH), 1) // HEAD_DIM
    zero = jnp.zeros_like(q)
    q_stack = jnp.concatenate(
        [jnp.where(lane_head == h, q, zero) for h in range(HEADS_PER_PASS)], axis=0)
    s = lax.dot_general(q_stack, k, (((1,), (1,)), ((), ())), preferred_element_type=jnp.float32)
    s = s.reshape(HEADS_PER_PASS, Q_BLOCK, nkeys) + bias
    m_tile = _fold_lane_tiles(s, jnp.maximum)
    if sinks is not None:
        tile_head = lax.broadcasted_iota(jnp.int32, m_tile.shape, 0)
        tile_lane = lax.broadcasted_iota(jnp.int32, m_tile.shape, 2)
        sk = jnp.zeros(m_tile.shape, jnp.float32)
        for h in range(HEADS_PER_PASS):
            sk = jnp.where(tile_head == h, sinks[h] * LOG2E, sk)
        m_tile = jnp.maximum(m_tile, sk)
    m = jnp.max(m_tile, axis=-1, keepdims=True)
    p = jnp.exp2(s - m)
    den_tile = _fold_lane_tiles(p, jnp.add)
    if sinks is not None:
        den_tile = den_tile + jnp.where(tile_lane == 0, jnp.exp2(sk - m), 0.0)
    den = jnp.sum(den_tile, axis=-1, keepdims=True)
    pv = _dot(p.reshape(HEADS_PER_PASS * Q_BLOCK, nkeys).astype(jnp.bfloat16), v)
    pv = pv.reshape(HEADS_PER_PASS, Q_BLOCK, PASS_WIDTH)
    shape = (Q_BLOCK, PASS_WIDTH)
    o, m_lanes, den_lanes = (jnp.zeros(shape, jnp.float32) for _ in range(3))
    for h in range(HEADS_PER_PASS):
        o = jnp.where(lane_head == h, pv[h], o)
        m_lanes = jnp.where(lane_head == h, m[h], m_lanes)
        den_lanes = jnp.where(lane_head == h, den[h], den_lanes)
    return o, m_lanes, den_lanes


A_HALF = 64
A_KEYS = Q_BLOCK + 2 * A_HALF
PASSES_PER_BODY = 32
MERGE_BLOCK = 256


def _attn_a_kernel(*refs, seq):
    n_cfg = len(A_CONFIGS)
    qkv = refs[:3 * n_cfg]
    o_ref, bias_ref = refs[3 * n_cfg], refs[3 * n_cfg + 1]
    scratch = refs[3 * n_cfg + 2:]
    pair = pl.program_id(1)

    @pl.when((pl.program_id(0) == 0) & (pair == 0))
    def _():
        for c, (_, dil) in enumerate(A_CONFIGS):
            _fill_bias(bias_ref.at[c], A_HALF, A_KEYS, float(dil))

    for c, (window, dil) in enumerate(A_CONFIGS):
        assert window // (2 * dil) == A_HALF
        q_ref, k_ref, v_ref = qkv[3 * c:3 * c + 3]
        o_sc, m_sc, d_sc = scratch[3 * c:3 * c + 3]
        sub_len = seq // dil
        blocks = sub_len // Q_BLOCK

        def body(idx, carry, q_ref=q_ref, k_ref=k_ref, v_ref=v_ref, o_sc=o_sc, m_sc=m_sc, d_sc=d_sc, c=c,
                 dil=dil, sub_len=sub_len, blocks=blocks):
            rho = idx // blocks
            q0 = pl.multiple_of((idx % blocks) * Q_BLOCK, Q_BLOCK)
            ks, var = _key_window(q0, sub_len, A_HALF, A_KEYS)
            bias = bias_ref[c, var, pl.ds(pair * HEADS_PER_PASS, HEADS_PER_PASS)]
            o, m, den = _attend_pass(q_ref[rho, pl.ds(q0, Q_BLOCK), :], k_ref[rho, pl.ds(ks, A_KEYS), :],
                                     v_ref[rho, pl.ds(ks, A_KEYS), :], bias, None)
            tokens = pl.ds(rho + dil * q0, Q_BLOCK, stride=dil) if dil > 1 else pl.ds(q0, Q_BLOCK)
            o_sc[tokens, :] = o
            m_sc[tokens, :] = m
            d_sc[tokens, :] = den
            return carry

        lax.fori_loop(0, dil * blocks, body, 0, unroll=PASSES_PER_BODY)

    n_cfg = len(A_CONFIGS)

    def merge(blk, carry):
        rows = pl.ds(pl.multiple_of(blk * MERGE_BLOCK, MERGE_BLOCK), MERGE_BLOCK)
        pvs = [scratch[3 * c][rows, :] for c in range(n_cfg)]
        ms = [scratch[3 * c + 1][rows, :] for c in range(n_cfg)]
        dens = [scratch[3 * c + 2][rows, :] for c in range(n_cfg)]
        mx = functools.reduce(jnp.maximum, ms)
        es = [jnp.exp2(m - mx) for m in ms]
        total = functools.reduce(jnp.add, [e * d for e, d in zip(es, dens)])
        o_ref[rows, :] = functools.reduce(jnp.add, [e * pv for e, pv in zip(es, pvs)]) * (1.0 / total)
        return carry

    lax.fori_loop(0, seq // MERGE_BLOCK, merge, 0, unroll=2)


def _attn_a(qkv, batch, seq):
    n_pairs = N_HEADS // HEADS_PER_PASS
    in_specs, scratch = [], [pltpu.VMEM((len(A_CONFIGS), 3, N_HEADS, Q_BLOCK, A_KEYS), jnp.float32)]
    for _, dil in A_CONFIGS:
        blk = (None, None, dil, seq // dil, PASS_WIDTH)
        in_specs += [pl.BlockSpec(blk, lambda b, p: (b, p, 0, 0, 0))] * 3
        scratch += [pltpu.VMEM((seq, PASS_WIDTH), jnp.float32)] * 3
    return pl.pallas_call(
        functools.partial(_attn_a_kernel, seq=seq),
        out_shape=jax.ShapeDtypeStruct((batch, seq, GROUP_WIDTH), jnp.float32),
        grid=(batch, n_pairs),
        in_specs=in_specs,
        out_specs=pl.BlockSpec((None, seq, PASS_WIDTH), lambda b, p: (b, 0, p)),
        scratch_shapes=scratch,
        compiler_params=pltpu.CompilerParams(dimension_semantics=("arbitrary", "arbitrary"),
                                             vmem_limit_bytes=ATTN_VMEM_LIMIT),
        name="attn_a",
    )(*qkv).reshape(batch * seq, GROUP_WIDTH)


def _attn_b_kernel(sink_ref, q_ref, k_ref, v_ref, o_ref, bias_ref, k4_ref, v4_ref, *, half, nkeys,
                   seq_len, q_tile):
    first = (pl.program_id(0) == 0) & (pl.program_id(1) == 0)

    @pl.when(first)
    def _():
        _fill_bias(bias_ref, half, nkeys, 1.0)

    @pl.when(pl.program_id(1) == 0)
    def _():
        c = lax.broadcasted_iota(jnp.int32, (B_KV_WIDTH, GROUP_WIDTH), 0)
        j = lax.broadcasted_iota(jnp.int32, (B_KV_WIDTH, GROUP_WIDTH), 1)
        src = (j // (GROUP_WIDTH // B_KV_HEADS)) * HEAD_DIM + j % HEAD_DIM
        rep = jnp.where(c == src, 1.0, 0.0).astype(jnp.bfloat16)
        chunk = 512

        def body(i, carry):
            rows = pl.ds(pl.multiple_of(i * chunk, chunk), chunk)
            k4_ref[rows, :] = _dot(k_ref[rows, :], rep).astype(jnp.bfloat16)
            v4_ref[rows, :] = _dot(v_ref[rows, :], rep).astype(jnp.bfloat16)
            return carry

        lax.fori_loop(0, seq_len // chunk, body, 0)

    base = pl.program_id(1) * q_tile

    def body(jb, carry):
        row0 = pl.multiple_of(jb * Q_BLOCK, Q_BLOCK)
        ks, var = _key_window(base + row0, seq_len, half, nkeys)
        for g in range(N_HEADS // HEADS_PER_PASS):
            cols = slice(g * PASS_WIDTH, (g + 1) * PASS_WIDTH)
            heads = pl.ds(g * HEADS_PER_PASS, HEADS_PER_PASS)
            sinks = [sink_ref[g * HEADS_PER_PASS + h] for h in range(HEADS_PER_PASS)]
            o, _, den = _attend_pass(q_ref[pl.ds(row0, Q_BLOCK), cols], k4_ref[pl.ds(ks, nkeys), cols],
                                     v4_ref[pl.ds(ks, nkeys), cols], bias_ref[var, heads], sinks)
            o_ref[pl.ds(row0, Q_BLOCK), cols] = (o * (1.0 / den)).astype(o_ref.dtype)
        return carry

    lax.fori_loop(0, q_tile // Q_BLOCK, body, 0,
                  unroll=PASSES_PER_BODY // (N_HEADS // HEADS_PER_PASS))


def _attn_b(q, k, v, sink, batch, seq):
    half = B_HALF_WINDOW
    nkeys = Q_BLOCK + 2 * half
    q_tile = 1024
    view = lambda a: a.reshape(batch, seq, a.shape[-1])
    kern = functools.partial(_attn_b_kernel, half=half, nkeys=nkeys, seq_len=seq, q_tile=q_tile)
    o = pl.pallas_call(
        kern,
        out_shape=jax.ShapeDtypeStruct((batch, seq, GROUP_WIDTH), jnp.bfloat16),
        grid=(batch, seq // q_tile),
        in_specs=[pl.BlockSpec(memory_space=pltpu.SMEM),
                  pl.BlockSpec((None, q_tile, GROUP_WIDTH), lambda b, i: (b, i, 0)),
                  pl.BlockSpec((None, seq, B_KV_WIDTH), lambda b, i: (b, 0, 0)),
                  pl.BlockSpec((None, seq, B_KV_WIDTH), lambda b, i: (b, 0, 0))],
        out_specs=pl.BlockSpec((None, q_tile, GROUP_WIDTH), lambda b, i: (b, i, 0)),
        scratch_shapes=[pltpu.VMEM((3, N_HEADS, Q_BLOCK, nkeys), jnp.float32),
                        pltpu.VMEM((seq, GROUP_WIDTH), jnp.bfloat16),
                        pltpu.VMEM((seq, GROUP_WIDTH), jnp.bfloat16)],
        compiler_params=pltpu.CompilerParams(dimension_semantics=("arbitrary", "arbitrary"),
                                             vmem_limit_bytes=VMEM_LIMIT),
        name="attn_b",
    )(sink, view(q), view(k), view(v))
    return o.reshape(batch * seq, GROUP_WIDTH)


def _mix_kernel(oa_ref, ob_ref, x_ref, ga_ref, gb_ref, wout_ref, gffn_ref, wr_ref, br_ref,
                x1_ref, h2_ref, ri_ref, rc_ref, cnt_ref):
    tm = x_ref.shape[0]
    na = _rms(oa_ref[...], ga_ref[...]).astype(jnp.bfloat16)
    nb = _rms(ob_ref[...].astype(jnp.float32), gb_ref[...]).astype(jnp.bfloat16)
    x1 = x_ref[...] + _dot(jnp.concatenate([na, nb], axis=1), wout_ref[...])
    x1_ref[...] = x1
    h2 = _rms(x1, gffn_ref[...])
    h2_ref[...] = h2.astype(jnp.bfloat16)

    h_hi = h2.astype(jnp.bfloat16)
    h_lo = (h2 - h_hi.astype(jnp.float32)).astype(jnp.bfloat16)
    wr = wr_ref[...]
    w_hi = wr.astype(jnp.bfloat16)
    w_lo = (wr - w_hi.astype(jnp.float32)).astype(jnp.bfloat16)
    w_cat = jnp.concatenate([w_hi, w_lo], axis=0)
    nt = (((1,), (1,)), ((), ()))
    lg2 = (lax.dot_general(w_cat, h_hi, nt, preferred_element_type=jnp.float32)
           + lax.dot_general(w_cat, h_lo, nt, preferred_element_type=jnp.float32))
    lgt = lg2[0:ROUTER_ROWS] + lg2[ROUTER_ROWS:2 * ROUTER_ROWS] + br_ref[...]

    rows = lax.broadcasted_iota(jnp.int32, (EXPERTS_PER_GROUP, tm), 0)
    big = jnp.int32(1 << 20)
    glog = jnp.where(rows < N_GROUPS, lgt[0:EXPERTS_PER_GROUP], -jnp.inf)
    gmax = jnp.max(glog, axis=0, keepdims=True)
    gsel = jnp.min(jnp.where(glog == gmax, rows, big), axis=0, keepdims=True)
    gw = 1.0 / jnp.sum(jnp.exp(glog - gmax), axis=0, keepdims=True)
    esel = jnp.zeros((EXPERTS_PER_GROUP, tm), jnp.float32)
    for grp in range(N_GROUPS):
        lo = EXPERTS_PER_GROUP * (grp + 1)
        esel = jnp.where(gsel == grp, lgt[lo:lo + EXPERTS_PER_GROUP], esel)
    v0 = jnp.max(esel, axis=0, keepdims=True)
    i0 = jnp.min(jnp.where(esel == v0, rows, big), axis=0, keepdims=True)
    rest = jnp.where(rows == i0, -jnp.inf, esel)
    v1 = jnp.max(rest, axis=0, keepdims=True)
    i1 = jnp.min(jnp.where(rest == v1, rows, big), axis=0, keepdims=True)
    e = jnp.exp(v1 - v0)
    w0 = (1.0 / (1.0 + e)) * gw
    w1 = (e / (1.0 + e)) * gw
    eid0 = gsel * EXPERTS_PER_GROUP + i0
    eid1 = gsel * EXPERTS_PER_GROUP + i1
    ri_ref[...] = jnp.where(rows == 0, eid0, jnp.where(rows == 1, eid1, 0))
    rows_t = lax.broadcasted_iota(jnp.int32, (ROUTER_ROWS, tm), 0)
    slab = jnp.where(rows_t == 0, w0, jnp.where(rows_t == 1, w1, 0.0))
    slab = jnp.where(rows_t == 2, eid0.astype(jnp.float32),
                     jnp.where(rows_t == 3, eid1.astype(jnp.float32), slab))
    rc_ref[...] = slab.T[:, 0:EXPERTS_PER_GROUP]
    ids = lax.broadcasted_iota(jnp.int32, (N_EXPERTS, tm), 0)
    member = jnp.where((ids == eid0) | (ids == eid1), 1.0, 0.0)
    cnt_ref[...] = jnp.sum(member, axis=1, keepdims=True).astype(jnp.int32)


def _mix(oa, ob, x2d, g_a, g_b, w_out_bf, g_ffn, wr, br):
    t = x2d.shape[0]
    tm = TOKEN_TILE
    row = lambda w: pl.BlockSpec((tm, w), lambda i: (i, 0))
    full = lambda a: pl.BlockSpec(a.shape, lambda i: (0, 0))
    return pl.pallas_call(
        _mix_kernel,
        out_shape=(jax.ShapeDtypeStruct((t, D_MODEL), jnp.float32),
                   jax.ShapeDtypeStruct((t, D_MODEL), jnp.bfloat16),
                   jax.ShapeDtypeStruct((EXPERTS_PER_GROUP, t), jnp.int32),
                   jax.ShapeDtypeStruct((t, EXPERTS_PER_GROUP), jnp.float32),
                   jax.ShapeDtypeStruct((t // tm, N_EXPERTS, 1), jnp.int32)),
        grid=(t // tm,),
        in_specs=[row(GROUP_WIDTH), row(GROUP_WIDTH), row(D_MODEL),
                  full(g_a), full(g_b), full(w_out_bf), full(g_ffn), full(wr), full(br)],
        out_specs=(row(D_MODEL), row(D_MODEL),
                   pl.BlockSpec((EXPERTS_PER_GROUP, tm), lambda i: (0, i)),
                   row(EXPERTS_PER_GROUP),
                   pl.BlockSpec((None, N_EXPERTS, 1), lambda i: (i, 0, 0))),
        compiler_params=pltpu.CompilerParams(dimension_semantics=("arbitrary",),
                                             vmem_limit_bytes=VMEM_LIMIT),
        name="mix_router",
    )(oa, ob, x2d, g_a, g_b, w_out_bf, g_ffn, wr, br)


RUN_ALIGN = 16
LOCAL_ROWS = 2 * TOKEN_TILE + N_EXPERTS * RUN_ALIGN
MAX_CHUNKS = LOCAL_ROWS // RUN_ALIGN
START_UNROLL = 4
SORT_BLOCK = 256
FINAL_BLOCK = 256


def _local_slots(member, lo_vec, prior, e_sel, axis):
    ids = lax.broadcasted_iota(jnp.int32, member.shape, axis)
    slot = jnp.where(ids == e_sel, lo_vec + prior, 0.0)
    return jnp.sum(slot, axis=axis, keepdims=True).astype(jnp.int32)


def _chunk_copy(dst_ref, tile, c, local, glob, sem, to_global):
    g = pl.multiple_of(dst_ref[tile * MAX_CHUNKS + c], RUN_ALIGN)
    l = pl.multiple_of(c * RUN_ALIGN, RUN_ALIGN)
    src, dst = local.at[pl.ds(l, RUN_ALIGN), :], glob.at[pl.ds(g, RUN_ALIGN), :]
    if not to_global:
        src, dst = dst, src
    return pltpu.make_async_copy(src, dst, sem)


def _start_chunks(nchunk_ref, dst_ref, tile, local, glob, sem, to_global):
    n = nchunk_ref[tile]
    groups = n // START_UNROLL

    def group(gi, carry):
        for u in range(START_UNROLL):
            _chunk_copy(dst_ref, tile, gi * START_UNROLL + u, local, glob, sem, to_global).start()
        return carry

    def single(c, carry):
        _chunk_copy(dst_ref, tile, groups * START_UNROLL + c, local, glob, sem, to_global).start()
        return carry

    lax.fori_loop(0, groups, group, 0)
    lax.fori_loop(0, n % START_UNROLL, single, 0)


def _wait_chunks(nchunk_ref, tile, local, glob, sem):
    n = nchunk_ref[tile]
    size = 1
    while size <= MAX_CHUNKS:
        rows = size * RUN_ALIGN

        @pl.when((n // size) % 2 == 1)
        def _(rows=rows):
            pltpu.make_async_copy(local.at[pl.ds(0, rows), :], glob.at[pl.ds(0, rows), :], sem).wait()

        size *= 2


def _lo_vector(lo_ref, tile, shape, axis):
    ids = lax.broadcasted_iota(jnp.int32, shape, axis)
    vec = jnp.zeros(shape, jnp.float32)
    for e in range(N_EXPERTS):
        vec = jnp.where(ids == e, lo_ref[tile * N_EXPERTS + e].astype(jnp.float32), vec)
    return vec


def _dispatch_kernel(lo_ref, nchunk_ref, dst_ref, pad_tile_ref, nused_ref, ri_ref, h_ref, xs_hbm,
                     upper, sbuf, zbuf, sem, zsem, tail_sem, *, n_tiles):
    i = pl.program_id(0)
    n_steps = pl.num_programs(0)
    slot = i % 2

    def zero_tile(tile, zero_sem):
        rows = pl.ds(pl.multiple_of(tile * ROW_TILE, ROW_TILE), ROW_TILE)
        return pltpu.make_async_copy(zbuf, xs_hbm.at[rows, :], zero_sem)

    @pl.when(i == 0)
    def _():
        r = lax.broadcasted_iota(jnp.int32, (TOKEN_TILE, TOKEN_TILE), 0)
        c = lax.broadcasted_iota(jnp.int32, (TOKEN_TILE, TOKEN_TILE), 1)
        upper[...] = jnp.where(r < c, 1.0, 0.0).astype(jnp.bfloat16)
        zbuf[...] = jnp.zeros_like(zbuf)
        for e in range(N_EXPERTS):
            @pl.when(pad_tile_ref[e] >= 0)
            def _():
                zero_tile(pad_tile_ref[e], zsem).start()
        lax.fori_loop(nused_ref[0], n_tiles, lambda t, c: (zero_tile(t, tail_sem).start(), c)[1], 0)
        for e in range(N_EXPERTS):
            @pl.when(pad_tile_ref[e] >= 0)
            def _():
                zero_tile(pad_tile_ref[e], zsem).wait()

    e0 = ri_ref[0:1, :]
    e1 = ri_ref[1:2, :]
    ids = lax.broadcasted_iota(jnp.int32, (N_EXPERTS, TOKEN_TILE), 0)
    member = jnp.where((ids == e0) | (ids == e1), 1.0, 0.0)
    prior = _dot(member.astype(jnp.bfloat16), upper[...])
    lo_vec = _lo_vector(lo_ref, i, (N_EXPERTS, 1), 0)
    d0 = _local_slots(member, lo_vec, prior, e0, 0)
    d1 = _local_slots(member, lo_vec, prior, e1, 0)
    local = sbuf.at[slot]
    used_rows = nchunk_ref[i] * RUN_ALIGN
    h = h_ref[...]
    for blk in range(LOCAL_ROWS // SORT_BLOCK):
        @pl.when(blk * SORT_BLOCK < used_rows)
        def _():
            rows = blk * SORT_BLOCK + lax.broadcasted_iota(jnp.int32, (SORT_BLOCK, TOKEN_TILE), 0)
            select = jnp.where((rows == d0) | (rows == d1), 1.0, 0.0).astype(jnp.bfloat16)
            local[blk * SORT_BLOCK:(blk + 1) * SORT_BLOCK, :] = _dot(select, h).astype(jnp.bfloat16)

    @pl.when(i > 0)
    def _():
        _wait_chunks(nchunk_ref, i - 1, sbuf.at[1 - slot], xs_hbm, sem)

    _start_chunks(nchunk_ref, dst_ref, i, local, xs_hbm, sem, True)

    @pl.when(i == n_steps - 1)
    def _():
        _wait_chunks(nchunk_ref, i, local, xs_hbm, sem)
        lax.fori_loop(nused_ref[0], n_tiles, lambda t, c: (zero_tile(t, tail_sem).wait(), c)[1], 0)


def _dispatch(lo, nchunk, dst, pad_tile, nused, route_i, h2, p_rows):
    n_tok = h2.shape[0]
    grid_spec = pltpu.PrefetchScalarGridSpec(
        num_scalar_prefetch=5,
        grid=(n_tok // TOKEN_TILE,),
        in_specs=[pl.BlockSpec((EXPERTS_PER_GROUP, TOKEN_TILE), lambda i, *_: (0, i)),
                  pl.BlockSpec((TOKEN_TILE, D_MODEL), lambda i, *_: (i, 0))],
        out_specs=pl.BlockSpec(memory_space=pl.ANY),
        scratch_shapes=[pltpu.VMEM((TOKEN_TILE, TOKEN_TILE), jnp.bfloat16),
                        pltpu.VMEM((2, LOCAL_ROWS, D_MODEL), jnp.bfloat16),
                        pltpu.VMEM((ROW_TILE, D_MODEL), jnp.bfloat16),
                        pltpu.SemaphoreType.DMA, pltpu.SemaphoreType.DMA, pltpu.SemaphoreType.DMA],
    )
    return pl.pallas_call(
        functools.partial(_dispatch_kernel, n_tiles=p_rows // ROW_TILE),
        out_shape=jax.ShapeDtypeStruct((p_rows, D_MODEL), jnp.bfloat16),
        grid_spec=grid_spec,
        compiler_params=pltpu.CompilerParams(dimension_semantics=("arbitrary",),
                                             vmem_limit_bytes=VMEM_LIMIT),
        name="dispatch",
    )(lo, nchunk, dst, pad_tile, nused, route_i, h2)


def _moe_kernel(texp_ref, first_ref, nused_ref, slot_ref, next_ref, xs_ref, wg_hbm, wu_hbm, wd_hbm,
                y_ref, wg_f32, wu_f32, wd_f32, wg_bf, wu_bf, wd_bf, sems):
    j = pl.program_id(0)

    def fetch(expert, slot):
        return [pltpu.make_async_copy(src.at[expert], dst.at[slot], sems.at[slot, k])
                for k, (src, dst) in enumerate(((wg_hbm, wg_f32), (wu_hbm, wu_f32), (wd_hbm, wd_f32)))]

    @pl.when(j == 0)
    def _():
        for cp in fetch(texp_ref[0], 0):
            cp.start()

    @pl.when(j < nused_ref[0])
    def _():
        @pl.when(first_ref[j] == 1)
        def _():
            for s in range(2):
                @pl.when(slot_ref[j] == s)
                def _():
                    for cp in fetch(texp_ref[j], s):
                        cp.wait()
                    wg_bf[...] = wg_f32[s].astype(jnp.bfloat16)
                    wu_bf[...] = wu_f32[s].astype(jnp.bfloat16)
                    wd_bf[...] = wd_f32[s].astype(jnp.bfloat16)

                    @pl.when(next_ref[j] >= 0)
                    def _():
                        for cp in fetch(next_ref[j], 1 - s):
                            cp.start()

        x = xs_ref[...]
        a = _dot(x, wg_bf[...])
        u = _dot(x, wu_bf[...])
        hid = (a * (1.0 / (1.0 + jnp.exp(-a)))) * u
        y_ref[...] = _dot(hid.astype(jnp.bfloat16), wd_bf[...]).astype(jnp.bfloat16)

    @pl.when(j >= nused_ref[0])
    def _():
        y_ref[...] = jnp.zeros_like(y_ref)


def _moe(texp, first, nused, slot, nxt, xs, w_gate, w_up, w_down):
    p_rows = xs.shape[0]
    n_tiles = p_rows // ROW_TILE
    used = lambda j, nu: jnp.minimum(j, nu[0] - 1)
    grid_spec = pltpu.PrefetchScalarGridSpec(
        num_scalar_prefetch=5,
        grid=(n_tiles,),
        in_specs=[
            pl.BlockSpec((ROW_TILE, D_MODEL), lambda j, te, fi, nu, sl, nx: (used(j, nu), 0)),
            pl.BlockSpec(memory_space=pl.ANY),
            pl.BlockSpec(memory_space=pl.ANY),
            pl.BlockSpec(memory_space=pl.ANY),
        ],
        out_specs=pl.BlockSpec((ROW_TILE, D_MODEL), lambda j, te, fi, nu, sl, nx: (j, 0)),
        scratch_shapes=[pltpu.VMEM((2, D_MODEL, D_EXPERT), jnp.float32),
                        pltpu.VMEM((2, D_MODEL, D_EXPERT), jnp.float32),
                        pltpu.VMEM((2, D_EXPERT, D_MODEL), jnp.float32),
                        pltpu.VMEM((D_MODEL, D_EXPERT), jnp.bfloat16),
                        pltpu.VMEM((D_MODEL, D_EXPERT), jnp.bfloat16),
                        pltpu.VMEM((D_EXPERT, D_MODEL), jnp.bfloat16),
                        pltpu.SemaphoreType.DMA((2, 3))],
    )
    return pl.pallas_call(
        _moe_kernel,
        out_shape=jax.ShapeDtypeStruct((p_rows, D_MODEL), jnp.bfloat16),
        grid_spec=grid_spec,
        compiler_params=pltpu.CompilerParams(dimension_semantics=("arbitrary",),
                                             vmem_limit_bytes=VMEM_LIMIT),
        name="moe_experts",
    )(texp, first, nused, slot, nxt, xs, w_gate, w_up, w_down)


def _final_kernel(lo_ref, nchunk_ref, dst_ref, y_hbm, x1_ref, rc_ref, p_ref, gple_ref, wg_ref, wp_ref,
                  gfin_ref, out_ref, lower, ybuf, sems):
    i = pl.program_id(0)
    n_steps = pl.num_programs(0)
    slot = i % 2

    @pl.when(i == 0)
    def _():
        r = lax.broadcasted_iota(jnp.int32, (TOKEN_TILE, TOKEN_TILE), 0)
        c = lax.broadcasted_iota(jnp.int32, (TOKEN_TILE, TOKEN_TILE), 1)
        lower[...] = jnp.where(c < r, 1.0, 0.0).astype(jnp.bfloat16)
        ybuf[...] = jnp.zeros_like(ybuf)
        _start_chunks(nchunk_ref, dst_ref, 0, ybuf.at[0], y_hbm, sems.at[0], False)

    @pl.when(i + 1 < n_steps)
    def _():
        _start_chunks(nchunk_ref, dst_ref, i + 1, ybuf.at[1 - slot], y_hbm, sems.at[1 - slot], False)

    rc = rc_ref[...]
    w0, w1 = rc[:, 0:1], rc[:, 1:2]
    e0, e1 = rc[:, 2:3].astype(jnp.int32), rc[:, 3:4].astype(jnp.int32)
    ids = lax.broadcasted_iota(jnp.int32, (TOKEN_TILE, ROUTER_ROWS), 1)
    member = jnp.where((ids == e0) | (ids == e1), 1.0, 0.0)
    prior = _dot(lower[...], member.astype(jnp.bfloat16))
    lo_vec = _lo_vector(lo_ref, i, (1, ROUTER_ROWS), 1)
    d0 = _local_slots(member, lo_vec, prior, e0, 1)
    d1 = _local_slots(member, lo_vec, prior, e1, 1)

    local = ybuf.at[slot]
    _wait_chunks(nchunk_ref, i, local, y_hbm, sems.at[slot])
    for blk in range(TOKEN_TILE // FINAL_BLOCK):
        rows = slice(blk * FINAL_BLOCK, (blk + 1) * FINAL_BLOCK)
        cols = lax.broadcasted_iota(jnp.int32, (FINAL_BLOCK, LOCAL_ROWS), 1)
        pick = jnp.where(cols == d0[rows], w0[rows], jnp.where(cols == d1[rows], w1[rows], 0.0))
        x2 = x1_ref[rows, :] + _dot(pick.astype(jnp.bfloat16), local[...])
        z = _dot(_rms(x2, gple_ref[...]).astype(jnp.bfloat16), wg_ref[...])
        gate = 1.0 / (1.0 + jnp.exp(-z))
        x3 = x2 + _dot(p_ref[rows, :].astype(jnp.bfloat16), wp_ref[...]) * gate
        out_ref[rows, :] = _rms(x3, gfin_ref[...])


def _final(lo, nchunk, dst, y, x1, rc, p2d, g_ple, w_gate_bf, w_proj_bf, g_final):
    t = x1.shape[0]
    tm = TOKEN_TILE
    row = lambda w: pl.BlockSpec((tm, w), lambda i, *_: (i, 0))
    full = lambda a: pl.BlockSpec(a.shape, lambda i, *_: (0, 0))
    grid_spec = pltpu.PrefetchScalarGridSpec(
        num_scalar_prefetch=3,
        grid=(t // tm,),
        in_specs=[pl.BlockSpec(memory_space=pl.ANY), row(D_MODEL), row(EXPERTS_PER_GROUP),
                  row(PLE_DIM), full(g_ple), full(w_gate_bf), full(w_proj_bf), full(g_final)],
        out_specs=row(D_MODEL),
        scratch_shapes=[pltpu.VMEM((TOKEN_TILE, TOKEN_TILE), jnp.bfloat16),
                        pltpu.VMEM((2, LOCAL_ROWS, D_MODEL), jnp.bfloat16),
                        pltpu.SemaphoreType.DMA((2,))],
    )
    return pl.pallas_call(
        _final_kernel,
        out_shape=jax.ShapeDtypeStruct((t, D_MODEL), jnp.float32),
        grid_spec=grid_spec,
        compiler_params=pltpu.CompilerParams(dimension_semantics=("arbitrary",),
                                             vmem_limit_bytes=VMEM_LIMIT),
        name="combine_ple_final",
    )(lo, nchunk, dst, y, x1, rc, p2d, g_ple, w_gate_bf, w_proj_bf, g_final)


def _dispatch_tables(counts):
    n_tt = counts.shape[0]
    p_rows = 2 * n_tt * TOKEN_TILE + n_tt * N_EXPERTS * (RUN_ALIGN - 1) + N_EXPERTS * ROW_TILE
    p_rows = -(-p_rows // ROW_TILE) * ROW_TILE
    n_tiles = p_rows // ROW_TILE
    n8 = (counts + RUN_ALIGN - 1) // RUN_ALIGN * RUN_ALIGN
    lo = jnp.cumsum(n8, axis=1) - n8
    total = jnp.sum(n8, axis=0)
    tiles_per = (total + ROW_TILE - 1) // ROW_TILE
    tile_end = jnp.cumsum(tiles_per)
    tile_start = tile_end - tiles_per
    g = tile_start[None, :] * ROW_TILE + (jnp.cumsum(n8, axis=0) - n8)
    n_used = tile_end[-1]
    tile_ids = jnp.arange(n_tiles, dtype=jnp.int32)
    clamped = jnp.minimum(tile_ids, n_used - 1)
    texp = jnp.sum((tile_end[None, :] <= clamped[:, None]).astype(jnp.int32), axis=1)
    experts = jnp.arange(N_EXPERTS, dtype=jnp.int32)
    of_tile = texp[:, None] == experts[None, :]
    per_tile = lambda v: jnp.sum(jnp.where(of_tile, v[None, :], 0), axis=1)
    first = ((tile_ids == per_tile(tile_start)) & (tile_ids < n_used)).astype(jnp.int32)
    pad_tile = jnp.where(total > 0, tile_end - 1, -1).astype(jnp.int32)
    has_rows = total > 0
    later = has_rows[None, :] & (experts[None, :] > experts[:, None])
    next_expert = jnp.min(jnp.where(later, experts[None, :], N_EXPERTS), axis=1)
    next_expert = jnp.where(next_expert < N_EXPERTS, next_expert, -1)
    slot = (jnp.cumsum(has_rows.astype(jnp.int32)) - 1) % 2
    chunk_row = jnp.arange(MAX_CHUNKS, dtype=jnp.int32)[None, :, None] * RUN_ALIGN
    in_run = (chunk_row >= lo[:, None, :]) & (chunk_row < (lo + n8)[:, None, :])
    dst = jnp.sum(jnp.where(in_run, g[:, None, :] + chunk_row - lo[:, None, :], 0), axis=2)
    nchunk = jnp.sum(n8, axis=1) // RUN_ALIGN
    i32 = lambda a: a.reshape(-1).astype(jnp.int32)
    return (texp, first, n_used.reshape(1).astype(jnp.int32), i32(per_tile(slot)),
            i32(per_tile(next_expert)), pad_tile, i32(lo), i32(nchunk), i32(dst), p_rows)


def kernel(x, p, g_mix, w_in, sink, g_grp_a, g_grp_b, w_out, g_ffn, w_router_group, b_router_group,
           w_router_expert, b_router_expert, w_expert_gate, w_expert_up, w_expert_down, g_ple,
           w_ple_gate, w_ple_proj, g_final):
    batch, seq, _ = x.shape
    n_tok = batch * seq
    depth = w_in.shape[0]
    assert depth == 1, "the final RMSNorm is fused into the last layer's epilogue"
    bf = jnp.bfloat16
    xc = x.reshape(n_tok, D_MODEL)
    for i in range(depth):
        *qkv_a, qb, kb, vb = _proj(xc, g_mix[i][None, :], w_in[i].astype(bf), batch, seq)
        oa = _attn_a(qkv_a, batch, seq)
        ob = _attn_b(qb, kb, vb, sink[i], batch, seq)

        wr = jnp.zeros((ROUTER_ROWS, D_MODEL), jnp.float32)
        wr = wr.at[0:N_GROUPS].set(w_router_group[i].T)
        wr = wr.at[EXPERTS_PER_GROUP:EXPERTS_PER_GROUP + N_EXPERTS].set(w_router_expert[i].T)
        br = jnp.zeros((ROUTER_ROWS, 1), jnp.float32)
        br = br.at[0:N_GROUPS, 0].set(b_router_group[i])
        br = br.at[EXPERTS_PER_GROUP:EXPERTS_PER_GROUP + N_EXPERTS, 0].set(b_router_expert[i])

        x1, h2, route_i, route_c, counts = _mix(
            oa, ob, xc, g_grp_a[i][None, :], g_grp_b[i][None, :], w_out[i].astype(bf), g_ffn[i][None, :], wr, br)

        texp, first, nused, slot, nxt, pad_tile, lo, nchunk, dst, p_rows = _dispatch_tables(counts[:, :, 0])
        xs = _dispatch(lo, nchunk, dst, pad_tile, nused, route_i, h2, p_rows)
        y = _moe(texp, first, nused, slot, nxt, xs, w_expert_gate[i], w_expert_up[i], w_expert_down[i])
        xc = _final(lo, nchunk, dst, y, x1, route_c, p[i].reshape(n_tok, PLE_DIM), g_ple[i][None, :],
                    w_ple_gate[i].astype(bf), w_ple_proj[i].astype(bf), g_final[None, :])
    return xc.reshape(batch, seq, D_MODEL)
```

```python
import functools

import jax
import jax.numpy as jnp
from jax import lax
from jax.experimental import pallas as pl
from jax.experimental.pallas import tpu as pltpu

D_MODEL = 1024
HEAD_DIM = 64
GROUP_WIDTH = 512
N_HEADS = 8
B_KV_HEADS = 2
B_KV_WIDTH = B_KV_HEADS * HEAD_DIM
A_CONFIGS = ((128, 1), (512, 4), (2048, 16))
B_HALF_WINDOW = 128
N_GROUPS = 4
EXPERTS_PER_GROUP = 8
N_EXPERTS = N_GROUPS * EXPERTS_PER_GROUP
D_EXPERT = 512
PLE_DIM = 256
EPS = 1e-6
NEG = -1e30
LOG2E = 1.4426950408889634

LANES = 128
Q_BLOCK = 128
HEADS_PER_PASS = 2
PASS_WIDTH = HEADS_PER_PASS * HEAD_DIM
ROW_TILE = 448
TOKEN_TILE = 512
ROUTER_ROWS = 128
VMEM_LIMIT = 48 * 1024 * 1024
ATTN_VMEM_LIMIT = 56 * 1024 * 1024


def _rms(x, g):
    r = lax.rsqrt(jnp.mean(x * x, axis=-1, keepdims=True) + EPS)
    return (x * r) * g


def _dot(a, b):
    return jnp.dot(a, b, preferred_element_type=jnp.float32)


PROJ_TILE = 1024


def _proj_kernel(x_ref, g_ref, w_ref, *refs):
    a_refs, (qb_ref, kb_ref, vb_ref, res, *stage_refs) = refs[:9], refs[9:]
    h = _rms(x_ref[...], g_ref[...]).astype(jnp.bfloat16)
    scale = HEAD_DIM ** -0.5 * LOG2E
    W = GROUP_WIDTH
    tiles = GROUP_WIDTH // PASS_WIDTH
    qkv_a = (_dot(h, w_ref[:, 0:W]) * scale, _dot(h, w_ref[:, W:2 * W]), _dot(h, w_ref[:, 2 * W:3 * W]))
    for part in range(3):
        for pair in range(tiles):
            res[part * tiles + pair, 0] = qkv_a[part][:, pair * PASS_WIDTH:(pair + 1) * PASS_WIDTH]
    qkv_b = _dot(h, w_ref[:, 3 * W:4 * W + 2 * B_KV_WIDTH])
    qb_ref[...] = (qkv_b[:, 0:W] * scale).astype(jnp.bfloat16)
    kb_ref[...] = qkv_b[:, W:W + B_KV_WIDTH].astype(jnp.bfloat16)
    vb_ref[...] = qkv_b[:, W + B_KV_WIDTH:W + 2 * B_KV_WIDTH].astype(jnp.bfloat16)
    stages = (res,) + tuple(stage_refs)
    prev_dil = 1
    for c, (_, dil) in enumerate(A_CONFIGS):
        ratio = dil // prev_dil
        rows = PROJ_TILE // dil
        for idx in range(3 * tiles):
            part, pair = idx // tiles, idx % tiles
            out = a_refs[3 * c + part]
            for rho in range(dil):
                if c == 0:
                    sub = res[idx, 0]
                else:
                    sub = stages[c - 1].at[idx, rho % prev_dil][pl.ds(rho // prev_dil, rows, stride=ratio), :]
                    if c < len(A_CONFIGS) - 1:
                        stages[c][idx, rho] = sub
                out[pair, rho] = sub.astype(jnp.bfloat16)
        prev_dil = dil


def _proj(x2d, g_mix, w_in_bf, batch, seq):
    t = x2d.shape[0]
    tm = PROJ_TILE
    tiles_per_seq = seq // tm
    in_width = w_in_bf.shape[1]
    n_pairs = N_HEADS // HEADS_PER_PASS
    row = lambda w: pl.BlockSpec((tm, w), lambda i: (i, 0))
    a_shapes, a_specs = [], []
    for _, dil in A_CONFIGS:
        for _ in range(3):
            a_shapes.append(jax.ShapeDtypeStruct((batch, n_pairs, dil, seq // dil, PASS_WIDTH),
                                                 jnp.bfloat16))
            a_specs.append(pl.BlockSpec((None, n_pairs, dil, tm // dil, PASS_WIDTH),
                                        lambda i: (i // tiles_per_seq, 0, 0, i % tiles_per_seq, 0)))
    wide = jax.ShapeDtypeStruct((t, GROUP_WIDTH), jnp.bfloat16)
    narrow = jax.ShapeDtypeStruct((t, B_KV_WIDTH), jnp.bfloat16)
    return pl.pallas_call(
        _proj_kernel,
        out_shape=tuple(a_shapes) + (wide, narrow, narrow),
        grid=(t // tm,),
        in_specs=[row(D_MODEL),
                  pl.BlockSpec((1, D_MODEL), lambda i: (0, 0)),
                  pl.BlockSpec((D_MODEL, in_width), lambda i: (0, 0), pipeline_mode=pl.Buffered(1))],
        out_specs=tuple(a_specs) + (row(GROUP_WIDTH), row(B_KV_WIDTH), row(B_KV_WIDTH)),
        scratch_shapes=[pltpu.VMEM((3 * GROUP_WIDTH // PASS_WIDTH, dil, tm // dil, PASS_WIDTH), jnp.float32)
                        for _, dil in A_CONFIGS[:-1]],
        compiler_params=pltpu.CompilerParams(dimension_semantics=("arbitrary",),
                                             vmem_limit_bytes=ATTN_VMEM_LIMIT),
        name="proj",
    )(x2d, g_mix, w_in_bf)


def _fill_bias(bias_ref, half, nkeys, dist_scale):
    i = lax.broadcasted_iota(jnp.int32, (Q_BLOCK, nkeys), 0)
    j = lax.broadcasted_iota(jnp.int32, (Q_BLOCK, nkeys), 1)
    for var, rel0 in enumerate((0, -half, Q_BLOCK - nkeys)):
        d = jnp.abs(j + rel0 - i)
        dist = d.astype(jnp.float32) * dist_scale
        for h in range(N_HEADS):
            slope = 2.0 ** (-(h + 1))
            bias_ref[var, h] = jnp.where(d <= half, -(slope * dist) * LOG2E, NEG)


def _key_window(q0, seq_len, half, nkeys):
    lo = q0 - half
    if isinstance(q0, int):
        return min(max(lo, 0), seq_len - nkeys), (0 if lo < 0 else 2 if lo > seq_len - nkeys else 1)
    ks = pl.multiple_of(jnp.clip(lo, 0, seq_len - nkeys), HEAD_DIM)
    var = jnp.where(lo < 0, 0, jnp.where(lo > seq_len - nkeys, 2, 1))
    return ks, var


def _fold_lane_tiles(a, op):
    out = a[..., 0:LANES]
    for t in range(1, a.shape[-1] // LANES):
        out = op(out, a[..., t * LANES:(t + 1) * LANES])
    return out


def _attend_pass(q, k, v, bias, sinks):
    nkeys = k.shape[0]
    lane_head = lax.broadcasted_iota(jnp.int32, (Q_BLOCK, PASS_WIDTH), 1) // HEAD_DIM
    zero = jnp.zeros_like(q)
    q_stack = jnp.concatenate(
        [jnp.where(lane_head == h, q, zero) for h in range(HEADS_PER_PASS)], axis=0)
    s = lax.dot_general(q_stack, k, (((1,), (1,)), ((), ())), preferred_element_type=jnp.float32)
    s = s.reshape(HEADS_PER_PASS, Q_BLOCK, nkeys) + bias
    m_tile = _fold_lane_tiles(s, jnp.maximum)
    if sinks is not None:
        tile_head = lax.broadcasted_iota(jnp.int32, m_tile.shape, 0)
        tile_lane = lax.broadcasted_iota(jnp.int32, m_tile.shape, 2)
        sk = jnp.zeros(m_tile.shape, jnp.float32)
        for h in range(HEADS_PER_PASS):
            sk = jnp.where(tile_head == h, sinks[h] * LOG2E, sk)
        m_tile = jnp.maximum(m_tile, sk)
    m = jnp.max(m_tile, axis=-1, keepdims=True)
    p = jnp.exp2(s - m)
    den_tile = _fold_lane_tiles(p, jnp.add)
    if sinks is not None:
        den_tile = den_tile + jnp.where(tile_lane == 0, jnp.exp2(sk - m), 0.0)
    den = jnp.sum(den_tile, axis=-1, keepdims=True)
    pv = _dot(p.reshape(HEADS_PER_PASS * Q_BLOCK, nkeys).astype(jnp.bfloat16), v)
    pv = pv.reshape(HEADS_PER_PASS, Q_BLOCK, PASS_WIDTH)
    shape = (Q_BLOCK, PASS_WIDTH)
    o, m_lanes, den_lanes = (jnp.zeros(shape, jnp.float32) for _ in range(3))
    for h in range(HEADS_PER_PASS):
        o = jnp.where(lane_head == h, pv[h], o)
        m_lanes = jnp.where(lane_head == h, m[h], m_lanes)
        den_lanes = jnp.where(lane_head == h, den[h], den_lanes)
    return o, m_lanes, den_lanes


A_HALF = 64
A_KEYS = Q_BLOCK + 2 * A_HALF
PASSES_PER_BODY = 32
B_Q_TILE = 1024
KV_REPLICATE_ROWS = 512
NO_INDEX = 1 << 20
MERGE_BLOCK = 256


def _attn_a_kernel(*refs, seq):
    n_cfg = len(A_CONFIGS)
    qkv = refs[:3 * n_cfg]
    o_ref, bias_ref = refs[3 * n_cfg], refs[3 * n_cfg + 1]
    scratch = refs[3 * n_cfg + 2:]
    pair = pl.program_id(1)

    @pl.when((pl.program_id(0) == 0) & (pair == 0))
    def _():
        for c, (_, dil) in enumerate(A_CONFIGS):
            _fill_bias(bias_ref.at[c], A_HALF, A_KEYS, float(dil))

    def attend(c, idx):
        window, dil = A_CONFIGS[c]
        assert window // (2 * dil) == A_HALF
        q_ref, k_ref, v_ref = qkv[3 * c:3 * c + 3]
        o_sc, m_sc, d_sc = scratch[3 * c:3 * c + 3]
        sub_len = seq // dil
        rho, q0 = divmod(idx, sub_len // Q_BLOCK)
        q0 *= Q_BLOCK
        ks, var = _key_window(q0, sub_len, A_HALF, A_KEYS)
        bias = bias_ref[c, var, pl.ds(pair * HEADS_PER_PASS, HEADS_PER_PASS)]
        o, m, den = _attend_pass(q_ref[rho, pl.ds(q0, Q_BLOCK), :], k_ref[rho, pl.ds(ks, A_KEYS), :],
                                 v_ref[rho, pl.ds(ks, A_KEYS), :], bias, None)
        tokens = pl.ds(rho + dil * q0, Q_BLOCK, stride=dil) if dil > 1 else pl.ds(q0, Q_BLOCK)
        o_sc[tokens, :] = o
        m_sc[tokens, :] = m
        d_sc[tokens, :] = den

    def merge(blk, carry):
        rows = pl.ds(pl.multiple_of(blk * MERGE_BLOCK, MERGE_BLOCK), MERGE_BLOCK)
        pvs = [scratch[3 * c][rows, :] for c in range(n_cfg)]
        ms = [scratch[3 * c + 1][rows, :] for c in range(n_cfg)]
        dens = [scratch[3 * c + 2][rows, :] for c in range(n_cfg)]
        mx = functools.reduce(jnp.maximum, ms)
        es = [jnp.exp2(m - mx) for m in ms]
        total = functools.reduce(jnp.add, [e * d for e, d in zip(es, dens)])
        o_ref[rows, :] = functools.reduce(jnp.add, [e * pv for e, pv in zip(es, pvs)]) * (1.0 / total)
        return carry

    for c in range(n_cfg):
        for idx in range(seq // Q_BLOCK):
            attend(c, idx)
    lax.fori_loop(0, seq // MERGE_BLOCK, merge, 0, unroll=2)


def _attn_a(qkv, batch, seq):
    n_pairs = N_HEADS // HEADS_PER_PASS
    in_specs, scratch = [], [pltpu.VMEM((len(A_CONFIGS), 3, N_HEADS, Q_BLOCK, A_KEYS), jnp.float32)]
    for _, dil in A_CONFIGS:
        blk = (None, None, dil, seq // dil, PASS_WIDTH)
        in_specs += [pl.BlockSpec(blk, lambda b, p: (b, p, 0, 0, 0))] * 3
        scratch += [pltpu.VMEM((seq, PASS_WIDTH), jnp.float32)] * 3
    return pl.pallas_call(
        functools.partial(_attn_a_kernel, seq=seq),
        out_shape=jax.ShapeDtypeStruct((batch, seq, GROUP_WIDTH), jnp.float32),
        grid=(batch, n_pairs),
        in_specs=in_specs,
        out_specs=pl.BlockSpec((None, seq, PASS_WIDTH), lambda b, p: (b, 0, p)),
        scratch_shapes=scratch,
        compiler_params=pltpu.CompilerParams(dimension_semantics=("arbitrary", "arbitrary"),
                                             vmem_limit_bytes=ATTN_VMEM_LIMIT),
        name="attn_a",
    )(*qkv).reshape(batch * seq, GROUP_WIDTH)


def _attn_b_kernel(sink_ref, q_ref, k_ref, v_ref, o_ref, bias_ref, k4_ref, v4_ref, *, half, nkeys,
                   seq_len, q_tile):
    first = (pl.program_id(0) == 0) & (pl.program_id(1) == 0)

    @pl.when(first)
    def _():
        _fill_bias(bias_ref, half, nkeys, 1.0)

    @pl.when(pl.program_id(1) == 0)
    def _():
        c = lax.broadcasted_iota(jnp.int32, (B_KV_WIDTH, GROUP_WIDTH), 0)
        j = lax.broadcasted_iota(jnp.int32, (B_KV_WIDTH, GROUP_WIDTH), 1)
        src = (j // (GROUP_WIDTH // B_KV_HEADS)) * HEAD_DIM + j % HEAD_DIM
        rep = jnp.where(c == src, 1.0, 0.0).astype(jnp.bfloat16)
        chunk = KV_REPLICATE_ROWS

        def body(i, carry):
            rows = pl.ds(pl.multiple_of(i * chunk, chunk), chunk)
            k4_ref[rows, :] = _dot(k_ref[rows, :], rep).astype(jnp.bfloat16)
            v4_ref[rows, :] = _dot(v_ref[rows, :], rep).astype(jnp.bfloat16)
            return carry

        lax.fori_loop(0, seq_len // chunk, body, 0)

    base = pl.program_id(1) * q_tile

    def body(jb, carry):
        row0 = pl.multiple_of(jb * Q_BLOCK, Q_BLOCK)
        ks, var = _key_window(base + row0, seq_len, half, nkeys)
        for g in range(N_HEADS // HEADS_PER_PASS):
            cols = slice(g * PASS_WIDTH, (g + 1) * PASS_WIDTH)
            heads = pl.ds(g * HEADS_PER_PASS, HEADS_PER_PASS)
            sinks = [sink_ref[g * HEADS_PER_PASS + h] for h in range(HEADS_PER_PASS)]
            o, _, den = _attend_pass(q_ref[pl.ds(row0, Q_BLOCK), cols], k4_ref[pl.ds(ks, nkeys), cols],
                                     v4_ref[pl.ds(ks, nkeys), cols], bias_ref[var, heads], sinks)
            o_ref[pl.ds(row0, Q_BLOCK), cols] = (o * (1.0 / den)).astype(o_ref.dtype)
        return carry

    lax.fori_loop(0, q_tile // Q_BLOCK, body, 0,
                  unroll=PASSES_PER_BODY // (N_HEADS // HEADS_PER_PASS))


def _attn_b(q, k, v, sink, batch, seq):
    half = B_HALF_WINDOW
    nkeys = Q_BLOCK + 2 * half
    q_tile = B_Q_TILE
    view = lambda a: a.reshape(batch, seq, a.shape[-1])
    kern = functools.partial(_attn_b_kernel, half=half, nkeys=nkeys, seq_len=seq, q_tile=q_tile)
    o = pl.pallas_call(
        kern,
        out_shape=jax.ShapeDtypeStruct((batch, seq, GROUP_WIDTH), jnp.bfloat16),
        grid=(batch, seq // q_tile),
        in_specs=[pl.BlockSpec(memory_space=pltpu.SMEM),
                  pl.BlockSpec((None, q_tile, GROUP_WIDTH), lambda b, i: (b, i, 0)),
                  pl.BlockSpec((None, seq, B_KV_WIDTH), lambda b, i: (b, 0, 0)),
                  pl.BlockSpec((None, seq, B_KV_WIDTH), lambda b, i: (b, 0, 0))],
        out_specs=pl.BlockSpec((None, q_tile, GROUP_WIDTH), lambda b, i: (b, i, 0)),
        scratch_shapes=[pltpu.VMEM((3, N_HEADS, Q_BLOCK, nkeys), jnp.float32),
                        pltpu.VMEM((seq, GROUP_WIDTH), jnp.bfloat16),
                        pltpu.VMEM((seq, GROUP_WIDTH), jnp.bfloat16)],
        compiler_params=pltpu.CompilerParams(dimension_semantics=("arbitrary", "arbitrary"),
                                             vmem_limit_bytes=VMEM_LIMIT),
        name="attn_b",
    )(sink, view(q), view(k), view(v))
    return o.reshape(batch * seq, GROUP_WIDTH)


def _mix_kernel(oa_ref, ob_ref, x_ref, ga_ref, gb_ref, wout_ref, gffn_ref, wr_ref, br_ref,
                x1_ref, h2_ref, ri_ref, rc_ref, cnt_ref):
    tm = x_ref.shape[0]
    na = _rms(oa_ref[...], ga_ref[...]).astype(jnp.bfloat16)
    nb = _rms(ob_ref[...].astype(jnp.float32), gb_ref[...]).astype(jnp.bfloat16)
    x1 = x_ref[...] + _dot(jnp.concatenate([na, nb], axis=1), wout_ref[...])
    x1_ref[...] = x1
    h2 = _rms(x1, gffn_ref[...])
    h2_ref[...] = h2.astype(jnp.bfloat16)

    h_hi = h2.astype(jnp.bfloat16)
    h_lo = (h2 - h_hi.astype(jnp.float32)).astype(jnp.bfloat16)
    wr = wr_ref[...]
    w_hi = wr.astype(jnp.bfloat16)
    w_lo = (wr - w_hi.astype(jnp.float32)).astype(jnp.bfloat16)
    w_cat = jnp.concatenate([w_hi, w_lo], axis=0)
    nt = (((1,), (1,)), ((), ()))
    lg2 = (lax.dot_general(w_cat, h_hi, nt, preferred_element_type=jnp.float32)
           + lax.dot_general(w_cat, h_lo, nt, preferred_element_type=jnp.float32))
    lgt = lg2[0:ROUTER_ROWS] + lg2[ROUTER_ROWS:2 * ROUTER_ROWS] + br_ref[...]

    rows = lax.broadcasted_iota(jnp.int32, (EXPERTS_PER_GROUP, tm), 0)
    big = jnp.int32(NO_INDEX)
    glog = jnp.where(rows < N_GROUPS, lgt[0:EXPERTS_PER_GROUP], -jnp.inf)
    gmax = jnp.max(glog, axis=0, keepdims=True)
    gsel = jnp.min(jnp.where(glog == gmax, rows, big), axis=0, keepdims=True)
    gw = 1.0 / jnp.sum(jnp.exp(glog - gmax), axis=0, keepdims=True)
    esel = jnp.zeros((EXPERTS_PER_GROUP, tm), jnp.float32)
    for grp in range(N_GROUPS):
        lo = EXPERTS_PER_GROUP * (grp + 1)
        esel = jnp.where(gsel == grp, lgt[lo:lo + EXPERTS_PER_GROUP], esel)
    v0 = jnp.max(esel, axis=0, keepdims=True)
    i0 = jnp.min(jnp.where(esel == v0, rows, big), axis=0, keepdims=True)
    rest = jnp.where(rows == i0, -jnp.inf, esel)
    v1 = jnp.max(rest, axis=0, keepdims=True)
    i1 = jnp.min(jnp.where(rest == v1, rows, big), axis=0, keepdims=True)
    e = jnp.exp(v1 - v0)
    w0 = (1.0 / (1.0 + e)) * gw
    w1 = (e / (1.0 + e)) * gw
    eid0 = gsel * EXPERTS_PER_GROUP + i0
    eid1 = gsel * EXPERTS_PER_GROUP + i1
    ri_ref[...] = jnp.where(rows == 0, eid0, jnp.where(rows == 1, eid1, 0))
    rows_t = lax.broadcasted_iota(jnp.int32, (ROUTER_ROWS, tm), 0)
    slab = jnp.where(rows_t == 0, w0, jnp.where(rows_t == 1, w1, 0.0))
    slab = jnp.where(rows_t == 2, eid0.astype(jnp.float32),
                     jnp.where(rows_t == 3, eid1.astype(jnp.float32), slab))
    rc_ref[...] = slab.T[:, 0:EXPERTS_PER_GROUP]
    ids = lax.broadcasted_iota(jnp.int32, (N_EXPERTS, tm), 0)
    member = jnp.where((ids == eid0) | (ids == eid1), 1.0, 0.0)
    cnt_ref[...] = jnp.sum(member, axis=1, keepdims=True).astype(jnp.int32)


def _mix(oa, ob, x2d, g_a, g_b, w_out_bf, g_ffn, wr, br):
    t = x2d.shape[0]
    tm = TOKEN_TILE
    row = lambda w: pl.BlockSpec((tm, w), lambda i: (i, 0))
    full = lambda a: pl.BlockSpec(a.shape, lambda i: (0, 0))
    return pl.pallas_call(
        _mix_kernel,
        out_shape=(jax.ShapeDtypeStruct((t, D_MODEL), jnp.float32),
                   jax.ShapeDtypeStruct((t, D_MODEL), jnp.bfloat16),
                   jax.ShapeDtypeStruct((EXPERTS_PER_GROUP, t), jnp.int32),
                   jax.ShapeDtypeStruct((t, EXPERTS_PER_GROUP), jnp.float32),
                   jax.ShapeDtypeStruct((t // tm, N_EXPERTS, 1), jnp.int32)),
        grid=(t // tm,),
        in_specs=[row(GROUP_WIDTH), row(GROUP_WIDTH), row(D_MODEL),
                  full(g_a), full(g_b), full(w_out_bf), full(g_ffn), full(wr), full(br)],
        out_specs=(row(D_MODEL), row(D_MODEL),
                   pl.BlockSpec((EXPERTS_PER_GROUP, tm), lambda i: (0, i)),
                   row(EXPERTS_PER_GROUP),
                   pl.BlockSpec((None, N_EXPERTS, 1), lambda i: (i, 0, 0))),
        compiler_params=pltpu.CompilerParams(dimension_semantics=("arbitrary",),
                                             vmem_limit_bytes=VMEM_LIMIT),
        name="mix_router",
    )(oa, ob, x2d, g_a, g_b, w_out_bf, g_ffn, wr, br)


RUN_ALIGN = 16
LOCAL_ROWS = 2 * TOKEN_TILE + N_EXPERTS * RUN_ALIGN
MAX_CHUNKS = LOCAL_ROWS // RUN_ALIGN
START_UNROLL = 4
SORT_BLOCK = 256
FINAL_BLOCK = 256


def _local_slots(member, lo_vec, prior, e_sel, axis):
    ids = lax.broadcasted_iota(jnp.int32, member.shape, axis)
    slot = jnp.where(ids == e_sel, lo_vec + prior, 0.0)
    return jnp.sum(slot, axis=axis, keepdims=True).astype(jnp.int32)


def _chunk_copy(dst_ref, tile, c, local, glob, sem, to_global):
    g = pl.multiple_of(dst_ref[tile * MAX_CHUNKS + c], RUN_ALIGN)
    l = pl.multiple_of(c * RUN_ALIGN, RUN_ALIGN)
    src, dst = local.at[pl.ds(l, RUN_ALIGN), :], glob.at[pl.ds(g, RUN_ALIGN), :]
    if not to_global:
        src, dst = dst, src
    return pltpu.make_async_copy(src, dst, sem)


def _start_chunks(nchunk_ref, dst_ref, tile, local, glob, sem, to_global):
    n = nchunk_ref[tile]
    groups = n // START_UNROLL

    def group(gi, carry):
        for u in range(START_UNROLL):
            _chunk_copy(dst_ref, tile, gi * START_UNROLL + u, local, glob, sem, to_global).start()
        return carry

    def single(c, carry):
        _chunk_copy(dst_ref, tile, groups * START_UNROLL + c, local, glob, sem, to_global).start()
        return carry

    lax.fori_loop(0, groups, group, 0)
    lax.fori_loop(0, n % START_UNROLL, single, 0)


def _wait_chunks(nchunk_ref, tile, local, glob, sem):
    n = nchunk_ref[tile]
    size = 1
    while size <= MAX_CHUNKS:
        rows = size * RUN_ALIGN

        @pl.when((n // size) % 2 == 1)
        def _(rows=rows):
            pltpu.make_async_copy(local.at[pl.ds(0, rows), :], glob.at[pl.ds(0, rows), :], sem).wait()

        size *= 2


def _lo_vector(lo_ref, tile, shape, axis):
    ids = lax.broadcasted_iota(jnp.int32, shape, axis)
    vec = jnp.zeros(shape, jnp.float32)
    for e in range(N_EXPERTS):
        vec = jnp.where(ids == e, lo_ref[tile * N_EXPERTS + e].astype(jnp.float32), vec)
    return vec


def _dispatch_kernel(lo_ref, nchunk_ref, dst_ref, pad_tile_ref, nused_ref, ri_ref, h_ref, xs_hbm,
                     upper, sbuf, zbuf, sem, zsem, tail_sem, *, n_tiles):
    i = pl.program_id(0)
    n_steps = pl.num_programs(0)
    slot = i % 2

    def zero_tile(tile, zero_sem):
        rows = pl.ds(pl.multiple_of(tile * ROW_TILE, ROW_TILE), ROW_TILE)
        return pltpu.make_async_copy(zbuf, xs_hbm.at[rows, :], zero_sem)

    @pl.when(i == 0)
    def _():
        r = lax.broadcasted_iota(jnp.int32, (TOKEN_TILE, TOKEN_TILE), 0)
        c = lax.broadcasted_iota(jnp.int32, (TOKEN_TILE, TOKEN_TILE), 1)
        upper[...] = jnp.where(r < c, 1.0, 0.0).astype(jnp.bfloat16)
        zbuf[...] = jnp.zeros_like(zbuf)
        for e in range(N_EXPERTS):
            @pl.when(pad_tile_ref[e] >= 0)
            def _():
                zero_tile(pad_tile_ref[e], zsem).start()
        lax.fori_loop(nused_ref[0], n_tiles, lambda t, c: (zero_tile(t, tail_sem).start(), c)[1], 0)
        for e in range(N_EXPERTS):
            @pl.when(pad_tile_ref[e] >= 0)
            def _():
                zero_tile(pad_tile_ref[e], zsem).wait()

    e0 = ri_ref[0:1, :]
    e1 = ri_ref[1:2, :]
    ids = lax.broadcasted_iota(jnp.int32, (N_EXPERTS, TOKEN_TILE), 0)
    member = jnp.where((ids == e0) | (ids == e1), 1.0, 0.0)
    prior = _dot(member.astype(jnp.bfloat16), upper[...])
    lo_vec = _lo_vector(lo_ref, i, (N_EXPERTS, 1), 0)
    d0 = _local_slots(member, lo_vec, prior, e0, 0)
    d1 = _local_slots(member, lo_vec, prior, e1, 0)
    local = sbuf.at[slot]
    used_rows = nchunk_ref[i] * RUN_ALIGN
    h = h_ref[...]
    for blk in range(LOCAL_ROWS // SORT_BLOCK):
        @pl.when(blk * SORT_BLOCK < used_rows)
        def _():
            rows = blk * SORT_BLOCK + lax.broadcasted_iota(jnp.int32, (SORT_BLOCK, TOKEN_TILE), 0)
            select = jnp.where((rows == d0) | (rows == d1), 1.0, 0.0).astype(jnp.bfloat16)
            local[blk * SORT_BLOCK:(blk + 1) * SORT_BLOCK, :] = _dot(select, h).astype(jnp.bfloat16)

    @pl.when(i > 0)
    def _():
        _wait_chunks(nchunk_ref, i - 1, sbuf.at[1 - slot], xs_hbm, sem)

    _start_chunks(nchunk_ref, dst_ref, i, local, xs_hbm, sem, True)

    @pl.when(i == n_steps - 1)
    def _():
        _wait_chunks(nchunk_ref, i, local, xs_hbm, sem)
        lax.fori_loop(nused_ref[0], n_tiles, lambda t, c: (zero_tile(t, tail_sem).wait(), c)[1], 0)


def _dispatch(lo, nchunk, dst, pad_tile, nused, route_i, h2, p_rows):
    n_tok = h2.shape[0]
    grid_spec = pltpu.PrefetchScalarGridSpec(
        num_scalar_prefetch=5,
        grid=(n_tok // TOKEN_TILE,),
        in_specs=[pl.BlockSpec((EXPERTS_PER_GROUP, TOKEN_TILE), lambda i, *_: (0, i)),
                  pl.BlockSpec((TOKEN_TILE, D_MODEL), lambda i, *_: (i, 0))],
        out_specs=pl.BlockSpec(memory_space=pl.ANY),
        scratch_shapes=[pltpu.VMEM((TOKEN_TILE, TOKEN_TILE), jnp.bfloat16),
                        pltpu.VMEM((2, LOCAL_ROWS, D_MODEL), jnp.bfloat16),
                        pltpu.VMEM((ROW_TILE, D_MODEL), jnp.bfloat16),
                        pltpu.SemaphoreType.DMA, pltpu.SemaphoreType.DMA, pltpu.SemaphoreType.DMA],
    )
    return pl.pallas_call(
        functools.partial(_dispatch_kernel, n_tiles=p_rows // ROW_TILE),
        out_shape=jax.ShapeDtypeStruct((p_rows, D_MODEL), jnp.bfloat16),
        grid_spec=grid_spec,
        compiler_params=pltpu.CompilerParams(dimension_semantics=("arbitrary",),
                                             vmem_limit_bytes=VMEM_LIMIT),
        name="dispatch",
    )(lo, nchunk, dst, pad_tile, nused, route_i, h2)


def _moe_kernel(texp_ref, first_ref, nused_ref, slot_ref, next_ref, xs_ref, wg_hbm, wu_hbm, wd_hbm,
                y_ref, wg_f32, wu_f32, wd_f32, wg_bf, wu_bf, wd_bf, sems):
    j = pl.program_id(0)

    def fetch(expert, slot):
        return [pltpu.make_async_copy(src.at[expert], dst.at[slot], sems.at[slot, k])
                for k, (src, dst) in enumerate(((wg_hbm, wg_f32), (wu_hbm, wu_f32), (wd_hbm, wd_f32)))]

    @pl.when(j == 0)
    def _():
        for cp in fetch(texp_ref[0], 0):
            cp.start()

    @pl.when(j < nused_ref[0])
    def _():
        @pl.when(first_ref[j] == 1)
        def _():
            for s in range(2):
                @pl.when(slot_ref[j] == s)
                def _():
                    for cp in fetch(texp_ref[j], s):
                        cp.wait()
                    wg_bf[...] = wg_f32[s].astype(jnp.bfloat16)
                    wu_bf[...] = wu_f32[s].astype(jnp.bfloat16)
                    wd_bf[...] = wd_f32[s].astype(jnp.bfloat16)

                    @pl.when(next_ref[j] >= 0)
                    def _():
                        for cp in fetch(next_ref[j], 1 - s):
                            cp.start()

        x = xs_ref[...]
        a = _dot(x, wg_bf[...])
        u = _dot(x, wu_bf[...])
        hid = (a * (1.0 / (1.0 + jnp.exp(-a)))) * u
        y_ref[...] = _dot(hid.astype(jnp.bfloat16), wd_bf[...]).astype(jnp.bfloat16)

    @pl.when(j >= nused_ref[0])
    def _():
        y_ref[...] = jnp.zeros_like(y_ref)


def _moe(texp, first, nused, slot, nxt, xs, w_gate, w_up, w_down):
    p_rows = xs.shape[0]
    n_tiles = p_rows // ROW_TILE
    used = lambda j, nu: jnp.minimum(j, nu[0] - 1)
    grid_spec = pltpu.PrefetchScalarGridSpec(
        num_scalar_prefetch=5,
        grid=(n_tiles,),
        in_specs=[
            pl.BlockSpec((ROW_TILE, D_MODEL), lambda j, te, fi, nu, sl, nx: (used(j, nu), 0)),
            pl.BlockSpec(memory_space=pl.ANY),
            pl.BlockSpec(memory_space=pl.ANY),
            pl.BlockSpec(memory_space=pl.ANY),
        ],
        out_specs=pl.BlockSpec((ROW_TILE, D_MODEL), lambda j, te, fi, nu, sl, nx: (j, 0)),
        scratch_shapes=[pltpu.VMEM((2, D_MODEL, D_EXPERT), jnp.float32),
                        pltpu.VMEM((2, D_MODEL, D_EXPERT), jnp.float32),
                        pltpu.VMEM((2, D_EXPERT, D_MODEL), jnp.float32),
                        pltpu.VMEM((D_MODEL, D_EXPERT), jnp.bfloat16),
                        pltpu.VMEM((D_MODEL, D_EXPERT), jnp.bfloat16),
                        pltpu.VMEM((D_EXPERT, D_MODEL), jnp.bfloat16),
                        pltpu.SemaphoreType.DMA((2, 3))],
    )
    return pl.pallas_call(
        _moe_kernel,
        out_shape=jax.ShapeDtypeStruct((p_rows, D_MODEL), jnp.bfloat16),
        grid_spec=grid_spec,
        compiler_params=pltpu.CompilerParams(dimension_semantics=("arbitrary",),
                                             vmem_limit_bytes=VMEM_LIMIT),
        name="moe_experts",
    )(texp, first, nused, slot, nxt, xs, w_gate, w_up, w_down)


def _final_kernel(lo_ref, nchunk_ref, dst_ref, y_hbm, x1_ref, rc_ref, p_ref, gple_ref, wg_ref, wp_ref,
                  gfin_ref, out_ref, lower, ybuf, sems):
    i = pl.program_id(0)
    n_steps = pl.num_programs(0)
    slot = i % 2

    @pl.when(i == 0)
    def _():
        r = lax.broadcasted_iota(jnp.int32, (TOKEN_TILE, TOKEN_TILE), 0)
        c = lax.broadcasted_iota(jnp.int32, (TOKEN_TILE, TOKEN_TILE), 1)
        lower[...] = jnp.where(c < r, 1.0, 0.0).astype(jnp.bfloat16)
        ybuf[...] = jnp.zeros_like(ybuf)
        _start_chunks(nchunk_ref, dst_ref, 0, ybuf.at[0], y_hbm, sems.at[0], False)

    @pl.when(i + 1 < n_steps)
    def _():
        _start_chunks(nchunk_ref, dst_ref, i + 1, ybuf.at[1 - slot], y_hbm, sems.at[1 - slot], False)

    rc = rc_ref[...]
    w0, w1 = rc[:, 0:1], rc[:, 1:2]
    e0, e1 = rc[:, 2:3].astype(jnp.int32), rc[:, 3:4].astype(jnp.int32)
    ids = lax.broadcasted_iota(jnp.int32, (TOKEN_TILE, ROUTER_ROWS), 1)
    member = jnp.where((ids == e0) | (ids == e1), 1.0, 0.0)
    prior = _dot(lower[...], member.astype(jnp.bfloat16))
    lo_vec = _lo_vector(lo_ref, i, (1, ROUTER_ROWS), 1)
    d0 = _local_slots(member, lo_vec, prior, e0, 1)
    d1 = _local_slots(member, lo_vec, prior, e1, 1)

    local = ybuf.at[slot]
    _wait_chunks(nchunk_ref, i, local, y_hbm, sems.at[slot])
    for blk in range(TOKEN_TILE // FINAL_BLOCK):
        rows = slice(blk * FINAL_BLOCK, (blk + 1) * FINAL_BLOCK)
        cols = lax.broadcasted_iota(jnp.int32, (FINAL_BLOCK, LOCAL_ROWS), 1)
        pick = jnp.where(cols == d0[rows], w0[rows], jnp.where(cols == d1[rows], w1[rows], 0.0))
        x2 = x1_ref[rows, :] + _dot(pick.astype(jnp.bfloat16), local[...])
        z = _dot(_rms(x2, gple_ref[...]).astype(jnp.bfloat16), wg_ref[...])
        gate = 1.0 / (1.0 + jnp.exp(-z))
        x3 = x2 + _dot(p_ref[rows, :].astype(jnp.bfloat16), wp_ref[...]) * gate
        out_ref[rows, :] = _rms(x3, gfin_ref[...])


def _final(lo, nchunk, dst, y, x1, rc, p2d, g_ple, w_gate_bf, w_proj_bf, g_final):
    t = x1.shape[0]
    tm = TOKEN_TILE
    row = lambda w: pl.BlockSpec((tm, w), lambda i, *_: (i, 0))
    full = lambda a: pl.BlockSpec(a.shape, lambda i, *_: (0, 0))
    grid_spec = pltpu.PrefetchScalarGridSpec(
        num_scalar_prefetch=3,
        grid=(t // tm,),
        in_specs=[pl.BlockSpec(memory_space=pl.ANY), row(D_MODEL), row(EXPERTS_PER_GROUP),
                  row(PLE_DIM), full(g_ple), full(w_gate_bf), full(w_proj_bf), full(g_final)],
        out_specs=row(D_MODEL),
        scratch_shapes=[pltpu.VMEM((TOKEN_TILE, TOKEN_TILE), jnp.bfloat16),
                        pltpu.VMEM((2, LOCAL_ROWS, D_MODEL), jnp.bfloat16),
                        pltpu.SemaphoreType.DMA((2,))],
    )
    return pl.pallas_call(
        _final_kernel,
        out_shape=jax.ShapeDtypeStruct((t, D_MODEL), jnp.float32),
        grid_spec=grid_spec,
        compiler_params=pltpu.CompilerParams(dimension_semantics=("arbitrary",),
                                             vmem_limit_bytes=VMEM_LIMIT),
        name="combine_ple_final",
    )(lo, nchunk, dst, y, x1, rc, p2d, g_ple, w_gate_bf, w_proj_bf, g_final)


def _dispatch_tables(counts):
    n_tt = counts.shape[0]
    p_rows = 2 * n_tt * TOKEN_TILE + n_tt * N_EXPERTS * (RUN_ALIGN - 1) + N_EXPERTS * ROW_TILE
    p_rows = -(-p_rows // ROW_TILE) * ROW_TILE
    n_tiles = p_rows // ROW_TILE
    n8 = (counts + RUN_ALIGN - 1) // RUN_ALIGN * RUN_ALIGN
    lo = jnp.cumsum(n8, axis=1) - n8
    total = jnp.sum(n8, axis=0)
    tiles_per = (total + ROW_TILE - 1) // ROW_TILE
    tile_end = jnp.cumsum(tiles_per)
    tile_start = tile_end - tiles_per
    g = tile_start[None, :] * ROW_TILE + (jnp.cumsum(n8, axis=0) - n8)
    n_used = tile_end[-1]
    tile_ids = jnp.arange(n_tiles, dtype=jnp.int32)
    clamped = jnp.minimum(tile_ids, n_used - 1)
    texp = jnp.sum((tile_end[None, :] <= clamped[:, None]).astype(jnp.int32), axis=1)
    experts = jnp.arange(N_EXPERTS, dtype=jnp.int32)
    of_tile = texp[:, None] == experts[None, :]
    per_tile = lambda v: jnp.sum(jnp.where(of_tile, v[None, :], 0), axis=1)
    first = ((tile_ids == per_tile(tile_start)) & (tile_ids < n_used)).astype(jnp.int32)
    pad_tile = jnp.where(total > 0, tile_end - 1, -1).astype(jnp.int32)
    has_rows = total > 0
    later = has_rows[None, :] & (experts[None, :] > experts[:, None])
    next_expert = jnp.min(jnp.where(later, experts[None, :], N_EXPERTS), axis=1)
    next_expert = jnp.where(next_expert < N_EXPERTS, next_expert, -1)
    slot = (jnp.cumsum(has_rows.astype(jnp.int32)) - 1) % 2
    chunk_row = jnp.arange(MAX_CHUNKS, dtype=jnp.int32)[None, :, None] * RUN_ALIGN
    in_run = (chunk_row >= lo[:, None, :]) & (chunk_row < (lo + n8)[:, None, :])
    dst = jnp.sum(jnp.where(in_run, g[:, None, :] + chunk_row - lo[:, None, :], 0), axis=2)
    nchunk = jnp.sum(n8, axis=1) // RUN_ALIGN
    i32 = lambda a: a.reshape(-1).astype(jnp.int32)
    return (texp, first, n_used.reshape(1).astype(jnp.int32), i32(per_tile(slot)),
            i32(per_tile(next_expert)), pad_tile, i32(lo), i32(nchunk), i32(dst), p_rows)


def kernel(x, p, g_mix, w_in, sink, g_grp_a, g_grp_b, w_out, g_ffn, w_router_group, b_router_group,
           w_router_expert, b_router_expert, w_expert_gate, w_expert_up, w_expert_down, g_ple,
           w_ple_gate, w_ple_proj, g_final):
    batch, seq, _ = x.shape
    n_tok = batch * seq
    depth = w_in.shape[0]
    assert depth == 1, "the final RMSNorm is fused into the last layer's epilogue"
    bf = jnp.bfloat16
    xc = x.reshape(n_tok, D_MODEL)
    for i in range(depth):
        *qkv_a, qb, kb, vb = _proj(xc, g_mix[i][None, :], w_in[i].astype(bf), batch, seq)
        oa = _attn_a(qkv_a, batch, seq)
        ob = _attn_b(qb, kb, vb, sink[i], batch, seq)

        wr = jnp.zeros((ROUTER_ROWS, D_MODEL), jnp.float32)
        wr = wr.at[0:N_GROUPS].set(w_router_group[i].T)
        wr = wr.at[EXPERTS_PER_GROUP:EXPERTS_PER_GROUP + N_EXPERTS].set(w_router_expert[i].T)
        br = jnp.zeros((ROUTER_ROWS, 1), jnp.float32)
        br = br.at[0:N_GROUPS, 0].set(b_router_group[i])
        br = br.at[EXPERTS_PER_GROUP:EXPERTS_PER_GROUP + N_EXPERTS, 0].set(b_router_expert[i])

        x1, h2, route_i, route_c, counts = _mix(
            oa, ob, xc, g_grp_a[i][None, :], g_grp_b[i][None, :], w_out[i].astype(bf), g_ffn[i][None, :], wr, br)

        texp, first, nused, slot, nxt, pad_tile, lo, nchunk, dst, p_rows = _dispatch_tables(counts[:, :, 0])
        xs = _dispatch(lo, nchunk, dst, pad_tile, nused, route_i, h2, p_rows)
        y = _moe(texp, first, nused, slot, nxt, xs, w_expert_gate[i], w_expert_up[i], w_expert_down[i])
        xc = _final(lo, nchunk, dst, y, x1, route_c, p[i].reshape(n_tok, PLE_DIM), g_ple[i][None, :],
                    w_ple_gate[i].astype(bf), w_ple_proj[i].astype(bf), g_final[None, :])
    return xc.reshape(batch, seq, D_MODEL)
```

```python
import functools

import jax
import jax.numpy as jnp
from jax import lax
from jax.experimental import pallas as pl
from jax.experimental.pallas import tpu as pltpu

D_MODEL = 1024
HEAD_DIM = 64
GROUP_WIDTH = 512
N_HEADS = 8
B_KV_HEADS = 2
B_KV_WIDTH = B_KV_HEADS * HEAD_DIM
A_CONFIGS = ((128, 1), (512, 4), (2048, 16))
B_HALF_WINDOW = 128
N_GROUPS = 4
EXPERTS_PER_GROUP = 8
N_EXPERTS = N_GROUPS * EXPERTS_PER_GROUP
D_EXPERT = 512
PLE_DIM = 256
EPS = 1e-6
NEG = -1e30
LOG2E = 1.4426950408889634

LANES = 128
Q_BLOCK = 128
HEADS_PER_PASS = 2
PASS_WIDTH = HEADS_PER_PASS * HEAD_DIM
ROW_TILE = 448
TOKEN_TILE = 512
ROUTER_ROWS = 128
VMEM_LIMIT = 48 * 1024 * 1024
ATTN_VMEM_LIMIT = 56 * 1024 * 1024


def _rms(x, g):
    r = lax.rsqrt(jnp.mean(x * x, axis=-1, keepdims=True) + EPS)
    return (x * r) * g


def _dot(a, b):
    return jnp.dot(a, b, preferred_element_type=jnp.float32)


PROJ_TILE = 1024


def _proj_kernel(x_ref, g_ref, w_ref, *refs):
    a_refs, (qb_ref, kb_ref, vb_ref, res, *stage_refs) = refs[:9], refs[9:]
    h = _rms(x_ref[...], g_ref[...]).astype(jnp.bfloat16)
    scale = HEAD_DIM ** -0.5 * LOG2E
    W = GROUP_WIDTH
    tiles = GROUP_WIDTH // PASS_WIDTH
    qkv_a = (_dot(h, w_ref[:, 0:W]) * scale, _dot(h, w_ref[:, W:2 * W]), _dot(h, w_ref[:, 2 * W:3 * W]))
    for part in range(3):
        for pair in range(tiles):
            res[part * tiles + pair, 0] = qkv_a[part][:, pair * PASS_WIDTH:(pair + 1) * PASS_WIDTH]
    qkv_b = _dot(h, w_ref[:, 3 * W:4 * W + 2 * B_KV_WIDTH])
    qb_ref[...] = (qkv_b[:, 0:W] * scale).astype(jnp.bfloat16)
    kb_ref[...] = qkv_b[:, W:W + B_KV_WIDTH].astype(jnp.bfloat16)
    vb_ref[...] = qkv_b[:, W + B_KV_WIDTH:W + 2 * B_KV_WIDTH].astype(jnp.bfloat16)
    stages = (res,) + tuple(stage_refs)
    prev_dil = 1
    for c, (_, dil) in enumerate(A_CONFIGS):
        ratio = dil // prev_dil
        rows = PROJ_TILE // dil
        for idx in range(3 * tiles):
            part, pair = idx // tiles, idx % tiles
            out = a_refs[3 * c + part]
            for rho in range(dil):
                if c == 0:
                    sub = res[idx, 0]
                else:
                    sub = stages[c - 1].at[idx, rho % prev_dil][pl.ds(rho // prev_dil, rows, stride=ratio), :]
                    if c < len(A_CONFIGS) - 1:
                        stages[c][idx, rho] = sub
                out[pair, rho] = sub.astype(jnp.bfloat16)
        prev_dil = dil


def _proj(x2d, g_mix, w_in_bf, batch, seq):
    t = x2d.shape[0]
    tm = PROJ_TILE
    tiles_per_seq = seq // tm
    in_width = w_in_bf.shape[1]
    n_pairs = N_HEADS // HEADS_PER_PASS
    row = lambda w: pl.BlockSpec((tm, w), lambda i: (i, 0))
    a_shapes, a_specs = [], []
    for _, dil in A_CONFIGS:
        for _ in range(3):
            a_shapes.append(jax.ShapeDtypeStruct((batch, n_pairs, dil, seq // dil, PASS_WIDTH),
                                                 jnp.bfloat16))
            a_specs.append(pl.BlockSpec((None, n_pairs, dil, tm // dil, PASS_WIDTH),
                                        lambda i: (i // tiles_per_seq, 0, 0, i % tiles_per_seq, 0)))
    wide = jax.ShapeDtypeStruct((t, GROUP_WIDTH), jnp.bfloat16)
    narrow = jax.ShapeDtypeStruct((t, B_KV_WIDTH), jnp.bfloat16)
    return pl.pallas_call(
        _proj_kernel,
        out_shape=tuple(a_shapes) + (wide, narrow, narrow),
        grid=(t // tm,),
        in_specs=[row(D_MODEL),
                  pl.BlockSpec((1, D_MODEL), lambda i: (0, 0)),
                  pl.BlockSpec((D_MODEL, in_width), lambda i: (0, 0), pipeline_mode=pl.Buffered(1))],
        out_specs=tuple(a_specs) + (row(GROUP_WIDTH), row(B_KV_WIDTH), row(B_KV_WIDTH)),
        scratch_shapes=[pltpu.VMEM((3 * GROUP_WIDTH // PASS_WIDTH, dil, tm // dil, PASS_WIDTH), jnp.float32)
                        for _, dil in A_CONFIGS[:-1]],
        compiler_params=pltpu.CompilerParams(dimension_semantics=("arbitrary",),
                                             vmem_limit_bytes=ATTN_VMEM_LIMIT),
        name="proj",
    )(x2d, g_mix, w_in_bf)


def _fill_bias(bias_ref, half, nkeys, dist_scale):
    i = lax.broadcasted_iota(jnp.int32, (Q_BLOCK, nkeys), 0)
    j = lax.broadcasted_iota(jnp.int32, (Q_BLOCK, nkeys), 1)
    for var, rel0 in enumerate((0, -half, Q_BLOCK - nkeys)):
        d = jnp.abs(j + rel0 - i)
        dist = d.astype(jnp.float32) * dist_scale
        for h in range(N_HEADS):
            slope = 2.0 ** (-(h + 1))
            bias_ref[var, h] = jnp.where(d <= half, -(slope * dist) * LOG2E, NEG)


def _key_window(q0, seq_len, half, nkeys):
    lo = q0 - half
    if isinstance(q0, int):
        return min(max(lo, 0), seq_len - nkeys), (0 if lo < 0 else 2 if lo > seq_len - nkeys else 1)
    ks = pl.multiple_of(jnp.clip(lo, 0, seq_len - nkeys), HEAD_DIM)
    var = jnp.where(lo < 0, 0, jnp.where(lo > seq_len - nkeys, 2, 1))
    return ks, var


def _fold_lane_tiles(a, op):
    out = a[..., 0:LANES]
    for t in range(1, a.shape[-1] // LANES):
        out = op(out, a[..., t * LANES:(t + 1) * LANES])
    return out


def _attend_pass(q, k, v, bias, sinks):
    nkeys = k.shape[0]
    lane_head = lax.broadcasted_iota(jnp.int32, (Q_BLOCK, PASS_WIDTH), 1) // HEAD_DIM
    zero = jnp.zeros_like(q)
    q_stack = jnp.concatenate(
        [jnp.where(lane_head == h, q, zero) for h in range(HEADS_PER_PASS)], axis=0)
    s = lax.dot_general(q_stack, k, (((1,), (1,)), ((), ())), preferred_element_type=jnp.float32)
    s = s.reshape(HEADS_PER_PASS, Q_BLOCK, nkeys) + bias
    m_tile = _fold_lane_tiles(s, jnp.maximum)
    if sinks is not None:
        tile_head = lax.broadcasted_iota(jnp.int32, m_tile.shape, 0)
        tile_lane = lax.broadcasted_iota(jnp.int32, m_tile.shape, 2)
        sk = jnp.zeros(m_tile.shape, jnp.float32)
        for h in range(HEADS_PER_PASS):
            sk = jnp.where(tile_head == h, sinks[h] * LOG2E, sk)
        m_tile = jnp.maximum(m_tile, sk)
    m = jnp.max(m_tile, axis=-1, keepdims=True)
    p = jnp.exp2(s - m)
    den_tile = _fold_lane_tiles(p, jnp.add)
    if sinks is not None:
        den_tile = den_tile + jnp.where(tile_lane == 0, jnp.exp2(sk - m), 0.0)
    den = jnp.sum(den_tile, axis=-1, keepdims=True)
    pv = _dot(p.reshape(HEADS_PER_PASS * Q_BLOCK, nkeys).astype(jnp.bfloat16), v)
    pv = pv.reshape(HEADS_PER_PASS, Q_BLOCK, PASS_WIDTH)
    shape = (Q_BLOCK, PASS_WIDTH)
    o, m_lanes, den_lanes = (jnp.zeros(shape, jnp.float32) for _ in range(3))
    for h in range(HEADS_PER_PASS):
        o = jnp.where(lane_head == h, pv[h], o)
        m_lanes = jnp.where(lane_head == h, m[h], m_lanes)
        den_lanes = jnp.where(lane_head == h, den[h], den_lanes)
    return o, m_lanes, den_lanes


A_HALF = 64
A_KEYS = Q_BLOCK + 2 * A_HALF
PASSES_PER_BODY = 32
B_Q_TILE = 1024
KV_REPLICATE_ROWS = 512
NO_INDEX = 1 << 20
MERGE_BLOCK = 256


def _attn_a_kernel(*refs, seq):
    n_cfg = len(A_CONFIGS)
    qkv = refs[:3 * n_cfg]
    o_ref, bias_ref = refs[3 * n_cfg], refs[3 * n_cfg + 1]
    scratch = refs[3 * n_cfg + 2:]
    pair = pl.program_id(1)

    @pl.when((pl.program_id(0) == 0) & (pair == 0))
    def _():
        for c, (_, dil) in enumerate(A_CONFIGS):
            _fill_bias(bias_ref.at[c], A_HALF, A_KEYS, float(dil))

    def attend(c, idx):
        window, dil = A_CONFIGS[c]
        assert window // (2 * dil) == A_HALF
        q_ref, k_ref, v_ref = qkv[3 * c:3 * c + 3]
        o_sc, m_sc, d_sc = scratch[3 * c:3 * c + 3]
        sub_len = seq // dil
        rho, q0 = divmod(idx, sub_len // Q_BLOCK)
        q0 *= Q_BLOCK
        ks, var = _key_window(q0, sub_len, A_HALF, A_KEYS)
        bias = bias_ref[c, var, pl.ds(pair * HEADS_PER_PASS, HEADS_PER_PASS)]
        o, m, den = _attend_pass(q_ref[rho, pl.ds(q0, Q_BLOCK), :], k_ref[rho, pl.ds(ks, A_KEYS), :],
                                 v_ref[rho, pl.ds(ks, A_KEYS), :], bias, None)
        tokens = pl.ds(rho + dil * q0, Q_BLOCK, stride=dil) if dil > 1 else pl.ds(q0, Q_BLOCK)
        o_sc[tokens, :] = o
        m_sc[tokens, :] = m
        d_sc[tokens, :] = den

    def merge(blk, carry):
        rows = pl.ds(pl.multiple_of(blk * MERGE_BLOCK, MERGE_BLOCK), MERGE_BLOCK)
        pvs = [scratch[3 * c][rows, :] for c in range(n_cfg)]
        ms = [scratch[3 * c + 1][rows, :] for c in range(n_cfg)]
        dens = [scratch[3 * c + 2][rows, :] for c in range(n_cfg)]
        mx = functools.reduce(jnp.maximum, ms)
        es = [jnp.exp2(m - mx) for m in ms]
        total = functools.reduce(jnp.add, [e * d for e, d in zip(es, dens)])
        o_ref[rows, :] = functools.reduce(jnp.add, [e * pv for e, pv in zip(es, pvs)]) * (1.0 / total)
        return carry

    for c in range(n_cfg):
        for idx in range(seq // Q_BLOCK):
            attend(c, idx)
    lax.fori_loop(0, seq // MERGE_BLOCK, merge, 0, unroll=2)


def _attn_a(qkv, batch, seq):
    n_pairs = N_HEADS // HEADS_PER_PASS
    in_specs, scratch = [], [pltpu.VMEM((len(A_CONFIGS), 3, N_HEADS, Q_BLOCK, A_KEYS), jnp.float32)]
    for _, dil in A_CONFIGS:
        blk = (None, None, dil, seq // dil, PASS_WIDTH)
        in_specs += [pl.BlockSpec(blk, lambda b, p: (b, p, 0, 0, 0))] * 3
        scratch += [pltpu.VMEM((seq, PASS_WIDTH), jnp.float32)] * 3
    return pl.pallas_call(
        functools.partial(_attn_a_kernel, seq=seq),
        out_shape=jax.ShapeDtypeStruct((batch, seq, GROUP_WIDTH), jnp.float32),
        grid=(batch, n_pairs),
        in_specs=in_specs,
        out_specs=pl.BlockSpec((None, seq, PASS_WIDTH), lambda b, p: (b, 0, p)),
        scratch_shapes=scratch,
        compiler_params=pltpu.CompilerParams(dimension_semantics=("arbitrary", "arbitrary"),
                                             vmem_limit_bytes=ATTN_VMEM_LIMIT),
        name="attn_a",
    )(*qkv).reshape(batch * seq, GROUP_WIDTH)


def _attn_b_kernel(sink_ref, q_ref, k_ref, v_ref, o_ref, bias_ref, k4_ref, v4_ref, *, half, nkeys,
                   seq_len, q_tile):
    first = (pl.program_id(0) == 0) & (pl.program_id(1) == 0)

    @pl.when(first)
    def _():
        _fill_bias(bias_ref, half, nkeys, 1.0)

    @pl.when(pl.program_id(1) == 0)
    def _():
        c = lax.broadcasted_iota(jnp.int32, (B_KV_WIDTH, GROUP_WIDTH), 0)
        j = lax.broadcasted_iota(jnp.int32, (B_KV_WIDTH, GROUP_WIDTH), 1)
        src = (j // (GROUP_WIDTH // B_KV_HEADS)) * HEAD_DIM + j % HEAD_DIM
        rep = jnp.where(c == src, 1.0, 0.0).astype(jnp.bfloat16)
        chunk = KV_REPLICATE_ROWS

        def body(i, carry):
            rows = pl.ds(pl.multiple_of(i * chunk, chunk), chunk)
            k4_ref[rows, :] = _dot(k_ref[rows, :], rep).astype(jnp.bfloat16)
            v4_ref[rows, :] = _dot(v_ref[rows, :], rep).astype(jnp.bfloat16)
            return carry

        lax.fori_loop(0, seq_len // chunk, body, 0)

    base = pl.program_id(1) * q_tile

    def body(jb, carry):
        row0 = pl.multiple_of(jb * Q_BLOCK, Q_BLOCK)
        ks, var = _key_window(base + row0, seq_len, half, nkeys)
        for g in range(N_HEADS // HEADS_PER_PASS):
            cols = slice(g * PASS_WIDTH, (g + 1) * PASS_WIDTH)
            heads = pl.ds(g * HEADS_PER_PASS, HEADS_PER_PASS)
            sinks = [sink_ref[g * HEADS_PER_PASS + h] for h in range(HEADS_PER_PASS)]
            o, _, den = _attend_pass(q_ref[pl.ds(row0, Q_BLOCK), cols], k4_ref[pl.ds(ks, nkeys), cols],
                                     v4_ref[pl.ds(ks, nkeys), cols], bias_ref[var, heads], sinks)
            o_ref[pl.ds(row0, Q_BLOCK), cols] = (o * (1.0 / den)).astype(o_ref.dtype)
        return carry

    lax.fori_loop(0, q_tile // Q_BLOCK, body, 0,
                  unroll=PASSES_PER_BODY // (N_HEADS // HEADS_PER_PASS))


def _attn_b(q, k, v, sink, batch, seq):
    half = B_HALF_WINDOW
    nkeys = Q_BLOCK + 2 * half
    q_tile = B_Q_TILE
    view = lambda a: a.reshape(batch, seq, a.shape[-1])
    kern = functools.partial(_attn_b_kernel, half=half, nkeys=nkeys, seq_len=seq, q_tile=q_tile)
    o = pl.pallas_call(
        kern,
        out_shape=jax.ShapeDtypeStruct((batch, seq, GROUP_WIDTH), jnp.bfloat16),
        grid=(batch, seq // q_tile),
        in_specs=[pl.BlockSpec(memory_space=pltpu.SMEM),
                  pl.BlockSpec((None, q_tile, GROUP_WIDTH), lambda b, i: (b, i, 0)),
                  pl.BlockSpec((None, seq, B_KV_WIDTH), lambda b, i: (b, 0, 0)),
                  pl.BlockSpec((None, seq, B_KV_WIDTH), lambda b, i: (b, 0, 0))],
        out_specs=pl.BlockSpec((None, q_tile, GROUP_WIDTH), lambda b, i: (b, i, 0)),
        scratch_shapes=[pltpu.VMEM((3, N_HEADS, Q_BLOCK, nkeys), jnp.float32),
                        pltpu.VMEM((seq, GROUP_WIDTH), jnp.bfloat16),
                        pltpu.VMEM((seq, GROUP_WIDTH), jnp.bfloat16)],
        compiler_params=pltpu.CompilerParams(dimension_semantics=("arbitrary", "arbitrary"),
                                             vmem_limit_bytes=VMEM_LIMIT),
        name="attn_b",
    )(sink, view(q), view(k), view(v))
    return o.reshape(batch * seq, GROUP_WIDTH)


def _mix_kernel(oa_ref, ob_ref, x_ref, ga_ref, gb_ref, wout_ref, gffn_ref, wr_ref, br_ref,
                x1_ref, h2_ref, ri_ref, rc_ref, cnt_ref):
    tm = x_ref.shape[0]
    na = _rms(oa_ref[...], ga_ref[...]).astype(jnp.bfloat16)
    nb = _rms(ob_ref[...].astype(jnp.float32), gb_ref[...]).astype(jnp.bfloat16)
    x1 = x_ref[...] + _dot(jnp.concatenate([na, nb], axis=1), wout_ref[...])
    x1_ref[...] = x1
    h2 = _rms(x1, gffn_ref[...])
    h2_ref[...] = h2.astype(jnp.bfloat16)

    h_hi = h2.astype(jnp.bfloat16)
    h_lo = (h2 - h_hi.astype(jnp.float32)).astype(jnp.bfloat16)
    wr = wr_ref[...]
    w_hi = wr.astype(jnp.bfloat16)
    w_lo = (wr - w_hi.astype(jnp.float32)).astype(jnp.bfloat16)
    w_cat = jnp.concatenate([w_hi, w_lo], axis=0)
    nt = (((1,), (1,)), ((), ()))
    lg2 = (lax.dot_general(w_cat, h_hi, nt, preferred_element_type=jnp.float32)
           + lax.dot_general(w_cat, h_lo, nt, preferred_element_type=jnp.float32))
    lgt = lg2[0:ROUTER_ROWS] + lg2[ROUTER_ROWS:2 * ROUTER_ROWS] + br_ref[...]

    rows = lax.broadcasted_iota(jnp.int32, (EXPERTS_PER_GROUP, tm), 0)
    big = jnp.int32(NO_INDEX)
    glog = jnp.where(rows < N_GROUPS, lgt[0:EXPERTS_PER_GROUP], -jnp.inf)
    gmax = jnp.max(glog, axis=0, keepdims=True)
    gsel = jnp.min(jnp.where(glog == gmax, rows, big), axis=0, keepdims=True)
    gw = 1.0 / jnp.sum(jnp.exp(glog - gmax), axis=0, keepdims=True)
    esel = jnp.zeros((EXPERTS_PER_GROUP, tm), jnp.float32)
    for grp in range(N_GROUPS):
        lo = EXPERTS_PER_GROUP * (grp + 1)
        esel = jnp.where(gsel == grp, lgt[lo:lo + EXPERTS_PER_GROUP], esel)
    v0 = jnp.max(esel, axis=0, keepdims=True)
    i0 = jnp.min(jnp.where(esel == v0, rows, big), axis=0, keepdims=True)
    rest = jnp.where(rows == i0, -jnp.inf, esel)
    v1 = jnp.max(rest, axis=0, keepdims=True)
    i1 = jnp.min(jnp.where(rest == v1, rows, big), axis=0, keepdims=True)
    e = jnp.exp(v1 - v0)
    w0 = (1.0 / (1.0 + e)) * gw
    w1 = (e / (1.0 + e)) * gw
    eid0 = gsel * EXPERTS_PER_GROUP + i0
    eid1 = gsel * EXPERTS_PER_GROUP + i1
    ri_ref[...] = jnp.where(rows == 0, eid0, jnp.where(rows == 1, eid1, 0))
    rows_t = lax.broadcasted_iota(jnp.int32, (ROUTER_ROWS, tm), 0)
    slab = jnp.where(rows_t == 0, w0, jnp.where(rows_t == 1, w1, 0.0))
    slab = jnp.where(rows_t == 2, eid0.astype(jnp.float32),
                     jnp.where(rows_t == 3, eid1.astype(jnp.float32), slab))
    rc_ref[...] = slab.T[:, 0:EXPERTS_PER_GROUP]
    ids = lax.broadcasted_iota(jnp.int32, (N_EXPERTS, tm), 0)
    member = jnp.where((ids == eid0) | (ids == eid1), 1.0, 0.0)
    cnt_ref[...] = jnp.sum(member, axis=1, keepdims=True).astype(jnp.int32)


def _mix(oa, ob, x2d, g_a, g_b, w_out_bf, g_ffn, wr, br):
    t = x2d.shape[0]
    tm = TOKEN_TILE
    row = lambda w: pl.BlockSpec((tm, w), lambda i: (i, 0))
    full = lambda a: pl.BlockSpec(a.shape, lambda i: (0, 0))
    return pl.pallas_call(
        _mix_kernel,
        out_shape=(jax.ShapeDtypeStruct((t, D_MODEL), jnp.float32),
                   jax.ShapeDtypeStruct((t, D_MODEL), jnp.bfloat16),
                   jax.ShapeDtypeStruct((EXPERTS_PER_GROUP, t), jnp.int32),
                   jax.ShapeDtypeStruct((t, EXPERTS_PER_GROUP), jnp.float32),
                   jax.ShapeDtypeStruct((t // tm, N_EXPERTS, 1), jnp.int32)),
        grid=(t // tm,),
        in_specs=[row(GROUP_WIDTH), row(GROUP_WIDTH), row(D_MODEL),
                  full(g_a), full(g_b), full(w_out_bf), full(g_ffn), full(wr), full(br)],
        out_specs=(row(D_MODEL), row(D_MODEL),
                   pl.BlockSpec((EXPERTS_PER_GROUP, tm), lambda i: (0, i)),
                   row(EXPERTS_PER_GROUP),
                   pl.BlockSpec((None, N_EXPERTS, 1), lambda i: (i, 0, 0))),
        compiler_params=pltpu.CompilerParams(dimension_semantics=("arbitrary",),
                                             vmem_limit_bytes=VMEM_LIMIT),
        name="mix_router",
    )(oa, ob, x2d, g_a, g_b, w_out_bf, g_ffn, wr, br)


RUN_ALIGN = 16
LOCAL_ROWS = 2 * TOKEN_TILE + N_EXPERTS * RUN_ALIGN
MAX_CHUNKS = LOCAL_ROWS // RUN_ALIGN
START_UNROLL = 4
SORT_BLOCK = 256
FINAL_BLOCK = 256


def _local_slots(member, lo_vec, prior, e_sel, axis):
    ids = lax.broadcasted_iota(jnp.int32, member.shape, axis)
    slot = jnp.where(ids == e_sel, lo_vec + prior, 0.0)
    return jnp.sum(slot, axis=axis, keepdims=True).astype(jnp.int32)


def _chunk_copy(dst_ref, tile, c, local, glob, sem, to_global):
    g = pl.multiple_of(dst_ref[tile * MAX_CHUNKS + c], RUN_ALIGN)
    l = pl.multiple_of(c * RUN_ALIGN, RUN_ALIGN)
    src, dst = local.at[pl.ds(l, RUN_ALIGN), :], glob.at[pl.ds(g, RUN_ALIGN), :]
    if not to_global:
        src, dst = dst, src
    return pltpu.make_async_copy(src, dst, sem)


def _start_chunks(nchunk_ref, dst_ref, tile, local, glob, sem, to_global):
    n = nchunk_ref[tile]
    groups = n // START_UNROLL

    def group(gi, carry):
        for u in range(START_UNROLL):
            _chunk_copy(dst_ref, tile, gi * START_UNROLL + u, local, glob, sem, to_global).start()
        return carry

    def single(c, carry):
        _chunk_copy(dst_ref, tile, groups * START_UNROLL + c, local, glob, sem, to_global).start()
        return carry

    lax.fori_loop(0, groups, group, 0)
    lax.fori_loop(0, n % START_UNROLL, single, 0)


def _wait_chunks(nchunk_ref, tile, local, glob, sem):
    n = nchunk_ref[tile]
    size = 1
    while size <= MAX_CHUNKS:
        rows = size * RUN_ALIGN

        @pl.when((n // size) % 2 == 1)
        def _(rows=rows):
            pltpu.make_async_copy(local.at[pl.ds(0, rows), :], glob.at[pl.ds(0, rows), :], sem).wait()

        size *= 2


def _lo_vector(lo_ref, tile, shape, axis):
    ids = lax.broadcasted_iota(jnp.int32, shape, axis)
    vec = jnp.zeros(shape, jnp.float32)
    for e in range(N_EXPERTS):
        vec = jnp.where(ids == e, lo_ref[tile * N_EXPERTS + e].astype(jnp.float32), vec)
    return vec


def _dispatch_kernel(lo_ref, nchunk_ref, dst_ref, pad_tile_ref, nused_ref, ri_ref, h_ref, xs_hbm,
                     upper, sbuf, zbuf, sem, zsem, tail_sem, *, n_tiles):
    i = pl.program_id(0)
    n_steps = pl.num_programs(0)
    slot = i % 2

    def zero_tile(tile, zero_sem):
        rows = pl.ds(pl.multiple_of(tile * ROW_TILE, ROW_TILE), ROW_TILE)
        return pltpu.make_async_copy(zbuf, xs_hbm.at[rows, :], zero_sem)

    @pl.when(i == 0)
    def _():
        r = lax.broadcasted_iota(jnp.int32, (TOKEN_TILE, TOKEN_TILE), 0)
        c = lax.broadcasted_iota(jnp.int32, (TOKEN_TILE, TOKEN_TILE), 1)
        upper[...] = jnp.where(r < c, 1.0, 0.0).astype(jnp.bfloat16)
        zbuf[...] = jnp.zeros_like(zbuf)
        for e in range(N_EXPERTS):
            @pl.when(pad_tile_ref[e] >= 0)
            def _():
                zero_tile(pad_tile_ref[e], zsem).start()
        lax.fori_loop(nused_ref[0], n_tiles, lambda t, c: (zero_tile(t, tail_sem).start(), c)[1], 0)
        for e in range(N_EXPERTS):
            @pl.when(pad_tile_ref[e] >= 0)
            def _():
                zero_tile(pad_tile_ref[e], zsem).wait()

    e0 = ri_ref[0:1, :]
    e1 = ri_ref[1:2, :]
    ids = lax.broadcasted_iota(jnp.int32, (N_EXPERTS, TOKEN_TILE), 0)
    member = jnp.where((ids == e0) | (ids == e1), 1.0, 0.0)
    prior = _dot(member.astype(jnp.bfloat16), upper[...])
    lo_vec = _lo_vector(lo_ref, i, (N_EXPERTS, 1), 0)
    d0 = _local_slots(member, lo_vec, prior, e0, 0)
    d1 = _local_slots(member, lo_vec, prior, e1, 0)
    local = sbuf.at[slot]
    used_rows = nchunk_ref[i] * RUN_ALIGN
    h = h_ref[...]
    def sort_block(blk):
        rows = blk * SORT_BLOCK + lax.broadcasted_iota(jnp.int32, (SORT_BLOCK, TOKEN_TILE), 0)
        select = jnp.where((rows == d0) | (rows == d1), 1.0, 0.0).astype(jnp.bfloat16)
        local[blk * SORT_BLOCK:(blk + 1) * SORT_BLOCK, :] = _dot(select, h).astype(jnp.bfloat16)

    for blk in range(LOCAL_ROWS // SORT_BLOCK):
        if (blk + 1) * SORT_BLOCK <= 2 * TOKEN_TILE:
            sort_block(blk)
        else:
            pl.when(blk * SORT_BLOCK < used_rows)(functools.partial(sort_block, blk))

    @pl.when(i > 0)
    def _():
        _wait_chunks(nchunk_ref, i - 1, sbuf.at[1 - slot], xs_hbm, sem)

    _start_chunks(nchunk_ref, dst_ref, i, local, xs_hbm, sem, True)

    @pl.when(i == n_steps - 1)
    def _():
        _wait_chunks(nchunk_ref, i, local, xs_hbm, sem)
        lax.fori_loop(nused_ref[0], n_tiles, lambda t, c: (zero_tile(t, tail_sem).wait(), c)[1], 0)


def _dispatch(lo, nchunk, dst, pad_tile, nused, route_i, h2, p_rows):
    n_tok = h2.shape[0]
    grid_spec = pltpu.PrefetchScalarGridSpec(
        num_scalar_prefetch=5,
        grid=(n_tok // TOKEN_TILE,),
        in_specs=[pl.BlockSpec((EXPERTS_PER_GROUP, TOKEN_TILE), lambda i, *_: (0, i)),
                  pl.BlockSpec((TOKEN_TILE, D_MODEL), lambda i, *_: (i, 0))],
        out_specs=pl.BlockSpec(memory_space=pl.ANY),
        scratch_shapes=[pltpu.VMEM((TOKEN_TILE, TOKEN_TILE), jnp.bfloat16),
                        pltpu.VMEM((2, LOCAL_ROWS, D_MODEL), jnp.bfloat16),
                        pltpu.VMEM((ROW_TILE, D_MODEL), jnp.bfloat16),
                        pltpu.SemaphoreType.DMA, pltpu.SemaphoreType.DMA, pltpu.SemaphoreType.DMA],
    )
    return pl.pallas_call(
        functools.partial(_dispatch_kernel, n_tiles=p_rows // ROW_TILE),
        out_shape=jax.ShapeDtypeStruct((p_rows, D_MODEL), jnp.bfloat16),
        grid_spec=grid_spec,
        compiler_params=pltpu.CompilerParams(dimension_semantics=("arbitrary",),
                                             vmem_limit_bytes=VMEM_LIMIT),
        name="dispatch",
    )(lo, nchunk, dst, pad_tile, nused, route_i, h2)


def _moe_kernel(texp_ref, first_ref, nused_ref, slot_ref, next_ref, xs_ref, wg_hbm, wu_hbm, wd_hbm,
                y_ref, wg_f32, wu_f32, wd_f32, wg_bf, wu_bf, wd_bf, sems):
    j = pl.program_id(0)

    def fetch(expert, slot):
        return [pltpu.make_async_copy(src.at[expert], dst.at[slot], sems.at[slot, k])
                for k, (src, dst) in enumerate(((wg_hbm, wg_f32), (wu_hbm, wu_f32), (wd_hbm, wd_f32)))]

    @pl.when(j == 0)
    def _():
        for cp in fetch(texp_ref[0], 0):
            cp.start()

    @pl.when(j < nused_ref[0])
    def _():
        @pl.when(first_ref[j] == 1)
        def _():
            for s in range(2):
                @pl.when(slot_ref[j] == s)
                def _():
                    for cp in fetch(texp_ref[j], s):
                        cp.wait()
                    wg_bf[...] = wg_f32[s].astype(jnp.bfloat16)
                    wu_bf[...] = wu_f32[s].astype(jnp.bfloat16)
                    wd_bf[...] = wd_f32[s].astype(jnp.bfloat16)

                    @pl.when(next_ref[j] >= 0)
                    def _():
                        for cp in fetch(next_ref[j], 1 - s):
                            cp.start()

        x = xs_ref[...]
        a = _dot(x, wg_bf[...])
        u = _dot(x, wu_bf[...])
        hid = (a * (1.0 / (1.0 + jnp.exp(-a)))) * u
        y_ref[...] = _dot(hid.astype(jnp.bfloat16), wd_bf[...]).astype(jnp.bfloat16)

    @pl.when(j >= nused_ref[0])
    def _():
        y_ref[...] = jnp.zeros_like(y_ref)


def _moe(texp, first, nused, slot, nxt, xs, w_gate, w_up, w_down):
    p_rows = xs.shape[0]
    n_tiles = p_rows // ROW_TILE
    used = lambda j, nu: jnp.minimum(j, nu[0] - 1)
    grid_spec = pltpu.PrefetchScalarGridSpec(
        num_scalar_prefetch=5,
        grid=(n_tiles,),
        in_specs=[
            pl.BlockSpec((ROW_TILE, D_MODEL), lambda j, te, fi, nu, sl, nx: (used(j, nu), 0)),
            pl.BlockSpec(memory_space=pl.ANY),
            pl.BlockSpec(memory_space=pl.ANY),
            pl.BlockSpec(memory_space=pl.ANY),
        ],
        out_specs=pl.BlockSpec((ROW_TILE, D_MODEL), lambda j, te, fi, nu, sl, nx: (j, 0)),
        scratch_shapes=[pltpu.VMEM((2, D_MODEL, D_EXPERT), jnp.float32),
                        pltpu.VMEM((2, D_MODEL, D_EXPERT), jnp.float32),
                        pltpu.VMEM((2, D_EXPERT, D_MODEL), jnp.float32),
                        pltpu.VMEM((D_MODEL, D_EXPERT), jnp.bfloat16),
                        pltpu.VMEM((D_MODEL, D_EXPERT), jnp.bfloat16),
                        pltpu.VMEM((D_EXPERT, D_MODEL), jnp.bfloat16),
                        pltpu.SemaphoreType.DMA((2, 3))],
    )
    return pl.pallas_call(
        _moe_kernel,
        out_shape=jax.ShapeDtypeStruct((p_rows, D_MODEL), jnp.bfloat16),
        grid_spec=grid_spec,
        compiler_params=pltpu.CompilerParams(dimension_semantics=("arbitrary",),
                                             vmem_limit_bytes=VMEM_LIMIT),
        name="moe_experts",
    )(texp, first, nused, slot, nxt, xs, w_gate, w_up, w_down)


def _final_kernel(lo_ref, nchunk_ref, dst_ref, y_hbm, x1_ref, rc_ref, p_ref, gple_ref, wg_ref, wp_ref,
                  gfin_ref, out_ref, lower, ybuf, sems):
    i = pl.program_id(0)
    n_steps = pl.num_programs(0)
    slot = i % 2

    @pl.when(i == 0)
    def _():
        r = lax.broadcasted_iota(jnp.int32, (TOKEN_TILE, TOKEN_TILE), 0)
        c = lax.broadcasted_iota(jnp.int32, (TOKEN_TILE, TOKEN_TILE), 1)
        lower[...] = jnp.where(c < r, 1.0, 0.0).astype(jnp.bfloat16)
        ybuf[...] = jnp.zeros_like(ybuf)
        _start_chunks(nchunk_ref, dst_ref, 0, ybuf.at[0], y_hbm, sems.at[0], False)

    @pl.when(i + 1 < n_steps)
    def _():
        _start_chunks(nchunk_ref, dst_ref, i + 1, ybuf.at[1 - slot], y_hbm, sems.at[1 - slot], False)

    rc = rc_ref[...]
    w0, w1 = rc[:, 0:1], rc[:, 1:2]
    e0, e1 = rc[:, 2:3].astype(jnp.int32), rc[:, 3:4].astype(jnp.int32)
    ids = lax.broadcasted_iota(jnp.int32, (TOKEN_TILE, ROUTER_ROWS), 1)
    member = jnp.where((ids == e0) | (ids == e1), 1.0, 0.0)
    prior = _dot(lower[...], member.astype(jnp.bfloat16))
    lo_vec = _lo_vector(lo_ref, i, (1, ROUTER_ROWS), 1)
    d0 = _local_slots(member, lo_vec, prior, e0, 1)
    d1 = _local_slots(member, lo_vec, prior, e1, 1)

    local = ybuf.at[slot]
    _wait_chunks(nchunk_ref, i, local, y_hbm, sems.at[slot])
    for blk in range(TOKEN_TILE // FINAL_BLOCK):
        rows = slice(blk * FINAL_BLOCK, (blk + 1) * FINAL_BLOCK)
        cols = lax.broadcasted_iota(jnp.int32, (FINAL_BLOCK, LOCAL_ROWS), 1)
        pick = jnp.where(cols == d0[rows], w0[rows], jnp.where(cols == d1[rows], w1[rows], 0.0))
        x2 = x1_ref[rows, :] + _dot(pick.astype(jnp.bfloat16), local[...])
        z = _dot(_rms(x2, gple_ref[...]).astype(jnp.bfloat16), wg_ref[...])
        gate = 1.0 / (1.0 + jnp.exp(-z))
        x3 = x2 + _dot(p_ref[rows, :].astype(jnp.bfloat16), wp_ref[...]) * gate
        out_ref[rows, :] = _rms(x3, gfin_ref[...])


def _final(lo, nchunk, dst, y, x1, rc, p2d, g_ple, w_gate_bf, w_proj_bf, g_final):
    t = x1.shape[0]
    tm = TOKEN_TILE
    row = lambda w: pl.BlockSpec((tm, w), lambda i, *_: (i, 0))
    full = lambda a: pl.BlockSpec(a.shape, lambda i, *_: (0, 0))
    grid_spec = pltpu.PrefetchScalarGridSpec(
        num_scalar_prefetch=3,
        grid=(t // tm,),
        in_specs=[pl.BlockSpec(memory_space=pl.ANY), row(D_MODEL), row(EXPERTS_PER_GROUP),
                  row(PLE_DIM), full(g_ple), full(w_gate_bf), full(w_proj_bf), full(g_final)],
        out_specs=row(D_MODEL),
        scratch_shapes=[pltpu.VMEM((TOKEN_TILE, TOKEN_TILE), jnp.bfloat16),
                        pltpu.VMEM((2, LOCAL_ROWS, D_MODEL), jnp.bfloat16),
                        pltpu.SemaphoreType.DMA((2,))],
    )
    return pl.pallas_call(
        _final_kernel,
        out_shape=jax.ShapeDtypeStruct((t, D_MODEL), jnp.float32),
        grid_spec=grid_spec,
        compiler_params=pltpu.CompilerParams(dimension_semantics=("arbitrary",),
                                             vmem_limit_bytes=VMEM_LIMIT),
        name="combine_ple_final",
    )(lo, nchunk, dst, y, x1, rc, p2d, g_ple, w_gate_bf, w_proj_bf, g_final)


def _dispatch_tables(counts):
    n_tt = counts.shape[0]
    p_rows = 2 * n_tt * TOKEN_TILE + n_tt * N_EXPERTS * (RUN_ALIGN - 1) + N_EXPERTS * ROW_TILE
    p_rows = -(-p_rows // ROW_TILE) * ROW_TILE
    n_tiles = p_rows // ROW_TILE
    n8 = (counts + RUN_ALIGN - 1) // RUN_ALIGN * RUN_ALIGN
    lo = jnp.cumsum(n8, axis=1) - n8
    total = jnp.sum(n8, axis=0)
    tiles_per = (total + ROW_TILE - 1) // ROW_TILE
    tile_end = jnp.cumsum(tiles_per)
    tile_start = tile_end - tiles_per
    g = tile_start[None, :] * ROW_TILE + (jnp.cumsum(n8, axis=0) - n8)
    n_used = tile_end[-1]
    tile_ids = jnp.arange(n_tiles, dtype=jnp.int32)
    clamped = jnp.minimum(tile_ids, n_used - 1)
    texp = jnp.sum((tile_end[None, :] <= clamped[:, None]).astype(jnp.int32), axis=1)
    experts = jnp.arange(N_EXPERTS, dtype=jnp.int32)
    of_tile = texp[:, None] == experts[None, :]
    per_tile = lambda v: jnp.sum(jnp.where(of_tile, v[None, :], 0), axis=1)
    first = ((tile_ids == per_tile(tile_start)) & (tile_ids < n_used)).astype(jnp.int32)
    pad_tile = jnp.where(total > 0, tile_end - 1, -1).astype(jnp.int32)
    has_rows = total > 0
    later = has_rows[None, :] & (experts[None, :] > experts[:, None])
    next_expert = jnp.min(jnp.where(later, experts[None, :], N_EXPERTS), axis=1)
    next_expert = jnp.where(next_expert < N_EXPERTS, next_expert, -1)
    slot = (jnp.cumsum(has_rows.astype(jnp.int32)) - 1) % 2
    chunk_row = jnp.arange(MAX_CHUNKS, dtype=jnp.int32)[None, :, None] * RUN_ALIGN
    in_run = (chunk_row >= lo[:, None, :]) & (chunk_row < (lo + n8)[:, None, :])
    dst = jnp.sum(jnp.where(in_run, g[:, None, :] + chunk_row - lo[:, None, :], 0), axis=2)
    nchunk = jnp.sum(n8, axis=1) // RUN_ALIGN
    i32 = lambda a: a.reshape(-1).astype(jnp.int32)
    return (texp, first, n_used.reshape(1).astype(jnp.int32), i32(per_tile(slot)),
            i32(per_tile(next_expert)), pad_tile, i32(lo), i32(nchunk), i32(dst), p_rows)


def kernel(x, p, g_mix, w_in, sink, g_grp_a, g_grp_b, w_out, g_ffn, w_router_group, b_router_group,
           w_router_expert, b_router_expert, w_expert_gate, w_expert_up, w_expert_down, g_ple,
           w_ple_gate, w_ple_proj, g_final):
    batch, seq, _ = x.shape
    n_tok = batch * seq
    depth = w_in.shape[0]
    assert depth == 1, "the final RMSNorm is fused into the last layer's epilogue"
    bf = jnp.bfloat16
    xc = x.reshape(n_tok, D_MODEL)
    for i in range(depth):
        *qkv_a, qb, kb, vb = _proj(xc, g_mix[i][None, :], w_in[i].astype(bf), batch, seq)
        oa = _attn_a(qkv_a, batch, seq)
        ob = _attn_b(qb, kb, vb, sink[i], batch, seq)

        wr = jnp.zeros((ROUTER_ROWS, D_MODEL), jnp.float32)
        wr = wr.at[0:N_GROUPS].set(w_router_group[i].T)
        wr = wr.at[EXPERTS_PER_GROUP:EXPERTS_PER_GROUP + N_EXPERTS].set(w_router_expert[i].T)
        br = jnp.zeros((ROUTER_ROWS, 1), jnp.float32)
        br = br.at[0:N_GROUPS, 0].set(b_router_group[i])
        br = br.at[EXPERTS_PER_GROUP:EXPERTS_PER_GROUP + N_EXPERTS, 0].set(b_router_expert[i])

        x1, h2, route_i, route_c, counts = _mix(
            oa, ob, xc, g_grp_a[i][None, :], g_grp_b[i][None, :], w_out[i].astype(bf), g_ffn[i][None, :], wr, br)

        texp, first, nused, slot, nxt, pad_tile, lo, nchunk, dst, p_rows = _dispatch_tables(counts[:, :, 0])
        xs = _dispatch(lo, nchunk, dst, pad_tile, nused, route_i, h2, p_rows)
        y = _moe(texp, first, nused, slot, nxt, xs, w_expert_gate[i], w_expert_up[i], w_expert_down[i])
        xc = _final(lo, nchunk, dst, y, x1, route_c, p[i].reshape(n_tok, PLE_DIM), g_ple[i][None, :],
                    w_ple_gate[i].astype(bf), w_ple_proj[i].astype(bf), g_final[None, :])
    return xc.reshape(batch, seq, D_MODEL)
```
